```python
import jax, jax.numpy as jnp
from jax import lax
import numpy as np

D_MODEL = 1024
BATCH = 16
SEQ = 2048
DEPTH = 1
DEC_BATCH = 32
DEC_SEQ = 32
PAST_LEN = 1024

CHUNK = 64
D_CONV = D_MODEL // 2
CONV_A_W = 3
D_SSD = D_MODEL // 2
SSD_HEAD_DIM = 64
SSD_HEADS = D_SSD // SSD_HEAD_DIM
SSD_GROUPS = 2
SSD_HPG = SSD_HEADS // SSD_GROUPS
SSD_STATE = 128
SSD_CONV_W = 4
XBC_DIM = D_SSD + 2 * SSD_GROUPS * SSD_STATE
IN_SIZES = (D_CONV, D_CONV, D_CONV, D_SSD, XBC_DIM, SSD_HEADS)
IN_DIM = sum(IN_SIZES)
D_MIX = D_CONV + D_SSD
N_MEM = 256
MEM_HEADS = 4
MEM_HEAD_DIM = D_MODEL // MEM_HEADS
D_ATT = MEM_HEADS * MEM_HEAD_DIM
PEER_HEADS = 8
N_KEYS = 128
N_EXPERTS = N_KEYS * N_KEYS
PEER_TOPK = 16
PEER_DK = 256
PEER_DK_HALF = PEER_DK // 2
PEER_BLOCK = 128
EPS = 1e-6

kernel_name = "hybrid_stream_conv_ssd_peer_step"


def rmsnorm(x, g):
    xf = x.astype(jnp.float32)
    y = xf * lax.rsqrt(jnp.mean(xf * xf, axis=-1, keepdims=True) + EPS)
    return (y * g.astype(jnp.float32)).astype(x.dtype)


def causal_dwconv(u, hist, w):
    cat = jnp.concatenate([hist.astype(u.dtype), u], axis=1)
    y = lax.conv_general_dilated(cat, w[:, None, :].astype(u.dtype), window_strides=(1,),
                                 padding='VALID', dimension_numbers=('NWC', 'WIO', 'NWC'),
                                 feature_group_count=u.shape[-1])
    return y, cat[:, -(w.shape[0] - 1):]


def ssd_scan(x, dt, A, Bm, Cm, h0):
    b, s = x.shape[:2]
    l = min(CHUNK, s)
    c = s // l
    f32 = jnp.float32
    xr = x.astype(f32).reshape(b, c, l, SSD_GROUPS, SSD_HPG, SSD_HEAD_DIM)
    dtr = dt.reshape(b, c, l, SSD_GROUPS, SSD_HPG)
    Br = Bm.astype(f32).reshape(b, c, l, SSD_GROUPS, SSD_STATE)
    Cr = Cm.astype(f32).reshape(b, c, l, SSD_GROUPS, SSD_STATE)
    acum = jnp.cumsum(dtr * A.astype(f32).reshape(SSD_GROUPS, SSD_HPG), axis=2)
    seg = acum[:, :, :, None] - acum[:, :, None, :]
    mask = jnp.tril(jnp.ones((l, l), bool))[:, :, None, None]
    Lm = jnp.exp(jnp.where(mask, seg, -jnp.inf))
    xdt = xr * dtr[..., None]
    CB = jnp.einsum('bclgn,bcsgn->bclsg', Cr, Br)
    y_diag = jnp.einsum('bclsg,bclsgh,bcsghp->bclghp', CB, Lm, xdt)
    decay = jnp.exp(acum[:, :, -1:] - acum)
    chunk_states = jnp.einsum('bclgn,bclgh,bclghp->bcghpn', Br, decay, xdt)
    chunk_decay = jnp.exp(acum[:, :, -1])

    def step(hc, inp):
        st, dc = inp
        return dc[..., None, None] * hc + st, hc

    h_init = h0.astype(f32).reshape(b, SSD_GROUPS, SSD_HPG, SSD_HEAD_DIM, SSD_STATE)
    hT, h_in = lax.scan(step, h_init, (jnp.swapaxes(chunk_states, 0, 1), jnp.swapaxes(chunk_decay, 0, 1)))
    h_in = jnp.swapaxes(h_in, 0, 1)
    y_off = jnp.einsum('bclgn,bcghpn,bclgh->bclghp', Cr, h_in, jnp.exp(acum))
    y = (y_diag + y_off).reshape(b, s, SSD_HEADS, SSD_HEAD_DIM)
    return y, hT.reshape(b, SSD_HEADS, SSD_HEAD_DIM, SSD_STATE)


def token_mixer(xn, hist_a, hist_s, h0, w_in, conv_a_w, conv_s_w, conv_s_b, dt_bias, a_log,
                d_skip, ssd_norm_w, w_out):
    b, s, _ = xn.shape
    splits = list(np.cumsum(IN_SIZES)[:-1])
    g_b, g_c, v_in, z, xbc, dt = jnp.split(xn @ w_in, splits, axis=-1)
    conv_a, new_a = causal_dwconv(g_c * v_in, hist_a, conv_a_w)
    y_a = g_b * conv_a
    xbc_c, new_s = causal_dwconv(xbc, hist_s, conv_s_w)
    xbc_c = jax.nn.silu(xbc_c + conv_s_b)
    xs, Bm, Cm = jnp.split(xbc_c, [D_SSD, D_SSD + SSD_GROUPS * SSD_STATE], axis=-1)
    xs = xs.reshape(b, s, SSD_HEADS, SSD_HEAD_DIM)
    Bm = Bm.reshape(b, s, SSD_GROUPS, SSD_STATE)
    Cm = Cm.reshape(b, s, SSD_GROUPS, SSD_STATE)
    dtp = jax.nn.softplus(dt.astype(jnp.float32) + dt_bias.astype(jnp.float32))
    A = -jnp.exp(a_log.astype(jnp.float32))
    y, new_h = ssd_scan(xs, dtp, A, Bm, Cm, h0)
    y = (y + d_skip.astype(jnp.float32)[:, None] * xs.astype(jnp.float32)).astype(xn.dtype)
    y_b = rmsnorm(y.reshape(b, s, D_SSD) * jax.nn.silu(z), ssd_norm_w)
    out = jnp.concatenate([y_a, y_b], axis=-1) @ w_out
    return out, new_a, new_s, new_h.astype(xn.dtype)


def memory_kv(mem, mem_norm_w, w_mk, w_mv):
    b, m, _ = mem.shape
    mn = rmsnorm(mem, mem_norm_w)
    k = (mn @ w_mk).reshape(b, m, MEM_HEADS, MEM_HEAD_DIM)
    v = (mn @ w_mv).reshape(b, m, MEM_HEADS, MEM_HEAD_DIM)
    return k, v


def cross_attn(xn, k, v, w_mq, w_mo):
    b, s, _ = xn.shape
    q = (xn @ w_mq).reshape(b, s, MEM_HEADS, MEM_HEAD_DIM)
    sc = jnp.einsum('bshd,bmhd->bhsm', q, k.astype(q.dtype)).astype(jnp.float32) * (MEM_HEAD_DIM ** -0.5)
    p = jax.nn.softmax(sc, axis=-1).astype(xn.dtype)
    o = jnp.einsum('bhsm,bmhd->bshd', p, v.astype(xn.dtype)).reshape(b, s, D_ATT)
    return o @ w_mo


def peer(xn, w_pq, sub_keys, u_tab, v_tab):
    b, s, d = xn.shape
    t = b * s
    nblk = -(-t // PEER_BLOCK)
    xt = jnp.pad(xn.reshape(t, d), ((0, nblk * PEER_BLOCK - t), (0, 0))).reshape(nblk, PEER_BLOCK, d)

    def block(xb):
        q = (xb @ w_pq).reshape(-1, PEER_HEADS, 2, PEER_DK_HALF)
        sc = jnp.einsum('thpd,hpkd->thpk', q, sub_keys).astype(jnp.float32)
        v1, i1 = lax.top_k(sc[:, :, 0], PEER_TOPK)
        v2, i2 = lax.top_k(sc[:, :, 1], PEER_TOPK)
        comb = (v1[..., :, None] + v2[..., None, :]).reshape(-1, PEER_HEADS, PEER_TOPK * PEER_TOPK)
        cidx = (i1[..., :, None] * N_KEYS + i2[..., None, :]).reshape(-1, PEER_HEADS, PEER_TOPK * PEER_TOPK)
        top_sc, pos = lax.top_k(comb, PEER_TOPK)
        eidx = jnp.take_along_axis(cidx, pos, axis=-1)
        gate = jax.nn.softmax(top_sc, axis=-1)
        hid = jax.nn.gelu(jnp.einsum('td,thkd->thk', xb, u_tab[eidx]).astype(jnp.float32), approximate=False)
        return jnp.einsum('thk,thkd->td', (gate * hid).astype(xb.dtype), v_tab[eidx])

    out = lax.map(block, xt).reshape(nblk * PEER_BLOCK, d)[:t]
    return out.reshape(b, s, d)


def layer(x, mk, mv, hist_a, hist_s, h0, ln_mix_w, w_in, conv_a_w, conv_s_w, conv_s_b, dt_bias,
          a_log, d_skip, ssd_norm_w, w_out, ln_mem_w, w_mq, w_mo, ln_ffn_w, w_pq, sub_keys, u_tab, v_tab):
    mix, new_a, new_s, new_h = token_mixer(rmsnorm(x, ln_mix_w), hist_a, hist_s, h0, w_in, conv_a_w,
                                           conv_s_w, conv_s_b, dt_bias, a_log, d_skip, ssd_norm_w, w_out)
    x = x + mix
    x = x + cross_attn(rmsnorm(x, ln_mem_w), mk, mv, w_mq, w_mo)
    x = x + peer(rmsnorm(x, ln_ffn_w), w_pq, sub_keys, u_tab, v_tab)
    return x, new_a, new_s, new_h


def setup_inputs(seed: int = 0) -> dict:
    key = jax.random.key(seed)
    ks = iter(jax.random.split(key, 40))

    def nrm(shape, scale=1.0):
        return jax.random.normal(next(ks), shape, jnp.float32) * scale

    def gain(shape):
        return 1.0 + 0.01 * nrm(shape)

    dt0 = jnp.exp(jax.random.uniform(next(ks), (DEPTH, SSD_HEADS), jnp.float32,
                                     np.log(1e-3).astype(np.float32), np.log(1e-1).astype(np.float32)))
    dt_bias = dt0 + jnp.log(-jnp.expm1(-dt0))
    a_log = jnp.log(jax.random.uniform(next(ks), (DEPTH, SSD_HEADS), jnp.float32, 1.0, 16.0))
    return {
        "x_prompt": nrm((BATCH, SEQ, D_MODEL)),
        "x_sample": nrm((DEC_BATCH, DEC_SEQ, D_MODEL)),
        "mem_prompt": nrm((BATCH, N_MEM, D_MODEL)),
        "cache_conv_a": nrm((DEPTH, DEC_BATCH, CONV_A_W - 1, D_CONV)),
        "cache_conv_ssd": nrm((DEPTH, DEC_BATCH, SSD_CONV_W - 1, XBC_DIM)),
        "state_ssd": nrm((DEPTH, DEC_BATCH, SSD_HEADS, SSD_HEAD_DIM, SSD_STATE), 0.5),
        "cache_mem_k": nrm((DEPTH, DEC_BATCH, N_MEM, MEM_HEADS, MEM_HEAD_DIM)),
        "cache_mem_v": nrm((DEPTH, DEC_BATCH, N_MEM, MEM_HEADS, MEM_HEAD_DIM)),
        "ln_mix_w": gain((DEPTH, D_MODEL)),
        "w_in": nrm((DEPTH, D_MODEL, IN_DIM), D_MODEL ** -0.5),
        "conv_a_w": nrm((DEPTH, CONV_A_W, D_CONV), CONV_A_W ** -0.5),
        "conv_s_w": nrm((DEPTH, SSD_CONV_W, XBC_DIM), SSD_CONV_W ** -0.5),
        "conv_s_b": nrm((DEPTH, XBC_DIM), 0.01),
        "dt_bias": dt_bias,
        "a_log": a_log,
        "d_skip": gain((DEPTH, SSD_HEADS)),
        "ssd_norm_w": gain((DEPTH, D_SSD)),
        "w_out": nrm((DEPTH, D_MIX, D_MODEL), D_MIX ** -0.5),
        "ln_mem_w": gain((DEPTH, D_MODEL)),
        "mem_norm_w": gain((DEPTH, D_MODEL)),
        "w_mq": nrm((DEPTH, D_MODEL, D_ATT), D_MODEL ** -0.5),
        "w_mk": nrm((DEPTH, D_MODEL, D_ATT), D_MODEL ** -0.5),
        "w_mv": nrm((DEPTH, D_MODEL, D_ATT), D_MODEL ** -0.5),
        "w_mo": nrm((DEPTH, D_ATT, D_MODEL), D_ATT ** -0.5),
        "ln_ffn_w": gain((DEPTH, D_MODEL)),
        "w_pq": nrm((DEPTH, D_MODEL, PEER_HEADS * PEER_DK), D_MODEL ** -0.5),
        "sub_keys": nrm((DEPTH, PEER_HEADS, 2, N_KEYS, PEER_DK_HALF), PEER_DK_HALF ** -0.5),
        "u_tab": nrm((DEPTH, N_EXPERTS, D_MODEL), D_MODEL ** -0.5),
        "v_tab": nrm((DEPTH, N_EXPERTS, D_MODEL), PEER_HEADS ** -0.5),
        "final_norm_w": gain((D_MODEL,)),
    }


def reference(x_prompt, x_sample, mem_prompt, cache_conv_a, cache_conv_ssd, state_ssd, cache_mem_k,
              cache_mem_v, ln_mix_w, w_in, conv_a_w, conv_s_w, conv_s_b, dt_bias, a_log, d_skip,
              ssd_norm_w, w_out, ln_mem_w, mem_norm_w, w_mq, w_mk, w_mv, w_mo, ln_ffn_w, w_pq,
              sub_keys, u_tab, v_tab, final_norm_w):
    bp = x_prompt.shape[0]
    hp, hs = x_prompt, x_sample
    pa, ps, ph, pk, pv, sa, ss, sh = [], [], [], [], [], [], [], []
    for l in range(DEPTH):
        lw = (ln_mix_w[l], w_in[l], conv_a_w[l], conv_s_w[l], conv_s_b[l], dt_bias[l], a_log[l],
              d_skip[l], ssd_norm_w[l], w_out[l], ln_mem_w[l], w_mq[l], w_mo[l], ln_ffn_w[l],
              w_pq[l], sub_keys[l], u_tab[l], v_tab[l])
        mk, mv = memory_kv(mem_prompt, mem_norm_w[l], w_mk[l], w_mv[l])
        zero_a = jnp.zeros((bp, CONV_A_W - 1, D_CONV), x_prompt.dtype)
        zero_s = jnp.zeros((bp, SSD_CONV_W - 1, XBC_DIM), x_prompt.dtype)
        zero_h = jnp.zeros((bp, SSD_HEADS, SSD_HEAD_DIM, SSD_STATE), x_prompt.dtype)
        hp, na, ns, nh = layer(hp, mk, mv, zero_a, zero_s, zero_h, *lw)
        pa.append(na); ps.append(ns); ph.append(nh); pk.append(mk); pv.append(mv)
        hs, na, ns, nh = layer(hs, cache_mem_k[l], cache_mem_v[l], cache_conv_a[l], cache_conv_ssd[l],
                               state_ssd[l], *lw)
        sa.append(na); ss.append(ns); sh.append(nh)
    y_prompt = rmsnorm(hp, final_norm_w)
    y_sample = rmsnorm(hs, final_norm_w)
    return (y_prompt, y_sample, jnp.stack(pa), jnp.stack(ps), jnp.stack(ph), jnp.stack(pk), jnp.stack(pv),
            jnp.stack(sa), jnp.stack(ss), jnp.stack(sh))
```

```python
import functools
import math

import jax
import jax.numpy as jnp
from jax import lax
from jax.experimental import pallas as pl
from jax.experimental.pallas import tpu as pltpu

D_MODEL = 1024
CHUNK = 64
D_CONV = 512
CONV_A_W = 3
D_SSD = 512
SSD_HEAD_DIM = 64
SSD_HEADS = 8
SSD_GROUPS = 2
SSD_HPG = 4
SSD_STATE = 128
SSD_CONV_W = 4
XBC_DIM = 1024
N_MEM = 256
MEM_HEADS = 4
MEM_HEAD_DIM = 256
PEER_HEADS = 8
N_KEYS = 128
N_EXPERTS = N_KEYS * N_KEYS
PEER_TOPK = 16
PEER_DK = 256
PEER_DK_HALF = 128
PICKS = PEER_HEADS * PEER_TOPK
EPS = 1e-6

LANES = 128
SUBLANES = 8
HALF_EXPERTS = N_EXPERTS // 2
PAIR_STATE = (SSD_HEADS // 2, 2 * SSD_HEAD_DIM, SSD_STATE)
VMEM_LIMIT = 56 * 1024 * 1024

F32 = jnp.float32
BF16 = jnp.bfloat16
HI = lax.Precision.HIGHEST


def _rms(x, g):
    return x * lax.rsqrt(jnp.mean(x * x, axis=-1, keepdims=True) + EPS) * g


def _silu(x):
    return x * (1.0 / (1.0 + jnp.exp(-x)))


def _softplus(x):
    return jnp.maximum(x, 0.0) + jnp.log1p(jnp.exp(-jnp.abs(x)))


def _dot(a, b):
    return jnp.dot(a.astype(BF16), b.astype(BF16), preferred_element_type=F32)


def _dot_nt(a, b):
    return lax.dot_general(a.astype(BF16), b.astype(BF16), (((1,), (1,)), ((), ())),
                           preferred_element_type=F32)


def _dot_tn(a, b):
    return lax.dot_general(a.astype(BF16), b.astype(BF16), (((0,), (0,)), ((), ())),
                           preferred_element_type=F32)


def _const_spec(shape):
    n = len(shape)
    return pl.BlockSpec(shape, lambda *_: (0,) * n)


def _mixer_kernel(has_hist, ts, lc, *refs):
    if has_hist:
        x_ref, hista_ref, hists_ref, h0_ref = refs[:4]
        refs = refs[4:]
    else:
        x_ref = refs[0]
        refs = refs[1:]
    (lnw_ref, win_ref, wdt_ref, wdtt_ref, caw_ref, csw_ref, csb_ref, dtb_c_ref, alog_c_ref, dsk_c_ref,
     dtb_r_ref, alog_r_ref, nw_ref, wout_ref,
     h1_ref, na_ref, ns_ref, nh_ref, cata, cats, hst) = refs
    s = pl.program_id(1)
    pad = SUBLANES

    @pl.when(s == 0)
    def _():
        if has_hist:
            cata[pad - 2:pad, :] = hista_ref[...]
            cats[pad - 3:pad, :] = hists_ref[...]
            hst[...] = h0_ref[...]
        else:
            cata[0:pad, :] = jnp.zeros((pad, D_CONV), F32)
            cats[0:pad, :] = jnp.zeros((pad, XBC_DIM), F32)
            hst[...] = jnp.zeros(hst.shape, F32)

    x = x_ref[...]
    xn = _rms(x, lnw_ref[...]).astype(BF16)
    proj = jnp.dot(xn, win_ref[...], preferred_element_type=F32)
    dt_c = jnp.dot(xn, wdt_ref[...], preferred_element_type=F32)
    dt_r = lax.dot_general(wdtt_ref[...], xn, (((1,), (1,)), ((), ())),
                           preferred_element_type=F32)
    g_b = proj[:, 0:D_CONV]
    g_c = proj[:, D_CONV:2 * D_CONV]
    v_in = proj[:, 2 * D_CONV:3 * D_CONV]
    z = proj[:, 3 * D_CONV:3 * D_CONV + D_SSD]
    xbc = proj[:, 3 * D_CONV + D_SSD:3 * D_CONV + D_SSD + XBC_DIM]

    ua = g_c * v_in
    cata[pad:pad + ts, :] = ua
    caw = caw_ref[...]
    conv_a = (caw[0:1] * cata[pad - 2:pad - 2 + ts, :] + caw[1:2] * cata[pad - 1:pad - 1 + ts, :]
              + caw[2:3] * ua)
    y_a = g_b * conv_a
    hist_a = cata[pad + ts - 2:pad + ts, :]
    na_ref[...] = hist_a
    cata[pad - 2:pad, :] = hist_a

    cats[pad:pad + ts, :] = xbc
    csw = csw_ref[...]
    conv_s = (csw[0:1] * cats[pad - 3:pad - 3 + ts, :] + csw[1:2] * cats[pad - 2:pad - 2 + ts, :]
              + csw[2:3] * cats[pad - 1:pad - 1 + ts, :] + csw[3:4] * xbc)
    hist_s = cats[pad + ts - 3:pad + ts, :]
    ns_ref[...] = hist_s
    cats[pad - 3:pad, :] = hist_s
    xbc_c = _silu(conv_s + csb_ref[...])
    xs = xbc_c[:, 0:D_SSD]

    dtp_c = _softplus(dt_c + dtb_c_ref[...])
    dtp_r = _softplus(dt_r + dtb_r_ref[...])
    a_c = dtp_c * (-jnp.exp(alog_c_ref[...]))
    a_r = dtp_r * (-jnp.exp(alog_r_ref[...]))
    dsk_c = dsk_c_ref[...]

    lane = lax.broadcasted_iota(jnp.int32, (1, LANES), 1)
    first_head = lane < SSD_HEAD_DIM
    row2 = lax.broadcasted_iota(jnp.int32, (2 * SSD_HEAD_DIM, 1), 0) < SSD_HEAD_DIM
    li = lax.broadcasted_iota(jnp.int32, (lc, lc), 0)
    si = lax.broadcasted_iota(jnp.int32, (lc, lc), 1)
    causal = si <= li
    tril = causal.astype(F32)
    jrow = lax.broadcasted_iota(jnp.int32, (ts, lc), 0)
    scol = lax.broadcasted_iota(jnp.int32, (ts, lc), 1)

    def pair(col, h):
        return jnp.where(first_head, col[:, h:h + 1], col[:, h + 1:h + 2])

    y_chunks = []
    for c in range(ts // lc):
        r0 = c * lc
        rows = slice(r0, r0 + lc)
        acum_c = jnp.dot(tril, a_c[rows], preferred_element_type=F32, precision=HI)
        upper = ((jrow >= r0) & (jrow <= r0 + scol)).astype(F32)
        acum_r = jnp.dot(a_r, upper, preferred_element_type=F32, precision=HI)
        last_c = acum_c[lc - 1:lc, :]
        dec_c = jnp.exp(last_c - acum_c)
        eac_c = jnp.exp(acum_c)
        y_pairs = []
        for g in range(SSD_GROUPS):
            b_g = xbc_c[rows, D_SSD + g * SSD_STATE:D_SSD + (g + 1) * SSD_STATE]
            c_g = xbc_c[rows, D_SSD + (SSD_GROUPS + g) * SSD_STATE:D_SSD + (SSD_GROUPS + g + 1) * SSD_STATE]
            cb = _dot_nt(c_g, b_g)
            for q in range(SSD_HPG // 2):
                h = g * SSD_HPG + 2 * q
                pi = h // 2
                xs_p = xs[rows, pi * LANES:(pi + 1) * LANES]
                xdt = xs_p * pair(dtp_c[rows], h)
                res = []
                for hh in (h, h + 1):
                    seg = acum_c[:, hh:hh + 1] - acum_r[hh:hh + 1, :]
                    m_h = cb * jnp.exp(jnp.where(causal, seg, -jnp.inf))
                    res.append(_dot(m_h, xdt))
                y_diag = jnp.where(first_head, res[0], res[1])
                h_in = hst[pi]
                y_off = _dot_nt(c_g, h_in) * pair(eac_c, h)
                st = _dot_tn(xdt * pair(dec_c, h), b_g)
                cd = jnp.exp(jnp.where(row2, last_c[:, h:h + 1], last_c[:, h + 1:h + 2]))
                hst[pi] = cd * h_in + st
                y_pairs.append(y_diag + y_off + pair(dsk_c, h) * xs_p)
        y_chunks.append(jnp.concatenate(y_pairs, axis=1))
    y = y_chunks[0] if len(y_chunks) == 1 else jnp.concatenate(y_chunks, axis=0)
    y_b = _rms(y * _silu(z), nw_ref[...])
    wout = wout_ref[...]
    out = (jnp.dot(y_a.astype(BF16), wout[0:D_CONV], preferred_element_type=F32)
           + jnp.dot(y_b.astype(BF16), wout[D_CONV:], preferred_element_type=F32))
    h1_ref[...] = x + out
    nh_ref[...] = hst[...]


def _mixer(x, hist, w, ts):
    b, s, _ = x.shape
    lc = min(CHUNK, s)
    ts = min(ts, s)
    has_hist = hist is not None
    per_b3 = lambda shape: pl.BlockSpec((None,) + shape, lambda i, j: (i,) + (0,) * len(shape))
    in_specs = [pl.BlockSpec((None, ts, D_MODEL), lambda i, j: (i, j, 0))]
    args = [x]
    if has_hist:
        in_specs += [per_b3((CONV_A_W - 1, D_CONV)), per_b3((SSD_CONV_W - 1, XBC_DIM)),
                     per_b3(PAIR_STATE)]
        args += list(hist)
    wargs = [w["ln_mix"], w["w_in"], w["w_dt"], w["w_dtt"], w["conv_a_w"], w["conv_s_w"], w["conv_s_b"],
             w["dtb_c"], w["alog_c"], w["dsk_c"], w["dtb_r"], w["alog_r"], w["ssd_norm"], w["w_out"]]
    in_specs += [_const_spec(a.shape) for a in wargs]
    args += wargs
    out_shape = (jax.ShapeDtypeStruct((b, s, D_MODEL), F32),
                 jax.ShapeDtypeStruct((b, CONV_A_W - 1, D_CONV), F32),
                 jax.ShapeDtypeStruct((b, SSD_CONV_W - 1, XBC_DIM), F32),
                 jax.ShapeDtypeStruct((b,) + PAIR_STATE, F32))
    out_specs = (pl.BlockSpec((None, ts, D_MODEL), lambda i, j: (i, j, 0)),
                 per_b3((CONV_A_W - 1, D_CONV)), per_b3((SSD_CONV_W - 1, XBC_DIM)),
                 per_b3(PAIR_STATE))
    return pl.pallas_call(
        functools.partial(_mixer_kernel, has_hist, ts, lc),
        out_shape=out_shape, grid=(b, s // ts), in_specs=in_specs, out_specs=out_specs,
        scratch_shapes=[pltpu.VMEM((SUBLANES + ts, D_CONV), F32), pltpu.VMEM((SUBLANES + ts, XBC_DIM), F32),
                        pltpu.VMEM(PAIR_STATE, F32)],
        compiler_params=pltpu.CompilerParams(dimension_semantics=("parallel", "arbitrary"),
                                             vmem_limit_bytes=VMEM_LIMIT),
        name="mixer_hist" if has_hist else "mixer",
    )(*args)


def _memkv_kernel(m_ref, g_ref, wk_ref, wv_ref, k_ref, v_ref):
    mn = _rms(m_ref[...], g_ref[...]).astype(BF16)
    k_ref[...] = jnp.dot(mn, wk_ref[...], preferred_element_type=F32)
    v_ref[...] = jnp.dot(mn, wv_ref[...], preferred_element_type=F32)


def _memory_kv(mem, g, wk, wv, tm=256):
    t = mem.shape[0]
    blk = pl.BlockSpec((tm, D_MODEL), lambda i: (i, 0))
    return pl.pallas_call(
        _memkv_kernel,
        out_shape=(jax.ShapeDtypeStruct((t, D_MODEL), F32),) * 2,
        grid=(t // tm,), in_specs=[blk, _const_spec(g.shape), _const_spec(wk.shape), _const_spec(wv.shape)],
        out_specs=(blk, blk),
        compiler_params=pltpu.CompilerParams(dimension_semantics=("parallel",), vmem_limit_bytes=VMEM_LIMIT),
        name="memory_kv",
    )(mem, g, wk, wv)


def _attn_kernel(x_ref, k_ref, v_ref, g_ref, wq_ref, wo_ref, o_ref):
    x = x_ref[...]
    xn = _rms(x, g_ref[...]).astype(BF16)
    q = jnp.dot(xn, wq_ref[...], preferred_element_type=F32)
    k = k_ref[...]
    v = v_ref[...]
    wo = wo_ref[...]
    acc = x
    for h in range(MEM_HEADS):
        cols = slice(h * MEM_HEAD_DIM, (h + 1) * MEM_HEAD_DIM)
        sc = _dot_nt(q[:, cols], k[:, cols]) * (MEM_HEAD_DIM ** -0.5)
        sc = sc - jnp.max(sc, axis=-1, keepdims=True)
        p = jnp.exp(sc)
        p = p / jnp.sum(p, axis=-1, keepdims=True)
        o_h = _dot(p, v[:, cols])
        acc = acc + jnp.dot(o_h.astype(BF16), wo[cols, :], preferred_element_type=F32)
    o_ref[...] = acc


def _attn(x, k, v, g, wq, wo, ts):
    b, s, _ = x.shape
    ts = min(ts, s)
    xblk = pl.BlockSpec((None, ts, D_MODEL), lambda i, j: (i, j, 0))
    kvblk = pl.BlockSpec((None, N_MEM, D_MODEL), lambda i, j: (i, 0, 0))
    return pl.pallas_call(
        _attn_kernel,
        out_shape=jax.ShapeDtypeStruct((b, s, D_MODEL), F32),
        grid=(b, s // ts),
        in_specs=[xblk, kvblk, kvblk, _const_spec(g.shape), _const_spec(wq.shape), _const_spec(wo.shape)],
        out_specs=xblk,
        compiler_params=pltpu.CompilerParams(dimension_semantics=("parallel", "parallel"),
                                             vmem_limit_bytes=VMEM_LIMIT),
        name="attn",
    )(x, k, v, g, wq, wo)


def _top_k_rows(sc, iota, n, k):
    vals, idxs = [], []
    for _ in range(k):
        m = jnp.max(sc, axis=0, keepdims=True)
        idx = jnp.min(jnp.where(sc == m, iota, n), axis=0, keepdims=True)
        sc = jnp.where(iota == idx, -jnp.inf, sc)
        vals.append(m)
        idxs.append(idx)
    return jnp.concatenate(vals, axis=0), jnp.concatenate(idxs, axis=0)


def _select_kernel(tb, x_ref, g_ref, wq_ref, keys_ref, xn_ref, tile_ref, shift_ref, gate_ref):
    h = pl.program_id(1)
    xn = _rms(x_ref[...], g_ref[...])

    @pl.when(h == 0)
    def _():
        xn_ref[...] = xn

    q = jnp.dot(xn.astype(BF16), wq_ref[...], preferred_element_type=F32)
    iota_k = lax.broadcasted_iota(jnp.int32, (N_KEYS, tb), 0)
    v1, i1 = _top_k_rows(_dot_nt(keys_ref[0, 0], q[:, 0:PEER_DK_HALF]), iota_k, N_KEYS, PEER_TOPK)
    v2, i2 = _top_k_rows(_dot_nt(keys_ref[0, 1], q[:, PEER_DK_HALF:PEER_DK]), iota_k, N_KEYS, PEER_TOPK)
    comb = jnp.concatenate([v1[a:a + 1] + v2 for a in range(PEER_TOPK)], axis=0)
    cidx = jnp.concatenate([i1[a:a + 1] * N_KEYS + i2 for a in range(PEER_TOPK)], axis=0)
    n2 = PEER_TOPK * PEER_TOPK
    iota_c = lax.broadcasted_iota(jnp.int32, (n2, tb), 0)
    vals, eids = [], []
    for _ in range(PEER_TOPK):
        m = jnp.max(comb, axis=0, keepdims=True)
        pos = jnp.min(jnp.where(comb == m, iota_c, n2), axis=0, keepdims=True)
        hit = iota_c == pos
        eids.append(jnp.sum(jnp.where(hit, cidx, 0), axis=0, keepdims=True))
        comb = jnp.where(hit, -jnp.inf, comb)
        vals.append(m)
    top = jnp.concatenate(vals, axis=0)
    eid = jnp.concatenate(eids, axis=0)
    ex = jnp.exp(top - top[0:1])
    gate_ref[0] = ex / jnp.sum(ex, axis=0, keepdims=True)
    upper = eid >= HALF_EXPERTS
    tile_ref[0] = jnp.where(upper, eid - HALF_EXPERTS, eid)
    shift_ref[0] = jnp.where(upper, 0, 16)


def _select(x, g, wq, keys, tb):
    t = x.shape[0]
    xblk = pl.BlockSpec((tb, D_MODEL), lambda i, j: (i, 0))
    oblk = pl.BlockSpec((1, PEER_TOPK, tb), lambda i, j: (j, 0, i))
    return pl.pallas_call(
        functools.partial(_select_kernel, tb),
        out_shape=(jax.ShapeDtypeStruct((t, D_MODEL), F32),
                   jax.ShapeDtypeStruct((PEER_HEADS, PEER_TOPK, t), jnp.int32),
                   jax.ShapeDtypeStruct((PEER_HEADS, PEER_TOPK, t), jnp.int32),
                   jax.ShapeDtypeStruct((PEER_HEADS, PEER_TOPK, t), F32)),
        grid=(t // tb, PEER_HEADS),
        in_specs=[xblk, _const_spec(g.shape),
                  pl.BlockSpec((D_MODEL, PEER_DK), lambda i, j: (0, j)),
                  pl.BlockSpec((1, 2, N_KEYS, PEER_DK_HALF), lambda i, j: (j, 0, 0, 0))],
        out_specs=(xblk, oblk, oblk, oblk),
        compiler_params=pltpu.CompilerParams(dimension_semantics=("parallel", "arbitrary"),
                                             vmem_limit_bytes=VMEM_LIMIT),
        name="peer_select",
    )(x, g, wq, keys)


HI_HALF_MASK = -(1 << 16)
_BITREV3 = (0, 4, 2, 6, 1, 5, 3, 7)


def _unpack(word, shift):
    return lax.bitcast_convert_type((word << shift) & HI_HALF_MASK, F32)


def _merge(x, y, d, keep):
    return jnp.where(keep, x + pltpu.roll(x, SUBLANES - d, 0), y + pltpu.roll(y, d, 0))


def _hidden_kernel(tb, tile_ref, shift_ref, x_ref, gate_ref, tab_ref, gh_ref, part):
    sub = lax.broadcasted_iota(jnp.int32, (SUBLANES, LANES), 0)
    keep = {d: (sub & d) == 0 for d in (4, 2, 1)}
    ones = jnp.ones((SUBLANES, LANES), F32)

    def token(t, carry):
        x = x_ref[t]
        base = t * PICKS
        for j in range(PICKS // SUBLANES):
            p = []
            for r in _BITREV3:
                k = base + j * SUBLANES + r
                p.append(x * _unpack(tab_ref[tile_ref[k]], shift_ref[k]))
            for d in (4, 2, 1):
                p = [_merge(p[2 * i], p[2 * i + 1], d, keep[d]) for i in range(len(p) // 2)]
            part[j * SUBLANES:(j + 1) * SUBLANES, :] = p[0]
        hid = lax.dot_general(ones, part[...], (((1,), (1,)), ((), ())), preferred_element_type=F32,
                              precision=HI)[0:1]
        gh_ref[pl.ds(t, 1), :] = hid
        return carry

    lax.fori_loop(0, tb, token, 0)
    hid = gh_ref[...]
    gelu = 0.5 * hid * (1.0 + lax.erf(hid * (2.0 ** -0.5)))
    gh_ref[...] = gate_ref[...] * gelu


def _hidden(tile, shift, xg, gate, tab, tb):
    t = xg.shape[0]
    sblk = pl.BlockSpec((tb * PICKS,), lambda i: (i,), memory_space=pltpu.SMEM)
    return pl.pallas_call(
        functools.partial(_hidden_kernel, tb),
        out_shape=jax.ShapeDtypeStruct((t, PICKS), F32),
        grid=(t // tb,),
        in_specs=[sblk, sblk,
                  pl.BlockSpec((tb, SUBLANES, LANES), lambda i: (i, 0, 0)),
                  pl.BlockSpec((tb, PICKS), lambda i: (i, 0)),
                  pl.BlockSpec(tab.shape, lambda i: (0, 0, 0), pipeline_mode=pl.Buffered(1))],
        out_specs=pl.BlockSpec((tb, PICKS), lambda i: (i, 0)),
        scratch_shapes=[pltpu.VMEM((PICKS, LANES), F32)],
        compiler_params=pltpu.CompilerParams(dimension_semantics=("parallel",), vmem_limit_bytes=VMEM_LIMIT),
        name="peer_hidden",
    )(tile, shift, xg, gate, tab)


def _combine_kernel(tb, tile_ref, shift_ref, gh_ref, tab_ref, o_ref):
    def token(t, carry):
        base = t * PICKS
        acc = [jnp.zeros((SUBLANES, LANES), F32) for _ in range(4)]
        for k in range(PICKS):
            val = _unpack(tab_ref[tile_ref[base + k]], shift_ref[base + k])
            acc[k % 4] = acc[k % 4] + gh_ref[base + k] * val
        o_ref[t] = (acc[0] + acc[1]) + (acc[2] + acc[3])
        return carry

    lax.fori_loop(0, tb, token, 0)


def _combine(tile, shift, gh, tab, tb):
    t = gh.shape[0] // PICKS
    sblk = pl.BlockSpec((tb * PICKS,), lambda i: (i,), memory_space=pltpu.SMEM)
    return pl.pallas_call(
        functools.partial(_combine_kernel, tb),
        out_shape=jax.ShapeDtypeStruct((t, SUBLANES, LANES), F32),
        grid=(t // tb,),
        in_specs=[sblk, sblk, sblk,
                  pl.BlockSpec(tab.shape, lambda i: (0, 0, 0), pipeline_mode=pl.Buffered(1))],
        out_specs=pl.BlockSpec((tb, SUBLANES, LANES), lambda i: (i, 0, 0)),
        compiler_params=pltpu.CompilerParams(dimension_semantics=("parallel",), vmem_limit_bytes=VMEM_LIMIT),
        name="peer_combine",
    )(tile, shift, gh, tab)


def _final_kernel(x_ref, p_ref, g_ref, o_ref):
    o_ref[...] = _rms(x_ref[...] + p_ref[...], g_ref[...])


def _final(x, p, g, tb):
    t = x.shape[0]
    blk = pl.BlockSpec((tb, D_MODEL), lambda i: (i, 0))
    return pl.pallas_call(
        _final_kernel, out_shape=jax.ShapeDtypeStruct((t, D_MODEL), F32), grid=(t // tb,),
        in_specs=[blk, blk, _const_spec(g.shape)], out_specs=blk,
        compiler_params=pltpu.CompilerParams(dimension_semantics=("parallel",), vmem_limit_bytes=VMEM_LIMIT),
        name="final_norm",
    )(x, p, g)


def _pack_table(tab):
    bits = lax.bitcast_convert_type(tab.astype(BF16), jnp.uint16).astype(jnp.uint32)
    packed = (bits[HALF_EXPERTS:] << 16) | bits[:HALF_EXPERTS]
    return lax.bitcast_convert_type(packed, jnp.int32).reshape(HALF_EXPERTS, SUBLANES, LANES)


def _col(v, n=LANES):
    return jnp.pad(v.astype(F32), (0, n - v.shape[0])).reshape(1, n)


def _row(v, n=2 * SUBLANES):
    return jnp.pad(v.astype(F32), (0, n - v.shape[0])).reshape(n, 1)


def _peer(x, w, tb_sel, tb_exp):
    t = x.shape[0]
    xn, tile, shift, gate = _select(x, w["ln_ffn"], w["w_pq"], w["sub_keys"], tb_sel)
    to_tok = lambda a: a.reshape(PICKS, t).T
    tile_f = to_tok(tile).reshape(t * PICKS)
    shift_f = to_tok(shift).reshape(t * PICKS)
    gh = _hidden(tile_f, shift_f, xn.reshape(t, SUBLANES, LANES), to_tok(gate), w["u_pack"], tb_exp)
    out = _combine(tile_f, shift_f, gh.reshape(t * PICKS), w["v_pack"], tb_exp)
    return _final(x, out.reshape(t, D_MODEL), w["final_norm"], tb_sel)


def kernel(x_prompt, x_sample, mem_prompt, cache_conv_a, cache_conv_ssd, state_ssd, cache_mem_k, cache_mem_v, ln_mix_w, w_in, conv_a_w, conv_s_w, conv_s_b, dt_bias, a_log, d_skip, ssd_norm_w, w_out, ln_mem_w, mem_norm_w, w_mq, w_mk, w_mv, w_mo, ln_ffn_w, w_pq, sub_keys, u_tab, v_tab, final_norm_w):
    depth = w_in.shape[0]
    assert depth == 1
    l = 0
    n_main = 3 * D_CONV + D_SSD + XBC_DIM
    w_dt = w_in[l][:, n_main:]
    w = {
        "ln_mix": ln_mix_w[l].reshape(1, D_MODEL),
        "w_in": w_in[l][:, :n_main].astype(BF16),
        "w_dt": jnp.pad(w_dt, ((0, 0), (0, LANES - SSD_HEADS))).astype(BF16),
        "w_dtt": jnp.pad(w_dt.T, ((0, 2 * SUBLANES - SSD_HEADS), (0, 0))).astype(BF16),
        "conv_a_w": conv_a_w[l], "conv_s_w": conv_s_w[l], "conv_s_b": conv_s_b[l].reshape(1, XBC_DIM),
        "dtb_c": _col(dt_bias[l]), "alog_c": _col(a_log[l]), "dsk_c": _col(d_skip[l]),
        "dtb_r": _row(dt_bias[l]), "alog_r": _row(a_log[l]),
        "ssd_norm": ssd_norm_w[l].reshape(1, D_SSD),
        "w_out": w_out[l].astype(BF16),
        "ln_ffn": ln_ffn_w[l].reshape(1, D_MODEL),
        "w_pq": w_pq[l].astype(BF16),
        "sub_keys": sub_keys[l].astype(BF16),
        "u_pack": _pack_table(u_tab[l]),
        "v_pack": _pack_table(v_tab[l]),
        "final_norm": final_norm_w.reshape(1, D_MODEL),
    }
    ln_mem = ln_mem_w[l].reshape(1, D_MODEL)
    wq, wo = w_mq[l].astype(BF16), w_mo[l].astype(BF16)

    bp, sp, _ = x_prompt.shape
    bs, ss, _ = x_sample.shape

    mk, mv = _memory_kv(mem_prompt.reshape(bp * N_MEM, D_MODEL), mem_norm_w[l].reshape(1, D_MODEL),
                        w_mk[l].astype(BF16), w_mv[l].astype(BF16))
    hp, pa, ps, ph = _mixer(x_prompt, None, w, 256)
    hp = _attn(hp, mk.reshape(bp, N_MEM, D_MODEL), mv.reshape(bp, N_MEM, D_MODEL), ln_mem, wq, wo, 256)
    y_prompt = _peer(hp.reshape(bp * sp, D_MODEL), w, 256, 64).reshape(bp, sp, D_MODEL)

    hs, sa, ssd, sh = _mixer(x_sample, (cache_conv_a[l], cache_conv_ssd[l],
                                        state_ssd[l].reshape((bs,) + PAIR_STATE)), w, 256)
    hs = _attn(hs, cache_mem_k[l].reshape(bs, N_MEM, D_MODEL), cache_mem_v[l].reshape(bs, N_MEM, D_MODEL),
               ln_mem, wq, wo, 256)
    y_sample = _peer(hs.reshape(bs * ss, D_MODEL), w, 256, 64).reshape(bs, ss, D_MODEL)

    kv_shape = (1, bp, N_MEM, MEM_HEADS, MEM_HEAD_DIM)
    st_shape = (SSD_HEADS, SSD_HEAD_DIM, SSD_STATE)
    return (y_prompt, y_sample, pa[None], ps[None], ph.reshape((1, bp) + st_shape), mk.reshape(kv_shape),
            mv.reshape(kv_shape), sa[None], ssd[None], sh.reshape((1, bs) + st_shape))
```

```python
import functools
import math

import jax
import jax.numpy as jnp
from jax import lax
from jax.experimental import pallas as pl
from jax.experimental.pallas import tpu as pltpu

D_MODEL = 1024
CHUNK = 64
D_CONV = 512
CONV_A_W = 3
D_SSD = 512
SSD_HEAD_DIM = 64
SSD_HEADS = 8
SSD_GROUPS = 2
SSD_HPG = 4
SSD_STATE = 128
SSD_CONV_W = 4
XBC_DIM = 1024
N_MEM = 256
MEM_HEADS = 4
MEM_HEAD_DIM = 256
PEER_HEADS = 8
N_KEYS = 128
N_EXPERTS = N_KEYS * N_KEYS
PEER_TOPK = 16
PEER_DK = 256
PEER_DK_HALF = 128
PICKS = PEER_HEADS * PEER_TOPK
EPS = 1e-6

LANES = 128
SUBLANES = 8
HALF_EXPERTS = N_EXPERTS // 2
PAIR_STATE = (SSD_HEADS // 2, 2 * SSD_HEAD_DIM, SSD_STATE)
VMEM_LIMIT = 56 * 1024 * 1024

F32 = jnp.float32
BF16 = jnp.bfloat16
HI = lax.Precision.HIGHEST


def _rms(x, g):
    return x * lax.rsqrt(jnp.mean(x * x, axis=-1, keepdims=True) + EPS) * g


def _silu(x):
    return x * (1.0 / (1.0 + jnp.exp(-x)))


def _softplus(x):
    return jnp.maximum(x, 0.0) + jnp.log1p(jnp.exp(-jnp.abs(x)))


def _dot(a, b):
    return jnp.dot(a.astype(BF16), b.astype(BF16), preferred_element_type=F32)


def _dot_nt(a, b):
    return lax.dot_general(a.astype(BF16), b.astype(BF16), (((1,), (1,)), ((), ())),
                           preferred_element_type=F32)


def _dot_tn(a, b):
    return lax.dot_general(a.astype(BF16), b.astype(BF16), (((0,), (0,)), ((), ())),
                           preferred_element_type=F32)


def _const_spec(shape):
    n = len(shape)
    return pl.BlockSpec(shape, lambda *_: (0,) * n)


def _mixer_kernel(has_hist, ts, lc, *refs):
    if has_hist:
        x_ref, hista_ref, hists_ref, h0_ref = refs[:4]
        refs = refs[4:]
    else:
        x_ref = refs[0]
        refs = refs[1:]
    (lnw_ref, win_ref, wdt_ref, wdtt_ref, caw_ref, csw_ref, csb_ref, dtb_c_ref, alog_c_ref, dsk_c_ref,
     dtb_r_ref, alog_r_ref, nw_ref, wout_ref,
     h1_ref, na_ref, ns_ref, nh_ref, cata, cats, hst) = refs
    s = pl.program_id(1)
    pad = SUBLANES

    @pl.when(s == 0)
    def _():
        if has_hist:
            cata[pad - 2:pad, :] = hista_ref[...]
            cats[pad - 3:pad, :] = hists_ref[...]
            hst[...] = h0_ref[...]
        else:
            cata[0:pad, :] = jnp.zeros((pad, D_CONV), F32)
            cats[0:pad, :] = jnp.zeros((pad, XBC_DIM), F32)
            hst[...] = jnp.zeros(hst.shape, F32)

    x = x_ref[...]
    xn = _rms(x, lnw_ref[...]).astype(BF16)
    proj = jnp.dot(xn, win_ref[...], preferred_element_type=F32)
    dt_c = jnp.dot(xn, wdt_ref[...], preferred_element_type=F32)
    dt_r = lax.dot_general(wdtt_ref[...], xn, (((1,), (1,)), ((), ())),
                           preferred_element_type=F32)
    g_b = proj[:, 0:D_CONV]
    g_c = proj[:, D_CONV:2 * D_CONV]
    v_in = proj[:, 2 * D_CONV:3 * D_CONV]
    z = proj[:, 3 * D_CONV:3 * D_CONV + D_SSD]
    xbc = proj[:, 3 * D_CONV + D_SSD:3 * D_CONV + D_SSD + XBC_DIM]

    ua = g_c * v_in
    cata[pad:pad + ts, :] = ua
    caw = caw_ref[...]
    conv_a = (caw[0:1] * cata[pad - 2:pad - 2 + ts, :] + caw[1:2] * cata[pad - 1:pad - 1 + ts, :]
              + caw[2:3] * ua)
    y_a = g_b * conv_a
    hist_a = cata[pad + ts - 2:pad + ts, :]
    na_ref[...] = hist_a
    cata[pad - 2:pad, :] = hist_a

    cats[pad:pad + ts, :] = xbc
    csw = csw_ref[...]
    conv_s = (csw[0:1] * cats[pad - 3:pad - 3 + ts, :] + csw[1:2] * cats[pad - 2:pad - 2 + ts, :]
              + csw[2:3] * cats[pad - 1:pad - 1 + ts, :] + csw[3:4] * xbc)
    hist_s = cats[pad + ts - 3:pad + ts, :]
    ns_ref[...] = hist_s
    cats[pad - 3:pad, :] = hist_s
    xbc_c = _silu(conv_s + csb_ref[...])
    xs = xbc_c[:, 0:D_SSD]

    dtp_c = _softplus(dt_c + dtb_c_ref[...])
    dtp_r = _softplus(dt_r + dtb_r_ref[...])
    a_c = dtp_c * (-jnp.exp(alog_c_ref[...]))
    a_r = dtp_r * (-jnp.exp(alog_r_ref[...]))
    dsk_c = dsk_c_ref[...]

    lane = lax.broadcasted_iota(jnp.int32, (1, LANES), 1)
    first_head = lane < SSD_HEAD_DIM
    row2 = lax.broadcasted_iota(jnp.int32, (2 * SSD_HEAD_DIM, 1), 0) < SSD_HEAD_DIM
    li = lax.broadcasted_iota(jnp.int32, (lc, lc), 0)
    si = lax.broadcasted_iota(jnp.int32, (lc, lc), 1)
    causal = si <= li
    tril = causal.astype(F32)
    jrow = lax.broadcasted_iota(jnp.int32, (ts, lc), 0)
    scol = lax.broadcasted_iota(jnp.int32, (ts, lc), 1)

    def pair(col, h):
        return jnp.where(first_head, col[:, h:h + 1], col[:, h + 1:h + 2])

    y_chunks = []
    for c in range(ts // lc):
        r0 = c * lc
        rows = slice(r0, r0 + lc)
        acum_c = jnp.dot(tril, a_c[rows], preferred_element_type=F32, precision=HI)
        upper = ((jrow >= r0) & (jrow <= r0 + scol)).astype(F32)
        acum_r = jnp.dot(a_r, upper, preferred_element_type=F32, precision=HI)
        last_c = acum_c[lc - 1:lc, :]
        dec_c = jnp.exp(last_c - acum_c)
        eac_c = jnp.exp(acum_c)
        y_pairs = []
        for g in range(SSD_GROUPS):
            b_g = xbc_c[rows, D_SSD + g * SSD_STATE:D_SSD + (g + 1) * SSD_STATE]
            c_g = xbc_c[rows, D_SSD + (SSD_GROUPS + g) * SSD_STATE:D_SSD + (SSD_GROUPS + g + 1) * SSD_STATE]
            cb = _dot_nt(c_g, b_g)
            for q in range(SSD_HPG // 2):
                h = g * SSD_HPG + 2 * q
                pi = h // 2
                xs_p = xs[rows, pi * LANES:(pi + 1) * LANES]
                xdt = xs_p * pair(dtp_c[rows], h)
                res = []
                for hh in (h, h + 1):
                    seg = acum_c[:, hh:hh + 1] - acum_r[hh:hh + 1, :]
                    m_h = cb * jnp.exp(jnp.where(causal, seg, -jnp.inf))
                    res.append(_dot(m_h, xdt))
                y_diag = jnp.where(first_head, res[0], res[1])
                h_in = hst[pi]
                y_off = _dot_nt(c_g, h_in) * pair(eac_c, h)
                st = _dot_tn(xdt * pair(dec_c, h), b_g)
                cd = jnp.exp(jnp.where(row2, last_c[:, h:h + 1], last_c[:, h + 1:h + 2]))
                hst[pi] = cd * h_in + st
                y_pairs.append(y_diag + y_off + pair(dsk_c, h) * xs_p)
        y_chunks.append(jnp.concatenate(y_pairs, axis=1))
    y = y_chunks[0] if len(y_chunks) == 1 else jnp.concatenate(y_chunks, axis=0)
    y_b = _rms(y * _silu(z), nw_ref[...])
    wout = wout_ref[...]
    out = (jnp.dot(y_a.astype(BF16), wout[0:D_CONV], preferred_element_type=F32)
           + jnp.dot(y_b.astype(BF16), wout[D_CONV:], preferred_element_type=F32))
    h1_ref[...] = x + out
    nh_ref[...] = hst[...]


def _mixer(x, hist, w, ts):
    b, s, _ = x.shape
    lc = min(CHUNK, s)
    ts = min(ts, s)
    has_hist = hist is not None
    per_b3 = lambda shape: pl.BlockSpec((None,) + shape, lambda i, j: (i,) + (0,) * len(shape))
    in_specs = [pl.BlockSpec((None, ts, D_MODEL), lambda i, j: (i, j, 0))]
    args = [x]
    if has_hist:
        in_specs += [per_b3((CONV_A_W - 1, D_CONV)), per_b3((SSD_CONV_W - 1, XBC_DIM)),
                     per_b3(PAIR_STATE)]
        args += list(hist)
    wargs = [w["ln_mix"], w["w_in"], w["w_dt"], w["w_dtt"], w["conv_a_w"], w["conv_s_w"], w["conv_s_b"],
             w["dtb_c"], w["alog_c"], w["dsk_c"], w["dtb_r"], w["alog_r"], w["ssd_norm"], w["w_out"]]
    in_specs += [_const_spec(a.shape) for a in wargs]
    args += wargs
    out_shape = (jax.ShapeDtypeStruct((b, s, D_MODEL), F32),
                 jax.ShapeDtypeStruct((b, CONV_A_W - 1, D_CONV), F32),
                 jax.ShapeDtypeStruct((b, SSD_CONV_W - 1, XBC_DIM), F32),
                 jax.ShapeDtypeStruct((b,) + PAIR_STATE, F32))
    out_specs = (pl.BlockSpec((None, ts, D_MODEL), lambda i, j: (i, j, 0)),
                 per_b3((CONV_A_W - 1, D_CONV)), per_b3((SSD_CONV_W - 1, XBC_DIM)),
                 per_b3(PAIR_STATE))
    return pl.pallas_call(
        functools.partial(_mixer_kernel, has_hist, ts, lc),
        out_shape=out_shape, grid=(b, s // ts), in_specs=in_specs, out_specs=out_specs,
        scratch_shapes=[pltpu.VMEM((SUBLANES + ts, D_CONV), F32), pltpu.VMEM((SUBLANES + ts, XBC_DIM), F32),
                        pltpu.VMEM(PAIR_STATE, F32)],
        compiler_params=pltpu.CompilerParams(dimension_semantics=("parallel", "arbitrary"),
                                             vmem_limit_bytes=VMEM_LIMIT),
        name="mixer_hist" if has_hist else "mixer",
    )(*args)


def _memkv_kernel(m_ref, g_ref, wk_ref, wv_ref, k_ref, v_ref):
    mn = _rms(m_ref[...], g_ref[...]).astype(BF16)
    k_ref[...] = jnp.dot(mn, wk_ref[...], preferred_element_type=F32)
    v_ref[...] = jnp.dot(mn, wv_ref[...], preferred_element_type=F32)


def _memory_kv(mem, g, wk, wv, tm=256):
    t = mem.shape[0]
    blk = pl.BlockSpec((tm, D_MODEL), lambda i: (i, 0))
    return pl.pallas_call(
        _memkv_kernel,
        out_shape=(jax.ShapeDtypeStruct((t, D_MODEL), F32),) * 2,
        grid=(t // tm,), in_specs=[blk, _const_spec(g.shape), _const_spec(wk.shape), _const_spec(wv.shape)],
        out_specs=(blk, blk),
        compiler_params=pltpu.CompilerParams(dimension_semantics=("parallel",), vmem_limit_bytes=VMEM_LIMIT),
        name="memory_kv",
    )(mem, g, wk, wv)


def _attn_kernel(x_ref, k_ref, v_ref, g_ref, wq_ref, wo_ref, o_ref):
    x = x_ref[...]
    xn = _rms(x, g_ref[...]).astype(BF16)
    q = jnp.dot(xn, wq_ref[...], preferred_element_type=F32)
    k = k_ref[...]
    v = v_ref[...]
    wo = wo_ref[...]
    acc = x
    for h in range(MEM_HEADS):
        cols = slice(h * MEM_HEAD_DIM, (h + 1) * MEM_HEAD_DIM)
        sc = _dot_nt(q[:, cols], k[:, cols]) * (MEM_HEAD_DIM ** -0.5)
        sc = sc - jnp.max(sc, axis=-1, keepdims=True)
        p = jnp.exp(sc)
        p = p / jnp.sum(p, axis=-1, keepdims=True)
        o_h = _dot(p, v[:, cols])
        acc = acc + jnp.dot(o_h.astype(BF16), wo[cols, :], preferred_element_type=F32)
    o_ref[...] = acc


def _attn(x, k, v, g, wq, wo, ts):
    b, s, _ = x.shape
    ts = min(ts, s)
    xblk = pl.BlockSpec((None, ts, D_MODEL), lambda i, j: (i, j, 0))
    kvblk = pl.BlockSpec((None, N_MEM, D_MODEL), lambda i, j: (i, 0, 0))
    return pl.pallas_call(
        _attn_kernel,
        out_shape=jax.ShapeDtypeStruct((b, s, D_MODEL), F32),
        grid=(b, s // ts),
        in_specs=[xblk, kvblk, kvblk, _const_spec(g.shape), _const_spec(wq.shape), _const_spec(wo.shape)],
        out_specs=xblk,
        compiler_params=pltpu.CompilerParams(dimension_semantics=("parallel", "parallel"),
                                             vmem_limit_bytes=VMEM_LIMIT),
        name="attn",
    )(x, k, v, g, wq, wo)


def _top_k_rows(sc, iota, n, k):
    vals, idxs = [], []
    for _ in range(k):
        m = jnp.max(sc, axis=0, keepdims=True)
        idx = jnp.min(jnp.where(sc == m, iota, n), axis=0, keepdims=True)
        sc = jnp.where(iota == idx, -jnp.inf, sc)
        vals.append(m)
        idxs.append(idx)
    return jnp.concatenate(vals, axis=0), jnp.concatenate(idxs, axis=0)


def _select_kernel(tb, x_ref, g_ref, wq_ref, keys_ref, xn_ref, tile_ref, shift_ref, gate_ref):
    h = pl.program_id(1)
    xn = _rms(x_ref[...], g_ref[...])

    @pl.when(h == 0)
    def _():
        xn_ref[...] = xn

    q = jnp.dot(xn.astype(BF16), wq_ref[...], preferred_element_type=F32)
    iota_k = lax.broadcasted_iota(jnp.int32, (N_KEYS, tb), 0)
    v1, i1 = _top_k_rows(_dot_nt(keys_ref[0, 0], q[:, 0:PEER_DK_HALF]), iota_k, N_KEYS, PEER_TOPK)
    v2, i2 = _top_k_rows(_dot_nt(keys_ref[0, 1], q[:, PEER_DK_HALF:PEER_DK]), iota_k, N_KEYS, PEER_TOPK)
    comb = jnp.concatenate([v1[a:a + 1] + v2 for a in range(PEER_TOPK)], axis=0)
    cidx = jnp.concatenate([i1[a:a + 1] * N_KEYS + i2 for a in range(PEER_TOPK)], axis=0)
    n2 = PEER_TOPK * PEER_TOPK
    iota_c = lax.broadcasted_iota(jnp.int32, (n2, tb), 0)
    vals, eids = [], []
    for _ in range(PEER_TOPK):
        m = jnp.max(comb, axis=0, keepdims=True)
        pos = jnp.min(jnp.where(comb == m, iota_c, n2), axis=0, keepdims=True)
        hit = iota_c == pos
        eids.append(jnp.sum(jnp.where(hit, cidx, 0), axis=0, keepdims=True))
        comb = jnp.where(hit, -jnp.inf, comb)
        vals.append(m)
    top = jnp.concatenate(vals, axis=0)
    eid = jnp.concatenate(eids, axis=0)
    ex = jnp.exp(top - top[0:1])
    gate_ref[0] = ex / jnp.sum(ex, axis=0, keepdims=True)
    upper = eid >= HALF_EXPERTS
    tile_ref[0] = jnp.where(upper, eid - HALF_EXPERTS, eid) * SUBLANES
    shift_ref[0] = jnp.where(upper, 0, 16)


def _select(x, g, wq, keys, tb):
    t = x.shape[0]
    xblk = pl.BlockSpec((tb, D_MODEL), lambda i, j: (i, 0))
    oblk = pl.BlockSpec((1, PEER_TOPK, tb), lambda i, j: (j, 0, i))
    return pl.pallas_call(
        functools.partial(_select_kernel, tb),
        out_shape=(jax.ShapeDtypeStruct((t, D_MODEL), F32),
                   jax.ShapeDtypeStruct((PEER_HEADS, PEER_TOPK, t), jnp.int32),
                   jax.ShapeDtypeStruct((PEER_HEADS, PEER_TOPK, t), jnp.int32),
                   jax.ShapeDtypeStruct((PEER_HEADS, PEER_TOPK, t), F32)),
        grid=(t // tb, PEER_HEADS),
        in_specs=[xblk, _const_spec(g.shape),
                  pl.BlockSpec((D_MODEL, PEER_DK), lambda i, j: (0, j)),
                  pl.BlockSpec((1, 2, N_KEYS, PEER_DK_HALF), lambda i, j: (j, 0, 0, 0))],
        out_specs=(xblk, oblk, oblk, oblk),
        compiler_params=pltpu.CompilerParams(dimension_semantics=("parallel", "arbitrary"),
                                             vmem_limit_bytes=VMEM_LIMIT),
        name="peer_select",
    )(x, g, wq, keys)


HI_HALF_MASK = -(1 << 16)
_BITREV3 = (0, 4, 2, 6, 1, 5, 3, 7)


def _unpack(word, shift):
    return lax.bitcast_convert_type((word << shift) & HI_HALF_MASK, F32)


def _splat_picks(blk, one_pass):
    tb = blk.shape[0]
    r = lax.broadcasted_iota(jnp.int32, (PICKS, LANES), 0)
    c = lax.broadcasted_iota(jnp.int32, (PICKS, LANES), 1)
    eye = (r == c)[None]
    ones = jnp.ones((LANES, LANES), BF16)

    def one(piece):
        diag = jnp.where(eye, piece[:, None, :], 0.0).astype(BF16).reshape(tb * PICKS, LANES)
        return jnp.dot(diag, ones, preferred_element_type=F32)

    if one_pass:
        out = one(blk)
    else:
        hi = blk.astype(BF16).astype(F32)
        rest = blk - hi
        mid = rest.astype(BF16).astype(F32)
        out = one(hi) + one(mid) + one(rest - mid)
    return out.reshape(tb, PICKS, LANES)


def _full(ref, k):
    return jnp.broadcast_to(ref[k:k + 1, :], (SUBLANES, LANES))


def _expert_row(tab_ref, tile_ref, t, k, shifts):
    idx = pl.multiple_of(tile_ref[t, k], SUBLANES)
    return _unpack(tab_ref[pl.ds(idx, SUBLANES), :], _full(shifts, k))


def _hidden_kernel(tb, tile_ref, shift_ref, x_ref, gate_ref, tab_ref, gh_ref, bsh, stage, parts):
    bsh[...] = _splat_picks(shift_ref[...].astype(F32), True).astype(jnp.int32)

    def token(t, carry):
        x = x_ref[t]
        shifts = bsh.at[t]
        for k in range(PICKS):
            stage[k * SUBLANES:(k + 1) * SUBLANES, :] = x * _expert_row(tab_ref, tile_ref, t, k, shifts)
        part = stage[pl.ds(0, PICKS, stride=SUBLANES), :]
        for s in range(1, SUBLANES):
            part = part + stage[pl.ds(s, PICKS, stride=SUBLANES), :]
        parts[t] = part
        return carry

    lax.fori_loop(0, tb, token, 0)
    p = parts[...].reshape(tb * PICKS, LANES)
    ones = jnp.ones((LANES, LANES), BF16)
    hi = p.astype(BF16)
    rest = p - hi.astype(F32)
    mid = rest.astype(BF16)
    lo = (rest - mid.astype(F32)).astype(BF16)
    sums = (jnp.dot(hi, ones, preferred_element_type=F32) + jnp.dot(mid, ones, preferred_element_type=F32)
            + jnp.dot(lo, ones, preferred_element_type=F32)).reshape(tb, PICKS, LANES)
    r = lax.broadcasted_iota(jnp.int32, (PICKS, LANES), 0)
    c = lax.broadcasted_iota(jnp.int32, (PICKS, LANES), 1)
    hid = jnp.sum(jnp.where((r == c)[None], sums, 0.0), axis=1)
    gelu = 0.5 * hid * (1.0 + lax.erf(hid * (2.0 ** -0.5)))
    gh_ref[...] = gate_ref[...] * gelu


def _pick_block(tb, space=None):
    return pl.BlockSpec((tb, PICKS), lambda i: (i, 0), memory_space=space)


def _table_spec(tab):
    return pl.BlockSpec(tab.shape, lambda i: (0, 0), pipeline_mode=pl.Buffered(1))


def _hidden(tile, shift, xg, gate, tab, tb):
    t = xg.shape[0]
    return pl.pallas_call(
        functools.partial(_hidden_kernel, tb),
        out_shape=jax.ShapeDtypeStruct((t, PICKS), F32),
        grid=(t // tb,),
        in_specs=[_pick_block(tb, pltpu.SMEM), _pick_block(tb),
                  pl.BlockSpec((tb, SUBLANES, LANES), lambda i: (i, 0, 0)),
                  _pick_block(tb), _table_spec(tab)],
        out_specs=_pick_block(tb),
        scratch_shapes=[pltpu.VMEM((tb, PICKS, LANES), jnp.int32), pltpu.VMEM((PICKS * SUBLANES, LANES), F32),
                        pltpu.VMEM((tb, PICKS, LANES), F32)],
        compiler_params=pltpu.CompilerParams(dimension_semantics=("parallel",), vmem_limit_bytes=VMEM_LIMIT),
        name="peer_hidden",
    )(tile, shift, xg, gate, tab)


def _combine_kernel(tb, tile_ref, shift_ref, gh_ref, tab_ref, o_ref, bsh, bg):
    bsh[...] = _splat_picks(shift_ref[...].astype(F32), True).astype(jnp.int32)
    bg[...] = _splat_picks(gh_ref[...], False)

    def token(t, carry):
        shifts, gates = bsh.at[t], bg.at[t]
        acc = [jnp.zeros((SUBLANES, LANES), F32) for _ in range(4)]
        for k in range(PICKS):
            acc[k % 4] = acc[k % 4] + _full(gates, k) * _expert_row(tab_ref, tile_ref, t, k, shifts)
        o_ref[t] = (acc[0] + acc[1]) + (acc[2] + acc[3])
        return carry

    lax.fori_loop(0, tb, token, 0)


def _combine(tile, shift, gh, tab, tb):
    t = gh.shape[0]
    return pl.pallas_call(
        functools.partial(_combine_kernel, tb),
        out_shape=jax.ShapeDtypeStruct((t, SUBLANES, LANES), F32),
        grid=(t // tb,),
        in_specs=[_pick_block(tb, pltpu.SMEM), _pick_block(tb), _pick_block(tb), _table_spec(tab)],
        out_specs=pl.BlockSpec((tb, SUBLANES, LANES), lambda i: (i, 0, 0)),
        scratch_shapes=[pltpu.VMEM((tb, PICKS, LANES), jnp.int32), pltpu.VMEM((tb, PICKS, LANES), F32)],
        compiler_params=pltpu.CompilerParams(dimension_semantics=("parallel",), vmem_limit_bytes=VMEM_LIMIT),
        name="peer_combine",
    )(tile, shift, gh, tab)


def _final_kernel(x_ref, p_ref, g_ref, o_ref):
    o_ref[...] = _rms(x_ref[...] + p_ref[...], g_ref[...])


def _final(x, p, g, tb):
    t = x.shape[0]
    blk = pl.BlockSpec((tb, D_MODEL), lambda i: (i, 0))
    return pl.pallas_call(
        _final_kernel, out_shape=jax.ShapeDtypeStruct((t, D_MODEL), F32), grid=(t // tb,),
        in_specs=[blk, blk, _const_spec(g.shape)], out_specs=blk,
        compiler_params=pltpu.CompilerParams(dimension_semantics=("parallel",), vmem_limit_bytes=VMEM_LIMIT),
        name="final_norm",
    )(x, p, g)


def _pack_table(tab):
    bits = lax.bitcast_convert_type(tab.astype(BF16), jnp.uint16).astype(jnp.uint32)
    packed = (bits[HALF_EXPERTS:] << 16) | bits[:HALF_EXPERTS]
    return lax.bitcast_convert_type(packed, jnp.int32).reshape(HALF_EXPERTS * SUBLANES, LANES)


def _col(v, n=LANES):
    return jnp.pad(v.astype(F32), (0, n - v.shape[0])).reshape(1, n)


def _row(v, n=2 * SUBLANES):
    return jnp.pad(v.astype(F32), (0, n - v.shape[0])).reshape(n, 1)


def _peer(x, w, tb_sel, tb_exp):
    t = x.shape[0]
    xn, tile, shift, gate = _select(x, w["ln_ffn"], w["w_pq"], w["sub_keys"], tb_sel)
    to_tok = lambda a: a.reshape(PICKS, t).T
    tile_t, shift_t = to_tok(tile), to_tok(shift)
    gh = _hidden(tile_t, shift_t, xn.reshape(t, SUBLANES, LANES), to_tok(gate), w["u_pack"], tb_exp)
    out = _combine(tile_t, shift_t, gh, w["v_pack"], tb_exp)
    return _final(x, out.reshape(t, D_MODEL), w["final_norm"], tb_sel)


def kernel(x_prompt, x_sample, mem_prompt, cache_conv_a, cache_conv_ssd, state_ssd, cache_mem_k, cache_mem_v, ln_mix_w, w_in, conv_a_w, conv_s_w, conv_s_b, dt_bias, a_log, d_skip, ssd_norm_w, w_out, ln_mem_w, mem_norm_w, w_mq, w_mk, w_mv, w_mo, ln_ffn_w, w_pq, sub_keys, u_tab, v_tab, final_norm_w):
    depth = w_in.shape[0]
    assert depth == 1
    l = 0
    n_main = 3 * D_CONV + D_SSD + XBC_DIM
    w_dt = w_in[l][:, n_main:]
    w = {
        "ln_mix": ln_mix_w[l].reshape(1, D_MODEL),
        "w_in": w_in[l][:, :n_main].astype(BF16),
        "w_dt": jnp.pad(w_dt, ((0, 0), (0, LANES - SSD_HEADS))).astype(BF16),
        "w_dtt": jnp.pad(w_dt.T, ((0, 2 * SUBLANES - SSD_HEADS), (0, 0))).astype(BF16),
        "conv_a_w": conv_a_w[l], "conv_s_w": conv_s_w[l], "conv_s_b": conv_s_b[l].reshape(1, XBC_DIM),
        "dtb_c": _col(dt_bias[l]), "alog_c": _col(a_log[l]), "dsk_c": _col(d_skip[l]),
        "dtb_r": _row(dt_bias[l]), "alog_r": _row(a_log[l]),
        "ssd_norm": ssd_norm_w[l].reshape(1, D_SSD),
        "w_out": w_out[l].astype(BF16),
        "ln_ffn": ln_ffn_w[l].reshape(1, D_MODEL),
        "w_pq": w_pq[l].astype(BF16),
        "sub_keys": sub_keys[l].astype(BF16),
        "u_pack": _pack_table(u_tab[l]),
        "v_pack": _pack_table(v_tab[l]),
        "final_norm": final_norm_w.reshape(1, D_MODEL),
    }
    ln_mem = ln_mem_w[l].reshape(1, D_MODEL)
    wq, wo = w_mq[l].astype(BF16), w_mo[l].astype(BF16)

    bp, sp, _ = x_prompt.shape
    bs, ss, _ = x_sample.shape

    mk, mv = _memory_kv(mem_prompt.reshape(bp * N_MEM, D_MODEL), mem_norm_w[l].reshape(1, D_MODEL),
                        w_mk[l].astype(BF16), w_mv[l].astype(BF16))
    hp, pa, ps, ph = _mixer(x_prompt, None, w, 256)
    hp = _attn(hp, mk.reshape(bp, N_MEM, D_MODEL), mv.reshape(bp, N_MEM, D_MODEL), ln_mem, wq, wo, 256)
    y_prompt = _peer(hp.reshape(bp * sp, D_MODEL), w, 256, 64).reshape(bp, sp, D_MODEL)

    hs, sa, ssd, sh = _mixer(x_sample, (cache_conv_a[l], cache_conv_ssd[l],
                                        state_ssd[l].reshape((bs,) + PAIR_STATE)), w, 256)
    hs = _attn(hs, cache_mem_k[l].reshape(bs, N_MEM, D_MODEL), cache_mem_v[l].reshape(bs, N_MEM, D_MODEL),
               ln_mem, wq, wo, 256)
    y_sample = _peer(hs.reshape(bs * ss, D_MODEL), w, 256, 64).reshape(bs, ss, D_MODEL)

    kv_shape = (1, bp, N_MEM, MEM_HEADS, MEM_HEAD_DIM)
    st_shape = (SSD_HEADS, SSD_HEAD_DIM, SSD_STATE)
    return (y_prompt, y_sample, pa[None], ps[None], ph.reshape((1, bp) + st_shape), mk.reshape(kv_shape),
            mv.reshape(kv_shape), sa[None], ssd[None], sh.reshape((1, bs) + st_shape))
```

```python
import functools
import math

import jax
import jax.numpy as jnp
from jax import lax
from jax.experimental import pallas as pl
from jax.experimental.pallas import tpu as pltpu

D_MODEL = 1024
CHUNK = 64
D_CONV = 512
CONV_A_W = 3
D_SSD = 512
SSD_HEAD_DIM = 64
SSD_HEADS = 8
SSD_GROUPS = 2
SSD_HPG = 4
SSD_STATE = 128
SSD_CONV_W = 4
XBC_DIM = 1024
N_MEM = 256
MEM_HEADS = 4
MEM_HEAD_DIM = 256
PEER_HEADS = 8
N_KEYS = 128
N_EXPERTS = N_KEYS * N_KEYS
PEER_TOPK = 16
PEER_DK = 256
PEER_DK_HALF = 128
PICKS = PEER_HEADS * PEER_TOPK
EPS = 1e-6

LANES = 128
SUBLANES = 8
HALF_EXPERTS = N_EXPERTS // 2
PAIR_STATE = (SSD_HEADS // 2, 2 * SSD_HEAD_DIM, SSD_STATE)
VMEM_LIMIT = 56 * 1024 * 1024

F32 = jnp.float32
BF16 = jnp.bfloat16
HI = lax.Precision.HIGHEST


def _rms(x, g):
    return x * lax.rsqrt(jnp.mean(x * x, axis=-1, keepdims=True) + EPS) * g


def _silu(x):
    return x * (1.0 / (1.0 + jnp.exp(-x)))


def _softplus(x):
    return jnp.maximum(x, 0.0) + jnp.log1p(jnp.exp(-jnp.abs(x)))


def _dot(a, b):
    return jnp.dot(a.astype(BF16), b.astype(BF16), preferred_element_type=F32)


def _dot_nt(a, b):
    return lax.dot_general(a.astype(BF16), b.astype(BF16), (((1,), (1,)), ((), ())),
                           preferred_element_type=F32)


def _dot_tn(a, b):
    return lax.dot_general(a.astype(BF16), b.astype(BF16), (((0,), (0,)), ((), ())),
                           preferred_element_type=F32)


def _const_spec(shape):
    n = len(shape)
    return pl.BlockSpec(shape, lambda *_: (0,) * n)


def _mixer_kernel(has_hist, ts, lc, *refs):
    if has_hist:
        x_ref, hista_ref, hists_ref, h0_ref = refs[:4]
        refs = refs[4:]
    else:
        x_ref = refs[0]
        refs = refs[1:]
    (lnw_ref, win_ref, wdt_ref, wdtt_ref, caw_ref, csw_ref, csb_ref, dtb_c_ref, alog_c_ref, dsk_c_ref,
     dtb_r_ref, alog_r_ref, nw_ref, wout_ref,
     h1_ref, na_ref, ns_ref, nh_ref, cata, cats, hst) = refs
    s = pl.program_id(1)
    pad = SUBLANES

    @pl.when(s == 0)
    def _():
        if has_hist:
            cata[pad - 2:pad, :] = hista_ref[...]
            cats[pad - 3:pad, :] = hists_ref[...]
            hst[...] = h0_ref[...]
        else:
            cata[0:pad, :] = jnp.zeros((pad, D_CONV), F32)
            cats[0:pad, :] = jnp.zeros((pad, XBC_DIM), F32)
            hst[...] = jnp.zeros(hst.shape, F32)

    x = x_ref[...]
    xn = _rms(x, lnw_ref[...]).astype(BF16)
    proj = jnp.dot(xn, win_ref[...], preferred_element_type=F32)
    dt_c = jnp.dot(xn, wdt_ref[...], preferred_element_type=F32)
    dt_r = lax.dot_general(wdtt_ref[...], xn, (((1,), (1,)), ((), ())),
                           preferred_element_type=F32)
    g_b = proj[:, 0:D_CONV]
    g_c = proj[:, D_CONV:2 * D_CONV]
    v_in = proj[:, 2 * D_CONV:3 * D_CONV]
    z = proj[:, 3 * D_CONV:3 * D_CONV + D_SSD]
    xbc = proj[:, 3 * D_CONV + D_SSD:3 * D_CONV + D_SSD + XBC_DIM]

    ua = g_c * v_in
    cata[pad:pad + ts, :] = ua
    caw = caw_ref[...]
    conv_a = (caw[0:1] * cata[pad - 2:pad - 2 + ts, :] + caw[1:2] * cata[pad - 1:pad - 1 + ts, :]
              + caw[2:3] * ua)
    y_a = g_b * conv_a
    hist_a = cata[pad + ts - 2:pad + ts, :]
    na_ref[...] = hist_a
    cata[pad - 2:pad, :] = hist_a

    cats[pad:pad + ts, :] = xbc
    csw = csw_ref[...]
    conv_s = (csw[0:1] * cats[pad - 3:pad - 3 + ts, :] + csw[1:2] * cats[pad - 2:pad - 2 + ts, :]
              + csw[2:3] * cats[pad - 1:pad - 1 + ts, :] + csw[3:4] * xbc)
    hist_s = cats[pad + ts - 3:pad + ts, :]
    ns_ref[...] = hist_s
    cats[pad - 3:pad, :] = hist_s
    xbc_c = _silu(conv_s + csb_ref[...])
    xs = xbc_c[:, 0:D_SSD]

    dtp_c = _softplus(dt_c + dtb_c_ref[...])
    dtp_r = _softplus(dt_r + dtb_r_ref[...])
    a_c = dtp_c * (-jnp.exp(alog_c_ref[...]))
    a_r = dtp_r * (-jnp.exp(alog_r_ref[...]))
    dsk_c = dsk_c_ref[...]

    lane = lax.broadcasted_iota(jnp.int32, (1, LANES), 1)
    first_head = lane < SSD_HEAD_DIM
    row2 = lax.broadcasted_iota(jnp.int32, (2 * SSD_HEAD_DIM, 1), 0) < SSD_HEAD_DIM
    li = lax.broadcasted_iota(jnp.int32, (lc, lc), 0)
    si = lax.broadcasted_iota(jnp.int32, (lc, lc), 1)
    causal = si <= li
    tril = causal.astype(F32)
    jrow = lax.broadcasted_iota(jnp.int32, (ts, lc), 0)
    scol = lax.broadcasted_iota(jnp.int32, (ts, lc), 1)

    def pair(col, h):
        return jnp.where(first_head, col[:, h:h + 1], col[:, h + 1:h + 2])

    y_chunks = []
    for c in range(ts // lc):
        r0 = c * lc
        rows = slice(r0, r0 + lc)
        acum_c = jnp.dot(tril, a_c[rows], preferred_element_type=F32, precision=HI)
        upper = ((jrow >= r0) & (jrow <= r0 + scol)).astype(F32)
        acum_r = jnp.dot(a_r, upper, preferred_element_type=F32, precision=HI)
        last_c = acum_c[lc - 1:lc, :]
        dec_c = jnp.exp(last_c - acum_c)
        eac_c = jnp.exp(acum_c)
        y_pairs = []
        for g in range(SSD_GROUPS):
            b_g = xbc_c[rows, D_SSD + g * SSD_STATE:D_SSD + (g + 1) * SSD_STATE]
            c_g = xbc_c[rows, D_SSD + (SSD_GROUPS + g) * SSD_STATE:D_SSD + (SSD_GROUPS + g + 1) * SSD_STATE]
            cb = _dot_nt(c_g, b_g)
            for q in range(SSD_HPG // 2):
                h = g * SSD_HPG + 2 * q
                pi = h // 2
                xs_p = xs[rows, pi * LANES:(pi + 1) * LANES]
                xdt = xs_p * pair(dtp_c[rows], h)
                res = []
                for hh in (h, h + 1):
                    seg = acum_c[:, hh:hh + 1] - acum_r[hh:hh + 1, :]
                    m_h = cb * jnp.exp(jnp.where(causal, seg, -jnp.inf))
                    res.append(_dot(m_h, xdt))
                y_diag = jnp.where(first_head, res[0], res[1])
                h_in = hst[pi]
                y_off = _dot_nt(c_g, h_in) * pair(eac_c, h)
                st = _dot_tn(xdt * pair(dec_c, h), b_g)
                cd = jnp.exp(jnp.where(row2, last_c[:, h:h + 1], last_c[:, h + 1:h + 2]))
                hst[pi] = cd * h_in + st
                y_pairs.append(y_diag + y_off + pair(dsk_c, h) * xs_p)
        y_chunks.append(jnp.concatenate(y_pairs, axis=1))
    y = y_chunks[0] if len(y_chunks) == 1 else jnp.concatenate(y_chunks, axis=0)
    y_b = _rms(y * _silu(z), nw_ref[...])
    wout = wout_ref[...]
    out = (jnp.dot(y_a.astype(BF16), wout[0:D_CONV], preferred_element_type=F32)
           + jnp.dot(y_b.astype(BF16), wout[D_CONV:], preferred_element_type=F32))
    h1_ref[...] = x + out
    nh_ref[...] = hst[...]


def _mixer(x, hist, w, ts):
    b, s, _ = x.shape
    lc = min(CHUNK, s)
    ts = min(ts, s)
    has_hist = hist is not None
    per_b3 = lambda shape: pl.BlockSpec((None,) + shape, lambda i, j: (i,) + (0,) * len(shape))
    in_specs = [pl.BlockSpec((None, ts, D_MODEL), lambda i, j: (i, j, 0))]
    args = [x]
    if has_hist:
        in_specs += [per_b3((CONV_A_W - 1, D_CONV)), per_b3((SSD_CONV_W - 1, XBC_DIM)),
                     per_b3(PAIR_STATE)]
        args += list(hist)
    wargs = [w["ln_mix"], w["w_in"], w["w_dt"], w["w_dtt"], w["conv_a_w"], w["conv_s_w"], w["conv_s_b"],
             w["dtb_c"], w["alog_c"], w["dsk_c"], w["dtb_r"], w["alog_r"], w["ssd_norm"], w["w_out"]]
    in_specs += [_const_spec(a.shape) for a in wargs]
    args += wargs
    out_shape = (jax.ShapeDtypeStruct((b, s, D_MODEL), F32),
                 jax.ShapeDtypeStruct((b, CONV_A_W - 1, D_CONV), F32),
                 jax.ShapeDtypeStruct((b, SSD_CONV_W - 1, XBC_DIM), F32),
                 jax.ShapeDtypeStruct((b,) + PAIR_STATE, F32))
    out_specs = (pl.BlockSpec((None, ts, D_MODEL), lambda i, j: (i, j, 0)),
                 per_b3((CONV_A_W - 1, D_CONV)), per_b3((SSD_CONV_W - 1, XBC_DIM)),
                 per_b3(PAIR_STATE))
    return pl.pallas_call(
        functools.partial(_mixer_kernel, has_hist, ts, lc),
        out_shape=out_shape, grid=(b, s // ts), in_specs=in_specs, out_specs=out_specs,
        scratch_shapes=[pltpu.VMEM((SUBLANES + ts, D_CONV), F32), pltpu.VMEM((SUBLANES + ts, XBC_DIM), F32),
                        pltpu.VMEM(PAIR_STATE, F32)],
        compiler_params=pltpu.CompilerParams(dimension_semantics=("parallel", "arbitrary"),
                                             vmem_limit_bytes=VMEM_LIMIT),
        name="mixer_hist" if has_hist else "mixer",
    )(*args)


def _memkv_kernel(m_ref, g_ref, wk_ref, wv_ref, k_ref, v_ref):
    mn = _rms(m_ref[...], g_ref[...]).astype(BF16)
    k_ref[...] = jnp.dot(mn, wk_ref[...], preferred_element_type=F32)
    v_ref[...] = jnp.dot(mn, wv_ref[...], preferred_element_type=F32)


def _memory_kv(mem, g, wk, wv, tm=256):
    t = mem.shape[0]
    blk = pl.BlockSpec((tm, D_MODEL), lambda i: (i, 0))
    return pl.pallas_call(
        _memkv_kernel,
        out_shape=(jax.ShapeDtypeStruct((t, D_MODEL), F32),) * 2,
        grid=(t // tm,), in_specs=[blk, _const_spec(g.shape), _const_spec(wk.shape), _const_spec(wv.shape)],
        out_specs=(blk, blk),
        compiler_params=pltpu.CompilerParams(dimension_semantics=("parallel",), vmem_limit_bytes=VMEM_LIMIT),
        name="memory_kv",
    )(mem, g, wk, wv)


def _attn_kernel(x_ref, k_ref, v_ref, g_ref, wq_ref, wo_ref, o_ref):
    x = x_ref[...]
    xn = _rms(x, g_ref[...]).astype(BF16)
    q = jnp.dot(xn, wq_ref[...], preferred_element_type=F32)
    k = k_ref[...]
    v = v_ref[...]
    wo = wo_ref[...]
    acc = x
    for h in range(MEM_HEADS):
        cols = slice(h * MEM_HEAD_DIM, (h + 1) * MEM_HEAD_DIM)
        sc = _dot_nt(q[:, cols], k[:, cols]) * (MEM_HEAD_DIM ** -0.5)
        sc = sc - jnp.max(sc, axis=-1, keepdims=True)
        p = jnp.exp(sc)
        p = p / jnp.sum(p, axis=-1, keepdims=True)
        o_h = _dot(p, v[:, cols])
        acc = acc + jnp.dot(o_h.astype(BF16), wo[cols, :], preferred_element_type=F32)
    o_ref[...] = acc


def _attn(x, k, v, g, wq, wo, ts):
    b, s, _ = x.shape
    ts = min(ts, s)
    xblk = pl.BlockSpec((None, ts, D_MODEL), lambda i, j: (i, j, 0))
    kvblk = pl.BlockSpec((None, N_MEM, D_MODEL), lambda i, j: (i, 0, 0))
    return pl.pallas_call(
        _attn_kernel,
        out_shape=jax.ShapeDtypeStruct((b, s, D_MODEL), F32),
        grid=(b, s // ts),
        in_specs=[xblk, kvblk, kvblk, _const_spec(g.shape), _const_spec(wq.shape), _const_spec(wo.shape)],
        out_specs=xblk,
        compiler_params=pltpu.CompilerParams(dimension_semantics=("parallel", "parallel"),
                                             vmem_limit_bytes=VMEM_LIMIT),
        name="attn",
    )(x, k, v, g, wq, wo)


def _top_k_rows(sc, k, payload=None):
    n = sc.shape[0]
    rows = lax.broadcasted_iota(jnp.int32, sc.shape, 0).astype(F32)
    vals, outs = [], []
    for _ in range(k):
        m = jnp.max(sc, axis=0, keepdims=True)
        cand = jnp.where(sc == m, rows, float(n))
        first = jnp.min(cand, axis=0, keepdims=True)
        hit = cand == first
        sc = jnp.where(hit, -jnp.inf, sc)
        vals.append(m)
        if payload is None:
            outs.append(first.astype(jnp.int32))
        else:
            outs.append(jnp.sum(jnp.where(hit, payload, 0), axis=0, keepdims=True))
    return jnp.concatenate(vals, axis=0), jnp.concatenate(outs, axis=0)


_LIVE_B = tuple(PEER_TOPK // (a + 1) for a in range(1, SUBLANES))


def _pair_candidates(v1, i1, v2, i2):
    sub = lax.broadcasted_iota(jnp.int32, (SUBLANES, v1.shape[1]), 0)
    comb = [v1[0:1] + v2]
    cidx = [i1[0:1] * N_KEYS + i2]
    for a in range(1, SUBLANES):
        comb.append(jnp.where(sub < _LIVE_B[a - 1], v1[a:a + 1] + v2[0:SUBLANES], -jnp.inf))
        cidx.append(i1[a:a + 1] * N_KEYS + i2[0:SUBLANES])
    comb.append(v1[SUBLANES:] + v2[0:1])
    cidx.append(i1[SUBLANES:] * N_KEYS + i2[0:1])
    return jnp.concatenate(comb, axis=0), jnp.concatenate(cidx, axis=0)


def _select_kernel(tb, x_ref, g_ref, wq_ref, keys_ref, xn_ref, tile_ref, shift_ref, gate_ref):
    @pl.when(pl.program_id(1) == 0)
    def _():
        xn_ref[...] = _rms(x_ref[...], g_ref[...])

    q = jnp.dot(xn_ref[...].astype(BF16), wq_ref[...], preferred_element_type=F32)
    sc1 = _dot_nt(keys_ref[0, 0], q[:, 0:PEER_DK_HALF])
    sc2 = _dot_nt(keys_ref[0, 1], q[:, PEER_DK_HALF:PEER_DK])
    for c0 in range(0, tb, LANES):
        cols = slice(c0, c0 + LANES)
        v1, i1 = _top_k_rows(sc1[:, cols], PEER_TOPK)
        v2, i2 = _top_k_rows(sc2[:, cols], PEER_TOPK)
        comb, cidx = _pair_candidates(v1, i1, v2, i2)
        top, eid = _top_k_rows(comb, PEER_TOPK, cidx)
        ex = jnp.exp(top - top[0:1])
        gate_ref[0, :, cols] = ex / jnp.sum(ex, axis=0, keepdims=True)
        upper = eid >= HALF_EXPERTS
        tile_ref[0, :, cols] = jnp.where(upper, eid - HALF_EXPERTS, eid) * SUBLANES
        shift_ref[0, :, cols] = jnp.where(upper, 0, 16)


def _select(x, g, wq, keys, tb):
    t = x.shape[0]
    xblk = pl.BlockSpec((tb, D_MODEL), lambda i, j: (i, 0))
    oblk = pl.BlockSpec((1, PEER_TOPK, tb), lambda i, j: (j, 0, i))
    return pl.pallas_call(
        functools.partial(_select_kernel, tb),
        out_shape=(jax.ShapeDtypeStruct((t, D_MODEL), F32),
                   jax.ShapeDtypeStruct((PEER_HEADS, PEER_TOPK, t), jnp.int32),
                   jax.ShapeDtypeStruct((PEER_HEADS, PEER_TOPK, t), jnp.int32),
                   jax.ShapeDtypeStruct((PEER_HEADS, PEER_TOPK, t), F32)),
        grid=(t // tb, PEER_HEADS),
        in_specs=[xblk, _const_spec(g.shape),
                  pl.BlockSpec((D_MODEL, PEER_DK), lambda i, j: (0, j)),
                  pl.BlockSpec((1, 2, N_KEYS, PEER_DK_HALF), lambda i, j: (j, 0, 0, 0))],
        out_specs=(xblk, oblk, oblk, oblk),
        compiler_params=pltpu.CompilerParams(dimension_semantics=("parallel", "arbitrary"),
                                             vmem_limit_bytes=VMEM_LIMIT),
        name="peer_select",
    )(x, g, wq, keys)


HI_HALF_MASK = -(1 << 16)
_BITREV3 = (0, 4, 2, 6, 1, 5, 3, 7)


def _unpack(word, shift):
    return lax.bitcast_convert_type((word << shift) & HI_HALF_MASK, F32)


def _splat_picks(blk, one_pass):
    tb = blk.shape[0]
    r = lax.broadcasted_iota(jnp.int32, (PICKS, LANES), 0)
    c = lax.broadcasted_iota(jnp.int32, (PICKS, LANES), 1)
    eye = (r == c)[None]
    ones = jnp.ones((LANES, LANES), BF16)

    def one(piece):
        diag = jnp.where(eye, piece[:, None, :], 0.0).astype(BF16).reshape(tb * PICKS, LANES)
        return jnp.dot(diag, ones, preferred_element_type=F32)

    if one_pass:
        out = one(blk)
    else:
        hi = blk.astype(BF16).astype(F32)
        rest = blk - hi
        mid = rest.astype(BF16).astype(F32)
        out = one(hi) + one(mid) + one(rest - mid)
    return out.reshape(tb, PICKS, LANES)


def _full(ref, k):
    return jnp.broadcast_to(ref[k:k + 1, :], (SUBLANES, LANES))


def _expert_row(tab_ref, tile_ref, t, k, shifts):
    idx = pl.multiple_of(tile_ref[t, k], SUBLANES)
    return _unpack(tab_ref[pl.ds(idx, SUBLANES), :], _full(shifts, k))


def _hidden_kernel(tb, tile_ref, shift_ref, x_ref, gate_ref, tab_ref, gh_ref, bsh, stage, parts):
    bsh[...] = _splat_picks(shift_ref[...].astype(F32), True).astype(jnp.int32)

    half = PICKS // 2
    quad = SUBLANES // 2
    low = lax.broadcasted_iota(jnp.int32, (SUBLANES, LANES), 0) < quad

    def token(t, carry):
        x = x_ref[t]
        shifts = bsh.at[t]
        for j in range(half):
            a = x * _expert_row(tab_ref, tile_ref, t, j, shifts)
            b = x * _expert_row(tab_ref, tile_ref, t, j + half, shifts)
            stage[j * SUBLANES:(j + 1) * SUBLANES, :] = jnp.where(low, a + pltpu.roll(a, quad, 0),
                                                                  b + pltpu.roll(b, quad, 0))
        for p in range(2):
            part = stage[pl.ds(p * quad, half, stride=SUBLANES), :]
            for s in range(1, quad):
                part = part + stage[pl.ds(p * quad + s, half, stride=SUBLANES), :]
            parts[t, p * half:(p + 1) * half, :] = part
        return carry

    lax.fori_loop(0, tb, token, 0)
    gh_ref[...] = jnp.sum(parts[...], axis=-1)
    hid = gh_ref[...]
    gelu = 0.5 * hid * (1.0 + lax.erf(hid * (2.0 ** -0.5)))
    gh_ref[...] = gate_ref[...] * gelu


def _pick_block(tb, space=None):
    return pl.BlockSpec((tb, PICKS), lambda i: (i, 0), memory_space=space)


def _table_spec(tab):
    return pl.BlockSpec(tab.shape, lambda i: (0, 0), pipeline_mode=pl.Buffered(1))


def _hidden(tile, shift, xg, gate, tab, tb):
    t = xg.shape[0]
    return pl.pallas_call(
        functools.partial(_hidden_kernel, tb),
        out_shape=jax.ShapeDtypeStruct((t, PICKS), F32),
        grid=(t // tb,),
        in_specs=[_pick_block(tb, pltpu.SMEM), _pick_block(tb),
                  pl.BlockSpec((tb, SUBLANES, LANES), lambda i: (i, 0, 0)),
                  _pick_block(tb), _table_spec(tab)],
        out_specs=_pick_block(tb),
        scratch_shapes=[pltpu.VMEM((tb, PICKS, LANES), jnp.int32), pltpu.VMEM((PICKS // 2 * SUBLANES, LANES), F32),
                        pltpu.VMEM((tb, PICKS, LANES), F32)],
        compiler_params=pltpu.CompilerParams(dimension_semantics=("parallel",), vmem_limit_bytes=VMEM_LIMIT),
        name="peer_hidden",
    )(tile, shift, xg, gate, tab)


def _combine_kernel(tb, tile_ref, shift_ref, gh_ref, tab_ref, o_ref, bsh, bg):
    bsh[...] = _splat_picks(shift_ref[...].astype(F32), True).astype(jnp.int32)
    bg[...] = _splat_picks(gh_ref[...], False)

    def token(t, carry):
        shifts, gates = bsh.at[t], bg.at[t]
        acc = [jnp.zeros((SUBLANES, LANES), F32) for _ in range(4)]
        for k in range(PICKS):
            acc[k % 4] = acc[k % 4] + _full(gates, k) * _expert_row(tab_ref, tile_ref, t, k, shifts)
        o_ref[t] = (acc[0] + acc[1]) + (acc[2] + acc[3])
        return carry

    lax.fori_loop(0, tb, token, 0)


def _combine(tile, shift, gh, tab, tb):
    t = gh.shape[0]
    return pl.pallas_call(
        functools.partial(_combine_kernel, tb),
        out_shape=jax.ShapeDtypeStruct((t, SUBLANES, LANES), F32),
        grid=(t // tb,),
        in_specs=[_pick_block(tb, pltpu.SMEM), _pick_block(tb), _pick_block(tb), _table_spec(tab)],
        out_specs=pl.BlockSpec((tb, SUBLANES, LANES), lambda i: (i, 0, 0)),
        scratch_shapes=[pltpu.VMEM((tb, PICKS, LANES), jnp.int32), pltpu.VMEM((tb, PICKS, LANES), F32)],
        compiler_params=pltpu.CompilerParams(dimension_semantics=("parallel",), vmem_limit_bytes=VMEM_LIMIT),
        name="peer_combine",
    )(tile, shift, gh, tab)


def _final_kernel(x_ref, p_ref, g_ref, o_ref):
    o_ref[...] = _rms(x_ref[...] + p_ref[...], g_ref[...])


def _final(x, p, g, tb):
    t = x.shape[0]
    blk = pl.BlockSpec((tb, D_MODEL), lambda i: (i, 0))
    return pl.pallas_call(
        _final_kernel, out_shape=jax.ShapeDtypeStruct((t, D_MODEL), F32), grid=(t // tb,),
        in_specs=[blk, blk, _const_spec(g.shape)], out_specs=blk,
        compiler_params=pltpu.CompilerParams(dimension_semantics=("parallel",), vmem_limit_bytes=VMEM_LIMIT),
        name="final_norm",
    )(x, p, g)


def _pack_table(tab):
    bits = lax.bitcast_convert_type(tab.astype(BF16), jnp.uint16).astype(jnp.uint32)
    packed = (bits[HALF_EXPERTS:] << 16) | bits[:HALF_EXPERTS]
    return lax.bitcast_convert_type(packed, jnp.int32).reshape(HALF_EXPERTS * SUBLANES, LANES)


def _col(v, n=LANES):
    return jnp.pad(v.astype(F32), (0, n - v.shape[0])).reshape(1, n)


def _row(v, n=2 * SUBLANES):
    return jnp.pad(v.astype(F32), (0, n - v.shape[0])).reshape(n, 1)


def _peer(x, w, tb_sel, tb_exp):
    t = x.shape[0]
    xn, tile, shift, gate = _select(x, w["ln_ffn"], w["w_pq"], w["sub_keys"], tb_sel)
    to_tok = lambda a: a.reshape(PICKS, t).T
    tile_t, shift_t = to_tok(tile), to_tok(shift)
    gh = _hidden(tile_t, shift_t, xn.reshape(t, SUBLANES, LANES), to_tok(gate), w["u_pack"], tb_exp)
    out = _combine(tile_t, shift_t, gh, w["v_pack"], tb_exp)
    return _final(x, out.reshape(t, D_MODEL), w["final_norm"], tb_sel)


def kernel(x_prompt, x_sample, mem_prompt, cache_conv_a, cache_conv_ssd, state_ssd, cache_mem_k, cache_mem_v, ln_mix_w, w_in, conv_a_w, conv_s_w, conv_s_b, dt_bias, a_log, d_skip, ssd_norm_w, w_out, ln_mem_w, mem_norm_w, w_mq, w_mk, w_mv, w_mo, ln_ffn_w, w_pq, sub_keys, u_tab, v_tab, final_norm_w):
    depth = w_in.shape[0]
    assert depth == 1
    l = 0
    n_main = 3 * D_CONV + D_SSD + XBC_DIM
    w_dt = w_in[l][:, n_main:]
    w = {
        "ln_mix": ln_mix_w[l].reshape(1, D_MODEL),
        "w_in": w_in[l][:, :n_main].astype(BF16),
        "w_dt": jnp.pad(w_dt, ((0, 0), (0, LANES - SSD_HEADS))).astype(BF16),
        "w_dtt": jnp.pad(w_dt.T, ((0, 2 * SUBLANES - SSD_HEADS), (0, 0))).astype(BF16),
        "conv_a_w": conv_a_w[l], "conv_s_w": conv_s_w[l], "conv_s_b": conv_s_b[l].reshape(1, XBC_DIM),
        "dtb_c": _col(dt_bias[l]), "alog_c": _col(a_log[l]), "dsk_c": _col(d_skip[l]),
        "dtb_r": _row(dt_bias[l]), "alog_r": _row(a_log[l]),
        "ssd_norm": ssd_norm_w[l].reshape(1, D_SSD),
        "w_out": w_out[l].astype(BF16),
        "ln_ffn": ln_ffn_w[l].reshape(1, D_MODEL),
        "w_pq": w_pq[l].astype(BF16),
        "sub_keys": sub_keys[l].astype(BF16),
        "u_pack": _pack_table(u_tab[l]),
        "v_pack": _pack_table(v_tab[l]),
        "final_norm": final_norm_w.reshape(1, D_MODEL),
    }
    ln_mem = ln_mem_w[l].reshape(1, D_MODEL)
    wq, wo = w_mq[l].astype(BF16), w_mo[l].astype(BF16)

    bp, sp, _ = x_prompt.shape
    bs, ss, _ = x_sample.shape

    mk, mv = _memory_kv(mem_prompt.reshape(bp * N_MEM, D_MODEL), mem_norm_w[l].reshape(1, D_MODEL),
                        w_mk[l].astype(BF16), w_mv[l].astype(BF16))
    hp, pa, ps, ph = _mixer(x_prompt, None, w, 256)
    hp = _attn(hp, mk.reshape(bp, N_MEM, D_MODEL), mv.reshape(bp, N_MEM, D_MODEL), ln_mem, wq, wo, 256)
    y_prompt = _peer(hp.reshape(bp * sp, D_MODEL), w, 256, 64).reshape(bp, sp, D_MODEL)

    hs, sa, ssd, sh = _mixer(x_sample, (cache_conv_a[l], cache_conv_ssd[l],
                                        state_ssd[l].reshape((bs,) + PAIR_STATE)), w, 256)
    hs = _attn(hs, cache_mem_k[l].reshape(bs, N_MEM, D_MODEL), cache_mem_v[l].reshape(bs, N_MEM, D_MODEL),
               ln_mem, wq, wo, 256)
    y_sample = _peer(hs.reshape(bs * ss, D_MODEL), w, 256, 64).reshape(bs, ss, D_MODEL)

    kv_shape = (1, bp, N_MEM, MEM_HEADS, MEM_HEAD_DIM)
    st_shape = (SSD_HEADS, SSD_HEAD_DIM, SSD_STATE)
    return (y_prompt, y_sample, pa[None], ps[None], ph.reshape((1, bp) + st_shape), mk.reshape(kv_shape),
            mv.reshape(kv_shape), sa[None], ssd[None], sh.reshape((1, bs) + st_shape))
```

```python
import functools
import math

import jax
import jax.numpy as jnp
from jax import lax
from jax.experimental import pallas as pl
from jax.experimental.pallas import tpu as pltpu
from jax.experimental.pallas import tpu_sc as plsc

D_MODEL = 1024
CHUNK = 64
D_CONV = 512
CONV_A_W = 3
D_SSD = 512
SSD_HEAD_DIM = 64
SSD_HEADS = 8
SSD_GROUPS = 2
SSD_HPG = 4
SSD_STATE = 128
SSD_CONV_W = 4
XBC_DIM = 1024
N_MEM = 256
MEM_HEADS = 4
MEM_HEAD_DIM = 256
PEER_HEADS = 8
N_KEYS = 128
N_EXPERTS = N_KEYS * N_KEYS
PEER_TOPK = 16
PEER_DK = 256
PEER_DK_HALF = 128
PICKS = PEER_HEADS * PEER_TOPK
EPS = 1e-6

LANES = 128
SUBLANES = 8
HALF_EXPERTS = N_EXPERTS // 2
PAIR_STATE = (SSD_HEADS // 2, 2 * SSD_HEAD_DIM, SSD_STATE)
VMEM_LIMIT = 56 * 1024 * 1024

F32 = jnp.float32
BF16 = jnp.bfloat16
HI = lax.Precision.HIGHEST


def _rms(x, g):
    return x * lax.rsqrt(jnp.mean(x * x, axis=-1, keepdims=True) + EPS) * g


def _silu(x):
    return x * (1.0 / (1.0 + jnp.exp(-x)))


def _softplus(x):
    return jnp.maximum(x, 0.0) + jnp.log1p(jnp.exp(-jnp.abs(x)))


def _dot(a, b):
    return jnp.dot(a.astype(BF16), b.astype(BF16), preferred_element_type=F32)


def _dot_nt(a, b):
    return lax.dot_general(a.astype(BF16), b.astype(BF16), (((1,), (1,)), ((), ())),
                           preferred_element_type=F32)


def _dot_tn(a, b):
    return lax.dot_general(a.astype(BF16), b.astype(BF16), (((0,), (0,)), ((), ())),
                           preferred_element_type=F32)


def _const_spec(shape):
    n = len(shape)
    return pl.BlockSpec(shape, lambda *_: (0,) * n)


def _mixer_kernel(has_hist, ts, lc, *refs):
    if has_hist:
        x_ref, hista_ref, hists_ref, h0_ref = refs[:4]
        refs = refs[4:]
    else:
        x_ref = refs[0]
        refs = refs[1:]
    (lnw_ref, win_ref, wdt_ref, wdtt_ref, caw_ref, csw_ref, csb_ref, dtb_c_ref, alog_c_ref, dsk_c_ref,
     dtb_r_ref, alog_r_ref, nw_ref, wout_ref,
     h1_ref, na_ref, ns_ref, nh_ref, cata, cats, hst) = refs
    s = pl.program_id(1)
    pad = SUBLANES

    @pl.when(s == 0)
    def _():
        if has_hist:
            cata[pad - 2:pad, :] = hista_ref[...]
            cats[pad - 3:pad, :] = hists_ref[...]
            hst[...] = h0_ref[...]
        else:
            cata[0:pad, :] = jnp.zeros((pad, D_CONV), F32)
            cats[0:pad, :] = jnp.zeros((pad, XBC_DIM), F32)
            hst[...] = jnp.zeros(hst.shape, F32)

    x = x_ref[...]
    xn = _rms(x, lnw_ref[...]).astype(BF16)
    proj = jnp.dot(xn, win_ref[...], preferred_element_type=F32)
    dt_c = jnp.dot(xn, wdt_ref[...], preferred_element_type=F32)
    dt_r = lax.dot_general(wdtt_ref[...], xn, (((1,), (1,)), ((), ())),
                           preferred_element_type=F32)
    g_b = proj[:, 0:D_CONV]
    g_c = proj[:, D_CONV:2 * D_CONV]
    v_in = proj[:, 2 * D_CONV:3 * D_CONV]
    z = proj[:, 3 * D_CONV:3 * D_CONV + D_SSD]
    xbc = proj[:, 3 * D_CONV + D_SSD:3 * D_CONV + D_SSD + XBC_DIM]

    ua = g_c * v_in
    cata[pad:pad + ts, :] = ua
    caw = caw_ref[...]
    conv_a = (caw[0:1] * cata[pad - 2:pad - 2 + ts, :] + caw[1:2] * cata[pad - 1:pad - 1 + ts, :]
              + caw[2:3] * ua)
    y_a = g_b * conv_a
    hist_a = cata[pad + ts - 2:pad + ts, :]
    na_ref[...] = hist_a
    cata[pad - 2:pad, :] = hist_a

    cats[pad:pad + ts, :] = xbc
    csw = csw_ref[...]
    conv_s = (csw[0:1] * cats[pad - 3:pad - 3 + ts, :] + csw[1:2] * cats[pad - 2:pad - 2 + ts, :]
              + csw[2:3] * cats[pad - 1:pad - 1 + ts, :] + csw[3:4] * xbc)
    hist_s = cats[pad + ts - 3:pad + ts, :]
    ns_ref[...] = hist_s
    cats[pad - 3:pad, :] = hist_s
    xbc_c = _silu(conv_s + csb_ref[...])
    xs = xbc_c[:, 0:D_SSD]

    dtp_c = _softplus(dt_c + dtb_c_ref[...])
    dtp_r = _softplus(dt_r + dtb_r_ref[...])
    a_c = dtp_c * (-jnp.exp(alog_c_ref[...]))
    a_r = dtp_r * (-jnp.exp(alog_r_ref[...]))
    dsk_c = dsk_c_ref[...]

    lane = lax.broadcasted_iota(jnp.int32, (1, LANES), 1)
    first_head = lane < SSD_HEAD_DIM
    row2 = lax.broadcasted_iota(jnp.int32, (2 * SSD_HEAD_DIM, 1), 0) < SSD_HEAD_DIM
    li = lax.broadcasted_iota(jnp.int32, (lc, lc), 0)
    si = lax.broadcasted_iota(jnp.int32, (lc, lc), 1)
    causal = si <= li
    tril = causal.astype(F32)
    jrow = lax.broadcasted_iota(jnp.int32, (ts, lc), 0)
    scol = lax.broadcasted_iota(jnp.int32, (ts, lc), 1)

    def pair(col, h):
        return jnp.where(first_head, col[:, h:h + 1], col[:, h + 1:h + 2])

    y_chunks = []
    for c in range(ts // lc):
        r0 = c * lc
        rows = slice(r0, r0 + lc)
        acum_c = jnp.dot(tril, a_c[rows], preferred_element_type=F32, precision=HI)
        upper = ((jrow >= r0) & (jrow <= r0 + scol)).astype(F32)
        acum_r = jnp.dot(a_r, upper, preferred_element_type=F32, precision=HI)
        last_c = acum_c[lc - 1:lc, :]
        dec_c = jnp.exp(last_c - acum_c)
        eac_c = jnp.exp(acum_c)
        y_pairs = []
        for g in range(SSD_GROUPS):
            b_g = xbc_c[rows, D_SSD + g * SSD_STATE:D_SSD + (g + 1) * SSD_STATE]
            c_g = xbc_c[rows, D_SSD + (SSD_GROUPS + g) * SSD_STATE:D_SSD + (SSD_GROUPS + g + 1) * SSD_STATE]
            cb = _dot_nt(c_g, b_g)
            for q in range(SSD_HPG // 2):
                h = g * SSD_HPG + 2 * q
                pi = h // 2
                xs_p = xs[rows, pi * LANES:(pi + 1) * LANES]
                xdt = xs_p * pair(dtp_c[rows], h)
                res = []
                for hh in (h, h + 1):
                    seg = acum_c[:, hh:hh + 1] - acum_r[hh:hh + 1, :]
                    m_h = cb * jnp.exp(jnp.where(causal, seg, -jnp.inf))
                    res.append(_dot(m_h, xdt))
                y_diag = jnp.where(first_head, res[0], res[1])
                h_in = hst[pi]
                y_off = _dot_nt(c_g, h_in) * pair(eac_c, h)
                st = _dot_tn(xdt * pair(dec_c, h), b_g)
                cd = jnp.exp(jnp.where(row2, last_c[:, h:h + 1], last_c[:, h + 1:h + 2]))
                hst[pi] = cd * h_in + st
                y_pairs.append(y_diag + y_off + pair(dsk_c, h) * xs_p)
        y_chunks.append(jnp.concatenate(y_pairs, axis=1))
    y = y_chunks[0] if len(y_chunks) == 1 else jnp.concatenate(y_chunks, axis=0)
    y_b = _rms(y * _silu(z), nw_ref[...])
    wout = wout_ref[...]
    out = (jnp.dot(y_a.astype(BF16), wout[0:D_CONV], preferred_element_type=F32)
           + jnp.dot(y_b.astype(BF16), wout[D_CONV:], preferred_element_type=F32))
    h1_ref[...] = x + out
    nh_ref[...] = hst[...]


def _mixer(x, hist, w, ts):
    b, s, _ = x.shape
    lc = min(CHUNK, s)
    ts = min(ts, s)
    has_hist = hist is not None
    per_b3 = lambda shape: pl.BlockSpec((None,) + shape, lambda i, j: (i,) + (0,) * len(shape))
    in_specs = [pl.BlockSpec((None, ts, D_MODEL), lambda i, j: (i, j, 0))]
    args = [x]
    if has_hist:
        in_specs += [per_b3((CONV_A_W - 1, D_CONV)), per_b3((SSD_CONV_W - 1, XBC_DIM)),
                     per_b3(PAIR_STATE)]
        args += list(hist)
    wargs = [w["ln_mix"], w["w_in"], w["w_dt"], w["w_dtt"], w["conv_a_w"], w["conv_s_w"], w["conv_s_b"],
             w["dtb_c"], w["alog_c"], w["dsk_c"], w["dtb_r"], w["alog_r"], w["ssd_norm"], w["w_out"]]
    in_specs += [_const_spec(a.shape) for a in wargs]
    args += wargs
    out_shape = (jax.ShapeDtypeStruct((b, s, D_MODEL), F32),
                 jax.ShapeDtypeStruct((b, CONV_A_W - 1, D_CONV), F32),
                 jax.ShapeDtypeStruct((b, SSD_CONV_W - 1, XBC_DIM), F32),
                 jax.ShapeDtypeStruct((b,) + PAIR_STATE, F32))
    out_specs = (pl.BlockSpec((None, ts, D_MODEL), lambda i, j: (i, j, 0)),
                 per_b3((CONV_A_W - 1, D_CONV)), per_b3((SSD_CONV_W - 1, XBC_DIM)),
                 per_b3(PAIR_STATE))
    return pl.pallas_call(
        functools.partial(_mixer_kernel, has_hist, ts, lc),
        out_shape=out_shape, grid=(b, s // ts), in_specs=in_specs, out_specs=out_specs,
        scratch_shapes=[pltpu.VMEM((SUBLANES + ts, D_CONV), F32), pltpu.VMEM((SUBLANES + ts, XBC_DIM), F32),
                        pltpu.VMEM(PAIR_STATE, F32)],
        compiler_params=pltpu.CompilerParams(dimension_semantics=("parallel", "arbitrary"),
                                             vmem_limit_bytes=VMEM_LIMIT),
        name="mixer_hist" if has_hist else "mixer",
    )(*args)


def _memkv_kernel(m_ref, g_ref, wk_ref, wv_ref, k_ref, v_ref):
    mn = _rms(m_ref[...], g_ref[...]).astype(BF16)
    k_ref[...] = jnp.dot(mn, wk_ref[...], preferred_element_type=F32)
    v_ref[...] = jnp.dot(mn, wv_ref[...], preferred_element_type=F32)


def _memory_kv(mem, g, wk, wv, tm=256):
    t = mem.shape[0]
    blk = pl.BlockSpec((tm, D_MODEL), lambda i: (i, 0))
    return pl.pallas_call(
        _memkv_kernel,
        out_shape=(jax.ShapeDtypeStruct((t, D_MODEL), F32),) * 2,
        grid=(t // tm,), in_specs=[blk, _const_spec(g.shape), _const_spec(wk.shape), _const_spec(wv.shape)],
        out_specs=(blk, blk),
        compiler_params=pltpu.CompilerParams(dimension_semantics=("parallel",), vmem_limit_bytes=VMEM_LIMIT),
        name="memory_kv",
    )(mem, g, wk, wv)


def _attn_kernel(x_ref, k_ref, v_ref, g_ref, wq_ref, wo_ref, o_ref):
    x = x_ref[...]
    xn = _rms(x, g_ref[...]).astype(BF16)
    q = jnp.dot(xn, wq_ref[...], preferred_element_type=F32)
    k = k_ref[...]
    v = v_ref[...]
    wo = wo_ref[...]
    acc = x
    for h in range(MEM_HEADS):
        cols = slice(h * MEM_HEAD_DIM, (h + 1) * MEM_HEAD_DIM)
        sc = _dot_nt(q[:, cols], k[:, cols]) * (MEM_HEAD_DIM ** -0.5)
        sc = sc - jnp.max(sc, axis=-1, keepdims=True)
        p = jnp.exp(sc)
        p = p / jnp.sum(p, axis=-1, keepdims=True)
        o_h = _dot(p, v[:, cols])
        acc = acc + jnp.dot(o_h.astype(BF16), wo[cols, :], preferred_element_type=F32)
    o_ref[...] = acc


def _attn(x, k, v, g, wq, wo, ts):
    b, s, _ = x.shape
    ts = min(ts, s)
    xblk = pl.BlockSpec((None, ts, D_MODEL), lambda i, j: (i, j, 0))
    kvblk = pl.BlockSpec((None, N_MEM, D_MODEL), lambda i, j: (i, 0, 0))
    return pl.pallas_call(
        _attn_kernel,
        out_shape=jax.ShapeDtypeStruct((b, s, D_MODEL), F32),
        grid=(b, s // ts),
        in_specs=[xblk, kvblk, kvblk, _const_spec(g.shape), _const_spec(wq.shape), _const_spec(wo.shape)],
        out_specs=xblk,
        compiler_params=pltpu.CompilerParams(dimension_semantics=("parallel", "parallel"),
                                             vmem_limit_bytes=VMEM_LIMIT),
        name="attn",
    )(x, k, v, g, wq, wo)


def _top_k_rows(sc, k, payload=None):
    n = sc.shape[0]
    rows = lax.broadcasted_iota(jnp.int32, sc.shape, 0).astype(F32)
    vals, outs = [], []
    for _ in range(k):
        m = jnp.max(sc, axis=0, keepdims=True)
        cand = jnp.where(sc == m, rows, float(n))
        first = jnp.min(cand, axis=0, keepdims=True)
        hit = cand == first
        sc = jnp.where(hit, -jnp.inf, sc)
        vals.append(m)
        if payload is None:
            outs.append(first.astype(jnp.int32))
        else:
            outs.append(jnp.sum(jnp.where(hit, payload, 0), axis=0, keepdims=True))
    return jnp.concatenate(vals, axis=0), jnp.concatenate(outs, axis=0)


_LIVE_B = tuple(PEER_TOPK // (a + 1) for a in range(1, SUBLANES))


def _pair_candidates(v1, i1, v2, i2):
    sub = lax.broadcasted_iota(jnp.int32, (SUBLANES, v1.shape[1]), 0)
    comb = [v1[0:1] + v2]
    cidx = [i1[0:1] * N_KEYS + i2]
    for a in range(1, SUBLANES):
        comb.append(jnp.where(sub < _LIVE_B[a - 1], v1[a:a + 1] + v2[0:SUBLANES], -jnp.inf))
        cidx.append(i1[a:a + 1] * N_KEYS + i2[0:SUBLANES])
    comb.append(v1[SUBLANES:] + v2[0:1])
    cidx.append(i1[SUBLANES:] * N_KEYS + i2[0:1])
    return jnp.concatenate(comb, axis=0), jnp.concatenate(cidx, axis=0)


def _select_kernel(tb, x_ref, g_ref, wq_ref, keys_ref, xn_ref, tile_ref, shift_ref, gate_ref):
    @pl.when(pl.program_id(1) == 0)
    def _():
        xn_ref[...] = _rms(x_ref[...], g_ref[...])

    q = jnp.dot(xn_ref[...].astype(BF16), wq_ref[...], preferred_element_type=F32)
    sc1 = _dot_nt(keys_ref[0, 0], q[:, 0:PEER_DK_HALF])
    sc2 = _dot_nt(keys_ref[0, 1], q[:, PEER_DK_HALF:PEER_DK])
    for c0 in range(0, tb, LANES):
        cols = slice(c0, c0 + LANES)
        v1, i1 = _top_k_rows(sc1[:, cols], PEER_TOPK)
        v2, i2 = _top_k_rows(sc2[:, cols], PEER_TOPK)
        comb, cidx = _pair_candidates(v1, i1, v2, i2)
        top, eid = _top_k_rows(comb, PEER_TOPK, cidx)
        ex = jnp.exp(top - top[0:1])
        gate_ref[0, :, cols] = ex / jnp.sum(ex, axis=0, keepdims=True)
        upper = eid >= HALF_EXPERTS
        tile_ref[0, :, cols] = jnp.where(upper, eid - HALF_EXPERTS, eid) * SUBLANES
        shift_ref[0, :, cols] = jnp.where(upper, 0, 16)


def _select(x, g, wq, keys, tb, t0, t):
    b0 = t0 // tb
    xin = pl.BlockSpec((tb, D_MODEL), lambda i, j: (i + b0, 0))
    xblk = pl.BlockSpec((tb, D_MODEL), lambda i, j: (i, 0))
    oblk = pl.BlockSpec((1, PEER_TOPK, tb), lambda i, j: (j, 0, i))
    return pl.pallas_call(
        functools.partial(_select_kernel, tb),
        out_shape=(jax.ShapeDtypeStruct((t, D_MODEL), F32),
                   jax.ShapeDtypeStruct((PEER_HEADS, PEER_TOPK, t), jnp.int32),
                   jax.ShapeDtypeStruct((PEER_HEADS, PEER_TOPK, t), jnp.int32),
                   jax.ShapeDtypeStruct((PEER_HEADS, PEER_TOPK, t), F32)),
        grid=(t // tb, PEER_HEADS),
        in_specs=[xin, _const_spec(g.shape),
                  pl.BlockSpec((D_MODEL, PEER_DK), lambda i, j: (0, j)),
                  pl.BlockSpec((1, 2, N_KEYS, PEER_DK_HALF), lambda i, j: (j, 0, 0, 0))],
        out_specs=(xblk, oblk, oblk, oblk),
        compiler_params=pltpu.CompilerParams(dimension_semantics=("parallel", "arbitrary"),
                                             vmem_limit_bytes=VMEM_LIMIT),
        name="peer_select",
    )(x, g, wq, keys)


HI_HALF_MASK = -(1 << 16)
_BITREV3 = (0, 4, 2, 6, 1, 5, 3, 7)


def _unpack(word, shift):
    return lax.bitcast_convert_type((word << shift) & HI_HALF_MASK, F32)


def _splat_picks(blk, one_pass):
    tb = blk.shape[0]
    r = lax.broadcasted_iota(jnp.int32, (PICKS, LANES), 0)
    c = lax.broadcasted_iota(jnp.int32, (PICKS, LANES), 1)
    eye = (r == c)[None]
    ones = jnp.ones((LANES, LANES), BF16)

    def one(piece):
        diag = jnp.where(eye, piece[:, None, :], 0.0).astype(BF16).reshape(tb * PICKS, LANES)
        return jnp.dot(diag, ones, preferred_element_type=F32)

    if one_pass:
        out = one(blk)
    else:
        hi = blk.astype(BF16).astype(F32)
        rest = blk - hi
        mid = rest.astype(BF16).astype(F32)
        out = one(hi) + one(mid) + one(rest - mid)
    return out.reshape(tb, PICKS, LANES)


def _full(ref, k):
    return jnp.broadcast_to(ref[k:k + 1, :], (SUBLANES, LANES))


def _expert_row(tab_ref, tile_ref, t, k, shifts):
    idx = pl.multiple_of(tile_ref[t, k], SUBLANES)
    return _unpack(tab_ref[pl.ds(idx, SUBLANES), :], _full(shifts, k))


def _hidden_kernel(tb, tile_ref, shift_ref, x_ref, gate_ref, tab_ref, gh_ref, bsh, stage, parts):
    bsh[...] = _splat_picks(shift_ref[...].astype(F32), True).astype(jnp.int32)

    half = PICKS // 2
    quad = SUBLANES // 2
    low = lax.broadcasted_iota(jnp.int32, (SUBLANES, LANES), 0) < quad

    def token(t, carry):
        x = x_ref[t]
        shifts = bsh.at[t]
        for j in range(half):
            a = x * _expert_row(tab_ref, tile_ref, t, j, shifts)
            b = x * _expert_row(tab_ref, tile_ref, t, j + half, shifts)
            stage[j * SUBLANES:(j + 1) * SUBLANES, :] = jnp.where(low, a + pltpu.roll(a, quad, 0),
                                                                  b + pltpu.roll(b, quad, 0))
        for p in range(2):
            part = stage[pl.ds(p * quad, half, stride=SUBLANES), :]
            for s in range(1, quad):
                part = part + stage[pl.ds(p * quad + s, half, stride=SUBLANES), :]
            parts[t, p * half:(p + 1) * half, :] = part
        return carry

    lax.fori_loop(0, tb, token, 0)
    gh_ref[...] = jnp.sum(parts[...], axis=-1)
    hid = gh_ref[...]
    gelu = 0.5 * hid * (1.0 + lax.erf(hid * (2.0 ** -0.5)))
    gh_ref[...] = gate_ref[...] * gelu


def _pick_block(tb, space=None):
    return pl.BlockSpec((tb, PICKS), lambda i: (i, 0), memory_space=space)


def _table_spec(tab):
    return pl.BlockSpec(tab.shape, lambda i: (0, 0), pipeline_mode=pl.Buffered(1))


def _hidden(tile, shift, xg, gate, tab, tb):
    t = xg.shape[0]
    return pl.pallas_call(
        functools.partial(_hidden_kernel, tb),
        out_shape=jax.ShapeDtypeStruct((t, PICKS), F32),
        grid=(t // tb,),
        in_specs=[_pick_block(tb, pltpu.SMEM), _pick_block(tb),
                  pl.BlockSpec((tb, SUBLANES, LANES), lambda i: (i, 0, 0)),
                  _pick_block(tb), _table_spec(tab)],
        out_specs=_pick_block(tb),
        scratch_shapes=[pltpu.VMEM((tb, PICKS, LANES), jnp.int32), pltpu.VMEM((PICKS // 2 * SUBLANES, LANES), F32),
                        pltpu.VMEM((tb, PICKS, LANES), F32)],
        compiler_params=pltpu.CompilerParams(dimension_semantics=("parallel",), vmem_limit_bytes=VMEM_LIMIT),
        name="peer_hidden",
    )(tile, shift, xg, gate, tab)


def _combine_kernel(tb, tile_ref, shift_ref, gh_ref, tab_ref, o_ref, bsh, bg):
    bsh[...] = _splat_picks(shift_ref[...].astype(F32), True).astype(jnp.int32)
    bg[...] = _splat_picks(gh_ref[...], False)

    def token(t, carry):
        shifts, gates = bsh.at[t], bg.at[t]
        acc = [jnp.zeros((SUBLANES, LANES), F32) for _ in range(4)]
        for k in range(PICKS):
            acc[k % 4] = acc[k % 4] + _full(gates, k) * _expert_row(tab_ref, tile_ref, t, k, shifts)
        o_ref[t] = (acc[0] + acc[1]) + (acc[2] + acc[3])
        return carry

    lax.fori_loop(0, tb, token, 0)


def _combine(tile, shift, gh, tab, tb):
    t = gh.shape[0]
    return pl.pallas_call(
        functools.partial(_combine_kernel, tb),
        out_shape=jax.ShapeDtypeStruct((t, SUBLANES, LANES), F32),
        grid=(t // tb,),
        in_specs=[_pick_block(tb, pltpu.SMEM), _pick_block(tb), _pick_block(tb), _table_spec(tab)],
        out_specs=pl.BlockSpec((tb, SUBLANES, LANES), lambda i: (i, 0, 0)),
        scratch_shapes=[pltpu.VMEM((tb, PICKS, LANES), jnp.int32), pltpu.VMEM((tb, PICKS, LANES), F32)],
        compiler_params=pltpu.CompilerParams(dimension_semantics=("parallel",), vmem_limit_bytes=VMEM_LIMIT),
        name="peer_combine",
    )(tile, shift, gh, tab)


SC_LANES = 16
SC_WORKERS = 32
SC_CHUNK = 32
SC_FEATURE_BLOCK = 256
SC_PROMPT_TOKENS = 9472


def _sc_call(body, n, width, scratch):
    mesh = plsc.VectorSubcoreMesh(core_axis_name="c", subcore_axis_name="s")
    return pl.kernel(body, mesh=mesh, out_type=jax.ShapeDtypeStruct((n, width), F32),
                     compiler_params=pltpu.CompilerParams(needs_layout_passes=False),
                     scratch_types=[pltpu.VMEM((PICKS // SC_CHUNK, SC_CHUNK), jnp.int32)] + scratch
                     + [pltpu.VMEM((SC_CHUNK, D_MODEL), F32), pltpu.VMEM((SC_CHUNK, D_MODEL), F32),
                        pltpu.SemaphoreType.DMA, pltpu.SemaphoreType.DMA])


def _sc_token_loop(n, eid_hbm, side_hbm, tab_hbm, out_hbm, idx_v, side_v, out_v, bufs, sems, compute):
    per_worker = n // SC_WORKERS
    wid = lax.axis_index("s") * 2 + lax.axis_index("c")
    chunks = PICKS // SC_CHUNK

    def gather(c):
        return pltpu.make_async_copy(tab_hbm.at[idx_v.at[c]], bufs[c % 2], sems[c % 2])

    @pl.loop(0, per_worker)
    def _(i):
        t = wid * per_worker + i
        pltpu.sync_copy(eid_hbm.at[t], idx_v)
        pltpu.sync_copy(side_hbm.at[t], side_v)
        gather(0).start()
        gather(1).start()
        for c in range(chunks):
            gather(c).wait()
            compute(c, bufs[c % 2])
            if c + 2 < chunks:
                gather(c + 2).start()
        pltpu.sync_copy(out_v, out_hbm.at[t])


def _sc_hidden(eid, x, tab):
    n = eid.shape[0]
    lanes = SC_LANES

    def body(eid_hbm, x_hbm, tab_hbm, out_hbm, idx_v, x_v, hid_v, rows0, rows1, sem0, sem1):
        lane = lax.iota(jnp.int32, lanes)

        def compute(c, rows):
            for g in range(SC_CHUNK // lanes):
                def step(j, accs):
                    off = pl.multiple_of(j * lanes, lanes)
                    xj = x_v[pl.ds(off, lanes)]
                    return tuple(accs[r] + xj * rows[g * lanes + r, pl.ds(off, lanes)] for r in range(lanes))
                accs = lax.fori_loop(0, D_MODEL // lanes, step,
                                     tuple(jnp.zeros((lanes,), F32) for _ in range(lanes)))
                hv = jnp.zeros((lanes,), F32)
                for r in range(lanes):
                    hv = jnp.where(lane == r, jnp.sum(accs[r]), hv)
                hid_v[pl.ds(c * SC_CHUNK + g * lanes, lanes)] = hv

        _sc_token_loop(n, eid_hbm, x_hbm, tab_hbm, out_hbm, idx_v, x_v, hid_v, (rows0, rows1), (sem0, sem1), compute)

    call = _sc_call(body, n, PICKS, [pltpu.VMEM((D_MODEL,), F32), pltpu.VMEM((PICKS,), F32)])
    return call(eid.reshape(n, PICKS // SC_CHUNK, SC_CHUNK), x, tab)


def _sc_combine(eid, gh, tab):
    n = eid.shape[0]
    lanes = SC_LANES
    nacc = SC_FEATURE_BLOCK // lanes

    def body(eid_hbm, g_hbm, tab_hbm, out_hbm, idx_v, g_v, out_v, rows0, rows1, sem0, sem1):
        def compute(c, rows):
            for fb in range(D_MODEL // SC_FEATURE_BLOCK):
                f0 = fb * SC_FEATURE_BLOCK
                if c == 0:
                    init = tuple(jnp.zeros((lanes,), F32) for _ in range(nacc))
                else:
                    init = tuple(out_v[pl.ds(f0 + q * lanes, lanes)] for q in range(nacc))

                def step(r, accs):
                    gk = plsc.load_gather(g_v, [jnp.full((lanes,), c * SC_CHUNK + r, jnp.int32)])
                    return tuple(accs[q] + gk * rows[r, pl.ds(f0 + q * lanes, lanes)] for q in range(nacc))
                accs = lax.fori_loop(0, SC_CHUNK, step, init)
                for q in range(nacc):
                    out_v[pl.ds(f0 + q * lanes, lanes)] = accs[q]

        _sc_token_loop(n, eid_hbm, g_hbm, tab_hbm, out_hbm, idx_v, g_v, out_v, (rows0, rows1), (sem0, sem1), compute)

    call = _sc_call(body, n, D_MODEL, [pltpu.VMEM((PICKS,), F32), pltpu.VMEM((D_MODEL,), F32)])
    return call(eid.reshape(n, PICKS // SC_CHUNK, SC_CHUNK), gh, tab)


def _gate_gelu_kernel(h_ref, g_ref, o_ref):
    hid = h_ref[...]
    o_ref[...] = g_ref[...] * (0.5 * hid * (1.0 + lax.erf(hid * (2.0 ** -0.5))))


def _gate_gelu(hid, gate, tb):
    t = hid.shape[0]
    return pl.pallas_call(
        _gate_gelu_kernel, out_shape=jax.ShapeDtypeStruct((t, PICKS), F32), grid=(t // tb,),
        in_specs=[_pick_block(tb), _pick_block(tb)], out_specs=_pick_block(tb),
        compiler_params=pltpu.CompilerParams(dimension_semantics=("parallel",)),
        name="gate_gelu",
    )(hid, gate)


def _final_kernel(x_ref, p_ref, g_ref, o_ref):
    o_ref[...] = _rms(x_ref[...] + p_ref[...], g_ref[...])


def _final(x, p, g, tb, t0):
    t = p.shape[0]
    b0 = t0 // tb
    blk = pl.BlockSpec((tb, D_MODEL), lambda i: (i, 0))
    return pl.pallas_call(
        _final_kernel, out_shape=jax.ShapeDtypeStruct((t, D_MODEL), F32), grid=(t // tb,),
        in_specs=[pl.BlockSpec((tb, D_MODEL), lambda i: (i + b0, 0)), blk, _const_spec(g.shape)], out_specs=blk,
        compiler_params=pltpu.CompilerParams(dimension_semantics=("parallel",), vmem_limit_bytes=VMEM_LIMIT),
        name="final_norm",
    )(x, p, g)


def _pack_table(tab):
    bits = lax.bitcast_convert_type(tab.astype(BF16), jnp.uint16).astype(jnp.uint32)
    packed = (bits[HALF_EXPERTS:] << 16) | bits[:HALF_EXPERTS]
    return lax.bitcast_convert_type(packed, jnp.int32).reshape(HALF_EXPERTS * SUBLANES, LANES)


def _col(v, n=LANES):
    return jnp.pad(v.astype(F32), (0, n - v.shape[0])).reshape(1, n)


def _row(v, n=2 * SUBLANES):
    return jnp.pad(v.astype(F32), (0, n - v.shape[0])).reshape(n, 1)


def _peer_tc(x, w, tb_sel, tb_exp, t0, t):
    xn, tile, shift, gate = _select(x, w["ln_ffn"], w["w_pq"], w["sub_keys"], tb_sel, t0, t)
    to_tok = lambda a: a.reshape(PICKS, t).T
    tile_t, shift_t = to_tok(tile), to_tok(shift)
    gh = _hidden(tile_t, shift_t, xn.reshape(t, SUBLANES, LANES), to_tok(gate), w["u_pack"], tb_exp)
    out = _combine(tile_t, shift_t, gh, w["v_pack"], tb_exp)
    return _final(x, out.reshape(t, D_MODEL), w["final_norm"], tb_sel, t0)


def _peer_sc(x, w, tb_sel, t0, t):
    xn, tile, shift, gate = _select(x, w["ln_ffn"], w["w_pq"], w["sub_keys"], tb_sel, t0, t)
    to_tok = lambda a: a.reshape(PICKS, t).T
    eid = to_tok(tile) // SUBLANES + jnp.where(to_tok(shift) == 0, HALF_EXPERTS, 0)
    hid = _sc_hidden(eid, xn, w["u_tab"])
    gh = _gate_gelu(hid, to_tok(gate), tb_sel)
    out = _sc_combine(eid, gh, w["v_tab"])
    return _final(x, out, w["final_norm"], tb_sel, t0)


def _peer(x, w, tb_sel, tb_exp, n_sc):
    t = x.shape[0]
    if n_sc == 0:
        return _peer_tc(x, w, tb_sel, tb_exp, 0, t)
    y_sc = _peer_sc(x, w, tb_sel, 0, n_sc)
    y_tc = _peer_tc(x, w, tb_sel, tb_exp, n_sc, t - n_sc)
    return jnp.concatenate([y_sc, y_tc], axis=0)


def kernel(x_prompt, x_sample, mem_prompt, cache_conv_a, cache_conv_ssd, state_ssd, cache_mem_k, cache_mem_v, ln_mix_w, w_in, conv_a_w, conv_s_w, conv_s_b, dt_bias, a_log, d_skip, ssd_norm_w, w_out, ln_mem_w, mem_norm_w, w_mq, w_mk, w_mv, w_mo, ln_ffn_w, w_pq, sub_keys, u_tab, v_tab, final_norm_w):
    depth = w_in.shape[0]
    assert depth == 1
    l = 0
    n_main = 3 * D_CONV + D_SSD + XBC_DIM
    w_dt = w_in[l][:, n_main:]
    w = {
        "ln_mix": ln_mix_w[l].reshape(1, D_MODEL),
        "w_in": w_in[l][:, :n_main].astype(BF16),
        "w_dt": jnp.pad(w_dt, ((0, 0), (0, LANES - SSD_HEADS))).astype(BF16),
        "w_dtt": jnp.pad(w_dt.T, ((0, 2 * SUBLANES - SSD_HEADS), (0, 0))).astype(BF16),
        "conv_a_w": conv_a_w[l], "conv_s_w": conv_s_w[l], "conv_s_b": conv_s_b[l].reshape(1, XBC_DIM),
        "dtb_c": _col(dt_bias[l]), "alog_c": _col(a_log[l]), "dsk_c": _col(d_skip[l]),
        "dtb_r": _row(dt_bias[l]), "alog_r": _row(a_log[l]),
        "ssd_norm": ssd_norm_w[l].reshape(1, D_SSD),
        "w_out": w_out[l].astype(BF16),
        "ln_ffn": ln_ffn_w[l].reshape(1, D_MODEL),
        "w_pq": w_pq[l].astype(BF16),
        "sub_keys": sub_keys[l].astype(BF16),
        "u_pack": _pack_table(u_tab[l]),
        "v_pack": _pack_table(v_tab[l]),
        "u_tab": u_tab[l], "v_tab": v_tab[l],
        "final_norm": final_norm_w.reshape(1, D_MODEL),
    }
    ln_mem = ln_mem_w[l].reshape(1, D_MODEL)
    wq, wo = w_mq[l].astype(BF16), w_mo[l].astype(BF16)

    bp, sp, _ = x_prompt.shape
    bs, ss, _ = x_sample.shape

    mk, mv = _memory_kv(mem_prompt.reshape(bp * N_MEM, D_MODEL), mem_norm_w[l].reshape(1, D_MODEL),
                        w_mk[l].astype(BF16), w_mv[l].astype(BF16))
    hp, pa, ps, ph = _mixer(x_prompt, None, w, 256)
    hp = _attn(hp, mk.reshape(bp, N_MEM, D_MODEL), mv.reshape(bp, N_MEM, D_MODEL), ln_mem, wq, wo, 256)
    y_prompt = _peer(hp.reshape(bp * sp, D_MODEL), w, 256, 64, SC_PROMPT_TOKENS).reshape(bp, sp, D_MODEL)

    hs, sa, ssd, sh = _mixer(x_sample, (cache_conv_a[l], cache_conv_ssd[l],
                                        state_ssd[l].reshape((bs,) + PAIR_STATE)), w, 256)
    hs = _attn(hs, cache_mem_k[l].reshape(bs, N_MEM, D_MODEL), cache_mem_v[l].reshape(bs, N_MEM, D_MODEL),
               ln_mem, wq, wo, 256)
    y_sample = _peer(hs.reshape(bs * ss, D_MODEL), w, 256, 64, 0).reshape(bs, ss, D_MODEL)

    kv_shape = (1, bp, N_MEM, MEM_HEADS, MEM_HEAD_DIM)
    st_shape = (SSD_HEADS, SSD_HEAD_DIM, SSD_STATE)
    return (y_prompt, y_sample, pa[None], ps[None], ph.reshape((1, bp) + st_shape), mk.reshape(kv_shape),
            mv.reshape(kv_shape), sa[None], ssd[None], sh.reshape((1, bs) + st_shape))
```

```python
import functools
import math

import jax
import jax.numpy as jnp
from jax import lax
from jax.experimental import pallas as pl
from jax.experimental.pallas import tpu as pltpu
from jax.experimental.pallas import tpu_sc as plsc

D_MODEL = 1024
CHUNK = 64
D_CONV = 512
CONV_A_W = 3
D_SSD = 512
SSD_HEAD_DIM = 64
SSD_HEADS = 8
SSD_GROUPS = 2
SSD_HPG = 4
SSD_STATE = 128
SSD_CONV_W = 4
XBC_DIM = 1024
N_MEM = 256
MEM_HEADS = 4
MEM_HEAD_DIM = 256
PEER_HEADS = 8
N_KEYS = 128
N_EXPERTS = N_KEYS * N_KEYS
PEER_TOPK = 16
PEER_DK = 256
PEER_DK_HALF = 128
PICKS = PEER_HEADS * PEER_TOPK
EPS = 1e-6

LANES = 128
SUBLANES = 8
HALF_EXPERTS = N_EXPERTS // 2
PAIR_STATE = (SSD_HEADS // 2, 2 * SSD_HEAD_DIM, SSD_STATE)
VMEM_LIMIT = 56 * 1024 * 1024

F32 = jnp.float32
BF16 = jnp.bfloat16
HI = lax.Precision.HIGHEST


def _rms(x, g):
    return x * lax.rsqrt(jnp.mean(x * x, axis=-1, keepdims=True) + EPS) * g


def _silu(x):
    return x * (1.0 / (1.0 + jnp.exp(-x)))


def _softplus(x):
    return jnp.maximum(x, 0.0) + jnp.log1p(jnp.exp(-jnp.abs(x)))


def _dot(a, b):
    return jnp.dot(a.astype(BF16), b.astype(BF16), preferred_element_type=F32)


def _dot_nt(a, b):
    return lax.dot_general(a.astype(BF16), b.astype(BF16), (((1,), (1,)), ((), ())),
                           preferred_element_type=F32)


def _dot_tn(a, b):
    return lax.dot_general(a.astype(BF16), b.astype(BF16), (((0,), (0,)), ((), ())),
                           preferred_element_type=F32)


def _const_spec(shape):
    n = len(shape)
    return pl.BlockSpec(shape, lambda *_: (0,) * n)


def _mixer_kernel(has_hist, ts, lc, *refs):
    if has_hist:
        x_ref, hista_ref, hists_ref, h0_ref = refs[:4]
        refs = refs[4:]
    else:
        x_ref = refs[0]
        refs = refs[1:]
    (lnw_ref, win_ref, wdt_ref, wdtt_ref, caw_ref, csw_ref, csb_ref, dtb_c_ref, alog_c_ref, dsk_c_ref,
     dtb_r_ref, alog_r_ref, nw_ref, wout_ref,
     h1_ref, na_ref, ns_ref, nh_ref, cata, cats, hst) = refs
    s = pl.program_id(1)
    pad = SUBLANES

    @pl.when(s == 0)
    def _():
        if has_hist:
            cata[pad - 2:pad, :] = hista_ref[...]
            cats[pad - 3:pad, :] = hists_ref[...]
            hst[...] = h0_ref[...]
        else:
            cata[0:pad, :] = jnp.zeros((pad, D_CONV), F32)
            cats[0:pad, :] = jnp.zeros((pad, XBC_DIM), F32)
            hst[...] = jnp.zeros(hst.shape, F32)

    x = x_ref[...]
    xn = _rms(x, lnw_ref[...]).astype(BF16)
    proj = jnp.dot(xn, win_ref[...], preferred_element_type=F32)
    dt_c = jnp.dot(xn, wdt_ref[...], preferred_element_type=F32)
    dt_r = lax.dot_general(wdtt_ref[...], xn, (((1,), (1,)), ((), ())),
                           preferred_element_type=F32)
    g_b = proj[:, 0:D_CONV]
    g_c = proj[:, D_CONV:2 * D_CONV]
    v_in = proj[:, 2 * D_CONV:3 * D_CONV]
    z = proj[:, 3 * D_CONV:3 * D_CONV + D_SSD]
    xbc = proj[:, 3 * D_CONV + D_SSD:3 * D_CONV + D_SSD + XBC_DIM]

    ua = g_c * v_in
    cata[pad:pad + ts, :] = ua
    caw = caw_ref[...]
    conv_a = (caw[0:1] * cata[pad - 2:pad - 2 + ts, :] + caw[1:2] * cata[pad - 1:pad - 1 + ts, :]
              + caw[2:3] * ua)
    y_a = g_b * conv_a
    hist_a = cata[pad + ts - 2:pad + ts, :]
    na_ref[...] = hist_a
    cata[pad - 2:pad, :] = hist_a

    cats[pad:pad + ts, :] = xbc
    csw = csw_ref[...]
    conv_s = (csw[0:1] * cats[pad - 3:pad - 3 + ts, :] + csw[1:2] * cats[pad - 2:pad - 2 + ts, :]
              + csw[2:3] * cats[pad - 1:pad - 1 + ts, :] + csw[3:4] * xbc)
    hist_s = cats[pad + ts - 3:pad + ts, :]
    ns_ref[...] = hist_s
    cats[pad - 3:pad, :] = hist_s
    xbc_c = _silu(conv_s + csb_ref[...])
    xs = xbc_c[:, 0:D_SSD]

    dtp_c = _softplus(dt_c + dtb_c_ref[...])
    dtp_r = _softplus(dt_r + dtb_r_ref[...])
    a_c = dtp_c * (-jnp.exp(alog_c_ref[...]))
    a_r = dtp_r * (-jnp.exp(alog_r_ref[...]))
    dsk_c = dsk_c_ref[...]

    lane = lax.broadcasted_iota(jnp.int32, (1, LANES), 1)
    first_head = lane < SSD_HEAD_DIM
    row2 = lax.broadcasted_iota(jnp.int32, (2 * SSD_HEAD_DIM, 1), 0) < SSD_HEAD_DIM
    li = lax.broadcasted_iota(jnp.int32, (lc, lc), 0)
    si = lax.broadcasted_iota(jnp.int32, (lc, lc), 1)
    causal = si <= li
    tril = causal.astype(F32)
    jrow = lax.broadcasted_iota(jnp.int32, (ts, lc), 0)
    scol = lax.broadcasted_iota(jnp.int32, (ts, lc), 1)

    def pair(col, h):
        return jnp.where(first_head, col[:, h:h + 1], col[:, h + 1:h + 2])

    y_chunks = []
    for c in range(ts // lc):
        r0 = c * lc
        rows = slice(r0, r0 + lc)
        acum_c = jnp.dot(tril, a_c[rows], preferred_element_type=F32, precision=HI)
        upper = ((jrow >= r0) & (jrow <= r0 + scol)).astype(F32)
        acum_r = jnp.dot(a_r, upper, preferred_element_type=F32, precision=HI)
        last_c = acum_c[lc - 1:lc, :]
        dec_c = jnp.exp(last_c - acum_c)
        eac_c = jnp.exp(acum_c)
        y_pairs = []
        for g in range(SSD_GROUPS):
            b_g = xbc_c[rows, D_SSD + g * SSD_STATE:D_SSD + (g + 1) * SSD_STATE]
            c_g = xbc_c[rows, D_SSD + (SSD_GROUPS + g) * SSD_STATE:D_SSD + (SSD_GROUPS + g + 1) * SSD_STATE]
            cb = _dot_nt(c_g, b_g)
            for q in range(SSD_HPG // 2):
                h = g * SSD_HPG + 2 * q
                pi = h // 2
                xs_p = xs[rows, pi * LANES:(pi + 1) * LANES]
                xdt = xs_p * pair(dtp_c[rows], h)
                res = []
                for hh in (h, h + 1):
                    seg = acum_c[:, hh:hh + 1] - acum_r[hh:hh + 1, :]
                    m_h = cb * jnp.exp(jnp.where(causal, seg, -jnp.inf))
                    res.append(_dot(m_h, xdt))
                y_diag = jnp.where(first_head, res[0], res[1])
                h_in = hst[pi]
                y_off = _dot_nt(c_g, h_in) * pair(eac_c, h)
                st = _dot_tn(xdt * pair(dec_c, h), b_g)
                cd = jnp.exp(jnp.where(row2, last_c[:, h:h + 1], last_c[:, h + 1:h + 2]))
                hst[pi] = cd * h_in + st
                y_pairs.append(y_diag + y_off + pair(dsk_c, h) * xs_p)
        y_chunks.append(jnp.concatenate(y_pairs, axis=1))
    y = y_chunks[0] if len(y_chunks) == 1 else jnp.concatenate(y_chunks, axis=0)
    y_b = _rms(y * _silu(z), nw_ref[...])
    wout = wout_ref[...]
    out = (jnp.dot(y_a.astype(BF16), wout[0:D_CONV], preferred_element_type=F32)
           + jnp.dot(y_b.astype(BF16), wout[D_CONV:], preferred_element_type=F32))
    h1_ref[...] = x + out
    nh_ref[...] = hst[...]


def _mixer(x, hist, w, ts):
    b, s, _ = x.shape
    lc = min(CHUNK, s)
    ts = min(ts, s)
    has_hist = hist is not None
    per_b3 = lambda shape: pl.BlockSpec((None,) + shape, lambda i, j: (i,) + (0,) * len(shape))
    in_specs = [pl.BlockSpec((None, ts, D_MODEL), lambda i, j: (i, j, 0))]
    args = [x]
    if has_hist:
        in_specs += [per_b3((CONV_A_W - 1, D_CONV)), per_b3((SSD_CONV_W - 1, XBC_DIM)),
                     per_b3(PAIR_STATE)]
        args += list(hist)
    wargs = [w["ln_mix"], w["w_in"], w["w_dt"], w["w_dtt"], w["conv_a_w"], w["conv_s_w"], w["conv_s_b"],
             w["dtb_c"], w["alog_c"], w["dsk_c"], w["dtb_r"], w["alog_r"], w["ssd_norm"], w["w_out"]]
    in_specs += [_const_spec(a.shape) for a in wargs]
    args += wargs
    out_shape = (jax.ShapeDtypeStruct((b, s, D_MODEL), F32),
                 jax.ShapeDtypeStruct((b, CONV_A_W - 1, D_CONV), F32),
                 jax.ShapeDtypeStruct((b, SSD_CONV_W - 1, XBC_DIM), F32),
                 jax.ShapeDtypeStruct((b,) + PAIR_STATE, F32))
    out_specs = (pl.BlockSpec((None, ts, D_MODEL), lambda i, j: (i, j, 0)),
                 per_b3((CONV_A_W - 1, D_CONV)), per_b3((SSD_CONV_W - 1, XBC_DIM)),
                 per_b3(PAIR_STATE))
    return pl.pallas_call(
        functools.partial(_mixer_kernel, has_hist, ts, lc),
        out_shape=out_shape, grid=(b, s // ts), in_specs=in_specs, out_specs=out_specs,
        scratch_shapes=[pltpu.VMEM((SUBLANES + ts, D_CONV), F32), pltpu.VMEM((SUBLANES + ts, XBC_DIM), F32),
                        pltpu.VMEM(PAIR_STATE, F32)],
        compiler_params=pltpu.CompilerParams(dimension_semantics=("parallel", "arbitrary"),
                                             vmem_limit_bytes=VMEM_LIMIT),
        name="mixer_hist" if has_hist else "mixer",
    )(*args)


def _memkv_kernel(m_ref, g_ref, wk_ref, wv_ref, k_ref, v_ref):
    mn = _rms(m_ref[...], g_ref[...]).astype(BF16)
    k_ref[...] = jnp.dot(mn, wk_ref[...], preferred_element_type=F32)
    v_ref[...] = jnp.dot(mn, wv_ref[...], preferred_element_type=F32)


def _memory_kv(mem, g, wk, wv, tm=256):
    t = mem.shape[0]
    blk = pl.BlockSpec((tm, D_MODEL), lambda i: (i, 0))
    return pl.pallas_call(
        _memkv_kernel,
        out_shape=(jax.ShapeDtypeStruct((t, D_MODEL), F32),) * 2,
        grid=(t // tm,), in_specs=[blk, _const_spec(g.shape), _const_spec(wk.shape), _const_spec(wv.shape)],
        out_specs=(blk, blk),
        compiler_params=pltpu.CompilerParams(dimension_semantics=("parallel",), vmem_limit_bytes=VMEM_LIMIT),
        name="memory_kv",
    )(mem, g, wk, wv)


def _attn_kernel(x_ref, k_ref, v_ref, g_ref, wq_ref, wo_ref, o_ref):
    x = x_ref[...]
    xn = _rms(x, g_ref[...]).astype(BF16)
    q = jnp.dot(xn, wq_ref[...], preferred_element_type=F32)
    k = k_ref[...]
    v = v_ref[...]
    wo = wo_ref[...]
    acc = x
    for h in range(MEM_HEADS):
        cols = slice(h * MEM_HEAD_DIM, (h + 1) * MEM_HEAD_DIM)
        sc = _dot_nt(q[:, cols], k[:, cols]) * (MEM_HEAD_DIM ** -0.5)
        sc = sc - jnp.max(sc, axis=-1, keepdims=True)
        p = jnp.exp(sc)
        p = p / jnp.sum(p, axis=-1, keepdims=True)
        o_h = _dot(p, v[:, cols])
        acc = acc + jnp.dot(o_h.astype(BF16), wo[cols, :], preferred_element_type=F32)
    o_ref[...] = acc


def _attn(x, k, v, g, wq, wo, ts):
    b, s, _ = x.shape
    ts = min(ts, s)
    xblk = pl.BlockSpec((None, ts, D_MODEL), lambda i, j: (i, j, 0))
    kvblk = pl.BlockSpec((None, N_MEM, D_MODEL), lambda i, j: (i, 0, 0))
    return pl.pallas_call(
        _attn_kernel,
        out_shape=jax.ShapeDtypeStruct((b, s, D_MODEL), F32),
        grid=(b, s // ts),
        in_specs=[xblk, kvblk, kvblk, _const_spec(g.shape), _const_spec(wq.shape), _const_spec(wo.shape)],
        out_specs=xblk,
        compiler_params=pltpu.CompilerParams(dimension_semantics=("parallel", "parallel"),
                                             vmem_limit_bytes=VMEM_LIMIT),
        name="attn",
    )(x, k, v, g, wq, wo)


def _top_k_rows(sc, k, payload=None):
    n = sc.shape[0]
    rows = lax.broadcasted_iota(jnp.int32, sc.shape, 0).astype(F32)
    vals, outs = [], []
    for _ in range(k):
        m = jnp.max(sc, axis=0, keepdims=True)
        cand = jnp.where(sc == m, rows, float(n))
        first = jnp.min(cand, axis=0, keepdims=True)
        hit = cand == first
        sc = jnp.where(hit, -jnp.inf, sc)
        vals.append(m)
        if payload is None:
            outs.append(first.astype(jnp.int32))
        else:
            outs.append(jnp.sum(jnp.where(hit, payload, 0), axis=0, keepdims=True))
    return jnp.concatenate(vals, axis=0), jnp.concatenate(outs, axis=0)


_LIVE_B = tuple(PEER_TOPK // (a + 1) for a in range(1, SUBLANES))


def _pair_candidates(v1, i1, v2, i2):
    sub = lax.broadcasted_iota(jnp.int32, (SUBLANES, v1.shape[1]), 0)
    comb = [v1[0:1] + v2]
    cidx = [i1[0:1] * N_KEYS + i2]
    for a in range(1, SUBLANES):
        comb.append(jnp.where(sub < _LIVE_B[a - 1], v1[a:a + 1] + v2[0:SUBLANES], -jnp.inf))
        cidx.append(i1[a:a + 1] * N_KEYS + i2[0:SUBLANES])
    comb.append(v1[SUBLANES:] + v2[0:1])
    cidx.append(i1[SUBLANES:] * N_KEYS + i2[0:1])
    return jnp.concatenate(comb, axis=0), jnp.concatenate(cidx, axis=0)


def _select_kernel(tb, x_ref, g_ref, wq_ref, keys_ref, xn_ref, tile_ref, shift_ref, gate_ref):
    @pl.when(pl.program_id(1) == 0)
    def _():
        xn_ref[...] = _rms(x_ref[...], g_ref[...])

    q = jnp.dot(xn_ref[...].astype(BF16), wq_ref[...], preferred_element_type=F32)
    sc1 = _dot_nt(keys_ref[0, 0], q[:, 0:PEER_DK_HALF])
    sc2 = _dot_nt(keys_ref[0, 1], q[:, PEER_DK_HALF:PEER_DK])
    for c0 in range(0, tb, LANES):
        cols = slice(c0, c0 + LANES)
        v1, i1 = _top_k_rows(sc1[:, cols], PEER_TOPK)
        v2, i2 = _top_k_rows(sc2[:, cols], PEER_TOPK)
        comb, cidx = _pair_candidates(v1, i1, v2, i2)
        top, eid = _top_k_rows(comb, PEER_TOPK, cidx)
        ex = jnp.exp(top - top[0:1])
        gate_ref[0, :, cols] = ex / jnp.sum(ex, axis=0, keepdims=True)
        upper = eid >= HALF_EXPERTS
        tile_ref[0, :, cols] = jnp.where(upper, eid - HALF_EXPERTS, eid) * SUBLANES
        shift_ref[0, :, cols] = jnp.where(upper, 0, 16)


def _select(x, g, wq, keys, tb, t0, t):
    b0 = t0 // tb
    xin = pl.BlockSpec((tb, D_MODEL), lambda i, j: (i + b0, 0))
    xblk = pl.BlockSpec((tb, D_MODEL), lambda i, j: (i, 0))
    oblk = pl.BlockSpec((1, PEER_TOPK, tb), lambda i, j: (j, 0, i))
    return pl.pallas_call(
        functools.partial(_select_kernel, tb),
        out_shape=(jax.ShapeDtypeStruct((t, D_MODEL), F32),
                   jax.ShapeDtypeStruct((PEER_HEADS, PEER_TOPK, t), jnp.int32),
                   jax.ShapeDtypeStruct((PEER_HEADS, PEER_TOPK, t), jnp.int32),
                   jax.ShapeDtypeStruct((PEER_HEADS, PEER_TOPK, t), F32)),
        grid=(t // tb, PEER_HEADS),
        in_specs=[xin, _const_spec(g.shape),
                  pl.BlockSpec((D_MODEL, PEER_DK), lambda i, j: (0, j)),
                  pl.BlockSpec((1, 2, N_KEYS, PEER_DK_HALF), lambda i, j: (j, 0, 0, 0))],
        out_specs=(xblk, oblk, oblk, oblk),
        compiler_params=pltpu.CompilerParams(dimension_semantics=("parallel", "arbitrary"),
                                             vmem_limit_bytes=VMEM_LIMIT),
        name="peer_select",
    )(x, g, wq, keys)


HI_HALF_MASK = -(1 << 16)
_BITREV3 = (0, 4, 2, 6, 1, 5, 3, 7)


def _unpack(word, shift):
    return lax.bitcast_convert_type((word << shift) & HI_HALF_MASK, F32)


def _splat_picks(blk, one_pass):
    tb = blk.shape[0]
    r = lax.broadcasted_iota(jnp.int32, (PICKS, LANES), 0)
    c = lax.broadcasted_iota(jnp.int32, (PICKS, LANES), 1)
    eye = (r == c)[None]
    ones = jnp.ones((LANES, LANES), BF16)

    def one(piece):
        diag = jnp.where(eye, piece[:, None, :], 0.0).astype(BF16).reshape(tb * PICKS, LANES)
        return jnp.dot(diag, ones, preferred_element_type=F32)

    if one_pass:
        out = one(blk)
    else:
        hi = blk.astype(BF16).astype(F32)
        rest = blk - hi
        mid = rest.astype(BF16).astype(F32)
        out = one(hi) + one(mid) + one(rest - mid)
    return out.reshape(tb, PICKS, LANES)


def _full(ref, k):
    return jnp.broadcast_to(ref[k:k + 1, :], (SUBLANES, LANES))


def _expert_row(tab_ref, tile_ref, t, k, shifts):
    idx = pl.multiple_of(tile_ref[t, k], SUBLANES)
    return _unpack(tab_ref[pl.ds(idx, SUBLANES), :], _full(shifts, k))


def _hidden_kernel(tb, tile_ref, shift_ref, x_ref, gate_ref, tab_ref, gh_ref, bsh, stage, parts):
    bsh[...] = _splat_picks(shift_ref[...].astype(F32), True).astype(jnp.int32)

    half = PICKS // 2
    quad = SUBLANES // 2
    low = lax.broadcasted_iota(jnp.int32, (SUBLANES, LANES), 0) < quad

    def token(t, carry):
        x = x_ref[t]
        shifts = bsh.at[t]
        for j in range(half):
            a = x * _expert_row(tab_ref, tile_ref, t, j, shifts)
            b = x * _expert_row(tab_ref, tile_ref, t, j + half, shifts)
            stage[j * SUBLANES:(j + 1) * SUBLANES, :] = jnp.where(low, a + pltpu.roll(a, quad, 0),
                                                                  b + pltpu.roll(b, quad, 0))
        for p in range(2):
            part = stage[pl.ds(p * quad, half, stride=SUBLANES), :]
            for s in range(1, quad):
                part = part + stage[pl.ds(p * quad + s, half, stride=SUBLANES), :]
            parts[t, p * half:(p + 1) * half, :] = part
        return carry

    lax.fori_loop(0, tb, token, 0)
    gh_ref[...] = jnp.sum(parts[...], axis=-1)
    hid = gh_ref[...]
    gelu = 0.5 * hid * (1.0 + lax.erf(hid * (2.0 ** -0.5)))
    gh_ref[...] = gate_ref[...] * gelu


def _pick_block(tb, space=None):
    return pl.BlockSpec((tb, PICKS), lambda i: (i, 0), memory_space=space)


def _table_spec(tab):
    return pl.BlockSpec(tab.shape, lambda i: (0, 0), pipeline_mode=pl.Buffered(1))


def _hidden(tile, shift, xg, gate, tab, tb):
    t = xg.shape[0]
    return pl.pallas_call(
        functools.partial(_hidden_kernel, tb),
        out_shape=jax.ShapeDtypeStruct((t, PICKS), F32),
        grid=(t // tb,),
        in_specs=[_pick_block(tb, pltpu.SMEM), _pick_block(tb),
                  pl.BlockSpec((tb, SUBLANES, LANES), lambda i: (i, 0, 0)),
                  _pick_block(tb), _table_spec(tab)],
        out_specs=_pick_block(tb),
        scratch_shapes=[pltpu.VMEM((tb, PICKS, LANES), jnp.int32), pltpu.VMEM((PICKS // 2 * SUBLANES, LANES), F32),
                        pltpu.VMEM((tb, PICKS, LANES), F32)],
        compiler_params=pltpu.CompilerParams(dimension_semantics=("parallel",), vmem_limit_bytes=VMEM_LIMIT),
        name="peer_hidden",
    )(tile, shift, xg, gate, tab)


def _combine_kernel(tb, tile_ref, shift_ref, gh_ref, tab_ref, o_ref, bsh, bg):
    bsh[...] = _splat_picks(shift_ref[...].astype(F32), True).astype(jnp.int32)
    bg[...] = _splat_picks(gh_ref[...], False)

    def token(t, carry):
        shifts, gates = bsh.at[t], bg.at[t]
        acc = [jnp.zeros((SUBLANES, LANES), F32) for _ in range(4)]
        for k in range(PICKS):
            acc[k % 4] = acc[k % 4] + _full(gates, k) * _expert_row(tab_ref, tile_ref, t, k, shifts)
        o_ref[t] = (acc[0] + acc[1]) + (acc[2] + acc[3])
        return carry

    lax.fori_loop(0, tb, token, 0)


def _combine(tile, shift, gh, tab, tb):
    t = gh.shape[0]
    return pl.pallas_call(
        functools.partial(_combine_kernel, tb),
        out_shape=jax.ShapeDtypeStruct((t, SUBLANES, LANES), F32),
        grid=(t // tb,),
        in_specs=[_pick_block(tb, pltpu.SMEM), _pick_block(tb), _pick_block(tb), _table_spec(tab)],
        out_specs=pl.BlockSpec((tb, SUBLANES, LANES), lambda i: (i, 0, 0)),
        scratch_shapes=[pltpu.VMEM((tb, PICKS, LANES), jnp.int32), pltpu.VMEM((tb, PICKS, LANES), F32)],
        compiler_params=pltpu.CompilerParams(dimension_semantics=("parallel",), vmem_limit_bytes=VMEM_LIMIT),
        name="peer_combine",
    )(tile, shift, gh, tab)


SC_LANES = 16
SC_WORKERS = 32
SC_CHUNK = 32
SC_FEATURE_BLOCK = 256
SC_PROMPT_TOKENS = 11264


def _sc_call(body, n, width, scratch):
    mesh = plsc.VectorSubcoreMesh(core_axis_name="c", subcore_axis_name="s")
    return pl.kernel(body, mesh=mesh, out_type=jax.ShapeDtypeStruct((n, width), F32),
                     compiler_params=pltpu.CompilerParams(needs_layout_passes=False),
                     scratch_types=[pltpu.VMEM((PICKS // SC_CHUNK, SC_CHUNK), jnp.int32)] + scratch
                     + [pltpu.VMEM((SC_CHUNK, D_MODEL), F32), pltpu.VMEM((SC_CHUNK, D_MODEL), F32),
                        pltpu.SemaphoreType.DMA, pltpu.SemaphoreType.DMA])


def _sc_token_loop(n, eid_hbm, side_hbm, tab_hbm, out_hbm, idx_v, side_v, out_v, bufs, sems, compute):
    per_worker = n // SC_WORKERS
    wid = lax.axis_index("s") * 2 + lax.axis_index("c")
    chunks = PICKS // SC_CHUNK

    def gather(c):
        return pltpu.make_async_copy(tab_hbm.at[idx_v.at[c]], bufs[c % 2], sems[c % 2])

    @pl.loop(0, per_worker)
    def _(i):
        t = wid * per_worker + i
        pltpu.sync_copy(eid_hbm.at[t], idx_v)
        pltpu.sync_copy(side_hbm.at[t], side_v)
        gather(0).start()
        gather(1).start()
        for c in range(chunks):
            gather(c).wait()
            compute(c, bufs[c % 2])
            if c + 2 < chunks:
                gather(c + 2).start()
        pltpu.sync_copy(out_v, out_hbm.at[t])


def _sc_hidden(eid, x, tab):
    n = eid.shape[0]
    lanes = SC_LANES

    def body(eid_hbm, x_hbm, tab_hbm, out_hbm, idx_v, x_v, hid_v, rows0, rows1, sem0, sem1):
        lane = lax.iota(jnp.int32, lanes)

        def compute(c, rows):
            for g in range(SC_CHUNK // lanes):
                def step(j, accs):
                    off = pl.multiple_of(j * lanes, lanes)
                    xj = x_v[pl.ds(off, lanes)]
                    return tuple(accs[r] + xj * rows[g * lanes + r, pl.ds(off, lanes)] for r in range(lanes))
                accs = lax.fori_loop(0, D_MODEL // lanes, step,
                                     tuple(jnp.zeros((lanes,), F32) for _ in range(lanes)))
                hv = jnp.zeros((lanes,), F32)
                for r in range(lanes):
                    hv = jnp.where(lane == r, jnp.sum(accs[r]), hv)
                hid_v[pl.ds(c * SC_CHUNK + g * lanes, lanes)] = hv

        _sc_token_loop(n, eid_hbm, x_hbm, tab_hbm, out_hbm, idx_v, x_v, hid_v, (rows0, rows1), (sem0, sem1), compute)

    call = _sc_call(body, n, PICKS, [pltpu.VMEM((D_MODEL,), F32), pltpu.VMEM((PICKS,), F32)])
    return call(eid.reshape(n, PICKS // SC_CHUNK, SC_CHUNK), x, tab)


def _sc_combine(eid, gh, tab):
    n = eid.shape[0]
    lanes = SC_LANES
    nacc = SC_FEATURE_BLOCK // lanes

    def body(eid_hbm, g_hbm, tab_hbm, out_hbm, idx_v, g_v, out_v, rows0, rows1, sem0, sem1):
        def compute(c, rows):
            for fb in range(D_MODEL // SC_FEATURE_BLOCK):
                f0 = fb * SC_FEATURE_BLOCK
                if c == 0:
                    init = tuple(jnp.zeros((lanes,), F32) for _ in range(nacc))
                else:
                    init = tuple(out_v[pl.ds(f0 + q * lanes, lanes)] for q in range(nacc))

                def step(r, accs):
                    gk = plsc.load_gather(g_v, [jnp.full((lanes,), c * SC_CHUNK + r, jnp.int32)])
                    return tuple(accs[q] + gk * rows[r, pl.ds(f0 + q * lanes, lanes)] for q in range(nacc))
                accs = lax.fori_loop(0, SC_CHUNK, step, init)
                for q in range(nacc):
                    out_v[pl.ds(f0 + q * lanes, lanes)] = accs[q]

        _sc_token_loop(n, eid_hbm, g_hbm, tab_hbm, out_hbm, idx_v, g_v, out_v, (rows0, rows1), (sem0, sem1), compute)

    call = _sc_call(body, n, D_MODEL, [pltpu.VMEM((PICKS,), F32), pltpu.VMEM((D_MODEL,), F32)])
    return call(eid.reshape(n, PICKS // SC_CHUNK, SC_CHUNK), gh, tab)


def _after_spec(after):
    return pl.BlockSpec((SUBLANES, after.shape[1]), lambda i: (0, 0))


def _gate_gelu_kernel(h_ref, g_ref, after_ref, o_ref):
    hid = h_ref[...]
    o_ref[...] = g_ref[...] * (0.5 * hid * (1.0 + lax.erf(hid * (2.0 ** -0.5))))


def _gate_gelu(hid, gate, tb, after):
    t = hid.shape[0]
    return pl.pallas_call(
        _gate_gelu_kernel, out_shape=jax.ShapeDtypeStruct((t, PICKS), F32), grid=(t // tb,),
        in_specs=[_pick_block(tb), _pick_block(tb), _after_spec(after)], out_specs=_pick_block(tb),
        compiler_params=pltpu.CompilerParams(dimension_semantics=("parallel",)),
        name="gate_gelu",
    )(hid, gate, after)


def _final_kernel(x_ref, p_ref, g_ref, *rest):
    rest[-1][...] = _rms(x_ref[...] + p_ref[...], g_ref[...])


def _final(x, p, g, tb, t0, after=None):
    t = p.shape[0]
    b0 = t0 // tb
    blk = pl.BlockSpec((tb, D_MODEL), lambda i: (i, 0))
    extra = [] if after is None else [after]
    return pl.pallas_call(
        _final_kernel, out_shape=jax.ShapeDtypeStruct((t, D_MODEL), F32), grid=(t // tb,),
        in_specs=[pl.BlockSpec((tb, D_MODEL), lambda i: (i + b0, 0)), blk, _const_spec(g.shape)]
        + [_after_spec(a) for a in extra], out_specs=blk,
        compiler_params=pltpu.CompilerParams(dimension_semantics=("parallel",), vmem_limit_bytes=VMEM_LIMIT),
        name="final_norm",
    )(x, p, g, *extra)


def _pack_table(tab):
    bits = lax.bitcast_convert_type(tab.astype(BF16), jnp.uint16).astype(jnp.uint32)
    packed = (bits[HALF_EXPERTS:] << 16) | bits[:HALF_EXPERTS]
    return lax.bitcast_convert_type(packed, jnp.int32).reshape(HALF_EXPERTS * SUBLANES, LANES)


def _col(v, n=LANES):
    return jnp.pad(v.astype(F32), (0, n - v.shape[0])).reshape(1, n)


def _row(v, n=2 * SUBLANES):
    return jnp.pad(v.astype(F32), (0, n - v.shape[0])).reshape(n, 1)


def _peer(x, w, tb_sel, tb_exp, n_sc):
    t = x.shape[0]
    n_tc = t - n_sc
    to_tok = lambda a: a.reshape(PICKS, -1).T
    if n_sc:
        xn_s, tile_s, shift_s, gate_s = _select(x, w["ln_ffn"], w["w_pq"], w["sub_keys"], tb_sel, 0, n_sc)
        eid = to_tok(tile_s) // SUBLANES + jnp.where(to_tok(shift_s) == 0, HALF_EXPERTS, 0)
        hid_s = _sc_hidden(eid, xn_s, w["u_tab"])
    xn, tile, shift, gate = _select(x, w["ln_ffn"], w["w_pq"], w["sub_keys"], tb_sel, n_sc, n_tc)
    tile_t, shift_t = to_tok(tile), to_tok(shift)
    gh = _hidden(tile_t, shift_t, xn.reshape(n_tc, SUBLANES, LANES), to_tok(gate), w["u_pack"], tb_exp)
    if n_sc:
        out_s = _sc_combine(eid, _gate_gelu(hid_s, to_tok(gate_s), tb_sel, after=gh), w["v_tab"])
    out = _combine(tile_t, shift_t, gh, w["v_pack"], tb_exp).reshape(n_tc, D_MODEL)
    y = _final(x, out, w["final_norm"], tb_sel, n_sc)
    if n_sc:
        y = jnp.concatenate([_final(x, out_s, w["final_norm"], tb_sel, 0, after=out), y], axis=0)
    return y


def kernel(x_prompt, x_sample, mem_prompt, cache_conv_a, cache_conv_ssd, state_ssd, cache_mem_k, cache_mem_v, ln_mix_w, w_in, conv_a_w, conv_s_w, conv_s_b, dt_bias, a_log, d_skip, ssd_norm_w, w_out, ln_mem_w, mem_norm_w, w_mq, w_mk, w_mv, w_mo, ln_ffn_w, w_pq, sub_keys, u_tab, v_tab, final_norm_w):
    depth = w_in.shape[0]
    assert depth == 1
    l = 0
    n_main = 3 * D_CONV + D_SSD + XBC_DIM
    w_dt = w_in[l][:, n_main:]
    w = {
        "ln_mix": ln_mix_w[l].reshape(1, D_MODEL),
        "w_in": w_in[l][:, :n_main].astype(BF16),
        "w_dt": jnp.pad(w_dt, ((0, 0), (0, LANES - SSD_HEADS))).astype(BF16),
        "w_dtt": jnp.pad(w_dt.T, ((0, 2 * SUBLANES - SSD_HEADS), (0, 0))).astype(BF16),
        "conv_a_w": conv_a_w[l], "conv_s_w": conv_s_w[l], "conv_s_b": conv_s_b[l].reshape(1, XBC_DIM),
        "dtb_c": _col(dt_bias[l]), "alog_c": _col(a_log[l]), "dsk_c": _col(d_skip[l]),
        "dtb_r": _row(dt_bias[l]), "alog_r": _row(a_log[l]),
        "ssd_norm": ssd_norm_w[l].reshape(1, D_SSD),
        "w_out": w_out[l].astype(BF16),
        "ln_ffn": ln_ffn_w[l].reshape(1, D_MODEL),
        "w_pq": w_pq[l].astype(BF16),
        "sub_keys": sub_keys[l].astype(BF16),
        "u_pack": _pack_table(u_tab[l]),
        "v_pack": _pack_table(v_tab[l]),
        "u_tab": u_tab[l], "v_tab": v_tab[l],
        "final_norm": final_norm_w.reshape(1, D_MODEL),
    }
    ln_mem = ln_mem_w[l].reshape(1, D_MODEL)
    wq, wo = w_mq[l].astype(BF16), w_mo[l].astype(BF16)

    bp, sp, _ = x_prompt.shape
    bs, ss, _ = x_sample.shape

    mk, mv = _memory_kv(mem_prompt.reshape(bp * N_MEM, D_MODEL), mem_norm_w[l].reshape(1, D_MODEL),
                        w_mk[l].astype(BF16), w_mv[l].astype(BF16))
    hp, pa, ps, ph = _mixer(x_prompt, None, w, 256)
    hp = _attn(hp, mk.reshape(bp, N_MEM, D_MODEL), mv.reshape(bp, N_MEM, D_MODEL), ln_mem, wq, wo, 256)
    y_prompt = _peer(hp.reshape(bp * sp, D_MODEL), w, 256, 64, SC_PROMPT_TOKENS).reshape(bp, sp, D_MODEL)

    hs, sa, ssd, sh = _mixer(x_sample, (cache_conv_a[l], cache_conv_ssd[l],
                                        state_ssd[l].reshape((bs,) + PAIR_STATE)), w, 256)
    hs = _attn(hs, cache_mem_k[l].reshape(bs, N_MEM, D_MODEL), cache_mem_v[l].reshape(bs, N_MEM, D_MODEL),
               ln_mem, wq, wo, 256)
    y_sample = _peer(hs.reshape(bs * ss, D_MODEL), w, 256, 64, 0).reshape(bs, ss, D_MODEL)

    kv_shape = (1, bp, N_MEM, MEM_HEADS, MEM_HEAD_DIM)
    st_shape = (SSD_HEADS, SSD_HEAD_DIM, SSD_STATE)
    return (y_prompt, y_sample, pa[None], ps[None], ph.reshape((1, bp) + st_shape), mk.reshape(kv_shape),
            mv.reshape(kv_shape), sa[None], ssd[None], sh.reshape((1, bs) + st_shape))
```

```python
import functools
import math

import jax
import jax.numpy as jnp
from jax import lax
from jax.experimental import pallas as pl
from jax.experimental.pallas import tpu as pltpu
from jax.experimental.pallas import tpu_sc as plsc

D_MODEL = 1024
CHUNK = 64
D_CONV = 512
CONV_A_W = 3
D_SSD = 512
SSD_HEAD_DIM = 64
SSD_HEADS = 8
SSD_GROUPS = 2
SSD_HPG = 4
SSD_STATE = 128
SSD_CONV_W = 4
XBC_DIM = 1024
N_MEM = 256
MEM_HEADS = 4
MEM_HEAD_DIM = 256
PEER_HEADS = 8
N_KEYS = 128
N_EXPERTS = N_KEYS * N_KEYS
PEER_TOPK = 16
PEER_DK = 256
PEER_DK_HALF = 128
PICKS = PEER_HEADS * PEER_TOPK
EPS = 1e-6

LANES = 128
SUBLANES = 8
HALF_EXPERTS = N_EXPERTS // 2
PAIR_STATE = (SSD_HEADS // 2, 2 * SSD_HEAD_DIM, SSD_STATE)
VMEM_LIMIT = 56 * 1024 * 1024

F32 = jnp.float32
BF16 = jnp.bfloat16
HI = lax.Precision.HIGHEST


def _rms(x, g):
    return x * lax.rsqrt(jnp.mean(x * x, axis=-1, keepdims=True) + EPS) * g


def _silu(x):
    return x * (1.0 / (1.0 + jnp.exp(-x)))


def _softplus(x):
    return jnp.maximum(x, 0.0) + jnp.log1p(jnp.exp(-jnp.abs(x)))


def _dot(a, b):
    return jnp.dot(a.astype(BF16), b.astype(BF16), preferred_element_type=F32)


def _dot_nt(a, b):
    return lax.dot_general(a.astype(BF16), b.astype(BF16), (((1,), (1,)), ((), ())),
                           preferred_element_type=F32)


def _dot_tn(a, b):
    return lax.dot_general(a.astype(BF16), b.astype(BF16), (((0,), (0,)), ((), ())),
                           preferred_element_type=F32)


def _const_spec(shape):
    n = len(shape)
    return pl.BlockSpec(shape, lambda *_: (0,) * n)


def _mixer_kernel(has_hist, ts, lc, *refs):
    if has_hist:
        x_ref, hista_ref, hists_ref, h0_ref = refs[:4]
        refs = refs[4:]
    else:
        x_ref = refs[0]
        refs = refs[1:]
    (lnw_ref, win_ref, wdt_ref, wdtt_ref, caw_ref, csw_ref, csb_ref, dtb_c_ref, alog_c_ref, dsk_c_ref,
     dtb_r_ref, alog_r_ref, nw_ref, wout_ref,
     h1_ref, na_ref, ns_ref, nh_ref, cata, cats, hst) = refs
    s = pl.program_id(1)
    pad = SUBLANES

    @pl.when(s == 0)
    def _():
        if has_hist:
            cata[pad - 2:pad, :] = hista_ref[...]
            cats[pad - 3:pad, :] = hists_ref[...]
            hst[...] = h0_ref[...]
        else:
            cata[0:pad, :] = jnp.zeros((pad, D_CONV), F32)
            cats[0:pad, :] = jnp.zeros((pad, XBC_DIM), F32)
            hst[...] = jnp.zeros(hst.shape, F32)

    x = x_ref[...]
    xn = _rms(x, lnw_ref[...]).astype(BF16)
    proj = jnp.dot(xn, win_ref[...], preferred_element_type=F32)
    dt_c = jnp.dot(xn, wdt_ref[...], preferred_element_type=F32)
    dt_r = lax.dot_general(wdtt_ref[...], xn, (((1,), (1,)), ((), ())),
                           preferred_element_type=F32)
    g_b = proj[:, 0:D_CONV]
    g_c = proj[:, D_CONV:2 * D_CONV]
    v_in = proj[:, 2 * D_CONV:3 * D_CONV]
    z = proj[:, 3 * D_CONV:3 * D_CONV + D_SSD]
    xbc = proj[:, 3 * D_CONV + D_SSD:3 * D_CONV + D_SSD + XBC_DIM]

    ua = g_c * v_in
    cata[pad:pad + ts, :] = ua
    caw = caw_ref[...]
    conv_a = (caw[0:1] * cata[pad - 2:pad - 2 + ts, :] + caw[1:2] * cata[pad - 1:pad - 1 + ts, :]
              + caw[2:3] * ua)
    y_a = g_b * conv_a
    hist_a = cata[pad + ts - 2:pad + ts, :]
    na_ref[...] = hist_a
    cata[pad - 2:pad, :] = hist_a

    cats[pad:pad + ts, :] = xbc
    csw = csw_ref[...]
    conv_s = (csw[0:1] * cats[pad - 3:pad - 3 + ts, :] + csw[1:2] * cats[pad - 2:pad - 2 + ts, :]
              + csw[2:3] * cats[pad - 1:pad - 1 + ts, :] + csw[3:4] * xbc)
    hist_s = cats[pad + ts - 3:pad + ts, :]
    ns_ref[...] = hist_s
    cats[pad - 3:pad, :] = hist_s
    xbc_c = _silu(conv_s + csb_ref[...])
    xs = xbc_c[:, 0:D_SSD]

    dtp_c = _softplus(dt_c + dtb_c_ref[...])
    dtp_r = _softplus(dt_r + dtb_r_ref[...])
    a_c = dtp_c * (-jnp.exp(alog_c_ref[...]))
    a_r = dtp_r * (-jnp.exp(alog_r_ref[...]))
    dsk_c = dsk_c_ref[...]

    lane = lax.broadcasted_iota(jnp.int32, (1, LANES), 1)
    first_head = lane < SSD_HEAD_DIM
    row2 = lax.broadcasted_iota(jnp.int32, (2 * SSD_HEAD_DIM, 1), 0) < SSD_HEAD_DIM
    li = lax.broadcasted_iota(jnp.int32, (lc, lc), 0)
    si = lax.broadcasted_iota(jnp.int32, (lc, lc), 1)
    causal = si <= li
    tril = causal.astype(F32)
    jrow = lax.broadcasted_iota(jnp.int32, (ts, lc), 0)
    scol = lax.broadcasted_iota(jnp.int32, (ts, lc), 1)

    def pair(col, h):
        return jnp.where(first_head, col[:, h:h + 1], col[:, h + 1:h + 2])

    y_chunks = []
    for c in range(ts // lc):
        r0 = c * lc
        rows = slice(r0, r0 + lc)
        acum_c = jnp.dot(tril, a_c[rows], preferred_element_type=F32, precision=HI)
        upper = ((jrow >= r0) & (jrow <= r0 + scol)).astype(F32)
        acum_r = jnp.dot(a_r, upper, preferred_element_type=F32, precision=HI)
        last_c = acum_c[lc - 1:lc, :]
        dec_c = jnp.exp(last_c - acum_c)
        eac_c = jnp.exp(acum_c)
        y_pairs = []
        for g in range(SSD_GROUPS):
            b_g = xbc_c[rows, D_SSD + g * SSD_STATE:D_SSD + (g + 1) * SSD_STATE]
            c_g = xbc_c[rows, D_SSD + (SSD_GROUPS + g) * SSD_STATE:D_SSD + (SSD_GROUPS + g + 1) * SSD_STATE]
            cb = _dot_nt(c_g, b_g)
            for q in range(SSD_HPG // 2):
                h = g * SSD_HPG + 2 * q
                pi = h // 2
                xs_p = xs[rows, pi * LANES:(pi + 1) * LANES]
                xdt = xs_p * pair(dtp_c[rows], h)
                res = []
                for hh in (h, h + 1):
                    seg = acum_c[:, hh:hh + 1] - acum_r[hh:hh + 1, :]
                    m_h = cb * jnp.exp(jnp.where(causal, seg, -jnp.inf))
                    res.append(_dot(m_h, xdt))
                y_diag = jnp.where(first_head, res[0], res[1])
                h_in = hst[pi]
                y_off = _dot_nt(c_g, h_in) * pair(eac_c, h)
                st = _dot_tn(xdt * pair(dec_c, h), b_g)
                cd = jnp.exp(jnp.where(row2, last_c[:, h:h + 1], last_c[:, h + 1:h + 2]))
                hst[pi] = cd * h_in + st
                y_pairs.append(y_diag + y_off + pair(dsk_c, h) * xs_p)
        y_chunks.append(jnp.concatenate(y_pairs, axis=1))
    y = y_chunks[0] if len(y_chunks) == 1 else jnp.concatenate(y_chunks, axis=0)
    y_b = _rms(y * _silu(z), nw_ref[...])
    wout = wout_ref[...]
    out = (jnp.dot(y_a.astype(BF16), wout[0:D_CONV], preferred_element_type=F32)
           + jnp.dot(y_b.astype(BF16), wout[D_CONV:], preferred_element_type=F32))
    h1_ref[...] = x + out
    nh_ref[...] = hst[...]


def _mixer(x, hist, w, ts):
    b, s, _ = x.shape
    lc = min(CHUNK, s)
    ts = min(ts, s)
    has_hist = hist is not None
    per_b3 = lambda shape: pl.BlockSpec((None,) + shape, lambda i, j: (i,) + (0,) * len(shape))
    in_specs = [pl.BlockSpec((None, ts, D_MODEL), lambda i, j: (i, j, 0))]
    args = [x]
    if has_hist:
        in_specs += [per_b3((CONV_A_W - 1, D_CONV)), per_b3((SSD_CONV_W - 1, XBC_DIM)),
                     per_b3(PAIR_STATE)]
        args += list(hist)
    wargs = [w["ln_mix"], w["w_in"], w["w_dt"], w["w_dtt"], w["conv_a_w"], w["conv_s_w"], w["conv_s_b"],
             w["dtb_c"], w["alog_c"], w["dsk_c"], w["dtb_r"], w["alog_r"], w["ssd_norm"], w["w_out"]]
    in_specs += [_const_spec(a.shape) for a in wargs]
    args += wargs
    out_shape = (jax.ShapeDtypeStruct((b, s, D_MODEL), F32),
                 jax.ShapeDtypeStruct((b, CONV_A_W - 1, D_CONV), F32),
                 jax.ShapeDtypeStruct((b, SSD_CONV_W - 1, XBC_DIM), F32),
                 jax.ShapeDtypeStruct((b,) + PAIR_STATE, F32))
    out_specs = (pl.BlockSpec((None, ts, D_MODEL), lambda i, j: (i, j, 0)),
                 per_b3((CONV_A_W - 1, D_CONV)), per_b3((SSD_CONV_W - 1, XBC_DIM)),
                 per_b3(PAIR_STATE))
    return pl.pallas_call(
        functools.partial(_mixer_kernel, has_hist, ts, lc),
        out_shape=out_shape, grid=(b, s // ts), in_specs=in_specs, out_specs=out_specs,
        scratch_shapes=[pltpu.VMEM((SUBLANES + ts, D_CONV), F32), pltpu.VMEM((SUBLANES + ts, XBC_DIM), F32),
                        pltpu.VMEM(PAIR_STATE, F32)],
        compiler_params=pltpu.CompilerParams(dimension_semantics=("parallel", "arbitrary"),
                                             vmem_limit_bytes=VMEM_LIMIT),
        name="mixer_hist" if has_hist else "mixer",
    )(*args)


def _memkv_kernel(m_ref, g_ref, wk_ref, wv_ref, k_ref, v_ref):
    mn = _rms(m_ref[...], g_ref[...]).astype(BF16)
    k_ref[...] = jnp.dot(mn, wk_ref[...], preferred_element_type=F32)
    v_ref[...] = jnp.dot(mn, wv_ref[...], preferred_element_type=F32)


def _memory_kv(mem, g, wk, wv, tm=256):
    t = mem.shape[0]
    blk = pl.BlockSpec((tm, D_MODEL), lambda i: (i, 0))
    return pl.pallas_call(
        _memkv_kernel,
        out_shape=(jax.ShapeDtypeStruct((t, D_MODEL), F32),) * 2,
        grid=(t // tm,), in_specs=[blk, _const_spec(g.shape), _const_spec(wk.shape), _const_spec(wv.shape)],
        out_specs=(blk, blk),
        compiler_params=pltpu.CompilerParams(dimension_semantics=("parallel",), vmem_limit_bytes=VMEM_LIMIT),
        name="memory_kv",
    )(mem, g, wk, wv)


def _attn_kernel(x_ref, k_ref, v_ref, g_ref, wq_ref, wo_ref, o_ref):
    x = x_ref[...]
    xn = _rms(x, g_ref[...]).astype(BF16)
    q = jnp.dot(xn, wq_ref[...], preferred_element_type=F32)
    k = k_ref[...]
    v = v_ref[...]
    wo = wo_ref[...]
    acc = x
    for h in range(MEM_HEADS):
        cols = slice(h * MEM_HEAD_DIM, (h + 1) * MEM_HEAD_DIM)
        sc = _dot_nt(q[:, cols], k[:, cols]) * (MEM_HEAD_DIM ** -0.5)
        sc = sc - jnp.max(sc, axis=-1, keepdims=True)
        p = jnp.exp(sc)
        p = p / jnp.sum(p, axis=-1, keepdims=True)
        o_h = _dot(p, v[:, cols])
        acc = acc + jnp.dot(o_h.astype(BF16), wo[cols, :], preferred_element_type=F32)
    o_ref[...] = acc


def _attn(x, k, v, g, wq, wo, ts):
    b, s, _ = x.shape
    ts = min(ts, s)
    xblk = pl.BlockSpec((None, ts, D_MODEL), lambda i, j: (i, j, 0))
    kvblk = pl.BlockSpec((None, N_MEM, D_MODEL), lambda i, j: (i, 0, 0))
    return pl.pallas_call(
        _attn_kernel,
        out_shape=jax.ShapeDtypeStruct((b, s, D_MODEL), F32),
        grid=(b, s // ts),
        in_specs=[xblk, kvblk, kvblk, _const_spec(g.shape), _const_spec(wq.shape), _const_spec(wo.shape)],
        out_specs=xblk,
        compiler_params=pltpu.CompilerParams(dimension_semantics=("parallel", "parallel"),
                                             vmem_limit_bytes=VMEM_LIMIT),
        name="attn",
    )(x, k, v, g, wq, wo)


def _top_k_rows(sc, k, payload=None):
    n = sc.shape[0]
    rows = lax.broadcasted_iota(jnp.int32, sc.shape, 0).astype(F32)
    vals, outs = [], []
    for _ in range(k):
        m = jnp.max(sc, axis=0, keepdims=True)
        cand = jnp.where(sc == m, rows, float(n))
        first = jnp.min(cand, axis=0, keepdims=True)
        hit = cand == first
        sc = jnp.where(hit, -jnp.inf, sc)
        vals.append(m)
        if payload is None:
            outs.append(first.astype(jnp.int32))
        else:
            outs.append(jnp.sum(jnp.where(hit, payload, 0), axis=0, keepdims=True))
    return jnp.concatenate(vals, axis=0), jnp.concatenate(outs, axis=0)


_LIVE_B = tuple(PEER_TOPK // (a + 1) for a in range(1, SUBLANES))


def _pair_candidates(v1, i1, v2, i2):
    sub = lax.broadcasted_iota(jnp.int32, (SUBLANES, v1.shape[1]), 0)
    comb = [v1[0:1] + v2]
    cidx = [i1[0:1] * N_KEYS + i2]
    for a in range(1, SUBLANES):
        comb.append(jnp.where(sub < _LIVE_B[a - 1], v1[a:a + 1] + v2[0:SUBLANES], -jnp.inf))
        cidx.append(i1[a:a + 1] * N_KEYS + i2[0:SUBLANES])
    comb.append(v1[SUBLANES:] + v2[0:1])
    cidx.append(i1[SUBLANES:] * N_KEYS + i2[0:1])
    return jnp.concatenate(comb, axis=0), jnp.concatenate(cidx, axis=0)


def _select_kernel(tb, x_ref, g_ref, wq_ref, keys_ref, *rest):
    xn_ref, tile_ref, shift_ref, gate_ref = rest[-4:]

    @pl.when(pl.program_id(1) == 0)
    def _():
        xn_ref[...] = _rms(x_ref[...], g_ref[...])

    q = jnp.dot(xn_ref[...].astype(BF16), wq_ref[...], preferred_element_type=F32)
    sc1 = _dot_nt(keys_ref[0, 0], q[:, 0:PEER_DK_HALF])
    sc2 = _dot_nt(keys_ref[0, 1], q[:, PEER_DK_HALF:PEER_DK])
    for c0 in range(0, tb, LANES):
        cols = slice(c0, c0 + LANES)
        v1, i1 = _top_k_rows(sc1[:, cols], PEER_TOPK)
        v2, i2 = _top_k_rows(sc2[:, cols], PEER_TOPK)
        comb, cidx = _pair_candidates(v1, i1, v2, i2)
        top, eid = _top_k_rows(comb, PEER_TOPK, cidx)
        ex = jnp.exp(top - top[0:1])
        gate_ref[0, :, cols] = ex / jnp.sum(ex, axis=0, keepdims=True)
        upper = eid >= HALF_EXPERTS
        tile_ref[0, :, cols] = jnp.where(upper, eid - HALF_EXPERTS, eid) * SUBLANES
        shift_ref[0, :, cols] = jnp.where(upper, 0, 16)


def _select(x, g, wq, keys, tb, t0, t, after=None):
    b0 = t0 // tb
    extra = [] if after is None else [after]
    xin = pl.BlockSpec((tb, D_MODEL), lambda i, j: (i + b0, 0))
    xblk = pl.BlockSpec((tb, D_MODEL), lambda i, j: (i, 0))
    oblk = pl.BlockSpec((1, PEER_TOPK, tb), lambda i, j: (j, 0, i))
    return pl.pallas_call(
        functools.partial(_select_kernel, tb),
        out_shape=(jax.ShapeDtypeStruct((t, D_MODEL), F32),
                   jax.ShapeDtypeStruct((PEER_HEADS, PEER_TOPK, t), jnp.int32),
                   jax.ShapeDtypeStruct((PEER_HEADS, PEER_TOPK, t), jnp.int32),
                   jax.ShapeDtypeStruct((PEER_HEADS, PEER_TOPK, t), F32)),
        grid=(t // tb, PEER_HEADS),
        in_specs=[xin, _const_spec(g.shape),
                  pl.BlockSpec((D_MODEL, PEER_DK), lambda i, j: (0, j)),
                  pl.BlockSpec((1, 2, N_KEYS, PEER_DK_HALF), lambda i, j: (j, 0, 0, 0))]
        + [pl.BlockSpec((SUBLANES, a.shape[1]), lambda i, j: (0, 0)) for a in extra],
        out_specs=(xblk, oblk, oblk, oblk),
        compiler_params=pltpu.CompilerParams(dimension_semantics=("parallel", "arbitrary"),
                                             vmem_limit_bytes=VMEM_LIMIT),
        name="peer_select",
    )(x, g, wq, keys, *extra)


HI_HALF_MASK = -(1 << 16)
_BITREV3 = (0, 4, 2, 6, 1, 5, 3, 7)


def _unpack(word, shift):
    return lax.bitcast_convert_type((word << shift) & HI_HALF_MASK, F32)


def _splat_picks(blk, one_pass):
    tb = blk.shape[0]
    r = lax.broadcasted_iota(jnp.int32, (PICKS, LANES), 0)
    c = lax.broadcasted_iota(jnp.int32, (PICKS, LANES), 1)
    eye = (r == c)[None]
    ones = jnp.ones((LANES, LANES), BF16)

    def one(piece):
        diag = jnp.where(eye, piece[:, None, :], 0.0).astype(BF16).reshape(tb * PICKS, LANES)
        return jnp.dot(diag, ones, preferred_element_type=F32)

    if one_pass:
        out = one(blk)
    else:
        hi = blk.astype(BF16).astype(F32)
        rest = blk - hi
        mid = rest.astype(BF16).astype(F32)
        out = one(hi) + one(mid) + one(rest - mid)
    return out.reshape(tb, PICKS, LANES)


def _full(ref, k):
    return jnp.broadcast_to(ref[k:k + 1, :], (SUBLANES, LANES))


def _expert_row(tab_ref, tile_ref, t, k, shifts):
    idx = pl.multiple_of(tile_ref[t, k], SUBLANES)
    return _unpack(tab_ref[pl.ds(idx, SUBLANES), :], _full(shifts, k))


def _hidden_kernel(tb, tile_ref, shift_ref, x_ref, gate_ref, tab_ref, gh_ref, bsh, stage, parts):
    bsh[...] = _splat_picks(shift_ref[...].astype(F32), True).astype(jnp.int32)

    half = PICKS // 2
    quad = SUBLANES // 2
    low = lax.broadcasted_iota(jnp.int32, (SUBLANES, LANES), 0) < quad

    def token(t, carry):
        x = x_ref[t]
        shifts = bsh.at[t]
        for j in range(half):
            a = x * _expert_row(tab_ref, tile_ref, t, j, shifts)
            b = x * _expert_row(tab_ref, tile_ref, t, j + half, shifts)
            stage[j * SUBLANES:(j + 1) * SUBLANES, :] = jnp.where(low, a + pltpu.roll(a, quad, 0),
                                                                  b + pltpu.roll(b, quad, 0))
        for p in range(2):
            part = stage[pl.ds(p * quad, half, stride=SUBLANES), :]
            for s in range(1, quad):
                part = part + stage[pl.ds(p * quad + s, half, stride=SUBLANES), :]
            parts[t, p * half:(p + 1) * half, :] = part
        return carry

    lax.fori_loop(0, tb, token, 0)
    gh_ref[...] = jnp.sum(parts[...], axis=-1)
    hid = gh_ref[...]
    gelu = 0.5 * hid * (1.0 + lax.erf(hid * (2.0 ** -0.5)))
    gh_ref[...] = gate_ref[...] * gelu


def _pick_block(tb, space=None):
    return pl.BlockSpec((tb, PICKS), lambda i: (i, 0), memory_space=space)


def _table_spec(tab):
    return pl.BlockSpec(tab.shape, lambda i: (0, 0), pipeline_mode=pl.Buffered(1))


def _hidden(tile, shift, xg, gate, tab, tb):
    t = xg.shape[0]
    return pl.pallas_call(
        functools.partial(_hidden_kernel, tb),
        out_shape=jax.ShapeDtypeStruct((t, PICKS), F32),
        grid=(t // tb,),
        in_specs=[_pick_block(tb, pltpu.SMEM), _pick_block(tb),
                  pl.BlockSpec((tb, SUBLANES, LANES), lambda i: (i, 0, 0)),
                  _pick_block(tb), _table_spec(tab)],
        out_specs=_pick_block(tb),
        scratch_shapes=[pltpu.VMEM((tb, PICKS, LANES), jnp.int32), pltpu.VMEM((PICKS // 2 * SUBLANES, LANES), F32),
                        pltpu.VMEM((tb, PICKS, LANES), F32)],
        compiler_params=pltpu.CompilerParams(dimension_semantics=("parallel",), vmem_limit_bytes=VMEM_LIMIT),
        name="peer_hidden",
    )(tile, shift, xg, gate, tab)


def _combine_kernel(tb, tile_ref, shift_ref, gh_ref, tab_ref, o_ref, bsh, bg):
    bsh[...] = _splat_picks(shift_ref[...].astype(F32), True).astype(jnp.int32)
    bg[...] = _splat_picks(gh_ref[...], False)

    def token(t, carry):
        shifts, gates = bsh.at[t], bg.at[t]
        acc = [jnp.zeros((SUBLANES, LANES), F32) for _ in range(4)]
        for k in range(PICKS):
            acc[k % 4] = acc[k % 4] + _full(gates, k) * _expert_row(tab_ref, tile_ref, t, k, shifts)
        o_ref[t] = (acc[0] + acc[1]) + (acc[2] + acc[3])
        return carry

    lax.fori_loop(0, tb, token, 0)


def _combine(tile, shift, gh, tab, tb):
    t = gh.shape[0]
    return pl.pallas_call(
        functools.partial(_combine_kernel, tb),
        out_shape=jax.ShapeDtypeStruct((t, SUBLANES, LANES), F32),
        grid=(t // tb,),
        in_specs=[_pick_block(tb, pltpu.SMEM), _pick_block(tb), _pick_block(tb), _table_spec(tab)],
        out_specs=pl.BlockSpec((tb, SUBLANES, LANES), lambda i: (i, 0, 0)),
        scratch_shapes=[pltpu.VMEM((tb, PICKS, LANES), jnp.int32), pltpu.VMEM((tb, PICKS, LANES), F32)],
        compiler_params=pltpu.CompilerParams(dimension_semantics=("parallel",), vmem_limit_bytes=VMEM_LIMIT),
        name="peer_combine",
    )(tile, shift, gh, tab)


SC_LANES = 16
SC_WORKERS = 32
SC_CHUNK = 32
SC_FEATURE_BLOCK = 256
SC_PROMPT_TOKENS = 13312


def _sc_call(body, n, width, scratch):
    mesh = plsc.VectorSubcoreMesh(core_axis_name="c", subcore_axis_name="s")
    return pl.kernel(body, mesh=mesh, out_type=jax.ShapeDtypeStruct((n, width), F32),
                     compiler_params=pltpu.CompilerParams(needs_layout_passes=False),
                     scratch_types=[pltpu.VMEM((PICKS // SC_CHUNK, SC_CHUNK), jnp.int32)] + scratch
                     + [pltpu.VMEM((SC_CHUNK, D_MODEL), F32), pltpu.VMEM((SC_CHUNK, D_MODEL), F32),
                        pltpu.SemaphoreType.DMA, pltpu.SemaphoreType.DMA])


def _sc_token_loop(n, eid_hbm, side_hbm, tab_hbm, out_hbm, idx_v, side_v, out_v, bufs, sems, compute):
    per_worker = n // SC_WORKERS
    wid = lax.axis_index("s") * 2 + lax.axis_index("c")
    chunks = PICKS // SC_CHUNK

    def gather(c):
        return pltpu.make_async_copy(tab_hbm.at[idx_v.at[c]], bufs[c % 2], sems[c % 2])

    @pl.loop(0, per_worker)
    def _(i):
        t = wid * per_worker + i
        pltpu.sync_copy(eid_hbm.at[t], idx_v)
        pltpu.sync_copy(side_hbm.at[t], side_v)
        gather(0).start()
        gather(1).start()
        for c in range(chunks):
            gather(c).wait()
            compute(c, bufs[c % 2])
            if c + 2 < chunks:
                gather(c + 2).start()
        pltpu.sync_copy(out_v, out_hbm.at[t])


def _sc_hidden(eid, x, tab):
    n = eid.shape[0]
    lanes = SC_LANES

    def body(eid_hbm, x_hbm, tab_hbm, out_hbm, idx_v, x_v, hid_v, rows0, rows1, sem0, sem1):
        lane = lax.iota(jnp.int32, lanes)

        def compute(c, rows):
            for g in range(SC_CHUNK // lanes):
                def step(j, accs):
                    off = pl.multiple_of(j * lanes, lanes)
                    xj = x_v[pl.ds(off, lanes)]
                    return tuple(accs[r] + xj * rows[g * lanes + r, pl.ds(off, lanes)] for r in range(lanes))
                accs = lax.fori_loop(0, D_MODEL // lanes, step,
                                     tuple(jnp.zeros((lanes,), F32) for _ in range(lanes)))
                hv = jnp.zeros((lanes,), F32)
                for r in range(lanes):
                    hv = jnp.where(lane == r, jnp.sum(accs[r]), hv)
                hid_v[pl.ds(c * SC_CHUNK + g * lanes, lanes)] = hv

        _sc_token_loop(n, eid_hbm, x_hbm, tab_hbm, out_hbm, idx_v, x_v, hid_v, (rows0, rows1), (sem0, sem1), compute)

    call = _sc_call(body, n, PICKS, [pltpu.VMEM((D_MODEL,), F32), pltpu.VMEM((PICKS,), F32)])
    return call(eid.reshape(n, PICKS // SC_CHUNK, SC_CHUNK), x, tab)


def _sc_combine(eid, gh, tab):
    n = eid.shape[0]
    lanes = SC_LANES
    nacc = SC_FEATURE_BLOCK // lanes

    def body(eid_hbm, g_hbm, tab_hbm, out_hbm, idx_v, g_v, out_v, rows0, rows1, sem0, sem1):
        def compute(c, rows):
            for fb in range(D_MODEL // SC_FEATURE_BLOCK):
                f0 = fb * SC_FEATURE_BLOCK
                if c == 0:
                    init = tuple(jnp.zeros((lanes,), F32) for _ in range(nacc))
                else:
                    init = tuple(out_v[pl.ds(f0 + q * lanes, lanes)] for q in range(nacc))

                def step(r, accs):
                    gk = plsc.load_gather(g_v, [jnp.full((lanes,), c * SC_CHUNK + r, jnp.int32)])
                    return tuple(accs[q] + gk * rows[r, pl.ds(f0 + q * lanes, lanes)] for q in range(nacc))
                accs = lax.fori_loop(0, SC_CHUNK, step, init)
                for q in range(nacc):
                    out_v[pl.ds(f0 + q * lanes, lanes)] = accs[q]

        _sc_token_loop(n, eid_hbm, g_hbm, tab_hbm, out_hbm, idx_v, g_v, out_v, (rows0, rows1), (sem0, sem1), compute)

    call = _sc_call(body, n, D_MODEL, [pltpu.VMEM((PICKS,), F32), pltpu.VMEM((D_MODEL,), F32)])
    return call(eid.reshape(n, PICKS // SC_CHUNK, SC_CHUNK), gh, tab)


def _after_spec(after):
    return pl.BlockSpec((SUBLANES, after.shape[1]), lambda i: (0, 0))


def _gate_gelu_kernel(h_ref, g_ref, after_ref, o_ref):
    hid = h_ref[...]
    o_ref[...] = g_ref[...] * (0.5 * hid * (1.0 + lax.erf(hid * (2.0 ** -0.5))))


def _gate_gelu(hid, gate, tb, after):
    t = hid.shape[0]
    return pl.pallas_call(
        _gate_gelu_kernel, out_shape=jax.ShapeDtypeStruct((t, PICKS), F32), grid=(t // tb,),
        in_specs=[_pick_block(tb), _pick_block(tb), _after_spec(after)], out_specs=_pick_block(tb),
        compiler_params=pltpu.CompilerParams(dimension_semantics=("parallel",)),
        name="gate_gelu",
    )(hid, gate, after)


def _final_kernel(x_ref, p_ref, g_ref, *rest):
    rest[-1][...] = _rms(x_ref[...] + p_ref[...], g_ref[...])


def _final(x, p, g, tb, t0, after=None):
    t = p.shape[0]
    b0 = t0 // tb
    blk = pl.BlockSpec((tb, D_MODEL), lambda i: (i, 0))
    extra = [] if after is None else [after]
    return pl.pallas_call(
        _final_kernel, out_shape=jax.ShapeDtypeStruct((t, D_MODEL), F32), grid=(t // tb,),
        in_specs=[pl.BlockSpec((tb, D_MODEL), lambda i: (i + b0, 0)), blk, _const_spec(g.shape)]
        + [_after_spec(a) for a in extra], out_specs=blk,
        compiler_params=pltpu.CompilerParams(dimension_semantics=("parallel",), vmem_limit_bytes=VMEM_LIMIT),
        name="final_norm",
    )(x, p, g, *extra)


def _pack_table(tab):
    bits = lax.bitcast_convert_type(tab.astype(BF16), jnp.uint16).astype(jnp.uint32)
    packed = (bits[HALF_EXPERTS:] << 16) | bits[:HALF_EXPERTS]
    return lax.bitcast_convert_type(packed, jnp.int32).reshape(HALF_EXPERTS * SUBLANES, LANES)


def _col(v, n=LANES):
    return jnp.pad(v.astype(F32), (0, n - v.shape[0])).reshape(1, n)


def _row(v, n=2 * SUBLANES):
    return jnp.pad(v.astype(F32), (0, n - v.shape[0])).reshape(n, 1)


def _peer(x, w, tb_sel, tb_exp, n_sc):
    t = x.shape[0]
    n_tc = t - n_sc
    to_tok = lambda a: a.reshape(PICKS, -1).T
    if n_sc:
        xn_s, tile_s, shift_s, gate_s = _select(x, w["ln_ffn"], w["w_pq"], w["sub_keys"], tb_sel, 0, n_sc)
        eid = to_tok(tile_s) // SUBLANES + jnp.where(to_tok(shift_s) == 0, HALF_EXPERTS, 0)
        hid_s = _sc_hidden(eid, xn_s, w["u_tab"])
    xn, tile, shift, gate = _select(x, w["ln_ffn"], w["w_pq"], w["sub_keys"], tb_sel, n_sc, n_tc,
                                    after=xn_s if n_sc else None)
    tile_t, shift_t = to_tok(tile), to_tok(shift)
    gh = _hidden(tile_t, shift_t, xn.reshape(n_tc, SUBLANES, LANES), to_tok(gate), w["u_pack"], tb_exp)
    if n_sc:
        out_s = _sc_combine(eid, _gate_gelu(hid_s, to_tok(gate_s), tb_sel, after=gh), w["v_tab"])
    out = _combine(tile_t, shift_t, gh, w["v_pack"], tb_exp).reshape(n_tc, D_MODEL)
    y = _final(x, out, w["final_norm"], tb_sel, n_sc)
    if n_sc:
        y = jnp.concatenate([_final(x, out_s, w["final_norm"], tb_sel, 0, after=out), y], axis=0)
    return y


def kernel(x_prompt, x_sample, mem_prompt, cache_conv_a, cache_conv_ssd, state_ssd, cache_mem_k, cache_mem_v, ln_mix_w, w_in, conv_a_w, conv_s_w, conv_s_b, dt_bias, a_log, d_skip, ssd_norm_w, w_out, ln_mem_w, mem_norm_w, w_mq, w_mk, w_mv, w_mo, ln_ffn_w, w_pq, sub_keys, u_tab, v_tab, final_norm_w):
    depth = w_in.shape[0]
    assert depth == 1
    l = 0
    n_main = 3 * D_CONV + D_SSD + XBC_DIM
    w_dt = w_in[l][:, n_main:]
    w = {
        "ln_mix": ln_mix_w[l].reshape(1, D_MODEL),
        "w_in": w_in[l][:, :n_main].astype(BF16),
        "w_dt": jnp.pad(w_dt, ((0, 0), (0, LANES - SSD_HEADS))).astype(BF16),
        "w_dtt": jnp.pad(w_dt.T, ((0, 2 * SUBLANES - SSD_HEADS), (0, 0))).astype(BF16),
        "conv_a_w": conv_a_w[l], "conv_s_w": conv_s_w[l], "conv_s_b": conv_s_b[l].reshape(1, XBC_DIM),
        "dtb_c": _col(dt_bias[l]), "alog_c": _col(a_log[l]), "dsk_c": _col(d_skip[l]),
        "dtb_r": _row(dt_bias[l]), "alog_r": _row(a_log[l]),
        "ssd_norm": ssd_norm_w[l].reshape(1, D_SSD),
        "w_out": w_out[l].astype(BF16),
        "ln_ffn": ln_ffn_w[l].reshape(1, D_MODEL),
        "w_pq": w_pq[l].astype(BF16),
        "sub_keys": sub_keys[l].astype(BF16),
        "u_pack": _pack_table(u_tab[l]),
        "v_pack": _pack_table(v_tab[l]),
        "u_tab": u_tab[l], "v_tab": v_tab[l],
        "final_norm": final_norm_w.reshape(1, D_MODEL),
    }
    ln_mem = ln_mem_w[l].reshape(1, D_MODEL)
    wq, wo = w_mq[l].astype(BF16), w_mo[l].astype(BF16)

    bp, sp, _ = x_prompt.shape
    bs, ss, _ = x_sample.shape

    mk, mv = _memory_kv(mem_prompt.reshape(bp * N_MEM, D_MODEL), mem_norm_w[l].reshape(1, D_MODEL),
                        w_mk[l].astype(BF16), w_mv[l].astype(BF16))
    hp, pa, ps, ph = _mixer(x_prompt, None, w, 256)
    hp = _attn(hp, mk.reshape(bp, N_MEM, D_MODEL), mv.reshape(bp, N_MEM, D_MODEL), ln_mem, wq, wo, 256)
    y_prompt = _peer(hp.reshape(bp * sp, D_MODEL), w, 256, 64, SC_PROMPT_TOKENS).reshape(bp, sp, D_MODEL)

    hs, sa, ssd, sh = _mixer(x_sample, (cache_conv_a[l], cache_conv_ssd[l],
                                        state_ssd[l].reshape((bs,) + PAIR_STATE)), w, 256)
    hs = _attn(hs, cache_mem_k[l].reshape(bs, N_MEM, D_MODEL), cache_mem_v[l].reshape(bs, N_MEM, D_MODEL),
               ln_mem, wq, wo, 256)
    y_sample = _peer(hs.reshape(bs * ss, D_MODEL), w, 256, 64, 0).reshape(bs, ss, D_MODEL)

    kv_shape = (1, bp, N_MEM, MEM_HEADS, MEM_HEAD_DIM)
    st_shape = (SSD_HEADS, SSD_HEAD_DIM, SSD_STATE)
    return (y_prompt, y_sample, pa[None], ps[None], ph.reshape((1, bp) + st_shape), mk.reshape(kv_shape),
            mv.reshape(kv_shape), sa[None], ssd[None], sh.reshape((1, bs) + st_shape))
```

```python
import functools
import math

import jax
import jax.numpy as jnp
from jax import lax
from jax.experimental import pallas as pl
from jax.experimental.pallas import tpu as pltpu
from jax.experimental.pallas import tpu_sc as plsc

D_MODEL = 1024
CHUNK = 64
D_CONV = 512
CONV_A_W = 3
D_SSD = 512
SSD_HEAD_DIM = 64
SSD_HEADS = 8
SSD_GROUPS = 2
SSD_HPG = 4
SSD_STATE = 128
SSD_CONV_W = 4
XBC_DIM = 1024
N_MEM = 256
MEM_HEADS = 4
MEM_HEAD_DIM = 256
PEER_HEADS = 8
N_KEYS = 128
N_EXPERTS = N_KEYS * N_KEYS
PEER_TOPK = 16
PEER_DK = 256
PEER_DK_HALF = 128
PICKS = PEER_HEADS * PEER_TOPK
EPS = 1e-6

LANES = 128
SUBLANES = 8
HALF_EXPERTS = N_EXPERTS // 2
PAIR_STATE = (SSD_HEADS // 2, 2 * SSD_HEAD_DIM, SSD_STATE)
VMEM_LIMIT = 56 * 1024 * 1024

F32 = jnp.float32
BF16 = jnp.bfloat16
HI = lax.Precision.HIGHEST


def _rms(x, g):
    return x * lax.rsqrt(jnp.mean(x * x, axis=-1, keepdims=True) + EPS) * g


def _silu(x):
    return x * (1.0 / (1.0 + jnp.exp(-x)))


def _softplus(x):
    return jnp.maximum(x, 0.0) + jnp.log1p(jnp.exp(-jnp.abs(x)))


def _dot(a, b):
    return jnp.dot(a.astype(BF16), b.astype(BF16), preferred_element_type=F32)


def _dot_nt(a, b):
    return lax.dot_general(a.astype(BF16), b.astype(BF16), (((1,), (1,)), ((), ())),
                           preferred_element_type=F32)


def _dot_tn(a, b):
    return lax.dot_general(a.astype(BF16), b.astype(BF16), (((0,), (0,)), ((), ())),
                           preferred_element_type=F32)


def _const_spec(shape):
    n = len(shape)
    return pl.BlockSpec(shape, lambda *_: (0,) * n)


def _mixer_kernel(has_hist, ts, lc, *refs):
    if has_hist:
        x_ref, hista_ref, hists_ref, h0_ref = refs[:4]
        refs = refs[4:]
    else:
        x_ref = refs[0]
        refs = refs[1:]
    (lnw_ref, win_ref, wdt_ref, wdtt_ref, caw_ref, csw_ref, csb_ref, dtb_c_ref, alog_c_ref, dsk_c_ref,
     dtb_r_ref, alog_r_ref, nw_ref, wout_ref,
     h1_ref, na_ref, ns_ref, nh_ref, cata, cats, hst) = refs
    s = pl.program_id(1)
    pad = SUBLANES

    @pl.when(s == 0)
    def _():
        if has_hist:
            cata[pad - 2:pad, :] = hista_ref[...]
            cats[pad - 3:pad, :] = hists_ref[...]
            hst[...] = h0_ref[...]
        else:
            cata[0:pad, :] = jnp.zeros((pad, D_CONV), F32)
            cats[0:pad, :] = jnp.zeros((pad, XBC_DIM), F32)
            hst[...] = jnp.zeros(hst.shape, F32)

    x = x_ref[...]
    xn = _rms(x, lnw_ref[...]).astype(BF16)
    proj = jnp.dot(xn, win_ref[...], preferred_element_type=F32)
    dt_c = jnp.dot(xn, wdt_ref[...], preferred_element_type=F32)
    dt_r = lax.dot_general(wdtt_ref[...], xn, (((1,), (1,)), ((), ())),
                           preferred_element_type=F32)
    g_b = proj[:, 0:D_CONV]
    g_c = proj[:, D_CONV:2 * D_CONV]
    v_in = proj[:, 2 * D_CONV:3 * D_CONV]
    z = proj[:, 3 * D_CONV:3 * D_CONV + D_SSD]
    xbc = proj[:, 3 * D_CONV + D_SSD:3 * D_CONV + D_SSD + XBC_DIM]

    ua = g_c * v_in
    cata[pad:pad + ts, :] = ua
    caw = caw_ref[...]
    conv_a = (caw[0:1] * cata[pad - 2:pad - 2 + ts, :] + caw[1:2] * cata[pad - 1:pad - 1 + ts, :]
              + caw[2:3] * ua)
    y_a = g_b * conv_a
    hist_a = cata[pad + ts - 2:pad + ts, :]
    na_ref[...] = hist_a
    cata[pad - 2:pad, :] = hist_a

    cats[pad:pad + ts, :] = xbc
    csw = csw_ref[...]
    conv_s = (csw[0:1] * cats[pad - 3:pad - 3 + ts, :] + csw[1:2] * cats[pad - 2:pad - 2 + ts, :]
              + csw[2:3] * cats[pad - 1:pad - 1 + ts, :] + csw[3:4] * xbc)
    hist_s = cats[pad + ts - 3:pad + ts, :]
    ns_ref[...] = hist_s
    cats[pad - 3:pad, :] = hist_s
    xbc_c = _silu(conv_s + csb_ref[...])
    xs = xbc_c[:, 0:D_SSD]

    dtp_c = _softplus(dt_c + dtb_c_ref[...])
    dtp_r = _softplus(dt_r + dtb_r_ref[...])
    a_c = dtp_c * (-jnp.exp(alog_c_ref[...]))
    a_r = dtp_r * (-jnp.exp(alog_r_ref[...]))
    dsk_c = dsk_c_ref[...]

    lane = lax.broadcasted_iota(jnp.int32, (1, LANES), 1)
    first_head = lane < SSD_HEAD_DIM
    row2 = lax.broadcasted_iota(jnp.int32, (2 * SSD_HEAD_DIM, 1), 0) < SSD_HEAD_DIM
    li = lax.broadcasted_iota(jnp.int32, (lc, lc), 0)
    si = lax.broadcasted_iota(jnp.int32, (lc, lc), 1)
    causal = si <= li
    tril = causal.astype(F32)
    jrow = lax.broadcasted_iota(jnp.int32, (ts, lc), 0)
    scol = lax.broadcasted_iota(jnp.int32, (ts, lc), 1)

    def pair(col, h):
        return jnp.where(first_head, col[:, h:h + 1], col[:, h + 1:h + 2])

    y_chunks = []
    for c in range(ts // lc):
        r0 = c * lc
        rows = slice(r0, r0 + lc)
        acum_c = jnp.dot(tril, a_c[rows], preferred_element_type=F32, precision=HI)
        upper = ((jrow >= r0) & (jrow <= r0 + scol)).astype(F32)
        acum_r = jnp.dot(a_r, upper, preferred_element_type=F32, precision=HI)
        last_c = acum_c[lc - 1:lc, :]
        dec_c = jnp.exp(last_c - acum_c)
        eac_c = jnp.exp(acum_c)
        y_pairs = []
        for g in range(SSD_GROUPS):
            b_g = xbc_c[rows, D_SSD + g * SSD_STATE:D_SSD + (g + 1) * SSD_STATE]
            c_g = xbc_c[rows, D_SSD + (SSD_GROUPS + g) * SSD_STATE:D_SSD + (SSD_GROUPS + g + 1) * SSD_STATE]
            cb = _dot_nt(c_g, b_g)
            for q in range(SSD_HPG // 2):
                h = g * SSD_HPG + 2 * q
                pi = h // 2
                xs_p = xs[rows, pi * LANES:(pi + 1) * LANES]
                xdt = xs_p * pair(dtp_c[rows], h)
                res = []
                for hh in (h, h + 1):
                    seg = acum_c[:, hh:hh + 1] - acum_r[hh:hh + 1, :]
                    m_h = cb * jnp.exp(jnp.where(causal, seg, -jnp.inf))
                    res.append(_dot(m_h, xdt))
                y_diag = jnp.where(first_head, res[0], res[1])
                h_in = hst[pi]
                y_off = _dot_nt(c_g, h_in) * pair(eac_c, h)
                st = _dot_tn(xdt * pair(dec_c, h), b_g)
                cd = jnp.exp(jnp.where(row2, last_c[:, h:h + 1], last_c[:, h + 1:h + 2]))
                hst[pi] = cd * h_in + st
                y_pairs.append(y_diag + y_off + pair(dsk_c, h) * xs_p)
        y_chunks.append(jnp.concatenate(y_pairs, axis=1))
    y = y_chunks[0] if len(y_chunks) == 1 else jnp.concatenate(y_chunks, axis=0)
    y_b = _rms(y * _silu(z), nw_ref[...])
    wout = wout_ref[...]
    out = (jnp.dot(y_a.astype(BF16), wout[0:D_CONV], preferred_element_type=F32)
           + jnp.dot(y_b.astype(BF16), wout[D_CONV:], preferred_element_type=F32))
    h1_ref[...] = x + out
    nh_ref[...] = hst[...]


def _mixer(x, hist, w, ts):
    b, s, _ = x.shape
    lc = min(CHUNK, s)
    ts = min(ts, s)
    has_hist = hist is not None
    per_b3 = lambda shape: pl.BlockSpec((None,) + shape, lambda i, j: (i,) + (0,) * len(shape))
    in_specs = [pl.BlockSpec((None, ts, D_MODEL), lambda i, j: (i, j, 0))]
    args = [x]
    if has_hist:
        in_specs += [per_b3((CONV_A_W - 1, D_CONV)), per_b3((SSD_CONV_W - 1, XBC_DIM)),
                     per_b3(PAIR_STATE)]
        args += list(hist)
    wargs = [w["ln_mix"], w["w_in"], w["w_dt"], w["w_dtt"], w["conv_a_w"], w["conv_s_w"], w["conv_s_b"],
             w["dtb_c"], w["alog_c"], w["dsk_c"], w["dtb_r"], w["alog_r"], w["ssd_norm"], w["w_out"]]
    in_specs += [_const_spec(a.shape) for a in wargs]
    args += wargs
    out_shape = (jax.ShapeDtypeStruct((b, s, D_MODEL), F32),
                 jax.ShapeDtypeStruct((b, CONV_A_W - 1, D_CONV), F32),
                 jax.ShapeDtypeStruct((b, SSD_CONV_W - 1, XBC_DIM), F32),
                 jax.ShapeDtypeStruct((b,) + PAIR_STATE, F32))
    out_specs = (pl.BlockSpec((None, ts, D_MODEL), lambda i, j: (i, j, 0)),
                 per_b3((CONV_A_W - 1, D_CONV)), per_b3((SSD_CONV_W - 1, XBC_DIM)),
                 per_b3(PAIR_STATE))
    return pl.pallas_call(
        functools.partial(_mixer_kernel, has_hist, ts, lc),
        out_shape=out_shape, grid=(b, s // ts), in_specs=in_specs, out_specs=out_specs,
        scratch_shapes=[pltpu.VMEM((SUBLANES + ts, D_CONV), F32), pltpu.VMEM((SUBLANES + ts, XBC_DIM), F32),
                        pltpu.VMEM(PAIR_STATE, F32)],
        compiler_params=pltpu.CompilerParams(dimension_semantics=("parallel", "arbitrary"),
                                             vmem_limit_bytes=VMEM_LIMIT),
        name="mixer_hist" if has_hist else "mixer",
    )(*args)


def _memkv_kernel(m_ref, g_ref, wk_ref, wv_ref, k_ref, v_ref):
    mn = _rms(m_ref[...], g_ref[...]).astype(BF16)
    k_ref[...] = jnp.dot(mn, wk_ref[...], preferred_element_type=F32)
    v_ref[...] = jnp.dot(mn, wv_ref[...], preferred_element_type=F32)


def _memory_kv(mem, g, wk, wv, tm=256):
    t = mem.shape[0]
    blk = pl.BlockSpec((tm, D_MODEL), lambda i: (i, 0))
    return pl.pallas_call(
        _memkv_kernel,
        out_shape=(jax.ShapeDtypeStruct((t, D_MODEL), F32),) * 2,
        grid=(t // tm,), in_specs=[blk, _const_spec(g.shape), _const_spec(wk.shape), _const_spec(wv.shape)],
        out_specs=(blk, blk),
        compiler_params=pltpu.CompilerParams(dimension_semantics=("parallel",), vmem_limit_bytes=VMEM_LIMIT),
        name="memory_kv",
    )(mem, g, wk, wv)


def _attn_kernel(x_ref, k_ref, v_ref, g_ref, wq_ref, wo_ref, o_ref):
    x = x_ref[...]
    xn = _rms(x, g_ref[...]).astype(BF16)
    q = jnp.dot(xn, wq_ref[...], preferred_element_type=F32)
    k = k_ref[...]
    v = v_ref[...]
    wo = wo_ref[...]
    acc = x
    for h in range(MEM_HEADS):
        cols = slice(h * MEM_HEAD_DIM, (h + 1) * MEM_HEAD_DIM)
        sc = _dot_nt(q[:, cols], k[:, cols]) * (MEM_HEAD_DIM ** -0.5)
        sc = sc - jnp.max(sc, axis=-1, keepdims=True)
        p = jnp.exp(sc)
        p = p / jnp.sum(p, axis=-1, keepdims=True)
        o_h = _dot(p, v[:, cols])
        acc = acc + jnp.dot(o_h.astype(BF16), wo[cols, :], preferred_element_type=F32)
    o_ref[...] = acc


def _attn(x, k, v, g, wq, wo, ts):
    b, s, _ = x.shape
    ts = min(ts, s)
    xblk = pl.BlockSpec((None, ts, D_MODEL), lambda i, j: (i, j, 0))
    kvblk = pl.BlockSpec((None, N_MEM, D_MODEL), lambda i, j: (i, 0, 0))
    return pl.pallas_call(
        _attn_kernel,
        out_shape=jax.ShapeDtypeStruct((b, s, D_MODEL), F32),
        grid=(b, s // ts),
        in_specs=[xblk, kvblk, kvblk, _const_spec(g.shape), _const_spec(wq.shape), _const_spec(wo.shape)],
        out_specs=xblk,
        compiler_params=pltpu.CompilerParams(dimension_semantics=("parallel", "parallel"),
                                             vmem_limit_bytes=VMEM_LIMIT),
        name="attn",
    )(x, k, v, g, wq, wo)


def _top_k_rows(sc, k, payload=None):
    n = sc.shape[0]
    rows = lax.broadcasted_iota(jnp.int32, sc.shape, 0).astype(F32)
    vals, outs = [], []
    for _ in range(k):
        m = jnp.max(sc, axis=0, keepdims=True)
        cand = jnp.where(sc == m, rows, float(n))
        first = jnp.min(cand, axis=0, keepdims=True)
        hit = cand == first
        sc = jnp.where(hit, -jnp.inf, sc)
        vals.append(m)
        if payload is None:
            outs.append(first.astype(jnp.int32))
        else:
            outs.append(jnp.sum(jnp.where(hit, payload, 0), axis=0, keepdims=True))
    return jnp.concatenate(vals, axis=0), jnp.concatenate(outs, axis=0)


_LIVE_B = tuple(PEER_TOPK // (a + 1) for a in range(1, SUBLANES))


def _pair_candidates(v1, i1, v2, i2):
    sub = lax.broadcasted_iota(jnp.int32, (SUBLANES, v1.shape[1]), 0)
    comb = [v1[0:1] + v2]
    cidx = [i1[0:1] * N_KEYS + i2]
    for a in range(1, SUBLANES):
        comb.append(jnp.where(sub < _LIVE_B[a - 1], v1[a:a + 1] + v2[0:SUBLANES], -jnp.inf))
        cidx.append(i1[a:a + 1] * N_KEYS + i2[0:SUBLANES])
    comb.append(v1[SUBLANES:] + v2[0:1])
    cidx.append(i1[SUBLANES:] * N_KEYS + i2[0:1])
    return jnp.concatenate(comb, axis=0), jnp.concatenate(cidx, axis=0)


def _select_kernel(tb, x_ref, g_ref, wq_ref, keys_ref, *rest):
    xn_ref, eid_ref, tile_ref, shift_ref, gate_ref, eid_s, gate_s = rest[-7:]
    xn = _rms(x_ref[...], g_ref[...])
    xn_ref[...] = xn
    xn = xn.astype(BF16)

    def head(h, carry):
        q = jnp.dot(xn, wq_ref[h], preferred_element_type=F32)
        sc1 = _dot_nt(keys_ref[h, 0], q[:, 0:PEER_DK_HALF])
        sc2 = _dot_nt(keys_ref[h, 1], q[:, PEER_DK_HALF:PEER_DK])
        rows = pl.ds(pl.multiple_of(h * PEER_TOPK, PEER_TOPK), PEER_TOPK)
        for c0 in range(0, tb, LANES):
            cols = slice(c0, c0 + LANES)
            v1, i1 = _top_k_rows(sc1[:, cols], PEER_TOPK)
            v2, i2 = _top_k_rows(sc2[:, cols], PEER_TOPK)
            comb, cidx = _pair_candidates(v1, i1, v2, i2)
            top, eid = _top_k_rows(comb, PEER_TOPK, cidx)
            ex = jnp.exp(top - top[0:1])
            gate_s[rows, cols] = ex / jnp.sum(ex, axis=0, keepdims=True)
            eid_s[rows, cols] = eid
        return carry

    lax.fori_loop(0, PEER_HEADS, head, 0)
    for c0 in range(0, tb, LANES):
        toks = slice(c0, c0 + LANES)
        eid = eid_s[:, toks].T
        upper = eid >= HALF_EXPERTS
        eid_ref[toks, :] = eid
        tile_ref[toks, :] = jnp.where(upper, eid - HALF_EXPERTS, eid) * SUBLANES
        shift_ref[toks, :] = jnp.where(upper, 0, 16)
        gate_ref[toks, :] = gate_s[:, toks].T


def _select(x, g, wq, keys, tb, t0, t, after=None):
    b0 = t0 // tb
    extra = [] if after is None else [after]
    xin = pl.BlockSpec((tb, D_MODEL), lambda i: (i + b0, 0))
    xblk = pl.BlockSpec((tb, D_MODEL), lambda i: (i, 0))
    picks_i = jax.ShapeDtypeStruct((t, PICKS), jnp.int32)
    return pl.pallas_call(
        functools.partial(_select_kernel, tb),
        out_shape=(jax.ShapeDtypeStruct((t, D_MODEL), F32), picks_i, picks_i, picks_i,
                   jax.ShapeDtypeStruct((t, PICKS), F32)),
        grid=(t // tb,),
        in_specs=[xin, _const_spec(g.shape), _const_spec(wq.shape), _const_spec(keys.shape)]
        + [_after_spec(a) for a in extra],
        out_specs=(xblk,) + (_pick_block(tb),) * 4,
        scratch_shapes=[pltpu.VMEM((PICKS, tb), jnp.int32), pltpu.VMEM((PICKS, tb), F32)],
        compiler_params=pltpu.CompilerParams(dimension_semantics=("parallel",), vmem_limit_bytes=VMEM_LIMIT),
        name="peer_select",
    )(x, g, wq, keys, *extra)


HI_HALF_MASK = -(1 << 16)
_BITREV3 = (0, 4, 2, 6, 1, 5, 3, 7)


def _unpack(word, shift):
    return lax.bitcast_convert_type((word << shift) & HI_HALF_MASK, F32)


def _splat_picks(blk, one_pass):
    tb = blk.shape[0]
    r = lax.broadcasted_iota(jnp.int32, (PICKS, LANES), 0)
    c = lax.broadcasted_iota(jnp.int32, (PICKS, LANES), 1)
    eye = (r == c)[None]
    ones = jnp.ones((LANES, LANES), BF16)

    def one(piece):
        diag = jnp.where(eye, piece[:, None, :], 0.0).astype(BF16).reshape(tb * PICKS, LANES)
        return jnp.dot(diag, ones, preferred_element_type=F32)

    if one_pass:
        out = one(blk)
    else:
        hi = blk.astype(BF16).astype(F32)
        rest = blk - hi
        mid = rest.astype(BF16).astype(F32)
        out = one(hi) + one(mid) + one(rest - mid)
    return out.reshape(tb, PICKS, LANES)


def _full(ref, k):
    return jnp.broadcast_to(ref[k:k + 1, :], (SUBLANES, LANES))


def _expert_row(tab_ref, tile_ref, t, k, shifts):
    idx = pl.multiple_of(tile_ref[t, k], SUBLANES)
    return _unpack(tab_ref[pl.ds(idx, SUBLANES), :], _full(shifts, k))


def _hidden_kernel(tb, tile_ref, shift_ref, x_ref, gate_ref, tab_ref, gh_ref, bsh, stage, parts):
    bsh[...] = _splat_picks(shift_ref[...].astype(F32), True).astype(jnp.int32)

    half = PICKS // 2
    quad = SUBLANES // 2
    low = lax.broadcasted_iota(jnp.int32, (SUBLANES, LANES), 0) < quad

    def token(t, carry):
        x = x_ref[t]
        shifts = bsh.at[t]
        for j in range(half):
            a = x * _expert_row(tab_ref, tile_ref, t, j, shifts)
            b = x * _expert_row(tab_ref, tile_ref, t, j + half, shifts)
            stage[j * SUBLANES:(j + 1) * SUBLANES, :] = jnp.where(low, a + pltpu.roll(a, quad, 0),
                                                                  b + pltpu.roll(b, quad, 0))
        for p in range(2):
            part = stage[pl.ds(p * quad, half, stride=SUBLANES), :]
            for s in range(1, quad):
                part = part + stage[pl.ds(p * quad + s, half, stride=SUBLANES), :]
            parts[t, p * half:(p + 1) * half, :] = part
        return carry

    lax.fori_loop(0, tb, token, 0)
    gh_ref[...] = jnp.sum(parts[...], axis=-1)
    hid = gh_ref[...]
    gelu = 0.5 * hid * (1.0 + lax.erf(hid * (2.0 ** -0.5)))
    gh_ref[...] = gate_ref[...] * gelu


def _pick_block(tb, space=None):
    return pl.BlockSpec((tb, PICKS), lambda i: (i, 0), memory_space=space)


def _table_spec(tab):
    return pl.BlockSpec(tab.shape, lambda i: (0, 0), pipeline_mode=pl.Buffered(1))


def _hidden(tile, shift, xg, gate, tab, tb):
    t = xg.shape[0]
    return pl.pallas_call(
        functools.partial(_hidden_kernel, tb),
        out_shape=jax.ShapeDtypeStruct((t, PICKS), F32),
        grid=(t // tb,),
        in_specs=[_pick_block(tb, pltpu.SMEM), _pick_block(tb),
                  pl.BlockSpec((tb, SUBLANES, LANES), lambda i: (i, 0, 0)),
                  _pick_block(tb), _table_spec(tab)],
        out_specs=_pick_block(tb),
        scratch_shapes=[pltpu.VMEM((tb, PICKS, LANES), jnp.int32), pltpu.VMEM((PICKS // 2 * SUBLANES, LANES), F32),
                        pltpu.VMEM((tb, PICKS, LANES), F32)],
        compiler_params=pltpu.CompilerParams(dimension_semantics=("parallel",), vmem_limit_bytes=VMEM_LIMIT),
        name="peer_hidden",
    )(tile, shift, xg, gate, tab)


def _combine_kernel(tb, tile_ref, shift_ref, gh_ref, tab_ref, o_ref, bsh, bg):
    bsh[...] = _splat_picks(shift_ref[...].astype(F32), True).astype(jnp.int32)
    bg[...] = _splat_picks(gh_ref[...], False)

    def token(t, carry):
        shifts, gates = bsh.at[t], bg.at[t]
        acc = [jnp.zeros((SUBLANES, LANES), F32) for _ in range(4)]
        for k in range(PICKS):
            acc[k % 4] = acc[k % 4] + _full(gates, k) * _expert_row(tab_ref, tile_ref, t, k, shifts)
        o_ref[t] = (acc[0] + acc[1]) + (acc[2] + acc[3])
        return carry

    lax.fori_loop(0, tb, token, 0)


def _combine(tile, shift, gh, tab, tb):
    t = gh.shape[0]
    return pl.pallas_call(
        functools.partial(_combine_kernel, tb),
        out_shape=jax.ShapeDtypeStruct((t, SUBLANES, LANES), F32),
        grid=(t // tb,),
        in_specs=[_pick_block(tb, pltpu.SMEM), _pick_block(tb), _pick_block(tb), _table_spec(tab)],
        out_specs=pl.BlockSpec((tb, SUBLANES, LANES), lambda i: (i, 0, 0)),
        scratch_shapes=[pltpu.VMEM((tb, PICKS, LANES), jnp.int32), pltpu.VMEM((tb, PICKS, LANES), F32)],
        compiler_params=pltpu.CompilerParams(dimension_semantics=("parallel",), vmem_limit_bytes=VMEM_LIMIT),
        name="peer_combine",
    )(tile, shift, gh, tab)


SC_LANES = 16
SC_WORKERS = 32
SC_CHUNK = 32
SC_FEATURE_BLOCK = 256
SC_PROMPT_TOKENS = 13312


def _sc_call(body, n, width, scratch):
    mesh = plsc.VectorSubcoreMesh(core_axis_name="c", subcore_axis_name="s")
    return pl.kernel(body, mesh=mesh, out_type=jax.ShapeDtypeStruct((n, width), F32),
                     compiler_params=pltpu.CompilerParams(needs_layout_passes=False),
                     scratch_types=[pltpu.VMEM((PICKS // SC_CHUNK, SC_CHUNK), jnp.int32)] + scratch
                     + [pltpu.VMEM((SC_CHUNK, D_MODEL), F32), pltpu.VMEM((SC_CHUNK, D_MODEL), F32),
                        pltpu.SemaphoreType.DMA, pltpu.SemaphoreType.DMA])


def _sc_token_loop(n, eid_hbm, side_hbm, tab_hbm, out_hbm, idx_v, side_v, out_v, bufs, sems, compute):
    per_worker = n // SC_WORKERS
    wid = lax.axis_index("s") * 2 + lax.axis_index("c")
    chunks = PICKS // SC_CHUNK

    def gather(c):
        return pltpu.make_async_copy(tab_hbm.at[idx_v.at[c]], bufs[c % 2], sems[c % 2])

    @pl.loop(0, per_worker)
    def _(i):
        t = wid * per_worker + i
        pltpu.sync_copy(eid_hbm.at[t], idx_v)
        pltpu.sync_copy(side_hbm.at[t], side_v)
        gather(0).start()
        gather(1).start()
        for c in range(chunks):
            gather(c).wait()
            compute(c, bufs[c % 2])
            if c + 2 < chunks:
                gather(c + 2).start()
        pltpu.sync_copy(out_v, out_hbm.at[t])


def _sc_hidden(eid, x, tab):
    n = eid.shape[0]
    lanes = SC_LANES

    def body(eid_hbm, x_hbm, tab_hbm, out_hbm, idx_v, x_v, hid_v, rows0, rows1, sem0, sem1):
        lane = lax.iota(jnp.int32, lanes)

        def compute(c, rows):
            for g in range(SC_CHUNK // lanes):
                def step(j, accs):
                    off = pl.multiple_of(j * lanes, lanes)
                    xj = x_v[pl.ds(off, lanes)]
                    return tuple(accs[r] + xj * rows[g * lanes + r, pl.ds(off, lanes)] for r in range(lanes))
                accs = lax.fori_loop(0, D_MODEL // lanes, step,
                                     tuple(jnp.zeros((lanes,), F32) for _ in range(lanes)))
                hv = jnp.zeros((lanes,), F32)
                for r in range(lanes):
                    hv = jnp.where(lane == r, jnp.sum(accs[r]), hv)
                hid_v[pl.ds(c * SC_CHUNK + g * lanes, lanes)] = hv

        _sc_token_loop(n, eid_hbm, x_hbm, tab_hbm, out_hbm, idx_v, x_v, hid_v, (rows0, rows1), (sem0, sem1), compute)

    call = _sc_call(body, n, PICKS, [pltpu.VMEM((D_MODEL,), F32), pltpu.VMEM((PICKS,), F32)])
    return call(eid.reshape(n, PICKS // SC_CHUNK, SC_CHUNK), x, tab)


def _sc_combine(eid, gh, tab):
    n = eid.shape[0]
    lanes = SC_LANES
    nacc = SC_FEATURE_BLOCK // lanes

    def body(eid_hbm, g_hbm, tab_hbm, out_hbm, idx_v, g_v, out_v, rows0, rows1, sem0, sem1):
        def compute(c, rows):
            for fb in range(D_MODEL // SC_FEATURE_BLOCK):
                f0 = fb * SC_FEATURE_BLOCK
                if c == 0:
                    init = tuple(jnp.zeros((lanes,), F32) for _ in range(nacc))
                else:
                    init = tuple(out_v[pl.ds(f0 + q * lanes, lanes)] for q in range(nacc))

                def step(r, accs):
                    gk = plsc.load_gather(g_v, [jnp.full((lanes,), c * SC_CHUNK + r, jnp.int32)])
                    return tuple(accs[q] + gk * rows[r, pl.ds(f0 + q * lanes, lanes)] for q in range(nacc))
                accs = lax.fori_loop(0, SC_CHUNK, step, init)
                for q in range(nacc):
                    out_v[pl.ds(f0 + q * lanes, lanes)] = accs[q]

        _sc_token_loop(n, eid_hbm, g_hbm, tab_hbm, out_hbm, idx_v, g_v, out_v, (rows0, rows1), (sem0, sem1), compute)

    call = _sc_call(body, n, D_MODEL, [pltpu.VMEM((PICKS,), F32), pltpu.VMEM((D_MODEL,), F32)])
    return call(eid.reshape(n, PICKS // SC_CHUNK, SC_CHUNK), gh, tab)


def _after_spec(after):
    return pl.BlockSpec((SUBLANES, after.shape[1]), lambda i: (0, 0))


def _gate_gelu_kernel(h_ref, g_ref, after_ref, o_ref):
    hid = h_ref[...]
    o_ref[...] = g_ref[...] * (0.5 * hid * (1.0 + lax.erf(hid * (2.0 ** -0.5))))


def _gate_gelu(hid, gate, tb, after):
    t = hid.shape[0]
    return pl.pallas_call(
        _gate_gelu_kernel, out_shape=jax.ShapeDtypeStruct((t, PICKS), F32), grid=(t // tb,),
        in_specs=[_pick_block(tb), _pick_block(tb), _after_spec(after)], out_specs=_pick_block(tb),
        compiler_params=pltpu.CompilerParams(dimension_semantics=("parallel",)),
        name="gate_gelu",
    )(hid, gate, after)


def _final_kernel(x_ref, p_ref, g_ref, *rest):
    rest[-1][...] = _rms(x_ref[...] + p_ref[...], g_ref[...])


def _final(x, p, g, tb, t0, after=None):
    t = p.shape[0]
    b0 = t0 // tb
    blk = pl.BlockSpec((tb, D_MODEL), lambda i: (i, 0))
    extra = [] if after is None else [after]
    return pl.pallas_call(
        _final_kernel, out_shape=jax.ShapeDtypeStruct((t, D_MODEL), F32), grid=(t // tb,),
        in_specs=[pl.BlockSpec((tb, D_MODEL), lambda i: (i + b0, 0)), blk, _const_spec(g.shape)]
        + [_after_spec(a) for a in extra], out_specs=blk,
        compiler_params=pltpu.CompilerParams(dimension_semantics=("parallel",), vmem_limit_bytes=VMEM_LIMIT),
        name="final_norm",
    )(x, p, g, *extra)


def _pack_table(tab):
    bits = lax.bitcast_convert_type(tab.astype(BF16), jnp.uint16).astype(jnp.uint32)
    packed = (bits[HALF_EXPERTS:] << 16) | bits[:HALF_EXPERTS]
    return lax.bitcast_convert_type(packed, jnp.int32).reshape(HALF_EXPERTS * SUBLANES, LANES)


def _col(v, n=LANES):
    return jnp.pad(v.astype(F32), (0, n - v.shape[0])).reshape(1, n)


def _row(v, n=2 * SUBLANES):
    return jnp.pad(v.astype(F32), (0, n - v.shape[0])).reshape(n, 1)


def _peer(x, w, tb_sel, tb_exp, n_sc):
    t = x.shape[0]
    n_tc = t - n_sc
    if n_sc:
        xn_s, eid_s, _, _, gate_s = _select(x, w["ln_ffn"], w["w_pq"], w["sub_keys"], tb_sel, 0, n_sc)
        hid_s = _sc_hidden(eid_s, xn_s, w["u_tab"])
    xn, _, tile, shift, gate = _select(x, w["ln_ffn"], w["w_pq"], w["sub_keys"], tb_sel, n_sc, n_tc,
                                       after=xn_s if n_sc else None)
    gh = _hidden(tile, shift, xn.reshape(n_tc, SUBLANES, LANES), gate, w["u_pack"], tb_exp)
    if n_sc:
        out_s = _sc_combine(eid_s, _gate_gelu(hid_s, gate_s, tb_sel, after=gh), w["v_tab"])
    out = _combine(tile, shift, gh, w["v_pack"], tb_exp).reshape(n_tc, D_MODEL)
    y = _final(x, out, w["final_norm"], tb_sel, n_sc)
    if n_sc:
        y = jnp.concatenate([_final(x, out_s, w["final_norm"], tb_sel, 0, after=out), y], axis=0)
    return y


def kernel(x_prompt, x_sample, mem_prompt, cache_conv_a, cache_conv_ssd, state_ssd, cache_mem_k, cache_mem_v, ln_mix_w, w_in, conv_a_w, conv_s_w, conv_s_b, dt_bias, a_log, d_skip, ssd_norm_w, w_out, ln_mem_w, mem_norm_w, w_mq, w_mk, w_mv, w_mo, ln_ffn_w, w_pq, sub_keys, u_tab, v_tab, final_norm_w):
    depth = w_in.shape[0]
    assert depth == 1
    l = 0
    n_main = 3 * D_CONV + D_SSD + XBC_DIM
    w_dt = w_in[l][:, n_main:]
    w = {
        "ln_mix": ln_mix_w[l].reshape(1, D_MODEL),
        "w_in": w_in[l][:, :n_main].astype(BF16),
        "w_dt": jnp.pad(w_dt, ((0, 0), (0, LANES - SSD_HEADS))).astype(BF16),
        "w_dtt": jnp.pad(w_dt.T, ((0, 2 * SUBLANES - SSD_HEADS), (0, 0))).astype(BF16),
        "conv_a_w": conv_a_w[l], "conv_s_w": conv_s_w[l], "conv_s_b": conv_s_b[l].reshape(1, XBC_DIM),
        "dtb_c": _col(dt_bias[l]), "alog_c": _col(a_log[l]), "dsk_c": _col(d_skip[l]),
        "dtb_r": _row(dt_bias[l]), "alog_r": _row(a_log[l]),
        "ssd_norm": ssd_norm_w[l].reshape(1, D_SSD),
        "w_out": w_out[l].astype(BF16),
        "ln_ffn": ln_ffn_w[l].reshape(1, D_MODEL),
        "w_pq": w_pq[l].astype(BF16).reshape(D_MODEL, PEER_HEADS, PEER_DK).transpose(1, 0, 2),
        "sub_keys": sub_keys[l].astype(BF16),
        "u_pack": _pack_table(u_tab[l]),
        "v_pack": _pack_table(v_tab[l]),
        "u_tab": u_tab[l], "v_tab": v_tab[l],
        "final_norm": final_norm_w.reshape(1, D_MODEL),
    }
    ln_mem = ln_mem_w[l].reshape(1, D_MODEL)
    wq, wo = w_mq[l].astype(BF16), w_mo[l].astype(BF16)

    bp, sp, _ = x_prompt.shape
    bs, ss, _ = x_sample.shape

    mk, mv = _memory_kv(mem_prompt.reshape(bp * N_MEM, D_MODEL), mem_norm_w[l].reshape(1, D_MODEL),
                        w_mk[l].astype(BF16), w_mv[l].astype(BF16))
    hp, pa, ps, ph = _mixer(x_prompt, None, w, 256)
    hp = _attn(hp, mk.reshape(bp, N_MEM, D_MODEL), mv.reshape(bp, N_MEM, D_MODEL), ln_mem, wq, wo, 256)
    y_prompt = _peer(hp.reshape(bp * sp, D_MODEL), w, 256, 64, SC_PROMPT_TOKENS).reshape(bp, sp, D_MODEL)

    hs, sa, ssd, sh = _mixer(x_sample, (cache_conv_a[l], cache_conv_ssd[l],
                                        state_ssd[l].reshape((bs,) + PAIR_STATE)), w, 256)
    hs = _attn(hs, cache_mem_k[l].reshape(bs, N_MEM, D_MODEL), cache_mem_v[l].reshape(bs, N_MEM, D_MODEL),
               ln_mem, wq, wo, 256)
    y_sample = _peer(hs.reshape(bs * ss, D_MODEL), w, 256, 64, 0).reshape(bs, ss, D_MODEL)

    kv_shape = (1, bp, N_MEM, MEM_HEADS, MEM_HEAD_DIM)
    st_shape = (SSD_HEADS, SSD_HEAD_DIM, SSD_STATE)
    return (y_prompt, y_sample, pa[None], ps[None], ph.reshape((1, bp) + st_shape), mk.reshape(kv_shape),
            mv.reshape(kv_shape), sa[None], ssd[None], sh.reshape((1, bs) + st_shape))
```

```python
import functools
import math

import jax
import jax.numpy as jnp
from jax import lax
from jax.experimental import pallas as pl
from jax.experimental.pallas import tpu as pltpu
from jax.experimental.pallas import tpu_sc as plsc

D_MODEL = 1024
CHUNK = 64
D_CONV = 512
CONV_A_W = 3
D_SSD = 512
SSD_HEAD_DIM = 64
SSD_HEADS = 8
SSD_GROUPS = 2
SSD_HPG = 4
SSD_STATE = 128
SSD_CONV_W = 4
XBC_DIM = 1024
N_MEM = 256
MEM_HEADS = 4
MEM_HEAD_DIM = 256
PEER_HEADS = 8
N_KEYS = 128
N_EXPERTS = N_KEYS * N_KEYS
PEER_TOPK = 16
PEER_DK = 256
PEER_DK_HALF = 128
PICKS = PEER_HEADS * PEER_TOPK
EPS = 1e-6

LANES = 128
SUBLANES = 8
HALF_EXPERTS = N_EXPERTS // 2
PAIR_STATE = (SSD_HEADS // 2, 2 * SSD_HEAD_DIM, SSD_STATE)
VMEM_LIMIT = 56 * 1024 * 1024

F32 = jnp.float32
BF16 = jnp.bfloat16
HI = lax.Precision.HIGHEST


def _rms(x, g):
    return x * lax.rsqrt(jnp.mean(x * x, axis=-1, keepdims=True) + EPS) * g


def _silu(x):
    return x * (1.0 / (1.0 + jnp.exp(-x)))


def _softplus(x):
    return jnp.maximum(x, 0.0) + jnp.log1p(jnp.exp(-jnp.abs(x)))


def _dot(a, b):
    return jnp.dot(a.astype(BF16), b.astype(BF16), preferred_element_type=F32)


def _dot_nt(a, b):
    return lax.dot_general(a.astype(BF16), b.astype(BF16), (((1,), (1,)), ((), ())),
                           preferred_element_type=F32)


def _dot_tn(a, b):
    return lax.dot_general(a.astype(BF16), b.astype(BF16), (((0,), (0,)), ((), ())),
                           preferred_element_type=F32)


def _const_spec(shape):
    n = len(shape)
    return pl.BlockSpec(shape, lambda *_: (0,) * n)


def _mixer_kernel(has_hist, ts, lc, *refs):
    if has_hist:
        x_ref, hista_ref, hists_ref, h0_ref = refs[:4]
        refs = refs[4:]
    else:
        x_ref = refs[0]
        refs = refs[1:]
    (lnw_ref, win_ref, wdt_ref, wdtt_ref, caw_ref, csw_ref, csb_ref, dtb_c_ref, alog_c_ref, dsk_c_ref,
     dtb_r_ref, alog_r_ref, nw_ref, wout_ref,
     h1_ref, na_ref, ns_ref, nh_ref, cata, cats, hst) = refs
    s = pl.program_id(1)
    pad = SUBLANES

    @pl.when(s == 0)
    def _():
        if has_hist:
            cata[pad - 2:pad, :] = hista_ref[...]
            cats[pad - 3:pad, :] = hists_ref[...]
            hst[...] = h0_ref[...]
        else:
            cata[0:pad, :] = jnp.zeros((pad, D_CONV), F32)
            cats[0:pad, :] = jnp.zeros((pad, XBC_DIM), F32)
            hst[...] = jnp.zeros(hst.shape, F32)

    x = x_ref[...]
    xn = _rms(x, lnw_ref[...]).astype(BF16)
    proj = jnp.dot(xn, win_ref[...], preferred_element_type=F32)
    dt_c = jnp.dot(xn, wdt_ref[...], preferred_element_type=F32)
    dt_r = lax.dot_general(wdtt_ref[...], xn, (((1,), (1,)), ((), ())),
                           preferred_element_type=F32)
    g_b = proj[:, 0:D_CONV]
    g_c = proj[:, D_CONV:2 * D_CONV]
    v_in = proj[:, 2 * D_CONV:3 * D_CONV]
    z = proj[:, 3 * D_CONV:3 * D_CONV + D_SSD]
    xbc = proj[:, 3 * D_CONV + D_SSD:3 * D_CONV + D_SSD + XBC_DIM]

    ua = g_c * v_in
    cata[pad:pad + ts, :] = ua
    caw = caw_ref[...]
    conv_a = (caw[0:1] * cata[pad - 2:pad - 2 + ts, :] + caw[1:2] * cata[pad - 1:pad - 1 + ts, :]
              + caw[2:3] * ua)
    y_a = g_b * conv_a
    hist_a = cata[pad + ts - 2:pad + ts, :]
    na_ref[...] = hist_a
    cata[pad - 2:pad, :] = hist_a

    cats[pad:pad + ts, :] = xbc
    csw = csw_ref[...]
    conv_s = (csw[0:1] * cats[pad - 3:pad - 3 + ts, :] + csw[1:2] * cats[pad - 2:pad - 2 + ts, :]
              + csw[2:3] * cats[pad - 1:pad - 1 + ts, :] + csw[3:4] * xbc)
    hist_s = cats[pad + ts - 3:pad + ts, :]
    ns_ref[...] = hist_s
    cats[pad - 3:pad, :] = hist_s
    xbc_c = _silu(conv_s + csb_ref[...])
    xs = xbc_c[:, 0:D_SSD]

    dtp_c = _softplus(dt_c + dtb_c_ref[...])
    dtp_r = _softplus(dt_r + dtb_r_ref[...])
    a_c = dtp_c * (-jnp.exp(alog_c_ref[...]))
    a_r = dtp_r * (-jnp.exp(alog_r_ref[...]))
    dsk_c = dsk_c_ref[...]

    lane = lax.broadcasted_iota(jnp.int32, (1, LANES), 1)
    first_head = lane < SSD_HEAD_DIM
    row2 = lax.broadcasted_iota(jnp.int32, (2 * SSD_HEAD_DIM, 1), 0) < SSD_HEAD_DIM
    li = lax.broadcasted_iota(jnp.int32, (lc, lc), 0)
    si = lax.broadcasted_iota(jnp.int32, (lc, lc), 1)
    causal = si <= li
    tril = causal.astype(F32)
    jrow = lax.broadcasted_iota(jnp.int32, (ts, lc), 0)
    scol = lax.broadcasted_iota(jnp.int32, (ts, lc), 1)

    def pair(col, h):
        return jnp.where(first_head, col[:, h:h + 1], col[:, h + 1:h + 2])

    y_chunks = []
    for c in range(ts // lc):
        r0 = c * lc
        rows = slice(r0, r0 + lc)
        acum_c = jnp.dot(tril, a_c[rows], preferred_element_type=F32, precision=HI)
        upper = ((jrow >= r0) & (jrow <= r0 + scol)).astype(F32)
        acum_r = jnp.dot(a_r, upper, preferred_element_type=F32, precision=HI)
        last_c = acum_c[lc - 1:lc, :]
        dec_c = jnp.exp(last_c - acum_c)
        eac_c = jnp.exp(acum_c)
        y_pairs = []
        for g in range(SSD_GROUPS):
            b_g = xbc_c[rows, D_SSD + g * SSD_STATE:D_SSD + (g + 1) * SSD_STATE]
            c_g = xbc_c[rows, D_SSD + (SSD_GROUPS + g) * SSD_STATE:D_SSD + (SSD_GROUPS + g + 1) * SSD_STATE]
            cb = _dot_nt(c_g, b_g)
            for q in range(SSD_HPG // 2):
                h = g * SSD_HPG + 2 * q
                pi = h // 2
                xs_p = xs[rows, pi * LANES:(pi + 1) * LANES]
                xdt = xs_p * pair(dtp_c[rows], h)
                res = []
                for hh in (h, h + 1):
                    seg = acum_c[:, hh:hh + 1] - acum_r[hh:hh + 1, :]
                    m_h = cb * jnp.exp(jnp.where(causal, seg, -jnp.inf))
                    res.append(_dot(m_h, xdt))
                y_diag = jnp.where(first_head, res[0], res[1])
                h_in = hst[pi]
                y_off = _dot_nt(c_g, h_in) * pair(eac_c, h)
                st = _dot_tn(xdt * pair(dec_c, h), b_g)
                cd = jnp.exp(jnp.where(row2, last_c[:, h:h + 1], last_c[:, h + 1:h + 2]))
                hst[pi] = cd * h_in + st
                y_pairs.append(y_diag + y_off + pair(dsk_c, h) * xs_p)
        y_chunks.append(jnp.concatenate(y_pairs, axis=1))
    y = y_chunks[0] if len(y_chunks) == 1 else jnp.concatenate(y_chunks, axis=0)
    y_b = _rms(y * _silu(z), nw_ref[...])
    wout = wout_ref[...]
    out = (jnp.dot(y_a.astype(BF16), wout[0:D_CONV], preferred_element_type=F32)
           + jnp.dot(y_b.astype(BF16), wout[D_CONV:], preferred_element_type=F32))
    h1_ref[...] = x + out
    nh_ref[...] = hst[...]


def _mixer(x, hist, w, ts):
    b, s, _ = x.shape
    lc = min(CHUNK, s)
    ts = min(ts, s)
    has_hist = hist is not None
    per_b3 = lambda shape: pl.BlockSpec((None,) + shape, lambda i, j: (i,) + (0,) * len(shape))
    in_specs = [pl.BlockSpec((None, ts, D_MODEL), lambda i, j: (i, j, 0))]
    args = [x]
    if has_hist:
        in_specs += [per_b3((CONV_A_W - 1, D_CONV)), per_b3((SSD_CONV_W - 1, XBC_DIM)),
                     per_b3(PAIR_STATE)]
        args += list(hist)
    wargs = [w["ln_mix"], w["w_in"], w["w_dt"], w["w_dtt"], w["conv_a_w"], w["conv_s_w"], w["conv_s_b"],
             w["dtb_c"], w["alog_c"], w["dsk_c"], w["dtb_r"], w["alog_r"], w["ssd_norm"], w["w_out"]]
    in_specs += [_const_spec(a.shape) for a in wargs]
    args += wargs
    out_shape = (jax.ShapeDtypeStruct((b, s, D_MODEL), F32),
                 jax.ShapeDtypeStruct((b, CONV_A_W - 1, D_CONV), F32),
                 jax.ShapeDtypeStruct((b, SSD_CONV_W - 1, XBC_DIM), F32),
                 jax.ShapeDtypeStruct((b,) + PAIR_STATE, F32))
    out_specs = (pl.BlockSpec((None, ts, D_MODEL), lambda i, j: (i, j, 0)),
                 per_b3((CONV_A_W - 1, D_CONV)), per_b3((SSD_CONV_W - 1, XBC_DIM)),
                 per_b3(PAIR_STATE))
    return pl.pallas_call(
        functools.partial(_mixer_kernel, has_hist, ts, lc),
        out_shape=out_shape, grid=(b, s // ts), in_specs=in_specs, out_specs=out_specs,
        scratch_shapes=[pltpu.VMEM((SUBLANES + ts, D_CONV), F32), pltpu.VMEM((SUBLANES + ts, XBC_DIM), F32),
                        pltpu.VMEM(PAIR_STATE, F32)],
        compiler_params=pltpu.CompilerParams(dimension_semantics=("parallel", "arbitrary"),
                                             vmem_limit_bytes=VMEM_LIMIT),
        name="mixer_hist" if has_hist else "mixer",
    )(*args)


def _memkv_kernel(m_ref, g_ref, wk_ref, wv_ref, k_ref, v_ref):
    mn = _rms(m_ref[...], g_ref[...]).astype(BF16)
    k_ref[...] = jnp.dot(mn, wk_ref[...], preferred_element_type=F32)
    v_ref[...] = jnp.dot(mn, wv_ref[...], preferred_element_type=F32)


def _memory_kv(mem, g, wk, wv, tm=256):
    t = mem.shape[0]
    blk = pl.BlockSpec((tm, D_MODEL), lambda i: (i, 0))
    return pl.pallas_call(
        _memkv_kernel,
        out_shape=(jax.ShapeDtypeStruct((t, D_MODEL), F32),) * 2,
        grid=(t // tm,), in_specs=[blk, _const_spec(g.shape), _const_spec(wk.shape), _const_spec(wv.shape)],
        out_specs=(blk, blk),
        compiler_params=pltpu.CompilerParams(dimension_semantics=("parallel",), vmem_limit_bytes=VMEM_LIMIT),
        name="memory_kv",
    )(mem, g, wk, wv)


def _attn_kernel(x_ref, k_ref, v_ref, g_ref, wq_ref, wo_ref, o_ref):
    x = x_ref[...]
    xn = _rms(x, g_ref[...]).astype(BF16)
    q = jnp.dot(xn, wq_ref[...], preferred_element_type=F32)
    k = k_ref[...]
    v = v_ref[...]
    wo = wo_ref[...]
    acc = x
    for h in range(MEM_HEADS):
        cols = slice(h * MEM_HEAD_DIM, (h + 1) * MEM_HEAD_DIM)
        sc = _dot_nt(q[:, cols], k[:, cols]) * (MEM_HEAD_DIM ** -0.5)
        sc = sc - jnp.max(sc, axis=-1, keepdims=True)
        p = jnp.exp(sc)
        p = p / jnp.sum(p, axis=-1, keepdims=True)
        o_h = _dot(p, v[:, cols])
        acc = acc + jnp.dot(o_h.astype(BF16), wo[cols, :], preferred_element_type=F32)
    o_ref[...] = acc


def _attn(x, k, v, g, wq, wo, ts):
    b, s, _ = x.shape
    ts = min(ts, s)
    xblk = pl.BlockSpec((None, ts, D_MODEL), lambda i, j: (i, j, 0))
    kvblk = pl.BlockSpec((None, N_MEM, D_MODEL), lambda i, j: (i, 0, 0))
    return pl.pallas_call(
        _attn_kernel,
        out_shape=jax.ShapeDtypeStruct((b, s, D_MODEL), F32),
        grid=(b, s // ts),
        in_specs=[xblk, kvblk, kvblk, _const_spec(g.shape), _const_spec(wq.shape), _const_spec(wo.shape)],
        out_specs=xblk,
        compiler_params=pltpu.CompilerParams(dimension_semantics=("parallel", "parallel"),
                                             vmem_limit_bytes=VMEM_LIMIT),
        name="attn",
    )(x, k, v, g, wq, wo)


def _top_k_rows(sc, k, payload=None):
    n = sc.shape[0]
    rows = lax.broadcasted_iota(jnp.int32, sc.shape, 0).astype(F32)
    vals, outs = [], []
    for _ in range(k):
        m = jnp.max(sc, axis=0, keepdims=True)
        cand = jnp.where(sc == m, rows, float(n))
        first = jnp.min(cand, axis=0, keepdims=True)
        hit = cand == first
        sc = jnp.where(hit, -jnp.inf, sc)
        vals.append(m)
        if payload is None:
            outs.append(first.astype(jnp.int32))
        else:
            outs.append(jnp.sum(jnp.where(hit, payload, 0), axis=0, keepdims=True))
    return jnp.concatenate(vals, axis=0), jnp.concatenate(outs, axis=0)


_LIVE_B = tuple(PEER_TOPK // (a + 1) for a in range(1, SUBLANES))


def _pair_candidates(v1, i1, v2, i2):
    sub = lax.broadcasted_iota(jnp.int32, (SUBLANES, v1.shape[1]), 0)
    comb = [v1[0:1] + v2]
    cidx = [i1[0:1] * N_KEYS + i2]
    for a in range(1, SUBLANES):
        comb.append(jnp.where(sub < _LIVE_B[a - 1], v1[a:a + 1] + v2[0:SUBLANES], -jnp.inf))
        cidx.append(i1[a:a + 1] * N_KEYS + i2[0:SUBLANES])
    comb.append(v1[SUBLANES:] + v2[0:1])
    cidx.append(i1[SUBLANES:] * N_KEYS + i2[0:1])
    return jnp.concatenate(comb, axis=0), jnp.concatenate(cidx, axis=0)


def _select_kernel(tb, x_ref, g_ref, wq_ref, keys_ref, *rest):
    xn_ref, eid_ref, tile_ref, shift_ref, gate_ref, eid_s, gate_s = rest[-7:]
    xn = _rms(x_ref[...], g_ref[...])
    xn_ref[...] = xn
    xn = xn.astype(BF16)

    def head(h, carry):
        q = jnp.dot(xn, wq_ref[h], preferred_element_type=F32)
        sc1 = _dot_nt(keys_ref[h, 0], q[:, 0:PEER_DK_HALF])
        sc2 = _dot_nt(keys_ref[h, 1], q[:, PEER_DK_HALF:PEER_DK])
        rows = pl.ds(pl.multiple_of(h * PEER_TOPK, PEER_TOPK), PEER_TOPK)
        for c0 in range(0, tb, LANES):
            cols = slice(c0, c0 + LANES)
            v1, i1 = _top_k_rows(sc1[:, cols], PEER_TOPK)
            v2, i2 = _top_k_rows(sc2[:, cols], PEER_TOPK)
            comb, cidx = _pair_candidates(v1, i1, v2, i2)
            top, eid = _top_k_rows(comb, PEER_TOPK, cidx)
            ex = jnp.exp(top - top[0:1])
            gate_s[rows, cols] = ex / jnp.sum(ex, axis=0, keepdims=True)
            eid_s[rows, cols] = eid
        return carry

    lax.fori_loop(0, PEER_HEADS, head, 0)
    for c0 in range(0, tb, LANES):
        toks = slice(c0, c0 + LANES)
        eid = eid_s[:, toks].T
        upper = eid >= HALF_EXPERTS
        eid_ref[toks, :] = eid
        tile_ref[toks, :] = jnp.where(upper, eid - HALF_EXPERTS, eid) * SUBLANES
        shift_ref[toks, :] = jnp.where(upper, 0, 16)
        gate_ref[toks, :] = gate_s[:, toks].T


def _select(x, g, wq, keys, tb, t0, t, after=None):
    b0 = t0 // tb
    extra = [] if after is None else [after]
    xin = pl.BlockSpec((tb, D_MODEL), lambda i: (i + b0, 0))
    xblk = pl.BlockSpec((tb, D_MODEL), lambda i: (i, 0))
    picks_i = jax.ShapeDtypeStruct((t, PICKS), jnp.int32)
    return pl.pallas_call(
        functools.partial(_select_kernel, tb),
        out_shape=(jax.ShapeDtypeStruct((t, D_MODEL), F32), picks_i, picks_i, picks_i,
                   jax.ShapeDtypeStruct((t, PICKS), F32)),
        grid=(t // tb,),
        in_specs=[xin, _const_spec(g.shape), _const_spec(wq.shape), _const_spec(keys.shape)]
        + [_after_spec(a) for a in extra],
        out_specs=(xblk,) + (_pick_block(tb),) * 4,
        scratch_shapes=[pltpu.VMEM((PICKS, tb), jnp.int32), pltpu.VMEM((PICKS, tb), F32)],
        compiler_params=pltpu.CompilerParams(dimension_semantics=("parallel",), vmem_limit_bytes=VMEM_LIMIT),
        name="peer_select",
    )(x, g, wq, keys, *extra)


HI_HALF_MASK = -(1 << 16)
_BITREV3 = (0, 4, 2, 6, 1, 5, 3, 7)


def _unpack(word, shift):
    return lax.bitcast_convert_type((word << shift) & HI_HALF_MASK, F32)


def _splat_picks(blk, one_pass):
    tb = blk.shape[0]
    r = lax.broadcasted_iota(jnp.int32, (PICKS, LANES), 0)
    c = lax.broadcasted_iota(jnp.int32, (PICKS, LANES), 1)
    eye = (r == c)[None]
    ones = jnp.ones((LANES, LANES), BF16)

    def one(piece):
        diag = jnp.where(eye, piece[:, None, :], 0.0).astype(BF16).reshape(tb * PICKS, LANES)
        return jnp.dot(diag, ones, preferred_element_type=F32)

    if one_pass:
        out = one(blk)
    else:
        hi = blk.astype(BF16).astype(F32)
        rest = blk - hi
        mid = rest.astype(BF16).astype(F32)
        out = one(hi) + one(mid) + one(rest - mid)
    return out.reshape(tb, PICKS, LANES)


def _full(ref, k):
    return jnp.broadcast_to(ref[k:k + 1, :], (SUBLANES, LANES))


def _expert_row(tab_ref, tile_ref, t, k, shifts):
    idx = pl.multiple_of(tile_ref[t, k], SUBLANES)
    return _unpack(tab_ref[pl.ds(idx, SUBLANES), :], _full(shifts, k))


def _hidden_kernel(tb, tile_ref, shift_ref, x_ref, gate_ref, tab_ref, gh_ref, bsh, stage, parts):
    bsh[...] = _splat_picks(shift_ref[...].astype(F32), True).astype(jnp.int32)

    half = PICKS // 2
    quad = SUBLANES // 2
    low = lax.broadcasted_iota(jnp.int32, (SUBLANES, LANES), 0) < quad

    def token(t, carry):
        x = x_ref[t]
        shifts = bsh.at[t]
        for j in range(half):
            a = x * _expert_row(tab_ref, tile_ref, t, j, shifts)
            b = x * _expert_row(tab_ref, tile_ref, t, j + half, shifts)
            stage[j * SUBLANES:(j + 1) * SUBLANES, :] = jnp.where(low, a + pltpu.roll(a, quad, 0),
                                                                  b + pltpu.roll(b, quad, 0))
        for p in range(2):
            part = stage[pl.ds(p * quad, half, stride=SUBLANES), :]
            for s in range(1, quad):
                part = part + stage[pl.ds(p * quad + s, half, stride=SUBLANES), :]
            parts[t, p * half:(p + 1) * half, :] = part
        return carry

    lax.fori_loop(0, tb, token, 0)
    gh_ref[...] = jnp.sum(parts[...], axis=-1)
    hid = gh_ref[...]
    gelu = 0.5 * hid * (1.0 + lax.erf(hid * (2.0 ** -0.5)))
    gh_ref[...] = gate_ref[...] * gelu


def _pick_block(tb, space=None):
    return pl.BlockSpec((tb, PICKS), lambda i: (i, 0), memory_space=space)


def _table_spec(tab):
    return pl.BlockSpec(tab.shape, lambda i: (0, 0), pipeline_mode=pl.Buffered(1))


def _hidden(tile, shift, xg, gate, tab, tb):
    t = xg.shape[0]
    return pl.pallas_call(
        functools.partial(_hidden_kernel, tb),
        out_shape=jax.ShapeDtypeStruct((t, PICKS), F32),
        grid=(t // tb,),
        in_specs=[_pick_block(tb, pltpu.SMEM), _pick_block(tb),
                  pl.BlockSpec((tb, SUBLANES, LANES), lambda i: (i, 0, 0)),
                  _pick_block(tb), _table_spec(tab)],
        out_specs=_pick_block(tb),
        scratch_shapes=[pltpu.VMEM((tb, PICKS, LANES), jnp.int32), pltpu.VMEM((PICKS // 2 * SUBLANES, LANES), F32),
                        pltpu.VMEM((tb, PICKS, LANES), F32)],
        compiler_params=pltpu.CompilerParams(dimension_semantics=("parallel",), vmem_limit_bytes=VMEM_LIMIT),
        name="peer_hidden",
    )(tile, shift, xg, gate, tab)


def _combine_kernel(tb, tile_ref, shift_ref, gh_ref, tab_ref, o_ref, bsh, bg):
    bsh[...] = _splat_picks(shift_ref[...].astype(F32), True).astype(jnp.int32)
    bg[...] = _splat_picks(gh_ref[...], False)

    def token(t, carry):
        shifts, gates = bsh.at[t], bg.at[t]
        acc = [jnp.zeros((SUBLANES, LANES), F32) for _ in range(4)]
        for k in range(PICKS):
            acc[k % 4] = acc[k % 4] + _full(gates, k) * _expert_row(tab_ref, tile_ref, t, k, shifts)
        o_ref[t] = (acc[0] + acc[1]) + (acc[2] + acc[3])
        return carry

    lax.fori_loop(0, tb, token, 0)


def _combine(tile, shift, gh, tab, tb):
    t = gh.shape[0]
    return pl.pallas_call(
        functools.partial(_combine_kernel, tb),
        out_shape=jax.ShapeDtypeStruct((t, SUBLANES, LANES), F32),
        grid=(t // tb,),
        in_specs=[_pick_block(tb, pltpu.SMEM), _pick_block(tb), _pick_block(tb), _table_spec(tab)],
        out_specs=pl.BlockSpec((tb, SUBLANES, LANES), lambda i: (i, 0, 0)),
        scratch_shapes=[pltpu.VMEM((tb, PICKS, LANES), jnp.int32), pltpu.VMEM((tb, PICKS, LANES), F32)],
        compiler_params=pltpu.CompilerParams(dimension_semantics=("parallel",), vmem_limit_bytes=VMEM_LIMIT),
        name="peer_combine",
    )(tile, shift, gh, tab)


SC_LANES = 16
SC_WORKERS = 32
SC_CHUNK = 32
SC_FEATURE_BLOCK = 256
SC_PROMPT_TOKENS = 13312


def _sc_call(body, n, width, scratch):
    mesh = plsc.VectorSubcoreMesh(core_axis_name="c", subcore_axis_name="s")
    return pl.kernel(body, mesh=mesh, out_type=jax.ShapeDtypeStruct((n, width), F32),
                     compiler_params=pltpu.CompilerParams(needs_layout_passes=False),
                     scratch_types=[pltpu.VMEM((PICKS,), jnp.int32)] + scratch
                     + [pltpu.VMEM((SC_CHUNK, D_MODEL), F32), pltpu.VMEM((SC_CHUNK, D_MODEL), F32),
                        pltpu.SemaphoreType.DMA, pltpu.SemaphoreType.DMA])


def _sc_token_loop(n, eid_hbm, side_hbm, tab_hbm, out_hbm, idx_v, side_v, out_v, bufs, sems, compute):
    per_worker = n // SC_WORKERS
    wid = lax.axis_index("s") * 2 + lax.axis_index("c")
    chunks = PICKS // SC_CHUNK

    def gather(c):
        ids = idx_v.at[pl.ds(c * SC_CHUNK, SC_CHUNK)]
        return pltpu.make_async_copy(tab_hbm.at[ids], bufs[c % 2], sems[c % 2])

    @pl.loop(0, per_worker)
    def _(i):
        t = wid * per_worker + i
        pltpu.sync_copy(eid_hbm.at[t], idx_v)
        pltpu.sync_copy(side_hbm.at[t], side_v)
        gather(0).start()
        gather(1).start()
        for c in range(chunks):
            gather(c).wait()
            compute(c, bufs[c % 2])
            if c + 2 < chunks:
                gather(c + 2).start()
        pltpu.sync_copy(out_v, out_hbm.at[t])


def _sc_hidden(eid, x, tab):
    n = eid.shape[0]
    lanes = SC_LANES

    def body(eid_hbm, x_hbm, tab_hbm, out_hbm, idx_v, x_v, hid_v, rows0, rows1, sem0, sem1):
        lane = lax.iota(jnp.int32, lanes)

        def compute(c, rows):
            for g in range(SC_CHUNK // lanes):
                def step(j, accs):
                    off = pl.multiple_of(j * lanes, lanes)
                    xj = x_v[pl.ds(off, lanes)]
                    return tuple(accs[r] + xj * rows[g * lanes + r, pl.ds(off, lanes)] for r in range(lanes))
                accs = lax.fori_loop(0, D_MODEL // lanes, step,
                                     tuple(jnp.zeros((lanes,), F32) for _ in range(lanes)))
                hv = jnp.zeros((lanes,), F32)
                for r in range(lanes):
                    hv = jnp.where(lane == r, jnp.sum(accs[r]), hv)
                hid_v[pl.ds(c * SC_CHUNK + g * lanes, lanes)] = hv

        _sc_token_loop(n, eid_hbm, x_hbm, tab_hbm, out_hbm, idx_v, x_v, hid_v, (rows0, rows1), (sem0, sem1), compute)

    call = _sc_call(body, n, PICKS, [pltpu.VMEM((D_MODEL,), F32), pltpu.VMEM((PICKS,), F32)])
    return call(eid, x, tab)


def _sc_combine(eid, gh, tab):
    n = eid.shape[0]
    lanes = SC_LANES
    nacc = SC_FEATURE_BLOCK // lanes

    def body(eid_hbm, g_hbm, tab_hbm, out_hbm, idx_v, g_v, out_v, rows0, rows1, sem0, sem1):
        def compute(c, rows):
            for fb in range(D_MODEL // SC_FEATURE_BLOCK):
                f0 = fb * SC_FEATURE_BLOCK
                if c == 0:
                    init = tuple(jnp.zeros((lanes,), F32) for _ in range(nacc))
                else:
                    init = tuple(out_v[pl.ds(f0 + q * lanes, lanes)] for q in range(nacc))

                def step(r, accs):
                    gk = plsc.load_gather(g_v, [jnp.full((lanes,), c * SC_CHUNK + r, jnp.int32)])
                    return tuple(accs[q] + gk * rows[r, pl.ds(f0 + q * lanes, lanes)] for q in range(nacc))
                accs = lax.fori_loop(0, SC_CHUNK, step, init)
                for q in range(nacc):
                    out_v[pl.ds(f0 + q * lanes, lanes)] = accs[q]

        _sc_token_loop(n, eid_hbm, g_hbm, tab_hbm, out_hbm, idx_v, g_v, out_v, (rows0, rows1), (sem0, sem1), compute)

    call = _sc_call(body, n, D_MODEL, [pltpu.VMEM((PICKS,), F32), pltpu.VMEM((D_MODEL,), F32)])
    return call(eid, gh, tab)


def _after_spec(after):
    return pl.BlockSpec((SUBLANES, after.shape[1]), lambda i: (0, 0))


def _gate_gelu_kernel(h_ref, g_ref, after_ref, o_ref):
    hid = h_ref[...]
    o_ref[...] = g_ref[...] * (0.5 * hid * (1.0 + lax.erf(hid * (2.0 ** -0.5))))


def _gate_gelu(hid, gate, tb, after):
    t = hid.shape[0]
    return pl.pallas_call(
        _gate_gelu_kernel, out_shape=jax.ShapeDtypeStruct((t, PICKS), F32), grid=(t // tb,),
        in_specs=[_pick_block(tb), _pick_block(tb), _after_spec(after)], out_specs=_pick_block(tb),
        compiler_params=pltpu.CompilerParams(dimension_semantics=("parallel",)),
        name="gate_gelu",
    )(hid, gate, after)


def _final_kernel(x_ref, p_ref, g_ref, *rest):
    rest[-1][...] = _rms(x_ref[...] + p_ref[...], g_ref[...])


def _final(x, p, g, tb, t0, after=None):
    t = p.shape[0]
    b0 = t0 // tb
    blk = pl.BlockSpec((tb, D_MODEL), lambda i: (i, 0))
    extra = [] if after is None else [after]
    return pl.pallas_call(
        _final_kernel, out_shape=jax.ShapeDtypeStruct((t, D_MODEL), F32), grid=(t // tb,),
        in_specs=[pl.BlockSpec((tb, D_MODEL), lambda i: (i + b0, 0)), blk, _const_spec(g.shape)]
        + [_after_spec(a) for a in extra], out_specs=blk,
        compiler_params=pltpu.CompilerParams(dimension_semantics=("parallel",), vmem_limit_bytes=VMEM_LIMIT),
        name="final_norm",
    )(x, p, g, *extra)


def _pack_table(tab):
    bits = lax.bitcast_convert_type(tab.astype(BF16), jnp.uint16).astype(jnp.uint32)
    packed = (bits[HALF_EXPERTS:] << 16) | bits[:HALF_EXPERTS]
    return lax.bitcast_convert_type(packed, jnp.int32).reshape(HALF_EXPERTS * SUBLANES, LANES)


def _col(v, n=LANES):
    return jnp.pad(v.astype(F32), (0, n - v.shape[0])).reshape(1, n)


def _row(v, n=2 * SUBLANES):
    return jnp.pad(v.astype(F32), (0, n - v.shape[0])).reshape(n, 1)


def _peer(x, w, tb_sel, tb_exp, n_sc):
    t = x.shape[0]
    n_tc = t - n_sc
    if n_sc:
        xn_s, eid_s, _, _, gate_s = _select(x, w["ln_ffn"], w["w_pq"], w["sub_keys"], tb_sel, 0, n_sc)
        hid_s = _sc_hidden(eid_s, xn_s, w["u_tab"])
    xn, _, tile, shift, gate = _select(x, w["ln_ffn"], w["w_pq"], w["sub_keys"], tb_sel, n_sc, n_tc,
                                       after=xn_s if n_sc else None)
    gh = _hidden(tile, shift, xn.reshape(n_tc, SUBLANES, LANES), gate, w["u_pack"], tb_exp)
    if n_sc:
        out_s = _sc_combine(eid_s, _gate_gelu(hid_s, gate_s, tb_sel, after=gh), w["v_tab"])
    out = _combine(tile, shift, gh, w["v_pack"], tb_exp).reshape(n_tc, D_MODEL)
    y = _final(x, out, w["final_norm"], tb_sel, n_sc)
    if n_sc:
        y = jnp.concatenate([_final(x, out_s, w["final_norm"], tb_sel, 0, after=out), y], axis=0)
    return y


def kernel(x_prompt, x_sample, mem_prompt, cache_conv_a, cache_conv_ssd, state_ssd, cache_mem_k, cache_mem_v, ln_mix_w, w_in, conv_a_w, conv_s_w, conv_s_b, dt_bias, a_log, d_skip, ssd_norm_w, w_out, ln_mem_w, mem_norm_w, w_mq, w_mk, w_mv, w_mo, ln_ffn_w, w_pq, sub_keys, u_tab, v_tab, final_norm_w):
    depth = w_in.shape[0]
    assert depth == 1
    l = 0
    n_main = 3 * D_CONV + D_SSD + XBC_DIM
    w_dt = w_in[l][:, n_main:]
    w = {
        "ln_mix": ln_mix_w[l].reshape(1, D_MODEL),
        "w_in": w_in[l][:, :n_main].astype(BF16),
        "w_dt": jnp.pad(w_dt, ((0, 0), (0, LANES - SSD_HEADS))).astype(BF16),
        "w_dtt": jnp.pad(w_dt.T, ((0, 2 * SUBLANES - SSD_HEADS), (0, 0))).astype(BF16),
        "conv_a_w": conv_a_w[l], "conv_s_w": conv_s_w[l], "conv_s_b": conv_s_b[l].reshape(1, XBC_DIM),
        "dtb_c": _col(dt_bias[l]), "alog_c": _col(a_log[l]), "dsk_c": _col(d_skip[l]),
        "dtb_r": _row(dt_bias[l]), "alog_r": _row(a_log[l]),
        "ssd_norm": ssd_norm_w[l].reshape(1, D_SSD),
        "w_out": w_out[l].astype(BF16),
        "ln_ffn": ln_ffn_w[l].reshape(1, D_MODEL),
        "w_pq": w_pq[l].astype(BF16).reshape(D_MODEL, PEER_HEADS, PEER_DK).transpose(1, 0, 2),
        "sub_keys": sub_keys[l].astype(BF16),
        "u_pack": _pack_table(u_tab[l]),
        "v_pack": _pack_table(v_tab[l]),
        "u_tab": u_tab[l], "v_tab": v_tab[l],
        "final_norm": final_norm_w.reshape(1, D_MODEL),
    }
    ln_mem = ln_mem_w[l].reshape(1, D_MODEL)
    wq, wo = w_mq[l].astype(BF16), w_mo[l].astype(BF16)

    bp, sp, _ = x_prompt.shape
    bs, ss, _ = x_sample.shape

    mk, mv = _memory_kv(mem_prompt.reshape(bp * N_MEM, D_MODEL), mem_norm_w[l].reshape(1, D_MODEL),
                        w_mk[l].astype(BF16), w_mv[l].astype(BF16))
    hp, pa, ps, ph = _mixer(x_prompt, None, w, 256)
    hp = _attn(hp, mk.reshape(bp, N_MEM, D_MODEL), mv.reshape(bp, N_MEM, D_MODEL), ln_mem, wq, wo, 256)
    y_prompt = _peer(hp.reshape(bp * sp, D_MODEL), w, 256, 64, SC_PROMPT_TOKENS).reshape(bp, sp, D_MODEL)

    hs, sa, ssd, sh = _mixer(x_sample, (cache_conv_a[l], cache_conv_ssd[l],
                                        state_ssd[l].reshape((bs,) + PAIR_STATE)), w, 256)
    hs = _attn(hs, cache_mem_k[l].reshape(bs, N_MEM, D_MODEL), cache_mem_v[l].reshape(bs, N_MEM, D_MODEL),
               ln_mem, wq, wo, 256)
    y_sample = _peer(hs.reshape(bs * ss, D_MODEL), w, 256, 64, 0).reshape(bs, ss, D_MODEL)

    kv_shape = (1, bp, N_MEM, MEM_HEADS, MEM_HEAD_DIM)
    st_shape = (SSD_HEADS, SSD_HEAD_DIM, SSD_STATE)
    return (y_prompt, y_sample, pa[None], ps[None], ph.reshape((1, bp) + st_shape), mk.reshape(kv_shape),
            mv.reshape(kv_shape), sa[None], ssd[None], sh.reshape((1, bs) + st_shape))
```

```python
import functools
import math

import jax
import jax.numpy as jnp
from jax import lax
from jax.experimental import pallas as pl
from jax.experimental.pallas import tpu as pltpu
from jax.experimental.pallas import tpu_sc as plsc

D_MODEL = 1024
CHUNK = 64
D_CONV = 512
CONV_A_W = 3
D_SSD = 512
SSD_HEAD_DIM = 64
SSD_HEADS = 8
SSD_GROUPS = 2
SSD_HPG = 4
SSD_STATE = 128
SSD_CONV_W = 4
XBC_DIM = 1024
N_MEM = 256
MEM_HEADS = 4
MEM_HEAD_DIM = 256
PEER_HEADS = 8
N_KEYS = 128
N_EXPERTS = N_KEYS * N_KEYS
PEER_TOPK = 16
PEER_DK = 256
PEER_DK_HALF = 128
PICKS = PEER_HEADS * PEER_TOPK
EPS = 1e-6

LANES = 128
SUBLANES = 8
HALF_EXPERTS = N_EXPERTS // 2
PAIR_STATE = (SSD_HEADS // 2, 2 * SSD_HEAD_DIM, SSD_STATE)
VMEM_LIMIT = 56 * 1024 * 1024

F32 = jnp.float32
BF16 = jnp.bfloat16
HI = lax.Precision.HIGHEST


def _rms(x, g):
    return x * lax.rsqrt(jnp.mean(x * x, axis=-1, keepdims=True) + EPS) * g


def _silu(x):
    return x * (1.0 / (1.0 + jnp.exp(-x)))


def _softplus(x):
    return jnp.maximum(x, 0.0) + jnp.log1p(jnp.exp(-jnp.abs(x)))


def _dot(a, b):
    return jnp.dot(a.astype(BF16), b.astype(BF16), preferred_element_type=F32)


def _dot_nt(a, b):
    return lax.dot_general(a.astype(BF16), b.astype(BF16), (((1,), (1,)), ((), ())),
                           preferred_element_type=F32)


def _dot_tn(a, b):
    return lax.dot_general(a.astype(BF16), b.astype(BF16), (((0,), (0,)), ((), ())),
                           preferred_element_type=F32)


def _const_spec(shape):
    n = len(shape)
    return pl.BlockSpec(shape, lambda *_: (0,) * n)


def _mixer_kernel(has_hist, ts, lc, *refs):
    if has_hist:
        x_ref, hista_ref, hists_ref, h0_ref = refs[:4]
        refs = refs[4:]
    else:
        x_ref = refs[0]
        refs = refs[1:]
    (lnw_ref, win_ref, wdt_ref, wdtt_ref, caw_ref, csw_ref, csb_ref, dtb_c_ref, alog_c_ref, dsk_c_ref,
     dtb_r_ref, alog_r_ref, nw_ref, wout_ref,
     h1_ref, na_ref, ns_ref, nh_ref, cata, cats, hst) = refs
    s = pl.program_id(1)
    pad = SUBLANES

    @pl.when(s == 0)
    def _():
        if has_hist:
            cata[pad - 2:pad, :] = hista_ref[...]
            cats[pad - 3:pad, :] = hists_ref[...]
            hst[...] = h0_ref[...]
        else:
            cata[0:pad, :] = jnp.zeros((pad, D_CONV), F32)
            cats[0:pad, :] = jnp.zeros((pad, XBC_DIM), F32)
            hst[...] = jnp.zeros(hst.shape, F32)

    x = x_ref[...]
    xn = _rms(x, lnw_ref[...]).astype(BF16)
    proj = jnp.dot(xn, win_ref[...], preferred_element_type=F32)
    dt_c = jnp.dot(xn, wdt_ref[...], preferred_element_type=F32)
    dt_r = lax.dot_general(wdtt_ref[...], xn, (((1,), (1,)), ((), ())),
                           preferred_element_type=F32)
    g_b = proj[:, 0:D_CONV]
    g_c = proj[:, D_CONV:2 * D_CONV]
    v_in = proj[:, 2 * D_CONV:3 * D_CONV]
    z = proj[:, 3 * D_CONV:3 * D_CONV + D_SSD]
    xbc = proj[:, 3 * D_CONV + D_SSD:3 * D_CONV + D_SSD + XBC_DIM]

    ua = g_c * v_in
    cata[pad:pad + ts, :] = ua
    caw = caw_ref[...]
    conv_a = (caw[0:1] * cata[pad - 2:pad - 2 + ts, :] + caw[1:2] * cata[pad - 1:pad - 1 + ts, :]
              + caw[2:3] * ua)
    y_a = g_b * conv_a
    hist_a = cata[pad + ts - 2:pad + ts, :]
    na_ref[...] = hist_a
    cata[pad - 2:pad, :] = hist_a

    cats[pad:pad + ts, :] = xbc
    csw = csw_ref[...]
    conv_s = (csw[0:1] * cats[pad - 3:pad - 3 + ts, :] + csw[1:2] * cats[pad - 2:pad - 2 + ts, :]
              + csw[2:3] * cats[pad - 1:pad - 1 + ts, :] + csw[3:4] * xbc)
    hist_s = cats[pad + ts - 3:pad + ts, :]
    ns_ref[...] = hist_s
    cats[pad - 3:pad, :] = hist_s
    xbc_c = _silu(conv_s + csb_ref[...])
    xs = xbc_c[:, 0:D_SSD]

    dtp_c = _softplus(dt_c + dtb_c_ref[...])
    dtp_r = _softplus(dt_r + dtb_r_ref[...])
    a_c = dtp_c * (-jnp.exp(alog_c_ref[...]))
    a_r = dtp_r * (-jnp.exp(alog_r_ref[...]))
    dsk_c = dsk_c_ref[...]

    lane = lax.broadcasted_iota(jnp.int32, (1, LANES), 1)
    first_head = lane < SSD_HEAD_DIM
    row2 = lax.broadcasted_iota(jnp.int32, (2 * SSD_HEAD_DIM, 1), 0) < SSD_HEAD_DIM
    li = lax.broadcasted_iota(jnp.int32, (lc, lc), 0)
    si = lax.broadcasted_iota(jnp.int32, (lc, lc), 1)
    causal = si <= li
    tril = causal.astype(F32)
    jrow = lax.broadcasted_iota(jnp.int32, (ts, lc), 0)
    scol = lax.broadcasted_iota(jnp.int32, (ts, lc), 1)

    def pair(col, h):
        return jnp.where(first_head, col[:, h:h + 1], col[:, h + 1:h + 2])

    y_chunks = []
    for c in range(ts // lc):
        r0 = c * lc
        rows = slice(r0, r0 + lc)
        acum_c = jnp.dot(tril, a_c[rows], preferred_element_type=F32, precision=HI)
        upper = ((jrow >= r0) & (jrow <= r0 + scol)).astype(F32)
        acum_r = jnp.dot(a_r, upper, preferred_element_type=F32, precision=HI)
        last_c = acum_c[lc - 1:lc, :]
        dec_c = jnp.exp(last_c - acum_c)
        eac_c = jnp.exp(acum_c)
        y_pairs = []
        for g in range(SSD_GROUPS):
            b_g = xbc_c[rows, D_SSD + g * SSD_STATE:D_SSD + (g + 1) * SSD_STATE]
            c_g = xbc_c[rows, D_SSD + (SSD_GROUPS + g) * SSD_STATE:D_SSD + (SSD_GROUPS + g + 1) * SSD_STATE]
            cb = _dot_nt(c_g, b_g)
            for q in range(SSD_HPG // 2):
                h = g * SSD_HPG + 2 * q
                pi = h // 2
                xs_p = xs[rows, pi * LANES:(pi + 1) * LANES]
                xdt = xs_p * pair(dtp_c[rows], h)
                res = []
                for hh in (h, h + 1):
                    seg = acum_c[:, hh:hh + 1] - acum_r[hh:hh + 1, :]
                    m_h = cb * jnp.exp(jnp.where(causal, seg, -jnp.inf))
                    res.append(_dot(m_h, xdt))
                y_diag = jnp.where(first_head, res[0], res[1])
                h_in = hst[pi]
                y_off = _dot_nt(c_g, h_in) * pair(eac_c, h)
                st = _dot_tn(xdt * pair(dec_c, h), b_g)
                cd = jnp.exp(jnp.where(row2, last_c[:, h:h + 1], last_c[:, h + 1:h + 2]))
                hst[pi] = cd * h_in + st
                y_pairs.append(y_diag + y_off + pair(dsk_c, h) * xs_p)
        y_chunks.append(jnp.concatenate(y_pairs, axis=1))
    y = y_chunks[0] if len(y_chunks) == 1 else jnp.concatenate(y_chunks, axis=0)
    y_b = _rms(y * _silu(z), nw_ref[...])
    wout = wout_ref[...]
    out = (jnp.dot(y_a.astype(BF16), wout[0:D_CONV], preferred_element_type=F32)
           + jnp.dot(y_b.astype(BF16), wout[D_CONV:], preferred_element_type=F32))
    h1_ref[...] = x + out
    nh_ref[...] = hst[...]


def _mixer(x, hist, w, ts):
    b, s, _ = x.shape
    lc = min(CHUNK, s)
    ts = min(ts, s)
    has_hist = hist is not None
    per_b3 = lambda shape: pl.BlockSpec((None,) + shape, lambda i, j: (i,) + (0,) * len(shape))
    in_specs = [pl.BlockSpec((None, ts, D_MODEL), lambda i, j: (i, j, 0))]
    args = [x]
    if has_hist:
        in_specs += [per_b3((CONV_A_W - 1, D_CONV)), per_b3((SSD_CONV_W - 1, XBC_DIM)),
                     per_b3(PAIR_STATE)]
        args += list(hist)
    wargs = [w["ln_mix"], w["w_in"], w["w_dt"], w["w_dtt"], w["conv_a_w"], w["conv_s_w"], w["conv_s_b"],
             w["dtb_c"], w["alog_c"], w["dsk_c"], w["dtb_r"], w["alog_r"], w["ssd_norm"], w["w_out"]]
    in_specs += [_const_spec(a.shape) for a in wargs]
    args += wargs
    out_shape = (jax.ShapeDtypeStruct((b, s, D_MODEL), F32),
                 jax.ShapeDtypeStruct((b, CONV_A_W - 1, D_CONV), F32),
                 jax.ShapeDtypeStruct((b, SSD_CONV_W - 1, XBC_DIM), F32),
                 jax.ShapeDtypeStruct((b,) + PAIR_STATE, F32))
    out_specs = (pl.BlockSpec((None, ts, D_MODEL), lambda i, j: (i, j, 0)),
                 per_b3((CONV_A_W - 1, D_CONV)), per_b3((SSD_CONV_W - 1, XBC_DIM)),
                 per_b3(PAIR_STATE))
    return pl.pallas_call(
        functools.partial(_mixer_kernel, has_hist, ts, lc),
        out_shape=out_shape, grid=(b, s // ts), in_specs=in_specs, out_specs=out_specs,
        scratch_shapes=[pltpu.VMEM((SUBLANES + ts, D_CONV), F32), pltpu.VMEM((SUBLANES + ts, XBC_DIM), F32),
                        pltpu.VMEM(PAIR_STATE, F32)],
        compiler_params=pltpu.CompilerParams(dimension_semantics=("parallel", "arbitrary"),
                                             vmem_limit_bytes=VMEM_LIMIT),
        name="mixer_hist" if has_hist else "mixer",
    )(*args)


def _memkv_kernel(m_ref, g_ref, wk_ref, wv_ref, k_ref, v_ref):
    mn = _rms(m_ref[...], g_ref[...]).astype(BF16)
    k_ref[...] = jnp.dot(mn, wk_ref[...], preferred_element_type=F32)
    v_ref[...] = jnp.dot(mn, wv_ref[...], preferred_element_type=F32)


def _memory_kv(mem, g, wk, wv, tm=256):
    t = mem.shape[0]
    blk = pl.BlockSpec((tm, D_MODEL), lambda i: (i, 0))
    return pl.pallas_call(
        _memkv_kernel,
        out_shape=(jax.ShapeDtypeStruct((t, D_MODEL), F32),) * 2,
        grid=(t // tm,), in_specs=[blk, _const_spec(g.shape), _const_spec(wk.shape), _const_spec(wv.shape)],
        out_specs=(blk, blk),
        compiler_params=pltpu.CompilerParams(dimension_semantics=("parallel",), vmem_limit_bytes=VMEM_LIMIT),
        name="memory_kv",
    )(mem, g, wk, wv)


def _attn_kernel(x_ref, k_ref, v_ref, g_ref, wq_ref, wo_ref, o_ref):
    x = x_ref[...]
    xn = _rms(x, g_ref[...]).astype(BF16)
    q = jnp.dot(xn, wq_ref[...], preferred_element_type=F32)
    k = k_ref[...]
    v = v_ref[...]
    wo = wo_ref[...]
    acc = x
    for h in range(MEM_HEADS):
        cols = slice(h * MEM_HEAD_DIM, (h + 1) * MEM_HEAD_DIM)
        sc = _dot_nt(q[:, cols], k[:, cols]) * (MEM_HEAD_DIM ** -0.5)
        sc = sc - jnp.max(sc, axis=-1, keepdims=True)
        p = jnp.exp(sc)
        p = p / jnp.sum(p, axis=-1, keepdims=True)
        o_h = _dot(p, v[:, cols])
        acc = acc + jnp.dot(o_h.astype(BF16), wo[cols, :], preferred_element_type=F32)
    o_ref[...] = acc


def _attn(x, k, v, g, wq, wo, ts):
    b, s, _ = x.shape
    ts = min(ts, s)
    xblk = pl.BlockSpec((None, ts, D_MODEL), lambda i, j: (i, j, 0))
    kvblk = pl.BlockSpec((None, N_MEM, D_MODEL), lambda i, j: (i, 0, 0))
    return pl.pallas_call(
        _attn_kernel,
        out_shape=jax.ShapeDtypeStruct((b, s, D_MODEL), F32),
        grid=(b, s // ts),
        in_specs=[xblk, kvblk, kvblk, _const_spec(g.shape), _const_spec(wq.shape), _const_spec(wo.shape)],
        out_specs=xblk,
        compiler_params=pltpu.CompilerParams(dimension_semantics=("parallel", "parallel"),
                                             vmem_limit_bytes=VMEM_LIMIT),
        name="attn",
    )(x, k, v, g, wq, wo)


def _top_k_rows(sc, k, payload=None):
    n = sc.shape[0]
    rows = lax.broadcasted_iota(jnp.int32, sc.shape, 0).astype(F32)
    vals, outs = [], []
    for _ in range(k):
        m = jnp.max(sc, axis=0, keepdims=True)
        cand = jnp.where(sc == m, rows, float(n))
        first = jnp.min(cand, axis=0, keepdims=True)
        hit = cand == first
        sc = jnp.where(hit, -jnp.inf, sc)
        vals.append(m)
        if payload is None:
            outs.append(first.astype(jnp.int32))
        else:
            outs.append(jnp.sum(jnp.where(hit, payload, 0), axis=0, keepdims=True))
    return jnp.concatenate(vals, axis=0), jnp.concatenate(outs, axis=0)


_LIVE_B = tuple(PEER_TOPK // (a + 1) for a in range(1, SUBLANES))


def _pair_candidates(v1, i1, v2, i2):
    sub = lax.broadcasted_iota(jnp.int32, (SUBLANES, v1.shape[1]), 0)
    comb = [v1[0:1] + v2]
    cidx = [i1[0:1] * N_KEYS + i2]
    for a in range(1, SUBLANES):
        comb.append(jnp.where(sub < _LIVE_B[a - 1], v1[a:a + 1] + v2[0:SUBLANES], -jnp.inf))
        cidx.append(i1[a:a + 1] * N_KEYS + i2[0:SUBLANES])
    comb.append(v1[SUBLANES:] + v2[0:1])
    cidx.append(i1[SUBLANES:] * N_KEYS + i2[0:1])
    return jnp.concatenate(comb, axis=0), jnp.concatenate(cidx, axis=0)


def _select_kernel(tb, x_ref, g_ref, wq_ref, keys_ref, *rest):
    xn_ref, eid_ref, tile_ref, shift_ref, gate_ref, eid_s, gate_s = rest[-7:]
    xn = _rms(x_ref[...], g_ref[...])
    xn_ref[...] = xn
    xn = xn.astype(BF16)

    def head(h, carry):
        q = jnp.dot(xn, wq_ref[h], preferred_element_type=F32)
        sc1 = _dot_nt(keys_ref[h, 0], q[:, 0:PEER_DK_HALF])
        sc2 = _dot_nt(keys_ref[h, 1], q[:, PEER_DK_HALF:PEER_DK])
        rows = pl.ds(pl.multiple_of(h * PEER_TOPK, PEER_TOPK), PEER_TOPK)
        for c0 in range(0, tb, LANES):
            cols = slice(c0, c0 + LANES)
            v1, i1 = _top_k_rows(sc1[:, cols], PEER_TOPK)
            v2, i2 = _top_k_rows(sc2[:, cols], PEER_TOPK)
            comb, cidx = _pair_candidates(v1, i1, v2, i2)
            top, eid = _top_k_rows(comb, PEER_TOPK, cidx)
            ex = jnp.exp(top - top[0:1])
            gate_s[rows, cols] = ex / jnp.sum(ex, axis=0, keepdims=True)
            eid_s[rows, cols] = eid
        return carry

    lax.fori_loop(0, PEER_HEADS, head, 0)
    for c0 in range(0, tb, LANES):
        toks = slice(c0, c0 + LANES)
        eid = eid_s[:, toks].T
        upper = eid >= HALF_EXPERTS
        eid_ref[toks, :] = eid
        tile_ref[toks, :] = jnp.where(upper, eid - HALF_EXPERTS, eid) * SUBLANES
        shift_ref[toks, :] = jnp.where(upper, 0, 16)
        gate_ref[toks, :] = gate_s[:, toks].T


def _select(x, g, wq, keys, tb, t0, t, after=None):
    b0 = t0 // tb
    extra = [] if after is None else [after]
    xin = pl.BlockSpec((tb, D_MODEL), lambda i: (i + b0, 0))
    xblk = pl.BlockSpec((tb, D_MODEL), lambda i: (i, 0))
    picks_i = jax.ShapeDtypeStruct((t, PICKS), jnp.int32)
    return pl.pallas_call(
        functools.partial(_select_kernel, tb),
        out_shape=(jax.ShapeDtypeStruct((t, D_MODEL), F32), picks_i, picks_i, picks_i,
                   jax.ShapeDtypeStruct((t, PICKS), F32)),
        grid=(t // tb,),
        in_specs=[xin, _const_spec(g.shape), _const_spec(wq.shape), _const_spec(keys.shape)]
        + [_after_spec(a) for a in extra],
        out_specs=(xblk,) + (_pick_block(tb),) * 4,
        scratch_shapes=[pltpu.VMEM((PICKS, tb), jnp.int32), pltpu.VMEM((PICKS, tb), F32)],
        compiler_params=pltpu.CompilerParams(dimension_semantics=("parallel",), vmem_limit_bytes=VMEM_LIMIT),
        name="peer_select",
    )(x, g, wq, keys, *extra)


HI_HALF_MASK = -(1 << 16)
_BITREV3 = (0, 4, 2, 6, 1, 5, 3, 7)


def _unpack(word, shift):
    return lax.bitcast_convert_type((word << shift) & HI_HALF_MASK, F32)


def _splat_picks(blk, one_pass):
    tb = blk.shape[0]
    r = lax.broadcasted_iota(jnp.int32, (PICKS, LANES), 0)
    c = lax.broadcasted_iota(jnp.int32, (PICKS, LANES), 1)
    eye = (r == c)[None]
    ones = jnp.ones((LANES, LANES), BF16)

    def one(piece):
        diag = jnp.where(eye, piece[:, None, :], 0.0).astype(BF16).reshape(tb * PICKS, LANES)
        return jnp.dot(diag, ones, preferred_element_type=F32)

    if one_pass:
        out = one(blk)
    else:
        hi = blk.astype(BF16).astype(F32)
        rest = blk - hi
        mid = rest.astype(BF16).astype(F32)
        out = one(hi) + one(mid) + one(rest - mid)
    return out.reshape(tb, PICKS, LANES)


def _full(ref, k):
    return jnp.broadcast_to(ref[k:k + 1, :], (SUBLANES, LANES))


def _expert_row(tab_ref, tile_ref, t, k, shifts):
    idx = pl.multiple_of(tile_ref[t, k], SUBLANES)
    return _unpack(tab_ref[pl.ds(idx, SUBLANES), :], _full(shifts, k))


def _hidden_kernel(tb, tile_ref, shift_ref, x_ref, gate_ref, tab_ref, gh_ref, bsh, stage, parts, xt):
    bsh[...] = _splat_picks(shift_ref[...].astype(F32), True).astype(jnp.int32)
    for s in range(SUBLANES):
        xt[pl.ds(s, tb, stride=SUBLANES), :] = x_ref[:, s * LANES:(s + 1) * LANES]

    half = PICKS // 2
    quad = SUBLANES // 2
    low = lax.broadcasted_iota(jnp.int32, (SUBLANES, LANES), 0) < quad

    def token(t, carry):
        x = xt[pl.ds(pl.multiple_of(t * SUBLANES, SUBLANES), SUBLANES), :]
        shifts = bsh.at[t]
        for j in range(half):
            a = x * _expert_row(tab_ref, tile_ref, t, j, shifts)
            b = x * _expert_row(tab_ref, tile_ref, t, j + half, shifts)
            stage[j * SUBLANES:(j + 1) * SUBLANES, :] = jnp.where(low, a + pltpu.roll(a, quad, 0),
                                                                  b + pltpu.roll(b, quad, 0))
        for p in range(2):
            part = stage[pl.ds(p * quad, half, stride=SUBLANES), :]
            for s in range(1, quad):
                part = part + stage[pl.ds(p * quad + s, half, stride=SUBLANES), :]
            parts[t, p * half:(p + 1) * half, :] = part
        return carry

    lax.fori_loop(0, tb, token, 0)
    gh_ref[...] = jnp.sum(parts[...], axis=-1)
    hid = gh_ref[...]
    gelu = 0.5 * hid * (1.0 + lax.erf(hid * (2.0 ** -0.5)))
    gh_ref[...] = gate_ref[...] * gelu


def _pick_block(tb, space=None):
    return pl.BlockSpec((tb, PICKS), lambda i: (i, 0), memory_space=space)


def _table_spec(tab):
    return pl.BlockSpec(tab.shape, lambda i: (0, 0), pipeline_mode=pl.Buffered(1))


def _hidden(tile, shift, xg, gate, tab, tb):
    t = xg.shape[0]
    return pl.pallas_call(
        functools.partial(_hidden_kernel, tb),
        out_shape=jax.ShapeDtypeStruct((t, PICKS), F32),
        grid=(t // tb,),
        in_specs=[_pick_block(tb, pltpu.SMEM), _pick_block(tb),
                  pl.BlockSpec((tb, D_MODEL), lambda i: (i, 0)),
                  _pick_block(tb), _table_spec(tab)],
        out_specs=_pick_block(tb),
        scratch_shapes=[pltpu.VMEM((tb, PICKS, LANES), jnp.int32), pltpu.VMEM((PICKS // 2 * SUBLANES, LANES), F32),
                        pltpu.VMEM((tb, PICKS, LANES), F32), pltpu.VMEM((tb * SUBLANES, LANES), F32)],
        compiler_params=pltpu.CompilerParams(dimension_semantics=("parallel",), vmem_limit_bytes=VMEM_LIMIT),
        name="peer_hidden",
    )(tile, shift, xg, gate, tab)


def _combine_kernel(tb, tile_ref, shift_ref, gh_ref, tab_ref, o_ref, bsh, bg, ot):
    bsh[...] = _splat_picks(shift_ref[...].astype(F32), True).astype(jnp.int32)
    bg[...] = _splat_picks(gh_ref[...], False)

    def token(t, carry):
        shifts, gates = bsh.at[t], bg.at[t]
        acc = [jnp.zeros((SUBLANES, LANES), F32) for _ in range(4)]
        for k in range(PICKS):
            acc[k % 4] = acc[k % 4] + _full(gates, k) * _expert_row(tab_ref, tile_ref, t, k, shifts)
        ot[pl.ds(pl.multiple_of(t * SUBLANES, SUBLANES), SUBLANES), :] = (acc[0] + acc[1]) + (acc[2] + acc[3])
        return carry

    lax.fori_loop(0, tb, token, 0)
    for s in range(SUBLANES):
        o_ref[:, s * LANES:(s + 1) * LANES] = ot[pl.ds(s, tb, stride=SUBLANES), :]


def _combine(tile, shift, gh, tab, tb):
    t = gh.shape[0]
    return pl.pallas_call(
        functools.partial(_combine_kernel, tb),
        out_shape=jax.ShapeDtypeStruct((t, D_MODEL), F32),
        grid=(t // tb,),
        in_specs=[_pick_block(tb, pltpu.SMEM), _pick_block(tb), _pick_block(tb), _table_spec(tab)],
        out_specs=pl.BlockSpec((tb, D_MODEL), lambda i: (i, 0)),
        scratch_shapes=[pltpu.VMEM((tb, PICKS, LANES), jnp.int32), pltpu.VMEM((tb, PICKS, LANES), F32),
                        pltpu.VMEM((tb * SUBLANES, LANES), F32)],
        compiler_params=pltpu.CompilerParams(dimension_semantics=("parallel",), vmem_limit_bytes=VMEM_LIMIT),
        name="peer_combine",
    )(tile, shift, gh, tab)


SC_LANES = 16
SC_WORKERS = 32
SC_CHUNK = 32
SC_FEATURE_BLOCK = 256
SC_PROMPT_TOKENS = 13312


def _sc_call(body, n, width, scratch):
    mesh = plsc.VectorSubcoreMesh(core_axis_name="c", subcore_axis_name="s")
    return pl.kernel(body, mesh=mesh, out_type=jax.ShapeDtypeStruct((n, width), F32),
                     compiler_params=pltpu.CompilerParams(needs_layout_passes=False),
                     scratch_types=[pltpu.VMEM((PICKS,), jnp.int32)] + scratch
                     + [pltpu.VMEM((SC_CHUNK, D_MODEL), F32), pltpu.VMEM((SC_CHUNK, D_MODEL), F32),
                        pltpu.SemaphoreType.DMA, pltpu.SemaphoreType.DMA])


def _sc_token_loop(n, eid_hbm, side_hbm, tab_hbm, out_hbm, idx_v, side_v, out_v, bufs, sems, compute):
    per_worker = n // SC_WORKERS
    wid = lax.axis_index("s") * 2 + lax.axis_index("c")
    chunks = PICKS // SC_CHUNK

    def gather(c):
        ids = idx_v.at[pl.ds(c * SC_CHUNK, SC_CHUNK)]
        return pltpu.make_async_copy(tab_hbm.at[ids], bufs[c % 2], sems[c % 2])

    @pl.loop(0, per_worker)
    def _(i):
        t = wid * per_worker + i
        pltpu.sync_copy(eid_hbm.at[t], idx_v)
        pltpu.sync_copy(side_hbm.at[t], side_v)
        gather(0).start()
        gather(1).start()
        for c in range(chunks):
            gather(c).wait()
            compute(c, bufs[c % 2])
            if c + 2 < chunks:
                gather(c + 2).start()
        pltpu.sync_copy(out_v, out_hbm.at[t])


def _sc_hidden(eid, x, tab):
    n = eid.shape[0]
    lanes = SC_LANES

    def body(eid_hbm, x_hbm, tab_hbm, out_hbm, idx_v, x_v, hid_v, rows0, rows1, sem0, sem1):
        lane = lax.iota(jnp.int32, lanes)

        def compute(c, rows):
            for g in range(SC_CHUNK // lanes):
                def step(j, accs):
                    off = pl.multiple_of(j * lanes, lanes)
                    xj = x_v[pl.ds(off, lanes)]
                    return tuple(accs[r] + xj * rows[g * lanes + r, pl.ds(off, lanes)] for r in range(lanes))
                accs = lax.fori_loop(0, D_MODEL // lanes, step,
                                     tuple(jnp.zeros((lanes,), F32) for _ in range(lanes)))
                hv = jnp.zeros((lanes,), F32)
                for r in range(lanes):
                    hv = jnp.where(lane == r, jnp.sum(accs[r]), hv)
                hid_v[pl.ds(c * SC_CHUNK + g * lanes, lanes)] = hv

        _sc_token_loop(n, eid_hbm, x_hbm, tab_hbm, out_hbm, idx_v, x_v, hid_v, (rows0, rows1), (sem0, sem1), compute)

    call = _sc_call(body, n, PICKS, [pltpu.VMEM((D_MODEL,), F32), pltpu.VMEM((PICKS,), F32)])
    return call(eid, x, tab)


def _sc_combine(eid, gh, tab):
    n = eid.shape[0]
    lanes = SC_LANES
    nacc = SC_FEATURE_BLOCK // lanes

    def body(eid_hbm, g_hbm, tab_hbm, out_hbm, idx_v, g_v, out_v, rows0, rows1, sem0, sem1):
        def compute(c, rows):
            for fb in range(D_MODEL // SC_FEATURE_BLOCK):
                f0 = fb * SC_FEATURE_BLOCK
                if c == 0:
                    init = tuple(jnp.zeros((lanes,), F32) for _ in range(nacc))
                else:
                    init = tuple(out_v[pl.ds(f0 + q * lanes, lanes)] for q in range(nacc))

                def step(r, accs):
                    gk = plsc.load_gather(g_v, [jnp.full((lanes,), c * SC_CHUNK + r, jnp.int32)])
                    return tuple(accs[q] + gk * rows[r, pl.ds(f0 + q * lanes, lanes)] for q in range(nacc))
                accs = lax.fori_loop(0, SC_CHUNK, step, init)
                for q in range(nacc):
                    out_v[pl.ds(f0 + q * lanes, lanes)] = accs[q]

        _sc_token_loop(n, eid_hbm, g_hbm, tab_hbm, out_hbm, idx_v, g_v, out_v, (rows0, rows1), (sem0, sem1), compute)

    call = _sc_call(body, n, D_MODEL, [pltpu.VMEM((PICKS,), F32), pltpu.VMEM((D_MODEL,), F32)])
    return call(eid, gh, tab)


def _after_spec(after):
    return pl.BlockSpec((SUBLANES, after.shape[1]), lambda i: (0, 0))


def _gate_gelu_kernel(h_ref, g_ref, after_ref, o_ref):
    hid = h_ref[...]
    o_ref[...] = g_ref[...] * (0.5 * hid * (1.0 + lax.erf(hid * (2.0 ** -0.5))))


def _gate_gelu(hid, gate, tb, after):
    t = hid.shape[0]
    return pl.pallas_call(
        _gate_gelu_kernel, out_shape=jax.ShapeDtypeStruct((t, PICKS), F32), grid=(t // tb,),
        in_specs=[_pick_block(tb), _pick_block(tb), _after_spec(after)], out_specs=_pick_block(tb),
        compiler_params=pltpu.CompilerParams(dimension_semantics=("parallel",)),
        name="gate_gelu",
    )(hid, gate, after)


def _final_kernel(x_ref, p_ref, g_ref, *rest):
    rest[-1][...] = _rms(x_ref[...] + p_ref[...], g_ref[...])


def _final(x, p, g, tb, t0, after=None):
    t = p.shape[0]
    b0 = t0 // tb
    blk = pl.BlockSpec((tb, D_MODEL), lambda i: (i, 0))
    extra = [] if after is None else [after]
    return pl.pallas_call(
        _final_kernel, out_shape=jax.ShapeDtypeStruct((t, D_MODEL), F32), grid=(t // tb,),
        in_specs=[pl.BlockSpec((tb, D_MODEL), lambda i: (i + b0, 0)), blk, _const_spec(g.shape)]
        + [_after_spec(a) for a in extra], out_specs=blk,
        compiler_params=pltpu.CompilerParams(dimension_semantics=("parallel",), vmem_limit_bytes=VMEM_LIMIT),
        name="final_norm",
    )(x, p, g, *extra)


def _pack_table(tab):
    bits = lax.bitcast_convert_type(tab.astype(BF16), jnp.uint16).astype(jnp.uint32)
    packed = (bits[HALF_EXPERTS:] << 16) | bits[:HALF_EXPERTS]
    return lax.bitcast_convert_type(packed, jnp.int32).reshape(HALF_EXPERTS * SUBLANES, LANES)


def _col(v, n=LANES):
    return jnp.pad(v.astype(F32), (0, n - v.shape[0])).reshape(1, n)


def _row(v, n=2 * SUBLANES):
    return jnp.pad(v.astype(F32), (0, n - v.shape[0])).reshape(n, 1)


def _peer(x, w, tb_sel, tb_exp, n_sc):
    t = x.shape[0]
    n_tc = t - n_sc
    if n_sc:
        xn_s, eid_s, _, _, gate_s = _select(x, w["ln_ffn"], w["w_pq"], w["sub_keys"], tb_sel, 0, n_sc)
        hid_s = _sc_hidden(eid_s, xn_s, w["u_tab"])
    xn, _, tile, shift, gate = _select(x, w["ln_ffn"], w["w_pq"], w["sub_keys"], tb_sel, n_sc, n_tc,
                                       after=xn_s if n_sc else None)
    gh = _hidden(tile, shift, xn, gate, w["u_pack"], tb_exp)
    if n_sc:
        out_s = _sc_combine(eid_s, _gate_gelu(hid_s, gate_s, tb_sel, after=gh), w["v_tab"])
    out = _combine(tile, shift, gh, w["v_pack"], tb_exp)
    y = _final(x, out, w["final_norm"], tb_sel, n_sc)
    if n_sc:
        y = jnp.concatenate([_final(x, out_s, w["final_norm"], tb_sel, 0, after=out), y], axis=0)
    return y


def kernel(x_prompt, x_sample, mem_prompt, cache_conv_a, cache_conv_ssd, state_ssd, cache_mem_k, cache_mem_v, ln_mix_w, w_in, conv_a_w, conv_s_w, conv_s_b, dt_bias, a_log, d_skip, ssd_norm_w, w_out, ln_mem_w, mem_norm_w, w_mq, w_mk, w_mv, w_mo, ln_ffn_w, w_pq, sub_keys, u_tab, v_tab, final_norm_w):
    depth = w_in.shape[0]
    assert depth == 1
    l = 0
    n_main = 3 * D_CONV + D_SSD + XBC_DIM
    w_dt = w_in[l][:, n_main:]
    w = {
        "ln_mix": ln_mix_w[l].reshape(1, D_MODEL),
        "w_in": w_in[l][:, :n_main].astype(BF16),
        "w_dt": jnp.pad(w_dt, ((0, 0), (0, LANES - SSD_HEADS))).astype(BF16),
        "w_dtt": jnp.pad(w_dt.T, ((0, 2 * SUBLANES - SSD_HEADS), (0, 0))).astype(BF16),
        "conv_a_w": conv_a_w[l], "conv_s_w": conv_s_w[l], "conv_s_b": conv_s_b[l].reshape(1, XBC_DIM),
        "dtb_c": _col(dt_bias[l]), "alog_c": _col(a_log[l]), "dsk_c": _col(d_skip[l]),
        "dtb_r": _row(dt_bias[l]), "alog_r": _row(a_log[l]),
        "ssd_norm": ssd_norm_w[l].reshape(1, D_SSD),
        "w_out": w_out[l].astype(BF16),
        "ln_ffn": ln_ffn_w[l].reshape(1, D_MODEL),
        "w_pq": w_pq[l].astype(BF16).reshape(D_MODEL, PEER_HEADS, PEER_DK).transpose(1, 0, 2),
        "sub_keys": sub_keys[l].astype(BF16),
        "u_pack": _pack_table(u_tab[l]),
        "v_pack": _pack_table(v_tab[l]),
        "u_tab": u_tab[l], "v_tab": v_tab[l],
        "final_norm": final_norm_w.reshape(1, D_MODEL),
    }
    ln_mem = ln_mem_w[l].reshape(1, D_MODEL)
    wq, wo = w_mq[l].astype(BF16), w_mo[l].astype(BF16)

    bp, sp, _ = x_prompt.shape
    bs, ss, _ = x_sample.shape

    mk, mv = _memory_kv(mem_prompt.reshape(bp * N_MEM, D_MODEL), mem_norm_w[l].reshape(1, D_MODEL),
                        w_mk[l].astype(BF16), w_mv[l].astype(BF16))
    hp, pa, ps, ph = _mixer(x_prompt, None, w, 256)
    hp = _attn(hp, mk.reshape(bp, N_MEM, D_MODEL), mv.reshape(bp, N_MEM, D_MODEL), ln_mem, wq, wo, 256)
    y_prompt = _peer(hp.reshape(bp * sp, D_MODEL), w, 256, 64, SC_PROMPT_TOKENS).reshape(bp, sp, D_MODEL)

    hs, sa, ssd, sh = _mixer(x_sample, (cache_conv_a[l], cache_conv_ssd[l],
                                        state_ssd[l].reshape((bs,) + PAIR_STATE)), w, 256)
    hs = _attn(hs, cache_mem_k[l].reshape(bs, N_MEM, D_MODEL), cache_mem_v[l].reshape(bs, N_MEM, D_MODEL),
               ln_mem, wq, wo, 256)
    y_sample = _peer(hs.reshape(bs * ss, D_MODEL), w, 256, 64, 0).reshape(bs, ss, D_MODEL)

    kv_shape = (1, bp, N_MEM, MEM_HEADS, MEM_HEAD_DIM)
    st_shape = (SSD_HEADS, SSD_HEAD_DIM, SSD_STATE)
    return (y_prompt, y_sample, pa[None], ps[None], ph.reshape((1, bp) + st_shape), mk.reshape(kv_shape),
            mv.reshape(kv_shape), sa[None], ssd[None], sh.reshape((1, bs) + st_shape))
```

```python
import functools
import math

import jax
import jax.numpy as jnp
from jax import lax
from jax.experimental import pallas as pl
from jax.experimental.pallas import tpu as pltpu
from jax.experimental.pallas import tpu_sc as plsc

D_MODEL = 1024
CHUNK = 64
D_CONV = 512
CONV_A_W = 3
D_SSD = 512
SSD_HEAD_DIM = 64
SSD_HEADS = 8
SSD_GROUPS = 2
SSD_HPG = 4
SSD_STATE = 128
SSD_CONV_W = 4
XBC_DIM = 1024
N_MEM = 256
MEM_HEADS = 4
MEM_HEAD_DIM = 256
PEER_HEADS = 8
N_KEYS = 128
N_EXPERTS = N_KEYS * N_KEYS
PEER_TOPK = 16
PEER_DK = 256
PEER_DK_HALF = 128
PICKS = PEER_HEADS * PEER_TOPK
EPS = 1e-6

LANES = 128
SUBLANES = 8
HALF_EXPERTS = N_EXPERTS // 2
PAIR_STATE = (SSD_HEADS // 2, 2 * SSD_HEAD_DIM, SSD_STATE)
VMEM_LIMIT = 56 * 1024 * 1024

F32 = jnp.float32
BF16 = jnp.bfloat16
HI = lax.Precision.HIGHEST


def _rms(x, g):
    return x * lax.rsqrt(jnp.mean(x * x, axis=-1, keepdims=True) + EPS) * g


def _silu(x):
    return x * (1.0 / (1.0 + jnp.exp(-x)))


def _softplus(x):
    return jnp.maximum(x, 0.0) + jnp.log1p(jnp.exp(-jnp.abs(x)))


def _dot(a, b):
    return jnp.dot(a.astype(BF16), b.astype(BF16), preferred_element_type=F32)


def _dot_nt(a, b):
    return lax.dot_general(a.astype(BF16), b.astype(BF16), (((1,), (1,)), ((), ())),
                           preferred_element_type=F32)


def _dot_tn(a, b):
    return lax.dot_general(a.astype(BF16), b.astype(BF16), (((0,), (0,)), ((), ())),
                           preferred_element_type=F32)


def _const_spec(shape):
    n = len(shape)
    return pl.BlockSpec(shape, lambda *_: (0,) * n)


def _mixer_kernel(has_hist, ts, lc, *refs):
    if has_hist:
        x_ref, hista_ref, hists_ref, h0_ref = refs[:4]
        refs = refs[4:]
    else:
        x_ref = refs[0]
        refs = refs[1:]
    (lnw_ref, win_ref, wdt_ref, wdtt_ref, caw_ref, csw_ref, csb_ref, dtb_c_ref, alog_c_ref, dsk_c_ref,
     dtb_r_ref, alog_r_ref, nw_ref, wout_ref,
     h1_ref, na_ref, ns_ref, nh_ref, cata, cats, hst) = refs
    s = pl.program_id(1)
    pad = SUBLANES

    @pl.when(s == 0)
    def _():
        if has_hist:
            cata[pad - 2:pad, :] = hista_ref[...]
            cats[pad - 3:pad, :] = hists_ref[...]
            hst[...] = h0_ref[...]
        else:
            cata[0:pad, :] = jnp.zeros((pad, D_CONV), F32)
            cats[0:pad, :] = jnp.zeros((pad, XBC_DIM), F32)
            hst[...] = jnp.zeros(hst.shape, F32)

    x = x_ref[...]
    xn = _rms(x, lnw_ref[...]).astype(BF16)
    proj = jnp.dot(xn, win_ref[...], preferred_element_type=F32)
    dt_c = jnp.dot(xn, wdt_ref[...], preferred_element_type=F32)
    dt_r = lax.dot_general(wdtt_ref[...], xn, (((1,), (1,)), ((), ())),
                           preferred_element_type=F32)
    g_b = proj[:, 0:D_CONV]
    g_c = proj[:, D_CONV:2 * D_CONV]
    v_in = proj[:, 2 * D_CONV:3 * D_CONV]
    z = proj[:, 3 * D_CONV:3 * D_CONV + D_SSD]
    xbc = proj[:, 3 * D_CONV + D_SSD:3 * D_CONV + D_SSD + XBC_DIM]

    ua = g_c * v_in
    cata[pad:pad + ts, :] = ua
    caw = caw_ref[...]
    conv_a = (caw[0:1] * cata[pad - 2:pad - 2 + ts, :] + caw[1:2] * cata[pad - 1:pad - 1 + ts, :]
              + caw[2:3] * ua)
    y_a = g_b * conv_a
    hist_a = cata[pad + ts - 2:pad + ts, :]
    na_ref[...] = hist_a
    cata[pad - 2:pad, :] = hist_a

    cats[pad:pad + ts, :] = xbc
    csw = csw_ref[...]
    conv_s = (csw[0:1] * cats[pad - 3:pad - 3 + ts, :] + csw[1:2] * cats[pad - 2:pad - 2 + ts, :]
              + csw[2:3] * cats[pad - 1:pad - 1 + ts, :] + csw[3:4] * xbc)
    hist_s = cats[pad + ts - 3:pad + ts, :]
    ns_ref[...] = hist_s
    cats[pad - 3:pad, :] = hist_s
    xbc_c = _silu(conv_s + csb_ref[...])
    xs = xbc_c[:, 0:D_SSD]

    dtp_c = _softplus(dt_c + dtb_c_ref[...])
    dtp_r = _softplus(dt_r + dtb_r_ref[...])
    a_c = dtp_c * (-jnp.exp(alog_c_ref[...]))
    a_r = dtp_r * (-jnp.exp(alog_r_ref[...]))
    dsk_c = dsk_c_ref[...]

    lane = lax.broadcasted_iota(jnp.int32, (1, LANES), 1)
    first_head = lane < SSD_HEAD_DIM
    row2 = lax.broadcasted_iota(jnp.int32, (2 * SSD_HEAD_DIM, 1), 0) < SSD_HEAD_DIM
    li = lax.broadcasted_iota(jnp.int32, (lc, lc), 0)
    si = lax.broadcasted_iota(jnp.int32, (lc, lc), 1)
    causal = si <= li
    tril = causal.astype(F32)
    jrow = lax.broadcasted_iota(jnp.int32, (ts, lc), 0)
    scol = lax.broadcasted_iota(jnp.int32, (ts, lc), 1)

    def pair(col, h):
        return jnp.where(first_head, col[:, h:h + 1], col[:, h + 1:h + 2])

    y_chunks = []
    for c in range(ts // lc):
        r0 = c * lc
        rows = slice(r0, r0 + lc)
        acum_c = jnp.dot(tril, a_c[rows], preferred_element_type=F32, precision=HI)
        upper = ((jrow >= r0) & (jrow <= r0 + scol)).astype(F32)
        acum_r = jnp.dot(a_r, upper, preferred_element_type=F32, precision=HI)
        last_c = acum_c[lc - 1:lc, :]
        dec_c = jnp.exp(last_c - acum_c)
        eac_c = jnp.exp(acum_c)
        y_pairs = []
        for g in range(SSD_GROUPS):
            b_g = xbc_c[rows, D_SSD + g * SSD_STATE:D_SSD + (g + 1) * SSD_STATE]
            c_g = xbc_c[rows, D_SSD + (SSD_GROUPS + g) * SSD_STATE:D_SSD + (SSD_GROUPS + g + 1) * SSD_STATE]
            cb = _dot_nt(c_g, b_g)
            for q in range(SSD_HPG // 2):
                h = g * SSD_HPG + 2 * q
                pi = h // 2
                xs_p = xs[rows, pi * LANES:(pi + 1) * LANES]
                xdt = xs_p * pair(dtp_c[rows], h)
                res = []
                for hh in (h, h + 1):
                    seg = acum_c[:, hh:hh + 1] - acum_r[hh:hh + 1, :]
                    m_h = cb * jnp.exp(jnp.where(causal, seg, -jnp.inf))
                    res.append(_dot(m_h, xdt))
                y_diag = jnp.where(first_head, res[0], res[1])
                h_in = hst[pi]
                y_off = _dot_nt(c_g, h_in) * pair(eac_c, h)
                st = _dot_tn(xdt * pair(dec_c, h), b_g)
                cd = jnp.exp(jnp.where(row2, last_c[:, h:h + 1], last_c[:, h + 1:h + 2]))
                hst[pi] = cd * h_in + st
                y_pairs.append(y_diag + y_off + pair(dsk_c, h) * xs_p)
        y_chunks.append(jnp.concatenate(y_pairs, axis=1))
    y = y_chunks[0] if len(y_chunks) == 1 else jnp.concatenate(y_chunks, axis=0)
    y_b = _rms(y * _silu(z), nw_ref[...])
    wout = wout_ref[...]
    out = (jnp.dot(y_a.astype(BF16), wout[0:D_CONV], preferred_element_type=F32)
           + jnp.dot(y_b.astype(BF16), wout[D_CONV:], preferred_element_type=F32))
    h1_ref[...] = x + out
    nh_ref[...] = hst[...]


def _mixer(x, hist, w, ts):
    b, s, _ = x.shape
    lc = min(CHUNK, s)
    ts = min(ts, s)
    has_hist = hist is not None
    per_b3 = lambda shape: pl.BlockSpec((None,) + shape, lambda i, j: (i,) + (0,) * len(shape))
    in_specs = [pl.BlockSpec((None, ts, D_MODEL), lambda i, j: (i, j, 0))]
    args = [x]
    if has_hist:
        in_specs += [per_b3((CONV_A_W - 1, D_CONV)), per_b3((SSD_CONV_W - 1, XBC_DIM)),
                     per_b3(PAIR_STATE)]
        args += list(hist)
    wargs = [w["ln_mix"], w["w_in"], w["w_dt"], w["w_dtt"], w["conv_a_w"], w["conv_s_w"], w["conv_s_b"],
             w["dtb_c"], w["alog_c"], w["dsk_c"], w["dtb_r"], w["alog_r"], w["ssd_norm"], w["w_out"]]
    in_specs += [_const_spec(a.shape) for a in wargs]
    args += wargs
    out_shape = (jax.ShapeDtypeStruct((b, s, D_MODEL), F32),
                 jax.ShapeDtypeStruct((b, CONV_A_W - 1, D_CONV), F32),
                 jax.ShapeDtypeStruct((b, SSD_CONV_W - 1, XBC_DIM), F32),
                 jax.ShapeDtypeStruct((b,) + PAIR_STATE, F32))
    out_specs = (pl.BlockSpec((None, ts, D_MODEL), lambda i, j: (i, j, 0)),
                 per_b3((CONV_A_W - 1, D_CONV)), per_b3((SSD_CONV_W - 1, XBC_DIM)),
                 per_b3(PAIR_STATE))
    return pl.pallas_call(
        functools.partial(_mixer_kernel, has_hist, ts, lc),
        out_shape=out_shape, grid=(b, s // ts), in_specs=in_specs, out_specs=out_specs,
        scratch_shapes=[pltpu.VMEM((SUBLANES + ts, D_CONV), F32), pltpu.VMEM((SUBLANES + ts, XBC_DIM), F32),
                        pltpu.VMEM(PAIR_STATE, F32)],
        compiler_params=pltpu.CompilerParams(dimension_semantics=("parallel", "arbitrary"),
                                             vmem_limit_bytes=VMEM_LIMIT),
        name="mixer_hist" if has_hist else "mixer",
    )(*args)


def _memkv_kernel(m_ref, g_ref, wk_ref, wv_ref, k_ref, v_ref):
    mn = _rms(m_ref[...], g_ref[...]).astype(BF16)
    k_ref[...] = jnp.dot(mn, wk_ref[...], preferred_element_type=F32)
    v_ref[...] = jnp.dot(mn, wv_ref[...], preferred_element_type=F32)


def _memory_kv(mem, g, wk, wv, tm=256):
    t = mem.shape[0]
    blk = pl.BlockSpec((tm, D_MODEL), lambda i: (i, 0))
    return pl.pallas_call(
        _memkv_kernel,
        out_shape=(jax.ShapeDtypeStruct((t, D_MODEL), F32),) * 2,
        grid=(t // tm,), in_specs=[blk, _const_spec(g.shape), _const_spec(wk.shape), _const_spec(wv.shape)],
        out_specs=(blk, blk),
        compiler_params=pltpu.CompilerParams(dimension_semantics=("parallel",), vmem_limit_bytes=VMEM_LIMIT),
        name="memory_kv",
    )(mem, g, wk, wv)


def _attn_kernel(x_ref, k_ref, v_ref, g_ref, wq_ref, wo_ref, o_ref):
    x = x_ref[...]
    xn = _rms(x, g_ref[...]).astype(BF16)
    q = jnp.dot(xn, wq_ref[...], preferred_element_type=F32)
    k = k_ref[...]
    v = v_ref[...]
    wo = wo_ref[...]
    acc = x
    for h in range(MEM_HEADS):
        cols = slice(h * MEM_HEAD_DIM, (h + 1) * MEM_HEAD_DIM)
        sc = _dot_nt(q[:, cols], k[:, cols]) * (MEM_HEAD_DIM ** -0.5)
        sc = sc - jnp.max(sc, axis=-1, keepdims=True)
        p = jnp.exp(sc)
        p = p / jnp.sum(p, axis=-1, keepdims=True)
        o_h = _dot(p, v[:, cols])
        acc = acc + jnp.dot(o_h.astype(BF16), wo[cols, :], preferred_element_type=F32)
    o_ref[...] = acc


def _attn(x, k, v, g, wq, wo, ts):
    b, s, _ = x.shape
    ts = min(ts, s)
    xblk = pl.BlockSpec((None, ts, D_MODEL), lambda i, j: (i, j, 0))
    kvblk = pl.BlockSpec((None, N_MEM, D_MODEL), lambda i, j: (i, 0, 0))
    return pl.pallas_call(
        _attn_kernel,
        out_shape=jax.ShapeDtypeStruct((b, s, D_MODEL), F32),
        grid=(b, s // ts),
        in_specs=[xblk, kvblk, kvblk, _const_spec(g.shape), _const_spec(wq.shape), _const_spec(wo.shape)],
        out_specs=xblk,
        compiler_params=pltpu.CompilerParams(dimension_semantics=("parallel", "parallel"),
                                             vmem_limit_bytes=VMEM_LIMIT),
        name="attn",
    )(x, k, v, g, wq, wo)


def _top_k_rows(sc, k, payload=None):
    n = sc.shape[0]
    rows = lax.broadcasted_iota(jnp.int32, sc.shape, 0).astype(F32)
    vals, outs = [], []
    for _ in range(k):
        m = jnp.max(sc, axis=0, keepdims=True)
        cand = jnp.where(sc == m, rows, float(n))
        first = jnp.min(cand, axis=0, keepdims=True)
        hit = cand == first
        sc = jnp.where(hit, -jnp.inf, sc)
        vals.append(m)
        if payload is None:
            outs.append(first.astype(jnp.int32))
        else:
            outs.append(jnp.sum(jnp.where(hit, payload, 0), axis=0, keepdims=True))
    return jnp.concatenate(vals, axis=0), jnp.concatenate(outs, axis=0)


_LIVE_B = tuple(PEER_TOPK // (a + 1) for a in range(1, SUBLANES))


def _pair_candidates(v1, i1, v2, i2):
    sub = lax.broadcasted_iota(jnp.int32, (SUBLANES, v1.shape[1]), 0)
    comb = [v1[0:1] + v2]
    cidx = [i1[0:1] * N_KEYS + i2]
    for a in range(1, SUBLANES):
        comb.append(jnp.where(sub < _LIVE_B[a - 1], v1[a:a + 1] + v2[0:SUBLANES], -jnp.inf))
        cidx.append(i1[a:a + 1] * N_KEYS + i2[0:SUBLANES])
    comb.append(v1[SUBLANES:] + v2[0:1])
    cidx.append(i1[SUBLANES:] * N_KEYS + i2[0:1])
    return jnp.concatenate(comb, axis=0), jnp.concatenate(cidx, axis=0)


def _select_kernel(tb, x_ref, g_ref, wq_ref, keys_ref, *rest):
    xn_ref, eid_ref, tile_ref, shift_ref, gate_ref, eid_s, gate_s = rest[-7:]
    xn = _rms(x_ref[...], g_ref[...])
    xn_ref[...] = xn
    xn = xn.astype(BF16)

    def head(h, carry):
        q = jnp.dot(xn, wq_ref[h], preferred_element_type=F32)
        sc1 = _dot_nt(keys_ref[h, 0], q[:, 0:PEER_DK_HALF])
        sc2 = _dot_nt(keys_ref[h, 1], q[:, PEER_DK_HALF:PEER_DK])
        rows = pl.ds(pl.multiple_of(h * PEER_TOPK, PEER_TOPK), PEER_TOPK)
        for c0 in range(0, tb, LANES):
            cols = slice(c0, c0 + LANES)
            v1, i1 = _top_k_rows(sc1[:, cols], PEER_TOPK)
            v2, i2 = _top_k_rows(sc2[:, cols], PEER_TOPK)
            comb, cidx = _pair_candidates(v1, i1, v2, i2)
            top, eid = _top_k_rows(comb, PEER_TOPK, cidx)
            ex = jnp.exp(top - top[0:1])
            gate_s[rows, cols] = ex / jnp.sum(ex, axis=0, keepdims=True)
            eid_s[rows, cols] = eid
        return carry

    lax.fori_loop(0, PEER_HEADS, head, 0)
    for c0 in range(0, tb, LANES):
        toks = slice(c0, c0 + LANES)
        eid = eid_s[:, toks].T
        upper = eid >= HALF_EXPERTS
        eid_ref[toks, :] = eid
        tile_ref[toks, :] = jnp.where(upper, eid - HALF_EXPERTS, eid) * SUBLANES
        shift_ref[toks, :] = jnp.where(upper, 0, 16)
        gate_ref[toks, :] = gate_s[:, toks].T


def _select(x, g, wq, keys, tb, t0, t, after=None):
    b0 = t0 // tb
    extra = [] if after is None else [after]
    xin = pl.BlockSpec((tb, D_MODEL), lambda i: (i + b0, 0))
    xblk = pl.BlockSpec((tb, D_MODEL), lambda i: (i, 0))
    picks_i = jax.ShapeDtypeStruct((t, PICKS), jnp.int32)
    return pl.pallas_call(
        functools.partial(_select_kernel, tb),
        out_shape=(jax.ShapeDtypeStruct((t, D_MODEL), F32), picks_i, picks_i, picks_i,
                   jax.ShapeDtypeStruct((t, PICKS), F32)),
        grid=(t // tb,),
        in_specs=[xin, _const_spec(g.shape), _const_spec(wq.shape), _const_spec(keys.shape)]
        + [_after_spec(a) for a in extra],
        out_specs=(xblk,) + (_pick_block(tb),) * 4,
        scratch_shapes=[pltpu.VMEM((PICKS, tb), jnp.int32), pltpu.VMEM((PICKS, tb), F32)],
        compiler_params=pltpu.CompilerParams(dimension_semantics=("parallel",), vmem_limit_bytes=VMEM_LIMIT),
        name="peer_select",
    )(x, g, wq, keys, *extra)


HI_HALF_MASK = -(1 << 16)
_BITREV3 = (0, 4, 2, 6, 1, 5, 3, 7)


def _unpack(word, shift):
    return lax.bitcast_convert_type((word << shift) & HI_HALF_MASK, F32)


def _splat_picks(blk, one_pass):
    tb = blk.shape[0]
    r = lax.broadcasted_iota(jnp.int32, (PICKS, LANES), 0)
    c = lax.broadcasted_iota(jnp.int32, (PICKS, LANES), 1)
    eye = (r == c)[None]
    ones = jnp.ones((LANES, LANES), BF16)

    def one(piece):
        diag = jnp.where(eye, piece[:, None, :], 0.0).astype(BF16).reshape(tb * PICKS, LANES)
        return jnp.dot(diag, ones, preferred_element_type=F32)

    if one_pass:
        out = one(blk)
    else:
        hi = blk.astype(BF16).astype(F32)
        rest = blk - hi
        mid = rest.astype(BF16).astype(F32)
        out = one(hi) + one(mid) + one(rest - mid)
    return out.reshape(tb, PICKS, LANES)


def _full(ref, k):
    return jnp.broadcast_to(ref[k:k + 1, :], (SUBLANES, LANES))


def _expert_row(tab_ref, tile_ref, t, k, shifts):
    idx = pl.multiple_of(tile_ref[t, k], SUBLANES)
    return _unpack(tab_ref[pl.ds(idx, SUBLANES), :], _full(shifts, k))


def _hidden_kernel(tb, tile_ref, shift_ref, x_ref, gate_ref, tab_ref, gh_ref, bsh, stage, parts, xt):
    bsh[...] = _splat_picks(shift_ref[...].astype(F32), True).astype(jnp.int32)
    for s in range(SUBLANES):
        xt[pl.ds(s, tb, stride=SUBLANES), :] = x_ref[:, s * LANES:(s + 1) * LANES]

    half = PICKS // 2
    quad = SUBLANES // 2
    low = lax.broadcasted_iota(jnp.int32, (SUBLANES, LANES), 0) < quad

    def token(t, carry):
        x = xt[pl.ds(pl.multiple_of(t * SUBLANES, SUBLANES), SUBLANES), :]
        shifts = bsh.at[t]
        for j in range(half):
            a = x * _expert_row(tab_ref, tile_ref, t, j, shifts)
            b = x * _expert_row(tab_ref, tile_ref, t, j + half, shifts)
            stage[j * SUBLANES:(j + 1) * SUBLANES, :] = jnp.where(low, a + pltpu.roll(a, quad, 0),
                                                                  b + pltpu.roll(b, quad, 0))
        for p in range(2):
            part = stage[pl.ds(p * quad, half, stride=SUBLANES), :]
            for s in range(1, quad):
                part = part + stage[pl.ds(p * quad + s, half, stride=SUBLANES), :]
            parts[t, p * half:(p + 1) * half, :] = part
        return carry

    lax.fori_loop(0, tb, token, 0)
    gh_ref[...] = jnp.sum(parts[...], axis=-1)
    hid = gh_ref[...]
    gelu = 0.5 * hid * (1.0 + lax.erf(hid * (2.0 ** -0.5)))
    gh_ref[...] = gate_ref[...] * gelu


def _pick_block(tb, space=None):
    return pl.BlockSpec((tb, PICKS), lambda i: (i, 0), memory_space=space)


def _table_spec(tab):
    return pl.BlockSpec(tab.shape, lambda i: (0, 0), pipeline_mode=pl.Buffered(1))


def _hidden(tile, shift, xg, gate, tab, tb):
    t = xg.shape[0]
    return pl.pallas_call(
        functools.partial(_hidden_kernel, tb),
        out_shape=jax.ShapeDtypeStruct((t, PICKS), F32),
        grid=(t // tb,),
        in_specs=[_pick_block(tb, pltpu.SMEM), _pick_block(tb),
                  pl.BlockSpec((tb, D_MODEL), lambda i: (i, 0)),
                  _pick_block(tb), _table_spec(tab)],
        out_specs=_pick_block(tb),
        scratch_shapes=[pltpu.VMEM((tb, PICKS, LANES), jnp.int32), pltpu.VMEM((PICKS // 2 * SUBLANES, LANES), F32),
                        pltpu.VMEM((tb, PICKS, LANES), F32), pltpu.VMEM((tb * SUBLANES, LANES), F32)],
        compiler_params=pltpu.CompilerParams(dimension_semantics=("parallel",), vmem_limit_bytes=VMEM_LIMIT),
        name="peer_hidden",
    )(tile, shift, xg, gate, tab)


def _combine_kernel(tb, tile_ref, shift_ref, gh_ref, tab_ref, o_ref, bsh, bg, ot):
    bsh[...] = _splat_picks(shift_ref[...].astype(F32), True).astype(jnp.int32)
    bg[...] = _splat_picks(gh_ref[...], False)

    def token(t, carry):
        shifts, gates = bsh.at[t], bg.at[t]
        acc = [jnp.zeros((SUBLANES, LANES), F32) for _ in range(4)]
        for k in range(PICKS):
            acc[k % 4] = acc[k % 4] + _full(gates, k) * _expert_row(tab_ref, tile_ref, t, k, shifts)
        ot[pl.ds(pl.multiple_of(t * SUBLANES, SUBLANES), SUBLANES), :] = (acc[0] + acc[1]) + (acc[2] + acc[3])
        return carry

    lax.fori_loop(0, tb, token, 0)
    for s in range(SUBLANES):
        o_ref[:, s * LANES:(s + 1) * LANES] = ot[pl.ds(s, tb, stride=SUBLANES), :]


def _combine(tile, shift, gh, tab, tb):
    t = gh.shape[0]
    return pl.pallas_call(
        functools.partial(_combine_kernel, tb),
        out_shape=jax.ShapeDtypeStruct((t, D_MODEL), F32),
        grid=(t // tb,),
        in_specs=[_pick_block(tb, pltpu.SMEM), _pick_block(tb), _pick_block(tb), _table_spec(tab)],
        out_specs=pl.BlockSpec((tb, D_MODEL), lambda i: (i, 0)),
        scratch_shapes=[pltpu.VMEM((tb, PICKS, LANES), jnp.int32), pltpu.VMEM((tb, PICKS, LANES), F32),
                        pltpu.VMEM((tb * SUBLANES, LANES), F32)],
        compiler_params=pltpu.CompilerParams(dimension_semantics=("parallel",), vmem_limit_bytes=VMEM_LIMIT),
        name="peer_combine",
    )(tile, shift, gh, tab)


SC_LANES = 16
SC_WORKERS = 32
SC_CHUNK = 32
SC_FEATURE_BLOCK = 256
SC_PROMPT_TOKENS = 17152


def _sc_call(body, n, side_width, out_width):
    mesh = plsc.VectorSubcoreMesh(core_axis_name="c", subcore_axis_name="s")
    return pl.kernel(body, mesh=mesh, out_type=jax.ShapeDtypeStruct((n, out_width), F32),
                     compiler_params=pltpu.CompilerParams(needs_layout_passes=False),
                     scratch_types=[pltpu.VMEM((2 * PICKS,), jnp.int32), pltpu.VMEM((2 * side_width,), F32),
                                    pltpu.VMEM((2 * out_width,), F32),
                                    pltpu.VMEM((SC_CHUNK, D_MODEL), F32), pltpu.VMEM((SC_CHUNK, D_MODEL), F32)]
                     + [pltpu.SemaphoreType.DMA] * 5)


def _sc_token_loop(n, eid_hbm, side_hbm, tab_hbm, out_hbm, idx_v, side_v, out_v, bufs, sems, compute):
    sem_rows, sem_fetch, sem_put = sems[0:2], sems[2], sems[3:5]
    per_worker = n // SC_WORKERS
    first = (lax.axis_index("s") * 2 + lax.axis_index("c")) * per_worker
    last = first + per_worker - 1
    chunks = PICKS // SC_CHUNK
    side_w = side_v.shape[0] // 2
    out_w = out_v.shape[0] // 2

    def fetch(t, slot):
        return (pltpu.make_async_copy(eid_hbm.at[t], idx_v.at[pl.ds(slot * PICKS, PICKS)], sem_fetch),
                pltpu.make_async_copy(side_hbm.at[t], side_v.at[pl.ds(slot * side_w, side_w)], sem_fetch))

    def gather(slot, c):
        ids = idx_v.at[pl.ds(slot * PICKS + c * SC_CHUNK, SC_CHUNK)]
        return pltpu.make_async_copy(tab_hbm.at[ids], bufs[c % 2], sem_rows[c % 2])

    def put(t, slot):
        return pltpu.make_async_copy(out_v.at[pl.ds(slot * out_w, out_w)], out_hbm.at[t], sem_put[slot])

    for cp in fetch(first, 0):
        cp.start()
    for cp in fetch(first, 0):
        cp.wait()
    gather(0, 0).start()
    gather(0, 1).start()

    @pl.loop(0, per_worker, step=2)
    def _(i):
        for slot in range(2):
            t = first + i + slot
            nxt = jnp.minimum(t + 1, last)
            for cp in fetch(nxt, 1 - slot):
                cp.start()

            @pl.when(i >= 2)
            def _():
                put(t, slot).wait()

            for c in range(chunks):
                gather(slot, c).wait()
                compute(c, bufs[c % 2], slot * side_w, slot * out_w)
                if c + 2 < chunks:
                    gather(slot, c + 2).start()
                else:
                    if c + 2 == chunks:
                        for cp in fetch(nxt, 1 - slot):
                            cp.wait()
                    gather(1 - slot, c + 2 - chunks).start()
            put(t, slot).start()

    gather(0, 0).wait()
    gather(0, 1).wait()
    put(last, 0).wait()
    put(last, 1).wait()


def _sc_hidden(eid, x, tab):
    n = eid.shape[0]
    lanes = SC_LANES

    def body(eid_hbm, x_hbm, tab_hbm, out_hbm, idx_v, x_v, hid_v, rows0, rows1, *sems):
        lane = lax.iota(jnp.int32, lanes)

        def compute(c, rows, x0, h0):
            for g in range(SC_CHUNK // lanes):
                def step(j, accs):
                    off = pl.multiple_of(j * lanes, lanes)
                    xj = x_v[pl.ds(x0 + off, lanes)]
                    return tuple(accs[r] + xj * rows[g * lanes + r, pl.ds(off, lanes)] for r in range(lanes))
                accs = lax.fori_loop(0, D_MODEL // lanes, step,
                                     tuple(jnp.zeros((lanes,), F32) for _ in range(lanes)))
                hv = jnp.zeros((lanes,), F32)
                for r in range(lanes):
                    hv = jnp.where(lane == r, jnp.sum(accs[r]), hv)
                hid_v[pl.ds(h0 + c * SC_CHUNK + g * lanes, lanes)] = hv

        _sc_token_loop(n, eid_hbm, x_hbm, tab_hbm, out_hbm, idx_v, x_v, hid_v, (rows0, rows1), sems, compute)

    return _sc_call(body, n, D_MODEL, PICKS)(eid, x, tab)


def _sc_combine(eid, gh, tab):
    n = eid.shape[0]
    lanes = SC_LANES
    nacc = SC_FEATURE_BLOCK // lanes

    def body(eid_hbm, g_hbm, tab_hbm, out_hbm, idx_v, g_v, out_v, rows0, rows1, *sems):
        def compute(c, rows, g0, o0):
            for fb in range(D_MODEL // SC_FEATURE_BLOCK):
                f0 = fb * SC_FEATURE_BLOCK
                if c == 0:
                    init = tuple(jnp.zeros((lanes,), F32) for _ in range(nacc))
                else:
                    init = tuple(out_v[pl.ds(o0 + f0 + q * lanes, lanes)] for q in range(nacc))

                def step(r, accs):
                    gk = plsc.load_gather(g_v, [jnp.full((lanes,), g0 + c * SC_CHUNK + r, jnp.int32)])
                    return tuple(accs[q] + gk * rows[r, pl.ds(f0 + q * lanes, lanes)] for q in range(nacc))
                accs = lax.fori_loop(0, SC_CHUNK, step, init)
                for q in range(nacc):
                    out_v[pl.ds(o0 + f0 + q * lanes, lanes)] = accs[q]

        _sc_token_loop(n, eid_hbm, g_hbm, tab_hbm, out_hbm, idx_v, g_v, out_v, (rows0, rows1), sems, compute)

    return _sc_call(body, n, PICKS, D_MODEL)(eid, gh, tab)


def _after_spec(after):
    return pl.BlockSpec((SUBLANES, after.shape[1]), lambda i: (0, 0))


def _gate_gelu_kernel(h_ref, g_ref, after_ref, o_ref):
    hid = h_ref[...]
    o_ref[...] = g_ref[...] * (0.5 * hid * (1.0 + lax.erf(hid * (2.0 ** -0.5))))


def _gate_gelu(hid, gate, tb, after):
    t = hid.shape[0]
    return pl.pallas_call(
        _gate_gelu_kernel, out_shape=jax.ShapeDtypeStruct((t, PICKS), F32), grid=(t // tb,),
        in_specs=[_pick_block(tb), _pick_block(tb), _after_spec(after)], out_specs=_pick_block(tb),
        compiler_params=pltpu.CompilerParams(dimension_semantics=("parallel",)),
        name="gate_gelu",
    )(hid, gate, after)


def _final_kernel(x_ref, p_ref, g_ref, *rest):
    rest[-1][...] = _rms(x_ref[...] + p_ref[...], g_ref[...])


def _final(x, p, g, tb, t0, into=None, after=None):
    b0 = t0 // tb
    rows = pl.BlockSpec((tb, D_MODEL), lambda i: (i + b0, 0))
    in_specs = [rows, pl.BlockSpec((tb, D_MODEL), lambda i: (i, 0)), _const_spec(g.shape)]
    args = [x, p, g]
    aliases = {}
    if into is not None:
        aliases = {len(args): 0}
        in_specs.append(pl.BlockSpec(memory_space=pl.ANY))
        args.append(into)
    if after is not None:
        in_specs.append(_after_spec(after))
        args.append(after)
    return pl.pallas_call(
        _final_kernel, out_shape=jax.ShapeDtypeStruct(x.shape, F32), grid=(p.shape[0] // tb,),
        in_specs=in_specs, out_specs=rows, input_output_aliases=aliases,
        compiler_params=pltpu.CompilerParams(dimension_semantics=("parallel",), vmem_limit_bytes=VMEM_LIMIT),
        name="final_norm",
    )(*args)


def _pack_table(tab):
    bits = lax.bitcast_convert_type(tab.astype(BF16), jnp.uint16).astype(jnp.uint32)
    packed = (bits[HALF_EXPERTS:] << 16) | bits[:HALF_EXPERTS]
    return lax.bitcast_convert_type(packed, jnp.int32).reshape(HALF_EXPERTS * SUBLANES, LANES)


def _col(v, n=LANES):
    return jnp.pad(v.astype(F32), (0, n - v.shape[0])).reshape(1, n)


def _row(v, n=2 * SUBLANES):
    return jnp.pad(v.astype(F32), (0, n - v.shape[0])).reshape(n, 1)


def _peer(x, w, tb_sel, tb_exp, n_sc):
    t = x.shape[0]
    n_tc = t - n_sc
    if n_sc:
        xn_s, eid_s, _, _, gate_s = _select(x, w["ln_ffn"], w["w_pq"], w["sub_keys"], tb_sel, 0, n_sc)
        hid_s = _sc_hidden(eid_s, xn_s, w["u_tab"])
    xn, _, tile, shift, gate = _select(x, w["ln_ffn"], w["w_pq"], w["sub_keys"], tb_sel, n_sc, n_tc,
                                       after=xn_s if n_sc else None)
    gh = _hidden(tile, shift, xn, gate, w["u_pack"], tb_exp)
    if n_sc:
        out_s = _sc_combine(eid_s, _gate_gelu(hid_s, gate_s, tb_sel, after=gh), w["v_tab"])
    out = _combine(tile, shift, gh, w["v_pack"], tb_exp)
    y = _final(x, out, w["final_norm"], tb_sel, n_sc)
    if n_sc:
        y = _final(x, out_s, w["final_norm"], tb_sel, 0, into=y, after=out)
    return y


def kernel(x_prompt, x_sample, mem_prompt, cache_conv_a, cache_conv_ssd, state_ssd, cache_mem_k, cache_mem_v, ln_mix_w, w_in, conv_a_w, conv_s_w, conv_s_b, dt_bias, a_log, d_skip, ssd_norm_w, w_out, ln_mem_w, mem_norm_w, w_mq, w_mk, w_mv, w_mo, ln_ffn_w, w_pq, sub_keys, u_tab, v_tab, final_norm_w):
    depth = w_in.shape[0]
    assert depth == 1
    l = 0
    n_main = 3 * D_CONV + D_SSD + XBC_DIM
    w_dt = w_in[l][:, n_main:]
    w = {
        "ln_mix": ln_mix_w[l].reshape(1, D_MODEL),
        "w_in": w_in[l][:, :n_main].astype(BF16),
        "w_dt": jnp.pad(w_dt, ((0, 0), (0, LANES - SSD_HEADS))).astype(BF16),
        "w_dtt": jnp.pad(w_dt.T, ((0, 2 * SUBLANES - SSD_HEADS), (0, 0))).astype(BF16),
        "conv_a_w": conv_a_w[l], "conv_s_w": conv_s_w[l], "conv_s_b": conv_s_b[l].reshape(1, XBC_DIM),
        "dtb_c": _col(dt_bias[l]), "alog_c": _col(a_log[l]), "dsk_c": _col(d_skip[l]),
        "dtb_r": _row(dt_bias[l]), "alog_r": _row(a_log[l]),
        "ssd_norm": ssd_norm_w[l].reshape(1, D_SSD),
        "w_out": w_out[l].astype(BF16),
        "ln_ffn": ln_ffn_w[l].reshape(1, D_MODEL),
        "w_pq": w_pq[l].astype(BF16).reshape(D_MODEL, PEER_HEADS, PEER_DK).transpose(1, 0, 2),
        "sub_keys": sub_keys[l].astype(BF16),
        "u_pack": _pack_table(u_tab[l]),
        "v_pack": _pack_table(v_tab[l]),
        "u_tab": u_tab[l], "v_tab": v_tab[l],
        "final_norm": final_norm_w.reshape(1, D_MODEL),
    }
    ln_mem = ln_mem_w[l].reshape(1, D_MODEL)
    wq, wo = w_mq[l].astype(BF16), w_mo[l].astype(BF16)

    bp, sp, _ = x_prompt.shape
    bs, ss, _ = x_sample.shape

    mk, mv = _memory_kv(mem_prompt.reshape(bp * N_MEM, D_MODEL), mem_norm_w[l].reshape(1, D_MODEL),
                        w_mk[l].astype(BF16), w_mv[l].astype(BF16))
    hp, pa, ps, ph = _mixer(x_prompt, None, w, 256)
    hp = _attn(hp, mk.reshape(bp, N_MEM, D_MODEL), mv.reshape(bp, N_MEM, D_MODEL), ln_mem, wq, wo, 256)
    y_prompt = _peer(hp.reshape(bp * sp, D_MODEL), w, 256, 64, SC_PROMPT_TOKENS).reshape(bp, sp, D_MODEL)

    hs, sa, ssd, sh = _mixer(x_sample, (cache_conv_a[l], cache_conv_ssd[l],
                                        state_ssd[l].reshape((bs,) + PAIR_STATE)), w, 256)
    hs = _attn(hs, cache_mem_k[l].reshape(bs, N_MEM, D_MODEL), cache_mem_v[l].reshape(bs, N_MEM, D_MODEL),
               ln_mem, wq, wo, 256)
    y_sample = _peer(hs.reshape(bs * ss, D_MODEL), w, 256, 64, 0).reshape(bs, ss, D_MODEL)

    kv_shape = (1, bp, N_MEM, MEM_HEADS, MEM_HEAD_DIM)
    st_shape = (SSD_HEADS, SSD_HEAD_DIM, SSD_STATE)
    return (y_prompt, y_sample, pa[None], ps[None], ph.reshape((1, bp) + st_shape), mk.reshape(kv_shape),
            mv.reshape(kv_shape), sa[None], ssd[None], sh.reshape((1, bs) + st_shape))
```

```python
import functools
import math

import jax
import jax.numpy as jnp
from jax import lax
from jax.experimental import pallas as pl
from jax.experimental.pallas import tpu as pltpu
from jax.experimental.pallas import tpu_sc as plsc

D_MODEL = 1024
CHUNK = 64
D_CONV = 512
CONV_A_W = 3
D_SSD = 512
SSD_HEAD_DIM = 64
SSD_HEADS = 8
SSD_GROUPS = 2
SSD_HPG = 4
SSD_STATE = 128
SSD_CONV_W = 4
XBC_DIM = 1024
N_MEM = 256
MEM_HEADS = 4
MEM_HEAD_DIM = 256
PEER_HEADS = 8
N_KEYS = 128
N_EXPERTS = N_KEYS * N_KEYS
PEER_TOPK = 16
PEER_DK = 256
PEER_DK_HALF = 128
PICKS = PEER_HEADS * PEER_TOPK
EPS = 1e-6

LANES = 128
SUBLANES = 8
HALF_EXPERTS = N_EXPERTS // 2
PAIR_STATE = (SSD_HEADS // 2, 2 * SSD_HEAD_DIM, SSD_STATE)
VMEM_LIMIT = 56 * 1024 * 1024

F32 = jnp.float32
BF16 = jnp.bfloat16
HI = lax.Precision.HIGHEST


def _rms(x, g):
    return x * lax.rsqrt(jnp.mean(x * x, axis=-1, keepdims=True) + EPS) * g


def _silu(x):
    return x * (1.0 / (1.0 + jnp.exp(-x)))


def _softplus(x):
    return jnp.maximum(x, 0.0) + jnp.log1p(jnp.exp(-jnp.abs(x)))


def _dot(a, b):
    return jnp.dot(a.astype(BF16), b.astype(BF16), preferred_element_type=F32)


def _dot_nt(a, b):
    return lax.dot_general(a.astype(BF16), b.astype(BF16), (((1,), (1,)), ((), ())),
                           preferred_element_type=F32)


def _dot_tn(a, b):
    return lax.dot_general(a.astype(BF16), b.astype(BF16), (((0,), (0,)), ((), ())),
                           preferred_element_type=F32)


def _const_spec(shape):
    n = len(shape)
    return pl.BlockSpec(shape, lambda *_: (0,) * n)


def _mixer_kernel(has_hist, ts, lc, *refs):
    if has_hist:
        x_ref, hista_ref, hists_ref, h0_ref = refs[:4]
        refs = refs[4:]
    else:
        x_ref = refs[0]
        refs = refs[1:]
    (lnw_ref, win_ref, wdt_ref, wdtt_ref, caw_ref, csw_ref, csb_ref, dtb_c_ref, alog_c_ref, dsk_c_ref,
     dtb_r_ref, alog_r_ref, nw_ref, wout_ref,
     h1_ref, na_ref, ns_ref, nh_ref, cata, cats, hst) = refs
    s = pl.program_id(1)
    pad = SUBLANES

    @pl.when(s == 0)
    def _():
        if has_hist:
            cata[pad - 2:pad, :] = hista_ref[...]
            cats[pad - 3:pad, :] = hists_ref[...]
            hst[...] = h0_ref[...]
        else:
            cata[0:pad, :] = jnp.zeros((pad, D_CONV), F32)
            cats[0:pad, :] = jnp.zeros((pad, XBC_DIM), F32)
            hst[...] = jnp.zeros(hst.shape, F32)

    x = x_ref[...]
    xn = _rms(x, lnw_ref[...]).astype(BF16)
    proj = jnp.dot(xn, win_ref[...], preferred_element_type=F32)
    dt_c = jnp.dot(xn, wdt_ref[...], preferred_element_type=F32)
    dt_r = lax.dot_general(wdtt_ref[...], xn, (((1,), (1,)), ((), ())),
                           preferred_element_type=F32)
    g_b = proj[:, 0:D_CONV]
    g_c = proj[:, D_CONV:2 * D_CONV]
    v_in = proj[:, 2 * D_CONV:3 * D_CONV]
    z = proj[:, 3 * D_CONV:3 * D_CONV + D_SSD]
    xbc = proj[:, 3 * D_CONV + D_SSD:3 * D_CONV + D_SSD + XBC_DIM]

    ua = g_c * v_in
    cata[pad:pad + ts, :] = ua
    caw = caw_ref[...]
    conv_a = (caw[0:1] * cata[pad - 2:pad - 2 + ts, :] + caw[1:2] * cata[pad - 1:pad - 1 + ts, :]
              + caw[2:3] * ua)
    y_a = g_b * conv_a
    hist_a = cata[pad + ts - 2:pad + ts, :]
    na_ref[...] = hist_a
    cata[pad - 2:pad, :] = hist_a

    cats[pad:pad + ts, :] = xbc
    csw = csw_ref[...]
    conv_s = (csw[0:1] * cats[pad - 3:pad - 3 + ts, :] + csw[1:2] * cats[pad - 2:pad - 2 + ts, :]
              + csw[2:3] * cats[pad - 1:pad - 1 + ts, :] + csw[3:4] * xbc)
    hist_s = cats[pad + ts - 3:pad + ts, :]
    ns_ref[...] = hist_s
    cats[pad - 3:pad, :] = hist_s
    xbc_c = _silu(conv_s + csb_ref[...])
    xs = xbc_c[:, 0:D_SSD]

    dtp_c = _softplus(dt_c + dtb_c_ref[...])
    dtp_r = _softplus(dt_r + dtb_r_ref[...])
    a_c = dtp_c * (-jnp.exp(alog_c_ref[...]))
    a_r = dtp_r * (-jnp.exp(alog_r_ref[...]))
    dsk_c = dsk_c_ref[...]

    lane = lax.broadcasted_iota(jnp.int32, (1, LANES), 1)
    first_head = lane < SSD_HEAD_DIM
    row2 = lax.broadcasted_iota(jnp.int32, (2 * SSD_HEAD_DIM, 1), 0) < SSD_HEAD_DIM
    li = lax.broadcasted_iota(jnp.int32, (lc, lc), 0)
    si = lax.broadcasted_iota(jnp.int32, (lc, lc), 1)
    causal = si <= li
    tril = causal.astype(F32)
    jrow = lax.broadcasted_iota(jnp.int32, (ts, lc), 0)
    scol = lax.broadcasted_iota(jnp.int32, (ts, lc), 1)

    def pair(col, h):
        return jnp.where(first_head, col[:, h:h + 1], col[:, h + 1:h + 2])

    y_chunks = []
    for c in range(ts // lc):
        r0 = c * lc
        rows = slice(r0, r0 + lc)
        acum_c = jnp.dot(tril, a_c[rows], preferred_element_type=F32, precision=HI)
        upper = ((jrow >= r0) & (jrow <= r0 + scol)).astype(F32)
        acum_r = jnp.dot(a_r, upper, preferred_element_type=F32, precision=HI)
        last_c = acum_c[lc - 1:lc, :]
        dec_c = jnp.exp(last_c - acum_c)
        eac_c = jnp.exp(acum_c)
        y_pairs = []
        for g in range(SSD_GROUPS):
            b_g = xbc_c[rows, D_SSD + g * SSD_STATE:D_SSD + (g + 1) * SSD_STATE]
            c_g = xbc_c[rows, D_SSD + (SSD_GROUPS + g) * SSD_STATE:D_SSD + (SSD_GROUPS + g + 1) * SSD_STATE]
            cb = _dot_nt(c_g, b_g)
            for q in range(SSD_HPG // 2):
                h = g * SSD_HPG + 2 * q
                pi = h // 2
                xs_p = xs[rows, pi * LANES:(pi + 1) * LANES]
                xdt = xs_p * pair(dtp_c[rows], h)
                res = []
                for hh in (h, h + 1):
                    seg = acum_c[:, hh:hh + 1] - acum_r[hh:hh + 1, :]
                    m_h = cb * jnp.exp(jnp.where(causal, seg, -jnp.inf))
                    res.append(_dot(m_h, xdt))
                y_diag = jnp.where(first_head, res[0], res[1])
                h_in = hst[pi]
                y_off = _dot_nt(c_g, h_in) * pair(eac_c, h)
                st = _dot_tn(xdt * pair(dec_c, h), b_g)
                cd = jnp.exp(jnp.where(row2, last_c[:, h:h + 1], last_c[:, h + 1:h + 2]))
                hst[pi] = cd * h_in + st
                y_pairs.append(y_diag + y_off + pair(dsk_c, h) * xs_p)
        y_chunks.append(jnp.concatenate(y_pairs, axis=1))
    y = y_chunks[0] if len(y_chunks) == 1 else jnp.concatenate(y_chunks, axis=0)
    y_b = _rms(y * _silu(z), nw_ref[...])
    wout = wout_ref[...]
    out = (jnp.dot(y_a.astype(BF16), wout[0:D_CONV], preferred_element_type=F32)
           + jnp.dot(y_b.astype(BF16), wout[D_CONV:], preferred_element_type=F32))
    h1_ref[...] = x + out
    nh_ref[...] = hst[...]


def _mixer(x, hist, w, ts):
    b, s, _ = x.shape
    lc = min(CHUNK, s)
    ts = min(ts, s)
    has_hist = hist is not None
    per_b3 = lambda shape: pl.BlockSpec((None,) + shape, lambda i, j: (i,) + (0,) * len(shape))
    in_specs = [pl.BlockSpec((None, ts, D_MODEL), lambda i, j: (i, j, 0))]
    args = [x]
    if has_hist:
        in_specs += [per_b3((CONV_A_W - 1, D_CONV)), per_b3((SSD_CONV_W - 1, XBC_DIM)),
                     per_b3(PAIR_STATE)]
        args += list(hist)
    wargs = [w["ln_mix"], w["w_in"], w["w_dt"], w["w_dtt"], w["conv_a_w"], w["conv_s_w"], w["conv_s_b"],
             w["dtb_c"], w["alog_c"], w["dsk_c"], w["dtb_r"], w["alog_r"], w["ssd_norm"], w["w_out"]]
    in_specs += [_const_spec(a.shape) for a in wargs]
    args += wargs
    out_shape = (jax.ShapeDtypeStruct((b, s, D_MODEL), F32),
                 jax.ShapeDtypeStruct((b, CONV_A_W - 1, D_CONV), F32),
                 jax.ShapeDtypeStruct((b, SSD_CONV_W - 1, XBC_DIM), F32),
                 jax.ShapeDtypeStruct((b,) + PAIR_STATE, F32))
    out_specs = (pl.BlockSpec((None, ts, D_MODEL), lambda i, j: (i, j, 0)),
                 per_b3((CONV_A_W - 1, D_CONV)), per_b3((SSD_CONV_W - 1, XBC_DIM)),
                 per_b3(PAIR_STATE))
    return pl.pallas_call(
        functools.partial(_mixer_kernel, has_hist, ts, lc),
        out_shape=out_shape, grid=(b, s // ts), in_specs=in_specs, out_specs=out_specs,
        scratch_shapes=[pltpu.VMEM((SUBLANES + ts, D_CONV), F32), pltpu.VMEM((SUBLANES + ts, XBC_DIM), F32),
                        pltpu.VMEM(PAIR_STATE, F32)],
        compiler_params=pltpu.CompilerParams(dimension_semantics=("parallel", "arbitrary"),
                                             vmem_limit_bytes=VMEM_LIMIT),
        name="mixer_hist" if has_hist else "mixer",
    )(*args)


def _memkv_kernel(m_ref, g_ref, wk_ref, wv_ref, k_ref, v_ref):
    mn = _rms(m_ref[...], g_ref[...]).astype(BF16)
    k_ref[...] = jnp.dot(mn, wk_ref[...], preferred_element_type=F32)
    v_ref[...] = jnp.dot(mn, wv_ref[...], preferred_element_type=F32)


def _memory_kv(mem, g, wk, wv, tm=256):
    t = mem.shape[0]
    blk = pl.BlockSpec((tm, D_MODEL), lambda i: (i, 0))
    return pl.pallas_call(
        _memkv_kernel,
        out_shape=(jax.ShapeDtypeStruct((t, D_MODEL), F32),) * 2,
        grid=(t // tm,), in_specs=[blk, _const_spec(g.shape), _const_spec(wk.shape), _const_spec(wv.shape)],
        out_specs=(blk, blk),
        compiler_params=pltpu.CompilerParams(dimension_semantics=("parallel",), vmem_limit_bytes=VMEM_LIMIT),
        name="memory_kv",
    )(mem, g, wk, wv)


def _attn_kernel(x_ref, k_ref, v_ref, g_ref, wq_ref, wo_ref, o_ref):
    x = x_ref[...]
    xn = _rms(x, g_ref[...]).astype(BF16)
    q = jnp.dot(xn, wq_ref[...], preferred_element_type=F32)
    by_head = len(k_ref.shape) == 3
    wo = wo_ref[...]
    acc = x
    for h in range(MEM_HEADS):
        cols = slice(h * MEM_HEAD_DIM, (h + 1) * MEM_HEAD_DIM)
        k_h = k_ref[:, h, :] if by_head else k_ref[:, cols]
        v_h = v_ref[:, h, :] if by_head else v_ref[:, cols]
        sc = _dot_nt(q[:, cols], k_h) * (MEM_HEAD_DIM ** -0.5)
        sc = sc - jnp.max(sc, axis=-1, keepdims=True)
        p = jnp.exp(sc)
        p = p / jnp.sum(p, axis=-1, keepdims=True)
        o_h = _dot(p, v_h)
        acc = acc + jnp.dot(o_h.astype(BF16), wo[cols, :], preferred_element_type=F32)
    o_ref[...] = acc


def _attn(x, k, v, g, wq, wo, ts):
    b, s, _ = x.shape
    ts = min(ts, s)
    xblk = pl.BlockSpec((None, ts, D_MODEL), lambda i, j: (i, j, 0))
    if k.ndim == 3:
        kvblk = pl.BlockSpec((None, N_MEM, D_MODEL), lambda i, j: (i, 0, 0))
    else:
        kvblk = pl.BlockSpec((None, N_MEM, MEM_HEADS, MEM_HEAD_DIM), lambda i, j: (i, 0, 0, 0))
    return pl.pallas_call(
        _attn_kernel,
        out_shape=jax.ShapeDtypeStruct((b, s, D_MODEL), F32),
        grid=(b, s // ts),
        in_specs=[xblk, kvblk, kvblk, _const_spec(g.shape), _const_spec(wq.shape), _const_spec(wo.shape)],
        out_specs=xblk,
        compiler_params=pltpu.CompilerParams(dimension_semantics=("parallel", "parallel"),
                                             vmem_limit_bytes=VMEM_LIMIT),
        name="attn",
    )(x, k, v, g, wq, wo)


def _top_k_rows(sc, k, payload=None):
    n = sc.shape[0]
    rows = lax.broadcasted_iota(jnp.int32, sc.shape, 0).astype(F32)
    vals, outs = [], []
    for _ in range(k):
        m = jnp.max(sc, axis=0, keepdims=True)
        cand = jnp.where(sc == m, rows, float(n))
        first = jnp.min(cand, axis=0, keepdims=True)
        hit = cand == first
        sc = jnp.where(hit, -jnp.inf, sc)
        vals.append(m)
        if payload is None:
            outs.append(first.astype(jnp.int32))
        else:
            outs.append(jnp.sum(jnp.where(hit, payload, 0), axis=0, keepdims=True))
    return jnp.concatenate(vals, axis=0), jnp.concatenate(outs, axis=0)


_LIVE_B = tuple(PEER_TOPK // (a + 1) for a in range(1, SUBLANES))


def _pair_candidates(v1, i1, v2, i2):
    sub = lax.broadcasted_iota(jnp.int32, (SUBLANES, v1.shape[1]), 0)
    comb = [v1[0:1] + v2]
    cidx = [i1[0:1] * N_KEYS + i2]
    for a in range(1, SUBLANES):
        comb.append(jnp.where(sub < _LIVE_B[a - 1], v1[a:a + 1] + v2[0:SUBLANES], -jnp.inf))
        cidx.append(i1[a:a + 1] * N_KEYS + i2[0:SUBLANES])
    comb.append(v1[SUBLANES:] + v2[0:1])
    cidx.append(i1[SUBLANES:] * N_KEYS + i2[0:1])
    return jnp.concatenate(comb, axis=0), jnp.concatenate(cidx, axis=0)


def _select_kernel(tb, x_ref, g_ref, wq_ref, keys_ref, *rest):
    xn_ref, eid_ref, tile_ref, shift_ref, gate_ref, eid_s, gate_s = rest[-7:]
    xn = _rms(x_ref[...], g_ref[...])
    xn_ref[...] = xn
    xn = xn.astype(BF16)

    def head(h, carry):
        q = jnp.dot(xn, wq_ref[h], preferred_element_type=F32)
        sc1 = _dot_nt(keys_ref[h, 0], q[:, 0:PEER_DK_HALF])
        sc2 = _dot_nt(keys_ref[h, 1], q[:, PEER_DK_HALF:PEER_DK])
        rows = pl.ds(pl.multiple_of(h * PEER_TOPK, PEER_TOPK), PEER_TOPK)
        for c0 in range(0, tb, LANES):
            cols = slice(c0, c0 + LANES)
            v1, i1 = _top_k_rows(sc1[:, cols], PEER_TOPK)
            v2, i2 = _top_k_rows(sc2[:, cols], PEER_TOPK)
            comb, cidx = _pair_candidates(v1, i1, v2, i2)
            top, eid = _top_k_rows(comb, PEER_TOPK, cidx)
            ex = jnp.exp(top - top[0:1])
            gate_s[rows, cols] = ex / jnp.sum(ex, axis=0, keepdims=True)
            eid_s[rows, cols] = eid
        return carry

    lax.fori_loop(0, PEER_HEADS, head, 0)
    for c0 in range(0, tb, LANES):
        toks = slice(c0, c0 + LANES)
        eid = eid_s[:, toks].T
        upper = eid >= HALF_EXPERTS
        eid_ref[toks, :] = eid
        tile_ref[toks, :] = jnp.where(upper, eid - HALF_EXPERTS, eid) * SUBLANES
        shift_ref[toks, :] = jnp.where(upper, 0, 16)
        gate_ref[toks, :] = gate_s[:, toks].T


def _select(x, g, wq, keys, tb, t0, t, after=None):
    b0 = t0 // tb
    extra = [] if after is None else [after]
    xin = pl.BlockSpec((tb, D_MODEL), lambda i: (i + b0, 0))
    xblk = pl.BlockSpec((tb, D_MODEL), lambda i: (i, 0))
    picks_i = jax.ShapeDtypeStruct((t, PICKS), jnp.int32)
    return pl.pallas_call(
        functools.partial(_select_kernel, tb),
        out_shape=(jax.ShapeDtypeStruct((t, D_MODEL), F32), picks_i, picks_i, picks_i,
                   jax.ShapeDtypeStruct((t, PICKS), F32)),
        grid=(t // tb,),
        in_specs=[xin, _const_spec(g.shape), _const_spec(wq.shape), _const_spec(keys.shape)]
        + [_after_spec(a) for a in extra],
        out_specs=(xblk,) + (_pick_block(tb),) * 4,
        scratch_shapes=[pltpu.VMEM((PICKS, tb), jnp.int32), pltpu.VMEM((PICKS, tb), F32)],
        compiler_params=pltpu.CompilerParams(dimension_semantics=("parallel",), vmem_limit_bytes=VMEM_LIMIT),
        name="peer_select",
    )(x, g, wq, keys, *extra)


HI_HALF_MASK = -(1 << 16)
_BITREV3 = (0, 4, 2, 6, 1, 5, 3, 7)


def _unpack(word, shift):
    return lax.bitcast_convert_type((word << shift) & HI_HALF_MASK, F32)


def _splat_picks(blk, one_pass):
    tb = blk.shape[0]
    r = lax.broadcasted_iota(jnp.int32, (PICKS, LANES), 0)
    c = lax.broadcasted_iota(jnp.int32, (PICKS, LANES), 1)
    eye = (r == c)[None]
    ones = jnp.ones((LANES, LANES), BF16)

    def one(piece):
        diag = jnp.where(eye, piece[:, None, :], 0.0).astype(BF16).reshape(tb * PICKS, LANES)
        return jnp.dot(diag, ones, preferred_element_type=F32)

    if one_pass:
        out = one(blk)
    else:
        hi = blk.astype(BF16).astype(F32)
        rest = blk - hi
        mid = rest.astype(BF16).astype(F32)
        out = one(hi) + one(mid) + one(rest - mid)
    return out.reshape(tb, PICKS, LANES)


def _full(ref, k):
    return jnp.broadcast_to(ref[k:k + 1, :], (SUBLANES, LANES))


def _expert_row(tab_ref, tile_ref, t, k, shifts):
    idx = pl.multiple_of(tile_ref[t, k], SUBLANES)
    return _unpack(tab_ref[pl.ds(idx, SUBLANES), :], _full(shifts, k))


def _hidden_kernel(tb, tile_ref, shift_ref, x_ref, gate_ref, tab_ref, gh_ref, bsh, stage, parts, xt):
    bsh[...] = _splat_picks(shift_ref[...].astype(F32), True).astype(jnp.int32)
    for s in range(SUBLANES):
        xt[pl.ds(s, tb, stride=SUBLANES), :] = x_ref[:, s * LANES:(s + 1) * LANES]

    half = PICKS // 2
    quad = SUBLANES // 2
    low = lax.broadcasted_iota(jnp.int32, (SUBLANES, LANES), 0) < quad

    def token(t, carry):
        x = xt[pl.ds(pl.multiple_of(t * SUBLANES, SUBLANES), SUBLANES), :]
        shifts = bsh.at[t]
        for j in range(half):
            a = x * _expert_row(tab_ref, tile_ref, t, j, shifts)
            b = x * _expert_row(tab_ref, tile_ref, t, j + half, shifts)
            stage[j * SUBLANES:(j + 1) * SUBLANES, :] = jnp.where(low, a + pltpu.roll(a, quad, 0),
                                                                  b + pltpu.roll(b, quad, 0))
        for p in range(2):
            part = stage[pl.ds(p * quad, half, stride=SUBLANES), :]
            for s in range(1, quad):
                part = part + stage[pl.ds(p * quad + s, half, stride=SUBLANES), :]
            parts[t, p * half:(p + 1) * half, :] = part
        return carry

    lax.fori_loop(0, tb, token, 0)
    gh_ref[...] = jnp.sum(parts[...], axis=-1)
    hid = gh_ref[...]
    gelu = 0.5 * hid * (1.0 + lax.erf(hid * (2.0 ** -0.5)))
    gh_ref[...] = gate_ref[...] * gelu


def _pick_block(tb, space=None):
    return pl.BlockSpec((tb, PICKS), lambda i: (i, 0), memory_space=space)


def _table_spec(tab):
    return pl.BlockSpec(tab.shape, lambda i: (0, 0), pipeline_mode=pl.Buffered(1))


def _hidden(tile, shift, xg, gate, tab, tb):
    t = xg.shape[0]
    return pl.pallas_call(
        functools.partial(_hidden_kernel, tb),
        out_shape=jax.ShapeDtypeStruct((t, PICKS), F32),
        grid=(t // tb,),
        in_specs=[_pick_block(tb, pltpu.SMEM), _pick_block(tb),
                  pl.BlockSpec((tb, D_MODEL), lambda i: (i, 0)),
                  _pick_block(tb), _table_spec(tab)],
        out_specs=_pick_block(tb),
        scratch_shapes=[pltpu.VMEM((tb, PICKS, LANES), jnp.int32), pltpu.VMEM((PICKS // 2 * SUBLANES, LANES), F32),
                        pltpu.VMEM((tb, PICKS, LANES), F32), pltpu.VMEM((tb * SUBLANES, LANES), F32)],
        compiler_params=pltpu.CompilerParams(dimension_semantics=("parallel",), vmem_limit_bytes=VMEM_LIMIT),
        name="peer_hidden",
    )(tile, shift, xg, gate, tab)


def _combine_kernel(tb, tile_ref, shift_ref, gh_ref, tab_ref, x_ref, g_ref, o_ref, bsh, bg, ot):
    bsh[...] = _splat_picks(shift_ref[...].astype(F32), True).astype(jnp.int32)
    bg[...] = _splat_picks(gh_ref[...], False)

    def token(t, carry):
        shifts, gates = bsh.at[t], bg.at[t]
        acc = [jnp.zeros((SUBLANES, LANES), F32) for _ in range(4)]
        for k in range(PICKS):
            acc[k % 4] = acc[k % 4] + _full(gates, k) * _expert_row(tab_ref, tile_ref, t, k, shifts)
        ot[pl.ds(pl.multiple_of(t * SUBLANES, SUBLANES), SUBLANES), :] = (acc[0] + acc[1]) + (acc[2] + acc[3])
        return carry

    lax.fori_loop(0, tb, token, 0)
    for s in range(SUBLANES):
        o_ref[:, s * LANES:(s + 1) * LANES] = ot[pl.ds(s, tb, stride=SUBLANES), :]
    o_ref[...] = _rms(x_ref[...] + o_ref[...], g_ref[...])


def _combine(tile, shift, gh, tab, x, g, tb, t0):
    t = gh.shape[0]
    b0 = t0 // tb
    rows = pl.BlockSpec((tb, D_MODEL), lambda i: (i + b0, 0))
    return pl.pallas_call(
        functools.partial(_combine_kernel, tb),
        out_shape=jax.ShapeDtypeStruct(x.shape, F32),
        grid=(t // tb,),
        in_specs=[_pick_block(tb, pltpu.SMEM), _pick_block(tb), _pick_block(tb), _table_spec(tab), rows,
                  _const_spec(g.shape)],
        out_specs=rows,
        scratch_shapes=[pltpu.VMEM((tb, PICKS, LANES), jnp.int32), pltpu.VMEM((tb, PICKS, LANES), F32),
                        pltpu.VMEM((tb * SUBLANES, LANES), F32)],
        compiler_params=pltpu.CompilerParams(dimension_semantics=("parallel",), vmem_limit_bytes=VMEM_LIMIT),
        name="peer_combine",
    )(tile, shift, gh, tab, x, g)


SC_LANES = 16
SC_WORKERS = 32
SC_CHUNK = 32
SC_FEATURE_BLOCK = 256
SC_PROMPT_TOKENS = 17152


def _sc_call(body, n, side_width, out_width):
    mesh = plsc.VectorSubcoreMesh(core_axis_name="c", subcore_axis_name="s")
    return pl.kernel(body, mesh=mesh, out_type=jax.ShapeDtypeStruct((n, out_width), F32),
                     compiler_params=pltpu.CompilerParams(needs_layout_passes=False),
                     scratch_types=[pltpu.VMEM((2 * PICKS,), jnp.int32), pltpu.VMEM((2 * side_width,), F32),
                                    pltpu.VMEM((2 * out_width,), F32),
                                    pltpu.VMEM((SC_CHUNK, D_MODEL), F32), pltpu.VMEM((SC_CHUNK, D_MODEL), F32)]
                     + [pltpu.SemaphoreType.DMA] * 5)


def _sc_token_loop(n, eid_hbm, side_hbm, tab_hbm, out_hbm, idx_v, side_v, out_v, bufs, sems, compute):
    sem_rows, sem_fetch, sem_put = sems[0:2], sems[2], sems[3:5]
    per_worker = n // SC_WORKERS
    first = (lax.axis_index("s") * 2 + lax.axis_index("c")) * per_worker
    last = first + per_worker - 1
    chunks = PICKS // SC_CHUNK
    side_w = side_v.shape[0] // 2
    out_w = out_v.shape[0] // 2

    def fetch(t, slot):
        return (pltpu.make_async_copy(eid_hbm.at[t], idx_v.at[pl.ds(slot * PICKS, PICKS)], sem_fetch),
                pltpu.make_async_copy(side_hbm.at[t], side_v.at[pl.ds(slot * side_w, side_w)], sem_fetch))

    def gather(slot, c):
        ids = idx_v.at[pl.ds(slot * PICKS + c * SC_CHUNK, SC_CHUNK)]
        return pltpu.make_async_copy(tab_hbm.at[ids], bufs[c % 2], sem_rows[c % 2])

    def put(t, slot):
        return pltpu.make_async_copy(out_v.at[pl.ds(slot * out_w, out_w)], out_hbm.at[t], sem_put[slot])

    for cp in fetch(first, 0):
        cp.start()
    for cp in fetch(first, 0):
        cp.wait()
    gather(0, 0).start()
    gather(0, 1).start()

    @pl.loop(0, per_worker, step=2)
    def _(i):
        for slot in range(2):
            t = first + i + slot
            nxt = jnp.minimum(t + 1, last)
            for cp in fetch(nxt, 1 - slot):
                cp.start()

            @pl.when(i >= 2)
            def _():
                put(t, slot).wait()

            for c in range(chunks):
                gather(slot, c).wait()
                compute(c, bufs[c % 2], slot * side_w, slot * out_w)
                if c + 2 < chunks:
                    gather(slot, c + 2).start()
                else:
                    if c + 2 == chunks:
                        for cp in fetch(nxt, 1 - slot):
                            cp.wait()
                    gather(1 - slot, c + 2 - chunks).start()
            put(t, slot).start()

    gather(0, 0).wait()
    gather(0, 1).wait()
    put(last, 0).wait()
    put(last, 1).wait()


def _sc_hidden(eid, x, tab):
    n = eid.shape[0]
    lanes = SC_LANES

    def body(eid_hbm, x_hbm, tab_hbm, out_hbm, idx_v, x_v, hid_v, rows0, rows1, *sems):
        lane = lax.iota(jnp.int32, lanes)

        def compute(c, rows, x0, h0):
            for g in range(SC_CHUNK // lanes):
                def step(j, accs):
                    off = pl.multiple_of(j * lanes, lanes)
                    xj = x_v[pl.ds(x0 + off, lanes)]
                    return tuple(accs[r] + xj * rows[g * lanes + r, pl.ds(off, lanes)] for r in range(lanes))
                accs = lax.fori_loop(0, D_MODEL // lanes, step,
                                     tuple(jnp.zeros((lanes,), F32) for _ in range(lanes)))
                hv = jnp.zeros((lanes,), F32)
                for r in range(lanes):
                    hv = jnp.where(lane == r, jnp.sum(accs[r]), hv)
                hid_v[pl.ds(h0 + c * SC_CHUNK + g * lanes, lanes)] = hv

        _sc_token_loop(n, eid_hbm, x_hbm, tab_hbm, out_hbm, idx_v, x_v, hid_v, (rows0, rows1), sems, compute)

    return _sc_call(body, n, D_MODEL, PICKS)(eid, x, tab)


def _sc_combine(eid, gh, tab):
    n = eid.shape[0]
    lanes = SC_LANES
    nacc = SC_FEATURE_BLOCK // lanes

    def body(eid_hbm, g_hbm, tab_hbm, out_hbm, idx_v, g_v, out_v, rows0, rows1, *sems):
        def compute(c, rows, g0, o0):
            for fb in range(D_MODEL // SC_FEATURE_BLOCK):
                f0 = fb * SC_FEATURE_BLOCK
                if c == 0:
                    init = tuple(jnp.zeros((lanes,), F32) for _ in range(nacc))
                else:
                    init = tuple(out_v[pl.ds(o0 + f0 + q * lanes, lanes)] for q in range(nacc))

                def step(r, accs):
                    gk = plsc.load_gather(g_v, [jnp.full((lanes,), g0 + c * SC_CHUNK + r, jnp.int32)])
                    return tuple(accs[q] + gk * rows[r, pl.ds(f0 + q * lanes, lanes)] for q in range(nacc))
                accs = lax.fori_loop(0, SC_CHUNK, step, init)
                for q in range(nacc):
                    out_v[pl.ds(o0 + f0 + q * lanes, lanes)] = accs[q]

        _sc_token_loop(n, eid_hbm, g_hbm, tab_hbm, out_hbm, idx_v, g_v, out_v, (rows0, rows1), sems, compute)

    return _sc_call(body, n, PICKS, D_MODEL)(eid, gh, tab)


def _after_spec(after):
    return pl.BlockSpec((SUBLANES, after.shape[1]), lambda i: (0, 0))


def _gate_gelu_kernel(h_ref, g_ref, after_ref, o_ref):
    hid = h_ref[...]
    o_ref[...] = g_ref[...] * (0.5 * hid * (1.0 + lax.erf(hid * (2.0 ** -0.5))))


def _gate_gelu(hid, gate, tb, after):
    t = hid.shape[0]
    return pl.pallas_call(
        _gate_gelu_kernel, out_shape=jax.ShapeDtypeStruct((t, PICKS), F32), grid=(t // tb,),
        in_specs=[_pick_block(tb), _pick_block(tb), _after_spec(after)], out_specs=_pick_block(tb),
        compiler_params=pltpu.CompilerParams(dimension_semantics=("parallel",)),
        name="gate_gelu",
    )(hid, gate, after)


def _final_kernel(x_ref, p_ref, g_ref, *rest):
    rest[-1][...] = _rms(x_ref[...] + p_ref[...], g_ref[...])


def _final(x, p, g, tb, t0, into=None, after=None):
    b0 = t0 // tb
    rows = pl.BlockSpec((tb, D_MODEL), lambda i: (i + b0, 0))
    in_specs = [rows, pl.BlockSpec((tb, D_MODEL), lambda i: (i, 0)), _const_spec(g.shape)]
    args = [x, p, g]
    aliases = {}
    if into is not None:
        aliases = {len(args): 0}
        in_specs.append(pl.BlockSpec(memory_space=pl.ANY))
        args.append(into)
    if after is not None:
        in_specs.append(_after_spec(after))
        args.append(after)
    return pl.pallas_call(
        _final_kernel, out_shape=jax.ShapeDtypeStruct(x.shape, F32), grid=(p.shape[0] // tb,),
        in_specs=in_specs, out_specs=rows, input_output_aliases=aliases,
        compiler_params=pltpu.CompilerParams(dimension_semantics=("parallel",), vmem_limit_bytes=VMEM_LIMIT),
        name="final_norm",
    )(*args)


PACK_BLOCK = 64


def _pack_kernel(lo_ref, hi_ref, o_ref):
    def bf16_bits(x):
        return lax.bitcast_convert_type(x.astype(BF16).astype(F32), jnp.int32)
    word = bf16_bits(hi_ref[...]) | lax.shift_right_logical(bf16_bits(lo_ref[...]), 16)
    for s in range(SUBLANES):
        o_ref[pl.ds(s, PACK_BLOCK, stride=SUBLANES), :] = word[:, s * LANES:(s + 1) * LANES]


def _pack_table(tab):
    steps = HALF_EXPERTS // PACK_BLOCK
    return pl.pallas_call(
        _pack_kernel, out_shape=jax.ShapeDtypeStruct((HALF_EXPERTS * SUBLANES, LANES), jnp.int32), grid=(steps,),
        in_specs=[pl.BlockSpec((PACK_BLOCK, D_MODEL), lambda i: (i, 0)),
                  pl.BlockSpec((PACK_BLOCK, D_MODEL), lambda i: (i + steps, 0))],
        out_specs=pl.BlockSpec((PACK_BLOCK * SUBLANES, LANES), lambda i: (i, 0)),
        compiler_params=pltpu.CompilerParams(dimension_semantics=("parallel",)),
        name="pack_table",
    )(tab, tab)


def _col(v, n=LANES):
    return jnp.pad(v.astype(F32), (0, n - v.shape[0])).reshape(1, n)


def _row(v, n=2 * SUBLANES):
    return jnp.pad(v.astype(F32), (0, n - v.shape[0])).reshape(n, 1)


def _peer(x, w, tb_sel, tb_exp, n_sc):
    t = x.shape[0]
    n_tc = t - n_sc
    if n_sc:
        xn_s, eid_s, _, _, gate_s = _select(x, w["ln_ffn"], w["w_pq"], w["sub_keys"], tb_sel, 0, n_sc)
        hid_s = _sc_hidden(eid_s, xn_s, w["u_tab"])
    xn, _, tile, shift, gate = _select(x, w["ln_ffn"], w["w_pq"], w["sub_keys"], tb_sel, n_sc, n_tc,
                                       after=xn_s if n_sc else None)
    gh = _hidden(tile, shift, xn, gate, w["u_pack"], tb_exp)
    if n_sc:
        out_s = _sc_combine(eid_s, _gate_gelu(hid_s, gate_s, tb_sel, after=gh), w["v_tab"])
    y = _combine(tile, shift, gh, w["v_pack"], x, w["final_norm"], tb_exp, n_sc)
    if n_sc:
        y = _final(x, out_s, w["final_norm"], tb_sel, 0, into=y)
    return y


def kernel(x_prompt, x_sample, mem_prompt, cache_conv_a, cache_conv_ssd, state_ssd, cache_mem_k, cache_mem_v, ln_mix_w, w_in, conv_a_w, conv_s_w, conv_s_b, dt_bias, a_log, d_skip, ssd_norm_w, w_out, ln_mem_w, mem_norm_w, w_mq, w_mk, w_mv, w_mo, ln_ffn_w, w_pq, sub_keys, u_tab, v_tab, final_norm_w):
    depth = w_in.shape[0]
    assert depth == 1
    l = 0
    n_main = 3 * D_CONV + D_SSD + XBC_DIM
    w_dt = w_in[l][:, n_main:]
    w = {
        "ln_mix": ln_mix_w[l].reshape(1, D_MODEL),
        "w_in": w_in[l][:, :n_main].astype(BF16),
        "w_dt": jnp.pad(w_dt, ((0, 0), (0, LANES - SSD_HEADS))).astype(BF16),
        "w_dtt": jnp.pad(w_dt.T, ((0, 2 * SUBLANES - SSD_HEADS), (0, 0))).astype(BF16),
        "conv_a_w": conv_a_w[l], "conv_s_w": conv_s_w[l], "conv_s_b": conv_s_b[l].reshape(1, XBC_DIM),
        "dtb_c": _col(dt_bias[l]), "alog_c": _col(a_log[l]), "dsk_c": _col(d_skip[l]),
        "dtb_r": _row(dt_bias[l]), "alog_r": _row(a_log[l]),
        "ssd_norm": ssd_norm_w[l].reshape(1, D_SSD),
        "w_out": w_out[l].astype(BF16),
        "ln_ffn": ln_ffn_w[l].reshape(1, D_MODEL),
        "w_pq": w_pq[l].astype(BF16).reshape(D_MODEL, PEER_HEADS, PEER_DK).transpose(1, 0, 2),
        "sub_keys": sub_keys[l].astype(BF16),
        "u_pack": _pack_table(u_tab[l]),
        "v_pack": _pack_table(v_tab[l]),
        "u_tab": u_tab[l], "v_tab": v_tab[l],
        "final_norm": final_norm_w.reshape(1, D_MODEL),
    }
    ln_mem = ln_mem_w[l].reshape(1, D_MODEL)
    wq, wo = w_mq[l].astype(BF16), w_mo[l].astype(BF16)

    bp, sp, _ = x_prompt.shape
    bs, ss, _ = x_sample.shape

    mk, mv = _memory_kv(mem_prompt.reshape(bp * N_MEM, D_MODEL), mem_norm_w[l].reshape(1, D_MODEL),
                        w_mk[l].astype(BF16), w_mv[l].astype(BF16))
    hp, pa, ps, ph = _mixer(x_prompt, None, w, 256)
    hp = _attn(hp, mk.reshape(bp, N_MEM, D_MODEL), mv.reshape(bp, N_MEM, D_MODEL), ln_mem, wq, wo, 256)
    y_prompt = _peer(hp.reshape(bp * sp, D_MODEL), w, 256, 64, SC_PROMPT_TOKENS).reshape(bp, sp, D_MODEL)

    hs, sa, ssd, sh = _mixer(x_sample, (cache_conv_a[l], cache_conv_ssd[l],
                                        state_ssd[l].reshape((bs,) + PAIR_STATE)), w, 256)
    hs = _attn(hs, cache_mem_k[l], cache_mem_v[l], ln_mem, wq, wo, 256)
    y_sample = _peer(hs.reshape(bs * ss, D_MODEL), w, 256, 64, 0).reshape(bs, ss, D_MODEL)

    kv_shape = (1, bp, N_MEM, MEM_HEADS, MEM_HEAD_DIM)
    st_shape = (SSD_HEADS, SSD_HEAD_DIM, SSD_STATE)
    return (y_prompt, y_sample, pa[None], ps[None], ph.reshape((1, bp) + st_shape), mk.reshape(kv_shape),
            mv.reshape(kv_shape), sa[None], ssd[None], sh.reshape((1, bs) + st_shape))
```

```python
import functools
import math

import jax
import jax.numpy as jnp
from jax import lax
from jax.experimental import pallas as pl
from jax.experimental.pallas import tpu as pltpu
from jax.experimental.pallas import tpu_sc as plsc

D_MODEL = 1024
CHUNK = 64
D_CONV = 512
CONV_A_W = 3
D_SSD = 512
SSD_HEAD_DIM = 64
SSD_HEADS = 8
SSD_GROUPS = 2
SSD_HPG = 4
SSD_STATE = 128
SSD_CONV_W = 4
XBC_DIM = 1024
N_MEM = 256
MEM_HEADS = 4
MEM_HEAD_DIM = 256
PEER_HEADS = 8
N_KEYS = 128
N_EXPERTS = N_KEYS * N_KEYS
PEER_TOPK = 16
PEER_DK = 256
PEER_DK_HALF = 128
PICKS = PEER_HEADS * PEER_TOPK
EPS = 1e-6

LANES = 128
SUBLANES = 8
HALF_EXPERTS = N_EXPERTS // 2
PAIR_STATE = (SSD_HEADS // 2, 2 * SSD_HEAD_DIM, SSD_STATE)
VMEM_LIMIT = 56 * 1024 * 1024

F32 = jnp.float32
BF16 = jnp.bfloat16
HI = lax.Precision.HIGHEST


def _rms(x, g):
    return x * lax.rsqrt(jnp.mean(x * x, axis=-1, keepdims=True) + EPS) * g


def _silu(x):
    return x * (1.0 / (1.0 + jnp.exp(-x)))


def _softplus(x):
    return jnp.maximum(x, 0.0) + jnp.log1p(jnp.exp(-jnp.abs(x)))


def _dot(a, b):
    return jnp.dot(a.astype(BF16), b.astype(BF16), preferred_element_type=F32)


def _dot_nt(a, b):
    return lax.dot_general(a.astype(BF16), b.astype(BF16), (((1,), (1,)), ((), ())),
                           preferred_element_type=F32)


def _dot_tn(a, b):
    return lax.dot_general(a.astype(BF16), b.astype(BF16), (((0,), (0,)), ((), ())),
                           preferred_element_type=F32)


def _const_spec(shape):
    n = len(shape)
    return pl.BlockSpec(shape, lambda *_: (0,) * n)


def _mixer_kernel(has_hist, ts, lc, *refs):
    if has_hist:
        x_ref, hista_ref, hists_ref, h0_ref = refs[:4]
        refs = refs[4:]
    else:
        x_ref = refs[0]
        refs = refs[1:]
    (lnw_ref, win_ref, wdt_ref, wdtt_ref, caw_ref, csw_ref, csb_ref, dtb_c_ref, alog_c_ref, dsk_c_ref,
     dtb_r_ref, alog_r_ref, nw_ref, wout_ref,
     h1_ref, na_ref, ns_ref, nh_ref, cata, cats, hst) = refs
    s = pl.program_id(1)
    pad = SUBLANES

    @pl.when(s == 0)
    def _():
        if has_hist:
            cata[pad - 2:pad, :] = hista_ref[...]
            cats[pad - 3:pad, :] = hists_ref[...]
            hst[...] = h0_ref[...]
        else:
            cata[0:pad, :] = jnp.zeros((pad, D_CONV), F32)
            cats[0:pad, :] = jnp.zeros((pad, XBC_DIM), F32)
            hst[...] = jnp.zeros(hst.shape, F32)

    x = x_ref[...]
    xn = _rms(x, lnw_ref[...]).astype(BF16)
    proj = jnp.dot(xn, win_ref[...], preferred_element_type=F32)
    dt_c = jnp.dot(xn, wdt_ref[...], preferred_element_type=F32)
    dt_r = lax.dot_general(wdtt_ref[...], xn, (((1,), (1,)), ((), ())),
                           preferred_element_type=F32)
    g_b = proj[:, 0:D_CONV]
    g_c = proj[:, D_CONV:2 * D_CONV]
    v_in = proj[:, 2 * D_CONV:3 * D_CONV]
    z = proj[:, 3 * D_CONV:3 * D_CONV + D_SSD]
    xbc = proj[:, 3 * D_CONV + D_SSD:3 * D_CONV + D_SSD + XBC_DIM]

    ua = g_c * v_in
    cata[pad:pad + ts, :] = ua
    caw = caw_ref[...]
    conv_a = (caw[0:1] * cata[pad - 2:pad - 2 + ts, :] + caw[1:2] * cata[pad - 1:pad - 1 + ts, :]
              + caw[2:3] * ua)
    y_a = g_b * conv_a
    hist_a = cata[pad + ts - 2:pad + ts, :]
    na_ref[...] = hist_a
    cata[pad - 2:pad, :] = hist_a

    cats[pad:pad + ts, :] = xbc
    csw = csw_ref[...]
    conv_s = (csw[0:1] * cats[pad - 3:pad - 3 + ts, :] + csw[1:2] * cats[pad - 2:pad - 2 + ts, :]
              + csw[2:3] * cats[pad - 1:pad - 1 + ts, :] + csw[3:4] * xbc)
    hist_s = cats[pad + ts - 3:pad + ts, :]
    ns_ref[...] = hist_s
    cats[pad - 3:pad, :] = hist_s
    xbc_c = _silu(conv_s + csb_ref[...])
    xs = xbc_c[:, 0:D_SSD]

    dtp_c = _softplus(dt_c + dtb_c_ref[...])
    dtp_r = _softplus(dt_r + dtb_r_ref[...])
    a_c = dtp_c * (-jnp.exp(alog_c_ref[...]))
    a_r = dtp_r * (-jnp.exp(alog_r_ref[...]))
    dsk_c = dsk_c_ref[...]

    lane = lax.broadcasted_iota(jnp.int32, (1, LANES), 1)
    first_head = lane < SSD_HEAD_DIM
    row2 = lax.broadcasted_iota(jnp.int32, (2 * SSD_HEAD_DIM, 1), 0) < SSD_HEAD_DIM
    li = lax.broadcasted_iota(jnp.int32, (lc, lc), 0)
    si = lax.broadcasted_iota(jnp.int32, (lc, lc), 1)
    causal = si <= li
    tril = causal.astype(F32)
    jrow = lax.broadcasted_iota(jnp.int32, (ts, lc), 0)
    scol = lax.broadcasted_iota(jnp.int32, (ts, lc), 1)

    def pair(col, h):
        return jnp.where(first_head, col[:, h:h + 1], col[:, h + 1:h + 2])

    y_chunks = []
    for c in range(ts // lc):
        r0 = c * lc
        rows = slice(r0, r0 + lc)
        acum_c = jnp.dot(tril, a_c[rows], preferred_element_type=F32, precision=HI)
        upper = ((jrow >= r0) & (jrow <= r0 + scol)).astype(F32)
        acum_r = jnp.dot(a_r, upper, preferred_element_type=F32, precision=HI)
        last_c = acum_c[lc - 1:lc, :]
        dec_c = jnp.exp(last_c - acum_c)
        eac_c = jnp.exp(acum_c)
        y_pairs = []
        for g in range(SSD_GROUPS):
            b_g = xbc_c[rows, D_SSD + g * SSD_STATE:D_SSD + (g + 1) * SSD_STATE]
            c_g = xbc_c[rows, D_SSD + (SSD_GROUPS + g) * SSD_STATE:D_SSD + (SSD_GROUPS + g + 1) * SSD_STATE]
            cb = _dot_nt(c_g, b_g)
            for q in range(SSD_HPG // 2):
                h = g * SSD_HPG + 2 * q
                pi = h // 2
                xs_p = xs[rows, pi * LANES:(pi + 1) * LANES]
                xdt = xs_p * pair(dtp_c[rows], h)
                res = []
                for hh in (h, h + 1):
                    seg = acum_c[:, hh:hh + 1] - acum_r[hh:hh + 1, :]
                    m_h = cb * jnp.exp(jnp.where(causal, seg, -jnp.inf))
                    res.append(_dot(m_h, xdt))
                y_diag = jnp.where(first_head, res[0], res[1])
                h_in = hst[pi]
                y_off = _dot_nt(c_g, h_in) * pair(eac_c, h)
                st = _dot_tn(xdt * pair(dec_c, h), b_g)
                cd = jnp.exp(jnp.where(row2, last_c[:, h:h + 1], last_c[:, h + 1:h + 2]))
                hst[pi] = cd * h_in + st
                y_pairs.append(y_diag + y_off + pair(dsk_c, h) * xs_p)
        y_chunks.append(jnp.concatenate(y_pairs, axis=1))
    y = y_chunks[0] if len(y_chunks) == 1 else jnp.concatenate(y_chunks, axis=0)
    y_b = _rms(y * _silu(z), nw_ref[...])
    wout = wout_ref[...]
    out = (jnp.dot(y_a.astype(BF16), wout[0:D_CONV], preferred_element_type=F32)
           + jnp.dot(y_b.astype(BF16), wout[D_CONV:], preferred_element_type=F32))
    h1_ref[...] = x + out
    nh_ref[...] = hst[...]


def _mixer(x, hist, w, ts):
    b, s, _ = x.shape
    lc = min(CHUNK, s)
    ts = min(ts, s)
    has_hist = hist is not None
    per_b3 = lambda shape: pl.BlockSpec((None,) + shape, lambda i, j: (i,) + (0,) * len(shape))
    in_specs = [pl.BlockSpec((None, ts, D_MODEL), lambda i, j: (i, j, 0))]
    args = [x]
    if has_hist:
        in_specs += [per_b3((CONV_A_W - 1, D_CONV)), per_b3((SSD_CONV_W - 1, XBC_DIM)),
                     per_b3(PAIR_STATE)]
        args += list(hist)
    wargs = [w["ln_mix"], w["w_in"], w["w_dt"], w["w_dtt"], w["conv_a_w"], w["conv_s_w"], w["conv_s_b"],
             w["dtb_c"], w["alog_c"], w["dsk_c"], w["dtb_r"], w["alog_r"], w["ssd_norm"], w["w_out"]]
    in_specs += [_const_spec(a.shape) for a in wargs]
    args += wargs
    out_shape = (jax.ShapeDtypeStruct((b, s, D_MODEL), F32),
                 jax.ShapeDtypeStruct((b, CONV_A_W - 1, D_CONV), F32),
                 jax.ShapeDtypeStruct((b, SSD_CONV_W - 1, XBC_DIM), F32),
                 jax.ShapeDtypeStruct((b,) + PAIR_STATE, F32))
    out_specs = (pl.BlockSpec((None, ts, D_MODEL), lambda i, j: (i, j, 0)),
                 per_b3((CONV_A_W - 1, D_CONV)), per_b3((SSD_CONV_W - 1, XBC_DIM)),
                 per_b3(PAIR_STATE))
    return pl.pallas_call(
        functools.partial(_mixer_kernel, has_hist, ts, lc),
        out_shape=out_shape, grid=(b, s // ts), in_specs=in_specs, out_specs=out_specs,
        scratch_shapes=[pltpu.VMEM((SUBLANES + ts, D_CONV), F32), pltpu.VMEM((SUBLANES + ts, XBC_DIM), F32),
                        pltpu.VMEM(PAIR_STATE, F32)],
        compiler_params=pltpu.CompilerParams(dimension_semantics=("parallel", "arbitrary"),
                                             vmem_limit_bytes=VMEM_LIMIT),
        name="mixer_hist" if has_hist else "mixer",
    )(*args)


def _memkv_kernel(m_ref, g_ref, wk_ref, wv_ref, k_ref, v_ref):
    mn = _rms(m_ref[...], g_ref[...]).astype(BF16)
    k_ref[...] = jnp.dot(mn, wk_ref[...], preferred_element_type=F32)
    v_ref[...] = jnp.dot(mn, wv_ref[...], preferred_element_type=F32)


def _memory_kv(mem, g, wk, wv, tm=256):
    t = mem.shape[0]
    blk = pl.BlockSpec((tm, D_MODEL), lambda i: (i, 0))
    return pl.pallas_call(
        _memkv_kernel,
        out_shape=(jax.ShapeDtypeStruct((t, D_MODEL), F32),) * 2,
        grid=(t // tm,), in_specs=[blk, _const_spec(g.shape), _const_spec(wk.shape), _const_spec(wv.shape)],
        out_specs=(blk, blk),
        compiler_params=pltpu.CompilerParams(dimension_semantics=("parallel",), vmem_limit_bytes=VMEM_LIMIT),
        name="memory_kv",
    )(mem, g, wk, wv)


def _attn_kernel(x_ref, k_ref, v_ref, g_ref, wq_ref, wo_ref, o_ref):
    x = x_ref[...]
    xn = _rms(x, g_ref[...]).astype(BF16)
    q = jnp.dot(xn, wq_ref[...], preferred_element_type=F32)
    by_head = len(k_ref.shape) == 3
    wo = wo_ref[...]
    acc = x
    for h in range(MEM_HEADS):
        cols = slice(h * MEM_HEAD_DIM, (h + 1) * MEM_HEAD_DIM)
        k_h = k_ref[:, h, :] if by_head else k_ref[:, cols]
        v_h = v_ref[:, h, :] if by_head else v_ref[:, cols]
        sc = _dot_nt(q[:, cols], k_h) * (MEM_HEAD_DIM ** -0.5)
        sc = sc - jnp.max(sc, axis=-1, keepdims=True)
        p = jnp.exp(sc)
        p = p / jnp.sum(p, axis=-1, keepdims=True)
        o_h = _dot(p, v_h)
        acc = acc + jnp.dot(o_h.astype(BF16), wo[cols, :], preferred_element_type=F32)
    o_ref[...] = acc


def _attn(x, k, v, g, wq, wo, ts):
    b, s, _ = x.shape
    ts = min(ts, s)
    xblk = pl.BlockSpec((None, ts, D_MODEL), lambda i, j: (i, j, 0))
    if k.ndim == 3:
        kvblk = pl.BlockSpec((None, N_MEM, D_MODEL), lambda i, j: (i, 0, 0))
    else:
        kvblk = pl.BlockSpec((None, N_MEM, MEM_HEADS, MEM_HEAD_DIM), lambda i, j: (i, 0, 0, 0))
    return pl.pallas_call(
        _attn_kernel,
        out_shape=jax.ShapeDtypeStruct((b, s, D_MODEL), F32),
        grid=(b, s // ts),
        in_specs=[xblk, kvblk, kvblk, _const_spec(g.shape), _const_spec(wq.shape), _const_spec(wo.shape)],
        out_specs=xblk,
        compiler_params=pltpu.CompilerParams(dimension_semantics=("parallel", "parallel"),
                                             vmem_limit_bytes=VMEM_LIMIT),
        name="attn",
    )(x, k, v, g, wq, wo)


def _top_k_rows(sc, k, payload=None):
    n = sc.shape[0]
    rows = lax.broadcasted_iota(jnp.int32, sc.shape, 0).astype(F32)
    vals, outs = [], []
    for _ in range(k):
        m = jnp.max(sc, axis=0, keepdims=True)
        cand = jnp.where(sc == m, rows, float(n))
        first = jnp.min(cand, axis=0, keepdims=True)
        hit = cand == first
        sc = jnp.where(hit, -jnp.inf, sc)
        vals.append(m)
        if payload is None:
            outs.append(first.astype(jnp.int32))
        else:
            outs.append(jnp.sum(jnp.where(hit, payload, 0), axis=0, keepdims=True))
    return jnp.concatenate(vals, axis=0), jnp.concatenate(outs, axis=0)


_LIVE_B = tuple(PEER_TOPK // (a + 1) for a in range(1, SUBLANES))


def _pair_candidates(v1, i1, v2, i2):
    sub = lax.broadcasted_iota(jnp.int32, (SUBLANES, v1.shape[1]), 0)
    comb = [v1[0:1] + v2]
    cidx = [i1[0:1] * N_KEYS + i2]
    for a in range(1, SUBLANES):
        comb.append(jnp.where(sub < _LIVE_B[a - 1], v1[a:a + 1] + v2[0:SUBLANES], -jnp.inf))
        cidx.append(i1[a:a + 1] * N_KEYS + i2[0:SUBLANES])
    comb.append(v1[SUBLANES:] + v2[0:1])
    cidx.append(i1[SUBLANES:] * N_KEYS + i2[0:1])
    return jnp.concatenate(comb, axis=0), jnp.concatenate(cidx, axis=0)


def _select_kernel(tb, x_ref, g_ref, wq_ref, keys_ref, *rest):
    xn_ref, eid_ref, tile_ref, shift_ref, gate_ref, eid_s, gate_s = rest[-7:]
    xn = _rms(x_ref[...], g_ref[...])
    xn_ref[...] = xn
    xn = xn.astype(BF16)

    def head(h, carry):
        q = jnp.dot(xn, wq_ref[h], preferred_element_type=F32)
        sc1 = _dot_nt(keys_ref[h, 0], q[:, 0:PEER_DK_HALF])
        sc2 = _dot_nt(keys_ref[h, 1], q[:, PEER_DK_HALF:PEER_DK])
        rows = pl.ds(pl.multiple_of(h * PEER_TOPK, PEER_TOPK), PEER_TOPK)
        for c0 in range(0, tb, LANES):
            cols = slice(c0, c0 + LANES)
            v1, i1 = _top_k_rows(sc1[:, cols], PEER_TOPK)
            v2, i2 = _top_k_rows(sc2[:, cols], PEER_TOPK)
            comb, cidx = _pair_candidates(v1, i1, v2, i2)
            top, eid = _top_k_rows(comb, PEER_TOPK, cidx)
            ex = jnp.exp(top - top[0:1])
            gate_s[rows, cols] = ex / jnp.sum(ex, axis=0, keepdims=True)
            eid_s[rows, cols] = eid
        return carry

    lax.fori_loop(0, PEER_HEADS, head, 0)
    for c0 in range(0, tb, LANES):
        toks = slice(c0, c0 + LANES)
        eid = eid_s[:, toks].T
        upper = eid >= HALF_EXPERTS
        eid_ref[toks, :] = eid
        tile_ref[toks, :] = jnp.where(upper, eid - HALF_EXPERTS, eid) * SUBLANES
        shift_ref[toks, :] = jnp.where(upper, 0, 16)
        gate_ref[toks, :] = gate_s[:, toks].T


def _select(x, g, wq, keys, tb, t0, t, after=None):
    b0 = t0 // tb
    extra = [] if after is None else [after]
    xin = pl.BlockSpec((tb, D_MODEL), lambda i: (i + b0, 0))
    xblk = pl.BlockSpec((tb, D_MODEL), lambda i: (i, 0))
    picks_i = jax.ShapeDtypeStruct((t, PICKS), jnp.int32)
    return pl.pallas_call(
        functools.partial(_select_kernel, tb),
        out_shape=(jax.ShapeDtypeStruct((t, D_MODEL), F32), picks_i, picks_i, picks_i,
                   jax.ShapeDtypeStruct((t, PICKS), F32)),
        grid=(t // tb,),
        in_specs=[xin, _const_spec(g.shape), _const_spec(wq.shape), _const_spec(keys.shape)]
        + [_after_spec(a) for a in extra],
        out_specs=(xblk,) + (_pick_block(tb),) * 4,
        scratch_shapes=[pltpu.VMEM((PICKS, tb), jnp.int32), pltpu.VMEM((PICKS, tb), F32)],
        compiler_params=pltpu.CompilerParams(dimension_semantics=("parallel",), vmem_limit_bytes=VMEM_LIMIT),
        name="peer_select",
    )(x, g, wq, keys, *extra)


HI_HALF_MASK = -(1 << 16)
_BITREV3 = (0, 4, 2, 6, 1, 5, 3, 7)


def _unpack(word, shift):
    return lax.bitcast_convert_type((word << shift) & HI_HALF_MASK, F32)


def _splat_picks(blk, one_pass):
    tb = blk.shape[0]
    r = lax.broadcasted_iota(jnp.int32, (PICKS, LANES), 0)
    c = lax.broadcasted_iota(jnp.int32, (PICKS, LANES), 1)
    eye = (r == c)[None]
    ones = jnp.ones((LANES, LANES), BF16)

    def one(piece):
        diag = jnp.where(eye, piece[:, None, :], 0.0).astype(BF16).reshape(tb * PICKS, LANES)
        return jnp.dot(diag, ones, preferred_element_type=F32)

    if one_pass:
        out = one(blk)
    else:
        hi = blk.astype(BF16).astype(F32)
        rest = blk - hi
        mid = rest.astype(BF16).astype(F32)
        out = one(hi) + one(mid) + one(rest - mid)
    return out.reshape(tb, PICKS, LANES)


def _full(ref, k):
    return jnp.broadcast_to(ref[k:k + 1, :], (SUBLANES, LANES))


def _expert_row(tab_ref, tile_ref, t, k, shifts):
    idx = pl.multiple_of(tile_ref[t, k], SUBLANES)
    return _unpack(tab_ref[pl.ds(idx, SUBLANES), :], _full(shifts, k))


def _hidden_kernel(tb, tile_ref, shift_ref, x_ref, gate_ref, tab_ref, gh_ref, bsh, stage, parts, xt):
    bsh[...] = _splat_picks(shift_ref[...].astype(F32), True).astype(jnp.int32)
    for s in range(SUBLANES):
        xt[pl.ds(s, tb, stride=SUBLANES), :] = x_ref[:, s * LANES:(s + 1) * LANES]

    half = PICKS // 2
    quad = SUBLANES // 2
    low = lax.broadcasted_iota(jnp.int32, (SUBLANES, LANES), 0) < quad

    def token(t, carry):
        x = xt[pl.ds(pl.multiple_of(t * SUBLANES, SUBLANES), SUBLANES), :]
        shifts = bsh.at[t]
        for j in range(half):
            a = x * _expert_row(tab_ref, tile_ref, t, j, shifts)
            b = x * _expert_row(tab_ref, tile_ref, t, j + half, shifts)
            stage[j * SUBLANES:(j + 1) * SUBLANES, :] = jnp.where(low, a + pltpu.roll(a, quad, 0),
                                                                  b + pltpu.roll(b, quad, 0))
        for p in range(2):
            part = stage[pl.ds(p * quad, half, stride=SUBLANES), :]
            for s in range(1, quad):
                part = part + stage[pl.ds(p * quad + s, half, stride=SUBLANES), :]
            parts[t, p * half:(p + 1) * half, :] = part
        return carry

    lax.fori_loop(0, tb, token, 0)
    gh_ref[...] = jnp.sum(parts[...], axis=-1)
    hid = gh_ref[...]
    gelu = 0.5 * hid * (1.0 + lax.erf(hid * (2.0 ** -0.5)))
    gh_ref[...] = gate_ref[...] * gelu


def _pick_block(tb, space=None):
    return pl.BlockSpec((tb, PICKS), lambda i: (i, 0), memory_space=space)


def _table_spec(tab):
    return pl.BlockSpec(tab.shape, lambda i: (0, 0), pipeline_mode=pl.Buffered(1))


def _hidden(tile, shift, xg, gate, tab, tb):
    t = xg.shape[0]
    return pl.pallas_call(
        functools.partial(_hidden_kernel, tb),
        out_shape=jax.ShapeDtypeStruct((t, PICKS), F32),
        grid=(t // tb,),
        in_specs=[_pick_block(tb, pltpu.SMEM), _pick_block(tb),
                  pl.BlockSpec((tb, D_MODEL), lambda i: (i, 0)),
                  _pick_block(tb), _table_spec(tab)],
        out_specs=_pick_block(tb),
        scratch_shapes=[pltpu.VMEM((tb, PICKS, LANES), jnp.int32), pltpu.VMEM((PICKS // 2 * SUBLANES, LANES), F32),
                        pltpu.VMEM((tb, PICKS, LANES), F32), pltpu.VMEM((tb * SUBLANES, LANES), F32)],
        compiler_params=pltpu.CompilerParams(dimension_semantics=("parallel",), vmem_limit_bytes=VMEM_LIMIT),
        name="peer_hidden",
    )(tile, shift, xg, gate, tab)


def _combine_kernel(tb, tile_ref, shift_ref, gh_ref, tab_ref, x_ref, g_ref, o_ref, bsh, bg, ot):
    bsh[...] = _splat_picks(shift_ref[...].astype(F32), True).astype(jnp.int32)
    bg[...] = _splat_picks(gh_ref[...], False)

    def token(t, carry):
        shifts, gates = bsh.at[t], bg.at[t]
        acc = [jnp.zeros((SUBLANES, LANES), F32) for _ in range(4)]
        for k in range(PICKS):
            acc[k % 4] = acc[k % 4] + _full(gates, k) * _expert_row(tab_ref, tile_ref, t, k, shifts)
        ot[pl.ds(pl.multiple_of(t * SUBLANES, SUBLANES), SUBLANES), :] = (acc[0] + acc[1]) + (acc[2] + acc[3])
        return carry

    lax.fori_loop(0, tb, token, 0)
    for s in range(SUBLANES):
        o_ref[:, s * LANES:(s + 1) * LANES] = ot[pl.ds(s, tb, stride=SUBLANES), :]
    o_ref[...] = _rms(x_ref[...] + o_ref[...], g_ref[...])


def _combine(tile, shift, gh, tab, x, g, tb, t0):
    t = gh.shape[0]
    b0 = t0 // tb
    rows = pl.BlockSpec((tb, D_MODEL), lambda i: (i + b0, 0))
    return pl.pallas_call(
        functools.partial(_combine_kernel, tb),
        out_shape=jax.ShapeDtypeStruct(x.shape, F32),
        grid=(t // tb,),
        in_specs=[_pick_block(tb, pltpu.SMEM), _pick_block(tb), _pick_block(tb), _table_spec(tab), rows,
                  _const_spec(g.shape)],
        out_specs=rows,
        scratch_shapes=[pltpu.VMEM((tb, PICKS, LANES), jnp.int32), pltpu.VMEM((tb, PICKS, LANES), F32),
                        pltpu.VMEM((tb * SUBLANES, LANES), F32)],
        compiler_params=pltpu.CompilerParams(dimension_semantics=("parallel",), vmem_limit_bytes=VMEM_LIMIT),
        name="peer_combine",
    )(tile, shift, gh, tab, x, g)


SC_LANES = 16
SC_WORKERS = 32
SC_CHUNK = 32
SC_FEATURE_BLOCK = 256
SC_PROMPT_TOKENS = 17664


def _sc_call(body, n, side_width, out_width):
    mesh = plsc.VectorSubcoreMesh(core_axis_name="c", subcore_axis_name="s")
    return pl.kernel(body, mesh=mesh, out_type=jax.ShapeDtypeStruct((n, out_width), F32),
                     compiler_params=pltpu.CompilerParams(needs_layout_passes=False),
                     scratch_types=[pltpu.VMEM((2 * PICKS,), jnp.int32), pltpu.VMEM((2 * side_width,), F32),
                                    pltpu.VMEM((2 * out_width,), F32),
                                    pltpu.VMEM((SC_CHUNK, D_MODEL), F32), pltpu.VMEM((SC_CHUNK, D_MODEL), F32)]
                     + [pltpu.SemaphoreType.DMA] * 5)


def _sc_token_loop(n, eid_hbm, side_hbm, tab_hbm, out_hbm, idx_v, side_v, out_v, bufs, sems, compute):
    sem_rows, sem_fetch, sem_put = sems[0:2], sems[2], sems[3:5]
    per_worker = n // SC_WORKERS
    first = (lax.axis_index("s") * 2 + lax.axis_index("c")) * per_worker
    last = first + per_worker - 1
    chunks = PICKS // SC_CHUNK
    side_w = side_v.shape[0] // 2
    out_w = out_v.shape[0] // 2

    def fetch(t, slot):
        return (pltpu.make_async_copy(eid_hbm.at[t], idx_v.at[pl.ds(slot * PICKS, PICKS)], sem_fetch),
                pltpu.make_async_copy(side_hbm.at[t], side_v.at[pl.ds(slot * side_w, side_w)], sem_fetch))

    def gather(slot, c):
        ids = idx_v.at[pl.ds(slot * PICKS + c * SC_CHUNK, SC_CHUNK)]
        return pltpu.make_async_copy(tab_hbm.at[ids], bufs[c % 2], sem_rows[c % 2])

    def put(t, slot):
        return pltpu.make_async_copy(out_v.at[pl.ds(slot * out_w, out_w)], out_hbm.at[t], sem_put[slot])

    for cp in fetch(first, 0):
        cp.start()
    for cp in fetch(first, 0):
        cp.wait()
    gather(0, 0).start()
    gather(0, 1).start()

    @pl.loop(0, per_worker, step=2)
    def _(i):
        for slot in range(2):
            t = first + i + slot
            nxt = jnp.minimum(t + 1, last)
            for cp in fetch(nxt, 1 - slot):
                cp.start()

            @pl.when(i >= 2)
            def _():
                put(t, slot).wait()

            for c in range(chunks):
                gather(slot, c).wait()
                compute(c, bufs[c % 2], slot * side_w, slot * out_w)
                if c + 2 < chunks:
                    gather(slot, c + 2).start()
                else:
                    if c + 2 == chunks:
                        for cp in fetch(nxt, 1 - slot):
                            cp.wait()
                    gather(1 - slot, c + 2 - chunks).start()
            put(t, slot).start()

    gather(0, 0).wait()
    gather(0, 1).wait()
    put(last, 0).wait()
    put(last, 1).wait()


def _sc_hidden(eid, x, tab):
    n = eid.shape[0]
    lanes = SC_LANES

    def body(eid_hbm, x_hbm, tab_hbm, out_hbm, idx_v, x_v, hid_v, rows0, rows1, *sems):
        lane = lax.iota(jnp.int32, lanes)

        def compute(c, rows, x0, h0):
            for g in range(SC_CHUNK // lanes):
                def step(j, accs):
                    off = pl.multiple_of(j * lanes, lanes)
                    xj = x_v[pl.ds(x0 + off, lanes)]
                    return tuple(accs[r] + xj * rows[g * lanes + r, pl.ds(off, lanes)] for r in range(lanes))
                accs = lax.fori_loop(0, D_MODEL // lanes, step,
                                     tuple(jnp.zeros((lanes,), F32) for _ in range(lanes)))
                hv = jnp.zeros((lanes,), F32)
                for r in range(lanes):
                    hv = jnp.where(lane == r, jnp.sum(accs[r]), hv)
                hid_v[pl.ds(h0 + c * SC_CHUNK + g * lanes, lanes)] = hv

        _sc_token_loop(n, eid_hbm, x_hbm, tab_hbm, out_hbm, idx_v, x_v, hid_v, (rows0, rows1), sems, compute)

    return _sc_call(body, n, D_MODEL, PICKS)(eid, x, tab)


def _sc_combine(eid, gh, tab):
    n = eid.shape[0]
    lanes = SC_LANES
    nacc = SC_FEATURE_BLOCK // lanes

    def body(eid_hbm, g_hbm, tab_hbm, out_hbm, idx_v, g_v, out_v, rows0, rows1, *sems):
        def compute(c, rows, g0, o0):
            for fb in range(D_MODEL // SC_FEATURE_BLOCK):
                f0 = fb * SC_FEATURE_BLOCK
                if c == 0:
                    init = tuple(jnp.zeros((lanes,), F32) for _ in range(nacc))
                else:
                    init = tuple(out_v[pl.ds(o0 + f0 + q * lanes, lanes)] for q in range(nacc))

                def step(r, accs):
                    gk = plsc.load_gather(g_v, [jnp.full((lanes,), g0 + c * SC_CHUNK + r, jnp.int32)])
                    return tuple(accs[q] + gk * rows[r, pl.ds(f0 + q * lanes, lanes)] for q in range(nacc))
                accs = lax.fori_loop(0, SC_CHUNK, step, init)
                for q in range(nacc):
                    out_v[pl.ds(o0 + f0 + q * lanes, lanes)] = accs[q]

        _sc_token_loop(n, eid_hbm, g_hbm, tab_hbm, out_hbm, idx_v, g_v, out_v, (rows0, rows1), sems, compute)

    return _sc_call(body, n, PICKS, D_MODEL)(eid, gh, tab)


def _after_spec(after):
    return pl.BlockSpec((SUBLANES, after.shape[1]), lambda i: (0, 0))


def _gate_gelu_kernel(h_ref, g_ref, after_ref, o_ref):
    hid = h_ref[...]
    o_ref[...] = g_ref[...] * (0.5 * hid * (1.0 + lax.erf(hid * (2.0 ** -0.5))))


def _gate_gelu(hid, gate, tb, after):
    t = hid.shape[0]
    return pl.pallas_call(
        _gate_gelu_kernel, out_shape=jax.ShapeDtypeStruct((t, PICKS), F32), grid=(t // tb,),
        in_specs=[_pick_block(tb), _pick_block(tb), _after_spec(after)], out_specs=_pick_block(tb),
        compiler_params=pltpu.CompilerParams(dimension_semantics=("parallel",)),
        name="gate_gelu",
    )(hid, gate, after)


def _final_kernel(x_ref, p_ref, g_ref, *rest):
    rest[-1][...] = _rms(x_ref[...] + p_ref[...], g_ref[...])


def _final(x, p, g, tb, t0, into=None, after=None):
    b0 = t0 // tb
    rows = pl.BlockSpec((tb, D_MODEL), lambda i: (i + b0, 0))
    in_specs = [rows, pl.BlockSpec((tb, D_MODEL), lambda i: (i, 0)), _const_spec(g.shape)]
    args = [x, p, g]
    aliases = {}
    if into is not None:
        aliases = {len(args): 0}
        in_specs.append(pl.BlockSpec(memory_space=pl.ANY))
        args.append(into)
    if after is not None:
        in_specs.append(_after_spec(after))
        args.append(after)
    return pl.pallas_call(
        _final_kernel, out_shape=jax.ShapeDtypeStruct(x.shape, F32), grid=(p.shape[0] // tb,),
        in_specs=in_specs, out_specs=rows, input_output_aliases=aliases,
        compiler_params=pltpu.CompilerParams(dimension_semantics=("parallel",), vmem_limit_bytes=VMEM_LIMIT),
        name="final_norm",
    )(*args)


PACK_BLOCK = 256


def _pack_kernel(lo_ref, hi_ref, o_ref):
    def bf16_bits(x):
        return lax.bitcast_convert_type(x.astype(BF16).astype(F32), jnp.int32)
    word = bf16_bits(hi_ref[...]) | lax.shift_right_logical(bf16_bits(lo_ref[...]), 16)
    for s in range(SUBLANES):
        o_ref[pl.ds(s, PACK_BLOCK, stride=SUBLANES), :] = word[:, s * LANES:(s + 1) * LANES]


def _pack_table(tab):
    steps = HALF_EXPERTS // PACK_BLOCK
    return pl.pallas_call(
        _pack_kernel, out_shape=jax.ShapeDtypeStruct((HALF_EXPERTS * SUBLANES, LANES), jnp.int32), grid=(steps,),
        in_specs=[pl.BlockSpec((PACK_BLOCK, D_MODEL), lambda i: (i, 0)),
                  pl.BlockSpec((PACK_BLOCK, D_MODEL), lambda i: (i + steps, 0))],
        out_specs=pl.BlockSpec((PACK_BLOCK * SUBLANES, LANES), lambda i: (i, 0)),
        compiler_params=pltpu.CompilerParams(dimension_semantics=("parallel",)),
        name="pack_table",
    )(tab, tab)


def _col(v, n=LANES):
    return jnp.pad(v.astype(F32), (0, n - v.shape[0])).reshape(1, n)


def _row(v, n=2 * SUBLANES):
    return jnp.pad(v.astype(F32), (0, n - v.shape[0])).reshape(n, 1)


def _peer(x, w, tb_sel, tb_exp, n_sc):
    t = x.shape[0]
    n_tc = t - n_sc
    if n_sc:
        xn_s, eid_s, _, _, gate_s = _select(x, w["ln_ffn"], w["w_pq"], w["sub_keys"], tb_sel, 0, n_sc)
        hid_s = _sc_hidden(eid_s, xn_s, w["u_tab"])
    xn, _, tile, shift, gate = _select(x, w["ln_ffn"], w["w_pq"], w["sub_keys"], tb_sel, n_sc, n_tc,
                                       after=xn_s if n_sc else None)
    gh = _hidden(tile, shift, xn, gate, w["u_pack"], tb_exp)
    if n_sc:
        out_s = _sc_combine(eid_s, _gate_gelu(hid_s, gate_s, tb_sel, after=gh), w["v_tab"])
    y = _combine(tile, shift, gh, w["v_pack"], x, w["final_norm"], tb_exp, n_sc)
    if n_sc:
        y = _final(x, out_s, w["final_norm"], tb_sel, 0, into=y)
    return y


def kernel(x_prompt, x_sample, mem_prompt, cache_conv_a, cache_conv_ssd, state_ssd, cache_mem_k, cache_mem_v, ln_mix_w, w_in, conv_a_w, conv_s_w, conv_s_b, dt_bias, a_log, d_skip, ssd_norm_w, w_out, ln_mem_w, mem_norm_w, w_mq, w_mk, w_mv, w_mo, ln_ffn_w, w_pq, sub_keys, u_tab, v_tab, final_norm_w):
    depth = w_in.shape[0]
    assert depth == 1
    l = 0
    n_main = 3 * D_CONV + D_SSD + XBC_DIM
    w_dt = w_in[l][:, n_main:]
    w = {
        "ln_mix": ln_mix_w[l].reshape(1, D_MODEL),
        "w_in": w_in[l][:, :n_main].astype(BF16),
        "w_dt": jnp.pad(w_dt, ((0, 0), (0, LANES - SSD_HEADS))).astype(BF16),
        "w_dtt": jnp.pad(w_dt.T, ((0, 2 * SUBLANES - SSD_HEADS), (0, 0))).astype(BF16),
        "conv_a_w": conv_a_w[l], "conv_s_w": conv_s_w[l], "conv_s_b": conv_s_b[l].reshape(1, XBC_DIM),
        "dtb_c": _col(dt_bias[l]), "alog_c": _col(a_log[l]), "dsk_c": _col(d_skip[l]),
        "dtb_r": _row(dt_bias[l]), "alog_r": _row(a_log[l]),
        "ssd_norm": ssd_norm_w[l].reshape(1, D_SSD),
        "w_out": w_out[l].astype(BF16),
        "ln_ffn": ln_ffn_w[l].reshape(1, D_MODEL),
        "w_pq": w_pq[l].astype(BF16).reshape(D_MODEL, PEER_HEADS, PEER_DK).transpose(1, 0, 2),
        "sub_keys": sub_keys[l].astype(BF16),
        "u_pack": _pack_table(u_tab[l]),
        "v_pack": _pack_table(v_tab[l]),
        "u_tab": u_tab[l], "v_tab": v_tab[l],
        "final_norm": final_norm_w.reshape(1, D_MODEL),
    }
    ln_mem = ln_mem_w[l].reshape(1, D_MODEL)
    wq, wo = w_mq[l].astype(BF16), w_mo[l].astype(BF16)

    bp, sp, _ = x_prompt.shape
    bs, ss, _ = x_sample.shape

    mk, mv = _memory_kv(mem_prompt.reshape(bp * N_MEM, D_MODEL), mem_norm_w[l].reshape(1, D_MODEL),
                        w_mk[l].astype(BF16), w_mv[l].astype(BF16))
    hp, pa, ps, ph = _mixer(x_prompt, None, w, 256)
    hp = _attn(hp, mk.reshape(bp, N_MEM, D_MODEL), mv.reshape(bp, N_MEM, D_MODEL), ln_mem, wq, wo, 256)
    y_prompt = _peer(hp.reshape(bp * sp, D_MODEL), w, 256, 64, SC_PROMPT_TOKENS).reshape(bp, sp, D_MODEL)

    hs, sa, ssd, sh = _mixer(x_sample, (cache_conv_a[l], cache_conv_ssd[l],
                                        state_ssd[l].reshape((bs,) + PAIR_STATE)), w, 256)
    hs = _attn(hs, cache_mem_k[l], cache_mem_v[l], ln_mem, wq, wo, 256)
    y_sample = _peer(hs.reshape(bs * ss, D_MODEL), w, 256, 64, 0).reshape(bs, ss, D_MODEL)

    kv_shape = (1, bp, N_MEM, MEM_HEADS, MEM_HEAD_DIM)
    st_shape = (SSD_HEADS, SSD_HEAD_DIM, SSD_STATE)
    return (y_prompt, y_sample, pa[None], ps[None], ph.reshape((1, bp) + st_shape), mk.reshape(kv_shape),
            mv.reshape(kv_shape), sa[None], ssd[None], sh.reshape((1, bs) + st_shape))
```

```python
import functools
import math

import jax
import jax.numpy as jnp
from jax import lax
from jax.experimental import pallas as pl
from jax.experimental.pallas import tpu as pltpu
from jax.experimental.pallas import tpu_sc as plsc

D_MODEL = 1024
CHUNK = 64
D_CONV = 512
CONV_A_W = 3
D_SSD = 512
SSD_HEAD_DIM = 64
SSD_HEADS = 8
SSD_GROUPS = 2
SSD_HPG = 4
SSD_STATE = 128
SSD_CONV_W = 4
XBC_DIM = 1024
N_MEM = 256
MEM_HEADS = 4
MEM_HEAD_DIM = 256
PEER_HEADS = 8
N_KEYS = 128
N_EXPERTS = N_KEYS * N_KEYS
PEER_TOPK = 16
PEER_DK = 256
PEER_DK_HALF = 128
PICKS = PEER_HEADS * PEER_TOPK
EPS = 1e-6

LANES = 128
SUBLANES = 8
HALF_EXPERTS = N_EXPERTS // 2
PAIR_STATE = (SSD_HEADS // 2, 2 * SSD_HEAD_DIM, SSD_STATE)
VMEM_LIMIT = 56 * 1024 * 1024

F32 = jnp.float32
BF16 = jnp.bfloat16
HI = lax.Precision.HIGHEST


def _rms(x, g):
    return x * lax.rsqrt(jnp.mean(x * x, axis=-1, keepdims=True) + EPS) * g


def _silu(x):
    return x * (1.0 / (1.0 + jnp.exp(-x)))


def _softplus(x):
    return jnp.maximum(x, 0.0) + jnp.log1p(jnp.exp(-jnp.abs(x)))


def _dot(a, b):
    return jnp.dot(a.astype(BF16), b.astype(BF16), preferred_element_type=F32)


def _dot_nt(a, b):
    return lax.dot_general(a.astype(BF16), b.astype(BF16), (((1,), (1,)), ((), ())),
                           preferred_element_type=F32)


def _dot_tn(a, b):
    return lax.dot_general(a.astype(BF16), b.astype(BF16), (((0,), (0,)), ((), ())),
                           preferred_element_type=F32)


def _const_spec(shape):
    n = len(shape)
    return pl.BlockSpec(shape, lambda *_: (0,) * n)


def _mixer_kernel(has_hist, has_after, ts, lc, *refs):
    if has_after:
        refs = refs[1:]
    if has_hist:
        x_ref, hista_ref, hists_ref, h0_ref = refs[:4]
        refs = refs[4:]
    else:
        x_ref = refs[0]
        refs = refs[1:]
    (lnw_ref, win_ref, wdt_ref, wdtt_ref, caw_ref, csw_ref, csb_ref, dtb_c_ref, alog_c_ref, dsk_c_ref,
     dtb_r_ref, alog_r_ref, nw_ref, wout_ref,
     h1_ref, na_ref, ns_ref, nh_ref, cata, cats, hst) = refs
    s = pl.program_id(1)
    pad = SUBLANES

    @pl.when(s == 0)
    def _():
        if has_hist:
            cata[pad - 2:pad, :] = hista_ref[...]
            cats[pad - 3:pad, :] = hists_ref[...]
            hst[...] = h0_ref[...]
        else:
            cata[0:pad, :] = jnp.zeros((pad, D_CONV), F32)
            cats[0:pad, :] = jnp.zeros((pad, XBC_DIM), F32)
            hst[...] = jnp.zeros(hst.shape, F32)

    x = x_ref[...]
    xn = _rms(x, lnw_ref[...]).astype(BF16)
    proj = jnp.dot(xn, win_ref[...], preferred_element_type=F32)
    dt_c = jnp.dot(xn, wdt_ref[...], preferred_element_type=F32)
    dt_r = lax.dot_general(wdtt_ref[...], xn, (((1,), (1,)), ((), ())),
                           preferred_element_type=F32)
    g_b = proj[:, 0:D_CONV]
    g_c = proj[:, D_CONV:2 * D_CONV]
    v_in = proj[:, 2 * D_CONV:3 * D_CONV]
    z = proj[:, 3 * D_CONV:3 * D_CONV + D_SSD]
    xbc = proj[:, 3 * D_CONV + D_SSD:3 * D_CONV + D_SSD + XBC_DIM]

    ua = g_c * v_in
    cata[pad:pad + ts, :] = ua
    caw = caw_ref[...]
    conv_a = (caw[0:1] * cata[pad - 2:pad - 2 + ts, :] + caw[1:2] * cata[pad - 1:pad - 1 + ts, :]
              + caw[2:3] * ua)
    y_a = g_b * conv_a
    hist_a = cata[pad + ts - 2:pad + ts, :]
    na_ref[...] = hist_a
    cata[pad - 2:pad, :] = hist_a

    cats[pad:pad + ts, :] = xbc
    csw = csw_ref[...]
    conv_s = (csw[0:1] * cats[pad - 3:pad - 3 + ts, :] + csw[1:2] * cats[pad - 2:pad - 2 + ts, :]
              + csw[2:3] * cats[pad - 1:pad - 1 + ts, :] + csw[3:4] * xbc)
    hist_s = cats[pad + ts - 3:pad + ts, :]
    ns_ref[...] = hist_s
    cats[pad - 3:pad, :] = hist_s
    xbc_c = _silu(conv_s + csb_ref[...])
    xs = xbc_c[:, 0:D_SSD]

    dtp_c = _softplus(dt_c + dtb_c_ref[...])
    dtp_r = _softplus(dt_r + dtb_r_ref[...])
    a_c = dtp_c * (-jnp.exp(alog_c_ref[...]))
    a_r = dtp_r * (-jnp.exp(alog_r_ref[...]))
    dsk_c = dsk_c_ref[...]

    lane = lax.broadcasted_iota(jnp.int32, (1, LANES), 1)
    first_head = lane < SSD_HEAD_DIM
    row2 = lax.broadcasted_iota(jnp.int32, (2 * SSD_HEAD_DIM, 1), 0) < SSD_HEAD_DIM
    li = lax.broadcasted_iota(jnp.int32, (lc, lc), 0)
    si = lax.broadcasted_iota(jnp.int32, (lc, lc), 1)
    causal = si <= li
    tril = causal.astype(F32)
    jrow = lax.broadcasted_iota(jnp.int32, (ts, lc), 0)
    scol = lax.broadcasted_iota(jnp.int32, (ts, lc), 1)

    def pair(col, h):
        return jnp.where(first_head, col[:, h:h + 1], col[:, h + 1:h + 2])

    y_chunks = []
    for c in range(ts // lc):
        r0 = c * lc
        rows = slice(r0, r0 + lc)
        acum_c = jnp.dot(tril, a_c[rows], preferred_element_type=F32, precision=HI)
        upper = ((jrow >= r0) & (jrow <= r0 + scol)).astype(F32)
        acum_r = jnp.dot(a_r, upper, preferred_element_type=F32, precision=HI)
        last_c = acum_c[lc - 1:lc, :]
        dec_c = jnp.exp(last_c - acum_c)
        eac_c = jnp.exp(acum_c)
        y_pairs = []
        for g in range(SSD_GROUPS):
            b_g = xbc_c[rows, D_SSD + g * SSD_STATE:D_SSD + (g + 1) * SSD_STATE]
            c_g = xbc_c[rows, D_SSD + (SSD_GROUPS + g) * SSD_STATE:D_SSD + (SSD_GROUPS + g + 1) * SSD_STATE]
            cb = _dot_nt(c_g, b_g)
            for q in range(SSD_HPG // 2):
                h = g * SSD_HPG + 2 * q
                pi = h // 2
                xs_p = xs[rows, pi * LANES:(pi + 1) * LANES]
                xdt = xs_p * pair(dtp_c[rows], h)
                res = []
                for hh in (h, h + 1):
                    seg = acum_c[:, hh:hh + 1] - acum_r[hh:hh + 1, :]
                    m_h = cb * jnp.exp(jnp.where(causal, seg, -jnp.inf))
                    res.append(_dot(m_h, xdt))
                y_diag = jnp.where(first_head, res[0], res[1])
                h_in = hst[pi]
                y_off = _dot_nt(c_g, h_in) * pair(eac_c, h)
                st = _dot_tn(xdt * pair(dec_c, h), b_g)
                cd = jnp.exp(jnp.where(row2, last_c[:, h:h + 1], last_c[:, h + 1:h + 2]))
                hst[pi] = cd * h_in + st
                y_pairs.append(y_diag + y_off + pair(dsk_c, h) * xs_p)
        y_chunks.append(jnp.concatenate(y_pairs, axis=1))
    y = y_chunks[0] if len(y_chunks) == 1 else jnp.concatenate(y_chunks, axis=0)
    y_b = _rms(y * _silu(z), nw_ref[...])
    wout = wout_ref[...]
    out = (jnp.dot(y_a.astype(BF16), wout[0:D_CONV], preferred_element_type=F32)
           + jnp.dot(y_b.astype(BF16), wout[D_CONV:], preferred_element_type=F32))
    h1_ref[...] = x + out
    nh_ref[...] = hst[...]


def _mixer(x, hist, w, ts, after=None):
    b, s, _ = x.shape
    lc = min(CHUNK, s)
    ts = min(ts, s)
    has_hist = hist is not None
    per_b3 = lambda shape: pl.BlockSpec((None,) + shape, lambda i, j: (i,) + (0,) * len(shape))
    in_specs = [pl.BlockSpec((None, ts, D_MODEL), lambda i, j: (i, j, 0))]
    args = [x]
    if after is not None:
        in_specs.insert(0, pl.BlockSpec((SUBLANES, after.shape[1]), lambda i, j: (0, 0)))
        args.insert(0, after)
    if has_hist:
        in_specs += [per_b3((CONV_A_W - 1, D_CONV)), per_b3((SSD_CONV_W - 1, XBC_DIM)),
                     per_b3(PAIR_STATE)]
        args += list(hist)
    wargs = [w["ln_mix"], w["w_in"], w["w_dt"], w["w_dtt"], w["conv_a_w"], w["conv_s_w"], w["conv_s_b"],
             w["dtb_c"], w["alog_c"], w["dsk_c"], w["dtb_r"], w["alog_r"], w["ssd_norm"], w["w_out"]]
    in_specs += [_const_spec(a.shape) for a in wargs]
    args += wargs
    out_shape = (jax.ShapeDtypeStruct((b, s, D_MODEL), F32),
                 jax.ShapeDtypeStruct((b, CONV_A_W - 1, D_CONV), F32),
                 jax.ShapeDtypeStruct((b, SSD_CONV_W - 1, XBC_DIM), F32),
                 jax.ShapeDtypeStruct((b,) + PAIR_STATE, F32))
    out_specs = (pl.BlockSpec((None, ts, D_MODEL), lambda i, j: (i, j, 0)),
                 per_b3((CONV_A_W - 1, D_CONV)), per_b3((SSD_CONV_W - 1, XBC_DIM)),
                 per_b3(PAIR_STATE))
    return pl.pallas_call(
        functools.partial(_mixer_kernel, has_hist, after is not None, ts, lc),
        out_shape=out_shape, grid=(b, s // ts), in_specs=in_specs, out_specs=out_specs,
        scratch_shapes=[pltpu.VMEM((SUBLANES + ts, D_CONV), F32), pltpu.VMEM((SUBLANES + ts, XBC_DIM), F32),
                        pltpu.VMEM(PAIR_STATE, F32)],
        compiler_params=pltpu.CompilerParams(dimension_semantics=("parallel", "arbitrary"),
                                             vmem_limit_bytes=VMEM_LIMIT),
        name="mixer_hist" if has_hist else "mixer",
    )(*args)


def _memkv_kernel(m_ref, g_ref, wk_ref, wv_ref, k_ref, v_ref):
    mn = _rms(m_ref[...], g_ref[...]).astype(BF16)
    k_ref[...] = jnp.dot(mn, wk_ref[...], preferred_element_type=F32)
    v_ref[...] = jnp.dot(mn, wv_ref[...], preferred_element_type=F32)


def _memory_kv(mem, g, wk, wv, tm=256):
    t = mem.shape[0]
    blk = pl.BlockSpec((tm, D_MODEL), lambda i: (i, 0))
    return pl.pallas_call(
        _memkv_kernel,
        out_shape=(jax.ShapeDtypeStruct((t, D_MODEL), F32),) * 2,
        grid=(t // tm,), in_specs=[blk, _const_spec(g.shape), _const_spec(wk.shape), _const_spec(wv.shape)],
        out_specs=(blk, blk),
        compiler_params=pltpu.CompilerParams(dimension_semantics=("parallel",), vmem_limit_bytes=VMEM_LIMIT),
        name="memory_kv",
    )(mem, g, wk, wv)


def _attn_kernel(x_ref, k_ref, v_ref, g_ref, wq_ref, wo_ref, o_ref):
    x = x_ref[...]
    xn = _rms(x, g_ref[...]).astype(BF16)
    q = jnp.dot(xn, wq_ref[...], preferred_element_type=F32)
    by_head = len(k_ref.shape) == 3
    wo = wo_ref[...]
    acc = x
    for h in range(MEM_HEADS):
        cols = slice(h * MEM_HEAD_DIM, (h + 1) * MEM_HEAD_DIM)
        k_h = k_ref[:, h, :] if by_head else k_ref[:, cols]
        v_h = v_ref[:, h, :] if by_head else v_ref[:, cols]
        sc = _dot_nt(q[:, cols], k_h) * (MEM_HEAD_DIM ** -0.5)
        sc = sc - jnp.max(sc, axis=-1, keepdims=True)
        p = jnp.exp(sc)
        p = p / jnp.sum(p, axis=-1, keepdims=True)
        o_h = _dot(p, v_h)
        acc = acc + jnp.dot(o_h.astype(BF16), wo[cols, :], preferred_element_type=F32)
    o_ref[...] = acc


def _attn(x, k, v, g, wq, wo, ts):
    b, s, _ = x.shape
    ts = min(ts, s)
    xblk = pl.BlockSpec((None, ts, D_MODEL), lambda i, j: (i, j, 0))
    if k.ndim == 3:
        kvblk = pl.BlockSpec((None, N_MEM, D_MODEL), lambda i, j: (i, 0, 0))
    else:
        kvblk = pl.BlockSpec((None, N_MEM, MEM_HEADS, MEM_HEAD_DIM), lambda i, j: (i, 0, 0, 0))
    return pl.pallas_call(
        _attn_kernel,
        out_shape=jax.ShapeDtypeStruct((b, s, D_MODEL), F32),
        grid=(b, s // ts),
        in_specs=[xblk, kvblk, kvblk, _const_spec(g.shape), _const_spec(wq.shape), _const_spec(wo.shape)],
        out_specs=xblk,
        compiler_params=pltpu.CompilerParams(dimension_semantics=("parallel", "parallel"),
                                             vmem_limit_bytes=VMEM_LIMIT),
        name="attn",
    )(x, k, v, g, wq, wo)


def _top_k_rows(sc, k, payload=None):
    n = sc.shape[0]
    rows = lax.broadcasted_iota(jnp.int32, sc.shape, 0).astype(F32)
    vals, outs = [], []
    for _ in range(k):
        m = jnp.max(sc, axis=0, keepdims=True)
        cand = jnp.where(sc == m, rows, float(n))
        first = jnp.min(cand, axis=0, keepdims=True)
        hit = cand == first
        sc = jnp.where(hit, -jnp.inf, sc)
        vals.append(m)
        if payload is None:
            outs.append(first.astype(jnp.int32))
        else:
            outs.append(jnp.sum(jnp.where(hit, payload, 0), axis=0, keepdims=True))
    return jnp.concatenate(vals, axis=0), jnp.concatenate(outs, axis=0)


_LIVE_B = tuple(PEER_TOPK // (a + 1) for a in range(1, SUBLANES))


def _pair_candidates(v1, i1, v2, i2):
    sub = lax.broadcasted_iota(jnp.int32, (SUBLANES, v1.shape[1]), 0)
    comb = [v1[0:1] + v2]
    cidx = [i1[0:1] * N_KEYS + i2]
    for a in range(1, SUBLANES):
        comb.append(jnp.where(sub < _LIVE_B[a - 1], v1[a:a + 1] + v2[0:SUBLANES], -jnp.inf))
        cidx.append(i1[a:a + 1] * N_KEYS + i2[0:SUBLANES])
    comb.append(v1[SUBLANES:] + v2[0:1])
    cidx.append(i1[SUBLANES:] * N_KEYS + i2[0:1])
    return jnp.concatenate(comb, axis=0), jnp.concatenate(cidx, axis=0)


def _select_kernel(tb, x_ref, g_ref, wq_ref, keys_ref, *rest):
    xn_ref, eid_ref, tile_ref, shift_ref, gate_ref, eid_s, gate_s = rest[-7:]
    xn = _rms(x_ref[...], g_ref[...])
    xn_ref[...] = xn
    xn = xn.astype(BF16)

    def head(h, carry):
        q = jnp.dot(xn, wq_ref[h], preferred_element_type=F32)
        sc1 = _dot_nt(keys_ref[h, 0], q[:, 0:PEER_DK_HALF])
        sc2 = _dot_nt(keys_ref[h, 1], q[:, PEER_DK_HALF:PEER_DK])
        rows = pl.ds(pl.multiple_of(h * PEER_TOPK, PEER_TOPK), PEER_TOPK)
        for c0 in range(0, tb, LANES):
            cols = slice(c0, c0 + LANES)
            v1, i1 = _top_k_rows(sc1[:, cols], PEER_TOPK)
            v2, i2 = _top_k_rows(sc2[:, cols], PEER_TOPK)
            comb, cidx = _pair_candidates(v1, i1, v2, i2)
            top, eid = _top_k_rows(comb, PEER_TOPK, cidx)
            ex = jnp.exp(top - top[0:1])
            gate_s[rows, cols] = ex / jnp.sum(ex, axis=0, keepdims=True)
            eid_s[rows, cols] = eid
        return carry

    lax.fori_loop(0, PEER_HEADS, head, 0)
    for c0 in range(0, tb, LANES):
        toks = slice(c0, c0 + LANES)
        eid = eid_s[:, toks].T
        upper = eid >= HALF_EXPERTS
        eid_ref[toks, :] = eid
        tile_ref[toks, :] = jnp.where(upper, eid - HALF_EXPERTS, eid) * SUBLANES
        shift_ref[toks, :] = jnp.where(upper, 0, 16)
        gate_ref[toks, :] = gate_s[:, toks].T


def _select(x, g, wq, keys, tb, t0, t, after=None):
    b0 = t0 // tb
    extra = [] if after is None else [after]
    xin = pl.BlockSpec((tb, D_MODEL), lambda i: (i + b0, 0))
    xblk = pl.BlockSpec((tb, D_MODEL), lambda i: (i, 0))
    picks_i = jax.ShapeDtypeStruct((t, PICKS), jnp.int32)
    return pl.pallas_call(
        functools.partial(_select_kernel, tb),
        out_shape=(jax.ShapeDtypeStruct((t, D_MODEL), F32), picks_i, picks_i, picks_i,
                   jax.ShapeDtypeStruct((t, PICKS), F32)),
        grid=(t // tb,),
        in_specs=[xin, _const_spec(g.shape), _const_spec(wq.shape), _const_spec(keys.shape)]
        + [_after_spec(a) for a in extra],
        out_specs=(xblk,) + (_pick_block(tb),) * 4,
        scratch_shapes=[pltpu.VMEM((PICKS, tb), jnp.int32), pltpu.VMEM((PICKS, tb), F32)],
        compiler_params=pltpu.CompilerParams(dimension_semantics=("parallel",), vmem_limit_bytes=VMEM_LIMIT),
        name="peer_select",
    )(x, g, wq, keys, *extra)


HI_HALF_MASK = -(1 << 16)
_BITREV3 = (0, 4, 2, 6, 1, 5, 3, 7)


def _unpack(word, shift):
    return lax.bitcast_convert_type((word << shift) & HI_HALF_MASK, F32)


def _splat_picks(blk, one_pass):
    tb = blk.shape[0]
    r = lax.broadcasted_iota(jnp.int32, (PICKS, LANES), 0)
    c = lax.broadcasted_iota(jnp.int32, (PICKS, LANES), 1)
    eye = (r == c)[None]
    ones = jnp.ones((LANES, LANES), BF16)

    def one(piece):
        diag = jnp.where(eye, piece[:, None, :], 0.0).astype(BF16).reshape(tb * PICKS, LANES)
        return jnp.dot(diag, ones, preferred_element_type=F32)

    if one_pass:
        out = one(blk)
    else:
        hi = blk.astype(BF16).astype(F32)
        rest = blk - hi
        mid = rest.astype(BF16).astype(F32)
        out = one(hi) + one(mid) + one(rest - mid)
    return out.reshape(tb, PICKS, LANES)


def _full(ref, k):
    return jnp.broadcast_to(ref[k:k + 1, :], (SUBLANES, LANES))


def _expert_row(tab_ref, tile_ref, t, k, shifts):
    idx = pl.multiple_of(tile_ref[t, k], SUBLANES)
    return _unpack(tab_ref[pl.ds(idx, SUBLANES), :], _full(shifts, k))


def _hidden_kernel(tb, tile_ref, shift_ref, x_ref, gate_ref, tab_ref, gh_ref, bsh, stage, parts, xt):
    bsh[...] = _splat_picks(shift_ref[...].astype(F32), True).astype(jnp.int32)
    for s in range(SUBLANES):
        xt[pl.ds(s, tb, stride=SUBLANES), :] = x_ref[:, s * LANES:(s + 1) * LANES]

    half = PICKS // 2
    quad = SUBLANES // 2
    low = lax.broadcasted_iota(jnp.int32, (SUBLANES, LANES), 0) < quad

    def token(t, carry):
        x = xt[pl.ds(pl.multiple_of(t * SUBLANES, SUBLANES), SUBLANES), :]
        shifts = bsh.at[t]
        for j in range(half):
            a = x * _expert_row(tab_ref, tile_ref, t, j, shifts)
            b = x * _expert_row(tab_ref, tile_ref, t, j + half, shifts)
            stage[j * SUBLANES:(j + 1) * SUBLANES, :] = jnp.where(low, a + pltpu.roll(a, quad, 0),
                                                                  b + pltpu.roll(b, quad, 0))
        for p in range(2):
            part = stage[pl.ds(p * quad, half, stride=SUBLANES), :]
            for s in range(1, quad):
                part = part + stage[pl.ds(p * quad + s, half, stride=SUBLANES), :]
            parts[t, p * half:(p + 1) * half, :] = part
        return carry

    lax.fori_loop(0, tb, token, 0)
    gh_ref[...] = jnp.sum(parts[...], axis=-1)
    hid = gh_ref[...]
    gelu = 0.5 * hid * (1.0 + lax.erf(hid * (2.0 ** -0.5)))
    gh_ref[...] = gate_ref[...] * gelu


def _pick_block(tb, space=None):
    return pl.BlockSpec((tb, PICKS), lambda i: (i, 0), memory_space=space)


def _table_spec(tab):
    return pl.BlockSpec(tab.shape, lambda i: (0, 0), pipeline_mode=pl.Buffered(1))


def _hidden(tile, shift, xg, gate, tab, tb):
    t = xg.shape[0]
    return pl.pallas_call(
        functools.partial(_hidden_kernel, tb),
        out_shape=jax.ShapeDtypeStruct((t, PICKS), F32),
        grid=(t // tb,),
        in_specs=[_pick_block(tb, pltpu.SMEM), _pick_block(tb),
                  pl.BlockSpec((tb, D_MODEL), lambda i: (i, 0)),
                  _pick_block(tb), _table_spec(tab)],
        out_specs=_pick_block(tb),
        scratch_shapes=[pltpu.VMEM((tb, PICKS, LANES), jnp.int32), pltpu.VMEM((PICKS // 2 * SUBLANES, LANES), F32),
                        pltpu.VMEM((tb, PICKS, LANES), F32), pltpu.VMEM((tb * SUBLANES, LANES), F32)],
        compiler_params=pltpu.CompilerParams(dimension_semantics=("parallel",), vmem_limit_bytes=VMEM_LIMIT),
        name="peer_hidden",
    )(tile, shift, xg, gate, tab)


def _combine_kernel(tb, tile_ref, shift_ref, gh_ref, tab_ref, x_ref, g_ref, o_ref, bsh, bg, ot):
    bsh[...] = _splat_picks(shift_ref[...].astype(F32), True).astype(jnp.int32)
    bg[...] = _splat_picks(gh_ref[...], False)

    def token(t, carry):
        shifts, gates = bsh.at[t], bg.at[t]
        acc = [jnp.zeros((SUBLANES, LANES), F32) for _ in range(4)]
        for k in range(PICKS):
            acc[k % 4] = acc[k % 4] + _full(gates, k) * _expert_row(tab_ref, tile_ref, t, k, shifts)
        ot[pl.ds(pl.multiple_of(t * SUBLANES, SUBLANES), SUBLANES), :] = (acc[0] + acc[1]) + (acc[2] + acc[3])
        return carry

    lax.fori_loop(0, tb, token, 0)
    for s in range(SUBLANES):
        o_ref[:, s * LANES:(s + 1) * LANES] = ot[pl.ds(s, tb, stride=SUBLANES), :]
    o_ref[...] = _rms(x_ref[...] + o_ref[...], g_ref[...])


def _combine(tile, shift, gh, tab, x, g, tb, t0):
    t = gh.shape[0]
    b0 = t0 // tb
    rows = pl.BlockSpec((tb, D_MODEL), lambda i: (i + b0, 0))
    return pl.pallas_call(
        functools.partial(_combine_kernel, tb),
        out_shape=jax.ShapeDtypeStruct(x.shape, F32),
        grid=(t // tb,),
        in_specs=[_pick_block(tb, pltpu.SMEM), _pick_block(tb), _pick_block(tb), _table_spec(tab), rows,
                  _const_spec(g.shape)],
        out_specs=rows,
        scratch_shapes=[pltpu.VMEM((tb, PICKS, LANES), jnp.int32), pltpu.VMEM((tb, PICKS, LANES), F32),
                        pltpu.VMEM((tb * SUBLANES, LANES), F32)],
        compiler_params=pltpu.CompilerParams(dimension_semantics=("parallel",), vmem_limit_bytes=VMEM_LIMIT),
        name="peer_combine",
    )(tile, shift, gh, tab, x, g)


SC_LANES = 16
SC_WORKERS = 32
SC_CHUNK = 32
SC_FEATURE_BLOCK = 256
SC_PROMPT_TOKENS = 17664


def _sc_call(body, n, side_width, out_width):
    mesh = plsc.VectorSubcoreMesh(core_axis_name="c", subcore_axis_name="s")
    return pl.kernel(body, mesh=mesh, out_type=jax.ShapeDtypeStruct((n, out_width), F32),
                     compiler_params=pltpu.CompilerParams(needs_layout_passes=False),
                     scratch_types=[pltpu.VMEM((2 * PICKS,), jnp.int32), pltpu.VMEM((2 * side_width,), F32),
                                    pltpu.VMEM((2 * out_width,), F32),
                                    pltpu.VMEM((SC_CHUNK, D_MODEL), F32), pltpu.VMEM((SC_CHUNK, D_MODEL), F32)]
                     + [pltpu.SemaphoreType.DMA] * 5)


def _sc_token_loop(n, eid_hbm, side_hbm, tab_hbm, out_hbm, idx_v, side_v, out_v, bufs, sems, compute):
    sem_rows, sem_fetch, sem_put = sems[0:2], sems[2], sems[3:5]
    per_worker = n // SC_WORKERS
    first = (lax.axis_index("s") * 2 + lax.axis_index("c")) * per_worker
    last = first + per_worker - 1
    chunks = PICKS // SC_CHUNK
    side_w = side_v.shape[0] // 2
    out_w = out_v.shape[0] // 2

    def fetch(t, slot):
        return (pltpu.make_async_copy(eid_hbm.at[t], idx_v.at[pl.ds(slot * PICKS, PICKS)], sem_fetch),
                pltpu.make_async_copy(side_hbm.at[t], side_v.at[pl.ds(slot * side_w, side_w)], sem_fetch))

    def gather(slot, c):
        ids = idx_v.at[pl.ds(slot * PICKS + c * SC_CHUNK, SC_CHUNK)]
        return pltpu.make_async_copy(tab_hbm.at[ids], bufs[c % 2], sem_rows[c % 2])

    def put(t, slot):
        return pltpu.make_async_copy(out_v.at[pl.ds(slot * out_w, out_w)], out_hbm.at[t], sem_put[slot])

    for cp in fetch(first, 0):
        cp.start()
    for cp in fetch(first, 0):
        cp.wait()
    gather(0, 0).start()
    gather(0, 1).start()

    @pl.loop(0, per_worker, step=2)
    def _(i):
        for slot in range(2):
            t = first + i + slot
            nxt = jnp.minimum(t + 1, last)
            for cp in fetch(nxt, 1 - slot):
                cp.start()

            @pl.when(i >= 2)
            def _():
                put(t, slot).wait()

            for c in range(chunks):
                gather(slot, c).wait()
                compute(c, bufs[c % 2], slot * side_w, slot * out_w)
                if c + 2 < chunks:
                    gather(slot, c + 2).start()
                else:
                    if c + 2 == chunks:
                        for cp in fetch(nxt, 1 - slot):
                            cp.wait()
                    gather(1 - slot, c + 2 - chunks).start()
            put(t, slot).start()

    gather(0, 0).wait()
    gather(0, 1).wait()
    put(last, 0).wait()
    put(last, 1).wait()


def _sc_hidden(eid, x, tab):
    n = eid.shape[0]
    lanes = SC_LANES

    def body(eid_hbm, x_hbm, tab_hbm, out_hbm, idx_v, x_v, hid_v, rows0, rows1, *sems):
        lane = lax.iota(jnp.int32, lanes)

        def compute(c, rows, x0, h0):
            for g in range(SC_CHUNK // lanes):
                def step(j, accs):
                    off = pl.multiple_of(j * lanes, lanes)
                    xj = x_v[pl.ds(x0 + off, lanes)]
                    return tuple(accs[r] + xj * rows[g * lanes + r, pl.ds(off, lanes)] for r in range(lanes))
                accs = lax.fori_loop(0, D_MODEL // lanes, step,
                                     tuple(jnp.zeros((lanes,), F32) for _ in range(lanes)))
                hv = jnp.zeros((lanes,), F32)
                for r in range(lanes):
                    hv = jnp.where(lane == r, jnp.sum(accs[r]), hv)
                hid_v[pl.ds(h0 + c * SC_CHUNK + g * lanes, lanes)] = hv

        _sc_token_loop(n, eid_hbm, x_hbm, tab_hbm, out_hbm, idx_v, x_v, hid_v, (rows0, rows1), sems, compute)

    return _sc_call(body, n, D_MODEL, PICKS)(eid, x, tab)


def _sc_combine(eid, gh, tab):
    n = eid.shape[0]
    lanes = SC_LANES
    nacc = SC_FEATURE_BLOCK // lanes

    def body(eid_hbm, g_hbm, tab_hbm, out_hbm, idx_v, g_v, out_v, rows0, rows1, *sems):
        def compute(c, rows, g0, o0):
            for fb in range(D_MODEL // SC_FEATURE_BLOCK):
                f0 = fb * SC_FEATURE_BLOCK
                if c == 0:
                    init = tuple(jnp.zeros((lanes,), F32) for _ in range(nacc))
                else:
                    init = tuple(out_v[pl.ds(o0 + f0 + q * lanes, lanes)] for q in range(nacc))

                def step(r, accs):
                    gk = plsc.load_gather(g_v, [jnp.full((lanes,), g0 + c * SC_CHUNK + r, jnp.int32)])
                    return tuple(accs[q] + gk * rows[r, pl.ds(f0 + q * lanes, lanes)] for q in range(nacc))
                accs = lax.fori_loop(0, SC_CHUNK, step, init)
                for q in range(nacc):
                    out_v[pl.ds(o0 + f0 + q * lanes, lanes)] = accs[q]

        _sc_token_loop(n, eid_hbm, g_hbm, tab_hbm, out_hbm, idx_v, g_v, out_v, (rows0, rows1), sems, compute)

    return _sc_call(body, n, PICKS, D_MODEL)(eid, gh, tab)


def _after_spec(after):
    return pl.BlockSpec((SUBLANES, after.shape[1]), lambda i: (0, 0))


def _gate_gelu_kernel(h_ref, g_ref, after_ref, o_ref):
    hid = h_ref[...]
    o_ref[...] = g_ref[...] * (0.5 * hid * (1.0 + lax.erf(hid * (2.0 ** -0.5))))


def _gate_gelu(hid, gate, tb, after):
    t = hid.shape[0]
    return pl.pallas_call(
        _gate_gelu_kernel, out_shape=jax.ShapeDtypeStruct((t, PICKS), F32), grid=(t // tb,),
        in_specs=[_pick_block(tb), _pick_block(tb), _after_spec(after)], out_specs=_pick_block(tb),
        compiler_params=pltpu.CompilerParams(dimension_semantics=("parallel",)),
        name="gate_gelu",
    )(hid, gate, after)


def _final_kernel(x_ref, p_ref, g_ref, *rest):
    rest[-1][...] = _rms(x_ref[...] + p_ref[...], g_ref[...])


def _final(x, p, g, tb, t0, into=None, after=None):
    b0 = t0 // tb
    rows = pl.BlockSpec((tb, D_MODEL), lambda i: (i + b0, 0))
    in_specs = [rows, pl.BlockSpec((tb, D_MODEL), lambda i: (i, 0)), _const_spec(g.shape)]
    args = [x, p, g]
    aliases = {}
    if into is not None:
        aliases = {len(args): 0}
        in_specs.append(pl.BlockSpec(memory_space=pl.ANY))
        args.append(into)
    if after is not None:
        in_specs.append(_after_spec(after))
        args.append(after)
    return pl.pallas_call(
        _final_kernel, out_shape=jax.ShapeDtypeStruct(x.shape, F32), grid=(p.shape[0] // tb,),
        in_specs=in_specs, out_specs=rows, input_output_aliases=aliases,
        compiler_params=pltpu.CompilerParams(dimension_semantics=("parallel",), vmem_limit_bytes=VMEM_LIMIT),
        name="final_norm",
    )(*args)


PACK_BLOCK = 256


def _pack_kernel(lo_ref, hi_ref, o_ref):
    def bf16_bits(x):
        return lax.bitcast_convert_type(x.astype(BF16).astype(F32), jnp.int32)
    word = bf16_bits(hi_ref[...]) | lax.shift_right_logical(bf16_bits(lo_ref[...]), 16)
    for s in range(SUBLANES):
        o_ref[pl.ds(s, PACK_BLOCK, stride=SUBLANES), :] = word[:, s * LANES:(s + 1) * LANES]


def _pack_table(tab):
    steps = HALF_EXPERTS // PACK_BLOCK
    return pl.pallas_call(
        _pack_kernel, out_shape=jax.ShapeDtypeStruct((HALF_EXPERTS * SUBLANES, LANES), jnp.int32), grid=(steps,),
        in_specs=[pl.BlockSpec((PACK_BLOCK, D_MODEL), lambda i: (i, 0)),
                  pl.BlockSpec((PACK_BLOCK, D_MODEL), lambda i: (i + steps, 0))],
        out_specs=pl.BlockSpec((PACK_BLOCK * SUBLANES, LANES), lambda i: (i, 0)),
        compiler_params=pltpu.CompilerParams(dimension_semantics=("parallel",)),
        name="pack_table",
    )(tab, tab)


def _col(v, n=LANES):
    return jnp.pad(v.astype(F32), (0, n - v.shape[0])).reshape(1, n)


def _row(v, n=2 * SUBLANES):
    return jnp.pad(v.astype(F32), (0, n - v.shape[0])).reshape(n, 1)


def _peer(x, w, tb_sel, tb_exp, n_sc):
    t = x.shape[0]
    n_tc = t - n_sc
    if n_sc:
        xn_s, eid_s, _, _, gate_s = _select(x, w["ln_ffn"], w["w_pq"], w["sub_keys"], tb_sel, 0, n_sc)
        hid_s = _sc_hidden(eid_s, xn_s, w["u_tab"])
    xn, _, tile, shift, gate = _select(x, w["ln_ffn"], w["w_pq"], w["sub_keys"], tb_sel, n_sc, n_tc,
                                       after=xn_s if n_sc else None)
    gh = _hidden(tile, shift, xn, gate, w["u_pack"], tb_exp)
    if n_sc:
        out_s = _sc_combine(eid_s, _gate_gelu(hid_s, gate_s, tb_sel, after=gh), w["v_tab"])
    y = _combine(tile, shift, gh, w["v_pack"], x, w["final_norm"], tb_exp, n_sc)
    if n_sc:
        y = _final(x, out_s, w["final_norm"], tb_sel, 0, into=y)
    return y


def kernel(x_prompt, x_sample, mem_prompt, cache_conv_a, cache_conv_ssd, state_ssd, cache_mem_k, cache_mem_v, ln_mix_w, w_in, conv_a_w, conv_s_w, conv_s_b, dt_bias, a_log, d_skip, ssd_norm_w, w_out, ln_mem_w, mem_norm_w, w_mq, w_mk, w_mv, w_mo, ln_ffn_w, w_pq, sub_keys, u_tab, v_tab, final_norm_w):
    depth = w_in.shape[0]
    assert depth == 1
    l = 0
    n_main = 3 * D_CONV + D_SSD + XBC_DIM
    w_dt = w_in[l][:, n_main:]
    w = {
        "ln_mix": ln_mix_w[l].reshape(1, D_MODEL),
        "w_in": w_in[l][:, :n_main].astype(BF16),
        "w_dt": jnp.pad(w_dt, ((0, 0), (0, LANES - SSD_HEADS))).astype(BF16),
        "w_dtt": jnp.pad(w_dt.T, ((0, 2 * SUBLANES - SSD_HEADS), (0, 0))).astype(BF16),
        "conv_a_w": conv_a_w[l], "conv_s_w": conv_s_w[l], "conv_s_b": conv_s_b[l].reshape(1, XBC_DIM),
        "dtb_c": _col(dt_bias[l]), "alog_c": _col(a_log[l]), "dsk_c": _col(d_skip[l]),
        "dtb_r": _row(dt_bias[l]), "alog_r": _row(a_log[l]),
        "ssd_norm": ssd_norm_w[l].reshape(1, D_SSD),
        "w_out": w_out[l].astype(BF16),
        "ln_ffn": ln_ffn_w[l].reshape(1, D_MODEL),
        "w_pq": w_pq[l].astype(BF16).reshape(D_MODEL, PEER_HEADS, PEER_DK).transpose(1, 0, 2),
        "sub_keys": sub_keys[l].astype(BF16),
        "u_pack": _pack_table(u_tab[l]),
        "v_pack": _pack_table(v_tab[l]),
        "u_tab": u_tab[l], "v_tab": v_tab[l],
        "final_norm": final_norm_w.reshape(1, D_MODEL),
    }
    ln_mem = ln_mem_w[l].reshape(1, D_MODEL)
    wq, wo = w_mq[l].astype(BF16), w_mo[l].astype(BF16)

    bp, sp, _ = x_prompt.shape
    bs, ss, _ = x_sample.shape

    hs, sa, ssd, sh = _mixer(x_sample, (cache_conv_a[l], cache_conv_ssd[l],
                                        state_ssd[l].reshape((bs,) + PAIR_STATE)), w, 256)
    hs = _attn(hs, cache_mem_k[l], cache_mem_v[l], ln_mem, wq, wo, 256).reshape(bs * ss, D_MODEL)
    xn_s, eid_s, _, _, gate_s = _select(hs, w["ln_ffn"], w["w_pq"], w["sub_keys"], 256, 0, bs * ss)
    hid_s = _sc_hidden(eid_s, xn_s, w["u_tab"])

    mk, mv = _memory_kv(mem_prompt.reshape(bp * N_MEM, D_MODEL), mem_norm_w[l].reshape(1, D_MODEL),
                        w_mk[l].astype(BF16), w_mv[l].astype(BF16))
    hp, pa, ps, ph = _mixer(x_prompt, None, w, 256, after=xn_s)
    out_s = _sc_combine(eid_s, _gate_gelu(hid_s, gate_s, 256, after=hp.reshape(bp * sp, D_MODEL)), w["v_tab"])
    hp = _attn(hp, mk.reshape(bp, N_MEM, D_MODEL), mv.reshape(bp, N_MEM, D_MODEL), ln_mem, wq, wo, 256)
    hp = hp.reshape(bp * sp, D_MODEL)
    y_sample = _final(hs, out_s, w["final_norm"], 256, 0, after=hp).reshape(bs, ss, D_MODEL)
    y_prompt = _peer(hp, w, 256, 64, SC_PROMPT_TOKENS).reshape(bp, sp, D_MODEL)

    kv_shape = (1, bp, N_MEM, MEM_HEADS, MEM_HEAD_DIM)
    st_shape = (SSD_HEADS, SSD_HEAD_DIM, SSD_STATE)
    return (y_prompt, y_sample, pa[None], ps[None], ph.reshape((1, bp) + st_shape), mk.reshape(kv_shape),
            mv.reshape(kv_shape), sa[None], ssd[None], sh.reshape((1, bs) + st_shape))
```

```python
import functools
import math

import jax
import jax.numpy as jnp
from jax import lax
from jax.experimental import pallas as pl
from jax.experimental.pallas import tpu as pltpu
from jax.experimental.pallas import tpu_sc as plsc

D_MODEL = 1024
CHUNK = 64
D_CONV = 512
CONV_A_W = 3
D_SSD = 512
SSD_HEAD_DIM = 64
SSD_HEADS = 8
SSD_GROUPS = 2
SSD_HPG = 4
SSD_STATE = 128
SSD_CONV_W = 4
XBC_DIM = 1024
N_MEM = 256
MEM_HEADS = 4
MEM_HEAD_DIM = 256
PEER_HEADS = 8
N_KEYS = 128
N_EXPERTS = N_KEYS * N_KEYS
PEER_TOPK = 16
PEER_DK = 256
PEER_DK_HALF = 128
PICKS = PEER_HEADS * PEER_TOPK
EPS = 1e-6

LANES = 128
SUBLANES = 8
HALF_EXPERTS = N_EXPERTS // 2
PAIR_STATE = (SSD_HEADS // 2, 2 * SSD_HEAD_DIM, SSD_STATE)
VMEM_LIMIT = 56 * 1024 * 1024

F32 = jnp.float32
BF16 = jnp.bfloat16
HI = lax.Precision.HIGHEST


def _rms(x, g):
    return x * lax.rsqrt(jnp.mean(x * x, axis=-1, keepdims=True) + EPS) * g


def _silu(x):
    return x * (1.0 / (1.0 + jnp.exp(-x)))


def _softplus(x):
    return jnp.maximum(x, 0.0) + jnp.log1p(jnp.exp(-jnp.abs(x)))


def _dot(a, b):
    return jnp.dot(a.astype(BF16), b.astype(BF16), preferred_element_type=F32)


def _dot_nt(a, b):
    return lax.dot_general(a.astype(BF16), b.astype(BF16), (((1,), (1,)), ((), ())),
                           preferred_element_type=F32)


def _dot_tn(a, b):
    return lax.dot_general(a.astype(BF16), b.astype(BF16), (((0,), (0,)), ((), ())),
                           preferred_element_type=F32)


def _const_spec(shape):
    n = len(shape)
    return pl.BlockSpec(shape, lambda *_: (0,) * n)


def _mixer_kernel(has_hist, has_after, ts, lc, *refs):
    if has_after:
        refs = refs[1:]
    if has_hist:
        x_ref, hista_ref, hists_ref, h0_ref = refs[:4]
        refs = refs[4:]
    else:
        x_ref = refs[0]
        refs = refs[1:]
    (lnw_ref, win_ref, wdt_ref, wdtt_ref, caw_ref, csw_ref, csb_ref, dtb_c_ref, alog_c_ref, dsk_c_ref,
     dtb_r_ref, alog_r_ref, nw_ref, wout_ref,
     h1_ref, na_ref, ns_ref, nh_ref, cata, cats, hst) = refs
    s = pl.program_id(1)
    pad = SUBLANES

    @pl.when(s == 0)
    def _():
        if has_hist:
            cata[pad - 2:pad, :] = hista_ref[...]
            cats[pad - 3:pad, :] = hists_ref[...]
            hst[...] = h0_ref[...]
        else:
            cata[0:pad, :] = jnp.zeros((pad, D_CONV), F32)
            cats[0:pad, :] = jnp.zeros((pad, XBC_DIM), F32)
            hst[...] = jnp.zeros(hst.shape, F32)

    x = x_ref[...]
    xn = _rms(x, lnw_ref[...]).astype(BF16)
    proj = jnp.dot(xn, win_ref[...], preferred_element_type=F32)
    dt_c = jnp.dot(xn, wdt_ref[...], preferred_element_type=F32)
    dt_r = lax.dot_general(wdtt_ref[...], xn, (((1,), (1,)), ((), ())),
                           preferred_element_type=F32)
    g_b = proj[:, 0:D_CONV]
    g_c = proj[:, D_CONV:2 * D_CONV]
    v_in = proj[:, 2 * D_CONV:3 * D_CONV]
    z = proj[:, 3 * D_CONV:3 * D_CONV + D_SSD]
    xbc = proj[:, 3 * D_CONV + D_SSD:3 * D_CONV + D_SSD + XBC_DIM]

    ua = g_c * v_in
    cata[pad:pad + ts, :] = ua
    caw = caw_ref[...]
    conv_a = (caw[0:1] * cata[pad - 2:pad - 2 + ts, :] + caw[1:2] * cata[pad - 1:pad - 1 + ts, :]
              + caw[2:3] * ua)
    y_a = g_b * conv_a
    hist_a = cata[pad + ts - 2:pad + ts, :]
    na_ref[...] = hist_a
    cata[pad - 2:pad, :] = hist_a

    cats[pad:pad + ts, :] = xbc
    csw = csw_ref[...]
    conv_s = (csw[0:1] * cats[pad - 3:pad - 3 + ts, :] + csw[1:2] * cats[pad - 2:pad - 2 + ts, :]
              + csw[2:3] * cats[pad - 1:pad - 1 + ts, :] + csw[3:4] * xbc)
    hist_s = cats[pad + ts - 3:pad + ts, :]
    ns_ref[...] = hist_s
    cats[pad - 3:pad, :] = hist_s
    xbc_c = _silu(conv_s + csb_ref[...])
    xs = xbc_c[:, 0:D_SSD]

    dtp_c = _softplus(dt_c + dtb_c_ref[...])
    dtp_r = _softplus(dt_r + dtb_r_ref[...])
    a_c = dtp_c * (-jnp.exp(alog_c_ref[...]))
    a_r = dtp_r * (-jnp.exp(alog_r_ref[...]))
    dsk_c = dsk_c_ref[...]

    lane = lax.broadcasted_iota(jnp.int32, (1, LANES), 1)
    first_head = lane < SSD_HEAD_DIM
    row2 = lax.broadcasted_iota(jnp.int32, (2 * SSD_HEAD_DIM, 1), 0) < SSD_HEAD_DIM
    li = lax.broadcasted_iota(jnp.int32, (lc, lc), 0)
    si = lax.broadcasted_iota(jnp.int32, (lc, lc), 1)
    causal = si <= li
    tril = causal.astype(F32)
    jrow = lax.broadcasted_iota(jnp.int32, (ts, lc), 0)
    scol = lax.broadcasted_iota(jnp.int32, (ts, lc), 1)

    def pair(col, h):
        return jnp.where(first_head, col[:, h:h + 1], col[:, h + 1:h + 2])

    y_chunks = []
    for c in range(ts // lc):
        r0 = c * lc
        rows = slice(r0, r0 + lc)
        acum_c = jnp.dot(tril, a_c[rows], preferred_element_type=F32, precision=HI)
        upper = ((jrow >= r0) & (jrow <= r0 + scol)).astype(F32)
        acum_r = jnp.dot(a_r, upper, preferred_element_type=F32, precision=HI)
        last_c = acum_c[lc - 1:lc, :]
        dec_c = jnp.exp(last_c - acum_c)
        eac_c = jnp.exp(acum_c)
        y_pairs = []
        for g in range(SSD_GROUPS):
            b_g = xbc_c[rows, D_SSD + g * SSD_STATE:D_SSD + (g + 1) * SSD_STATE]
            c_g = xbc_c[rows, D_SSD + (SSD_GROUPS + g) * SSD_STATE:D_SSD + (SSD_GROUPS + g + 1) * SSD_STATE]
            cb = _dot_nt(c_g, b_g)
            for q in range(SSD_HPG // 2):
                h = g * SSD_HPG + 2 * q
                pi = h // 2
                xs_p = xs[rows, pi * LANES:(pi + 1) * LANES]
                xdt = xs_p * pair(dtp_c[rows], h)
                res = []
                for hh in (h, h + 1):
                    seg = acum_c[:, hh:hh + 1] - acum_r[hh:hh + 1, :]
                    m_h = cb * jnp.exp(jnp.where(causal, seg, -jnp.inf))
                    res.append(_dot(m_h, xdt))
                y_diag = jnp.where(first_head, res[0], res[1])
                h_in = hst[pi]
                y_off = _dot_nt(c_g, h_in) * pair(eac_c, h)
                st = _dot_tn(xdt * pair(dec_c, h), b_g)
                cd = jnp.exp(jnp.where(row2, last_c[:, h:h + 1], last_c[:, h + 1:h + 2]))
                hst[pi] = cd * h_in + st
                y_pairs.append(y_diag + y_off + pair(dsk_c, h) * xs_p)
        y_chunks.append(jnp.concatenate(y_pairs, axis=1))
    y = y_chunks[0] if len(y_chunks) == 1 else jnp.concatenate(y_chunks, axis=0)
    y_b = _rms(y * _silu(z), nw_ref[...])
    wout = wout_ref[...]
    out = (jnp.dot(y_a.astype(BF16), wout[0:D_CONV], preferred_element_type=F32)
           + jnp.dot(y_b.astype(BF16), wout[D_CONV:], preferred_element_type=F32))
    h1_ref[...] = x + out
    nh_ref[...] = hst[...]


def _mixer(x, hist, w, ts, b0=0, nb=None, after=None):
    _, s, _ = x.shape
    b = x.shape[0] if nb is None else nb
    lc = min(CHUNK, s)
    ts = min(ts, s)
    has_hist = hist is not None
    assert not (has_hist and b0)
    per_b3 = lambda shape: pl.BlockSpec((None,) + shape, lambda i, j: (i,) + (0,) * len(shape))
    in_specs = [pl.BlockSpec((None, ts, D_MODEL), lambda i, j: (i + b0, j, 0))]
    args = [x]
    if after is not None:
        in_specs.insert(0, pl.BlockSpec((SUBLANES, after.shape[1]), lambda i, j: (0, 0)))
        args.insert(0, after)
    if has_hist:
        in_specs += [per_b3((CONV_A_W - 1, D_CONV)), per_b3((SSD_CONV_W - 1, XBC_DIM)),
                     per_b3(PAIR_STATE)]
        args += list(hist)
    wargs = [w["ln_mix"], w["w_in"], w["w_dt"], w["w_dtt"], w["conv_a_w"], w["conv_s_w"], w["conv_s_b"],
             w["dtb_c"], w["alog_c"], w["dsk_c"], w["dtb_r"], w["alog_r"], w["ssd_norm"], w["w_out"]]
    in_specs += [_const_spec(a.shape) for a in wargs]
    args += wargs
    out_shape = (jax.ShapeDtypeStruct((b, s, D_MODEL), F32),
                 jax.ShapeDtypeStruct((b, CONV_A_W - 1, D_CONV), F32),
                 jax.ShapeDtypeStruct((b, SSD_CONV_W - 1, XBC_DIM), F32),
                 jax.ShapeDtypeStruct((b,) + PAIR_STATE, F32))
    out_specs = (pl.BlockSpec((None, ts, D_MODEL), lambda i, j: (i, j, 0)),
                 per_b3((CONV_A_W - 1, D_CONV)), per_b3((SSD_CONV_W - 1, XBC_DIM)),
                 per_b3(PAIR_STATE))
    return pl.pallas_call(
        functools.partial(_mixer_kernel, has_hist, after is not None, ts, lc),
        out_shape=out_shape, grid=(b, s // ts), in_specs=in_specs, out_specs=out_specs,
        scratch_shapes=[pltpu.VMEM((SUBLANES + ts, D_CONV), F32), pltpu.VMEM((SUBLANES + ts, XBC_DIM), F32),
                        pltpu.VMEM(PAIR_STATE, F32)],
        compiler_params=pltpu.CompilerParams(dimension_semantics=("parallel", "arbitrary"),
                                             vmem_limit_bytes=VMEM_LIMIT),
        name="mixer_hist" if has_hist else "mixer",
    )(*args)


def _memkv_kernel(m_ref, g_ref, wk_ref, wv_ref, k_ref, v_ref):
    mn = _rms(m_ref[...], g_ref[...]).astype(BF16)
    k_ref[...] = jnp.dot(mn, wk_ref[...], preferred_element_type=F32)
    v_ref[...] = jnp.dot(mn, wv_ref[...], preferred_element_type=F32)


def _memory_kv(mem, g, wk, wv, tm=256):
    t = mem.shape[0]
    blk = pl.BlockSpec((tm, D_MODEL), lambda i: (i, 0))
    return pl.pallas_call(
        _memkv_kernel,
        out_shape=(jax.ShapeDtypeStruct((t, D_MODEL), F32),) * 2,
        grid=(t // tm,), in_specs=[blk, _const_spec(g.shape), _const_spec(wk.shape), _const_spec(wv.shape)],
        out_specs=(blk, blk),
        compiler_params=pltpu.CompilerParams(dimension_semantics=("parallel",), vmem_limit_bytes=VMEM_LIMIT),
        name="memory_kv",
    )(mem, g, wk, wv)


def _attn_kernel(x_ref, k_ref, v_ref, g_ref, wq_ref, wo_ref, o_ref):
    x = x_ref[...]
    xn = _rms(x, g_ref[...]).astype(BF16)
    q = jnp.dot(xn, wq_ref[...], preferred_element_type=F32)
    by_head = len(k_ref.shape) == 3
    wo = wo_ref[...]
    acc = x
    for h in range(MEM_HEADS):
        cols = slice(h * MEM_HEAD_DIM, (h + 1) * MEM_HEAD_DIM)
        k_h = k_ref[:, h, :] if by_head else k_ref[:, cols]
        v_h = v_ref[:, h, :] if by_head else v_ref[:, cols]
        sc = _dot_nt(q[:, cols], k_h) * (MEM_HEAD_DIM ** -0.5)
        sc = sc - jnp.max(sc, axis=-1, keepdims=True)
        p = jnp.exp(sc)
        p = p / jnp.sum(p, axis=-1, keepdims=True)
        o_h = _dot(p, v_h)
        acc = acc + jnp.dot(o_h.astype(BF16), wo[cols, :], preferred_element_type=F32)
    o_ref[...] = acc


def _attn(x, k, v, g, wq, wo, ts, kv0=0):
    b, s, _ = x.shape
    ts = min(ts, s)
    xblk = pl.BlockSpec((None, ts, D_MODEL), lambda i, j: (i, j, 0))
    if k.ndim == 3:
        kvblk = pl.BlockSpec((None, N_MEM, D_MODEL), lambda i, j: (i + kv0, 0, 0))
    else:
        kvblk = pl.BlockSpec((None, N_MEM, MEM_HEADS, MEM_HEAD_DIM), lambda i, j: (i + kv0, 0, 0, 0))
    return pl.pallas_call(
        _attn_kernel,
        out_shape=jax.ShapeDtypeStruct((b, s, D_MODEL), F32),
        grid=(b, s // ts),
        in_specs=[xblk, kvblk, kvblk, _const_spec(g.shape), _const_spec(wq.shape), _const_spec(wo.shape)],
        out_specs=xblk,
        compiler_params=pltpu.CompilerParams(dimension_semantics=("parallel", "parallel"),
                                             vmem_limit_bytes=VMEM_LIMIT),
        name="attn",
    )(x, k, v, g, wq, wo)


def _top_k_rows(sc, k, payload=None):
    n = sc.shape[0]
    rows = lax.broadcasted_iota(jnp.int32, sc.shape, 0).astype(F32)
    vals, outs = [], []
    for _ in range(k):
        m = jnp.max(sc, axis=0, keepdims=True)
        cand = jnp.where(sc == m, rows, float(n))
        first = jnp.min(cand, axis=0, keepdims=True)
        hit = cand == first
        sc = jnp.where(hit, -jnp.inf, sc)
        vals.append(m)
        if payload is None:
            outs.append(first.astype(jnp.int32))
        else:
            outs.append(jnp.sum(jnp.where(hit, payload, 0), axis=0, keepdims=True))
    return jnp.concatenate(vals, axis=0), jnp.concatenate(outs, axis=0)


_LIVE_B = tuple(PEER_TOPK // (a + 1) for a in range(1, SUBLANES))


def _pair_candidates(v1, i1, v2, i2):
    sub = lax.broadcasted_iota(jnp.int32, (SUBLANES, v1.shape[1]), 0)
    comb = [v1[0:1] + v2]
    cidx = [i1[0:1] * N_KEYS + i2]
    for a in range(1, SUBLANES):
        comb.append(jnp.where(sub < _LIVE_B[a - 1], v1[a:a + 1] + v2[0:SUBLANES], -jnp.inf))
        cidx.append(i1[a:a + 1] * N_KEYS + i2[0:SUBLANES])
    comb.append(v1[SUBLANES:] + v2[0:1])
    cidx.append(i1[SUBLANES:] * N_KEYS + i2[0:1])
    return jnp.concatenate(comb, axis=0), jnp.concatenate(cidx, axis=0)


def _select_kernel(tb, x_ref, g_ref, wq_ref, keys_ref, *rest):
    xn_ref, eid_ref, tile_ref, shift_ref, gate_ref, eid_s, gate_s = rest[-7:]
    xn = _rms(x_ref[...], g_ref[...])
    xn_ref[...] = xn
    xn = xn.astype(BF16)

    def head(h, carry):
        q = jnp.dot(xn, wq_ref[h], preferred_element_type=F32)
        sc1 = _dot_nt(keys_ref[h, 0], q[:, 0:PEER_DK_HALF])
        sc2 = _dot_nt(keys_ref[h, 1], q[:, PEER_DK_HALF:PEER_DK])
        rows = pl.ds(pl.multiple_of(h * PEER_TOPK, PEER_TOPK), PEER_TOPK)
        for c0 in range(0, tb, LANES):
            cols = slice(c0, c0 + LANES)
            v1, i1 = _top_k_rows(sc1[:, cols], PEER_TOPK)
            v2, i2 = _top_k_rows(sc2[:, cols], PEER_TOPK)
            comb, cidx = _pair_candidates(v1, i1, v2, i2)
            top, eid = _top_k_rows(comb, PEER_TOPK, cidx)
            ex = jnp.exp(top - top[0:1])
            gate_s[rows, cols] = ex / jnp.sum(ex, axis=0, keepdims=True)
            eid_s[rows, cols] = eid
        return carry

    lax.fori_loop(0, PEER_HEADS, head, 0)
    for c0 in range(0, tb, LANES):
        toks = slice(c0, c0 + LANES)
        eid = eid_s[:, toks].T
        upper = eid >= HALF_EXPERTS
        eid_ref[toks, :] = eid
        tile_ref[toks, :] = jnp.where(upper, eid - HALF_EXPERTS, eid) * SUBLANES
        shift_ref[toks, :] = jnp.where(upper, 0, 16)
        gate_ref[toks, :] = gate_s[:, toks].T


def _select(x, g, wq, keys, tb, t0, t, after=None):
    b0 = t0 // tb
    extra = [] if after is None else [after]
    xin = pl.BlockSpec((tb, D_MODEL), lambda i: (i + b0, 0))
    xblk = pl.BlockSpec((tb, D_MODEL), lambda i: (i, 0))
    picks_i = jax.ShapeDtypeStruct((t, PICKS), jnp.int32)
    return pl.pallas_call(
        functools.partial(_select_kernel, tb),
        out_shape=(jax.ShapeDtypeStruct((t, D_MODEL), F32), picks_i, picks_i, picks_i,
                   jax.ShapeDtypeStruct((t, PICKS), F32)),
        grid=(t // tb,),
        in_specs=[xin, _const_spec(g.shape), _const_spec(wq.shape), _const_spec(keys.shape)]
        + [_after_spec(a) for a in extra],
        out_specs=(xblk,) + (_pick_block(tb),) * 4,
        scratch_shapes=[pltpu.VMEM((PICKS, tb), jnp.int32), pltpu.VMEM((PICKS, tb), F32)],
        compiler_params=pltpu.CompilerParams(dimension_semantics=("parallel",), vmem_limit_bytes=VMEM_LIMIT),
        name="peer_select",
    )(x, g, wq, keys, *extra)


HI_HALF_MASK = -(1 << 16)
_BITREV3 = (0, 4, 2, 6, 1, 5, 3, 7)


def _unpack(word, shift):
    return lax.bitcast_convert_type((word << shift) & HI_HALF_MASK, F32)


def _splat_picks(blk, one_pass):
    tb = blk.shape[0]
    r = lax.broadcasted_iota(jnp.int32, (PICKS, LANES), 0)
    c = lax.broadcasted_iota(jnp.int32, (PICKS, LANES), 1)
    eye = (r == c)[None]
    ones = jnp.ones((LANES, LANES), BF16)

    def one(piece):
        diag = jnp.where(eye, piece[:, None, :], 0.0).astype(BF16).reshape(tb * PICKS, LANES)
        return jnp.dot(diag, ones, preferred_element_type=F32)

    if one_pass:
        out = one(blk)
    else:
        hi = blk.astype(BF16).astype(F32)
        rest = blk - hi
        mid = rest.astype(BF16).astype(F32)
        out = one(hi) + one(mid) + one(rest - mid)
    return out.reshape(tb, PICKS, LANES)


def _full(ref, k):
    return jnp.broadcast_to(ref[k:k + 1, :], (SUBLANES, LANES))


def _expert_row(tab_ref, tile_ref, t, k, shifts):
    idx = pl.multiple_of(tile_ref[t, k], SUBLANES)
    return _unpack(tab_ref[pl.ds(idx, SUBLANES), :], _full(shifts, k))


def _hidden_kernel(tb, tile_ref, shift_ref, x_ref, gate_ref, tab_ref, *rest):
    gh_ref, bsh, stage, parts, xt = rest[-5:]
    bsh[...] = _splat_picks(shift_ref[...].astype(F32), True).astype(jnp.int32)
    for s in range(SUBLANES):
        xt[pl.ds(s, tb, stride=SUBLANES), :] = x_ref[:, s * LANES:(s + 1) * LANES]

    half = PICKS // 2
    quad = SUBLANES // 2
    low = lax.broadcasted_iota(jnp.int32, (SUBLANES, LANES), 0) < quad

    def token(t, carry):
        x = xt[pl.ds(pl.multiple_of(t * SUBLANES, SUBLANES), SUBLANES), :]
        shifts = bsh.at[t]
        for j in range(half):
            a = x * _expert_row(tab_ref, tile_ref, t, j, shifts)
            b = x * _expert_row(tab_ref, tile_ref, t, j + half, shifts)
            stage[j * SUBLANES:(j + 1) * SUBLANES, :] = jnp.where(low, a + pltpu.roll(a, quad, 0),
                                                                  b + pltpu.roll(b, quad, 0))
        for p in range(2):
            part = stage[pl.ds(p * quad, half, stride=SUBLANES), :]
            for s in range(1, quad):
                part = part + stage[pl.ds(p * quad + s, half, stride=SUBLANES), :]
            parts[t, p * half:(p + 1) * half, :] = part
        return carry

    lax.fori_loop(0, tb, token, 0)
    gh_ref[...] = jnp.sum(parts[...], axis=-1)
    hid = gh_ref[...]
    gelu = 0.5 * hid * (1.0 + lax.erf(hid * (2.0 ** -0.5)))
    gh_ref[...] = gate_ref[...] * gelu


def _pick_block(tb, space=None):
    return pl.BlockSpec((tb, PICKS), lambda i: (i, 0), memory_space=space)


def _table_spec(tab):
    return pl.BlockSpec(tab.shape, lambda i: (0, 0), pipeline_mode=pl.Buffered(1))


def _hidden(tile, shift, xg, gate, tab, tb, after=None):
    t = xg.shape[0]
    extra = [] if after is None else [after]
    return pl.pallas_call(
        functools.partial(_hidden_kernel, tb),
        out_shape=jax.ShapeDtypeStruct((t, PICKS), F32),
        grid=(t // tb,),
        in_specs=[_pick_block(tb, pltpu.SMEM), _pick_block(tb),
                  pl.BlockSpec((tb, D_MODEL), lambda i: (i, 0)),
                  _pick_block(tb), _table_spec(tab)] + [_after_spec(a) for a in extra],
        out_specs=_pick_block(tb),
        scratch_shapes=[pltpu.VMEM((tb, PICKS, LANES), jnp.int32), pltpu.VMEM((PICKS // 2 * SUBLANES, LANES), F32),
                        pltpu.VMEM((tb, PICKS, LANES), F32), pltpu.VMEM((tb * SUBLANES, LANES), F32)],
        compiler_params=pltpu.CompilerParams(dimension_semantics=("parallel",), vmem_limit_bytes=VMEM_LIMIT),
        name="peer_hidden",
    )(tile, shift, xg, gate, tab, *extra)


def _combine_kernel(tb, tile_ref, shift_ref, gh_ref, tab_ref, x_ref, g_ref, *rest):
    o_ref, bsh, bg, ot = rest[-4:]
    bsh[...] = _splat_picks(shift_ref[...].astype(F32), True).astype(jnp.int32)
    bg[...] = _splat_picks(gh_ref[...], False)

    def token(t, carry):
        shifts, gates = bsh.at[t], bg.at[t]
        acc = [jnp.zeros((SUBLANES, LANES), F32) for _ in range(4)]
        for k in range(PICKS):
            acc[k % 4] = acc[k % 4] + _full(gates, k) * _expert_row(tab_ref, tile_ref, t, k, shifts)
        ot[pl.ds(pl.multiple_of(t * SUBLANES, SUBLANES), SUBLANES), :] = (acc[0] + acc[1]) + (acc[2] + acc[3])
        return carry

    lax.fori_loop(0, tb, token, 0)
    for s in range(SUBLANES):
        o_ref[:, s * LANES:(s + 1) * LANES] = ot[pl.ds(s, tb, stride=SUBLANES), :]
    o_ref[...] = _rms(x_ref[...] + o_ref[...], g_ref[...])


def _combine(tile, shift, gh, tab, x, g, tb, x_row0, out_row0, out_rows, into=None):
    t = gh.shape[0]
    xb0, ob0 = x_row0 // tb, out_row0 // tb
    extra = [] if into is None else [into]
    return pl.pallas_call(
        functools.partial(_combine_kernel, tb),
        out_shape=jax.ShapeDtypeStruct((out_rows, D_MODEL), F32),
        grid=(t // tb,),
        in_specs=[_pick_block(tb, pltpu.SMEM), _pick_block(tb), _pick_block(tb), _table_spec(tab),
                  pl.BlockSpec((tb, D_MODEL), lambda i: (i + xb0, 0)), _const_spec(g.shape)]
        + [pl.BlockSpec(memory_space=pl.ANY) for _ in extra],
        input_output_aliases={6: 0} if extra else {},
        out_specs=pl.BlockSpec((tb, D_MODEL), lambda i: (i + ob0, 0)),
        scratch_shapes=[pltpu.VMEM((tb, PICKS, LANES), jnp.int32), pltpu.VMEM((tb, PICKS, LANES), F32),
                        pltpu.VMEM((tb * SUBLANES, LANES), F32)],
        compiler_params=pltpu.CompilerParams(dimension_semantics=("parallel",), vmem_limit_bytes=VMEM_LIMIT),
        name="peer_combine",
    )(tile, shift, gh, tab, x, g, *extra)


SC_LANES = 16
SC_WORKERS = 32
SC_CHUNK = 32
SC_FEATURE_BLOCK = 256
SC_EARLY_SEQS = 2
SC_PROMPT_TOKENS = 15360


def _sc_call(body, n, side_width, out_width):
    mesh = plsc.VectorSubcoreMesh(core_axis_name="c", subcore_axis_name="s")
    return pl.kernel(body, mesh=mesh, out_type=jax.ShapeDtypeStruct((n, out_width), F32),
                     compiler_params=pltpu.CompilerParams(needs_layout_passes=False),
                     scratch_types=[pltpu.VMEM((2 * PICKS,), jnp.int32), pltpu.VMEM((2 * side_width,), F32),
                                    pltpu.VMEM((2 * out_width,), F32),
                                    pltpu.VMEM((SC_CHUNK, D_MODEL), F32), pltpu.VMEM((SC_CHUNK, D_MODEL), F32)]
                     + [pltpu.SemaphoreType.DMA] * 5)


def _sc_token_loop(n, eid_hbm, side_hbm, tab_hbm, out_hbm, idx_v, side_v, out_v, bufs, sems, compute):
    sem_rows, sem_fetch, sem_put = sems[0:2], sems[2], sems[3:5]
    per_worker = n // SC_WORKERS
    first = (lax.axis_index("s") * 2 + lax.axis_index("c")) * per_worker
    last = first + per_worker - 1
    chunks = PICKS // SC_CHUNK
    side_w = side_v.shape[0] // 2
    out_w = out_v.shape[0] // 2

    def fetch(t, slot):
        return (pltpu.make_async_copy(eid_hbm.at[t], idx_v.at[pl.ds(slot * PICKS, PICKS)], sem_fetch),
                pltpu.make_async_copy(side_hbm.at[t], side_v.at[pl.ds(slot * side_w, side_w)], sem_fetch))

    def gather(slot, c):
        ids = idx_v.at[pl.ds(slot * PICKS + c * SC_CHUNK, SC_CHUNK)]
        return pltpu.make_async_copy(tab_hbm.at[ids], bufs[c % 2], sem_rows[c % 2])

    def put(t, slot):
        return pltpu.make_async_copy(out_v.at[pl.ds(slot * out_w, out_w)], out_hbm.at[t], sem_put[slot])

    for cp in fetch(first, 0):
        cp.start()
    for cp in fetch(first, 0):
        cp.wait()
    gather(0, 0).start()
    gather(0, 1).start()

    @pl.loop(0, per_worker, step=2)
    def _(i):
        for slot in range(2):
            t = first + i + slot
            nxt = jnp.minimum(t + 1, last)
            for cp in fetch(nxt, 1 - slot):
                cp.start()

            @pl.when(i >= 2)
            def _():
                put(t, slot).wait()

            for c in range(chunks):
                gather(slot, c).wait()
                compute(c, bufs[c % 2], slot * side_w, slot * out_w)
                if c + 2 < chunks:
                    gather(slot, c + 2).start()
                else:
                    if c + 2 == chunks:
                        for cp in fetch(nxt, 1 - slot):
                            cp.wait()
                    gather(1 - slot, c + 2 - chunks).start()
            put(t, slot).start()

    gather(0, 0).wait()
    gather(0, 1).wait()
    put(last, 0).wait()
    put(last, 1).wait()


def _sc_hidden(eid, x, tab):
    n = eid.shape[0]
    lanes = SC_LANES

    def body(eid_hbm, x_hbm, tab_hbm, out_hbm, idx_v, x_v, hid_v, rows0, rows1, *sems):
        lane = lax.iota(jnp.int32, lanes)

        def compute(c, rows, x0, h0):
            for g in range(SC_CHUNK // lanes):
                def step(j, accs):
                    off = pl.multiple_of(j * lanes, lanes)
                    xj = x_v[pl.ds(x0 + off, lanes)]
                    return tuple(accs[r] + xj * rows[g * lanes + r, pl.ds(off, lanes)] for r in range(lanes))
                accs = lax.fori_loop(0, D_MODEL // lanes, step,
                                     tuple(jnp.zeros((lanes,), F32) for _ in range(lanes)))
                hv = jnp.zeros((lanes,), F32)
                for r in range(lanes):
                    hv = jnp.where(lane == r, jnp.sum(accs[r]), hv)
                hid_v[pl.ds(h0 + c * SC_CHUNK + g * lanes, lanes)] = hv

        _sc_token_loop(n, eid_hbm, x_hbm, tab_hbm, out_hbm, idx_v, x_v, hid_v, (rows0, rows1), sems, compute)

    return _sc_call(body, n, D_MODEL, PICKS)(eid, x, tab)


def _sc_combine(eid, gh, tab):
    n = eid.shape[0]
    lanes = SC_LANES
    nacc = SC_FEATURE_BLOCK // lanes

    def body(eid_hbm, g_hbm, tab_hbm, out_hbm, idx_v, g_v, out_v, rows0, rows1, *sems):
        def compute(c, rows, g0, o0):
            for fb in range(D_MODEL // SC_FEATURE_BLOCK):
                f0 = fb * SC_FEATURE_BLOCK
                if c == 0:
                    init = tuple(jnp.zeros((lanes,), F32) for _ in range(nacc))
                else:
                    init = tuple(out_v[pl.ds(o0 + f0 + q * lanes, lanes)] for q in range(nacc))

                def step(r, accs):
                    gk = plsc.load_gather(g_v, [jnp.full((lanes,), g0 + c * SC_CHUNK + r, jnp.int32)])
                    return tuple(accs[q] + gk * rows[r, pl.ds(f0 + q * lanes, lanes)] for q in range(nacc))
                accs = lax.fori_loop(0, SC_CHUNK, step, init)
                for q in range(nacc):
                    out_v[pl.ds(o0 + f0 + q * lanes, lanes)] = accs[q]

        _sc_token_loop(n, eid_hbm, g_hbm, tab_hbm, out_hbm, idx_v, g_v, out_v, (rows0, rows1), sems, compute)

    return _sc_call(body, n, PICKS, D_MODEL)(eid, gh, tab)


def _after_spec(after):
    return pl.BlockSpec((SUBLANES, after.shape[1]), lambda i: (0, 0))


def _gate_gelu_kernel(h_ref, g_ref, after_ref, o_ref):
    hid = h_ref[...]
    o_ref[...] = g_ref[...] * (0.5 * hid * (1.0 + lax.erf(hid * (2.0 ** -0.5))))


def _gate_gelu(hid, gate, tb, after):
    t = hid.shape[0]
    return pl.pallas_call(
        _gate_gelu_kernel, out_shape=jax.ShapeDtypeStruct((t, PICKS), F32), grid=(t // tb,),
        in_specs=[_pick_block(tb), _pick_block(tb), _after_spec(after)], out_specs=_pick_block(tb),
        compiler_params=pltpu.CompilerParams(dimension_semantics=("parallel",)),
        name="gate_gelu",
    )(hid, gate, after)


def _final_kernel(x_ref, p_ref, g_ref, *rest):
    rest[-1][...] = _rms(x_ref[...] + p_ref[...], g_ref[...])


def _final(x, p, g, tb, out_row0, out_rows, into=None):
    b0 = out_row0 // tb
    local = pl.BlockSpec((tb, D_MODEL), lambda i: (i, 0))
    in_specs = [local, local, _const_spec(g.shape)]
    args = [x, p, g]
    aliases = {}
    if into is not None:
        aliases = {len(args): 0}
        in_specs.append(pl.BlockSpec(memory_space=pl.ANY))
        args.append(into)
    return pl.pallas_call(
        _final_kernel, out_shape=jax.ShapeDtypeStruct((out_rows, D_MODEL), F32), grid=(p.shape[0] // tb,),
        in_specs=in_specs, out_specs=pl.BlockSpec((tb, D_MODEL), lambda i: (i + b0, 0)),
        input_output_aliases=aliases,
        compiler_params=pltpu.CompilerParams(dimension_semantics=("parallel",), vmem_limit_bytes=VMEM_LIMIT),
        name="final_norm",
    )(*args)


PACK_BLOCK = 256


def _pack_kernel(lo_ref, hi_ref, o_ref):
    def bf16_bits(x):
        return lax.bitcast_convert_type(x.astype(BF16).astype(F32), jnp.int32)
    word = bf16_bits(hi_ref[...]) | lax.shift_right_logical(bf16_bits(lo_ref[...]), 16)
    for s in range(SUBLANES):
        o_ref[pl.ds(s, PACK_BLOCK, stride=SUBLANES), :] = word[:, s * LANES:(s + 1) * LANES]


def _pack_table(tab):
    steps = HALF_EXPERTS // PACK_BLOCK
    return pl.pallas_call(
        _pack_kernel, out_shape=jax.ShapeDtypeStruct((HALF_EXPERTS * SUBLANES, LANES), jnp.int32), grid=(steps,),
        in_specs=[pl.BlockSpec((PACK_BLOCK, D_MODEL), lambda i: (i, 0)),
                  pl.BlockSpec((PACK_BLOCK, D_MODEL), lambda i: (i + steps, 0))],
        out_specs=pl.BlockSpec((PACK_BLOCK * SUBLANES, LANES), lambda i: (i, 0)),
        compiler_params=pltpu.CompilerParams(dimension_semantics=("parallel",)),
        name="pack_table",
    )(tab, tab)


def _col(v, n=LANES):
    return jnp.pad(v.astype(F32), (0, n - v.shape[0])).reshape(1, n)


def _row(v, n=2 * SUBLANES):
    return jnp.pad(v.astype(F32), (0, n - v.shape[0])).reshape(n, 1)


def _peer(x, w, tb_sel, tb_exp, n_sc, out_row0=0, out_rows=None, after=None, into=None):
    t = x.shape[0]
    out_rows = t if out_rows is None else out_rows
    n_tc = t - n_sc
    if n_sc:
        xn_s, eid_s, _, _, gate_s = _select(x, w["ln_ffn"], w["w_pq"], w["sub_keys"], tb_sel, 0, n_sc, after=after)
        hid_s = _sc_hidden(eid_s, xn_s, w["u_tab"])
    xn, _, tile, shift, gate = _select(x, w["ln_ffn"], w["w_pq"], w["sub_keys"], tb_sel, n_sc, n_tc,
                                       after=xn_s if n_sc else None)
    gh = _hidden(tile, shift, xn, gate, w["u_pack"], tb_exp, after=into)
    if n_sc:
        out_s = _sc_combine(eid_s, _gate_gelu(hid_s, gate_s, tb_sel, after=gh), w["v_tab"])
    y = _combine(tile, shift, gh, w["v_pack"], x, w["final_norm"], tb_exp, n_sc, out_row0 + n_sc, out_rows, into)
    if n_sc:
        y = _final(x, out_s, w["final_norm"], tb_sel, out_row0, out_rows, into=y)
    return y


def kernel(x_prompt, x_sample, mem_prompt, cache_conv_a, cache_conv_ssd, state_ssd, cache_mem_k, cache_mem_v, ln_mix_w, w_in, conv_a_w, conv_s_w, conv_s_b, dt_bias, a_log, d_skip, ssd_norm_w, w_out, ln_mem_w, mem_norm_w, w_mq, w_mk, w_mv, w_mo, ln_ffn_w, w_pq, sub_keys, u_tab, v_tab, final_norm_w):
    depth = w_in.shape[0]
    assert depth == 1
    l = 0
    n_main = 3 * D_CONV + D_SSD + XBC_DIM
    w_dt = w_in[l][:, n_main:]
    w = {
        "ln_mix": ln_mix_w[l].reshape(1, D_MODEL),
        "w_in": w_in[l][:, :n_main].astype(BF16),
        "w_dt": jnp.pad(w_dt, ((0, 0), (0, LANES - SSD_HEADS))).astype(BF16),
        "w_dtt": jnp.pad(w_dt.T, ((0, 2 * SUBLANES - SSD_HEADS), (0, 0))).astype(BF16),
        "conv_a_w": conv_a_w[l], "conv_s_w": conv_s_w[l], "conv_s_b": conv_s_b[l].reshape(1, XBC_DIM),
        "dtb_c": _col(dt_bias[l]), "alog_c": _col(a_log[l]), "dsk_c": _col(d_skip[l]),
        "dtb_r": _row(dt_bias[l]), "alog_r": _row(a_log[l]),
        "ssd_norm": ssd_norm_w[l].reshape(1, D_SSD),
        "w_out": w_out[l].astype(BF16),
        "ln_ffn": ln_ffn_w[l].reshape(1, D_MODEL),
        "w_pq": w_pq[l].astype(BF16).reshape(D_MODEL, PEER_HEADS, PEER_DK).transpose(1, 0, 2),
        "sub_keys": sub_keys[l].astype(BF16),
        "u_pack": _pack_table(u_tab[l]),
        "v_pack": _pack_table(v_tab[l]),
        "u_tab": u_tab[l], "v_tab": v_tab[l],
        "final_norm": final_norm_w.reshape(1, D_MODEL),
    }
    ln_mem = ln_mem_w[l].reshape(1, D_MODEL)
    wq, wo = w_mq[l].astype(BF16), w_mo[l].astype(BF16)

    bp, sp, _ = x_prompt.shape
    bs, ss, _ = x_sample.shape

    mk, mv = _memory_kv(mem_prompt.reshape(bp * N_MEM, D_MODEL), mem_norm_w[l].reshape(1, D_MODEL),
                        w_mk[l].astype(BF16), w_mv[l].astype(BF16))
    mk3, mv3 = mk.reshape(bp, N_MEM, D_MODEL), mv.reshape(bp, N_MEM, D_MODEL)
    n1 = SC_EARLY_SEQS
    h1, pa1, ps1, ph1 = _mixer(x_prompt, None, w, 256, 0, n1)
    h1 = _attn(h1, mk3, mv3, ln_mem, wq, wo, 256).reshape(n1 * sp, D_MODEL)

    xn1, eid1, _, _, gate1 = _select(h1, w["ln_ffn"], w["w_pq"], w["sub_keys"], 256, 0, n1 * sp)
    hid1 = _sc_hidden(eid1, xn1, w["u_tab"])
    h2, pa2, ps2, ph2 = _mixer(x_prompt, None, w, 256, n1, bp - n1, after=xn1)
    h2 = _attn(h2, mk3, mv3, ln_mem, wq, wo, 256, kv0=n1).reshape((bp - n1) * sp, D_MODEL)
    gh1 = _gate_gelu(hid1, gate1, 256, after=h2)
    out1 = _sc_combine(eid1, gh1, w["v_tab"])
    y1 = _final(h1, out1, w["final_norm"], 256, 0, bp * sp)
    y_prompt = _peer(h2, w, 256, 64, SC_PROMPT_TOKENS, n1 * sp, bp * sp, after=gh1, into=y1).reshape(bp, sp, D_MODEL)
    pa, ps, ph = (jnp.concatenate(p, axis=0) for p in ((pa1, pa2), (ps1, ps2), (ph1, ph2)))

    hs, sa, ssd, sh = _mixer(x_sample, (cache_conv_a[l], cache_conv_ssd[l],
                                        state_ssd[l].reshape((bs,) + PAIR_STATE)), w, 256)
    hs = _attn(hs, cache_mem_k[l], cache_mem_v[l], ln_mem, wq, wo, 256)
    y_sample = _peer(hs.reshape(bs * ss, D_MODEL), w, 256, 64, 0).reshape(bs, ss, D_MODEL)

    kv_shape = (1, bp, N_MEM, MEM_HEADS, MEM_HEAD_DIM)
    st_shape = (SSD_HEADS, SSD_HEAD_DIM, SSD_STATE)
    return (y_prompt, y_sample, pa[None], ps[None], ph.reshape((1, bp) + st_shape), mk.reshape(kv_shape),
            mv.reshape(kv_shape), sa[None], ssd[None], sh.reshape((1, bs) + st_shape))
```

```python
import functools
import math

import jax
import jax.numpy as jnp
from jax import lax
from jax.experimental import pallas as pl
from jax.experimental.pallas import tpu as pltpu
from jax.experimental.pallas import tpu_sc as plsc

D_MODEL = 1024
CHUNK = 64
D_CONV = 512
CONV_A_W = 3
D_SSD = 512
SSD_HEAD_DIM = 64
SSD_HEADS = 8
SSD_GROUPS = 2
SSD_HPG = 4
SSD_STATE = 128
SSD_CONV_W = 4
XBC_DIM = 1024
N_MEM = 256
MEM_HEADS = 4
MEM_HEAD_DIM = 256
PEER_HEADS = 8
N_KEYS = 128
N_EXPERTS = N_KEYS * N_KEYS
PEER_TOPK = 16
PEER_DK = 256
PEER_DK_HALF = 128
PICKS = PEER_HEADS * PEER_TOPK
EPS = 1e-6

LANES = 128
SUBLANES = 8
HALF_EXPERTS = N_EXPERTS // 2
PAIR_STATE = (SSD_HEADS // 2, 2 * SSD_HEAD_DIM, SSD_STATE)
VMEM_LIMIT = 56 * 1024 * 1024

F32 = jnp.float32
BF16 = jnp.bfloat16
HI = lax.Precision.HIGHEST


def _rms(x, g):
    return x * lax.rsqrt(jnp.mean(x * x, axis=-1, keepdims=True) + EPS) * g


def _silu(x):
    return x * (1.0 / (1.0 + jnp.exp(-x)))


def _softplus(x):
    return jnp.maximum(x, 0.0) + jnp.log1p(jnp.exp(-jnp.abs(x)))


def _dot(a, b):
    return jnp.dot(a.astype(BF16), b.astype(BF16), preferred_element_type=F32)


def _dot_nt(a, b):
    return lax.dot_general(a.astype(BF16), b.astype(BF16), (((1,), (1,)), ((), ())),
                           preferred_element_type=F32)


def _dot_tn(a, b):
    return lax.dot_general(a.astype(BF16), b.astype(BF16), (((0,), (0,)), ((), ())),
                           preferred_element_type=F32)


def _const_spec(shape):
    n = len(shape)
    return pl.BlockSpec(shape, lambda *_: (0,) * n)


def _mixer_kernel(has_hist, has_after, ts, lc, *refs):
    if has_after:
        refs = refs[1:]
    if has_hist:
        x_ref, hista_ref, hists_ref, h0_ref = refs[:4]
        refs = refs[4:]
    else:
        x_ref = refs[0]
        refs = refs[1:]
    (lnw_ref, win_ref, wdt_ref, wdtt_ref, caw_ref, csw_ref, csb_ref, dtb_c_ref, alog_c_ref, dsk_c_ref,
     dtb_r_ref, alog_r_ref, nw_ref, wout_ref,
     h1_ref, na_ref, ns_ref, nh_ref, cata, cats, hst) = refs
    s = pl.program_id(1)
    pad = SUBLANES

    @pl.when(s == 0)
    def _():
        if has_hist:
            cata[pad - 2:pad, :] = hista_ref[...]
            cats[pad - 3:pad, :] = hists_ref[...]
            hst[...] = h0_ref[...]
        else:
            cata[0:pad, :] = jnp.zeros((pad, D_CONV), F32)
            cats[0:pad, :] = jnp.zeros((pad, XBC_DIM), F32)
            hst[...] = jnp.zeros(hst.shape, F32)

    x = x_ref[...]
    xn = _rms(x, lnw_ref[...]).astype(BF16)
    proj = jnp.dot(xn, win_ref[...], preferred_element_type=F32)
    dt_c = jnp.dot(xn, wdt_ref[...], preferred_element_type=F32)
    dt_r = lax.dot_general(wdtt_ref[...], xn, (((1,), (1,)), ((), ())),
                           preferred_element_type=F32)
    g_b = proj[:, 0:D_CONV]
    g_c = proj[:, D_CONV:2 * D_CONV]
    v_in = proj[:, 2 * D_CONV:3 * D_CONV]
    z = proj[:, 3 * D_CONV:3 * D_CONV + D_SSD]
    xbc = proj[:, 3 * D_CONV + D_SSD:3 * D_CONV + D_SSD + XBC_DIM]

    ua = g_c * v_in
    cata[pad:pad + ts, :] = ua
    caw = caw_ref[...]
    conv_a = (caw[0:1] * cata[pad - 2:pad - 2 + ts, :] + caw[1:2] * cata[pad - 1:pad - 1 + ts, :]
              + caw[2:3] * ua)
    y_a = g_b * conv_a
    hist_a = cata[pad + ts - 2:pad + ts, :]
    na_ref[...] = hist_a
    cata[pad - 2:pad, :] = hist_a

    cats[pad:pad + ts, :] = xbc
    csw = csw_ref[...]
    conv_s = (csw[0:1] * cats[pad - 3:pad - 3 + ts, :] + csw[1:2] * cats[pad - 2:pad - 2 + ts, :]
              + csw[2:3] * cats[pad - 1:pad - 1 + ts, :] + csw[3:4] * xbc)
    hist_s = cats[pad + ts - 3:pad + ts, :]
    ns_ref[...] = hist_s
    cats[pad - 3:pad, :] = hist_s
    xbc_c = _silu(conv_s + csb_ref[...])
    xs = xbc_c[:, 0:D_SSD]

    dtp_c = _softplus(dt_c + dtb_c_ref[...])
    dtp_r = _softplus(dt_r + dtb_r_ref[...])
    a_c = dtp_c * (-jnp.exp(alog_c_ref[...]))
    a_r = dtp_r * (-jnp.exp(alog_r_ref[...]))
    dsk_c = dsk_c_ref[...]

    lane = lax.broadcasted_iota(jnp.int32, (1, LANES), 1)
    first_head = lane < SSD_HEAD_DIM
    row2 = lax.broadcasted_iota(jnp.int32, (2 * SSD_HEAD_DIM, 1), 0) < SSD_HEAD_DIM
    li = lax.broadcasted_iota(jnp.int32, (lc, lc), 0)
    si = lax.broadcasted_iota(jnp.int32, (lc, lc), 1)
    causal = si <= li
    tril = causal.astype(F32)
    jrow = lax.broadcasted_iota(jnp.int32, (ts, lc), 0)
    scol = lax.broadcasted_iota(jnp.int32, (ts, lc), 1)

    def pair(col, h):
        return jnp.where(first_head, col[:, h:h + 1], col[:, h + 1:h + 2])

    y_chunks = []
    for c in range(ts // lc):
        r0 = c * lc
        rows = slice(r0, r0 + lc)
        acum_c = jnp.dot(tril, a_c[rows], preferred_element_type=F32, precision=HI)
        upper = ((jrow >= r0) & (jrow <= r0 + scol)).astype(F32)
        acum_r = jnp.dot(a_r, upper, preferred_element_type=F32, precision=HI)
        last_c = acum_c[lc - 1:lc, :]
        dec_c = jnp.exp(last_c - acum_c)
        eac_c = jnp.exp(acum_c)
        y_pairs = []
        for g in range(SSD_GROUPS):
            b_g = xbc_c[rows, D_SSD + g * SSD_STATE:D_SSD + (g + 1) * SSD_STATE]
            c_g = xbc_c[rows, D_SSD + (SSD_GROUPS + g) * SSD_STATE:D_SSD + (SSD_GROUPS + g + 1) * SSD_STATE]
            cb = _dot_nt(c_g, b_g)
            for q in range(SSD_HPG // 2):
                h = g * SSD_HPG + 2 * q
                pi = h // 2
                xs_p = xs[rows, pi * LANES:(pi + 1) * LANES]
                xdt = xs_p * pair(dtp_c[rows], h)
                res = []
                for hh in (h, h + 1):
                    seg = acum_c[:, hh:hh + 1] - acum_r[hh:hh + 1, :]
                    m_h = cb * jnp.exp(jnp.where(causal, seg, -jnp.inf))
                    res.append(_dot(m_h, xdt))
                y_diag = jnp.where(first_head, res[0], res[1])
                h_in = hst[pi]
                y_off = _dot_nt(c_g, h_in) * pair(eac_c, h)
                st = _dot_tn(xdt * pair(dec_c, h), b_g)
                cd = jnp.exp(jnp.where(row2, last_c[:, h:h + 1], last_c[:, h + 1:h + 2]))
                hst[pi] = cd * h_in + st
                y_pairs.append(y_diag + y_off + pair(dsk_c, h) * xs_p)
        y_chunks.append(jnp.concatenate(y_pairs, axis=1))
    y = y_chunks[0] if len(y_chunks) == 1 else jnp.concatenate(y_chunks, axis=0)
    y_b = _rms(y * _silu(z), nw_ref[...])
    wout = wout_ref[...]
    out = (jnp.dot(y_a.astype(BF16), wout[0:D_CONV], preferred_element_type=F32)
           + jnp.dot(y_b.astype(BF16), wout[D_CONV:], preferred_element_type=F32))
    h1_ref[...] = x + out
    nh_ref[...] = hst[...]


def _mixer(x, hist, w, ts, b0=0, nb=None, after=None):
    _, s, _ = x.shape
    b = x.shape[0] if nb is None else nb
    lc = min(CHUNK, s)
    ts = min(ts, s)
    has_hist = hist is not None
    assert not (has_hist and b0)
    per_b3 = lambda shape: pl.BlockSpec((None,) + shape, lambda i, j: (i,) + (0,) * len(shape))
    in_specs = [pl.BlockSpec((None, ts, D_MODEL), lambda i, j: (i + b0, j, 0))]
    args = [x]
    if after is not None:
        in_specs.insert(0, pl.BlockSpec((SUBLANES, after.shape[1]), lambda i, j: (0, 0)))
        args.insert(0, after)
    if has_hist:
        in_specs += [per_b3((CONV_A_W - 1, D_CONV)), per_b3((SSD_CONV_W - 1, XBC_DIM)),
                     per_b3(PAIR_STATE)]
        args += list(hist)
    wargs = [w["ln_mix"], w["w_in"], w["w_dt"], w["w_dtt"], w["conv_a_w"], w["conv_s_w"], w["conv_s_b"],
             w["dtb_c"], w["alog_c"], w["dsk_c"], w["dtb_r"], w["alog_r"], w["ssd_norm"], w["w_out"]]
    in_specs += [_const_spec(a.shape) for a in wargs]
    args += wargs
    out_shape = (jax.ShapeDtypeStruct((b, s, D_MODEL), F32),
                 jax.ShapeDtypeStruct((b, CONV_A_W - 1, D_CONV), F32),
                 jax.ShapeDtypeStruct((b, SSD_CONV_W - 1, XBC_DIM), F32),
                 jax.ShapeDtypeStruct((b,) + PAIR_STATE, F32))
    out_specs = (pl.BlockSpec((None, ts, D_MODEL), lambda i, j: (i, j, 0)),
                 per_b3((CONV_A_W - 1, D_CONV)), per_b3((SSD_CONV_W - 1, XBC_DIM)),
                 per_b3(PAIR_STATE))
    return pl.pallas_call(
        functools.partial(_mixer_kernel, has_hist, after is not None, ts, lc),
        out_shape=out_shape, grid=(b, s // ts), in_specs=in_specs, out_specs=out_specs,
        scratch_shapes=[pltpu.VMEM((SUBLANES + ts, D_CONV), F32), pltpu.VMEM((SUBLANES + ts, XBC_DIM), F32),
                        pltpu.VMEM(PAIR_STATE, F32)],
        compiler_params=pltpu.CompilerParams(dimension_semantics=("parallel", "arbitrary"),
                                             vmem_limit_bytes=VMEM_LIMIT),
        name="mixer_hist" if has_hist else "mixer",
    )(*args)


def _memkv_kernel(m_ref, g_ref, wk_ref, wv_ref, k_ref, v_ref):
    mn = _rms(m_ref[...], g_ref[...]).astype(BF16)
    k_ref[...] = jnp.dot(mn, wk_ref[...], preferred_element_type=F32)
    v_ref[...] = jnp.dot(mn, wv_ref[...], preferred_element_type=F32)


def _memory_kv(mem, g, wk, wv, tm=256):
    t = mem.shape[0]
    blk = pl.BlockSpec((tm, D_MODEL), lambda i: (i, 0))
    return pl.pallas_call(
        _memkv_kernel,
        out_shape=(jax.ShapeDtypeStruct((t, D_MODEL), F32),) * 2,
        grid=(t // tm,), in_specs=[blk, _const_spec(g.shape), _const_spec(wk.shape), _const_spec(wv.shape)],
        out_specs=(blk, blk),
        compiler_params=pltpu.CompilerParams(dimension_semantics=("parallel",), vmem_limit_bytes=VMEM_LIMIT),
        name="memory_kv",
    )(mem, g, wk, wv)


def _attn_kernel(x_ref, k_ref, v_ref, g_ref, wq_ref, wo_ref, o_ref):
    x = x_ref[...]
    xn = _rms(x, g_ref[...]).astype(BF16)
    q = jnp.dot(xn, wq_ref[...], preferred_element_type=F32)
    by_head = len(k_ref.shape) == 3
    wo = wo_ref[...]
    acc = x
    for h in range(MEM_HEADS):
        cols = slice(h * MEM_HEAD_DIM, (h + 1) * MEM_HEAD_DIM)
        k_h = k_ref[:, h, :] if by_head else k_ref[:, cols]
        v_h = v_ref[:, h, :] if by_head else v_ref[:, cols]
        sc = _dot_nt(q[:, cols], k_h) * (MEM_HEAD_DIM ** -0.5)
        sc = sc - jnp.max(sc, axis=-1, keepdims=True)
        p = jnp.exp(sc)
        p = p / jnp.sum(p, axis=-1, keepdims=True)
        o_h = _dot(p, v_h)
        acc = acc + jnp.dot(o_h.astype(BF16), wo[cols, :], preferred_element_type=F32)
    o_ref[...] = acc


def _attn(x, k, v, g, wq, wo, ts, kv0=0):
    b, s, _ = x.shape
    ts = min(ts, s)
    xblk = pl.BlockSpec((None, ts, D_MODEL), lambda i, j: (i, j, 0))
    if k.ndim == 3:
        kvblk = pl.BlockSpec((None, N_MEM, D_MODEL), lambda i, j: (i + kv0, 0, 0))
    else:
        kvblk = pl.BlockSpec((None, N_MEM, MEM_HEADS, MEM_HEAD_DIM), lambda i, j: (i + kv0, 0, 0, 0))
    return pl.pallas_call(
        _attn_kernel,
        out_shape=jax.ShapeDtypeStruct((b, s, D_MODEL), F32),
        grid=(b, s // ts),
        in_specs=[xblk, kvblk, kvblk, _const_spec(g.shape), _const_spec(wq.shape), _const_spec(wo.shape)],
        out_specs=xblk,
        compiler_params=pltpu.CompilerParams(dimension_semantics=("parallel", "parallel"),
                                             vmem_limit_bytes=VMEM_LIMIT),
        name="attn",
    )(x, k, v, g, wq, wo)


def _top_k_rows(sc, k, payload=None):
    n = sc.shape[0]
    rows = lax.broadcasted_iota(jnp.int32, sc.shape, 0).astype(F32)
    vals, outs = [], []
    for _ in range(k):
        m = jnp.max(sc, axis=0, keepdims=True)
        cand = jnp.where(sc == m, rows, float(n))
        first = jnp.min(cand, axis=0, keepdims=True)
        hit = cand == first
        sc = jnp.where(hit, -jnp.inf, sc)
        vals.append(m)
        if payload is None:
            outs.append(first.astype(jnp.int32))
        else:
            outs.append(jnp.sum(jnp.where(hit, payload, 0), axis=0, keepdims=True))
    return jnp.concatenate(vals, axis=0), jnp.concatenate(outs, axis=0)


_LIVE_B = tuple(PEER_TOPK // (a + 1) for a in range(1, SUBLANES))


def _pair_candidates(v1, i1, v2, i2):
    sub = lax.broadcasted_iota(jnp.int32, (SUBLANES, v1.shape[1]), 0)
    comb = [v1[0:1] + v2]
    cidx = [i1[0:1] * N_KEYS + i2]
    for a in range(1, SUBLANES):
        comb.append(jnp.where(sub < _LIVE_B[a - 1], v1[a:a + 1] + v2[0:SUBLANES], -jnp.inf))
        cidx.append(i1[a:a + 1] * N_KEYS + i2[0:SUBLANES])
    comb.append(v1[SUBLANES:] + v2[0:1])
    cidx.append(i1[SUBLANES:] * N_KEYS + i2[0:1])
    return jnp.concatenate(comb, axis=0), jnp.concatenate(cidx, axis=0)


def _select_kernel(tb, x_ref, g_ref, wq_ref, keys_ref, *rest):
    xn_ref, eid_ref, tile_ref, shift_ref, gate_ref, eid_s, gate_s = rest[-7:]
    xn = _rms(x_ref[...], g_ref[...])
    xn_ref[...] = xn
    xn = xn.astype(BF16)

    def head(h, carry):
        q = jnp.dot(xn, wq_ref[h], preferred_element_type=F32)
        sc1 = _dot_nt(keys_ref[h, 0], q[:, 0:PEER_DK_HALF])
        sc2 = _dot_nt(keys_ref[h, 1], q[:, PEER_DK_HALF:PEER_DK])
        rows = pl.ds(pl.multiple_of(h * PEER_TOPK, PEER_TOPK), PEER_TOPK)
        for c0 in range(0, tb, LANES):
            cols = slice(c0, c0 + LANES)
            v1, i1 = _top_k_rows(sc1[:, cols], PEER_TOPK)
            v2, i2 = _top_k_rows(sc2[:, cols], PEER_TOPK)
            comb, cidx = _pair_candidates(v1, i1, v2, i2)
            top, eid = _top_k_rows(comb, PEER_TOPK, cidx)
            ex = jnp.exp(top - top[0:1])
            gate_s[rows, cols] = ex / jnp.sum(ex, axis=0, keepdims=True)
            eid_s[rows, cols] = eid
        return carry

    lax.fori_loop(0, PEER_HEADS, head, 0)
    for c0 in range(0, tb, LANES):
        toks = slice(c0, c0 + LANES)
        eid = eid_s[:, toks].T
        upper = eid >= HALF_EXPERTS
        eid_ref[toks, :] = eid
        tile_ref[toks, :] = jnp.where(upper, eid - HALF_EXPERTS, eid) * SUBLANES
        shift_ref[toks, :] = jnp.where(upper, 0, 16)
        gate_ref[toks, :] = gate_s[:, toks].T


def _select(x, g, wq, keys, tb, t0, t, after=None):
    b0 = t0 // tb
    extra = [] if after is None else [after]
    xin = pl.BlockSpec((tb, D_MODEL), lambda i: (i + b0, 0))
    xblk = pl.BlockSpec((tb, D_MODEL), lambda i: (i, 0))
    picks_i = jax.ShapeDtypeStruct((t, PICKS), jnp.int32)
    return pl.pallas_call(
        functools.partial(_select_kernel, tb),
        out_shape=(jax.ShapeDtypeStruct((t, D_MODEL), F32), picks_i, picks_i, picks_i,
                   jax.ShapeDtypeStruct((t, PICKS), F32)),
        grid=(t // tb,),
        in_specs=[xin, _const_spec(g.shape), _const_spec(wq.shape), _const_spec(keys.shape)]
        + [_after_spec(a) for a in extra],
        out_specs=(xblk,) + (_pick_block(tb),) * 4,
        scratch_shapes=[pltpu.VMEM((PICKS, tb), jnp.int32), pltpu.VMEM((PICKS, tb), F32)],
        compiler_params=pltpu.CompilerParams(dimension_semantics=("parallel",), vmem_limit_bytes=VMEM_LIMIT),
        name="peer_select",
    )(x, g, wq, keys, *extra)


HI_HALF_MASK = -(1 << 16)
_BITREV3 = (0, 4, 2, 6, 1, 5, 3, 7)


def _unpack(word, shift):
    return lax.bitcast_convert_type((word << shift) & HI_HALF_MASK, F32)


def _splat_picks(blk, one_pass):
    tb = blk.shape[0]
    r = lax.broadcasted_iota(jnp.int32, (PICKS, LANES), 0)
    c = lax.broadcasted_iota(jnp.int32, (PICKS, LANES), 1)
    eye = (r == c)[None]
    ones = jnp.ones((LANES, LANES), BF16)

    def one(piece):
        diag = jnp.where(eye, piece[:, None, :], 0.0).astype(BF16).reshape(tb * PICKS, LANES)
        return jnp.dot(diag, ones, preferred_element_type=F32)

    if one_pass:
        out = one(blk)
    else:
        hi = blk.astype(BF16).astype(F32)
        rest = blk - hi
        mid = rest.astype(BF16).astype(F32)
        out = one(hi) + one(mid) + one(rest - mid)
    return out.reshape(tb, PICKS, LANES)


def _full(ref, k):
    return jnp.broadcast_to(ref[k:k + 1, :], (SUBLANES, LANES))


def _expert_row(tab_ref, tile_ref, t, k, shifts):
    idx = pl.multiple_of(tile_ref[t, k], SUBLANES)
    return _unpack(tab_ref[pl.ds(idx, SUBLANES), :], _full(shifts, k))


def _hidden_kernel(tb, tile_ref, shift_ref, x_ref, gate_ref, tab_ref, *rest):
    gh_ref, bsh, stage, parts, xt = rest[-5:]
    bsh[...] = _splat_picks(shift_ref[...].astype(F32), True).astype(jnp.int32)
    for s in range(SUBLANES):
        xt[pl.ds(s, tb, stride=SUBLANES), :] = x_ref[:, s * LANES:(s + 1) * LANES]

    half = PICKS // 2
    quad = SUBLANES // 2
    low = lax.broadcasted_iota(jnp.int32, (SUBLANES, LANES), 0) < quad

    def token(t, carry):
        x = xt[pl.ds(pl.multiple_of(t * SUBLANES, SUBLANES), SUBLANES), :]
        shifts = bsh.at[t]
        for j in range(half):
            a = x * _expert_row(tab_ref, tile_ref, t, j, shifts)
            b = x * _expert_row(tab_ref, tile_ref, t, j + half, shifts)
            stage[j * SUBLANES:(j + 1) * SUBLANES, :] = jnp.where(low, a + pltpu.roll(a, quad, 0),
                                                                  b + pltpu.roll(b, quad, 0))
        for p in range(2):
            part = stage[pl.ds(p * quad, half, stride=SUBLANES), :]
            for s in range(1, quad):
                part = part + stage[pl.ds(p * quad + s, half, stride=SUBLANES), :]
            parts[t, p * half:(p + 1) * half, :] = part
        return carry

    lax.fori_loop(0, tb, token, 0)
    gh_ref[...] = jnp.sum(parts[...], axis=-1)
    hid = gh_ref[...]
    gelu = 0.5 * hid * (1.0 + lax.erf(hid * (2.0 ** -0.5)))
    gh_ref[...] = gate_ref[...] * gelu


def _pick_block(tb, space=None):
    return pl.BlockSpec((tb, PICKS), lambda i: (i, 0), memory_space=space)


def _table_spec(tab):
    return pl.BlockSpec(tab.shape, lambda i: (0, 0), pipeline_mode=pl.Buffered(1))


def _hidden(tile, shift, xg, gate, tab, tb, after=None):
    t = xg.shape[0]
    extra = [] if after is None else [after]
    return pl.pallas_call(
        functools.partial(_hidden_kernel, tb),
        out_shape=jax.ShapeDtypeStruct((t, PICKS), F32),
        grid=(t // tb,),
        in_specs=[_pick_block(tb, pltpu.SMEM), _pick_block(tb),
                  pl.BlockSpec((tb, D_MODEL), lambda i: (i, 0)),
                  _pick_block(tb), _table_spec(tab)] + [_after_spec(a) for a in extra],
        out_specs=_pick_block(tb),
        scratch_shapes=[pltpu.VMEM((tb, PICKS, LANES), jnp.int32), pltpu.VMEM((PICKS // 2 * SUBLANES, LANES), F32),
                        pltpu.VMEM((tb, PICKS, LANES), F32), pltpu.VMEM((tb * SUBLANES, LANES), F32)],
        compiler_params=pltpu.CompilerParams(dimension_semantics=("parallel",), vmem_limit_bytes=VMEM_LIMIT),
        name="peer_hidden",
    )(tile, shift, xg, gate, tab, *extra)


def _combine_kernel(tb, tile_ref, shift_ref, gh_ref, tab_ref, x_ref, g_ref, *rest):
    o_ref, bsh, bg, ot = rest[-4:]
    bsh[...] = _splat_picks(shift_ref[...].astype(F32), True).astype(jnp.int32)
    bg[...] = _splat_picks(gh_ref[...], False)

    def token(t, carry):
        shifts, gates = bsh.at[t], bg.at[t]
        acc = [jnp.zeros((SUBLANES, LANES), F32) for _ in range(4)]
        for k in range(PICKS):
            acc[k % 4] = acc[k % 4] + _full(gates, k) * _expert_row(tab_ref, tile_ref, t, k, shifts)
        ot[pl.ds(pl.multiple_of(t * SUBLANES, SUBLANES), SUBLANES), :] = (acc[0] + acc[1]) + (acc[2] + acc[3])
        return carry

    lax.fori_loop(0, tb, token, 0)
    for s in range(SUBLANES):
        o_ref[:, s * LANES:(s + 1) * LANES] = ot[pl.ds(s, tb, stride=SUBLANES), :]
    o_ref[...] = _rms(x_ref[...] + o_ref[...], g_ref[...])


def _combine(tile, shift, gh, tab, x, g, tb, x_row0, out_row0, out_rows, into=None):
    t = gh.shape[0]
    xb0, ob0 = x_row0 // tb, out_row0 // tb
    extra = [] if into is None else [into]
    return pl.pallas_call(
        functools.partial(_combine_kernel, tb),
        out_shape=jax.ShapeDtypeStruct((out_rows, D_MODEL), F32),
        grid=(t // tb,),
        in_specs=[_pick_block(tb, pltpu.SMEM), _pick_block(tb), _pick_block(tb), _table_spec(tab),
                  pl.BlockSpec((tb, D_MODEL), lambda i: (i + xb0, 0)), _const_spec(g.shape)]
        + [pl.BlockSpec(memory_space=pl.ANY) for _ in extra],
        input_output_aliases={6: 0} if extra else {},
        out_specs=pl.BlockSpec((tb, D_MODEL), lambda i: (i + ob0, 0)),
        scratch_shapes=[pltpu.VMEM((tb, PICKS, LANES), jnp.int32), pltpu.VMEM((tb, PICKS, LANES), F32),
                        pltpu.VMEM((tb * SUBLANES, LANES), F32)],
        compiler_params=pltpu.CompilerParams(dimension_semantics=("parallel",), vmem_limit_bytes=VMEM_LIMIT),
        name="peer_combine",
    )(tile, shift, gh, tab, x, g, *extra)


SC_LANES = 16
SC_WORKERS = 32
SC_CHUNK = 32
SC_FEATURE_BLOCK = 256
SC_EARLY_SEQS = 3
SC_PROMPT_TOKENS = 14336


def _sc_call(body, n, side_width, out_width):
    mesh = plsc.VectorSubcoreMesh(core_axis_name="c", subcore_axis_name="s")
    return pl.kernel(body, mesh=mesh, out_type=jax.ShapeDtypeStruct((n, out_width), F32),
                     compiler_params=pltpu.CompilerParams(needs_layout_passes=False),
                     scratch_types=[pltpu.VMEM((2 * PICKS,), jnp.int32), pltpu.VMEM((2 * side_width,), F32),
                                    pltpu.VMEM((2 * out_width,), F32),
                                    pltpu.VMEM((SC_CHUNK, D_MODEL), F32), pltpu.VMEM((SC_CHUNK, D_MODEL), F32)]
                     + [pltpu.SemaphoreType.DMA] * 5)


def _sc_token_loop(n, eid_hbm, side_hbm, tab_hbm, out_hbm, idx_v, side_v, out_v, bufs, sems, compute):
    sem_rows, sem_fetch, sem_put = sems[0:2], sems[2], sems[3:5]
    per_worker = n // SC_WORKERS
    first = (lax.axis_index("s") * 2 + lax.axis_index("c")) * per_worker
    last = first + per_worker - 1
    chunks = PICKS // SC_CHUNK
    side_w = side_v.shape[0] // 2
    out_w = out_v.shape[0] // 2

    def fetch(t, slot):
        return (pltpu.make_async_copy(eid_hbm.at[t], idx_v.at[pl.ds(slot * PICKS, PICKS)], sem_fetch),
                pltpu.make_async_copy(side_hbm.at[t], side_v.at[pl.ds(slot * side_w, side_w)], sem_fetch))

    def gather(slot, c):
        ids = idx_v.at[pl.ds(slot * PICKS + c * SC_CHUNK, SC_CHUNK)]
        return pltpu.make_async_copy(tab_hbm.at[ids], bufs[c % 2], sem_rows[c % 2])

    def put(t, slot):
        return pltpu.make_async_copy(out_v.at[pl.ds(slot * out_w, out_w)], out_hbm.at[t], sem_put[slot])

    for cp in fetch(first, 0):
        cp.start()
    for cp in fetch(first, 0):
        cp.wait()
    gather(0, 0).start()
    gather(0, 1).start()

    @pl.loop(0, per_worker, step=2)
    def _(i):
        for slot in range(2):
            t = first + i + slot
            nxt = jnp.minimum(t + 1, last)
            for cp in fetch(nxt, 1 - slot):
                cp.start()

            @pl.when(i >= 2)
            def _():
                put(t, slot).wait()

            for c in range(chunks):
                gather(slot, c).wait()
                compute(c, bufs[c % 2], slot * side_w, slot * out_w)
                if c + 2 < chunks:
                    gather(slot, c + 2).start()
                else:
                    if c + 2 == chunks:
                        for cp in fetch(nxt, 1 - slot):
                            cp.wait()
                    gather(1 - slot, c + 2 - chunks).start()
            put(t, slot).start()

    gather(0, 0).wait()
    gather(0, 1).wait()
    put(last, 0).wait()
    put(last, 1).wait()


def _sc_hidden(eid, x, tab):
    n = eid.shape[0]
    lanes = SC_LANES

    def body(eid_hbm, x_hbm, tab_hbm, out_hbm, idx_v, x_v, hid_v, rows0, rows1, *sems):
        lane = lax.iota(jnp.int32, lanes)

        def compute(c, rows, x0, h0):
            for g in range(SC_CHUNK // lanes):
                def step(j, accs):
                    off = pl.multiple_of(j * lanes, lanes)
                    xj = x_v[pl.ds(x0 + off, lanes)]
                    return tuple(accs[r] + xj * rows[g * lanes + r, pl.ds(off, lanes)] for r in range(lanes))
                accs = lax.fori_loop(0, D_MODEL // lanes, step,
                                     tuple(jnp.zeros((lanes,), F32) for _ in range(lanes)))
                hv = jnp.zeros((lanes,), F32)
                for r in range(lanes):
                    hv = jnp.where(lane == r, jnp.sum(accs[r]), hv)
                hid_v[pl.ds(h0 + c * SC_CHUNK + g * lanes, lanes)] = hv

        _sc_token_loop(n, eid_hbm, x_hbm, tab_hbm, out_hbm, idx_v, x_v, hid_v, (rows0, rows1), sems, compute)

    return _sc_call(body, n, D_MODEL, PICKS)(eid, x, tab)


def _sc_combine(eid, gh, tab):
    n = eid.shape[0]
    lanes = SC_LANES
    nacc = SC_FEATURE_BLOCK // lanes

    def body(eid_hbm, g_hbm, tab_hbm, out_hbm, idx_v, g_v, out_v, rows0, rows1, *sems):
        def compute(c, rows, g0, o0):
            for fb in range(D_MODEL // SC_FEATURE_BLOCK):
                f0 = fb * SC_FEATURE_BLOCK
                if c == 0:
                    init = tuple(jnp.zeros((lanes,), F32) for _ in range(nacc))
                else:
                    init = tuple(out_v[pl.ds(o0 + f0 + q * lanes, lanes)] for q in range(nacc))

                def step(r, accs):
                    gk = plsc.load_gather(g_v, [jnp.full((lanes,), g0 + c * SC_CHUNK + r, jnp.int32)])
                    return tuple(accs[q] + gk * rows[r, pl.ds(f0 + q * lanes, lanes)] for q in range(nacc))
                accs = lax.fori_loop(0, SC_CHUNK, step, init)
                for q in range(nacc):
                    out_v[pl.ds(o0 + f0 + q * lanes, lanes)] = accs[q]

        _sc_token_loop(n, eid_hbm, g_hbm, tab_hbm, out_hbm, idx_v, g_v, out_v, (rows0, rows1), sems, compute)

    return _sc_call(body, n, PICKS, D_MODEL)(eid, gh, tab)


def _after_spec(after):
    return pl.BlockSpec((SUBLANES, after.shape[1]), lambda i: (0, 0))


def _gate_gelu_kernel(h_ref, g_ref, after_ref, o_ref):
    hid = h_ref[...]
    o_ref[...] = g_ref[...] * (0.5 * hid * (1.0 + lax.erf(hid * (2.0 ** -0.5))))


def _gate_gelu(hid, gate, tb, after):
    t = hid.shape[0]
    return pl.pallas_call(
        _gate_gelu_kernel, out_shape=jax.ShapeDtypeStruct((t, PICKS), F32), grid=(t // tb,),
        in_specs=[_pick_block(tb), _pick_block(tb), _after_spec(after)], out_specs=_pick_block(tb),
        compiler_params=pltpu.CompilerParams(dimension_semantics=("parallel",)),
        name="gate_gelu",
    )(hid, gate, after)


def _final_kernel(x_ref, p_ref, g_ref, *rest):
    rest[-1][...] = _rms(x_ref[...] + p_ref[...], g_ref[...])


def _final(x, p, g, tb, out_row0, out_rows, into=None):
    b0 = out_row0 // tb
    local = pl.BlockSpec((tb, D_MODEL), lambda i: (i, 0))
    in_specs = [local, local, _const_spec(g.shape)]
    args = [x, p, g]
    aliases = {}
    if into is not None:
        aliases = {len(args): 0}
        in_specs.append(pl.BlockSpec(memory_space=pl.ANY))
        args.append(into)
    return pl.pallas_call(
        _final_kernel, out_shape=jax.ShapeDtypeStruct((out_rows, D_MODEL), F32), grid=(p.shape[0] // tb,),
        in_specs=in_specs, out_specs=pl.BlockSpec((tb, D_MODEL), lambda i: (i + b0, 0)),
        input_output_aliases=aliases,
        compiler_params=pltpu.CompilerParams(dimension_semantics=("parallel",), vmem_limit_bytes=VMEM_LIMIT),
        name="final_norm",
    )(*args)


PACK_BLOCK = 256


def _pack_kernel(lo_ref, hi_ref, o_ref):
    def bf16_bits(x):
        return lax.bitcast_convert_type(x.astype(BF16).astype(F32), jnp.int32)
    word = bf16_bits(hi_ref[...]) | lax.shift_right_logical(bf16_bits(lo_ref[...]), 16)
    for s in range(SUBLANES):
        o_ref[pl.ds(s, PACK_BLOCK, stride=SUBLANES), :] = word[:, s * LANES:(s + 1) * LANES]


def _pack_table(tab):
    steps = HALF_EXPERTS // PACK_BLOCK
    return pl.pallas_call(
        _pack_kernel, out_shape=jax.ShapeDtypeStruct((HALF_EXPERTS * SUBLANES, LANES), jnp.int32), grid=(steps,),
        in_specs=[pl.BlockSpec((PACK_BLOCK, D_MODEL), lambda i: (i, 0)),
                  pl.BlockSpec((PACK_BLOCK, D_MODEL), lambda i: (i + steps, 0))],
        out_specs=pl.BlockSpec((PACK_BLOCK * SUBLANES, LANES), lambda i: (i, 0)),
        compiler_params=pltpu.CompilerParams(dimension_semantics=("parallel",)),
        name="pack_table",
    )(tab, tab)


def _col(v, n=LANES):
    return jnp.pad(v.astype(F32), (0, n - v.shape[0])).reshape(1, n)


def _row(v, n=2 * SUBLANES):
    return jnp.pad(v.astype(F32), (0, n - v.shape[0])).reshape(n, 1)


def _peer(x, w, tb_sel, tb_exp, n_sc, out_row0=0, out_rows=None, after=None, into=None):
    t = x.shape[0]
    out_rows = t if out_rows is None else out_rows
    n_tc = t - n_sc
    if n_sc:
        xn_s, eid_s, _, _, gate_s = _select(x, w["ln_ffn"], w["w_pq"], w["sub_keys"], tb_sel, 0, n_sc, after=after)
        hid_s = _sc_hidden(eid_s, xn_s, w["u_tab"])
    xn, _, tile, shift, gate = _select(x, w["ln_ffn"], w["w_pq"], w["sub_keys"], tb_sel, n_sc, n_tc,
                                       after=xn_s if n_sc else None)
    gh = _hidden(tile, shift, xn, gate, w["u_pack"], tb_exp, after=into)
    if n_sc:
        out_s = _sc_combine(eid_s, _gate_gelu(hid_s, gate_s, tb_sel, after=gh), w["v_tab"])
    y = _combine(tile, shift, gh, w["v_pack"], x, w["final_norm"], tb_exp, n_sc, out_row0 + n_sc, out_rows, into)
    if n_sc:
        y = _final(x, out_s, w["final_norm"], tb_sel, out_row0, out_rows, into=y)
    return y


def kernel(x_prompt, x_sample, mem_prompt, cache_conv_a, cache_conv_ssd, state_ssd, cache_mem_k, cache_mem_v, ln_mix_w, w_in, conv_a_w, conv_s_w, conv_s_b, dt_bias, a_log, d_skip, ssd_norm_w, w_out, ln_mem_w, mem_norm_w, w_mq, w_mk, w_mv, w_mo, ln_ffn_w, w_pq, sub_keys, u_tab, v_tab, final_norm_w):
    depth = w_in.shape[0]
    assert depth == 1
    l = 0
    n_main = 3 * D_CONV + D_SSD + XBC_DIM
    w_dt = w_in[l][:, n_main:]
    w = {
        "ln_mix": ln_mix_w[l].reshape(1, D_MODEL),
        "w_in": w_in[l][:, :n_main].astype(BF16),
        "w_dt": jnp.pad(w_dt, ((0, 0), (0, LANES - SSD_HEADS))).astype(BF16),
        "w_dtt": jnp.pad(w_dt.T, ((0, 2 * SUBLANES - SSD_HEADS), (0, 0))).astype(BF16),
        "conv_a_w": conv_a_w[l], "conv_s_w": conv_s_w[l], "conv_s_b": conv_s_b[l].reshape(1, XBC_DIM),
        "dtb_c": _col(dt_bias[l]), "alog_c": _col(a_log[l]), "dsk_c": _col(d_skip[l]),
        "dtb_r": _row(dt_bias[l]), "alog_r": _row(a_log[l]),
        "ssd_norm": ssd_norm_w[l].reshape(1, D_SSD),
        "w_out": w_out[l].astype(BF16),
        "ln_ffn": ln_ffn_w[l].reshape(1, D_MODEL),
        "w_pq": w_pq[l].astype(BF16).reshape(D_MODEL, PEER_HEADS, PEER_DK).transpose(1, 0, 2),
        "sub_keys": sub_keys[l].astype(BF16),
        "u_pack": _pack_table(u_tab[l]),
        "v_pack": _pack_table(v_tab[l]),
        "u_tab": u_tab[l], "v_tab": v_tab[l],
        "final_norm": final_norm_w.reshape(1, D_MODEL),
    }
    ln_mem = ln_mem_w[l].reshape(1, D_MODEL)
    wq, wo = w_mq[l].astype(BF16), w_mo[l].astype(BF16)

    bp, sp, _ = x_prompt.shape
    bs, ss, _ = x_sample.shape

    mk, mv = _memory_kv(mem_prompt.reshape(bp * N_MEM, D_MODEL), mem_norm_w[l].reshape(1, D_MODEL),
                        w_mk[l].astype(BF16), w_mv[l].astype(BF16))
    mk3, mv3 = mk.reshape(bp, N_MEM, D_MODEL), mv.reshape(bp, N_MEM, D_MODEL)
    n1 = SC_EARLY_SEQS
    h1, pa1, ps1, ph1 = _mixer(x_prompt, None, w, 256, 0, n1)
    h1 = _attn(h1, mk3, mv3, ln_mem, wq, wo, 256).reshape(n1 * sp, D_MODEL)

    xn1, eid1, _, _, gate1 = _select(h1, w["ln_ffn"], w["w_pq"], w["sub_keys"], 256, 0, n1 * sp)
    hid1 = _sc_hidden(eid1, xn1, w["u_tab"])
    h2, pa2, ps2, ph2 = _mixer(x_prompt, None, w, 256, n1, bp - n1, after=xn1)
    h2 = _attn(h2, mk3, mv3, ln_mem, wq, wo, 256, kv0=n1).reshape((bp - n1) * sp, D_MODEL)
    gh1 = _gate_gelu(hid1, gate1, 256, after=h2)
    out1 = _sc_combine(eid1, gh1, w["v_tab"])
    y1 = _final(h1, out1, w["final_norm"], 256, 0, bp * sp)
    y_prompt = _peer(h2, w, 256, 64, SC_PROMPT_TOKENS, n1 * sp, bp * sp, after=gh1, into=y1).reshape(bp, sp, D_MODEL)
    pa, ps, ph = (jnp.concatenate(p, axis=0) for p in ((pa1, pa2), (ps1, ps2), (ph1, ph2)))

    hs, sa, ssd, sh = _mixer(x_sample, (cache_conv_a[l], cache_conv_ssd[l],
                                        state_ssd[l].reshape((bs,) + PAIR_STATE)), w, 256)
    hs = _attn(hs, cache_mem_k[l], cache_mem_v[l], ln_mem, wq, wo, 256)
    y_sample = _peer(hs.reshape(bs * ss, D_MODEL), w, 256, 64, 0).reshape(bs, ss, D_MODEL)

    kv_shape = (1, bp, N_MEM, MEM_HEADS, MEM_HEAD_DIM)
    st_shape = (SSD_HEADS, SSD_HEAD_DIM, SSD_STATE)
    return (y_prompt, y_sample, pa[None], ps[None], ph.reshape((1, bp) + st_shape), mk.reshape(kv_shape),
            mv.reshape(kv_shape), sa[None], ssd[None], sh.reshape((1, bs) + st_shape))
```

```python
import functools
import math

import jax
import jax.numpy as jnp
from jax import lax
from jax.experimental import pallas as pl
from jax.experimental.pallas import tpu as pltpu
from jax.experimental.pallas import tpu_sc as plsc

D_MODEL = 1024
CHUNK = 64
D_CONV = 512
CONV_A_W = 3
D_SSD = 512
SSD_HEAD_DIM = 64
SSD_HEADS = 8
SSD_GROUPS = 2
SSD_HPG = 4
SSD_STATE = 128
SSD_CONV_W = 4
XBC_DIM = 1024
N_MEM = 256
MEM_HEADS = 4
MEM_HEAD_DIM = 256
PEER_HEADS = 8
N_KEYS = 128
N_EXPERTS = N_KEYS * N_KEYS
PEER_TOPK = 16
PEER_DK = 256
PEER_DK_HALF = 128
PICKS = PEER_HEADS * PEER_TOPK
EPS = 1e-6

LANES = 128
SUBLANES = 8
HALF_EXPERTS = N_EXPERTS // 2
PAIR_STATE = (SSD_HEADS // 2, 2 * SSD_HEAD_DIM, SSD_STATE)
VMEM_LIMIT = 56 * 1024 * 1024

F32 = jnp.float32
BF16 = jnp.bfloat16
HI = lax.Precision.HIGHEST


def _rms(x, g):
    return x * lax.rsqrt(jnp.mean(x * x, axis=-1, keepdims=True) + EPS) * g


def _silu(x):
    return x * (1.0 / (1.0 + jnp.exp(-x)))


def _softplus(x):
    return jnp.maximum(x, 0.0) + jnp.log1p(jnp.exp(-jnp.abs(x)))


def _dot(a, b):
    return jnp.dot(a.astype(BF16), b.astype(BF16), preferred_element_type=F32)


def _dot_nt(a, b):
    return lax.dot_general(a.astype(BF16), b.astype(BF16), (((1,), (1,)), ((), ())),
                           preferred_element_type=F32)


def _dot_tn(a, b):
    return lax.dot_general(a.astype(BF16), b.astype(BF16), (((0,), (0,)), ((), ())),
                           preferred_element_type=F32)


def _const_spec(shape):
    n = len(shape)
    return pl.BlockSpec(shape, lambda *_: (0,) * n)


def _mixer_kernel(has_hist, has_after, ts, lc, *refs):
    if has_after:
        refs = refs[1:]
    if has_hist:
        x_ref, hista_ref, hists_ref, h0_ref = refs[:4]
        refs = refs[4:]
    else:
        x_ref = refs[0]
        refs = refs[1:]
    (lnw_ref, win_ref, wdt_ref, wdtt_ref, caw_ref, csw_ref, csb_ref, dtb_c_ref, alog_c_ref, dsk_c_ref,
     dtb_r_ref, alog_r_ref, nw_ref, wout_ref,
     h1_ref, na_ref, ns_ref, nh_ref, cata, cats, hst) = refs
    s = pl.program_id(1)
    pad = SUBLANES

    @pl.when(s == 0)
    def _():
        if has_hist:
            cata[pad - 2:pad, :] = hista_ref[...]
            cats[pad - 3:pad, :] = hists_ref[...]
            hst[...] = h0_ref[...]
        else:
            cata[0:pad, :] = jnp.zeros((pad, D_CONV), F32)
            cats[0:pad, :] = jnp.zeros((pad, XBC_DIM), F32)
            hst[...] = jnp.zeros(hst.shape, F32)

    x = x_ref[...]
    xn = _rms(x, lnw_ref[...]).astype(BF16)
    proj = jnp.dot(xn, win_ref[...], preferred_element_type=F32)
    dt_c = jnp.dot(xn, wdt_ref[...], preferred_element_type=F32)
    dt_r = lax.dot_general(wdtt_ref[...], xn, (((1,), (1,)), ((), ())),
                           preferred_element_type=F32)
    g_b = proj[:, 0:D_CONV]
    g_c = proj[:, D_CONV:2 * D_CONV]
    v_in = proj[:, 2 * D_CONV:3 * D_CONV]
    z = proj[:, 3 * D_CONV:3 * D_CONV + D_SSD]
    xbc = proj[:, 3 * D_CONV + D_SSD:3 * D_CONV + D_SSD + XBC_DIM]

    ua = g_c * v_in
    cata[pad:pad + ts, :] = ua
    caw = caw_ref[...]
    conv_a = (caw[0:1] * cata[pad - 2:pad - 2 + ts, :] + caw[1:2] * cata[pad - 1:pad - 1 + ts, :]
              + caw[2:3] * ua)
    y_a = g_b * conv_a
    hist_a = cata[pad + ts - 2:pad + ts, :]
    na_ref[...] = hist_a
    cata[pad - 2:pad, :] = hist_a

    cats[pad:pad + ts, :] = xbc
    csw = csw_ref[...]
    conv_s = (csw[0:1] * cats[pad - 3:pad - 3 + ts, :] + csw[1:2] * cats[pad - 2:pad - 2 + ts, :]
              + csw[2:3] * cats[pad - 1:pad - 1 + ts, :] + csw[3:4] * xbc)
    hist_s = cats[pad + ts - 3:pad + ts, :]
    ns_ref[...] = hist_s
    cats[pad - 3:pad, :] = hist_s
    xbc_c = _silu(conv_s + csb_ref[...])
    xs = xbc_c[:, 0:D_SSD]

    dtp_c = _softplus(dt_c + dtb_c_ref[...])
    dtp_r = _softplus(dt_r + dtb_r_ref[...])
    a_c = dtp_c * (-jnp.exp(alog_c_ref[...]))
    a_r = dtp_r * (-jnp.exp(alog_r_ref[...]))
    dsk_c = dsk_c_ref[...]

    lane = lax.broadcasted_iota(jnp.int32, (1, LANES), 1)
    first_head = lane < SSD_HEAD_DIM
    row2 = lax.broadcasted_iota(jnp.int32, (2 * SSD_HEAD_DIM, 1), 0) < SSD_HEAD_DIM
    li = lax.broadcasted_iota(jnp.int32, (lc, lc), 0)
    si = lax.broadcasted_iota(jnp.int32, (lc, lc), 1)
    causal = si <= li
    tril = causal.astype(F32)
    jrow = lax.broadcasted_iota(jnp.int32, (ts, lc), 0)
    scol = lax.broadcasted_iota(jnp.int32, (ts, lc), 1)

    def pair(col, h):
        return jnp.where(first_head, col[:, h:h + 1], col[:, h + 1:h + 2])

    y_chunks = []
    for c in range(ts // lc):
        r0 = c * lc
        rows = slice(r0, r0 + lc)
        acum_c = jnp.dot(tril, a_c[rows], preferred_element_type=F32, precision=HI)
        upper = ((jrow >= r0) & (jrow <= r0 + scol)).astype(F32)
        acum_r = jnp.dot(a_r, upper, preferred_element_type=F32, precision=HI)
        last_c = acum_c[lc - 1:lc, :]
        dec_c = jnp.exp(last_c - acum_c)
        eac_c = jnp.exp(acum_c)
        y_pairs = []
        for g in range(SSD_GROUPS):
            b_g = xbc_c[rows, D_SSD + g * SSD_STATE:D_SSD + (g + 1) * SSD_STATE]
            c_g = xbc_c[rows, D_SSD + (SSD_GROUPS + g) * SSD_STATE:D_SSD + (SSD_GROUPS + g + 1) * SSD_STATE]
            cb = _dot_nt(c_g, b_g)
            for q in range(SSD_HPG // 2):
                h = g * SSD_HPG + 2 * q
                pi = h // 2
                xs_p = xs[rows, pi * LANES:(pi + 1) * LANES]
                xdt = xs_p * pair(dtp_c[rows], h)
                res = []
                for hh in (h, h + 1):
                    seg = acum_c[:, hh:hh + 1] - acum_r[hh:hh + 1, :]
                    m_h = cb * jnp.exp(jnp.where(causal, seg, -jnp.inf))
                    res.append(_dot(m_h, xdt))
                y_diag = jnp.where(first_head, res[0], res[1])
                h_in = hst[pi]
                y_off = _dot_nt(c_g, h_in) * pair(eac_c, h)
                st = _dot_tn(xdt * pair(dec_c, h), b_g)
                cd = jnp.exp(jnp.where(row2, last_c[:, h:h + 1], last_c[:, h + 1:h + 2]))
                hst[pi] = cd * h_in + st
                y_pairs.append(y_diag + y_off + pair(dsk_c, h) * xs_p)
        y_chunks.append(jnp.concatenate(y_pairs, axis=1))
    y = y_chunks[0] if len(y_chunks) == 1 else jnp.concatenate(y_chunks, axis=0)
    y_b = _rms(y * _silu(z), nw_ref[...])
    wout = wout_ref[...]
    out = (jnp.dot(y_a.astype(BF16), wout[0:D_CONV], preferred_element_type=F32)
           + jnp.dot(y_b.astype(BF16), wout[D_CONV:], preferred_element_type=F32))
    h1_ref[...] = x + out
    nh_ref[...] = hst[...]


def _mixer(x, hist, w, ts, b0=0, nb=None, after=None):
    _, s, _ = x.shape
    b = x.shape[0] if nb is None else nb
    lc = min(CHUNK, s)
    ts = min(ts, s)
    has_hist = hist is not None
    assert not (has_hist and b0)
    per_b3 = lambda shape: pl.BlockSpec((None,) + shape, lambda i, j: (i,) + (0,) * len(shape))
    in_specs = [pl.BlockSpec((None, ts, D_MODEL), lambda i, j: (i + b0, j, 0))]
    args = [x]
    if after is not None:
        in_specs.insert(0, pl.BlockSpec((SUBLANES, after.shape[1]), lambda i, j: (0, 0)))
        args.insert(0, after)
    if has_hist:
        in_specs += [per_b3((CONV_A_W - 1, D_CONV)), per_b3((SSD_CONV_W - 1, XBC_DIM)),
                     per_b3(PAIR_STATE)]
        args += list(hist)
    wargs = [w["ln_mix"], w["w_in"], w["w_dt"], w["w_dtt"], w["conv_a_w"], w["conv_s_w"], w["conv_s_b"],
             w["dtb_c"], w["alog_c"], w["dsk_c"], w["dtb_r"], w["alog_r"], w["ssd_norm"], w["w_out"]]
    in_specs += [_const_spec(a.shape) for a in wargs]
    args += wargs
    out_shape = (jax.ShapeDtypeStruct((b, s, D_MODEL), F32),
                 jax.ShapeDtypeStruct((b, CONV_A_W - 1, D_CONV), F32),
                 jax.ShapeDtypeStruct((b, SSD_CONV_W - 1, XBC_DIM), F32),
                 jax.ShapeDtypeStruct((b,) + PAIR_STATE, F32))
    out_specs = (pl.BlockSpec((None, ts, D_MODEL), lambda i, j: (i, j, 0)),
                 per_b3((CONV_A_W - 1, D_CONV)), per_b3((SSD_CONV_W - 1, XBC_DIM)),
                 per_b3(PAIR_STATE))
    return pl.pallas_call(
        functools.partial(_mixer_kernel, has_hist, after is not None, ts, lc),
        out_shape=out_shape, grid=(b, s // ts), in_specs=in_specs, out_specs=out_specs,
        scratch_shapes=[pltpu.VMEM((SUBLANES + ts, D_CONV), F32), pltpu.VMEM((SUBLANES + ts, XBC_DIM), F32),
                        pltpu.VMEM(PAIR_STATE, F32)],
        compiler_params=pltpu.CompilerParams(dimension_semantics=("parallel", "arbitrary"),
                                             vmem_limit_bytes=VMEM_LIMIT),
        name="mixer_hist" if has_hist else "mixer",
    )(*args)


def _memkv_kernel(m_ref, g_ref, wk_ref, wv_ref, k_ref, v_ref):
    mn = _rms(m_ref[...], g_ref[...]).astype(BF16)
    k_ref[...] = jnp.dot(mn, wk_ref[...], preferred_element_type=F32)
    v_ref[...] = jnp.dot(mn, wv_ref[...], preferred_element_type=F32)


def _memory_kv(mem, g, wk, wv, tm=256):
    t = mem.shape[0]
    blk = pl.BlockSpec((tm, D_MODEL), lambda i: (i, 0))
    return pl.pallas_call(
        _memkv_kernel,
        out_shape=(jax.ShapeDtypeStruct((t, D_MODEL), F32),) * 2,
        grid=(t // tm,), in_specs=[blk, _const_spec(g.shape), _const_spec(wk.shape), _const_spec(wv.shape)],
        out_specs=(blk, blk),
        compiler_params=pltpu.CompilerParams(dimension_semantics=("parallel",), vmem_limit_bytes=VMEM_LIMIT),
        name="memory_kv",
    )(mem, g, wk, wv)


def _attn_kernel(x_ref, k_ref, v_ref, g_ref, wq_ref, wo_ref, o_ref):
    x = x_ref[...]
    xn = _rms(x, g_ref[...]).astype(BF16)
    q = jnp.dot(xn, wq_ref[...], preferred_element_type=F32)
    by_head = len(k_ref.shape) == 3
    wo = wo_ref[...]
    acc = x
    for h in range(MEM_HEADS):
        cols = slice(h * MEM_HEAD_DIM, (h + 1) * MEM_HEAD_DIM)
        k_h = k_ref[:, h, :] if by_head else k_ref[:, cols]
        v_h = v_ref[:, h, :] if by_head else v_ref[:, cols]
        sc = _dot_nt(q[:, cols], k_h) * (MEM_HEAD_DIM ** -0.5)
        sc = sc - jnp.max(sc, axis=-1, keepdims=True)
        p = jnp.exp(sc)
        p = p / jnp.sum(p, axis=-1, keepdims=True)
        o_h = _dot(p, v_h)
        acc = acc + jnp.dot(o_h.astype(BF16), wo[cols, :], preferred_element_type=F32)
    o_ref[...] = acc


def _attn(x, k, v, g, wq, wo, ts, kv0=0):
    b, s, _ = x.shape
    ts = min(ts, s)
    xblk = pl.BlockSpec((None, ts, D_MODEL), lambda i, j: (i, j, 0))
    if k.ndim == 3:
        kvblk = pl.BlockSpec((None, N_MEM, D_MODEL), lambda i, j: (i + kv0, 0, 0))
    else:
        kvblk = pl.BlockSpec((None, N_MEM, MEM_HEADS, MEM_HEAD_DIM), lambda i, j: (i + kv0, 0, 0, 0))
    return pl.pallas_call(
        _attn_kernel,
        out_shape=jax.ShapeDtypeStruct((b, s, D_MODEL), F32),
        grid=(b, s // ts),
        in_specs=[xblk, kvblk, kvblk, _const_spec(g.shape), _const_spec(wq.shape), _const_spec(wo.shape)],
        out_specs=xblk,
        compiler_params=pltpu.CompilerParams(dimension_semantics=("parallel", "parallel"),
                                             vmem_limit_bytes=VMEM_LIMIT),
        name="attn",
    )(x, k, v, g, wq, wo)


def _top_k_rows(sc, k, payload=None):
    n = sc.shape[0]
    rows = lax.broadcasted_iota(jnp.int32, sc.shape, 0).astype(F32)
    vals, outs = [], []
    for _ in range(k):
        m = jnp.max(sc, axis=0, keepdims=True)
        cand = jnp.where(sc == m, rows, float(n))
        first = jnp.min(cand, axis=0, keepdims=True)
        hit = cand == first
        sc = jnp.where(hit, -jnp.inf, sc)
        vals.append(m)
        if payload is None:
            outs.append(first.astype(jnp.int32))
        else:
            outs.append(jnp.sum(jnp.where(hit, payload, 0), axis=0, keepdims=True))
    return jnp.concatenate(vals, axis=0), jnp.concatenate(outs, axis=0)


_LIVE_B = tuple(PEER_TOPK // (a + 1) for a in range(1, SUBLANES))


def _pair_candidates(v1, i1, v2, i2):
    sub = lax.broadcasted_iota(jnp.int32, (SUBLANES, v1.shape[1]), 0)
    comb = [v1[0:1] + v2]
    cidx = [i1[0:1] * N_KEYS + i2]
    for a in range(1, SUBLANES):
        comb.append(jnp.where(sub < _LIVE_B[a - 1], v1[a:a + 1] + v2[0:SUBLANES], -jnp.inf))
        cidx.append(i1[a:a + 1] * N_KEYS + i2[0:SUBLANES])
    comb.append(v1[SUBLANES:] + v2[0:1])
    cidx.append(i1[SUBLANES:] * N_KEYS + i2[0:1])
    return jnp.concatenate(comb, axis=0), jnp.concatenate(cidx, axis=0)


def _select_kernel(tb, x_ref, g_ref, wq_ref, keys_ref, *rest):
    xn_ref, eid_ref, tile_ref, shift_ref, gate_ref, eid_s, gate_s = rest[-7:]
    xn = _rms(x_ref[...], g_ref[...])
    xn_ref[...] = xn
    xn = xn.astype(BF16)

    def head(h, carry):
        q = jnp.dot(xn, wq_ref[h], preferred_element_type=F32)
        sc1 = _dot_nt(keys_ref[h, 0], q[:, 0:PEER_DK_HALF])
        sc2 = _dot_nt(keys_ref[h, 1], q[:, PEER_DK_HALF:PEER_DK])
        rows = pl.ds(pl.multiple_of(h * PEER_TOPK, PEER_TOPK), PEER_TOPK)
        for c0 in range(0, tb, LANES):
            cols = slice(c0, c0 + LANES)
            v1, i1 = _top_k_rows(sc1[:, cols], PEER_TOPK)
            v2, i2 = _top_k_rows(sc2[:, cols], PEER_TOPK)
            comb, cidx = _pair_candidates(v1, i1, v2, i2)
            top, eid = _top_k_rows(comb, PEER_TOPK, cidx)
            ex = jnp.exp(top - top[0:1])
            gate_s[rows, cols] = ex / jnp.sum(ex, axis=0, keepdims=True)
            eid_s[rows, cols] = eid
        return carry

    lax.fori_loop(0, PEER_HEADS, head, 0)
    for c0 in range(0, tb, LANES):
        toks = slice(c0, c0 + LANES)
        eid = eid_s[:, toks].T
        upper = eid >= HALF_EXPERTS
        eid_ref[toks, :] = eid
        tile_ref[toks, :] = jnp.where(upper, eid - HALF_EXPERTS, eid) * SUBLANES
        shift_ref[toks, :] = jnp.where(upper, 0, 16)
        gate_ref[toks, :] = gate_s[:, toks].T


def _select(x, g, wq, keys, tb, t0, t, after=None):
    b0 = t0 // tb
    extra = [] if after is None else [after]
    xin = pl.BlockSpec((tb, D_MODEL), lambda i: (i + b0, 0))
    xblk = pl.BlockSpec((tb, D_MODEL), lambda i: (i, 0))
    picks_i = jax.ShapeDtypeStruct((t, PICKS), jnp.int32)
    return pl.pallas_call(
        functools.partial(_select_kernel, tb),
        out_shape=(jax.ShapeDtypeStruct((t, D_MODEL), F32), picks_i, picks_i, picks_i,
                   jax.ShapeDtypeStruct((t, PICKS), F32)),
        grid=(t // tb,),
        in_specs=[xin, _const_spec(g.shape), _const_spec(wq.shape), _const_spec(keys.shape)]
        + [_after_spec(a) for a in extra],
        out_specs=(xblk,) + (_pick_block(tb),) * 4,
        scratch_shapes=[pltpu.VMEM((PICKS, tb), jnp.int32), pltpu.VMEM((PICKS, tb), F32)],
        compiler_params=pltpu.CompilerParams(dimension_semantics=("parallel",), vmem_limit_bytes=VMEM_LIMIT),
        name="peer_select",
    )(x, g, wq, keys, *extra)


HI_HALF_MASK = -(1 << 16)
_BITREV3 = (0, 4, 2, 6, 1, 5, 3, 7)


def _unpack(word, shift):
    return lax.bitcast_convert_type((word << shift) & HI_HALF_MASK, F32)


def _splat_picks(blk, one_pass):
    tb = blk.shape[0]
    r = lax.broadcasted_iota(jnp.int32, (PICKS, LANES), 0)
    c = lax.broadcasted_iota(jnp.int32, (PICKS, LANES), 1)
    eye = (r == c)[None]
    ones = jnp.ones((LANES, LANES), BF16)

    def one(piece):
        diag = jnp.where(eye, piece[:, None, :], 0.0).astype(BF16).reshape(tb * PICKS, LANES)
        return jnp.dot(diag, ones, preferred_element_type=F32)

    if one_pass:
        out = one(blk)
    else:
        hi = blk.astype(BF16).astype(F32)
        rest = blk - hi
        mid = rest.astype(BF16).astype(F32)
        out = one(hi) + one(mid) + one(rest - mid)
    return out.reshape(tb, PICKS, LANES)


def _full(ref, k):
    return jnp.broadcast_to(ref[k:k + 1, :], (SUBLANES, LANES))


def _expert_row(tab_ref, tile_ref, t, k, shifts):
    idx = pl.multiple_of(tile_ref[t, k], SUBLANES)
    return _unpack(tab_ref[pl.ds(idx, SUBLANES), :], _full(shifts, k))


def _hidden_kernel(tb, tile_ref, shift_ref, x_ref, gate_ref, tab_ref, *rest):
    gh_ref, bsh, stage, parts, xt = rest[-5:]
    bsh[...] = _splat_picks(shift_ref[...].astype(F32), True).astype(jnp.int32)
    for s in range(SUBLANES):
        xt[pl.ds(s, tb, stride=SUBLANES), :] = x_ref[:, s * LANES:(s + 1) * LANES]

    half = PICKS // 2
    quad = SUBLANES // 2
    low = lax.broadcasted_iota(jnp.int32, (SUBLANES, LANES), 0) < quad

    def token(t, carry):
        x = xt[pl.ds(pl.multiple_of(t * SUBLANES, SUBLANES), SUBLANES), :]
        shifts = bsh.at[t]
        for j in range(half):
            a = x * _expert_row(tab_ref, tile_ref, t, j, shifts)
            b = x * _expert_row(tab_ref, tile_ref, t, j + half, shifts)
            stage[j * SUBLANES:(j + 1) * SUBLANES, :] = jnp.where(low, a + pltpu.roll(a, quad, 0),
                                                                  b + pltpu.roll(b, quad, 0))
        for p in range(2):
            part = stage[pl.ds(p * quad, half, stride=SUBLANES), :]
            for s in range(1, quad):
                part = part + stage[pl.ds(p * quad + s, half, stride=SUBLANES), :]
            parts[t, p * half:(p + 1) * half, :] = part
        return carry

    lax.fori_loop(0, tb, token, 0)
    gh_ref[...] = jnp.sum(parts[...], axis=-1)
    hid = gh_ref[...]
    gelu = 0.5 * hid * (1.0 + lax.erf(hid * (2.0 ** -0.5)))
    gh_ref[...] = gate_ref[...] * gelu


def _pick_block(tb, space=None):
    return pl.BlockSpec((tb, PICKS), lambda i: (i, 0), memory_space=space)


def _table_spec(tab):
    return pl.BlockSpec(tab.shape, lambda i: (0, 0), pipeline_mode=pl.Buffered(1))


def _hidden(tile, shift, xg, gate, tab, tb, after=None):
    t = xg.shape[0]
    extra = [] if after is None else [after]
    return pl.pallas_call(
        functools.partial(_hidden_kernel, tb),
        out_shape=jax.ShapeDtypeStruct((t, PICKS), F32),
        grid=(t // tb,),
        in_specs=[_pick_block(tb, pltpu.SMEM), _pick_block(tb),
                  pl.BlockSpec((tb, D_MODEL), lambda i: (i, 0)),
                  _pick_block(tb), _table_spec(tab)] + [_after_spec(a) for a in extra],
        out_specs=_pick_block(tb),
        scratch_shapes=[pltpu.VMEM((tb, PICKS, LANES), jnp.int32), pltpu.VMEM((PICKS // 2 * SUBLANES, LANES), F32),
                        pltpu.VMEM((tb, PICKS, LANES), F32), pltpu.VMEM((tb * SUBLANES, LANES), F32)],
        compiler_params=pltpu.CompilerParams(dimension_semantics=("parallel",), vmem_limit_bytes=VMEM_LIMIT),
        name="peer_hidden",
    )(tile, shift, xg, gate, tab, *extra)


def _combine_kernel(tb, tile_ref, shift_ref, gh_ref, tab_ref, x_ref, g_ref, *rest):
    o_ref, bsh, bg, ot = rest[-4:]
    bsh[...] = _splat_picks(shift_ref[...].astype(F32), True).astype(jnp.int32)
    bg[...] = _splat_picks(gh_ref[...], False)

    def token(t, carry):
        shifts, gates = bsh.at[t], bg.at[t]
        acc = [jnp.zeros((SUBLANES, LANES), F32) for _ in range(4)]
        for k in range(PICKS):
            acc[k % 4] = acc[k % 4] + _full(gates, k) * _expert_row(tab_ref, tile_ref, t, k, shifts)
        ot[pl.ds(pl.multiple_of(t * SUBLANES, SUBLANES), SUBLANES), :] = (acc[0] + acc[1]) + (acc[2] + acc[3])
        return carry

    lax.fori_loop(0, tb, token, 0)
    for s in range(SUBLANES):
        o_ref[:, s * LANES:(s + 1) * LANES] = ot[pl.ds(s, tb, stride=SUBLANES), :]
    o_ref[...] = _rms(x_ref[...] + o_ref[...], g_ref[...])


def _combine(tile, shift, gh, tab, x, g, tb, x_row0, out_row0, out_rows, into=None):
    t = gh.shape[0]
    xb0, ob0 = x_row0 // tb, out_row0 // tb
    extra = [] if into is None else [into]
    return pl.pallas_call(
        functools.partial(_combine_kernel, tb),
        out_shape=jax.ShapeDtypeStruct((out_rows, D_MODEL), F32),
        grid=(t // tb,),
        in_specs=[_pick_block(tb, pltpu.SMEM), _pick_block(tb), _pick_block(tb), _table_spec(tab),
                  pl.BlockSpec((tb, D_MODEL), lambda i: (i + xb0, 0)), _const_spec(g.shape)]
        + [pl.BlockSpec(memory_space=pl.ANY) for _ in extra],
        input_output_aliases={6: 0} if extra else {},
        out_specs=pl.BlockSpec((tb, D_MODEL), lambda i: (i + ob0, 0)),
        scratch_shapes=[pltpu.VMEM((tb, PICKS, LANES), jnp.int32), pltpu.VMEM((tb, PICKS, LANES), F32),
                        pltpu.VMEM((tb * SUBLANES, LANES), F32)],
        compiler_params=pltpu.CompilerParams(dimension_semantics=("parallel",), vmem_limit_bytes=VMEM_LIMIT),
        name="peer_combine",
    )(tile, shift, gh, tab, x, g, *extra)


SC_LANES = 16
SC_WORKERS = 32
SC_CHUNK = 32
SC_FEATURE_BLOCK = 256
SC_EARLY_SEQS = 3
SC_PROMPT_TOKENS = 13824


def _sc_call(body, n, side_width, out_width):
    mesh = plsc.VectorSubcoreMesh(core_axis_name="c", subcore_axis_name="s")
    return pl.kernel(body, mesh=mesh, out_type=jax.ShapeDtypeStruct((n, out_width), F32),
                     compiler_params=pltpu.CompilerParams(needs_layout_passes=False),
                     scratch_types=[pltpu.VMEM((2 * PICKS,), jnp.int32), pltpu.VMEM((2 * side_width,), F32),
                                    pltpu.VMEM((2 * out_width,), F32),
                                    pltpu.VMEM((SC_CHUNK, D_MODEL), F32), pltpu.VMEM((SC_CHUNK, D_MODEL), F32)]
                     + [pltpu.SemaphoreType.DMA] * 5)


def _sc_token_loop(n, eid_hbm, side_hbm, tab_hbm, out_hbm, idx_v, side_v, out_v, bufs, sems, compute):
    sem_rows, sem_fetch, sem_put = sems[0:2], sems[2], sems[3:5]
    per_worker = n // SC_WORKERS
    first = (lax.axis_index("s") * 2 + lax.axis_index("c")) * per_worker
    last = first + per_worker - 1
    chunks = PICKS // SC_CHUNK
    side_w = side_v.shape[0] // 2
    out_w = out_v.shape[0] // 2

    def fetch(t, slot):
        return (pltpu.make_async_copy(eid_hbm.at[t], idx_v.at[pl.ds(slot * PICKS, PICKS)], sem_fetch),
                pltpu.make_async_copy(side_hbm.at[t], side_v.at[pl.ds(slot * side_w, side_w)], sem_fetch))

    def gather(slot, c):
        ids = idx_v.at[pl.ds(slot * PICKS + c * SC_CHUNK, SC_CHUNK)]
        return pltpu.make_async_copy(tab_hbm.at[ids], bufs[c % 2], sem_rows[c % 2])

    def put(t, slot):
        return pltpu.make_async_copy(out_v.at[pl.ds(slot * out_w, out_w)], out_hbm.at[t], sem_put[slot])

    for cp in fetch(first, 0):
        cp.start()
    for cp in fetch(first, 0):
        cp.wait()
    gather(0, 0).start()
    gather(0, 1).start()

    @pl.loop(0, per_worker, step=2)
    def _(i):
        for slot in range(2):
            t = first + i + slot
            nxt = jnp.minimum(t + 1, last)
            for cp in fetch(nxt, 1 - slot):
                cp.start()

            @pl.when(i >= 2)
            def _():
                put(t, slot).wait()

            for c in range(chunks):
                gather(slot, c).wait()
                compute(c, bufs[c % 2], slot * side_w, slot * out_w)
                if c + 2 < chunks:
                    gather(slot, c + 2).start()
                else:
                    if c + 2 == chunks:
                        for cp in fetch(nxt, 1 - slot):
                            cp.wait()
                    gather(1 - slot, c + 2 - chunks).start()
            put(t, slot).start()

    gather(0, 0).wait()
    gather(0, 1).wait()
    put(last, 0).wait()
    put(last, 1).wait()


def _sc_hidden(eid, x, tab):
    n = eid.shape[0]
    lanes = SC_LANES

    def body(eid_hbm, x_hbm, tab_hbm, out_hbm, idx_v, x_v, hid_v, rows0, rows1, *sems):
        lane = lax.iota(jnp.int32, lanes)

        def compute(c, rows, x0, h0):
            for g in range(SC_CHUNK // lanes):
                def step(j, accs):
                    off = pl.multiple_of(j * lanes, lanes)
                    xj = x_v[pl.ds(x0 + off, lanes)]
                    return tuple(accs[r] + xj * rows[g * lanes + r, pl.ds(off, lanes)] for r in range(lanes))
                accs = lax.fori_loop(0, D_MODEL // lanes, step,
                                     tuple(jnp.zeros((lanes,), F32) for _ in range(lanes)))
                hv = jnp.zeros((lanes,), F32)
                for r in range(lanes):
                    hv = jnp.where(lane == r, jnp.sum(accs[r]), hv)
                hid_v[pl.ds(h0 + c * SC_CHUNK + g * lanes, lanes)] = hv

        _sc_token_loop(n, eid_hbm, x_hbm, tab_hbm, out_hbm, idx_v, x_v, hid_v, (rows0, rows1), sems, compute)

    return _sc_call(body, n, D_MODEL, PICKS)(eid, x, tab)


def _sc_combine(eid, gh, tab):
    n = eid.shape[0]
    lanes = SC_LANES
    nacc = SC_FEATURE_BLOCK // lanes

    def body(eid_hbm, g_hbm, tab_hbm, out_hbm, idx_v, g_v, out_v, rows0, rows1, *sems):
        def compute(c, rows, g0, o0):
            for fb in range(D_MODEL // SC_FEATURE_BLOCK):
                f0 = fb * SC_FEATURE_BLOCK
                if c == 0:
                    init = tuple(jnp.zeros((lanes,), F32) for _ in range(nacc))
                else:
                    init = tuple(out_v[pl.ds(o0 + f0 + q * lanes, lanes)] for q in range(nacc))

                def step(r, accs):
                    gk = plsc.load_gather(g_v, [jnp.full((lanes,), g0 + c * SC_CHUNK + r, jnp.int32)])
                    return tuple(accs[q] + gk * rows[r, pl.ds(f0 + q * lanes, lanes)] for q in range(nacc))
                accs = lax.fori_loop(0, SC_CHUNK, step, init)
                for q in range(nacc):
                    out_v[pl.ds(o0 + f0 + q * lanes, lanes)] = accs[q]

        _sc_token_loop(n, eid_hbm, g_hbm, tab_hbm, out_hbm, idx_v, g_v, out_v, (rows0, rows1), sems, compute)

    return _sc_call(body, n, PICKS, D_MODEL)(eid, gh, tab)


def _after_spec(after):
    return pl.BlockSpec((SUBLANES, after.shape[1]), lambda i: (0, 0))


def _gate_gelu_kernel(h_ref, g_ref, after_ref, o_ref):
    hid = h_ref[...]
    o_ref[...] = g_ref[...] * (0.5 * hid * (1.0 + lax.erf(hid * (2.0 ** -0.5))))


def _gate_gelu(hid, gate, tb, after):
    t = hid.shape[0]
    return pl.pallas_call(
        _gate_gelu_kernel, out_shape=jax.ShapeDtypeStruct((t, PICKS), F32), grid=(t // tb,),
        in_specs=[_pick_block(tb), _pick_block(tb), _after_spec(after)], out_specs=_pick_block(tb),
        compiler_params=pltpu.CompilerParams(dimension_semantics=("parallel",)),
        name="gate_gelu",
    )(hid, gate, after)


def _final_kernel(x_ref, p_ref, g_ref, *rest):
    rest[-1][...] = _rms(x_ref[...] + p_ref[...], g_ref[...])


def _final(x, p, g, tb, out_row0, out_rows, into=None):
    b0 = out_row0 // tb
    local = pl.BlockSpec((tb, D_MODEL), lambda i: (i, 0))
    in_specs = [local, local, _const_spec(g.shape)]
    args = [x, p, g]
    aliases = {}
    if into is not None:
        aliases = {len(args): 0}
        in_specs.append(pl.BlockSpec(memory_space=pl.ANY))
        args.append(into)
    return pl.pallas_call(
        _final_kernel, out_shape=jax.ShapeDtypeStruct((out_rows, D_MODEL), F32), grid=(p.shape[0] // tb,),
        in_specs=in_specs, out_specs=pl.BlockSpec((tb, D_MODEL), lambda i: (i + b0, 0)),
        input_output_aliases=aliases,
        compiler_params=pltpu.CompilerParams(dimension_semantics=("parallel",), vmem_limit_bytes=VMEM_LIMIT),
        name="final_norm",
    )(*args)


PACK_BLOCK = 256


def _pack_kernel(lo_ref, hi_ref, o_ref):
    def bf16_bits(x):
        return lax.bitcast_convert_type(x.astype(BF16).astype(F32), jnp.int32)
    word = bf16_bits(hi_ref[...]) | lax.shift_right_logical(bf16_bits(lo_ref[...]), 16)
    for s in range(SUBLANES):
        o_ref[pl.ds(s, PACK_BLOCK, stride=SUBLANES), :] = word[:, s * LANES:(s + 1) * LANES]


def _pack_table(tab):
    steps = HALF_EXPERTS // PACK_BLOCK
    return pl.pallas_call(
        _pack_kernel, out_shape=jax.ShapeDtypeStruct((HALF_EXPERTS * SUBLANES, LANES), jnp.int32), grid=(steps,),
        in_specs=[pl.BlockSpec((PACK_BLOCK, D_MODEL), lambda i: (i, 0)),
                  pl.BlockSpec((PACK_BLOCK, D_MODEL), lambda i: (i + steps, 0))],
        out_specs=pl.BlockSpec((PACK_BLOCK * SUBLANES, LANES), lambda i: (i, 0)),
        compiler_params=pltpu.CompilerParams(dimension_semantics=("parallel",)),
        name="pack_table",
    )(tab, tab)


def _col(v, n=LANES):
    return jnp.pad(v.astype(F32), (0, n - v.shape[0])).reshape(1, n)


def _row(v, n=2 * SUBLANES):
    return jnp.pad(v.astype(F32), (0, n - v.shape[0])).reshape(n, 1)


def _peer(x, w, tb_sel, tb_exp, n_sc, out_row0=0, out_rows=None, after=None, into=None):
    t = x.shape[0]
    out_rows = t if out_rows is None else out_rows
    n_tc = t - n_sc
    if n_sc:
        xn_s, eid_s, _, _, gate_s = _select(x, w["ln_ffn"], w["w_pq"], w["sub_keys"], tb_sel, 0, n_sc, after=after)
        hid_s = _sc_hidden(eid_s, xn_s, w["u_tab"])
    xn, _, tile, shift, gate = _select(x, w["ln_ffn"], w["w_pq"], w["sub_keys"], tb_sel, n_sc, n_tc,
                                       after=xn_s if n_sc else None)
    gh = _hidden(tile, shift, xn, gate, w["u_pack"], tb_exp, after=into)
    if n_sc:
        out_s = _sc_combine(eid_s, _gate_gelu(hid_s, gate_s, tb_sel, after=gh), w["v_tab"])
    y = _combine(tile, shift, gh, w["v_pack"], x, w["final_norm"], tb_exp, n_sc, out_row0 + n_sc, out_rows, into)
    if n_sc:
        y = _final(x, out_s, w["final_norm"], tb_sel, out_row0, out_rows, into=y)
    return y


def kernel(x_prompt, x_sample, mem_prompt, cache_conv_a, cache_conv_ssd, state_ssd, cache_mem_k, cache_mem_v, ln_mix_w, w_in, conv_a_w, conv_s_w, conv_s_b, dt_bias, a_log, d_skip, ssd_norm_w, w_out, ln_mem_w, mem_norm_w, w_mq, w_mk, w_mv, w_mo, ln_ffn_w, w_pq, sub_keys, u_tab, v_tab, final_norm_w):
    depth = w_in.shape[0]
    assert depth == 1
    l = 0
    n_main = 3 * D_CONV + D_SSD + XBC_DIM
    w_dt = w_in[l][:, n_main:]
    w = {
        "ln_mix": ln_mix_w[l].reshape(1, D_MODEL),
        "w_in": w_in[l][:, :n_main].astype(BF16),
        "w_dt": jnp.pad(w_dt, ((0, 0), (0, LANES - SSD_HEADS))).astype(BF16),
        "w_dtt": jnp.pad(w_dt.T, ((0, 2 * SUBLANES - SSD_HEADS), (0, 0))).astype(BF16),
        "conv_a_w": conv_a_w[l], "conv_s_w": conv_s_w[l], "conv_s_b": conv_s_b[l].reshape(1, XBC_DIM),
        "dtb_c": _col(dt_bias[l]), "alog_c": _col(a_log[l]), "dsk_c": _col(d_skip[l]),
        "dtb_r": _row(dt_bias[l]), "alog_r": _row(a_log[l]),
        "ssd_norm": ssd_norm_w[l].reshape(1, D_SSD),
        "w_out": w_out[l].astype(BF16),
        "ln_ffn": ln_ffn_w[l].reshape(1, D_MODEL),
        "w_pq": w_pq[l].astype(BF16).reshape(D_MODEL, PEER_HEADS, PEER_DK).transpose(1, 0, 2),
        "sub_keys": sub_keys[l].astype(BF16),
        "u_pack": _pack_table(u_tab[l]),
        "v_pack": _pack_table(v_tab[l]),
        "u_tab": u_tab[l], "v_tab": v_tab[l],
        "final_norm": final_norm_w.reshape(1, D_MODEL),
    }
    ln_mem = ln_mem_w[l].reshape(1, D_MODEL)
    wq, wo = w_mq[l].astype(BF16), w_mo[l].astype(BF16)

    bp, sp, _ = x_prompt.shape
    bs, ss, _ = x_sample.shape

    mk, mv = _memory_kv(mem_prompt.reshape(bp * N_MEM, D_MODEL), mem_norm_w[l].reshape(1, D_MODEL),
                        w_mk[l].astype(BF16), w_mv[l].astype(BF16))
    mk3, mv3 = mk.reshape(bp, N_MEM, D_MODEL), mv.reshape(bp, N_MEM, D_MODEL)
    n1 = SC_EARLY_SEQS
    h1, pa1, ps1, ph1 = _mixer(x_prompt, None, w, 256, 0, n1)
    h1 = _attn(h1, mk3, mv3, ln_mem, wq, wo, 256).reshape(n1 * sp, D_MODEL)

    xn1, eid1, _, _, gate1 = _select(h1, w["ln_ffn"], w["w_pq"], w["sub_keys"], 256, 0, n1 * sp)
    hid1 = _sc_hidden(eid1, xn1, w["u_tab"])
    h2, pa2, ps2, ph2 = _mixer(x_prompt, None, w, 256, n1, bp - n1, after=xn1)
    h2 = _attn(h2, mk3, mv3, ln_mem, wq, wo, 256, kv0=n1).reshape((bp - n1) * sp, D_MODEL)

    hs, sa, ssd, sh = _mixer(x_sample, (cache_conv_a[l], cache_conv_ssd[l],
                                        state_ssd[l].reshape((bs,) + PAIR_STATE)), w, 256, after=h2)
    hs = _attn(hs, cache_mem_k[l], cache_mem_v[l], ln_mem, wq, wo, 256).reshape(bs * ss, D_MODEL)
    xn_q, _, tile_q, shift_q, gate_q = _select(hs, w["ln_ffn"], w["w_pq"], w["sub_keys"], 256, 0, bs * ss)

    gh1 = _gate_gelu(hid1, gate1, 256, after=xn_q)
    out1 = _sc_combine(eid1, gh1, w["v_tab"])
    y1 = _final(h1, out1, w["final_norm"], 256, 0, bp * sp)
    y_prompt = _peer(h2, w, 256, 64, SC_PROMPT_TOKENS, n1 * sp, bp * sp, after=gh1, into=y1).reshape(bp, sp, D_MODEL)
    pa, ps, ph = (jnp.concatenate(p, axis=0) for p in ((pa1, pa2), (ps1, ps2), (ph1, ph2)))

    gh_q = _hidden(tile_q, shift_q, xn_q, gate_q, w["u_pack"], 64)
    y_sample = _combine(tile_q, shift_q, gh_q, w["v_pack"], hs, w["final_norm"], 64, 0, 0, bs * ss)
    y_sample = y_sample.reshape(bs, ss, D_MODEL)

    kv_shape = (1, bp, N_MEM, MEM_HEADS, MEM_HEAD_DIM)
    st_shape = (SSD_HEADS, SSD_HEAD_DIM, SSD_STATE)
    return (y_prompt, y_sample, pa[None], ps[None], ph.reshape((1, bp) + st_shape), mk.reshape(kv_shape),
            mv.reshape(kv_shape), sa[None], ssd[None], sh.reshape((1, bs) + st_shape))
```

```python
import functools
import math

import jax
import jax.numpy as jnp
from jax import lax
from jax.experimental import pallas as pl
from jax.experimental.pallas import tpu as pltpu
from jax.experimental.pallas import tpu_sc as plsc

D_MODEL = 1024
CHUNK = 64
D_CONV = 512
CONV_A_W = 3
D_SSD = 512
SSD_HEAD_DIM = 64
SSD_HEADS = 8
SSD_GROUPS = 2
SSD_HPG = 4
SSD_STATE = 128
SSD_CONV_W = 4
XBC_DIM = 1024
N_MEM = 256
MEM_HEADS = 4
MEM_HEAD_DIM = 256
PEER_HEADS = 8
N_KEYS = 128
N_EXPERTS = N_KEYS * N_KEYS
PEER_TOPK = 16
PEER_DK = 256
PEER_DK_HALF = 128
PICKS = PEER_HEADS * PEER_TOPK
EPS = 1e-6

LANES = 128
SUBLANES = 8
HALF_EXPERTS = N_EXPERTS // 2
PAIR_STATE = (SSD_HEADS // 2, 2 * SSD_HEAD_DIM, SSD_STATE)
VMEM_LIMIT = 56 * 1024 * 1024

F32 = jnp.float32
BF16 = jnp.bfloat16
HI = lax.Precision.HIGHEST


def _rms(x, g):
    return x * lax.rsqrt(jnp.mean(x * x, axis=-1, keepdims=True) + EPS) * g


def _silu(x):
    return x * (1.0 / (1.0 + jnp.exp(-x)))


def _softplus(x):
    return jnp.maximum(x, 0.0) + jnp.log1p(jnp.exp(-jnp.abs(x)))


def _dot(a, b):
    return jnp.dot(a.astype(BF16), b.astype(BF16), preferred_element_type=F32)


def _dot_nt(a, b):
    return lax.dot_general(a.astype(BF16), b.astype(BF16), (((1,), (1,)), ((), ())),
                           preferred_element_type=F32)


def _dot_tn(a, b):
    return lax.dot_general(a.astype(BF16), b.astype(BF16), (((0,), (0,)), ((), ())),
                           preferred_element_type=F32)


def _const_spec(shape):
    n = len(shape)
    return pl.BlockSpec(shape, lambda *_: (0,) * n)


def _mixer_kernel(has_hist, has_after, ts, lc, *refs):
    if has_after:
        refs = refs[1:]
    if has_hist:
        x_ref, hista_ref, hists_ref, h0_ref = refs[:4]
        refs = refs[4:]
    else:
        x_ref = refs[0]
        refs = refs[1:]
    (lnw_ref, win_ref, wdt_ref, wdtt_ref, caw_ref, csw_ref, csb_ref, dtb_c_ref, alog_c_ref, dsk_c_ref,
     dtb_r_ref, alog_r_ref, nw_ref, wout_ref,
     h1_ref, na_ref, ns_ref, nh_ref, cata, cats, hst) = refs
    s = pl.program_id(1)
    pad = SUBLANES

    @pl.when(s == 0)
    def _():
        if has_hist:
            cata[pad - 2:pad, :] = hista_ref[...]
            cats[pad - 3:pad, :] = hists_ref[...]
            hst[...] = h0_ref[...]
        else:
            cata[0:pad, :] = jnp.zeros((pad, D_CONV), F32)
            cats[0:pad, :] = jnp.zeros((pad, XBC_DIM), F32)
            hst[...] = jnp.zeros(hst.shape, F32)

    x = x_ref[...]
    xn = _rms(x, lnw_ref[...]).astype(BF16)
    proj = jnp.dot(xn, win_ref[...], preferred_element_type=F32)
    dt_c = jnp.dot(xn, wdt_ref[...], preferred_element_type=F32)
    dt_r = lax.dot_general(wdtt_ref[...], xn, (((1,), (1,)), ((), ())),
                           preferred_element_type=F32)
    g_b = proj[:, 0:D_CONV]
    g_c = proj[:, D_CONV:2 * D_CONV]
    v_in = proj[:, 2 * D_CONV:3 * D_CONV]
    z = proj[:, 3 * D_CONV:3 * D_CONV + D_SSD]
    xbc = proj[:, 3 * D_CONV + D_SSD:3 * D_CONV + D_SSD + XBC_DIM]

    ua = g_c * v_in
    cata[pad:pad + ts, :] = ua
    caw = caw_ref[...]
    conv_a = (caw[0:1] * cata[pad - 2:pad - 2 + ts, :] + caw[1:2] * cata[pad - 1:pad - 1 + ts, :]
              + caw[2:3] * ua)
    y_a = g_b * conv_a
    hist_a = cata[pad + ts - 2:pad + ts, :]
    na_ref[...] = hist_a
    cata[pad - 2:pad, :] = hist_a

    cats[pad:pad + ts, :] = xbc
    csw = csw_ref[...]
    conv_s = (csw[0:1] * cats[pad - 3:pad - 3 + ts, :] + csw[1:2] * cats[pad - 2:pad - 2 + ts, :]
              + csw[2:3] * cats[pad - 1:pad - 1 + ts, :] + csw[3:4] * xbc)
    hist_s = cats[pad + ts - 3:pad + ts, :]
    ns_ref[...] = hist_s
    cats[pad - 3:pad, :] = hist_s
    xbc_c = _silu(conv_s + csb_ref[...])
    xs = xbc_c[:, 0:D_SSD]

    dtp_c = _softplus(dt_c + dtb_c_ref[...])
    dtp_r = _softplus(dt_r + dtb_r_ref[...])
    a_c = dtp_c * (-jnp.exp(alog_c_ref[...]))
    a_r = dtp_r * (-jnp.exp(alog_r_ref[...]))
    dsk_c = dsk_c_ref[...]

    lane = lax.broadcasted_iota(jnp.int32, (1, LANES), 1)
    first_head = lane < SSD_HEAD_DIM
    row2 = lax.broadcasted_iota(jnp.int32, (2 * SSD_HEAD_DIM, 1), 0) < SSD_HEAD_DIM
    li = lax.broadcasted_iota(jnp.int32, (lc, lc), 0)
    si = lax.broadcasted_iota(jnp.int32, (lc, lc), 1)
    causal = si <= li
    tril = causal.astype(F32)
    jrow = lax.broadcasted_iota(jnp.int32, (ts, lc), 0)
    scol = lax.broadcasted_iota(jnp.int32, (ts, lc), 1)

    def pair(col, h):
        return jnp.where(first_head, col[:, h:h + 1], col[:, h + 1:h + 2])

    y_chunks = []
    for c in range(ts // lc):
        r0 = c * lc
        rows = slice(r0, r0 + lc)
        acum_c = jnp.dot(tril, a_c[rows], preferred_element_type=F32, precision=HI)
        upper = ((jrow >= r0) & (jrow <= r0 + scol)).astype(F32)
        acum_r = jnp.dot(a_r, upper, preferred_element_type=F32, precision=HI)
        last_c = acum_c[lc - 1:lc, :]
        dec_c = jnp.exp(last_c - acum_c)
        eac_c = jnp.exp(acum_c)
        y_pairs = []
        for g in range(SSD_GROUPS):
            b_g = xbc_c[rows, D_SSD + g * SSD_STATE:D_SSD + (g + 1) * SSD_STATE]
            c_g = xbc_c[rows, D_SSD + (SSD_GROUPS + g) * SSD_STATE:D_SSD + (SSD_GROUPS + g + 1) * SSD_STATE]
            cb = _dot_nt(c_g, b_g)
            for q in range(SSD_HPG // 2):
                h = g * SSD_HPG + 2 * q
                pi = h // 2
                xs_p = xs[rows, pi * LANES:(pi + 1) * LANES]
                xdt = xs_p * pair(dtp_c[rows], h)
                res = []
                for hh in (h, h + 1):
                    seg = acum_c[:, hh:hh + 1] - acum_r[hh:hh + 1, :]
                    m_h = cb * jnp.exp(jnp.where(causal, seg, -jnp.inf))
                    res.append(_dot(m_h, xdt))
                y_diag = jnp.where(first_head, res[0], res[1])
                h_in = hst[pi]
                y_off = _dot_nt(c_g, h_in) * pair(eac_c, h)
                st = _dot_tn(xdt * pair(dec_c, h), b_g)
                cd = jnp.exp(jnp.where(row2, last_c[:, h:h + 1], last_c[:, h + 1:h + 2]))
                hst[pi] = cd * h_in + st
                y_pairs.append(y_diag + y_off + pair(dsk_c, h) * xs_p)
        y_chunks.append(jnp.concatenate(y_pairs, axis=1))
    y = y_chunks[0] if len(y_chunks) == 1 else jnp.concatenate(y_chunks, axis=0)
    y_b = _rms(y * _silu(z), nw_ref[...])
    wout = wout_ref[...]
    out = (jnp.dot(y_a.astype(BF16), wout[0:D_CONV], preferred_element_type=F32)
           + jnp.dot(y_b.astype(BF16), wout[D_CONV:], preferred_element_type=F32))
    h1_ref[...] = x + out
    nh_ref[...] = hst[...]


def _mixer(x, hist, w, ts, b0=0, nb=None, after=None):
    _, s, _ = x.shape
    b = x.shape[0] if nb is None else nb
    lc = min(CHUNK, s)
    ts = min(ts, s)
    has_hist = hist is not None
    assert not (has_hist and b0)
    per_b3 = lambda shape: pl.BlockSpec((None,) + shape, lambda i, j: (i,) + (0,) * len(shape))
    in_specs = [pl.BlockSpec((None, ts, D_MODEL), lambda i, j: (i + b0, j, 0))]
    args = [x]
    if after is not None:
        in_specs.insert(0, pl.BlockSpec((SUBLANES, after.shape[1]), lambda i, j: (0, 0)))
        args.insert(0, after)
    if has_hist:
        in_specs += [per_b3((CONV_A_W - 1, D_CONV)), per_b3((SSD_CONV_W - 1, XBC_DIM)),
                     per_b3(PAIR_STATE)]
        args += list(hist)
    wargs = [w["ln_mix"], w["w_in"], w["w_dt"], w["w_dtt"], w["conv_a_w"], w["conv_s_w"], w["conv_s_b"],
             w["dtb_c"], w["alog_c"], w["dsk_c"], w["dtb_r"], w["alog_r"], w["ssd_norm"], w["w_out"]]
    in_specs += [_const_spec(a.shape) for a in wargs]
    args += wargs
    out_shape = (jax.ShapeDtypeStruct((b, s, D_MODEL), F32),
                 jax.ShapeDtypeStruct((b, CONV_A_W - 1, D_CONV), F32),
                 jax.ShapeDtypeStruct((b, SSD_CONV_W - 1, XBC_DIM), F32),
                 jax.ShapeDtypeStruct((b,) + PAIR_STATE, F32))
    out_specs = (pl.BlockSpec((None, ts, D_MODEL), lambda i, j: (i, j, 0)),
                 per_b3((CONV_A_W - 1, D_CONV)), per_b3((SSD_CONV_W - 1, XBC_DIM)),
                 per_b3(PAIR_STATE))
    return pl.pallas_call(
        functools.partial(_mixer_kernel, has_hist, after is not None, ts, lc),
        out_shape=out_shape, grid=(b, s // ts), in_specs=in_specs, out_specs=out_specs,
        scratch_shapes=[pltpu.VMEM((SUBLANES + ts, D_CONV), F32), pltpu.VMEM((SUBLANES + ts, XBC_DIM), F32),
                        pltpu.VMEM(PAIR_STATE, F32)],
        compiler_params=pltpu.CompilerParams(dimension_semantics=("parallel", "arbitrary"),
                                             vmem_limit_bytes=VMEM_LIMIT),
        name="mixer_hist" if has_hist else "mixer",
    )(*args)


def _memkv_kernel(m_ref, g_ref, wk_ref, wv_ref, k_ref, v_ref):
    mn = _rms(m_ref[...], g_ref[...]).astype(BF16)
    k_ref[...] = jnp.dot(mn, wk_ref[...], preferred_element_type=F32)
    v_ref[...] = jnp.dot(mn, wv_ref[...], preferred_element_type=F32)


def _memory_kv(mem, g, wk, wv, tm=256):
    t = mem.shape[0]
    blk = pl.BlockSpec((tm, D_MODEL), lambda i: (i, 0))
    return pl.pallas_call(
        _memkv_kernel,
        out_shape=(jax.ShapeDtypeStruct((t, D_MODEL), F32),) * 2,
        grid=(t // tm,), in_specs=[blk, _const_spec(g.shape), _const_spec(wk.shape), _const_spec(wv.shape)],
        out_specs=(blk, blk),
        compiler_params=pltpu.CompilerParams(dimension_semantics=("parallel",), vmem_limit_bytes=VMEM_LIMIT),
        name="memory_kv",
    )(mem, g, wk, wv)


def _attn_kernel(x_ref, k_ref, v_ref, g_ref, wq_ref, wo_ref, o_ref):
    x = x_ref[...]
    xn = _rms(x, g_ref[...]).astype(BF16)
    q = jnp.dot(xn, wq_ref[...], preferred_element_type=F32)
    by_head = len(k_ref.shape) == 3
    wo = wo_ref[...]
    acc = x
    for h in range(MEM_HEADS):
        cols = slice(h * MEM_HEAD_DIM, (h + 1) * MEM_HEAD_DIM)
        k_h = k_ref[:, h, :] if by_head else k_ref[:, cols]
        v_h = v_ref[:, h, :] if by_head else v_ref[:, cols]
        sc = _dot_nt(q[:, cols], k_h) * (MEM_HEAD_DIM ** -0.5)
        sc = sc - jnp.max(sc, axis=-1, keepdims=True)
        p = jnp.exp(sc)
        p = p / jnp.sum(p, axis=-1, keepdims=True)
        o_h = _dot(p, v_h)
        acc = acc + jnp.dot(o_h.astype(BF16), wo[cols, :], preferred_element_type=F32)
    o_ref[...] = acc


def _attn(x, k, v, g, wq, wo, ts, kv0=0):
    b, s, _ = x.shape
    ts = min(ts, s)
    xblk = pl.BlockSpec((None, ts, D_MODEL), lambda i, j: (i, j, 0))
    if k.ndim == 3:
        kvblk = pl.BlockSpec((None, N_MEM, D_MODEL), lambda i, j: (i + kv0, 0, 0))
    else:
        kvblk = pl.BlockSpec((None, N_MEM, MEM_HEADS, MEM_HEAD_DIM), lambda i, j: (i + kv0, 0, 0, 0))
    return pl.pallas_call(
        _attn_kernel,
        out_shape=jax.ShapeDtypeStruct((b, s, D_MODEL), F32),
        grid=(b, s // ts),
        in_specs=[xblk, kvblk, kvblk, _const_spec(g.shape), _const_spec(wq.shape), _const_spec(wo.shape)],
        out_specs=xblk,
        compiler_params=pltpu.CompilerParams(dimension_semantics=("parallel", "parallel"),
                                             vmem_limit_bytes=VMEM_LIMIT),
        name="attn",
    )(x, k, v, g, wq, wo)


def _top_k_rows(sc, k, payload=None):
    n = sc.shape[0]
    rows = lax.broadcasted_iota(jnp.int32, sc.shape, 0).astype(F32)
    vals, outs = [], []
    for _ in range(k):
        m = jnp.max(sc, axis=0, keepdims=True)
        cand = jnp.where(sc == m, rows, float(n))
        first = jnp.min(cand, axis=0, keepdims=True)
        hit = cand == first
        sc = jnp.where(hit, -jnp.inf, sc)
        vals.append(m)
        if payload is None:
            outs.append(first.astype(jnp.int32))
        else:
            outs.append(jnp.sum(jnp.where(hit, payload, 0), axis=0, keepdims=True))
    return jnp.concatenate(vals, axis=0), jnp.concatenate(outs, axis=0)


_LIVE_B = tuple(PEER_TOPK // (a + 1) for a in range(1, SUBLANES))


def _pair_candidates(v1, i1, v2, i2):
    sub = lax.broadcasted_iota(jnp.int32, (SUBLANES, v1.shape[1]), 0)
    comb = [v1[0:1] + v2]
    cidx = [i1[0:1] * N_KEYS + i2]
    for a in range(1, SUBLANES):
        comb.append(jnp.where(sub < _LIVE_B[a - 1], v1[a:a + 1] + v2[0:SUBLANES], -jnp.inf))
        cidx.append(i1[a:a + 1] * N_KEYS + i2[0:SUBLANES])
    comb.append(v1[SUBLANES:] + v2[0:1])
    cidx.append(i1[SUBLANES:] * N_KEYS + i2[0:1])
    return jnp.concatenate(comb, axis=0), jnp.concatenate(cidx, axis=0)


def _select_kernel(tb, x_ref, g_ref, wq_ref, keys_ref, *rest):
    xn_ref, eid_ref, tile_ref, shift_ref, gate_ref, eid_s, gate_s = rest[-7:]
    xn = _rms(x_ref[...], g_ref[...])
    xn_ref[...] = xn
    xn = xn.astype(BF16)

    def head(h, carry):
        q = jnp.dot(xn, wq_ref[h], preferred_element_type=F32)
        sc1 = _dot_nt(keys_ref[h, 0], q[:, 0:PEER_DK_HALF])
        sc2 = _dot_nt(keys_ref[h, 1], q[:, PEER_DK_HALF:PEER_DK])
        rows = pl.ds(pl.multiple_of(h * PEER_TOPK, PEER_TOPK), PEER_TOPK)
        for c0 in range(0, tb, LANES):
            cols = slice(c0, c0 + LANES)
            v1, i1 = _top_k_rows(sc1[:, cols], PEER_TOPK)
            v2, i2 = _top_k_rows(sc2[:, cols], PEER_TOPK)
            comb, cidx = _pair_candidates(v1, i1, v2, i2)
            top, eid = _top_k_rows(comb, PEER_TOPK, cidx)
            ex = jnp.exp(top - top[0:1])
            gate_s[rows, cols] = ex / jnp.sum(ex, axis=0, keepdims=True)
            eid_s[rows, cols] = eid
        return carry

    lax.fori_loop(0, PEER_HEADS, head, 0)
    for c0 in range(0, tb, LANES):
        toks = slice(c0, c0 + LANES)
        eid = eid_s[:, toks].T
        upper = eid >= HALF_EXPERTS
        eid_ref[toks, :] = eid
        tile_ref[toks, :] = jnp.where(upper, eid - HALF_EXPERTS, eid) * SUBLANES
        shift_ref[toks, :] = jnp.where(upper, 0, 16)
        gate_ref[toks, :] = gate_s[:, toks].T


def _select(x, g, wq, keys, tb, t0, t, after=None):
    b0 = t0 // tb
    extra = [] if after is None else [after]
    xin = pl.BlockSpec((tb, D_MODEL), lambda i: (i + b0, 0))
    xblk = pl.BlockSpec((tb, D_MODEL), lambda i: (i, 0))
    picks_i = jax.ShapeDtypeStruct((t, PICKS), jnp.int32)
    return pl.pallas_call(
        functools.partial(_select_kernel, tb),
        out_shape=(jax.ShapeDtypeStruct((t, D_MODEL), F32), picks_i, picks_i, picks_i,
                   jax.ShapeDtypeStruct((t, PICKS), F32)),
        grid=(t // tb,),
        in_specs=[xin, _const_spec(g.shape), _const_spec(wq.shape), _const_spec(keys.shape)]
        + [_after_spec(a) for a in extra],
        out_specs=(xblk,) + (_pick_block(tb),) * 4,
        scratch_shapes=[pltpu.VMEM((PICKS, tb), jnp.int32), pltpu.VMEM((PICKS, tb), F32)],
        compiler_params=pltpu.CompilerParams(dimension_semantics=("parallel",), vmem_limit_bytes=VMEM_LIMIT),
        name="peer_select",
    )(x, g, wq, keys, *extra)


HI_HALF_MASK = -(1 << 16)
_BITREV3 = (0, 4, 2, 6, 1, 5, 3, 7)


def _unpack(word, shift):
    return lax.bitcast_convert_type((word << shift) & HI_HALF_MASK, F32)


def _splat_picks(blk, one_pass):
    tb = blk.shape[0]
    r = lax.broadcasted_iota(jnp.int32, (PICKS, LANES), 0)
    c = lax.broadcasted_iota(jnp.int32, (PICKS, LANES), 1)
    eye = (r == c)[None]
    ones = jnp.ones((LANES, LANES), BF16)

    def one(piece):
        diag = jnp.where(eye, piece[:, None, :], 0.0).astype(BF16).reshape(tb * PICKS, LANES)
        return jnp.dot(diag, ones, preferred_element_type=F32)

    if one_pass:
        out = one(blk)
    else:
        hi = blk.astype(BF16).astype(F32)
        rest = blk - hi
        mid = rest.astype(BF16).astype(F32)
        out = one(hi) + one(mid) + one(rest - mid)
    return out.reshape(tb, PICKS, LANES)


def _full(ref, k):
    return jnp.broadcast_to(ref[k:k + 1, :], (SUBLANES, LANES))


def _expert_row(tab_ref, tile_ref, t, k, shifts):
    idx = pl.multiple_of(tile_ref[t, k], SUBLANES)
    return _unpack(tab_ref[pl.ds(idx, SUBLANES), :], _full(shifts, k))


def _hidden_kernel(tb, tile_ref, shift_ref, x_ref, gate_ref, tab_ref, *rest):
    gh_ref, bsh, stage, parts, xt = rest[-5:]
    bsh[...] = _splat_picks(shift_ref[...].astype(F32), True).astype(jnp.int32)
    for s in range(SUBLANES):
        xt[pl.ds(s, tb, stride=SUBLANES), :] = x_ref[:, s * LANES:(s + 1) * LANES]

    half = PICKS // 2
    quad = SUBLANES // 2
    low = lax.broadcasted_iota(jnp.int32, (SUBLANES, LANES), 0) < quad

    def token(t, carry):
        x = xt[pl.ds(pl.multiple_of(t * SUBLANES, SUBLANES), SUBLANES), :]
        shifts = bsh.at[t]
        for j in range(half):
            a = x * _expert_row(tab_ref, tile_ref, t, j, shifts)
            b = x * _expert_row(tab_ref, tile_ref, t, j + half, shifts)
            stage[j * SUBLANES:(j + 1) * SUBLANES, :] = jnp.where(low, a + pltpu.roll(a, quad, 0),
                                                                  b + pltpu.roll(b, quad, 0))
        for p in range(2):
            part = stage[pl.ds(p * quad, half, stride=SUBLANES), :]
            for s in range(1, quad):
                part = part + stage[pl.ds(p * quad + s, half, stride=SUBLANES), :]
            parts[t, p * half:(p + 1) * half, :] = part
        return carry

    lax.fori_loop(0, tb, token, 0)
    gh_ref[...] = jnp.sum(parts[...], axis=-1)
    hid = gh_ref[...]
    gelu = 0.5 * hid * (1.0 + lax.erf(hid * (2.0 ** -0.5)))
    gh_ref[...] = gate_ref[...] * gelu


def _pick_block(tb, space=None):
    return pl.BlockSpec((tb, PICKS), lambda i: (i, 0), memory_space=space)


def _table_spec(tab):
    return pl.BlockSpec(tab.shape, lambda i: (0, 0), pipeline_mode=pl.Buffered(1))


def _hidden(tile, shift, xg, gate, tab, tb, after=None):
    t = xg.shape[0]
    extra = [] if after is None else [after]
    return pl.pallas_call(
        functools.partial(_hidden_kernel, tb),
        out_shape=jax.ShapeDtypeStruct((t, PICKS), F32),
        grid=(t // tb,),
        in_specs=[_pick_block(tb, pltpu.SMEM), _pick_block(tb),
                  pl.BlockSpec((tb, D_MODEL), lambda i: (i, 0)),
                  _pick_block(tb), _table_spec(tab)] + [_after_spec(a) for a in extra],
        out_specs=_pick_block(tb),
        scratch_shapes=[pltpu.VMEM((tb, PICKS, LANES), jnp.int32), pltpu.VMEM((PICKS // 2 * SUBLANES, LANES), F32),
                        pltpu.VMEM((tb, PICKS, LANES), F32), pltpu.VMEM((tb * SUBLANES, LANES), F32)],
        compiler_params=pltpu.CompilerParams(dimension_semantics=("parallel",), vmem_limit_bytes=VMEM_LIMIT),
        name="peer_hidden",
    )(tile, shift, xg, gate, tab, *extra)


def _combine_kernel(tb, tile_ref, shift_ref, gh_ref, tab_ref, x_ref, g_ref, *rest):
    o_ref, bsh, bg, ot = rest[-4:]
    bsh[...] = _splat_picks(shift_ref[...].astype(F32), True).astype(jnp.int32)
    bg[...] = _splat_picks(gh_ref[...], False)

    def token(t, carry):
        shifts, gates = bsh.at[t], bg.at[t]
        acc = [jnp.zeros((SUBLANES, LANES), F32) for _ in range(4)]
        for k in range(PICKS):
            acc[k % 4] = acc[k % 4] + _full(gates, k) * _expert_row(tab_ref, tile_ref, t, k, shifts)
        ot[pl.ds(pl.multiple_of(t * SUBLANES, SUBLANES), SUBLANES), :] = (acc[0] + acc[1]) + (acc[2] + acc[3])
        return carry

    lax.fori_loop(0, tb, token, 0)
    for s in range(SUBLANES):
        o_ref[:, s * LANES:(s + 1) * LANES] = ot[pl.ds(s, tb, stride=SUBLANES), :]
    o_ref[...] = _rms(x_ref[...] + o_ref[...], g_ref[...])


def _combine(tile, shift, gh, tab, x, g, tb, x_row0, out_row0, out_rows, into=None):
    t = gh.shape[0]
    xb0, ob0 = x_row0 // tb, out_row0 // tb
    extra = [] if into is None else [into]
    return pl.pallas_call(
        functools.partial(_combine_kernel, tb),
        out_shape=jax.ShapeDtypeStruct((out_rows, D_MODEL), F32),
        grid=(t // tb,),
        in_specs=[_pick_block(tb, pltpu.SMEM), _pick_block(tb), _pick_block(tb), _table_spec(tab),
                  pl.BlockSpec((tb, D_MODEL), lambda i: (i + xb0, 0)), _const_spec(g.shape)]
        + [pl.BlockSpec(memory_space=pl.ANY) for _ in extra],
        input_output_aliases={6: 0} if extra else {},
        out_specs=pl.BlockSpec((tb, D_MODEL), lambda i: (i + ob0, 0)),
        scratch_shapes=[pltpu.VMEM((tb, PICKS, LANES), jnp.int32), pltpu.VMEM((tb, PICKS, LANES), F32),
                        pltpu.VMEM((tb * SUBLANES, LANES), F32)],
        compiler_params=pltpu.CompilerParams(dimension_semantics=("parallel",), vmem_limit_bytes=VMEM_LIMIT),
        name="peer_combine",
    )(tile, shift, gh, tab, x, g, *extra)


SC_LANES = 16
SC_WORKERS = 32
SC_CHUNK = 32
SC_FEATURE_BLOCK = 256
SC_EARLY_GROUPS = (1, 2)
SC_PROMPT_TOKENS = 14336


def _sc_call(body, n, side_width, out_width):
    mesh = plsc.VectorSubcoreMesh(core_axis_name="c", subcore_axis_name="s")
    return pl.kernel(body, mesh=mesh, out_type=jax.ShapeDtypeStruct((n, out_width), F32),
                     compiler_params=pltpu.CompilerParams(needs_layout_passes=False),
                     scratch_types=[pltpu.VMEM((2 * PICKS,), jnp.int32), pltpu.VMEM((2 * side_width,), F32),
                                    pltpu.VMEM((2 * out_width,), F32),
                                    pltpu.VMEM((SC_CHUNK, D_MODEL), F32), pltpu.VMEM((SC_CHUNK, D_MODEL), F32)]
                     + [pltpu.SemaphoreType.DMA] * 5)


def _sc_token_loop(n, eid_hbm, side_hbm, tab_hbm, out_hbm, idx_v, side_v, out_v, bufs, sems, compute):
    sem_rows, sem_fetch, sem_put = sems[0:2], sems[2], sems[3:5]
    per_worker = n // SC_WORKERS
    first = (lax.axis_index("s") * 2 + lax.axis_index("c")) * per_worker
    last = first + per_worker - 1
    chunks = PICKS // SC_CHUNK
    side_w = side_v.shape[0] // 2
    out_w = out_v.shape[0] // 2

    def fetch(t, slot):
        return (pltpu.make_async_copy(eid_hbm.at[t], idx_v.at[pl.ds(slot * PICKS, PICKS)], sem_fetch),
                pltpu.make_async_copy(side_hbm.at[t], side_v.at[pl.ds(slot * side_w, side_w)], sem_fetch))

    def gather(slot, c):
        ids = idx_v.at[pl.ds(slot * PICKS + c * SC_CHUNK, SC_CHUNK)]
        return pltpu.make_async_copy(tab_hbm.at[ids], bufs[c % 2], sem_rows[c % 2])

    def put(t, slot):
        return pltpu.make_async_copy(out_v.at[pl.ds(slot * out_w, out_w)], out_hbm.at[t], sem_put[slot])

    for cp in fetch(first, 0):
        cp.start()
    for cp in fetch(first, 0):
        cp.wait()
    gather(0, 0).start()
    gather(0, 1).start()

    @pl.loop(0, per_worker, step=2)
    def _(i):
        for slot in range(2):
            t = first + i + slot
            nxt = jnp.minimum(t + 1, last)
            for cp in fetch(nxt, 1 - slot):
                cp.start()

            @pl.when(i >= 2)
            def _():
                put(t, slot).wait()

            for c in range(chunks):
                gather(slot, c).wait()
                compute(c, bufs[c % 2], slot * side_w, slot * out_w)
                if c + 2 < chunks:
                    gather(slot, c + 2).start()
                else:
                    if c + 2 == chunks:
                        for cp in fetch(nxt, 1 - slot):
                            cp.wait()
                    gather(1 - slot, c + 2 - chunks).start()
            put(t, slot).start()

    gather(0, 0).wait()
    gather(0, 1).wait()
    put(last, 0).wait()
    put(last, 1).wait()


def _sc_hidden(eid, x, tab):
    n = eid.shape[0]
    lanes = SC_LANES

    def body(eid_hbm, x_hbm, tab_hbm, out_hbm, idx_v, x_v, hid_v, rows0, rows1, *sems):
        lane = lax.iota(jnp.int32, lanes)

        def compute(c, rows, x0, h0):
            for g in range(SC_CHUNK // lanes):
                def step(j, accs):
                    off = pl.multiple_of(j * lanes, lanes)
                    xj = x_v[pl.ds(x0 + off, lanes)]
                    return tuple(accs[r] + xj * rows[g * lanes + r, pl.ds(off, lanes)] for r in range(lanes))
                accs = lax.fori_loop(0, D_MODEL // lanes, step,
                                     tuple(jnp.zeros((lanes,), F32) for _ in range(lanes)))
                hv = jnp.zeros((lanes,), F32)
                for r in range(lanes):
                    hv = jnp.where(lane == r, jnp.sum(accs[r]), hv)
                hid_v[pl.ds(h0 + c * SC_CHUNK + g * lanes, lanes)] = hv

        _sc_token_loop(n, eid_hbm, x_hbm, tab_hbm, out_hbm, idx_v, x_v, hid_v, (rows0, rows1), sems, compute)

    return _sc_call(body, n, D_MODEL, PICKS)(eid, x, tab)


def _sc_combine(eid, gh, tab):
    n = eid.shape[0]
    lanes = SC_LANES
    nacc = SC_FEATURE_BLOCK // lanes

    def body(eid_hbm, g_hbm, tab_hbm, out_hbm, idx_v, g_v, out_v, rows0, rows1, *sems):
        def compute(c, rows, g0, o0):
            for fb in range(D_MODEL // SC_FEATURE_BLOCK):
                f0 = fb * SC_FEATURE_BLOCK
                if c == 0:
                    init = tuple(jnp.zeros((lanes,), F32) for _ in range(nacc))
                else:
                    init = tuple(out_v[pl.ds(o0 + f0 + q * lanes, lanes)] for q in range(nacc))

                def step(r, accs):
                    gk = plsc.load_gather(g_v, [jnp.full((lanes,), g0 + c * SC_CHUNK + r, jnp.int32)])
                    return tuple(accs[q] + gk * rows[r, pl.ds(f0 + q * lanes, lanes)] for q in range(nacc))
                accs = lax.fori_loop(0, SC_CHUNK, step, init)
                for q in range(nacc):
                    out_v[pl.ds(o0 + f0 + q * lanes, lanes)] = accs[q]

        _sc_token_loop(n, eid_hbm, g_hbm, tab_hbm, out_hbm, idx_v, g_v, out_v, (rows0, rows1), sems, compute)

    return _sc_call(body, n, PICKS, D_MODEL)(eid, gh, tab)


def _after_spec(after):
    return pl.BlockSpec((SUBLANES, after.shape[1]), lambda i: (0, 0))


def _gate_gelu_kernel(h_ref, g_ref, after_ref, o_ref):
    hid = h_ref[...]
    o_ref[...] = g_ref[...] * (0.5 * hid * (1.0 + lax.erf(hid * (2.0 ** -0.5))))


def _gate_gelu(hid, gate, tb, after):
    t = hid.shape[0]
    return pl.pallas_call(
        _gate_gelu_kernel, out_shape=jax.ShapeDtypeStruct((t, PICKS), F32), grid=(t // tb,),
        in_specs=[_pick_block(tb), _pick_block(tb), _after_spec(after)], out_specs=_pick_block(tb),
        compiler_params=pltpu.CompilerParams(dimension_semantics=("parallel",)),
        name="gate_gelu",
    )(hid, gate, after)


def _final_kernel(x_ref, p_ref, g_ref, *rest):
    rest[-1][...] = _rms(x_ref[...] + p_ref[...], g_ref[...])


def _final(x, p, g, tb, out_row0, out_rows, into=None):
    b0 = out_row0 // tb
    local = pl.BlockSpec((tb, D_MODEL), lambda i: (i, 0))
    in_specs = [local, local, _const_spec(g.shape)]
    args = [x, p, g]
    aliases = {}
    if into is not None:
        aliases = {len(args): 0}
        in_specs.append(pl.BlockSpec(memory_space=pl.ANY))
        args.append(into)
    return pl.pallas_call(
        _final_kernel, out_shape=jax.ShapeDtypeStruct((out_rows, D_MODEL), F32), grid=(p.shape[0] // tb,),
        in_specs=in_specs, out_specs=pl.BlockSpec((tb, D_MODEL), lambda i: (i + b0, 0)),
        input_output_aliases=aliases,
        compiler_params=pltpu.CompilerParams(dimension_semantics=("parallel",), vmem_limit_bytes=VMEM_LIMIT),
        name="final_norm",
    )(*args)


PACK_BLOCK = 256


def _pack_kernel(lo_ref, hi_ref, o_ref):
    def bf16_bits(x):
        return lax.bitcast_convert_type(x.astype(BF16).astype(F32), jnp.int32)
    word = bf16_bits(hi_ref[...]) | lax.shift_right_logical(bf16_bits(lo_ref[...]), 16)
    for s in range(SUBLANES):
        o_ref[pl.ds(s, PACK_BLOCK, stride=SUBLANES), :] = word[:, s * LANES:(s + 1) * LANES]


def _pack_table(tab):
    steps = HALF_EXPERTS // PACK_BLOCK
    return pl.pallas_call(
        _pack_kernel, out_shape=jax.ShapeDtypeStruct((HALF_EXPERTS * SUBLANES, LANES), jnp.int32), grid=(steps,),
        in_specs=[pl.BlockSpec((PACK_BLOCK, D_MODEL), lambda i: (i, 0)),
                  pl.BlockSpec((PACK_BLOCK, D_MODEL), lambda i: (i + steps, 0))],
        out_specs=pl.BlockSpec((PACK_BLOCK * SUBLANES, LANES), lambda i: (i, 0)),
        compiler_params=pltpu.CompilerParams(dimension_semantics=("parallel",)),
        name="pack_table",
    )(tab, tab)


def _col(v, n=LANES):
    return jnp.pad(v.astype(F32), (0, n - v.shape[0])).reshape(1, n)


def _row(v, n=2 * SUBLANES):
    return jnp.pad(v.astype(F32), (0, n - v.shape[0])).reshape(n, 1)


def _peer(x, w, tb_sel, tb_exp, n_sc, out_row0=0, out_rows=None, after=None, into=None):
    t = x.shape[0]
    out_rows = t if out_rows is None else out_rows
    n_tc = t - n_sc
    if n_sc:
        xn_s, eid_s, _, _, gate_s = _select(x, w["ln_ffn"], w["w_pq"], w["sub_keys"], tb_sel, 0, n_sc, after=after)
        hid_s = _sc_hidden(eid_s, xn_s, w["u_tab"])
    xn, _, tile, shift, gate = _select(x, w["ln_ffn"], w["w_pq"], w["sub_keys"], tb_sel, n_sc, n_tc,
                                       after=xn_s if n_sc else None)
    gh = _hidden(tile, shift, xn, gate, w["u_pack"], tb_exp, after=into)
    if n_sc:
        out_s = _sc_combine(eid_s, _gate_gelu(hid_s, gate_s, tb_sel, after=gh), w["v_tab"])
    y = _combine(tile, shift, gh, w["v_pack"], x, w["final_norm"], tb_exp, n_sc, out_row0 + n_sc, out_rows, into)
    if n_sc:
        y = _final(x, out_s, w["final_norm"], tb_sel, out_row0, out_rows, into=y)
    return y


def kernel(x_prompt, x_sample, mem_prompt, cache_conv_a, cache_conv_ssd, state_ssd, cache_mem_k, cache_mem_v, ln_mix_w, w_in, conv_a_w, conv_s_w, conv_s_b, dt_bias, a_log, d_skip, ssd_norm_w, w_out, ln_mem_w, mem_norm_w, w_mq, w_mk, w_mv, w_mo, ln_ffn_w, w_pq, sub_keys, u_tab, v_tab, final_norm_w):
    depth = w_in.shape[0]
    assert depth == 1
    l = 0
    n_main = 3 * D_CONV + D_SSD + XBC_DIM
    w_dt = w_in[l][:, n_main:]
    w = {
        "ln_mix": ln_mix_w[l].reshape(1, D_MODEL),
        "w_in": w_in[l][:, :n_main].astype(BF16),
        "w_dt": jnp.pad(w_dt, ((0, 0), (0, LANES - SSD_HEADS))).astype(BF16),
        "w_dtt": jnp.pad(w_dt.T, ((0, 2 * SUBLANES - SSD_HEADS), (0, 0))).astype(BF16),
        "conv_a_w": conv_a_w[l], "conv_s_w": conv_s_w[l], "conv_s_b": conv_s_b[l].reshape(1, XBC_DIM),
        "dtb_c": _col(dt_bias[l]), "alog_c": _col(a_log[l]), "dsk_c": _col(d_skip[l]),
        "dtb_r": _row(dt_bias[l]), "alog_r": _row(a_log[l]),
        "ssd_norm": ssd_norm_w[l].reshape(1, D_SSD),
        "w_out": w_out[l].astype(BF16),
        "ln_ffn": ln_ffn_w[l].reshape(1, D_MODEL),
        "w_pq": w_pq[l].astype(BF16).reshape(D_MODEL, PEER_HEADS, PEER_DK).transpose(1, 0, 2),
        "sub_keys": sub_keys[l].astype(BF16),
        "u_pack": _pack_table(u_tab[l]),
        "v_pack": _pack_table(v_tab[l]),
        "u_tab": u_tab[l], "v_tab": v_tab[l],
        "final_norm": final_norm_w.reshape(1, D_MODEL),
    }
    ln_mem = ln_mem_w[l].reshape(1, D_MODEL)
    wq, wo = w_mq[l].astype(BF16), w_mo[l].astype(BF16)

    bp, sp, _ = x_prompt.shape
    bs, ss, _ = x_sample.shape

    mk, mv = _memory_kv(mem_prompt.reshape(bp * N_MEM, D_MODEL), mem_norm_w[l].reshape(1, D_MODEL),
                        w_mk[l].astype(BF16), w_mv[l].astype(BF16))
    mk3, mv3 = mk.reshape(bp, N_MEM, D_MODEL), mv.reshape(bp, N_MEM, D_MODEL)
    early, states, b0, xn_prev = [], [], 0, None
    for nseq in SC_EARLY_GROUPS:
        hg, *st = _mixer(x_prompt, None, w, 256, b0, nseq, after=xn_prev)
        hg = _attn(hg, mk3, mv3, ln_mem, wq, wo, 256, kv0=b0).reshape(nseq * sp, D_MODEL)
        xn_g, eid_g, _, _, gate_g = _select(hg, w["ln_ffn"], w["w_pq"], w["sub_keys"], 256, 0, nseq * sp)
        early.append((b0 * sp, hg, xn_g, eid_g, gate_g, _sc_hidden(eid_g, xn_g, w["u_tab"])))
        states.append(st)
        b0, xn_prev = b0 + nseq, xn_g
    n1 = b0
    h2, *st = _mixer(x_prompt, None, w, 256, n1, bp - n1, after=xn_prev)
    states.append(st)
    h2 = _attn(h2, mk3, mv3, ln_mem, wq, wo, 256, kv0=n1).reshape((bp - n1) * sp, D_MODEL)

    hs, sa, ssd, sh = _mixer(x_sample, (cache_conv_a[l], cache_conv_ssd[l],
                                        state_ssd[l].reshape((bs,) + PAIR_STATE)), w, 256, after=h2)
    hs = _attn(hs, cache_mem_k[l], cache_mem_v[l], ln_mem, wq, wo, 256).reshape(bs * ss, D_MODEL)
    xn_q, _, tile_q, shift_q, gate_q = _select(hs, w["ln_ffn"], w["w_pq"], w["sub_keys"], 256, 0, bs * ss)

    y1, gh_g = None, None
    for g, (row0, hg, _, eid_g, gate_g, hid_g) in enumerate(early):
        waits_for = early[g + 1][2] if g + 1 < len(early) else xn_q
        gh_g = _gate_gelu(hid_g, gate_g, 256, after=waits_for)
        y1 = _final(hg, _sc_combine(eid_g, gh_g, w["v_tab"]), w["final_norm"], 256, row0, bp * sp, into=y1)
    y_prompt = _peer(h2, w, 256, 64, SC_PROMPT_TOKENS, n1 * sp, bp * sp, after=gh_g, into=y1).reshape(bp, sp, D_MODEL)
    pa, ps, ph = (jnp.concatenate(p, axis=0) for p in zip(*states))

    gh_q = _hidden(tile_q, shift_q, xn_q, gate_q, w["u_pack"], 64)
    y_sample = _combine(tile_q, shift_q, gh_q, w["v_pack"], hs, w["final_norm"], 64, 0, 0, bs * ss)
    y_sample = y_sample.reshape(bs, ss, D_MODEL)

    kv_shape = (1, bp, N_MEM, MEM_HEADS, MEM_HEAD_DIM)
    st_shape = (SSD_HEADS, SSD_HEAD_DIM, SSD_STATE)
    return (y_prompt, y_sample, pa[None], ps[None], ph.reshape((1, bp) + st_shape), mk.reshape(kv_shape),
            mv.reshape(kv_shape), sa[None], ssd[None], sh.reshape((1, bs) + st_shape))
```

```python
import functools

import jax
import jax.numpy as jnp
from jax import lax
from jax.experimental import pallas as pl
from jax.experimental.pallas import tpu as pltpu
from jax.experimental.pallas import tpu_sc as plsc

D_MODEL = 1024
CHUNK = 64
D_CONV = 512
CONV_A_W = 3
D_SSD = 512
SSD_HEAD_DIM = 64
SSD_HEADS = 8
SSD_GROUPS = 2
SSD_HPG = 4
SSD_STATE = 128
SSD_CONV_W = 4
XBC_DIM = 1024
N_MEM = 256
MEM_HEADS = 4
MEM_HEAD_DIM = 256
PEER_HEADS = 8
N_KEYS = 128
N_EXPERTS = N_KEYS * N_KEYS
PEER_TOPK = 16
PEER_DK = 256
PEER_DK_HALF = 128
PICKS = PEER_HEADS * PEER_TOPK
EPS = 1e-6

LANES = 128
SUBLANES = 8
HALF_EXPERTS = N_EXPERTS // 2
PAIR_STATE = (SSD_HEADS // 2, 2 * SSD_HEAD_DIM, SSD_STATE)
VMEM_LIMIT = 56 * 1024 * 1024

F32 = jnp.float32
BF16 = jnp.bfloat16
HI = lax.Precision.HIGHEST


def _rms(x, g):
    return x * lax.rsqrt(jnp.mean(x * x, axis=-1, keepdims=True) + EPS) * g


def _silu(x):
    return x * (1.0 / (1.0 + jnp.exp(-x)))


def _softplus(x):
    return jnp.maximum(x, 0.0) + jnp.log1p(jnp.exp(-jnp.abs(x)))


def _dot(a, b):
    return jnp.dot(a.astype(BF16), b.astype(BF16), preferred_element_type=F32)


def _dot_nt(a, b):
    return lax.dot_general(a.astype(BF16), b.astype(BF16), (((1,), (1,)), ((), ())),
                           preferred_element_type=F32)


def _dot_tn(a, b):
    return lax.dot_general(a.astype(BF16), b.astype(BF16), (((0,), (0,)), ((), ())),
                           preferred_element_type=F32)


def _const_spec(shape):
    n = len(shape)
    return pl.BlockSpec(shape, lambda *_: (0,) * n)


def _mixer_kernel(has_hist, has_after, ts, lc, *refs):
    if has_after:
        refs = refs[1:]
    if has_hist:
        x_ref, hista_ref, hists_ref, h0_ref = refs[:4]
        refs = refs[4:]
    else:
        x_ref = refs[0]
        refs = refs[1:]
    (lnw_ref, win_ref, wdt_ref, wdtt_ref, caw_ref, csw_ref, csb_ref, dtb_c_ref, alog_c_ref, dsk_c_ref,
     dtb_r_ref, alog_r_ref, nw_ref, wout_ref,
     h1_ref, na_ref, ns_ref, nh_ref, cata, cats, hst) = refs
    s = pl.program_id(1)
    pad = SUBLANES

    @pl.when(s == 0)
    def _():
        if has_hist:
            cata[pad - 2:pad, :] = hista_ref[...]
            cats[pad - 3:pad, :] = hists_ref[...]
            hst[...] = h0_ref[...]
        else:
            cata[0:pad, :] = jnp.zeros((pad, D_CONV), F32)
            cats[0:pad, :] = jnp.zeros((pad, XBC_DIM), F32)
            hst[...] = jnp.zeros(hst.shape, F32)

    x = x_ref[...]
    xn = _rms(x, lnw_ref[...]).astype(BF16)
    proj = jnp.dot(xn, win_ref[...], preferred_element_type=F32)
    dt_c = jnp.dot(xn, wdt_ref[...], preferred_element_type=F32)
    dt_r = lax.dot_general(wdtt_ref[...], xn, (((1,), (1,)), ((), ())),
                           preferred_element_type=F32)
    g_b = proj[:, 0:D_CONV]
    g_c = proj[:, D_CONV:2 * D_CONV]
    v_in = proj[:, 2 * D_CONV:3 * D_CONV]
    z = proj[:, 3 * D_CONV:3 * D_CONV + D_SSD]
    xbc = proj[:, 3 * D_CONV + D_SSD:3 * D_CONV + D_SSD + XBC_DIM]

    ua = g_c * v_in
    cata[pad:pad + ts, :] = ua
    caw = caw_ref[...]
    conv_a = (caw[0:1] * cata[pad - 2:pad - 2 + ts, :] + caw[1:2] * cata[pad - 1:pad - 1 + ts, :]
              + caw[2:3] * ua)
    y_a = g_b * conv_a
    hist_a = cata[pad + ts - 2:pad + ts, :]
    na_ref[...] = hist_a
    cata[pad - 2:pad, :] = hist_a

    cats[pad:pad + ts, :] = xbc
    csw = csw_ref[...]
    conv_s = (csw[0:1] * cats[pad - 3:pad - 3 + ts, :] + csw[1:2] * cats[pad - 2:pad - 2 + ts, :]
              + csw[2:3] * cats[pad - 1:pad - 1 + ts, :] + csw[3:4] * xbc)
    hist_s = cats[pad + ts - 3:pad + ts, :]
    ns_ref[...] = hist_s
    cats[pad - 3:pad, :] = hist_s
    xbc_c = _silu(conv_s + csb_ref[...])
    xs = xbc_c[:, 0:D_SSD]

    dtp_c = _softplus(dt_c + dtb_c_ref[...])
    dtp_r = _softplus(dt_r + dtb_r_ref[...])
    a_c = dtp_c * (-jnp.exp(alog_c_ref[...]))
    a_r = dtp_r * (-jnp.exp(alog_r_ref[...]))
    dsk_c = dsk_c_ref[...]

    lane = lax.broadcasted_iota(jnp.int32, (1, LANES), 1)
    first_head = lane < SSD_HEAD_DIM
    row2 = lax.broadcasted_iota(jnp.int32, (2 * SSD_HEAD_DIM, 1), 0) < SSD_HEAD_DIM
    li = lax.broadcasted_iota(jnp.int32, (lc, lc), 0)
    si = lax.broadcasted_iota(jnp.int32, (lc, lc), 1)
    causal = si <= li
    tril = causal.astype(F32)
    jrow = lax.broadcasted_iota(jnp.int32, (ts, lc), 0)
    scol = lax.broadcasted_iota(jnp.int32, (ts, lc), 1)

    def pair(col, h):
        return jnp.where(first_head, col[:, h:h + 1], col[:, h + 1:h + 2])

    y_chunks = []
    for c in range(ts // lc):
        r0 = c * lc
        rows = slice(r0, r0 + lc)
        acum_c = jnp.dot(tril, a_c[rows], preferred_element_type=F32, precision=HI)
        upper = ((jrow >= r0) & (jrow <= r0 + scol)).astype(F32)
        acum_r = jnp.dot(a_r, upper, preferred_element_type=F32, precision=HI)
        last_c = acum_c[lc - 1:lc, :]
        dec_c = jnp.exp(last_c - acum_c)
        eac_c = jnp.exp(acum_c)
        y_pairs = []
        for g in range(SSD_GROUPS):
            b_g = xbc_c[rows, D_SSD + g * SSD_STATE:D_SSD + (g + 1) * SSD_STATE]
            c_g = xbc_c[rows, D_SSD + (SSD_GROUPS + g) * SSD_STATE:D_SSD + (SSD_GROUPS + g + 1) * SSD_STATE]
            cb = _dot_nt(c_g, b_g)
            for q in range(SSD_HPG // 2):
                h = g * SSD_HPG + 2 * q
                pi = h // 2
                xs_p = xs[rows, pi * LANES:(pi + 1) * LANES]
                xdt = xs_p * pair(dtp_c[rows], h)
                res = []
                for hh in (h, h + 1):
                    seg = acum_c[:, hh:hh + 1] - acum_r[hh:hh + 1, :]
                    m_h = cb * jnp.exp(jnp.where(causal, seg, -jnp.inf))
                    res.append(_dot(m_h, xdt))
                y_diag = jnp.where(first_head, res[0], res[1])
                h_in = hst[pi]
                y_off = _dot_nt(c_g, h_in) * pair(eac_c, h)
                st = _dot_tn(xdt * pair(dec_c, h), b_g)
                cd = jnp.exp(jnp.where(row2, last_c[:, h:h + 1], last_c[:, h + 1:h + 2]))
                hst[pi] = cd * h_in + st
                y_pairs.append(y_diag + y_off + pair(dsk_c, h) * xs_p)
        y_chunks.append(jnp.concatenate(y_pairs, axis=1))
    y = y_chunks[0] if len(y_chunks) == 1 else jnp.concatenate(y_chunks, axis=0)
    y_b = _rms(y * _silu(z), nw_ref[...])
    wout = wout_ref[...]
    out = (jnp.dot(y_a.astype(BF16), wout[0:D_CONV], preferred_element_type=F32)
           + jnp.dot(y_b.astype(BF16), wout[D_CONV:], preferred_element_type=F32))
    h1_ref[...] = x + out
    nh_ref[...] = hst[...]


def _mixer(x, hist, w, ts, b0=0, nb=None, after=None):
    _, s, _ = x.shape
    b = x.shape[0] if nb is None else nb
    lc = min(CHUNK, s)
    ts = min(ts, s)
    has_hist = hist is not None
    assert not (has_hist and b0)
    per_b3 = lambda shape: pl.BlockSpec((None,) + shape, lambda i, j: (i,) + (0,) * len(shape))
    in_specs = [pl.BlockSpec((None, ts, D_MODEL), lambda i, j: (i + b0, j, 0))]
    args = [x]
    if after is not None:
        in_specs.insert(0, pl.BlockSpec((SUBLANES, after.shape[1]), lambda i, j: (0, 0)))
        args.insert(0, after)
    if has_hist:
        in_specs += [per_b3((CONV_A_W - 1, D_CONV)), per_b3((SSD_CONV_W - 1, XBC_DIM)),
                     per_b3(PAIR_STATE)]
        args += list(hist)
    wargs = [w["ln_mix"], w["w_in"], w["w_dt"], w["w_dtt"], w["conv_a_w"], w["conv_s_w"], w["conv_s_b"],
             w["dtb_c"], w["alog_c"], w["dsk_c"], w["dtb_r"], w["alog_r"], w["ssd_norm"], w["w_out"]]
    in_specs += [_const_spec(a.shape) for a in wargs]
    args += wargs
    out_shape = (jax.ShapeDtypeStruct((b, s, D_MODEL), F32),
                 jax.ShapeDtypeStruct((b, CONV_A_W - 1, D_CONV), F32),
                 jax.ShapeDtypeStruct((b, SSD_CONV_W - 1, XBC_DIM), F32),
                 jax.ShapeDtypeStruct((b,) + PAIR_STATE, F32))
    out_specs = (pl.BlockSpec((None, ts, D_MODEL), lambda i, j: (i, j, 0)),
                 per_b3((CONV_A_W - 1, D_CONV)), per_b3((SSD_CONV_W - 1, XBC_DIM)),
                 per_b3(PAIR_STATE))
    return pl.pallas_call(
        functools.partial(_mixer_kernel, has_hist, after is not None, ts, lc),
        out_shape=out_shape, grid=(b, s // ts), in_specs=in_specs, out_specs=out_specs,
        scratch_shapes=[pltpu.VMEM((SUBLANES + ts, D_CONV), F32), pltpu.VMEM((SUBLANES + ts, XBC_DIM), F32),
                        pltpu.VMEM(PAIR_STATE, F32)],
        compiler_params=pltpu.CompilerParams(dimension_semantics=("parallel", "arbitrary"),
                                             vmem_limit_bytes=VMEM_LIMIT),
        name="mixer_hist" if has_hist else "mixer",
    )(*args)


def _memkv_kernel(m_ref, g_ref, wk_ref, wv_ref, k_ref, v_ref):
    mn = _rms(m_ref[...], g_ref[...]).astype(BF16)
    k_ref[...] = jnp.dot(mn, wk_ref[...], preferred_element_type=F32)
    v_ref[...] = jnp.dot(mn, wv_ref[...], preferred_element_type=F32)


def _memory_kv(mem, g, wk, wv, tm=256):
    t = mem.shape[0]
    blk = pl.BlockSpec((tm, D_MODEL), lambda i: (i, 0))
    return pl.pallas_call(
        _memkv_kernel,
        out_shape=(jax.ShapeDtypeStruct((t, D_MODEL), F32),) * 2,
        grid=(t // tm,), in_specs=[blk, _const_spec(g.shape), _const_spec(wk.shape), _const_spec(wv.shape)],
        out_specs=(blk, blk),
        compiler_params=pltpu.CompilerParams(dimension_semantics=("parallel",), vmem_limit_bytes=VMEM_LIMIT),
        name="memory_kv",
    )(mem, g, wk, wv)


def _attn_kernel(x_ref, k_ref, v_ref, g_ref, wq_ref, wo_ref, o_ref):
    x = x_ref[...]
    xn = _rms(x, g_ref[...]).astype(BF16)
    q = jnp.dot(xn, wq_ref[...], preferred_element_type=F32)
    by_head = len(k_ref.shape) == 3
    wo = wo_ref[...]
    acc = x
    for h in range(MEM_HEADS):
        cols = slice(h * MEM_HEAD_DIM, (h + 1) * MEM_HEAD_DIM)
        k_h = k_ref[:, h, :] if by_head else k_ref[:, cols]
        v_h = v_ref[:, h, :] if by_head else v_ref[:, cols]
        sc = _dot_nt(q[:, cols], k_h) * (MEM_HEAD_DIM ** -0.5)
        sc = sc - jnp.max(sc, axis=-1, keepdims=True)
        p = jnp.exp(sc)
        p = p / jnp.sum(p, axis=-1, keepdims=True)
        o_h = _dot(p, v_h)
        acc = acc + jnp.dot(o_h.astype(BF16), wo[cols, :], preferred_element_type=F32)
    o_ref[...] = acc


def _attn(x, k, v, g, wq, wo, ts, kv0=0):
    b, s, _ = x.shape
    ts = min(ts, s)
    xblk = pl.BlockSpec((None, ts, D_MODEL), lambda i, j: (i, j, 0))
    if k.ndim == 3:
        kvblk = pl.BlockSpec((None, N_MEM, D_MODEL), lambda i, j: (i + kv0, 0, 0))
    else:
        kvblk = pl.BlockSpec((None, N_MEM, MEM_HEADS, MEM_HEAD_DIM), lambda i, j: (i + kv0, 0, 0, 0))
    return pl.pallas_call(
        _attn_kernel,
        out_shape=jax.ShapeDtypeStruct((b, s, D_MODEL), F32),
        grid=(b, s // ts),
        in_specs=[xblk, kvblk, kvblk, _const_spec(g.shape), _const_spec(wq.shape), _const_spec(wo.shape)],
        out_specs=xblk,
        compiler_params=pltpu.CompilerParams(dimension_semantics=("parallel", "parallel"),
                                             vmem_limit_bytes=VMEM_LIMIT),
        name="attn",
    )(x, k, v, g, wq, wo)


def _top_k_rows(sc, k, payload=None):
    n = sc.shape[0]
    rows = lax.broadcasted_iota(jnp.int32, sc.shape, 0).astype(F32)
    vals, outs = [], []
    for _ in range(k):
        m = jnp.max(sc, axis=0, keepdims=True)
        cand = jnp.where(sc == m, rows, float(n))
        first = jnp.min(cand, axis=0, keepdims=True)
        hit = cand == first
        sc = jnp.where(hit, -jnp.inf, sc)
        vals.append(m)
        if payload is None:
            outs.append(first.astype(jnp.int32))
        else:
            outs.append(jnp.sum(jnp.where(hit, payload, 0), axis=0, keepdims=True))
    return jnp.concatenate(vals, axis=0), jnp.concatenate(outs, axis=0)


_LIVE_B = tuple(PEER_TOPK // (a + 1) for a in range(1, SUBLANES))


def _pair_candidates(v1, i1, v2, i2):
    sub = lax.broadcasted_iota(jnp.int32, (SUBLANES, v1.shape[1]), 0)
    comb = [v1[0:1] + v2]
    cidx = [i1[0:1] * N_KEYS + i2]
    for a in range(1, SUBLANES):
        comb.append(jnp.where(sub < _LIVE_B[a - 1], v1[a:a + 1] + v2[0:SUBLANES], -jnp.inf))
        cidx.append(i1[a:a + 1] * N_KEYS + i2[0:SUBLANES])
    comb.append(v1[SUBLANES:] + v2[0:1])
    cidx.append(i1[SUBLANES:] * N_KEYS + i2[0:1])
    return jnp.concatenate(comb, axis=0), jnp.concatenate(cidx, axis=0)


def _select_kernel(tb, x_ref, g_ref, wq_ref, keys_ref, *rest):
    xn_ref, eid_ref, tile_ref, shift_ref, gate_ref, eid_s, gate_s = rest[-7:]
    xn = _rms(x_ref[...], g_ref[...])
    xn_ref[...] = xn
    xn = xn.astype(BF16)

    def head(h, carry):
        q = jnp.dot(xn, wq_ref[h], preferred_element_type=F32)
        sc1 = _dot_nt(keys_ref[h, 0], q[:, 0:PEER_DK_HALF])
        sc2 = _dot_nt(keys_ref[h, 1], q[:, PEER_DK_HALF:PEER_DK])
        rows = pl.ds(pl.multiple_of(h * PEER_TOPK, PEER_TOPK), PEER_TOPK)
        for c0 in range(0, tb, LANES):
            cols = slice(c0, c0 + LANES)
            v1, i1 = _top_k_rows(sc1[:, cols], PEER_TOPK)
            v2, i2 = _top_k_rows(sc2[:, cols], PEER_TOPK)
            comb, cidx = _pair_candidates(v1, i1, v2, i2)
            top, eid = _top_k_rows(comb, PEER_TOPK, cidx)
            ex = jnp.exp(top - top[0:1])
            gate_s[rows, cols] = ex / jnp.sum(ex, axis=0, keepdims=True)
            eid_s[rows, cols] = eid
        return carry

    lax.fori_loop(0, PEER_HEADS, head, 0)
    for c0 in range(0, tb, LANES):
        toks = slice(c0, c0 + LANES)
        eid = eid_s[:, toks].T
        upper = eid >= HALF_EXPERTS
        eid_ref[toks, :] = eid
        tile_ref[toks, :] = jnp.where(upper, eid - HALF_EXPERTS, eid) * SUBLANES
        shift_ref[toks, :] = jnp.where(upper, 0, 16)
        gate_ref[toks, :] = gate_s[:, toks].T


def _select(x, g, wq, keys, tb, t0, t, after=None):
    b0 = t0 // tb
    extra = [] if after is None else [after]
    xin = pl.BlockSpec((tb, D_MODEL), lambda i: (i + b0, 0))
    xblk = pl.BlockSpec((tb, D_MODEL), lambda i: (i, 0))
    picks_i = jax.ShapeDtypeStruct((t, PICKS), jnp.int32)
    return pl.pallas_call(
        functools.partial(_select_kernel, tb),
        out_shape=(jax.ShapeDtypeStruct((t, D_MODEL), F32), picks_i, picks_i, picks_i,
                   jax.ShapeDtypeStruct((t, PICKS), F32)),
        grid=(t // tb,),
        in_specs=[xin, _const_spec(g.shape), _const_spec(wq.shape), _const_spec(keys.shape)]
        + [_after_spec(a) for a in extra],
        out_specs=(xblk,) + (_pick_block(tb),) * 4,
        scratch_shapes=[pltpu.VMEM((PICKS, tb), jnp.int32), pltpu.VMEM((PICKS, tb), F32)],
        compiler_params=pltpu.CompilerParams(dimension_semantics=("parallel",), vmem_limit_bytes=VMEM_LIMIT),
        name="peer_select",
    )(x, g, wq, keys, *extra)


HI_HALF_MASK = -(1 << 16)


def _unpack(word, shift):
    return lax.bitcast_convert_type((word << shift) & HI_HALF_MASK, F32)


def _splat_picks(blk, one_pass):
    tb = blk.shape[0]
    r = lax.broadcasted_iota(jnp.int32, (PICKS, LANES), 0)
    c = lax.broadcasted_iota(jnp.int32, (PICKS, LANES), 1)
    eye = (r == c)[None]
    ones = jnp.ones((LANES, LANES), BF16)

    def one(piece):
        diag = jnp.where(eye, piece[:, None, :], 0.0).astype(BF16).reshape(tb * PICKS, LANES)
        return jnp.dot(diag, ones, preferred_element_type=F32)

    if one_pass:
        out = one(blk)
    else:
        hi = blk.astype(BF16).astype(F32)
        rest = blk - hi
        mid = rest.astype(BF16).astype(F32)
        out = one(hi) + one(mid) + one(rest - mid)
    return out.reshape(tb, PICKS, LANES)


def _full(ref, k):
    return jnp.broadcast_to(ref[k:k + 1, :], (SUBLANES, LANES))


def _expert_row(tab_ref, tile_ref, t, k, shifts):
    idx = pl.multiple_of(tile_ref[t, k], SUBLANES)
    return _unpack(tab_ref[pl.ds(idx, SUBLANES), :], _full(shifts, k))


def _hidden_kernel(tb, tile_ref, shift_ref, x_ref, gate_ref, tab_ref, *rest):
    gh_ref, bsh, stage, parts, xt = rest[-5:]
    bsh[...] = _splat_picks(shift_ref[...].astype(F32), True).astype(jnp.int32)
    for s in range(SUBLANES):
        xt[pl.ds(s, tb, stride=SUBLANES), :] = x_ref[:, s * LANES:(s + 1) * LANES]

    half = PICKS // 2
    quad = SUBLANES // 2
    low = lax.broadcasted_iota(jnp.int32, (SUBLANES, LANES), 0) < quad

    def token(t, carry):
        x = xt[pl.ds(pl.multiple_of(t * SUBLANES, SUBLANES), SUBLANES), :]
        shifts = bsh.at[t]
        for j in range(half):
            a = x * _expert_row(tab_ref, tile_ref, t, j, shifts)
            b = x * _expert_row(tab_ref, tile_ref, t, j + half, shifts)
            stage[j * SUBLANES:(j + 1) * SUBLANES, :] = jnp.where(low, a + pltpu.roll(a, quad, 0),
                                                                  b + pltpu.roll(b, quad, 0))
        for p in range(2):
            part = stage[pl.ds(p * quad, half, stride=SUBLANES), :]
            for s in range(1, quad):
                part = part + stage[pl.ds(p * quad + s, half, stride=SUBLANES), :]
            parts[t, p * half:(p + 1) * half, :] = part
        return carry

    lax.fori_loop(0, tb, token, 0)
    gh_ref[...] = jnp.sum(parts[...], axis=-1)
    hid = gh_ref[...]
    gelu = 0.5 * hid * (1.0 + lax.erf(hid * (2.0 ** -0.5)))
    gh_ref[...] = gate_ref[...] * gelu


def _pick_block(tb, space=None):
    return pl.BlockSpec((tb, PICKS), lambda i: (i, 0), memory_space=space)


def _table_spec(tab):
    return pl.BlockSpec(tab.shape, lambda i: (0, 0), pipeline_mode=pl.Buffered(1))


def _hidden(tile, shift, xg, gate, tab, tb, after=None):
    t = xg.shape[0]
    extra = [] if after is None else [after]
    return pl.pallas_call(
        functools.partial(_hidden_kernel, tb),
        out_shape=jax.ShapeDtypeStruct((t, PICKS), F32),
        grid=(t // tb,),
        in_specs=[_pick_block(tb, pltpu.SMEM), _pick_block(tb),
                  pl.BlockSpec((tb, D_MODEL), lambda i: (i, 0)),
                  _pick_block(tb), _table_spec(tab)] + [_after_spec(a) for a in extra],
        out_specs=_pick_block(tb),
        scratch_shapes=[pltpu.VMEM((tb, PICKS, LANES), jnp.int32), pltpu.VMEM((PICKS // 2 * SUBLANES, LANES), F32),
                        pltpu.VMEM((tb, PICKS, LANES), F32), pltpu.VMEM((tb * SUBLANES, LANES), F32)],
        compiler_params=pltpu.CompilerParams(dimension_semantics=("parallel",), vmem_limit_bytes=VMEM_LIMIT),
        name="peer_hidden",
    )(tile, shift, xg, gate, tab, *extra)


def _combine_kernel(tb, tile_ref, shift_ref, gh_ref, tab_ref, x_ref, g_ref, *rest):
    o_ref, bsh, bg, ot = rest[-4:]
    bsh[...] = _splat_picks(shift_ref[...].astype(F32), True).astype(jnp.int32)
    bg[...] = _splat_picks(gh_ref[...], False)

    def token(t, carry):
        shifts, gates = bsh.at[t], bg.at[t]
        acc = [jnp.zeros((SUBLANES, LANES), F32) for _ in range(4)]
        for k in range(PICKS):
            acc[k % 4] = acc[k % 4] + _full(gates, k) * _expert_row(tab_ref, tile_ref, t, k, shifts)
        ot[pl.ds(pl.multiple_of(t * SUBLANES, SUBLANES), SUBLANES), :] = (acc[0] + acc[1]) + (acc[2] + acc[3])
        return carry

    lax.fori_loop(0, tb, token, 0)
    for s in range(SUBLANES):
        o_ref[:, s * LANES:(s + 1) * LANES] = ot[pl.ds(s, tb, stride=SUBLANES), :]
    o_ref[...] = _rms(x_ref[...] + o_ref[...], g_ref[...])


def _combine(tile, shift, gh, tab, x, g, tb, x_row0, out_row0, out_rows, into=None):
    t = gh.shape[0]
    xb0, ob0 = x_row0 // tb, out_row0 // tb
    extra = [] if into is None else [into]
    return pl.pallas_call(
        functools.partial(_combine_kernel, tb),
        out_shape=jax.ShapeDtypeStruct((out_rows, D_MODEL), F32),
        grid=(t // tb,),
        in_specs=[_pick_block(tb, pltpu.SMEM), _pick_block(tb), _pick_block(tb), _table_spec(tab),
                  pl.BlockSpec((tb, D_MODEL), lambda i: (i + xb0, 0)), _const_spec(g.shape)]
        + [pl.BlockSpec(memory_space=pl.ANY) for _ in extra],
        input_output_aliases={6: 0} if extra else {},
        out_specs=pl.BlockSpec((tb, D_MODEL), lambda i: (i + ob0, 0)),
        scratch_shapes=[pltpu.VMEM((tb, PICKS, LANES), jnp.int32), pltpu.VMEM((tb, PICKS, LANES), F32),
                        pltpu.VMEM((tb * SUBLANES, LANES), F32)],
        compiler_params=pltpu.CompilerParams(dimension_semantics=("parallel",), vmem_limit_bytes=VMEM_LIMIT),
        name="peer_combine",
    )(tile, shift, gh, tab, x, g, *extra)


SC_LANES = 16
SC_WORKERS = 32
SC_CHUNK = 32
SC_FEATURE_BLOCK = 256
SC_EARLY_SEQS = 3
SC_PROMPT_TOKENS = 13824


def _sc_call(body, n, side_width, out_width):
    mesh = plsc.VectorSubcoreMesh(core_axis_name="c", subcore_axis_name="s")
    return pl.kernel(body, mesh=mesh, out_type=jax.ShapeDtypeStruct((n, out_width), F32),
                     compiler_params=pltpu.CompilerParams(needs_layout_passes=False),
                     scratch_types=[pltpu.VMEM((2 * PICKS,), jnp.int32), pltpu.VMEM((2 * side_width,), F32),
                                    pltpu.VMEM((2 * out_width,), F32),
                                    pltpu.VMEM((SC_CHUNK, D_MODEL), F32), pltpu.VMEM((SC_CHUNK, D_MODEL), F32)]
                     + [pltpu.SemaphoreType.DMA] * 5)


def _sc_token_loop(n, eid_hbm, side_hbm, tab_hbm, out_hbm, idx_v, side_v, out_v, bufs, sems, compute):
    sem_rows, sem_fetch, sem_put = sems[0:2], sems[2], sems[3:5]
    per_worker = n // SC_WORKERS
    first = (lax.axis_index("s") * 2 + lax.axis_index("c")) * per_worker
    last = first + per_worker - 1
    chunks = PICKS // SC_CHUNK
    side_w = side_v.shape[0] // 2
    out_w = out_v.shape[0] // 2

    def fetch(t, slot):
        return (pltpu.make_async_copy(eid_hbm.at[t], idx_v.at[pl.ds(slot * PICKS, PICKS)], sem_fetch),
                pltpu.make_async_copy(side_hbm.at[t], side_v.at[pl.ds(slot * side_w, side_w)], sem_fetch))

    def gather(slot, c):
        ids = idx_v.at[pl.ds(slot * PICKS + c * SC_CHUNK, SC_CHUNK)]
        return pltpu.make_async_copy(tab_hbm.at[ids], bufs[c % 2], sem_rows[c % 2])

    def put(t, slot):
        return pltpu.make_async_copy(out_v.at[pl.ds(slot * out_w, out_w)], out_hbm.at[t], sem_put[slot])

    for cp in fetch(first, 0):
        cp.start()
    for cp in fetch(first, 0):
        cp.wait()
    gather(0, 0).start()
    gather(0, 1).start()

    @pl.loop(0, per_worker, step=2)
    def _(i):
        for slot in range(2):
            t = first + i + slot
            nxt = jnp.minimum(t + 1, last)
            for cp in fetch(nxt, 1 - slot):
                cp.start()

            @pl.when(i >= 2)
            def _():
                put(t, slot).wait()

            for c in range(chunks):
                gather(slot, c).wait()
                compute(c, bufs[c % 2], slot * side_w, slot * out_w)
                if c + 2 < chunks:
                    gather(slot, c + 2).start()
                else:
                    if c + 2 == chunks:
                        for cp in fetch(nxt, 1 - slot):
                            cp.wait()
                    gather(1 - slot, c + 2 - chunks).start()
            put(t, slot).start()

    gather(0, 0).wait()
    gather(0, 1).wait()
    put(last, 0).wait()
    put(last, 1).wait()


def _sc_hidden(eid, x, tab):
    n = eid.shape[0]
    lanes = SC_LANES

    def body(eid_hbm, x_hbm, tab_hbm, out_hbm, idx_v, x_v, hid_v, rows0, rows1, *sems):
        lane = lax.iota(jnp.int32, lanes)

        def compute(c, rows, x0, h0):
            for g in range(SC_CHUNK // lanes):
                def step(j, accs):
                    off = pl.multiple_of(j * lanes, lanes)
                    xj = x_v[pl.ds(x0 + off, lanes)]
                    return tuple(accs[r] + xj * rows[g * lanes + r, pl.ds(off, lanes)] for r in range(lanes))
                accs = lax.fori_loop(0, D_MODEL // lanes, step,
                                     tuple(jnp.zeros((lanes,), F32) for _ in range(lanes)))
                hv = jnp.zeros((lanes,), F32)
                for r in range(lanes):
                    hv = jnp.where(lane == r, jnp.sum(accs[r]), hv)
                hid_v[pl.ds(h0 + c * SC_CHUNK + g * lanes, lanes)] = hv

        _sc_token_loop(n, eid_hbm, x_hbm, tab_hbm, out_hbm, idx_v, x_v, hid_v, (rows0, rows1), sems, compute)

    return _sc_call(body, n, D_MODEL, PICKS)(eid, x, tab)


def _sc_combine(eid, gh, tab):
    n = eid.shape[0]
    lanes = SC_LANES
    nacc = SC_FEATURE_BLOCK // lanes

    def body(eid_hbm, g_hbm, tab_hbm, out_hbm, idx_v, g_v, out_v, rows0, rows1, *sems):
        def compute(c, rows, g0, o0):
            for fb in range(D_MODEL // SC_FEATURE_BLOCK):
                f0 = fb * SC_FEATURE_BLOCK
                if c == 0:
                    init = tuple(jnp.zeros((lanes,), F32) for _ in range(nacc))
                else:
                    init = tuple(out_v[pl.ds(o0 + f0 + q * lanes, lanes)] for q in range(nacc))

                def step(r, accs):
                    gk = plsc.load_gather(g_v, [jnp.full((lanes,), g0 + c * SC_CHUNK + r, jnp.int32)])
                    return tuple(accs[q] + gk * rows[r, pl.ds(f0 + q * lanes, lanes)] for q in range(nacc))
                accs = lax.fori_loop(0, SC_CHUNK, step, init)
                for q in range(nacc):
                    out_v[pl.ds(o0 + f0 + q * lanes, lanes)] = accs[q]

        _sc_token_loop(n, eid_hbm, g_hbm, tab_hbm, out_hbm, idx_v, g_v, out_v, (rows0, rows1), sems, compute)

    return _sc_call(body, n, PICKS, D_MODEL)(eid, gh, tab)


def _after_spec(after):
    return pl.BlockSpec((SUBLANES, after.shape[1]), lambda i: (0, 0))


def _gate_gelu_kernel(h_ref, g_ref, after_ref, o_ref):
    hid = h_ref[...]
    o_ref[...] = g_ref[...] * (0.5 * hid * (1.0 + lax.erf(hid * (2.0 ** -0.5))))


def _gate_gelu(hid, gate, tb, after):
    t = hid.shape[0]
    return pl.pallas_call(
        _gate_gelu_kernel, out_shape=jax.ShapeDtypeStruct((t, PICKS), F32), grid=(t // tb,),
        in_specs=[_pick_block(tb), _pick_block(tb), _after_spec(after)], out_specs=_pick_block(tb),
        compiler_params=pltpu.CompilerParams(dimension_semantics=("parallel",)),
        name="gate_gelu",
    )(hid, gate, after)


def _final_kernel(x_ref, p_ref, g_ref, *rest):
    rest[-1][...] = _rms(x_ref[...] + p_ref[...], g_ref[...])


def _final(x, p, g, tb, out_row0, out_rows, into=None):
    b0 = out_row0 // tb
    local = pl.BlockSpec((tb, D_MODEL), lambda i: (i, 0))
    in_specs = [local, local, _const_spec(g.shape)]
    args = [x, p, g]
    aliases = {}
    if into is not None:
        aliases = {len(args): 0}
        in_specs.append(pl.BlockSpec(memory_space=pl.ANY))
        args.append(into)
    return pl.pallas_call(
        _final_kernel, out_shape=jax.ShapeDtypeStruct((out_rows, D_MODEL), F32), grid=(p.shape[0] // tb,),
        in_specs=in_specs, out_specs=pl.BlockSpec((tb, D_MODEL), lambda i: (i + b0, 0)),
        input_output_aliases=aliases,
        compiler_params=pltpu.CompilerParams(dimension_semantics=("parallel",), vmem_limit_bytes=VMEM_LIMIT),
        name="final_norm",
    )(*args)


PACK_BLOCK = 256


def _pack_kernel(lo_ref, hi_ref, o_ref):
    def bf16_bits(x):
        return lax.bitcast_convert_type(x.astype(BF16).astype(F32), jnp.int32)
    word = bf16_bits(hi_ref[...]) | lax.shift_right_logical(bf16_bits(lo_ref[...]), 16)
    for s in range(SUBLANES):
        o_ref[pl.ds(s, PACK_BLOCK, stride=SUBLANES), :] = word[:, s * LANES:(s + 1) * LANES]


def _pack_table(tab):
    steps = HALF_EXPERTS // PACK_BLOCK
    return pl.pallas_call(
        _pack_kernel, out_shape=jax.ShapeDtypeStruct((HALF_EXPERTS * SUBLANES, LANES), jnp.int32), grid=(steps,),
        in_specs=[pl.BlockSpec((PACK_BLOCK, D_MODEL), lambda i: (i, 0)),
                  pl.BlockSpec((PACK_BLOCK, D_MODEL), lambda i: (i + steps, 0))],
        out_specs=pl.BlockSpec((PACK_BLOCK * SUBLANES, LANES), lambda i: (i, 0)),
        compiler_params=pltpu.CompilerParams(dimension_semantics=("parallel",)),
        name="pack_table",
    )(tab, tab)


def _col(v, n=LANES):
    return jnp.pad(v.astype(F32), (0, n - v.shape[0])).reshape(1, n)


def _row(v, n=2 * SUBLANES):
    return jnp.pad(v.astype(F32), (0, n - v.shape[0])).reshape(n, 1)


def _peer(x, w, tb_sel, tb_exp, n_sc, out_row0=0, out_rows=None, after=None, into=None):
    t = x.shape[0]
    out_rows = t if out_rows is None else out_rows
    n_tc = t - n_sc
    if n_sc:
        xn_s, eid_s, _, _, gate_s = _select(x, w["ln_ffn"], w["w_pq"], w["sub_keys"], tb_sel, 0, n_sc, after=after)
        hid_s = _sc_hidden(eid_s, xn_s, w["u_tab"])
    xn, _, tile, shift, gate = _select(x, w["ln_ffn"], w["w_pq"], w["sub_keys"], tb_sel, n_sc, n_tc,
                                       after=xn_s if n_sc else None)
    gh = _hidden(tile, shift, xn, gate, w["u_pack"], tb_exp, after=into)
    if n_sc:
        out_s = _sc_combine(eid_s, _gate_gelu(hid_s, gate_s, tb_sel, after=gh), w["v_tab"])
    y = _combine(tile, shift, gh, w["v_pack"], x, w["final_norm"], tb_exp, n_sc, out_row0 + n_sc, out_rows, into)
    if n_sc:
        y = _final(x, out_s, w["final_norm"], tb_sel, out_row0, out_rows, into=y)
    return y


def kernel(x_prompt, x_sample, mem_prompt, cache_conv_a, cache_conv_ssd, state_ssd, cache_mem_k, cache_mem_v, ln_mix_w, w_in, conv_a_w, conv_s_w, conv_s_b, dt_bias, a_log, d_skip, ssd_norm_w, w_out, ln_mem_w, mem_norm_w, w_mq, w_mk, w_mv, w_mo, ln_ffn_w, w_pq, sub_keys, u_tab, v_tab, final_norm_w):
    depth = w_in.shape[0]
    assert depth == 1
    l = 0
    n_main = 3 * D_CONV + D_SSD + XBC_DIM
    w_dt = w_in[l][:, n_main:]
    w = {
        "ln_mix": ln_mix_w[l].reshape(1, D_MODEL),
        "w_in": w_in[l][:, :n_main].astype(BF16),
        "w_dt": jnp.pad(w_dt, ((0, 0), (0, LANES - SSD_HEADS))).astype(BF16),
        "w_dtt": jnp.pad(w_dt.T, ((0, 2 * SUBLANES - SSD_HEADS), (0, 0))).astype(BF16),
        "conv_a_w": conv_a_w[l], "conv_s_w": conv_s_w[l], "conv_s_b": conv_s_b[l].reshape(1, XBC_DIM),
        "dtb_c": _col(dt_bias[l]), "alog_c": _col(a_log[l]), "dsk_c": _col(d_skip[l]),
        "dtb_r": _row(dt_bias[l]), "alog_r": _row(a_log[l]),
        "ssd_norm": ssd_norm_w[l].reshape(1, D_SSD),
        "w_out": w_out[l].astype(BF16),
        "ln_ffn": ln_ffn_w[l].reshape(1, D_MODEL),
        "w_pq": w_pq[l].astype(BF16).reshape(D_MODEL, PEER_HEADS, PEER_DK).transpose(1, 0, 2),
        "sub_keys": sub_keys[l].astype(BF16),
        "u_pack": _pack_table(u_tab[l]),
        "v_pack": _pack_table(v_tab[l]),
        "u_tab": u_tab[l], "v_tab": v_tab[l],
        "final_norm": final_norm_w.reshape(1, D_MODEL),
    }
    ln_mem = ln_mem_w[l].reshape(1, D_MODEL)
    wq, wo = w_mq[l].astype(BF16), w_mo[l].astype(BF16)

    bp, sp, _ = x_prompt.shape
    bs, ss, _ = x_sample.shape

    mk, mv = _memory_kv(mem_prompt.reshape(bp * N_MEM, D_MODEL), mem_norm_w[l].reshape(1, D_MODEL),
                        w_mk[l].astype(BF16), w_mv[l].astype(BF16))
    mk3, mv3 = mk.reshape(bp, N_MEM, D_MODEL), mv.reshape(bp, N_MEM, D_MODEL)
    n1 = SC_EARLY_SEQS
    h1, pa1, ps1, ph1 = _mixer(x_prompt, None, w, 256, 0, n1)
    h1 = _attn(h1, mk3, mv3, ln_mem, wq, wo, 256).reshape(n1 * sp, D_MODEL)

    xn1, eid1, _, _, gate1 = _select(h1, w["ln_ffn"], w["w_pq"], w["sub_keys"], 256, 0, n1 * sp)
    hid1 = _sc_hidden(eid1, xn1, w["u_tab"])
    h2, pa2, ps2, ph2 = _mixer(x_prompt, None, w, 256, n1, bp - n1, after=xn1)
    h2 = _attn(h2, mk3, mv3, ln_mem, wq, wo, 256, kv0=n1).reshape((bp - n1) * sp, D_MODEL)

    hs, sa, ssd, sh = _mixer(x_sample, (cache_conv_a[l], cache_conv_ssd[l],
                                        state_ssd[l].reshape((bs,) + PAIR_STATE)), w, 256, after=h2)
    hs = _attn(hs, cache_mem_k[l], cache_mem_v[l], ln_mem, wq, wo, 256).reshape(bs * ss, D_MODEL)
    xn_q, _, tile_q, shift_q, gate_q = _select(hs, w["ln_ffn"], w["w_pq"], w["sub_keys"], 256, 0, bs * ss)

    gh1 = _gate_gelu(hid1, gate1, 256, after=xn_q)
    out1 = _sc_combine(eid1, gh1, w["v_tab"])
    y1 = _final(h1, out1, w["final_norm"], 256, 0, bp * sp)
    y_prompt = _peer(h2, w, 256, 64, SC_PROMPT_TOKENS, n1 * sp, bp * sp, after=gh1, into=y1).reshape(bp, sp, D_MODEL)
    pa, ps, ph = (jnp.concatenate(p, axis=0) for p in ((pa1, pa2), (ps1, ps2), (ph1, ph2)))

    gh_q = _hidden(tile_q, shift_q, xn_q, gate_q, w["u_pack"], 64)
    y_sample = _combine(tile_q, shift_q, gh_q, w["v_pack"], hs, w["final_norm"], 64, 0, 0, bs * ss)
    y_sample = y_sample.reshape(bs, ss, D_MODEL)

    kv_shape = (1, bp, N_MEM, MEM_HEADS, MEM_HEAD_DIM)
    st_shape = (SSD_HEADS, SSD_HEAD_DIM, SSD_STATE)
    return (y_prompt, y_sample, pa[None], ps[None], ph.reshape((1, bp) + st_shape), mk.reshape(kv_shape),
            mv.reshape(kv_shape), sa[None], ssd[None], sh.reshape((1, bs) + st_shape))
```

```python
import functools

import jax
import jax.numpy as jnp
from jax import lax
from jax.experimental import pallas as pl
from jax.experimental.pallas import tpu as pltpu
from jax.experimental.pallas import tpu_sc as plsc

D_MODEL = 1024
CHUNK = 64
D_CONV = 512
CONV_A_W = 3
D_SSD = 512
SSD_HEAD_DIM = 64
SSD_HEADS = 8
SSD_GROUPS = 2
SSD_HPG = 4
SSD_STATE = 128
SSD_CONV_W = 4
XBC_DIM = 1024
N_MEM = 256
MEM_HEADS = 4
MEM_HEAD_DIM = 256
PEER_HEADS = 8
N_KEYS = 128
N_EXPERTS = N_KEYS * N_KEYS
PEER_TOPK = 16
PEER_DK = 256
PEER_DK_HALF = 128
PICKS = PEER_HEADS * PEER_TOPK
EPS = 1e-6

LANES = 128
SUBLANES = 8
HALF_EXPERTS = N_EXPERTS // 2
PAIR_STATE = (SSD_HEADS // 2, 2 * SSD_HEAD_DIM, SSD_STATE)
VMEM_LIMIT = 56 * 1024 * 1024

F32 = jnp.float32
BF16 = jnp.bfloat16
HI = lax.Precision.HIGHEST


def _rms(x, g):
    return x * lax.rsqrt(jnp.mean(x * x, axis=-1, keepdims=True) + EPS) * g


def _silu(x):
    return x * (1.0 / (1.0 + jnp.exp(-x)))


def _softplus(x):
    return jnp.maximum(x, 0.0) + jnp.log1p(jnp.exp(-jnp.abs(x)))


def _dot(a, b):
    return jnp.dot(a.astype(BF16), b.astype(BF16), preferred_element_type=F32)


def _dot_nt(a, b):
    return lax.dot_general(a.astype(BF16), b.astype(BF16), (((1,), (1,)), ((), ())),
                           preferred_element_type=F32)


def _dot_tn(a, b):
    return lax.dot_general(a.astype(BF16), b.astype(BF16), (((0,), (0,)), ((), ())),
                           preferred_element_type=F32)


def _const_spec(shape):
    n = len(shape)
    return pl.BlockSpec(shape, lambda *_: (0,) * n)


def _mixer_kernel(has_hist, has_after, ts, lc, *refs):
    if has_after:
        refs = refs[1:]
    if has_hist:
        x_ref, hista_ref, hists_ref, h0_ref = refs[:4]
        refs = refs[4:]
    else:
        x_ref = refs[0]
        refs = refs[1:]
    (lnw_ref, win_ref, wdt_ref, wdtt_ref, caw_ref, csw_ref, csb_ref, dtb_c_ref, alog_c_ref, dsk_c_ref,
     dtb_r_ref, alog_r_ref, nw_ref, wout_ref,
     h1_ref, na_ref, ns_ref, nh_ref, cata, cats, hst) = refs
    s = pl.program_id(1)
    pad = SUBLANES

    @pl.when(s == 0)
    def _():
        if has_hist:
            cata[pad - 2:pad, :] = hista_ref[...]
            cats[pad - 3:pad, :] = hists_ref[...]
            hst[...] = h0_ref[...]
        else:
            cata[0:pad, :] = jnp.zeros((pad, D_CONV), F32)
            cats[0:pad, :] = jnp.zeros((pad, XBC_DIM), F32)
            hst[...] = jnp.zeros(hst.shape, F32)

    x = x_ref[...]
    xn = _rms(x, lnw_ref[...]).astype(BF16)
    proj = jnp.dot(xn, win_ref[...], preferred_element_type=F32)
    dt_c = jnp.dot(xn, wdt_ref[...], preferred_element_type=F32)
    dt_r = lax.dot_general(wdtt_ref[...], xn, (((1,), (1,)), ((), ())),
                           preferred_element_type=F32)
    g_b = proj[:, 0:D_CONV]
    g_c = proj[:, D_CONV:2 * D_CONV]
    v_in = proj[:, 2 * D_CONV:3 * D_CONV]
    z = proj[:, 3 * D_CONV:3 * D_CONV + D_SSD]
    xbc = proj[:, 3 * D_CONV + D_SSD:3 * D_CONV + D_SSD + XBC_DIM]

    ua = g_c * v_in
    cata[pad:pad + ts, :] = ua
    caw = caw_ref[...]
    conv_a = (caw[0:1] * cata[pad - 2:pad - 2 + ts, :] + caw[1:2] * cata[pad - 1:pad - 1 + ts, :]
              + caw[2:3] * ua)
    y_a = g_b * conv_a
    hist_a = cata[pad + ts - 2:pad + ts, :]
    na_ref[...] = hist_a
    cata[pad - 2:pad, :] = hist_a

    cats[pad:pad + ts, :] = xbc
    csw = csw_ref[...]
    conv_s = (csw[0:1] * cats[pad - 3:pad - 3 + ts, :] + csw[1:2] * cats[pad - 2:pad - 2 + ts, :]
              + csw[2:3] * cats[pad - 1:pad - 1 + ts, :] + csw[3:4] * xbc)
    hist_s = cats[pad + ts - 3:pad + ts, :]
    ns_ref[...] = hist_s
    cats[pad - 3:pad, :] = hist_s
    xbc_c = _silu(conv_s + csb_ref[...])
    xs = xbc_c[:, 0:D_SSD]

    dtp_c = _softplus(dt_c + dtb_c_ref[...])
    dtp_r = _softplus(dt_r + dtb_r_ref[...])
    a_c = dtp_c * (-jnp.exp(alog_c_ref[...]))
    a_r = dtp_r * (-jnp.exp(alog_r_ref[...]))
    dsk_c = dsk_c_ref[...]

    lane = lax.broadcasted_iota(jnp.int32, (1, LANES), 1)
    first_head = lane < SSD_HEAD_DIM
    row2 = lax.broadcasted_iota(jnp.int32, (2 * SSD_HEAD_DIM, 1), 0) < SSD_HEAD_DIM
    li = lax.broadcasted_iota(jnp.int32, (lc, lc), 0)
    si = lax.broadcasted_iota(jnp.int32, (lc, lc), 1)
    causal = si <= li
    tril = causal.astype(F32)
    jrow = lax.broadcasted_iota(jnp.int32, (ts, lc), 0)
    scol = lax.broadcasted_iota(jnp.int32, (ts, lc), 1)

    def pair(col, h):
        return jnp.where(first_head, col[:, h:h + 1], col[:, h + 1:h + 2])

    y_chunks = []
    for c in range(ts // lc):
        r0 = c * lc
        rows = slice(r0, r0 + lc)
        acum_c = jnp.dot(tril, a_c[rows], preferred_element_type=F32, precision=HI)
        upper = ((jrow >= r0) & (jrow <= r0 + scol)).astype(F32)
        acum_r = jnp.dot(a_r, upper, preferred_element_type=F32, precision=HI)
        last_c = acum_c[lc - 1:lc, :]
        dec_c = jnp.exp(last_c - acum_c)
        eac_c = jnp.exp(acum_c)
        y_pairs = []
        for g in range(SSD_GROUPS):
            b_g = xbc_c[rows, D_SSD + g * SSD_STATE:D_SSD + (g + 1) * SSD_STATE]
            c_g = xbc_c[rows, D_SSD + (SSD_GROUPS + g) * SSD_STATE:D_SSD + (SSD_GROUPS + g + 1) * SSD_STATE]
            cb = _dot_nt(c_g, b_g)
            for q in range(SSD_HPG // 2):
                h = g * SSD_HPG + 2 * q
                pi = h // 2
                xs_p = xs[rows, pi * LANES:(pi + 1) * LANES]
                xdt = xs_p * pair(dtp_c[rows], h)
                res = []
                for hh in (h, h + 1):
                    seg = acum_c[:, hh:hh + 1] - acum_r[hh:hh + 1, :]
                    m_h = cb * jnp.exp(jnp.where(causal, seg, -jnp.inf))
                    res.append(_dot(m_h, xdt))
                y_diag = jnp.where(first_head, res[0], res[1])
                h_in = hst[pi]
                y_off = _dot_nt(c_g, h_in) * pair(eac_c, h)
                st = _dot_tn(xdt * pair(dec_c, h), b_g)
                cd = jnp.exp(jnp.where(row2, last_c[:, h:h + 1], last_c[:, h + 1:h + 2]))
                hst[pi] = cd * h_in + st
                y_pairs.append(y_diag + y_off + pair(dsk_c, h) * xs_p)
        y_chunks.append(jnp.concatenate(y_pairs, axis=1))
    y = y_chunks[0] if len(y_chunks) == 1 else jnp.concatenate(y_chunks, axis=0)
    y_b = _rms(y * _silu(z), nw_ref[...])
    wout = wout_ref[...]
    out = (jnp.dot(y_a.astype(BF16), wout[0:D_CONV], preferred_element_type=F32)
           + jnp.dot(y_b.astype(BF16), wout[D_CONV:], preferred_element_type=F32))
    h1_ref[...] = x + out
    nh_ref[...] = hst[...]


def _mixer(x, hist, w, ts, b0=0, nb=None, after=None):
    _, s, _ = x.shape
    b = x.shape[0] if nb is None else nb
    lc = min(CHUNK, s)
    ts = min(ts, s)
    has_hist = hist is not None
    assert not (has_hist and b0)
    per_b3 = lambda shape: pl.BlockSpec((None,) + shape, lambda i, j: (i,) + (0,) * len(shape))
    in_specs = [pl.BlockSpec((None, ts, D_MODEL), lambda i, j: (i + b0, j, 0))]
    args = [x]
    if after is not None:
        in_specs.insert(0, pl.BlockSpec((SUBLANES, after.shape[1]), lambda i, j: (0, 0)))
        args.insert(0, after)
    if has_hist:
        in_specs += [per_b3((CONV_A_W - 1, D_CONV)), per_b3((SSD_CONV_W - 1, XBC_DIM)),
                     per_b3(PAIR_STATE)]
        args += list(hist)
    wargs = [w["ln_mix"], w["w_in"], w["w_dt"], w["w_dtt"], w["conv_a_w"], w["conv_s_w"], w["conv_s_b"],
             w["dtb_c"], w["alog_c"], w["dsk_c"], w["dtb_r"], w["alog_r"], w["ssd_norm"], w["w_out"]]
    in_specs += [_const_spec(a.shape) for a in wargs]
    args += wargs
    out_shape = (jax.ShapeDtypeStruct((b, s, D_MODEL), F32),
                 jax.ShapeDtypeStruct((b, CONV_A_W - 1, D_CONV), F32),
                 jax.ShapeDtypeStruct((b, SSD_CONV_W - 1, XBC_DIM), F32),
                 jax.ShapeDtypeStruct((b,) + PAIR_STATE, F32))
    out_specs = (pl.BlockSpec((None, ts, D_MODEL), lambda i, j: (i, j, 0)),
                 per_b3((CONV_A_W - 1, D_CONV)), per_b3((SSD_CONV_W - 1, XBC_DIM)),
                 per_b3(PAIR_STATE))
    return pl.pallas_call(
        functools.partial(_mixer_kernel, has_hist, after is not None, ts, lc),
        out_shape=out_shape, grid=(b, s // ts), in_specs=in_specs, out_specs=out_specs,
        scratch_shapes=[pltpu.VMEM((SUBLANES + ts, D_CONV), F32), pltpu.VMEM((SUBLANES + ts, XBC_DIM), F32),
                        pltpu.VMEM(PAIR_STATE, F32)],
        compiler_params=pltpu.CompilerParams(dimension_semantics=("parallel", "arbitrary"),
                                             vmem_limit_bytes=VMEM_LIMIT),
        name="mixer_hist" if has_hist else "mixer",
    )(*args)


def _memkv_kernel(m_ref, g_ref, wk_ref, wv_ref, k_ref, v_ref):
    mn = _rms(m_ref[...], g_ref[...]).astype(BF16)
    k_ref[...] = jnp.dot(mn, wk_ref[...], preferred_element_type=F32)
    v_ref[...] = jnp.dot(mn, wv_ref[...], preferred_element_type=F32)


def _memory_kv(mem, g, wk, wv, tm=256):
    t = mem.shape[0]
    blk = pl.BlockSpec((tm, D_MODEL), lambda i: (i, 0))
    return pl.pallas_call(
        _memkv_kernel,
        out_shape=(jax.ShapeDtypeStruct((t, D_MODEL), F32),) * 2,
        grid=(t // tm,), in_specs=[blk, _const_spec(g.shape), _const_spec(wk.shape), _const_spec(wv.shape)],
        out_specs=(blk, blk),
        compiler_params=pltpu.CompilerParams(dimension_semantics=("parallel",), vmem_limit_bytes=VMEM_LIMIT),
        name="memory_kv",
    )(mem, g, wk, wv)


def _attn_kernel(x_ref, k_ref, v_ref, g_ref, wq_ref, wo_ref, o_ref):
    x = x_ref[...]
    xn = _rms(x, g_ref[...]).astype(BF16)
    q = jnp.dot(xn, wq_ref[...], preferred_element_type=F32)
    by_head = len(k_ref.shape) == 3
    wo = wo_ref[...]
    acc = x
    for h in range(MEM_HEADS):
        cols = slice(h * MEM_HEAD_DIM, (h + 1) * MEM_HEAD_DIM)
        k_h = k_ref[:, h, :] if by_head else k_ref[:, cols]
        v_h = v_ref[:, h, :] if by_head else v_ref[:, cols]
        sc = _dot_nt(q[:, cols], k_h) * (MEM_HEAD_DIM ** -0.5)
        sc = sc - jnp.max(sc, axis=-1, keepdims=True)
        p = jnp.exp(sc)
        p = p / jnp.sum(p, axis=-1, keepdims=True)
        o_h = _dot(p, v_h)
        acc = acc + jnp.dot(o_h.astype(BF16), wo[cols, :], preferred_element_type=F32)
    o_ref[...] = acc


def _attn(x, k, v, g, wq, wo, ts, kv0=0):
    b, s, _ = x.shape
    ts = min(ts, s)
    xblk = pl.BlockSpec((None, ts, D_MODEL), lambda i, j: (i, j, 0))
    if k.ndim == 3:
        kvblk = pl.BlockSpec((None, N_MEM, D_MODEL), lambda i, j: (i + kv0, 0, 0))
    else:
        kvblk = pl.BlockSpec((None, N_MEM, MEM_HEADS, MEM_HEAD_DIM), lambda i, j: (i + kv0, 0, 0, 0))
    return pl.pallas_call(
        _attn_kernel,
        out_shape=jax.ShapeDtypeStruct((b, s, D_MODEL), F32),
        grid=(b, s // ts),
        in_specs=[xblk, kvblk, kvblk, _const_spec(g.shape), _const_spec(wq.shape), _const_spec(wo.shape)],
        out_specs=xblk,
        compiler_params=pltpu.CompilerParams(dimension_semantics=("parallel", "parallel"),
                                             vmem_limit_bytes=VMEM_LIMIT),
        name="attn",
    )(x, k, v, g, wq, wo)


def _top_k_rows(sc, k, payload=None):
    n = sc.shape[0]
    rows = lax.broadcasted_iota(jnp.int32, sc.shape, 0).astype(F32)
    vals, outs = [], []
    for _ in range(k):
        m = jnp.max(sc, axis=0, keepdims=True)
        cand = jnp.where(sc == m, rows, float(n))
        first = jnp.min(cand, axis=0, keepdims=True)
        hit = cand == first
        sc = jnp.where(hit, -jnp.inf, sc)
        vals.append(m)
        if payload is None:
            outs.append(first.astype(jnp.int32))
        else:
            outs.append(jnp.sum(jnp.where(hit, payload, 0), axis=0, keepdims=True))
    return jnp.concatenate(vals, axis=0), jnp.concatenate(outs, axis=0)


_LIVE_B = tuple(PEER_TOPK // (a + 1) for a in range(1, SUBLANES))


def _pair_candidates(v1, i1, v2, i2):
    sub = lax.broadcasted_iota(jnp.int32, (SUBLANES, v1.shape[1]), 0)
    comb = [v1[0:1] + v2]
    cidx = [i1[0:1] * N_KEYS + i2]
    for a in range(1, SUBLANES):
        comb.append(jnp.where(sub < _LIVE_B[a - 1], v1[a:a + 1] + v2[0:SUBLANES], -jnp.inf))
        cidx.append(i1[a:a + 1] * N_KEYS + i2[0:SUBLANES])
    comb.append(v1[SUBLANES:] + v2[0:1])
    cidx.append(i1[SUBLANES:] * N_KEYS + i2[0:1])
    return jnp.concatenate(comb, axis=0), jnp.concatenate(cidx, axis=0)


def _select_kernel(tb, x_ref, g_ref, wq_ref, keys_ref, *rest):
    xn_ref, eid_ref, tile_ref, shift_ref, gate_ref, eid_s, gate_s = rest[-7:]
    xn = _rms(x_ref[...], g_ref[...])
    xn_ref[...] = xn
    xn = xn.astype(BF16)

    def head(h, carry):
        q = jnp.dot(xn, wq_ref[h], preferred_element_type=F32)
        sc1 = _dot_nt(keys_ref[h, 0], q[:, 0:PEER_DK_HALF])
        sc2 = _dot_nt(keys_ref[h, 1], q[:, PEER_DK_HALF:PEER_DK])
        rows = pl.ds(pl.multiple_of(h * PEER_TOPK, PEER_TOPK), PEER_TOPK)
        for c0 in range(0, tb, LANES):
            cols = slice(c0, c0 + LANES)
            v1, i1 = _top_k_rows(sc1[:, cols], PEER_TOPK)
            v2, i2 = _top_k_rows(sc2[:, cols], PEER_TOPK)
            comb, cidx = _pair_candidates(v1, i1, v2, i2)
            top, eid = _top_k_rows(comb, PEER_TOPK, cidx)
            ex = jnp.exp(top - top[0:1])
            gate_s[rows, cols] = ex / jnp.sum(ex, axis=0, keepdims=True)
            eid_s[rows, cols] = eid
        return carry

    lax.fori_loop(0, PEER_HEADS, head, 0)
    for c0 in range(0, tb, LANES):
        toks = slice(c0, c0 + LANES)
        eid = eid_s[:, toks].T
        upper = eid >= HALF_EXPERTS
        eid_ref[toks, :] = eid
        tile_ref[toks, :] = jnp.where(upper, eid - HALF_EXPERTS, eid) * SUBLANES
        shift_ref[toks, :] = jnp.where(upper, 0, 16)
        gate_ref[toks, :] = gate_s[:, toks].T


def _select(x, g, wq, keys, tb, t0, t, after=None):
    b0 = t0 // tb
    extra = [] if after is None else [after]
    xin = pl.BlockSpec((tb, D_MODEL), lambda i: (i + b0, 0))
    xblk = pl.BlockSpec((tb, D_MODEL), lambda i: (i, 0))
    picks_i = jax.ShapeDtypeStruct((t, PICKS), jnp.int32)
    return pl.pallas_call(
        functools.partial(_select_kernel, tb),
        out_shape=(jax.ShapeDtypeStruct((t, D_MODEL), F32), picks_i, picks_i, picks_i,
                   jax.ShapeDtypeStruct((t, PICKS), F32)),
        grid=(t // tb,),
        in_specs=[xin, _const_spec(g.shape), _const_spec(wq.shape), _const_spec(keys.shape)]
        + [_after_spec(a) for a in extra],
        out_specs=(xblk,) + (_pick_block(tb),) * 4,
        scratch_shapes=[pltpu.VMEM((PICKS, tb), jnp.int32), pltpu.VMEM((PICKS, tb), F32)],
        compiler_params=pltpu.CompilerParams(dimension_semantics=("parallel",), vmem_limit_bytes=VMEM_LIMIT),
        name="peer_select",
    )(x, g, wq, keys, *extra)


HI_HALF_MASK = -(1 << 16)


def _unpack(word, shift):
    return lax.bitcast_convert_type((word << shift) & HI_HALF_MASK, F32)


def _splat_picks(blk, one_pass):
    tb = blk.shape[0]
    r = lax.broadcasted_iota(jnp.int32, (PICKS, LANES), 0)
    c = lax.broadcasted_iota(jnp.int32, (PICKS, LANES), 1)
    eye = (r == c)[None]
    ones = jnp.ones((LANES, LANES), BF16)

    def one(piece):
        diag = jnp.where(eye, piece[:, None, :], 0.0).astype(BF16).reshape(tb * PICKS, LANES)
        return jnp.dot(diag, ones, preferred_element_type=F32)

    if one_pass:
        out = one(blk)
    else:
        hi = blk.astype(BF16).astype(F32)
        rest = blk - hi
        mid = rest.astype(BF16).astype(F32)
        out = one(hi) + one(mid) + one(rest - mid)
    return out.reshape(tb, PICKS, LANES)


def _full(ref, k):
    return jnp.broadcast_to(ref[k:k + 1, :], (SUBLANES, LANES))


def _expert_row(tab_ref, tile_ref, t, k, shifts):
    idx = pl.multiple_of(tile_ref[t, k], SUBLANES)
    return _unpack(tab_ref[pl.ds(idx, SUBLANES), :], _full(shifts, k))


def _hidden_kernel(tb, tile_ref, shift_ref, x_ref, gate_ref, tab_ref, *rest):
    gh_ref, bsh, stage, parts, xt = rest[-5:]
    bsh[...] = _splat_picks(shift_ref[...].astype(F32), True).astype(jnp.int32)
    for s in range(SUBLANES):
        xt[pl.ds(s, tb, stride=SUBLANES), :] = x_ref[:, s * LANES:(s + 1) * LANES]

    half = PICKS // 2
    quad = SUBLANES // 2
    low = lax.broadcasted_iota(jnp.int32, (SUBLANES, LANES), 0) < quad

    def token(t, carry):
        x = xt[pl.ds(pl.multiple_of(t * SUBLANES, SUBLANES), SUBLANES), :]
        shifts = bsh.at[t]
        for j in range(half):
            a = x * _expert_row(tab_ref, tile_ref, t, j, shifts)
            b = x * _expert_row(tab_ref, tile_ref, t, j + half, shifts)
            stage[j * SUBLANES:(j + 1) * SUBLANES, :] = jnp.where(low, a + pltpu.roll(a, quad, 0),
                                                                  b + pltpu.roll(b, quad, 0))
        for p in range(2):
            part = stage[pl.ds(p * quad, half, stride=SUBLANES), :]
            for s in range(1, quad):
                part = part + stage[pl.ds(p * quad + s, half, stride=SUBLANES), :]
            parts[t, p * half:(p + 1) * half, :] = part
        return carry

    lax.fori_loop(0, tb, token, 0)
    gh_ref[...] = jnp.sum(parts[...], axis=-1)
    hid = gh_ref[...]
    gelu = 0.5 * hid * (1.0 + lax.erf(hid * (2.0 ** -0.5)))
    gh_ref[...] = gate_ref[...] * gelu


def _pick_block(tb, space=None):
    return pl.BlockSpec((tb, PICKS), lambda i: (i, 0), memory_space=space)


def _table_spec(tab):
    return pl.BlockSpec(tab.shape, lambda i: (0, 0), pipeline_mode=pl.Buffered(1))


def _hidden(tile, shift, xg, gate, tab, tb, after=None):
    t = xg.shape[0]
    extra = [] if after is None else [after]
    return pl.pallas_call(
        functools.partial(_hidden_kernel, tb),
        out_shape=jax.ShapeDtypeStruct((t, PICKS), F32),
        grid=(t // tb,),
        in_specs=[_pick_block(tb, pltpu.SMEM), _pick_block(tb),
                  pl.BlockSpec((tb, D_MODEL), lambda i: (i, 0)),
                  _pick_block(tb), _table_spec(tab)] + [_after_spec(a) for a in extra],
        out_specs=_pick_block(tb),
        scratch_shapes=[pltpu.VMEM((tb, PICKS, LANES), jnp.int32), pltpu.VMEM((PICKS // 2 * SUBLANES, LANES), F32),
                        pltpu.VMEM((tb, PICKS, LANES), F32), pltpu.VMEM((tb * SUBLANES, LANES), F32)],
        compiler_params=pltpu.CompilerParams(dimension_semantics=("parallel",), vmem_limit_bytes=VMEM_LIMIT),
        name="peer_hidden",
    )(tile, shift, xg, gate, tab, *extra)


def _combine_kernel(tb, tile_ref, shift_ref, gh_ref, tab_ref, x_ref, g_ref, *rest):
    o_ref, bsh, bg, ot = rest[-4:]
    bsh[...] = _splat_picks(shift_ref[...].astype(F32), True).astype(jnp.int32)
    bg[...] = _splat_picks(gh_ref[...], False)

    def token(t, carry):
        shifts, gates = bsh.at[t], bg.at[t]
        acc = [jnp.zeros((SUBLANES, LANES), F32) for _ in range(4)]
        for k in range(PICKS):
            acc[k % 4] = acc[k % 4] + _full(gates, k) * _expert_row(tab_ref, tile_ref, t, k, shifts)
        ot[pl.ds(pl.multiple_of(t * SUBLANES, SUBLANES), SUBLANES), :] = (acc[0] + acc[1]) + (acc[2] + acc[3])
        return carry

    lax.fori_loop(0, tb, token, 0)
    for s in range(SUBLANES):
        o_ref[:, s * LANES:(s + 1) * LANES] = ot[pl.ds(s, tb, stride=SUBLANES), :]
    o_ref[...] = _rms(x_ref[...] + o_ref[...], g_ref[...])


def _combine(tile, shift, gh, tab, x, g, tb, x_row0, out_row0, out_rows, into=None):
    t = gh.shape[0]
    xb0, ob0 = x_row0 // tb, out_row0 // tb
    extra = [] if into is None else [into]
    return pl.pallas_call(
        functools.partial(_combine_kernel, tb),
        out_shape=jax.ShapeDtypeStruct((out_rows, D_MODEL), F32),
        grid=(t // tb,),
        in_specs=[_pick_block(tb, pltpu.SMEM), _pick_block(tb), _pick_block(tb), _table_spec(tab),
                  pl.BlockSpec((tb, D_MODEL), lambda i: (i + xb0, 0)), _const_spec(g.shape)]
        + [pl.BlockSpec(memory_space=pl.ANY) for _ in extra],
        input_output_aliases={6: 0} if extra else {},
        out_specs=pl.BlockSpec((tb, D_MODEL), lambda i: (i + ob0, 0)),
        scratch_shapes=[pltpu.VMEM((tb, PICKS, LANES), jnp.int32), pltpu.VMEM((tb, PICKS, LANES), F32),
                        pltpu.VMEM((tb * SUBLANES, LANES), F32)],
        compiler_params=pltpu.CompilerParams(dimension_semantics=("parallel",), vmem_limit_bytes=VMEM_LIMIT),
        name="peer_combine",
    )(tile, shift, gh, tab, x, g, *extra)


SC_LANES = 16
SC_WORKERS = 32
SC_CHUNK = 32
SC_FEATURE_BLOCK = 256
SC_EARLY_GROUPS = (1, 3)
SC_PROMPT_TOKENS = 12800


def _sc_call(body, n, side_width, out_width):
    mesh = plsc.VectorSubcoreMesh(core_axis_name="c", subcore_axis_name="s")
    return pl.kernel(body, mesh=mesh, out_type=jax.ShapeDtypeStruct((n, out_width), F32),
                     compiler_params=pltpu.CompilerParams(needs_layout_passes=False),
                     scratch_types=[pltpu.VMEM((2 * PICKS,), jnp.int32), pltpu.VMEM((2 * side_width,), F32),
                                    pltpu.VMEM((2 * out_width,), F32),
                                    pltpu.VMEM((SC_CHUNK, D_MODEL), F32), pltpu.VMEM((SC_CHUNK, D_MODEL), F32)]
                     + [pltpu.SemaphoreType.DMA] * 5)


def _sc_token_loop(n, eid_hbm, side_hbm, tab_hbm, out_hbm, idx_v, side_v, out_v, bufs, sems, compute):
    sem_rows, sem_fetch, sem_put = sems[0:2], sems[2], sems[3:5]
    per_worker = n // SC_WORKERS
    first = (lax.axis_index("s") * 2 + lax.axis_index("c")) * per_worker
    last = first + per_worker - 1
    chunks = PICKS // SC_CHUNK
    side_w = side_v.shape[0] // 2
    out_w = out_v.shape[0] // 2

    def fetch(t, slot):
        return (pltpu.make_async_copy(eid_hbm.at[t], idx_v.at[pl.ds(slot * PICKS, PICKS)], sem_fetch),
                pltpu.make_async_copy(side_hbm.at[t], side_v.at[pl.ds(slot * side_w, side_w)], sem_fetch))

    def gather(slot, c):
        ids = idx_v.at[pl.ds(slot * PICKS + c * SC_CHUNK, SC_CHUNK)]
        return pltpu.make_async_copy(tab_hbm.at[ids], bufs[c % 2], sem_rows[c % 2])

    def put(t, slot):
        return pltpu.make_async_copy(out_v.at[pl.ds(slot * out_w, out_w)], out_hbm.at[t], sem_put[slot])

    for cp in fetch(first, 0):
        cp.start()
    for cp in fetch(first, 0):
        cp.wait()
    gather(0, 0).start()
    gather(0, 1).start()

    @pl.loop(0, per_worker, step=2)
    def _(i):
        for slot in range(2):
            t = first + i + slot
            nxt = jnp.minimum(t + 1, last)
            for cp in fetch(nxt, 1 - slot):
                cp.start()

            @pl.when(i >= 2)
            def _():
                put(t, slot).wait()

            for c in range(chunks):
                gather(slot, c).wait()
                compute(c, bufs[c % 2], slot * side_w, slot * out_w)
                if c + 2 < chunks:
                    gather(slot, c + 2).start()
                else:
                    if c + 2 == chunks:
                        for cp in fetch(nxt, 1 - slot):
                            cp.wait()
                    gather(1 - slot, c + 2 - chunks).start()
            put(t, slot).start()

    gather(0, 0).wait()
    gather(0, 1).wait()
    put(last, 0).wait()
    put(last, 1).wait()


def _sc_hidden(eid, x, tab):
    n = eid.shape[0]
    lanes = SC_LANES

    def body(eid_hbm, x_hbm, tab_hbm, out_hbm, idx_v, x_v, hid_v, rows0, rows1, *sems):
        lane = lax.iota(jnp.int32, lanes)

        def compute(c, rows, x0, h0):
            for g in range(SC_CHUNK // lanes):
                def step(j, accs):
                    off = pl.multiple_of(j * lanes, lanes)
                    xj = x_v[pl.ds(x0 + off, lanes)]
                    return tuple(accs[r] + xj * rows[g * lanes + r, pl.ds(off, lanes)] for r in range(lanes))
                accs = lax.fori_loop(0, D_MODEL // lanes, step,
                                     tuple(jnp.zeros((lanes,), F32) for _ in range(lanes)))
                hv = jnp.zeros((lanes,), F32)
                for r in range(lanes):
                    hv = jnp.where(lane == r, jnp.sum(accs[r]), hv)
                hid_v[pl.ds(h0 + c * SC_CHUNK + g * lanes, lanes)] = hv

        _sc_token_loop(n, eid_hbm, x_hbm, tab_hbm, out_hbm, idx_v, x_v, hid_v, (rows0, rows1), sems, compute)

    return _sc_call(body, n, D_MODEL, PICKS)(eid, x, tab)


def _sc_combine(eid, gh, tab):
    n = eid.shape[0]
    lanes = SC_LANES
    nacc = SC_FEATURE_BLOCK // lanes

    def body(eid_hbm, g_hbm, tab_hbm, out_hbm, idx_v, g_v, out_v, rows0, rows1, *sems):
        def compute(c, rows, g0, o0):
            for fb in range(D_MODEL // SC_FEATURE_BLOCK):
                f0 = fb * SC_FEATURE_BLOCK
                if c == 0:
                    init = tuple(jnp.zeros((lanes,), F32) for _ in range(nacc))
                else:
                    init = tuple(out_v[pl.ds(o0 + f0 + q * lanes, lanes)] for q in range(nacc))

                def step(r, accs):
                    gk = plsc.load_gather(g_v, [jnp.full((lanes,), g0 + c * SC_CHUNK + r, jnp.int32)])
                    return tuple(accs[q] + gk * rows[r, pl.ds(f0 + q * lanes, lanes)] for q in range(nacc))
                accs = lax.fori_loop(0, SC_CHUNK, step, init)
                for q in range(nacc):
                    out_v[pl.ds(o0 + f0 + q * lanes, lanes)] = accs[q]

        _sc_token_loop(n, eid_hbm, g_hbm, tab_hbm, out_hbm, idx_v, g_v, out_v, (rows0, rows1), sems, compute)

    return _sc_call(body, n, PICKS, D_MODEL)(eid, gh, tab)


def _after_spec(after):
    return pl.BlockSpec((SUBLANES, after.shape[1]), lambda i: (0, 0))


def _gate_gelu_kernel(h_ref, g_ref, after_ref, o_ref):
    hid = h_ref[...]
    o_ref[...] = g_ref[...] * (0.5 * hid * (1.0 + lax.erf(hid * (2.0 ** -0.5))))


def _gate_gelu(hid, gate, tb, after):
    t = hid.shape[0]
    return pl.pallas_call(
        _gate_gelu_kernel, out_shape=jax.ShapeDtypeStruct((t, PICKS), F32), grid=(t // tb,),
        in_specs=[_pick_block(tb), _pick_block(tb), _after_spec(after)], out_specs=_pick_block(tb),
        compiler_params=pltpu.CompilerParams(dimension_semantics=("parallel",)),
        name="gate_gelu",
    )(hid, gate, after)


def _final_kernel(x_ref, p_ref, g_ref, *rest):
    rest[-1][...] = _rms(x_ref[...] + p_ref[...], g_ref[...])


def _final(x, p, g, tb, out_row0, out_rows, into=None):
    b0 = out_row0 // tb
    local = pl.BlockSpec((tb, D_MODEL), lambda i: (i, 0))
    in_specs = [local, local, _const_spec(g.shape)]
    args = [x, p, g]
    aliases = {}
    if into is not None:
        aliases = {len(args): 0}
        in_specs.append(pl.BlockSpec(memory_space=pl.ANY))
        args.append(into)
    return pl.pallas_call(
        _final_kernel, out_shape=jax.ShapeDtypeStruct((out_rows, D_MODEL), F32), grid=(p.shape[0] // tb,),
        in_specs=in_specs, out_specs=pl.BlockSpec((tb, D_MODEL), lambda i: (i + b0, 0)),
        input_output_aliases=aliases,
        compiler_params=pltpu.CompilerParams(dimension_semantics=("parallel",), vmem_limit_bytes=VMEM_LIMIT),
        name="final_norm",
    )(*args)


PACK_BLOCK = 256


def _pack_kernel(lo_ref, hi_ref, o_ref):
    def bf16_bits(x):
        return lax.bitcast_convert_type(x.astype(BF16).astype(F32), jnp.int32)
    word = bf16_bits(hi_ref[...]) | lax.shift_right_logical(bf16_bits(lo_ref[...]), 16)
    for s in range(SUBLANES):
        o_ref[pl.ds(s, PACK_BLOCK, stride=SUBLANES), :] = word[:, s * LANES:(s + 1) * LANES]


def _pack_table(tab):
    steps = HALF_EXPERTS // PACK_BLOCK
    return pl.pallas_call(
        _pack_kernel, out_shape=jax.ShapeDtypeStruct((HALF_EXPERTS * SUBLANES, LANES), jnp.int32), grid=(steps,),
        in_specs=[pl.BlockSpec((PACK_BLOCK, D_MODEL), lambda i: (i, 0)),
                  pl.BlockSpec((PACK_BLOCK, D_MODEL), lambda i: (i + steps, 0))],
        out_specs=pl.BlockSpec((PACK_BLOCK * SUBLANES, LANES), lambda i: (i, 0)),
        compiler_params=pltpu.CompilerParams(dimension_semantics=("parallel",)),
        name="pack_table",
    )(tab, tab)


def _col(v, n=LANES):
    return jnp.pad(v.astype(F32), (0, n - v.shape[0])).reshape(1, n)


def _row(v, n=2 * SUBLANES):
    return jnp.pad(v.astype(F32), (0, n - v.shape[0])).reshape(n, 1)


def _peer(x, w, tb_sel, tb_exp, n_sc, out_row0=0, out_rows=None, after=None, into=None):
    t = x.shape[0]
    out_rows = t if out_rows is None else out_rows
    n_tc = t - n_sc
    if n_sc:
        xn_s, eid_s, _, _, gate_s = _select(x, w["ln_ffn"], w["w_pq"], w["sub_keys"], tb_sel, 0, n_sc, after=after)
        hid_s = _sc_hidden(eid_s, xn_s, w["u_tab"])
    xn, _, tile, shift, gate = _select(x, w["ln_ffn"], w["w_pq"], w["sub_keys"], tb_sel, n_sc, n_tc,
                                       after=xn_s if n_sc else None)
    gh = _hidden(tile, shift, xn, gate, w["u_pack"], tb_exp, after=into)
    if n_sc:
        out_s = _sc_combine(eid_s, _gate_gelu(hid_s, gate_s, tb_sel, after=gh), w["v_tab"])
    y = _combine(tile, shift, gh, w["v_pack"], x, w["final_norm"], tb_exp, n_sc, out_row0 + n_sc, out_rows, into)
    if n_sc:
        y = _final(x, out_s, w["final_norm"], tb_sel, out_row0, out_rows, into=y)
    return y


def kernel(x_prompt, x_sample, mem_prompt, cache_conv_a, cache_conv_ssd, state_ssd, cache_mem_k, cache_mem_v, ln_mix_w, w_in, conv_a_w, conv_s_w, conv_s_b, dt_bias, a_log, d_skip, ssd_norm_w, w_out, ln_mem_w, mem_norm_w, w_mq, w_mk, w_mv, w_mo, ln_ffn_w, w_pq, sub_keys, u_tab, v_tab, final_norm_w):
    depth = w_in.shape[0]
    assert depth == 1
    l = 0
    n_main = 3 * D_CONV + D_SSD + XBC_DIM
    w_dt = w_in[l][:, n_main:]
    w = {
        "ln_mix": ln_mix_w[l].reshape(1, D_MODEL),
        "w_in": w_in[l][:, :n_main].astype(BF16),
        "w_dt": jnp.pad(w_dt, ((0, 0), (0, LANES - SSD_HEADS))).astype(BF16),
        "w_dtt": jnp.pad(w_dt.T, ((0, 2 * SUBLANES - SSD_HEADS), (0, 0))).astype(BF16),
        "conv_a_w": conv_a_w[l], "conv_s_w": conv_s_w[l], "conv_s_b": conv_s_b[l].reshape(1, XBC_DIM),
        "dtb_c": _col(dt_bias[l]), "alog_c": _col(a_log[l]), "dsk_c": _col(d_skip[l]),
        "dtb_r": _row(dt_bias[l]), "alog_r": _row(a_log[l]),
        "ssd_norm": ssd_norm_w[l].reshape(1, D_SSD),
        "w_out": w_out[l].astype(BF16),
        "ln_ffn": ln_ffn_w[l].reshape(1, D_MODEL),
        "w_pq": w_pq[l].astype(BF16).reshape(D_MODEL, PEER_HEADS, PEER_DK).transpose(1, 0, 2),
        "sub_keys": sub_keys[l].astype(BF16),
        "u_pack": _pack_table(u_tab[l]),
        "v_pack": _pack_table(v_tab[l]),
        "u_tab": u_tab[l], "v_tab": v_tab[l],
        "final_norm": final_norm_w.reshape(1, D_MODEL),
    }
    ln_mem = ln_mem_w[l].reshape(1, D_MODEL)
    wq, wo = w_mq[l].astype(BF16), w_mo[l].astype(BF16)

    bp, sp, _ = x_prompt.shape
    bs, ss, _ = x_sample.shape

    mk, mv = _memory_kv(mem_prompt.reshape(bp * N_MEM, D_MODEL), mem_norm_w[l].reshape(1, D_MODEL),
                        w_mk[l].astype(BF16), w_mv[l].astype(BF16))
    mk3, mv3 = mk.reshape(bp, N_MEM, D_MODEL), mv.reshape(bp, N_MEM, D_MODEL)
    early, states, b0, xn_prev = [], [], 0, None
    for nseq in SC_EARLY_GROUPS:
        hg, *st = _mixer(x_prompt, None, w, 256, b0, nseq, after=xn_prev)
        hg = _attn(hg, mk3, mv3, ln_mem, wq, wo, 256, kv0=b0).reshape(nseq * sp, D_MODEL)
        xn_g, eid_g, _, _, gate_g = _select(hg, w["ln_ffn"], w["w_pq"], w["sub_keys"], 256, 0, nseq * sp)
        early.append((b0 * sp, hg, xn_g, eid_g, gate_g, _sc_hidden(eid_g, xn_g, w["u_tab"])))
        states.append(st)
        b0, xn_prev = b0 + nseq, xn_g
    n1 = b0
    h2, *st = _mixer(x_prompt, None, w, 256, n1, bp - n1, after=xn_prev)
    states.append(st)
    h2 = _attn(h2, mk3, mv3, ln_mem, wq, wo, 256, kv0=n1).reshape((bp - n1) * sp, D_MODEL)

    hs, sa, ssd, sh = _mixer(x_sample, (cache_conv_a[l], cache_conv_ssd[l],
                                        state_ssd[l].reshape((bs,) + PAIR_STATE)), w, 256, after=h2)
    hs = _attn(hs, cache_mem_k[l], cache_mem_v[l], ln_mem, wq, wo, 256).reshape(bs * ss, D_MODEL)
    xn_q, _, tile_q, shift_q, gate_q = _select(hs, w["ln_ffn"], w["w_pq"], w["sub_keys"], 256, 0, bs * ss)

    y1, gh_g = None, None
    for g, (row0, hg, _, eid_g, gate_g, hid_g) in enumerate(early):
        waits_for = early[g + 1][2] if g + 1 < len(early) else xn_q
        gh_g = _gate_gelu(hid_g, gate_g, 256, after=waits_for)
        y1 = _final(hg, _sc_combine(eid_g, gh_g, w["v_tab"]), w["final_norm"], 256, row0, bp * sp, into=y1)
    y_prompt = _peer(h2, w, 256, 64, SC_PROMPT_TOKENS, n1 * sp, bp * sp, after=gh_g, into=y1).reshape(bp, sp, D_MODEL)
    pa, ps, ph = (jnp.concatenate(p, axis=0) for p in zip(*states))

    gh_q = _hidden(tile_q, shift_q, xn_q, gate_q, w["u_pack"], 64)
    y_sample = _combine(tile_q, shift_q, gh_q, w["v_pack"], hs, w["final_norm"], 64, 0, 0, bs * ss)
    y_sample = y_sample.reshape(bs, ss, D_MODEL)

    kv_shape = (1, bp, N_MEM, MEM_HEADS, MEM_HEAD_DIM)
    st_shape = (SSD_HEADS, SSD_HEAD_DIM, SSD_STATE)
    return (y_prompt, y_sample, pa[None], ps[None], ph.reshape((1, bp) + st_shape), mk.reshape(kv_shape),
            mv.reshape(kv_shape), sa[None], ssd[None], sh.reshape((1, bs) + st_shape))
```

```python
import functools

import jax
import jax.numpy as jnp
from jax import lax
from jax.experimental import pallas as pl
from jax.experimental.pallas import tpu as pltpu
from jax.experimental.pallas import tpu_sc as plsc

D_MODEL = 1024
CHUNK = 64
D_CONV = 512
CONV_A_W = 3
D_SSD = 512
SSD_HEAD_DIM = 64
SSD_HEADS = 8
SSD_GROUPS = 2
SSD_HPG = 4
SSD_STATE = 128
SSD_CONV_W = 4
XBC_DIM = 1024
N_MEM = 256
MEM_HEADS = 4
MEM_HEAD_DIM = 256
PEER_HEADS = 8
N_KEYS = 128
N_EXPERTS = N_KEYS * N_KEYS
PEER_TOPK = 16
PEER_DK = 256
PEER_DK_HALF = 128
PICKS = PEER_HEADS * PEER_TOPK
EPS = 1e-6

LANES = 128
SUBLANES = 8
HALF_EXPERTS = N_EXPERTS // 2
PAIR_STATE = (SSD_HEADS // 2, 2 * SSD_HEAD_DIM, SSD_STATE)
VMEM_LIMIT = 56 * 1024 * 1024

F32 = jnp.float32
BF16 = jnp.bfloat16
HI = lax.Precision.HIGHEST


def _rms(x, g):
    return x * lax.rsqrt(jnp.mean(x * x, axis=-1, keepdims=True) + EPS) * g


def _silu(x):
    return x * (1.0 / (1.0 + jnp.exp(-x)))


def _softplus(x):
    return jnp.maximum(x, 0.0) + jnp.log1p(jnp.exp(-jnp.abs(x)))


def _dot(a, b):
    return jnp.dot(a.astype(BF16), b.astype(BF16), preferred_element_type=F32)


def _dot_nt(a, b):
    return lax.dot_general(a.astype(BF16), b.astype(BF16), (((1,), (1,)), ((), ())),
                           preferred_element_type=F32)


def _dot_tn(a, b):
    return lax.dot_general(a.astype(BF16), b.astype(BF16), (((0,), (0,)), ((), ())),
                           preferred_element_type=F32)


def _const_spec(shape):
    n = len(shape)
    return pl.BlockSpec(shape, lambda *_: (0,) * n)


def _mixer_kernel(has_hist, has_after, ts, lc, *refs):
    if has_after:
        refs = refs[1:]
    if has_hist:
        x_ref, hista_ref, hists_ref, h0_ref = refs[:4]
        refs = refs[4:]
    else:
        x_ref = refs[0]
        refs = refs[1:]
    (lnw_ref, win_ref, wdt_ref, wdtt_ref, caw_ref, csw_ref, csb_ref, dtb_c_ref, alog_c_ref, dsk_c_ref,
     dtb_r_ref, alog_r_ref, nw_ref, wout_ref,
     h1_ref, na_ref, ns_ref, nh_ref, cata, cats, hst) = refs
    s = pl.program_id(1)
    pad = SUBLANES

    @pl.when(s == 0)
    def _():
        if has_hist:
            cata[pad - 2:pad, :] = hista_ref[...]
            cats[pad - 3:pad, :] = hists_ref[...]
            hst[...] = h0_ref[...]
        else:
            cata[0:pad, :] = jnp.zeros((pad, D_CONV), F32)
            cats[0:pad, :] = jnp.zeros((pad, XBC_DIM), F32)
            hst[...] = jnp.zeros(hst.shape, F32)

    x = x_ref[...]
    xn = _rms(x, lnw_ref[...]).astype(BF16)
    proj = jnp.dot(xn, win_ref[...], preferred_element_type=F32)
    dt_c = jnp.dot(xn, wdt_ref[...], preferred_element_type=F32)
    dt_r = lax.dot_general(wdtt_ref[...], xn, (((1,), (1,)), ((), ())),
                           preferred_element_type=F32)
    g_b = proj[:, 0:D_CONV]
    g_c = proj[:, D_CONV:2 * D_CONV]
    v_in = proj[:, 2 * D_CONV:3 * D_CONV]
    z = proj[:, 3 * D_CONV:3 * D_CONV + D_SSD]
    xbc = proj[:, 3 * D_CONV + D_SSD:3 * D_CONV + D_SSD + XBC_DIM]

    ua = g_c * v_in
    cata[pad:pad + ts, :] = ua
    caw = caw_ref[...]
    conv_a = (caw[0:1] * cata[pad - 2:pad - 2 + ts, :] + caw[1:2] * cata[pad - 1:pad - 1 + ts, :]
              + caw[2:3] * ua)
    y_a = g_b * conv_a
    hist_a = cata[pad + ts - 2:pad + ts, :]
    na_ref[...] = hist_a
    cata[pad - 2:pad, :] = hist_a

    cats[pad:pad + ts, :] = xbc
    csw = csw_ref[...]
    conv_s = (csw[0:1] * cats[pad - 3:pad - 3 + ts, :] + csw[1:2] * cats[pad - 2:pad - 2 + ts, :]
              + csw[2:3] * cats[pad - 1:pad - 1 + ts, :] + csw[3:4] * xbc)
    hist_s = cats[pad + ts - 3:pad + ts, :]
    ns_ref[...] = hist_s
    cats[pad - 3:pad, :] = hist_s
    xbc_c = _silu(conv_s + csb_ref[...])
    xs = xbc_c[:, 0:D_SSD]

    dtp_c = _softplus(dt_c + dtb_c_ref[...])
    dtp_r = _softplus(dt_r + dtb_r_ref[...])
    a_c = dtp_c * (-jnp.exp(alog_c_ref[...]))
    a_r = dtp_r * (-jnp.exp(alog_r_ref[...]))
    dsk_c = dsk_c_ref[...]

    lane = lax.broadcasted_iota(jnp.int32, (1, LANES), 1)
    first_head = lane < SSD_HEAD_DIM
    row2 = lax.broadcasted_iota(jnp.int32, (2 * SSD_HEAD_DIM, 1), 0) < SSD_HEAD_DIM
    li = lax.broadcasted_iota(jnp.int32, (lc, lc), 0)
    si = lax.broadcasted_iota(jnp.int32, (lc, lc), 1)
    causal = si <= li
    tril = causal.astype(F32)
    jrow = lax.broadcasted_iota(jnp.int32, (ts, lc), 0)
    scol = lax.broadcasted_iota(jnp.int32, (ts, lc), 1)

    def pair(col, h):
        return jnp.where(first_head, col[:, h:h + 1], col[:, h + 1:h + 2])

    y_chunks = []
    for c in range(ts // lc):
        r0 = c * lc
        rows = slice(r0, r0 + lc)
        acum_c = jnp.dot(tril, a_c[rows], preferred_element_type=F32, precision=HI)
        upper = ((jrow >= r0) & (jrow <= r0 + scol)).astype(F32)
        acum_r = jnp.dot(a_r, upper, preferred_element_type=F32, precision=HI)
        last_c = acum_c[lc - 1:lc, :]
        dec_c = jnp.exp(last_c - acum_c)
        eac_c = jnp.exp(acum_c)
        y_pairs = []
        for g in range(SSD_GROUPS):
            b_g = xbc_c[rows, D_SSD + g * SSD_STATE:D_SSD + (g + 1) * SSD_STATE]
            c_g = xbc_c[rows, D_SSD + (SSD_GROUPS + g) * SSD_STATE:D_SSD + (SSD_GROUPS + g + 1) * SSD_STATE]
            cb = _dot_nt(c_g, b_g)
            for q in range(SSD_HPG // 2):
                h = g * SSD_HPG + 2 * q
                pi = h // 2
                xs_p = xs[rows, pi * LANES:(pi + 1) * LANES]
                xdt = xs_p * pair(dtp_c[rows], h)
                res = []
                for hh in (h, h + 1):
                    seg = acum_c[:, hh:hh + 1] - acum_r[hh:hh + 1, :]
                    m_h = cb * jnp.exp(jnp.where(causal, seg, -jnp.inf))
                    res.append(_dot(m_h, xdt))
                y_diag = jnp.where(first_head, res[0], res[1])
                h_in = hst[pi]
                y_off = _dot_nt(c_g, h_in) * pair(eac_c, h)
                st = _dot_tn(xdt * pair(dec_c, h), b_g)
                cd = jnp.exp(jnp.where(row2, last_c[:, h:h + 1], last_c[:, h + 1:h + 2]))
                hst[pi] = cd * h_in + st
                y_pairs.append(y_diag + y_off + pair(dsk_c, h) * xs_p)
        y_chunks.append(jnp.concatenate(y_pairs, axis=1))
    y = y_chunks[0] if len(y_chunks) == 1 else jnp.concatenate(y_chunks, axis=0)
    y_b = _rms(y * _silu(z), nw_ref[...])
    wout = wout_ref[...]
    out = (jnp.dot(y_a.astype(BF16), wout[0:D_CONV], preferred_element_type=F32)
           + jnp.dot(y_b.astype(BF16), wout[D_CONV:], preferred_element_type=F32))
    h1_ref[...] = x + out
    nh_ref[...] = hst[...]


def _mixer(x, hist, w, ts, b0=0, nb=None, after=None):
    _, s, _ = x.shape
    b = x.shape[0] if nb is None else nb
    lc = min(CHUNK, s)
    ts = min(ts, s)
    has_hist = hist is not None
    assert not (has_hist and b0)
    per_b3 = lambda shape: pl.BlockSpec((None,) + shape, lambda i, j: (i,) + (0,) * len(shape))
    in_specs = [pl.BlockSpec((None, ts, D_MODEL), lambda i, j: (i + b0, j, 0))]
    args = [x]
    if after is not None:
        in_specs.insert(0, pl.BlockSpec((SUBLANES, after.shape[1]), lambda i, j: (0, 0)))
        args.insert(0, after)
    if has_hist:
        in_specs += [per_b3((CONV_A_W - 1, D_CONV)), per_b3((SSD_CONV_W - 1, XBC_DIM)),
                     per_b3(PAIR_STATE)]
        args += list(hist)
    wargs = [w["ln_mix"], w["w_in"], w["w_dt"], w["w_dtt"], w["conv_a_w"], w["conv_s_w"], w["conv_s_b"],
             w["dtb_c"], w["alog_c"], w["dsk_c"], w["dtb_r"], w["alog_r"], w["ssd_norm"], w["w_out"]]
    in_specs += [_const_spec(a.shape) for a in wargs]
    args += wargs
    out_shape = (jax.ShapeDtypeStruct((b, s, D_MODEL), F32),
                 jax.ShapeDtypeStruct((b, CONV_A_W - 1, D_CONV), F32),
                 jax.ShapeDtypeStruct((b, SSD_CONV_W - 1, XBC_DIM), F32),
                 jax.ShapeDtypeStruct((b,) + PAIR_STATE, F32))
    out_specs = (pl.BlockSpec((None, ts, D_MODEL), lambda i, j: (i, j, 0)),
                 per_b3((CONV_A_W - 1, D_CONV)), per_b3((SSD_CONV_W - 1, XBC_DIM)),
                 per_b3(PAIR_STATE))
    return pl.pallas_call(
        functools.partial(_mixer_kernel, has_hist, after is not None, ts, lc),
        out_shape=out_shape, grid=(b, s // ts), in_specs=in_specs, out_specs=out_specs,
        scratch_shapes=[pltpu.VMEM((SUBLANES + ts, D_CONV), F32), pltpu.VMEM((SUBLANES + ts, XBC_DIM), F32),
                        pltpu.VMEM(PAIR_STATE, F32)],
        compiler_params=pltpu.CompilerParams(dimension_semantics=("parallel", "arbitrary"),
                                             vmem_limit_bytes=VMEM_LIMIT),
        name="mixer_hist" if has_hist else "mixer",
    )(*args)


def _memkv_kernel(m_ref, g_ref, wk_ref, wv_ref, k_ref, v_ref):
    mn = _rms(m_ref[...], g_ref[...]).astype(BF16)
    k_ref[...] = jnp.dot(mn, wk_ref[...], preferred_element_type=F32)
    v_ref[...] = jnp.dot(mn, wv_ref[...], preferred_element_type=F32)


def _memory_kv(mem, g, wk, wv, tm=256):
    t = mem.shape[0]
    blk = pl.BlockSpec((tm, D_MODEL), lambda i: (i, 0))
    return pl.pallas_call(
        _memkv_kernel,
        out_shape=(jax.ShapeDtypeStruct((t, D_MODEL), F32),) * 2,
        grid=(t // tm,), in_specs=[blk, _const_spec(g.shape), _const_spec(wk.shape), _const_spec(wv.shape)],
        out_specs=(blk, blk),
        compiler_params=pltpu.CompilerParams(dimension_semantics=("parallel",), vmem_limit_bytes=VMEM_LIMIT),
        name="memory_kv",
    )(mem, g, wk, wv)


def _attn_kernel(x_ref, k_ref, v_ref, g_ref, wq_ref, wo_ref, o_ref):
    x = x_ref[...]
    xn = _rms(x, g_ref[...]).astype(BF16)
    q = jnp.dot(xn, wq_ref[...], preferred_element_type=F32)
    by_head = len(k_ref.shape) == 3
    wo = wo_ref[...]
    acc = x
    for h in range(MEM_HEADS):
        cols = slice(h * MEM_HEAD_DIM, (h + 1) * MEM_HEAD_DIM)
        k_h = k_ref[:, h, :] if by_head else k_ref[:, cols]
        v_h = v_ref[:, h, :] if by_head else v_ref[:, cols]
        sc = _dot_nt(q[:, cols], k_h) * (MEM_HEAD_DIM ** -0.5)
        sc = sc - jnp.max(sc, axis=-1, keepdims=True)
        p = jnp.exp(sc)
        p = p / jnp.sum(p, axis=-1, keepdims=True)
        o_h = _dot(p, v_h)
        acc = acc + jnp.dot(o_h.astype(BF16), wo[cols, :], preferred_element_type=F32)
    o_ref[...] = acc


def _attn(x, k, v, g, wq, wo, ts, kv0=0):
    b, s, _ = x.shape
    ts = min(ts, s)
    xblk = pl.BlockSpec((None, ts, D_MODEL), lambda i, j: (i, j, 0))
    if k.ndim == 3:
        kvblk = pl.BlockSpec((None, N_MEM, D_MODEL), lambda i, j: (i + kv0, 0, 0))
    else:
        kvblk = pl.BlockSpec((None, N_MEM, MEM_HEADS, MEM_HEAD_DIM), lambda i, j: (i + kv0, 0, 0, 0))
    return pl.pallas_call(
        _attn_kernel,
        out_shape=jax.ShapeDtypeStruct((b, s, D_MODEL), F32),
        grid=(b, s // ts),
        in_specs=[xblk, kvblk, kvblk, _const_spec(g.shape), _const_spec(wq.shape), _const_spec(wo.shape)],
        out_specs=xblk,
        compiler_params=pltpu.CompilerParams(dimension_semantics=("parallel", "parallel"),
                                             vmem_limit_bytes=VMEM_LIMIT),
        name="attn",
    )(x, k, v, g, wq, wo)


def _tree_reduce(op, x):
    n = x.shape[0]
    while n > 1:
        half = n // 2
        head = op(x[:half], x[half:2 * half])
        x = head if n % 2 == 0 else jnp.concatenate([head, x[2 * half:]], axis=0)
        n = x.shape[0]
    return x


def _top_k_rows(sc, k, payload=None):
    n = sc.shape[0]
    rows = lax.broadcasted_iota(jnp.int32, sc.shape, 0).astype(F32)
    vals, outs = [], []
    for _ in range(k):
        m = _tree_reduce(jnp.maximum, sc)
        cand = jnp.where(sc == m, rows, float(n))
        first = _tree_reduce(jnp.minimum, cand)
        hit = cand == first
        sc = jnp.where(hit, -jnp.inf, sc)
        vals.append(m)
        if payload is None:
            outs.append(first.astype(jnp.int32))
        else:
            outs.append(jnp.sum(jnp.where(hit, payload, 0), axis=0, keepdims=True))
    return jnp.concatenate(vals, axis=0), jnp.concatenate(outs, axis=0)


def _pair_candidates(v1, i1, v2, i2):
    comb = [v1[a:a + 1] + v2[0:PEER_TOPK // (a + 1)] for a in range(PEER_TOPK)]
    cidx = [i1[a:a + 1] * N_KEYS + i2[0:PEER_TOPK // (a + 1)] for a in range(PEER_TOPK)]
    return jnp.concatenate(comb, axis=0), jnp.concatenate(cidx, axis=0)


SELECT_BLOCK = SUBLANES * LANES


def _select_kernel(tb, x_ref, g_ref, wq_ref, keys_ref, *rest):
    xn_ref, eid_ref, tile_ref, shift_ref, gate_ref, s1, s2, eid_s, gate_s = rest[-9:]
    groups = tb // LANES
    assert groups == SUBLANES
    xn = _rms(x_ref[...], g_ref[...])
    xn_ref[...] = xn
    xn = xn.astype(BF16)

    def head(h, carry):
        q = jnp.dot(xn, wq_ref[h], preferred_element_type=F32)
        for g in range(groups):
            qg = q[g * LANES:(g + 1) * LANES]
            s1[pl.ds(g, N_KEYS, stride=SUBLANES), :] = _dot_nt(keys_ref[h, 0], qg[:, 0:PEER_DK_HALF])
            s2[pl.ds(g, N_KEYS, stride=SUBLANES), :] = _dot_nt(keys_ref[h, 1], qg[:, PEER_DK_HALF:PEER_DK])
        v1, i1 = _top_k_rows(s1[...].reshape(N_KEYS, SUBLANES, LANES), PEER_TOPK)
        v2, i2 = _top_k_rows(s2[...].reshape(N_KEYS, SUBLANES, LANES), PEER_TOPK)
        comb, cidx = _pair_candidates(v1, i1, v2, i2)
        top, eid = _top_k_rows(comb, PEER_TOPK, cidx)
        ex = jnp.exp(top - top[0:1])
        gate = ex / jnp.sum(ex, axis=0, keepdims=True)
        rows = pl.ds(pl.multiple_of(h * PEER_TOPK * SUBLANES, PEER_TOPK * SUBLANES), PEER_TOPK * SUBLANES)
        gate_s[rows, :] = gate.reshape(PEER_TOPK * SUBLANES, LANES)
        eid_s[rows, :] = eid.reshape(PEER_TOPK * SUBLANES, LANES)
        return carry

    lax.fori_loop(0, PEER_HEADS, head, 0)
    for g in range(groups):
        toks = slice(g * LANES, (g + 1) * LANES)
        eid = eid_s[pl.ds(g, PICKS, stride=SUBLANES), :].T
        upper = eid >= HALF_EXPERTS
        eid_ref[toks, :] = eid
        tile_ref[toks, :] = jnp.where(upper, eid - HALF_EXPERTS, eid) * SUBLANES
        shift_ref[toks, :] = jnp.where(upper, 0, 16)
        gate_ref[toks, :] = gate_s[pl.ds(g, PICKS, stride=SUBLANES), :].T


def _select(x, g, wq, keys, tb, t0, t, after=None):
    b0 = t0 // tb
    extra = [] if after is None else [after]
    xin = pl.BlockSpec((tb, D_MODEL), lambda i: (i + b0, 0))
    xblk = pl.BlockSpec((tb, D_MODEL), lambda i: (i, 0))
    picks_i = jax.ShapeDtypeStruct((t, PICKS), jnp.int32)
    return pl.pallas_call(
        functools.partial(_select_kernel, tb),
        out_shape=(jax.ShapeDtypeStruct((t, D_MODEL), F32), picks_i, picks_i, picks_i,
                   jax.ShapeDtypeStruct((t, PICKS), F32)),
        grid=(t // tb,),
        in_specs=[xin, _const_spec(g.shape), _const_spec(wq.shape), _const_spec(keys.shape)]
        + [_after_spec(a) for a in extra],
        out_specs=(xblk,) + (_pick_block(tb),) * 4,
        scratch_shapes=[pltpu.VMEM((N_KEYS * SUBLANES, LANES), F32), pltpu.VMEM((N_KEYS * SUBLANES, LANES), F32),
                        pltpu.VMEM((PICKS * SUBLANES, LANES), jnp.int32), pltpu.VMEM((PICKS * SUBLANES, LANES), F32)],
        compiler_params=pltpu.CompilerParams(dimension_semantics=("parallel",), vmem_limit_bytes=VMEM_LIMIT),
        name="peer_select",
    )(x, g, wq, keys, *extra)


HI_HALF_MASK = -(1 << 16)


def _unpack(word, shift):
    return lax.bitcast_convert_type((word << shift) & HI_HALF_MASK, F32)


def _splat_picks(blk, one_pass):
    tb = blk.shape[0]
    r = lax.broadcasted_iota(jnp.int32, (PICKS, LANES), 0)
    c = lax.broadcasted_iota(jnp.int32, (PICKS, LANES), 1)
    eye = (r == c)[None]
    ones = jnp.ones((LANES, LANES), BF16)

    def one(piece):
        diag = jnp.where(eye, piece[:, None, :], 0.0).astype(BF16).reshape(tb * PICKS, LANES)
        return jnp.dot(diag, ones, preferred_element_type=F32)

    if one_pass:
        out = one(blk)
    else:
        hi = blk.astype(BF16).astype(F32)
        rest = blk - hi
        mid = rest.astype(BF16).astype(F32)
        out = one(hi) + one(mid) + one(rest - mid)
    return out.reshape(tb, PICKS, LANES)


def _full(ref, k):
    return jnp.broadcast_to(ref[k:k + 1, :], (SUBLANES, LANES))


def _expert_row(tab_ref, tile_ref, t, k, shifts):
    idx = pl.multiple_of(tile_ref[t, k], SUBLANES)
    return _unpack(tab_ref[pl.ds(idx, SUBLANES), :], _full(shifts, k))


def _hidden_kernel(tb, tile_ref, shift_ref, x_ref, gate_ref, tab_ref, *rest):
    gh_ref, bsh, stage, parts, xt = rest[-5:]
    bsh[...] = _splat_picks(shift_ref[...].astype(F32), True).astype(jnp.int32)
    for s in range(SUBLANES):
        xt[pl.ds(s, tb, stride=SUBLANES), :] = x_ref[:, s * LANES:(s + 1) * LANES]

    half = PICKS // 2
    quad = SUBLANES // 2
    low = lax.broadcasted_iota(jnp.int32, (SUBLANES, LANES), 0) < quad

    def token(t, carry):
        x = xt[pl.ds(pl.multiple_of(t * SUBLANES, SUBLANES), SUBLANES), :]
        shifts = bsh.at[t]
        for j in range(half):
            a = x * _expert_row(tab_ref, tile_ref, t, j, shifts)
            b = x * _expert_row(tab_ref, tile_ref, t, j + half, shifts)
            stage[j * SUBLANES:(j + 1) * SUBLANES, :] = jnp.where(low, a + pltpu.roll(a, quad, 0),
                                                                  b + pltpu.roll(b, quad, 0))
        for p in range(2):
            part = stage[pl.ds(p * quad, half, stride=SUBLANES), :]
            for s in range(1, quad):
                part = part + stage[pl.ds(p * quad + s, half, stride=SUBLANES), :]
            parts[t, p * half:(p + 1) * half, :] = part
        return carry

    lax.fori_loop(0, tb, token, 0)
    gh_ref[...] = jnp.sum(parts[...], axis=-1)
    hid = gh_ref[...]
    gelu = 0.5 * hid * (1.0 + lax.erf(hid * (2.0 ** -0.5)))
    gh_ref[...] = gate_ref[...] * gelu


def _pick_block(tb, space=None):
    return pl.BlockSpec((tb, PICKS), lambda i: (i, 0), memory_space=space)


def _table_spec(tab):
    return pl.BlockSpec(tab.shape, lambda i: (0, 0), pipeline_mode=pl.Buffered(1))


def _hidden(tile, shift, xg, gate, tab, tb, after=None):
    t = xg.shape[0]
    extra = [] if after is None else [after]
    return pl.pallas_call(
        functools.partial(_hidden_kernel, tb),
        out_shape=jax.ShapeDtypeStruct((t, PICKS), F32),
        grid=(t // tb,),
        in_specs=[_pick_block(tb, pltpu.SMEM), _pick_block(tb),
                  pl.BlockSpec((tb, D_MODEL), lambda i: (i, 0)),
                  _pick_block(tb), _table_spec(tab)] + [_after_spec(a) for a in extra],
        out_specs=_pick_block(tb),
        scratch_shapes=[pltpu.VMEM((tb, PICKS, LANES), jnp.int32), pltpu.VMEM((PICKS // 2 * SUBLANES, LANES), F32),
                        pltpu.VMEM((tb, PICKS, LANES), F32), pltpu.VMEM((tb * SUBLANES, LANES), F32)],
        compiler_params=pltpu.CompilerParams(dimension_semantics=("parallel",), vmem_limit_bytes=VMEM_LIMIT),
        name="peer_hidden",
    )(tile, shift, xg, gate, tab, *extra)


def _combine_kernel(tb, tile_ref, shift_ref, gh_ref, tab_ref, x_ref, g_ref, *rest):
    o_ref, bsh, bg, ot = rest[-4:]
    bsh[...] = _splat_picks(shift_ref[...].astype(F32), True).astype(jnp.int32)
    bg[...] = _splat_picks(gh_ref[...], False)

    def token(t, carry):
        shifts, gates = bsh.at[t], bg.at[t]
        acc = [jnp.zeros((SUBLANES, LANES), F32) for _ in range(4)]
        for k in range(PICKS):
            acc[k % 4] = acc[k % 4] + _full(gates, k) * _expert_row(tab_ref, tile_ref, t, k, shifts)
        ot[pl.ds(pl.multiple_of(t * SUBLANES, SUBLANES), SUBLANES), :] = (acc[0] + acc[1]) + (acc[2] + acc[3])
        return carry

    lax.fori_loop(0, tb, token, 0)
    for s in range(SUBLANES):
        o_ref[:, s * LANES:(s + 1) * LANES] = ot[pl.ds(s, tb, stride=SUBLANES), :]
    o_ref[...] = _rms(x_ref[...] + o_ref[...], g_ref[...])


def _combine(tile, shift, gh, tab, x, g, tb, x_row0, out_row0, out_rows, into=None):
    t = gh.shape[0]
    xb0, ob0 = x_row0 // tb, out_row0 // tb
    extra = [] if into is None else [into]
    return pl.pallas_call(
        functools.partial(_combine_kernel, tb),
        out_shape=jax.ShapeDtypeStruct((out_rows, D_MODEL), F32),
        grid=(t // tb,),
        in_specs=[_pick_block(tb, pltpu.SMEM), _pick_block(tb), _pick_block(tb), _table_spec(tab),
                  pl.BlockSpec((tb, D_MODEL), lambda i: (i + xb0, 0)), _const_spec(g.shape)]
        + [pl.BlockSpec(memory_space=pl.ANY) for _ in extra],
        input_output_aliases={6: 0} if extra else {},
        out_specs=pl.BlockSpec((tb, D_MODEL), lambda i: (i + ob0, 0)),
        scratch_shapes=[pltpu.VMEM((tb, PICKS, LANES), jnp.int32), pltpu.VMEM((tb, PICKS, LANES), F32),
                        pltpu.VMEM((tb * SUBLANES, LANES), F32)],
        compiler_params=pltpu.CompilerParams(dimension_semantics=("parallel",), vmem_limit_bytes=VMEM_LIMIT),
        name="peer_combine",
    )(tile, shift, gh, tab, x, g, *extra)


SC_LANES = 16
SC_WORKERS = 32
SC_CHUNK = 32
SC_FEATURE_BLOCK = 256
SC_EARLY_SEQS = 3
SC_PROMPT_TOKENS = 13312


def _sc_call(body, n, side_width, out_width):
    mesh = plsc.VectorSubcoreMesh(core_axis_name="c", subcore_axis_name="s")
    return pl.kernel(body, mesh=mesh, out_type=jax.ShapeDtypeStruct((n, out_width), F32),
                     compiler_params=pltpu.CompilerParams(needs_layout_passes=False),
                     scratch_types=[pltpu.VMEM((2 * PICKS,), jnp.int32), pltpu.VMEM((2 * side_width,), F32),
                                    pltpu.VMEM((2 * out_width,), F32),
                                    pltpu.VMEM((SC_CHUNK, D_MODEL), F32), pltpu.VMEM((SC_CHUNK, D_MODEL), F32)]
                     + [pltpu.SemaphoreType.DMA] * 5)


def _sc_token_loop(n, eid_hbm, side_hbm, tab_hbm, out_hbm, idx_v, side_v, out_v, bufs, sems, compute):
    sem_rows, sem_fetch, sem_put = sems[0:2], sems[2], sems[3:5]
    per_worker = n // SC_WORKERS
    first = (lax.axis_index("s") * 2 + lax.axis_index("c")) * per_worker
    last = first + per_worker - 1
    chunks = PICKS // SC_CHUNK
    side_w = side_v.shape[0] // 2
    out_w = out_v.shape[0] // 2

    def fetch(t, slot):
        return (pltpu.make_async_copy(eid_hbm.at[t], idx_v.at[pl.ds(slot * PICKS, PICKS)], sem_fetch),
                pltpu.make_async_copy(side_hbm.at[t], side_v.at[pl.ds(slot * side_w, side_w)], sem_fetch))

    def gather(slot, c):
        ids = idx_v.at[pl.ds(slot * PICKS + c * SC_CHUNK, SC_CHUNK)]
        return pltpu.make_async_copy(tab_hbm.at[ids], bufs[c % 2], sem_rows[c % 2])

    def put(t, slot):
        return pltpu.make_async_copy(out_v.at[pl.ds(slot * out_w, out_w)], out_hbm.at[t], sem_put[slot])

    for cp in fetch(first, 0):
        cp.start()
    for cp in fetch(first, 0):
        cp.wait()
    gather(0, 0).start()
    gather(0, 1).start()

    @pl.loop(0, per_worker, step=2)
    def _(i):
        for slot in range(2):
            t = first + i + slot
            nxt = jnp.minimum(t + 1, last)
            for cp in fetch(nxt, 1 - slot):
                cp.start()

            @pl.when(i >= 2)
            def _():
                put(t, slot).wait()

            for c in range(chunks):
                gather(slot, c).wait()
                compute(c, bufs[c % 2], slot * side_w, slot * out_w)
                if c + 2 < chunks:
                    gather(slot, c + 2).start()
                else:
                    if c + 2 == chunks:
                        for cp in fetch(nxt, 1 - slot):
                            cp.wait()
                    gather(1 - slot, c + 2 - chunks).start()
            put(t, slot).start()

    gather(0, 0).wait()
    gather(0, 1).wait()
    put(last, 0).wait()
    put(last, 1).wait()


def _sc_hidden(eid, x, tab):
    n = eid.shape[0]
    lanes = SC_LANES

    def body(eid_hbm, x_hbm, tab_hbm, out_hbm, idx_v, x_v, hid_v, rows0, rows1, *sems):
        lane = lax.iota(jnp.int32, lanes)

        def compute(c, rows, x0, h0):
            for g in range(SC_CHUNK // lanes):
                def step(j, accs):
                    off = pl.multiple_of(j * lanes, lanes)
                    xj = x_v[pl.ds(x0 + off, lanes)]
                    return tuple(accs[r] + xj * rows[g * lanes + r, pl.ds(off, lanes)] for r in range(lanes))
                accs = lax.fori_loop(0, D_MODEL // lanes, step,
                                     tuple(jnp.zeros((lanes,), F32) for _ in range(lanes)))
                hv = jnp.zeros((lanes,), F32)
                for r in range(lanes):
                    hv = jnp.where(lane == r, jnp.sum(accs[r]), hv)
                hid_v[pl.ds(h0 + c * SC_CHUNK + g * lanes, lanes)] = hv

        _sc_token_loop(n, eid_hbm, x_hbm, tab_hbm, out_hbm, idx_v, x_v, hid_v, (rows0, rows1), sems, compute)

    return _sc_call(body, n, D_MODEL, PICKS)(eid, x, tab)


def _sc_combine(eid, gh, tab):
    n = eid.shape[0]
    lanes = SC_LANES
    nacc = SC_FEATURE_BLOCK // lanes

    def body(eid_hbm, g_hbm, tab_hbm, out_hbm, idx_v, g_v, out_v, rows0, rows1, *sems):
        def compute(c, rows, g0, o0):
            for fb in range(D_MODEL // SC_FEATURE_BLOCK):
                f0 = fb * SC_FEATURE_BLOCK
                if c == 0:
                    init = tuple(jnp.zeros((lanes,), F32) for _ in range(nacc))
                else:
                    init = tuple(out_v[pl.ds(o0 + f0 + q * lanes, lanes)] for q in range(nacc))

                def step(r, accs):
                    gk = plsc.load_gather(g_v, [jnp.full((lanes,), g0 + c * SC_CHUNK + r, jnp.int32)])
                    return tuple(accs[q] + gk * rows[r, pl.ds(f0 + q * lanes, lanes)] for q in range(nacc))
                accs = lax.fori_loop(0, SC_CHUNK, step, init)
                for q in range(nacc):
                    out_v[pl.ds(o0 + f0 + q * lanes, lanes)] = accs[q]

        _sc_token_loop(n, eid_hbm, g_hbm, tab_hbm, out_hbm, idx_v, g_v, out_v, (rows0, rows1), sems, compute)

    return _sc_call(body, n, PICKS, D_MODEL)(eid, gh, tab)


def _after_spec(after):
    return pl.BlockSpec((SUBLANES, after.shape[1]), lambda i: (0, 0))


def _gate_gelu_kernel(h_ref, g_ref, after_ref, o_ref):
    hid = h_ref[...]
    o_ref[...] = g_ref[...] * (0.5 * hid * (1.0 + lax.erf(hid * (2.0 ** -0.5))))


def _gate_gelu(hid, gate, tb, after):
    t = hid.shape[0]
    return pl.pallas_call(
        _gate_gelu_kernel, out_shape=jax.ShapeDtypeStruct((t, PICKS), F32), grid=(t // tb,),
        in_specs=[_pick_block(tb), _pick_block(tb), _after_spec(after)], out_specs=_pick_block(tb),
        compiler_params=pltpu.CompilerParams(dimension_semantics=("parallel",)),
        name="gate_gelu",
    )(hid, gate, after)


def _final_kernel(x_ref, p_ref, g_ref, *rest):
    rest[-1][...] = _rms(x_ref[...] + p_ref[...], g_ref[...])


def _final(x, p, g, tb, out_row0, out_rows, into=None):
    b0 = out_row0 // tb
    local = pl.BlockSpec((tb, D_MODEL), lambda i: (i, 0))
    in_specs = [local, local, _const_spec(g.shape)]
    args = [x, p, g]
    aliases = {}
    if into is not None:
        aliases = {len(args): 0}
        in_specs.append(pl.BlockSpec(memory_space=pl.ANY))
        args.append(into)
    return pl.pallas_call(
        _final_kernel, out_shape=jax.ShapeDtypeStruct((out_rows, D_MODEL), F32), grid=(p.shape[0] // tb,),
        in_specs=in_specs, out_specs=pl.BlockSpec((tb, D_MODEL), lambda i: (i + b0, 0)),
        input_output_aliases=aliases,
        compiler_params=pltpu.CompilerParams(dimension_semantics=("parallel",), vmem_limit_bytes=VMEM_LIMIT),
        name="final_norm",
    )(*args)


PACK_BLOCK = 256


def _pack_kernel(lo_ref, hi_ref, o_ref):
    def bf16_bits(x):
        return lax.bitcast_convert_type(x.astype(BF16).astype(F32), jnp.int32)
    word = bf16_bits(hi_ref[...]) | lax.shift_right_logical(bf16_bits(lo_ref[...]), 16)
    for s in range(SUBLANES):
        o_ref[pl.ds(s, PACK_BLOCK, stride=SUBLANES), :] = word[:, s * LANES:(s + 1) * LANES]


def _pack_table(tab):
    steps = HALF_EXPERTS // PACK_BLOCK
    return pl.pallas_call(
        _pack_kernel, out_shape=jax.ShapeDtypeStruct((HALF_EXPERTS * SUBLANES, LANES), jnp.int32), grid=(steps,),
        in_specs=[pl.BlockSpec((PACK_BLOCK, D_MODEL), lambda i: (i, 0)),
                  pl.BlockSpec((PACK_BLOCK, D_MODEL), lambda i: (i + steps, 0))],
        out_specs=pl.BlockSpec((PACK_BLOCK * SUBLANES, LANES), lambda i: (i, 0)),
        compiler_params=pltpu.CompilerParams(dimension_semantics=("parallel",)),
        name="pack_table",
    )(tab, tab)


def _col(v, n=LANES):
    return jnp.pad(v.astype(F32), (0, n - v.shape[0])).reshape(1, n)


def _row(v, n=2 * SUBLANES):
    return jnp.pad(v.astype(F32), (0, n - v.shape[0])).reshape(n, 1)


def _peer(x, w, tb_sel, tb_exp, n_sc, out_row0=0, out_rows=None, after=None, into=None):
    t = x.shape[0]
    out_rows = t if out_rows is None else out_rows
    n_tc = t - n_sc
    if n_sc:
        xn_s, eid_s, _, _, gate_s = _select(x, w["ln_ffn"], w["w_pq"], w["sub_keys"], SELECT_BLOCK,0, n_sc, after=after)
        hid_s = _sc_hidden(eid_s, xn_s, w["u_tab"])
    xn, _, tile, shift, gate = _select(x, w["ln_ffn"], w["w_pq"], w["sub_keys"], SELECT_BLOCK,n_sc, n_tc,
                                       after=xn_s if n_sc else None)
    gh = _hidden(tile, shift, xn, gate, w["u_pack"], tb_exp, after=into)
    if n_sc:
        out_s = _sc_combine(eid_s, _gate_gelu(hid_s, gate_s, tb_sel, after=gh), w["v_tab"])
    y = _combine(tile, shift, gh, w["v_pack"], x, w["final_norm"], tb_exp, n_sc, out_row0 + n_sc, out_rows, into)
    if n_sc:
        y = _final(x, out_s, w["final_norm"], tb_sel, out_row0, out_rows, into=y)
    return y


def kernel(x_prompt, x_sample, mem_prompt, cache_conv_a, cache_conv_ssd, state_ssd, cache_mem_k, cache_mem_v, ln_mix_w, w_in, conv_a_w, conv_s_w, conv_s_b, dt_bias, a_log, d_skip, ssd_norm_w, w_out, ln_mem_w, mem_norm_w, w_mq, w_mk, w_mv, w_mo, ln_ffn_w, w_pq, sub_keys, u_tab, v_tab, final_norm_w):
    depth = w_in.shape[0]
    assert depth == 1
    l = 0
    n_main = 3 * D_CONV + D_SSD + XBC_DIM
    w_dt = w_in[l][:, n_main:]
    w = {
        "ln_mix": ln_mix_w[l].reshape(1, D_MODEL),
        "w_in": w_in[l][:, :n_main].astype(BF16),
        "w_dt": jnp.pad(w_dt, ((0, 0), (0, LANES - SSD_HEADS))).astype(BF16),
        "w_dtt": jnp.pad(w_dt.T, ((0, 2 * SUBLANES - SSD_HEADS), (0, 0))).astype(BF16),
        "conv_a_w": conv_a_w[l], "conv_s_w": conv_s_w[l], "conv_s_b": conv_s_b[l].reshape(1, XBC_DIM),
        "dtb_c": _col(dt_bias[l]), "alog_c": _col(a_log[l]), "dsk_c": _col(d_skip[l]),
        "dtb_r": _row(dt_bias[l]), "alog_r": _row(a_log[l]),
        "ssd_norm": ssd_norm_w[l].reshape(1, D_SSD),
        "w_out": w_out[l].astype(BF16),
        "ln_ffn": ln_ffn_w[l].reshape(1, D_MODEL),
        "w_pq": w_pq[l].astype(BF16).reshape(D_MODEL, PEER_HEADS, PEER_DK).transpose(1, 0, 2),
        "sub_keys": sub_keys[l].astype(BF16),
        "u_pack": _pack_table(u_tab[l]),
        "v_pack": _pack_table(v_tab[l]),
        "u_tab": u_tab[l], "v_tab": v_tab[l],
        "final_norm": final_norm_w.reshape(1, D_MODEL),
    }
    ln_mem = ln_mem_w[l].reshape(1, D_MODEL)
    wq, wo = w_mq[l].astype(BF16), w_mo[l].astype(BF16)

    bp, sp, _ = x_prompt.shape
    bs, ss, _ = x_sample.shape

    mk, mv = _memory_kv(mem_prompt.reshape(bp * N_MEM, D_MODEL), mem_norm_w[l].reshape(1, D_MODEL),
                        w_mk[l].astype(BF16), w_mv[l].astype(BF16))
    mk3, mv3 = mk.reshape(bp, N_MEM, D_MODEL), mv.reshape(bp, N_MEM, D_MODEL)
    n1 = SC_EARLY_SEQS
    h1, pa1, ps1, ph1 = _mixer(x_prompt, None, w, 256, 0, n1)
    h1 = _attn(h1, mk3, mv3, ln_mem, wq, wo, 256).reshape(n1 * sp, D_MODEL)

    xn1, eid1, _, _, gate1 = _select(h1, w["ln_ffn"], w["w_pq"], w["sub_keys"], SELECT_BLOCK,0, n1 * sp)
    hid1 = _sc_hidden(eid1, xn1, w["u_tab"])
    h2, pa2, ps2, ph2 = _mixer(x_prompt, None, w, 256, n1, bp - n1, after=xn1)
    h2 = _attn(h2, mk3, mv3, ln_mem, wq, wo, 256, kv0=n1).reshape((bp - n1) * sp, D_MODEL)

    hs, sa, ssd, sh = _mixer(x_sample, (cache_conv_a[l], cache_conv_ssd[l],
                                        state_ssd[l].reshape((bs,) + PAIR_STATE)), w, 256, after=h2)
    hs = _attn(hs, cache_mem_k[l], cache_mem_v[l], ln_mem, wq, wo, 256).reshape(bs * ss, D_MODEL)
    xn_q, _, tile_q, shift_q, gate_q = _select(hs, w["ln_ffn"], w["w_pq"], w["sub_keys"], SELECT_BLOCK,0, bs * ss)

    gh1 = _gate_gelu(hid1, gate1, 256, after=xn_q)
    out1 = _sc_combine(eid1, gh1, w["v_tab"])
    y1 = _final(h1, out1, w["final_norm"], 256, 0, bp * sp)
    y_prompt = _peer(h2, w, 256, 64, SC_PROMPT_TOKENS, n1 * sp, bp * sp, after=gh1, into=y1).reshape(bp, sp, D_MODEL)
    pa, ps, ph = (jnp.concatenate(p, axis=0) for p in ((pa1, pa2), (ps1, ps2), (ph1, ph2)))

    gh_q = _hidden(tile_q, shift_q, xn_q, gate_q, w["u_pack"], 64)
    y_sample = _combine(tile_q, shift_q, gh_q, w["v_pack"], hs, w["final_norm"], 64, 0, 0, bs * ss)
    y_sample = y_sample.reshape(bs, ss, D_MODEL)

    kv_shape = (1, bp, N_MEM, MEM_HEADS, MEM_HEAD_DIM)
    st_shape = (SSD_HEADS, SSD_HEAD_DIM, SSD_STATE)
    return (y_prompt, y_sample, pa[None], ps[None], ph.reshape((1, bp) + st_shape), mk.reshape(kv_shape),
            mv.reshape(kv_shape), sa[None], ssd[None], sh.reshape((1, bs) + st_shape))
```

```python
import functools

import jax
import jax.numpy as jnp
from jax import lax
from jax.experimental import pallas as pl
from jax.experimental.pallas import tpu as pltpu
from jax.experimental.pallas import tpu_sc as plsc

D_MODEL = 1024
CHUNK = 64
D_CONV = 512
CONV_A_W = 3
D_SSD = 512
SSD_HEAD_DIM = 64
SSD_HEADS = 8
SSD_GROUPS = 2
SSD_HPG = 4
SSD_STATE = 128
SSD_CONV_W = 4
XBC_DIM = 1024
N_MEM = 256
MEM_HEADS = 4
MEM_HEAD_DIM = 256
PEER_HEADS = 8
N_KEYS = 128
N_EXPERTS = N_KEYS * N_KEYS
PEER_TOPK = 16
PEER_DK = 256
PEER_DK_HALF = 128
PICKS = PEER_HEADS * PEER_TOPK
EPS = 1e-6

LANES = 128
SUBLANES = 8
HALF_EXPERTS = N_EXPERTS // 2
PAIR_STATE = (SSD_HEADS // 2, 2 * SSD_HEAD_DIM, SSD_STATE)
VMEM_LIMIT = 56 * 1024 * 1024

F32 = jnp.float32
BF16 = jnp.bfloat16
HI = lax.Precision.HIGHEST


def _rms(x, g):
    return x * lax.rsqrt(jnp.mean(x * x, axis=-1, keepdims=True) + EPS) * g


def _silu(x):
    return x * (1.0 / (1.0 + jnp.exp(-x)))


def _softplus(x):
    return jnp.maximum(x, 0.0) + jnp.log1p(jnp.exp(-jnp.abs(x)))


def _dot(a, b):
    return jnp.dot(a.astype(BF16), b.astype(BF16), preferred_element_type=F32)


def _dot_nt(a, b):
    return lax.dot_general(a.astype(BF16), b.astype(BF16), (((1,), (1,)), ((), ())),
                           preferred_element_type=F32)


def _dot_tn(a, b):
    return lax.dot_general(a.astype(BF16), b.astype(BF16), (((0,), (0,)), ((), ())),
                           preferred_element_type=F32)


def _const_spec(shape):
    n = len(shape)
    return pl.BlockSpec(shape, lambda *_: (0,) * n)


def _mixer_kernel(has_hist, has_after, ts, lc, *refs):
    if has_after:
        refs = refs[1:]
    if has_hist:
        x_ref, hista_ref, hists_ref, h0_ref = refs[:4]
        refs = refs[4:]
    else:
        x_ref = refs[0]
        refs = refs[1:]
    (lnw_ref, win_ref, wdt_ref, wdtt_ref, caw_ref, csw_ref, csb_ref, dtb_c_ref, alog_c_ref, dsk_c_ref,
     dtb_r_ref, alog_r_ref, nw_ref, wout_ref,
     h1_ref, na_ref, ns_ref, nh_ref, cata, cats, hst) = refs
    s = pl.program_id(1)
    pad = SUBLANES

    @pl.when(s == 0)
    def _():
        if has_hist:
            cata[pad - 2:pad, :] = hista_ref[...]
            cats[pad - 3:pad, :] = hists_ref[...]
            hst[...] = h0_ref[...]
        else:
            cata[0:pad, :] = jnp.zeros((pad, D_CONV), F32)
            cats[0:pad, :] = jnp.zeros((pad, XBC_DIM), F32)
            hst[...] = jnp.zeros(hst.shape, F32)

    x = x_ref[...]
    xn = _rms(x, lnw_ref[...]).astype(BF16)
    proj = jnp.dot(xn, win_ref[...], preferred_element_type=F32)
    dt_c = jnp.dot(xn, wdt_ref[...], preferred_element_type=F32)
    dt_r = lax.dot_general(wdtt_ref[...], xn, (((1,), (1,)), ((), ())),
                           preferred_element_type=F32)
    g_b = proj[:, 0:D_CONV]
    g_c = proj[:, D_CONV:2 * D_CONV]
    v_in = proj[:, 2 * D_CONV:3 * D_CONV]
    z = proj[:, 3 * D_CONV:3 * D_CONV + D_SSD]
    xbc = proj[:, 3 * D_CONV + D_SSD:3 * D_CONV + D_SSD + XBC_DIM]

    ua = g_c * v_in
    cata[pad:pad + ts, :] = ua
    caw = caw_ref[...]
    conv_a = (caw[0:1] * cata[pad - 2:pad - 2 + ts, :] + caw[1:2] * cata[pad - 1:pad - 1 + ts, :]
              + caw[2:3] * ua)
    y_a = g_b * conv_a
    hist_a = cata[pad + ts - 2:pad + ts, :]
    na_ref[...] = hist_a
    cata[pad - 2:pad, :] = hist_a

    cats[pad:pad + ts, :] = xbc
    csw = csw_ref[...]
    conv_s = (csw[0:1] * cats[pad - 3:pad - 3 + ts, :] + csw[1:2] * cats[pad - 2:pad - 2 + ts, :]
              + csw[2:3] * cats[pad - 1:pad - 1 + ts, :] + csw[3:4] * xbc)
    hist_s = cats[pad + ts - 3:pad + ts, :]
    ns_ref[...] = hist_s
    cats[pad - 3:pad, :] = hist_s
    xbc_c = _silu(conv_s + csb_ref[...])
    xs = xbc_c[:, 0:D_SSD]

    dtp_c = _softplus(dt_c + dtb_c_ref[...])
    dtp_r = _softplus(dt_r + dtb_r_ref[...])
    a_c = dtp_c * (-jnp.exp(alog_c_ref[...]))
    a_r = dtp_r * (-jnp.exp(alog_r_ref[...]))
    dsk_c = dsk_c_ref[...]

    lane = lax.broadcasted_iota(jnp.int32, (1, LANES), 1)
    first_head = lane < SSD_HEAD_DIM
    row2 = lax.broadcasted_iota(jnp.int32, (2 * SSD_HEAD_DIM, 1), 0) < SSD_HEAD_DIM
    li = lax.broadcasted_iota(jnp.int32, (lc, lc), 0)
    si = lax.broadcasted_iota(jnp.int32, (lc, lc), 1)
    causal = si <= li
    tril = causal.astype(F32)
    jrow = lax.broadcasted_iota(jnp.int32, (ts, lc), 0)
    scol = lax.broadcasted_iota(jnp.int32, (ts, lc), 1)

    def pair(col, h):
        return jnp.where(first_head, col[:, h:h + 1], col[:, h + 1:h + 2])

    y_chunks = []
    for c in range(ts // lc):
        r0 = c * lc
        rows = slice(r0, r0 + lc)
        acum_c = jnp.dot(tril, a_c[rows], preferred_element_type=F32, precision=HI)
        upper = ((jrow >= r0) & (jrow <= r0 + scol)).astype(F32)
        acum_r = jnp.dot(a_r, upper, preferred_element_type=F32, precision=HI)
        last_c = acum_c[lc - 1:lc, :]
        dec_c = jnp.exp(last_c - acum_c)
        eac_c = jnp.exp(acum_c)
        y_pairs = []
        for g in range(SSD_GROUPS):
            b_g = xbc_c[rows, D_SSD + g * SSD_STATE:D_SSD + (g + 1) * SSD_STATE]
            c_g = xbc_c[rows, D_SSD + (SSD_GROUPS + g) * SSD_STATE:D_SSD + (SSD_GROUPS + g + 1) * SSD_STATE]
            cb = _dot_nt(c_g, b_g)
            for q in range(SSD_HPG // 2):
                h = g * SSD_HPG + 2 * q
                pi = h // 2
                xs_p = xs[rows, pi * LANES:(pi + 1) * LANES]
                xdt = xs_p * pair(dtp_c[rows], h)
                res = []
                for hh in (h, h + 1):
                    seg = acum_c[:, hh:hh + 1] - acum_r[hh:hh + 1, :]
                    m_h = cb * jnp.exp(jnp.where(causal, seg, -jnp.inf))
                    res.append(_dot(m_h, xdt))
                y_diag = jnp.where(first_head, res[0], res[1])
                h_in = hst[pi]
                y_off = _dot_nt(c_g, h_in) * pair(eac_c, h)
                st = _dot_tn(xdt * pair(dec_c, h), b_g)
                cd = jnp.exp(jnp.where(row2, last_c[:, h:h + 1], last_c[:, h + 1:h + 2]))
                hst[pi] = cd * h_in + st
                y_pairs.append(y_diag + y_off + pair(dsk_c, h) * xs_p)
        y_chunks.append(jnp.concatenate(y_pairs, axis=1))
    y = y_chunks[0] if len(y_chunks) == 1 else jnp.concatenate(y_chunks, axis=0)
    y_b = _rms(y * _silu(z), nw_ref[...])
    wout = wout_ref[...]
    out = (jnp.dot(y_a.astype(BF16), wout[0:D_CONV], preferred_element_type=F32)
           + jnp.dot(y_b.astype(BF16), wout[D_CONV:], preferred_element_type=F32))
    h1_ref[...] = x + out
    nh_ref[...] = hst[...]


def _mixer(x, hist, w, ts, b0=0, nb=None, after=None):
    _, s, _ = x.shape
    b = x.shape[0] if nb is None else nb
    lc = min(CHUNK, s)
    ts = min(ts, s)
    has_hist = hist is not None
    assert not (has_hist and b0)
    per_b3 = lambda shape: pl.BlockSpec((None,) + shape, lambda i, j: (i,) + (0,) * len(shape))
    in_specs = [pl.BlockSpec((None, ts, D_MODEL), lambda i, j: (i + b0, j, 0))]
    args = [x]
    if after is not None:
        in_specs.insert(0, pl.BlockSpec((SUBLANES, after.shape[1]), lambda i, j: (0, 0)))
        args.insert(0, after)
    if has_hist:
        in_specs += [per_b3((CONV_A_W - 1, D_CONV)), per_b3((SSD_CONV_W - 1, XBC_DIM)),
                     per_b3(PAIR_STATE)]
        args += list(hist)
    wargs = [w["ln_mix"], w["w_in"], w["w_dt"], w["w_dtt"], w["conv_a_w"], w["conv_s_w"], w["conv_s_b"],
             w["dtb_c"], w["alog_c"], w["dsk_c"], w["dtb_r"], w["alog_r"], w["ssd_norm"], w["w_out"]]
    in_specs += [_const_spec(a.shape) for a in wargs]
    args += wargs
    out_shape = (jax.ShapeDtypeStruct((b, s, D_MODEL), F32),
                 jax.ShapeDtypeStruct((b, CONV_A_W - 1, D_CONV), F32),
                 jax.ShapeDtypeStruct((b, SSD_CONV_W - 1, XBC_DIM), F32),
                 jax.ShapeDtypeStruct((b,) + PAIR_STATE, F32))
    out_specs = (pl.BlockSpec((None, ts, D_MODEL), lambda i, j: (i, j, 0)),
                 per_b3((CONV_A_W - 1, D_CONV)), per_b3((SSD_CONV_W - 1, XBC_DIM)),
                 per_b3(PAIR_STATE))
    return pl.pallas_call(
        functools.partial(_mixer_kernel, has_hist, after is not None, ts, lc),
        out_shape=out_shape, grid=(b, s // ts), in_specs=in_specs, out_specs=out_specs,
        scratch_shapes=[pltpu.VMEM((SUBLANES + ts, D_CONV), F32), pltpu.VMEM((SUBLANES + ts, XBC_DIM), F32),
                        pltpu.VMEM(PAIR_STATE, F32)],
        compiler_params=pltpu.CompilerParams(dimension_semantics=("parallel", "arbitrary"),
                                             vmem_limit_bytes=VMEM_LIMIT),
        name="mixer_hist" if has_hist else "mixer",
    )(*args)


def _memkv_kernel(m_ref, g_ref, wk_ref, wv_ref, k_ref, v_ref):
    mn = _rms(m_ref[...], g_ref[...]).astype(BF16)
    k_ref[...] = jnp.dot(mn, wk_ref[...], preferred_element_type=F32)
    v_ref[...] = jnp.dot(mn, wv_ref[...], preferred_element_type=F32)


def _memory_kv(mem, g, wk, wv, tm=256):
    t = mem.shape[0]
    blk = pl.BlockSpec((tm, D_MODEL), lambda i: (i, 0))
    return pl.pallas_call(
        _memkv_kernel,
        out_shape=(jax.ShapeDtypeStruct((t, D_MODEL), F32),) * 2,
        grid=(t // tm,), in_specs=[blk, _const_spec(g.shape), _const_spec(wk.shape), _const_spec(wv.shape)],
        out_specs=(blk, blk),
        compiler_params=pltpu.CompilerParams(dimension_semantics=("parallel",), vmem_limit_bytes=VMEM_LIMIT),
        name="memory_kv",
    )(mem, g, wk, wv)


def _attn_kernel(x_ref, k_ref, v_ref, g_ref, wq_ref, wo_ref, o_ref):
    x = x_ref[...]
    xn = _rms(x, g_ref[...]).astype(BF16)
    q = jnp.dot(xn, wq_ref[...], preferred_element_type=F32)
    by_head = len(k_ref.shape) == 3
    wo = wo_ref[...]
    acc = x
    for h in range(MEM_HEADS):
        cols = slice(h * MEM_HEAD_DIM, (h + 1) * MEM_HEAD_DIM)
        k_h = k_ref[:, h, :] if by_head else k_ref[:, cols]
        v_h = v_ref[:, h, :] if by_head else v_ref[:, cols]
        sc = _dot_nt(q[:, cols], k_h) * (MEM_HEAD_DIM ** -0.5)
        sc = sc - jnp.max(sc, axis=-1, keepdims=True)
        p = jnp.exp(sc)
        p = p / jnp.sum(p, axis=-1, keepdims=True)
        o_h = _dot(p, v_h)
        acc = acc + jnp.dot(o_h.astype(BF16), wo[cols, :], preferred_element_type=F32)
    o_ref[...] = acc


def _attn(x, k, v, g, wq, wo, ts, kv0=0):
    b, s, _ = x.shape
    ts = min(ts, s)
    xblk = pl.BlockSpec((None, ts, D_MODEL), lambda i, j: (i, j, 0))
    if k.ndim == 3:
        kvblk = pl.BlockSpec((None, N_MEM, D_MODEL), lambda i, j: (i + kv0, 0, 0))
    else:
        kvblk = pl.BlockSpec((None, N_MEM, MEM_HEADS, MEM_HEAD_DIM), lambda i, j: (i + kv0, 0, 0, 0))
    return pl.pallas_call(
        _attn_kernel,
        out_shape=jax.ShapeDtypeStruct((b, s, D_MODEL), F32),
        grid=(b, s // ts),
        in_specs=[xblk, kvblk, kvblk, _const_spec(g.shape), _const_spec(wq.shape), _const_spec(wo.shape)],
        out_specs=xblk,
        compiler_params=pltpu.CompilerParams(dimension_semantics=("parallel", "parallel"),
                                             vmem_limit_bytes=VMEM_LIMIT),
        name="attn",
    )(x, k, v, g, wq, wo)


def _tree_reduce(op, x):
    n = x.shape[0]
    while n > 1:
        half = n // 2
        head = op(x[:half], x[half:2 * half])
        x = head if n % 2 == 0 else jnp.concatenate([head, x[2 * half:]], axis=0)
        n = x.shape[0]
    return x


def _top_k_rows(sc, k, payload=None):
    n = sc.shape[0]
    rows = lax.broadcasted_iota(jnp.int32, sc.shape, 0).astype(F32)
    vals, outs = [], []
    for _ in range(k):
        m = _tree_reduce(jnp.maximum, sc)
        cand = jnp.where(sc == m, rows, float(n))
        first = _tree_reduce(jnp.minimum, cand)
        hit = cand == first
        sc = jnp.where(hit, -jnp.inf, sc)
        vals.append(m)
        if payload is None:
            outs.append(first.astype(jnp.int32))
        else:
            outs.append(jnp.sum(jnp.where(hit, payload, 0), axis=0, keepdims=True))
    return jnp.concatenate(vals, axis=0), jnp.concatenate(outs, axis=0)


def _pair_candidates(v1, i1, v2, i2):
    comb = [v1[a:a + 1] + v2[0:PEER_TOPK // (a + 1)] for a in range(PEER_TOPK)]
    cidx = [i1[a:a + 1] * N_KEYS + i2[0:PEER_TOPK // (a + 1)] for a in range(PEER_TOPK)]
    return jnp.concatenate(comb, axis=0), jnp.concatenate(cidx, axis=0)


SELECT_BLOCK = SUBLANES * LANES


def _select_kernel(tb, x_ref, g_ref, wq_ref, keys_ref, *rest):
    xn_ref, eid_ref, tile_ref, shift_ref, gate_ref, s1, s2, eid_s, gate_s = rest[-9:]
    groups = tb // LANES
    assert groups == SUBLANES
    xn = _rms(x_ref[...], g_ref[...])
    xn_ref[...] = xn
    xn = xn.astype(BF16)

    def head(h, carry):
        q = jnp.dot(xn, wq_ref[h], preferred_element_type=F32)
        for g in range(groups):
            qg = q[g * LANES:(g + 1) * LANES]
            s1[pl.ds(g, N_KEYS, stride=SUBLANES), :] = _dot_nt(keys_ref[h, 0], qg[:, 0:PEER_DK_HALF])
            s2[pl.ds(g, N_KEYS, stride=SUBLANES), :] = _dot_nt(keys_ref[h, 1], qg[:, PEER_DK_HALF:PEER_DK])
        v1, i1 = _top_k_rows(s1[...].reshape(N_KEYS, SUBLANES, LANES), PEER_TOPK)
        v2, i2 = _top_k_rows(s2[...].reshape(N_KEYS, SUBLANES, LANES), PEER_TOPK)
        comb, cidx = _pair_candidates(v1, i1, v2, i2)
        top, eid = _top_k_rows(comb, PEER_TOPK, cidx)
        ex = jnp.exp(top - top[0:1])
        gate = ex / jnp.sum(ex, axis=0, keepdims=True)
        rows = pl.ds(pl.multiple_of(h * PEER_TOPK * SUBLANES, PEER_TOPK * SUBLANES), PEER_TOPK * SUBLANES)
        gate_s[rows, :] = gate.reshape(PEER_TOPK * SUBLANES, LANES)
        eid_s[rows, :] = eid.reshape(PEER_TOPK * SUBLANES, LANES)
        return carry

    lax.fori_loop(0, PEER_HEADS, head, 0)
    for g in range(groups):
        toks = slice(g * LANES, (g + 1) * LANES)
        eid = eid_s[pl.ds(g, PICKS, stride=SUBLANES), :].T
        upper = eid >= HALF_EXPERTS
        eid_ref[toks, :] = eid
        tile_ref[toks, :] = jnp.where(upper, eid - HALF_EXPERTS, eid) * SUBLANES
        shift_ref[toks, :] = jnp.where(upper, 0, 16)
        gate_ref[toks, :] = gate_s[pl.ds(g, PICKS, stride=SUBLANES), :].T


def _select(x, g, wq, keys, tb, t0, t, after=None):
    b0 = t0 // tb
    extra = [] if after is None else [after]
    xin = pl.BlockSpec((tb, D_MODEL), lambda i: (i + b0, 0))
    xblk = pl.BlockSpec((tb, D_MODEL), lambda i: (i, 0))
    picks_i = jax.ShapeDtypeStruct((t, PICKS), jnp.int32)
    return pl.pallas_call(
        functools.partial(_select_kernel, tb),
        out_shape=(jax.ShapeDtypeStruct((t, D_MODEL), F32), picks_i, picks_i, picks_i,
                   jax.ShapeDtypeStruct((t, PICKS), F32)),
        grid=(t // tb,),
        in_specs=[xin, _const_spec(g.shape), _const_spec(wq.shape), _const_spec(keys.shape)]
        + [_after_spec(a) for a in extra],
        out_specs=(xblk,) + (_pick_block(tb),) * 4,
        scratch_shapes=[pltpu.VMEM((N_KEYS * SUBLANES, LANES), F32), pltpu.VMEM((N_KEYS * SUBLANES, LANES), F32),
                        pltpu.VMEM((PICKS * SUBLANES, LANES), jnp.int32), pltpu.VMEM((PICKS * SUBLANES, LANES), F32)],
        compiler_params=pltpu.CompilerParams(dimension_semantics=("parallel",), vmem_limit_bytes=VMEM_LIMIT),
        name="peer_select",
    )(x, g, wq, keys, *extra)


HI_HALF_MASK = -(1 << 16)


def _unpack(word, shift):
    return lax.bitcast_convert_type((word << shift) & HI_HALF_MASK, F32)


def _splat_picks(blk, one_pass):
    tb = blk.shape[0]
    r = lax.broadcasted_iota(jnp.int32, (PICKS, LANES), 0)
    c = lax.broadcasted_iota(jnp.int32, (PICKS, LANES), 1)
    eye = (r == c)[None]
    ones = jnp.ones((LANES, LANES), BF16)

    def one(piece):
        diag = jnp.where(eye, piece[:, None, :], 0.0).astype(BF16).reshape(tb * PICKS, LANES)
        return jnp.dot(diag, ones, preferred_element_type=F32)

    if one_pass:
        out = one(blk)
    else:
        hi = blk.astype(BF16).astype(F32)
        rest = blk - hi
        mid = rest.astype(BF16).astype(F32)
        out = one(hi) + one(mid) + one(rest - mid)
    return out.reshape(tb, PICKS, LANES)


def _full(ref, k):
    return jnp.broadcast_to(ref[k:k + 1, :], (SUBLANES, LANES))


def _expert_row(tab_ref, tile_ref, t, k, shifts):
    idx = pl.multiple_of(tile_ref[t, k], SUBLANES)
    return _unpack(tab_ref[pl.ds(idx, SUBLANES), :], _full(shifts, k))


def _hidden_kernel(tb, tile_ref, shift_ref, x_ref, gate_ref, tab_ref, *rest):
    gh_ref, bsh, stage, parts, xt = rest[-5:]
    bsh[...] = _splat_picks(shift_ref[...].astype(F32), True).astype(jnp.int32)
    for s in range(SUBLANES):
        xt[pl.ds(s, tb, stride=SUBLANES), :] = x_ref[:, s * LANES:(s + 1) * LANES]

    half = PICKS // 2
    quad = SUBLANES // 2
    low = lax.broadcasted_iota(jnp.int32, (SUBLANES, LANES), 0) < quad

    def token(t, carry):
        x = xt[pl.ds(pl.multiple_of(t * SUBLANES, SUBLANES), SUBLANES), :]
        shifts = bsh.at[t]
        for j in range(half):
            a = x * _expert_row(tab_ref, tile_ref, t, j, shifts)
            b = x * _expert_row(tab_ref, tile_ref, t, j + half, shifts)
            stage[j * SUBLANES:(j + 1) * SUBLANES, :] = jnp.where(low, a + pltpu.roll(a, quad, 0),
                                                                  b + pltpu.roll(b, quad, 0))
        for p in range(2):
            part = stage[pl.ds(p * quad, half, stride=SUBLANES), :]
            for s in range(1, quad):
                part = part + stage[pl.ds(p * quad + s, half, stride=SUBLANES), :]
            parts[t, p * half:(p + 1) * half, :] = part
        return carry

    lax.fori_loop(0, tb, token, 0)
    gh_ref[...] = jnp.sum(parts[...], axis=-1)
    hid = gh_ref[...]
    gelu = 0.5 * hid * (1.0 + lax.erf(hid * (2.0 ** -0.5)))
    gh_ref[...] = gate_ref[...] * gelu


def _pick_block(tb, space=None):
    return pl.BlockSpec((tb, PICKS), lambda i: (i, 0), memory_space=space)


def _table_spec(tab):
    return pl.BlockSpec(tab.shape, lambda i: (0, 0), pipeline_mode=pl.Buffered(1))


def _hidden(tile, shift, xg, gate, tab, tb, after=None):
    t = xg.shape[0]
    extra = [] if after is None else [after]
    return pl.pallas_call(
        functools.partial(_hidden_kernel, tb),
        out_shape=jax.ShapeDtypeStruct((t, PICKS), F32),
        grid=(t // tb,),
        in_specs=[_pick_block(tb, pltpu.SMEM), _pick_block(tb),
                  pl.BlockSpec((tb, D_MODEL), lambda i: (i, 0)),
                  _pick_block(tb), _table_spec(tab)] + [_after_spec(a) for a in extra],
        out_specs=_pick_block(tb),
        scratch_shapes=[pltpu.VMEM((tb, PICKS, LANES), jnp.int32), pltpu.VMEM((PICKS // 2 * SUBLANES, LANES), F32),
                        pltpu.VMEM((tb, PICKS, LANES), F32), pltpu.VMEM((tb * SUBLANES, LANES), F32)],
        compiler_params=pltpu.CompilerParams(dimension_semantics=("parallel",), vmem_limit_bytes=VMEM_LIMIT),
        name="peer_hidden",
    )(tile, shift, xg, gate, tab, *extra)


def _combine_kernel(tb, tile_ref, shift_ref, gh_ref, tab_ref, x_ref, g_ref, *rest):
    o_ref, bsh, bg, ot = rest[-4:]
    bsh[...] = _splat_picks(shift_ref[...].astype(F32), True).astype(jnp.int32)
    bg[...] = _splat_picks(gh_ref[...], False)

    def token(t, carry):
        shifts, gates = bsh.at[t], bg.at[t]
        acc = [jnp.zeros((SUBLANES, LANES), F32) for _ in range(4)]
        for k in range(PICKS):
            acc[k % 4] = acc[k % 4] + _full(gates, k) * _expert_row(tab_ref, tile_ref, t, k, shifts)
        ot[pl.ds(pl.multiple_of(t * SUBLANES, SUBLANES), SUBLANES), :] = (acc[0] + acc[1]) + (acc[2] + acc[3])
        return carry

    lax.fori_loop(0, tb, token, 0)
    for s in range(SUBLANES):
        o_ref[:, s * LANES:(s + 1) * LANES] = ot[pl.ds(s, tb, stride=SUBLANES), :]
    o_ref[...] = _rms(x_ref[...] + o_ref[...], g_ref[...])


def _combine(tile, shift, gh, tab, x, g, tb, x_row0, out_row0, out_rows, into=None):
    t = gh.shape[0]
    xb0, ob0 = x_row0 // tb, out_row0 // tb
    extra = [] if into is None else [into]
    return pl.pallas_call(
        functools.partial(_combine_kernel, tb),
        out_shape=jax.ShapeDtypeStruct((out_rows, D_MODEL), F32),
        grid=(t // tb,),
        in_specs=[_pick_block(tb, pltpu.SMEM), _pick_block(tb), _pick_block(tb), _table_spec(tab),
                  pl.BlockSpec((tb, D_MODEL), lambda i: (i + xb0, 0)), _const_spec(g.shape)]
        + [pl.BlockSpec(memory_space=pl.ANY) for _ in extra],
        input_output_aliases={6: 0} if extra else {},
        out_specs=pl.BlockSpec((tb, D_MODEL), lambda i: (i + ob0, 0)),
        scratch_shapes=[pltpu.VMEM((tb, PICKS, LANES), jnp.int32), pltpu.VMEM((tb, PICKS, LANES), F32),
                        pltpu.VMEM((tb * SUBLANES, LANES), F32)],
        compiler_params=pltpu.CompilerParams(dimension_semantics=("parallel",), vmem_limit_bytes=VMEM_LIMIT),
        name="peer_combine",
    )(tile, shift, gh, tab, x, g, *extra)


SC_LANES = 16
SC_WORKERS = 32
SC_CHUNK = 32
SC_FEATURE_BLOCK = 256
SC_EARLY_SEQS = 3
SC_PROMPT_TOKENS = 13312


def _sc_call(body, n, side_width, out_width):
    mesh = plsc.VectorSubcoreMesh(core_axis_name="c", subcore_axis_name="s")
    return pl.kernel(body, mesh=mesh, out_type=jax.ShapeDtypeStruct((n, out_width), F32),
                     compiler_params=pltpu.CompilerParams(needs_layout_passes=False),
                     scratch_types=[pltpu.VMEM((2 * PICKS,), jnp.int32), pltpu.VMEM((2 * side_width,), F32),
                                    pltpu.VMEM((2 * out_width,), F32),
                                    pltpu.VMEM((SC_CHUNK, D_MODEL), F32), pltpu.VMEM((SC_CHUNK, D_MODEL), F32)]
                     + [pltpu.SemaphoreType.DMA] * 5)


def _sc_token_loop(n, eid_hbm, side_hbm, tab_hbm, out_hbm, idx_v, side_v, out_v, bufs, sems, compute):
    sem_rows, sem_fetch, sem_put = sems[0:2], sems[2], sems[3:5]
    per_worker = n // SC_WORKERS
    first = (lax.axis_index("s") * 2 + lax.axis_index("c")) * per_worker
    last = first + per_worker - 1
    chunks = PICKS // SC_CHUNK
    side_w = side_v.shape[0] // 2
    out_w = out_v.shape[0] // 2

    def fetch(t, slot):
        return (pltpu.make_async_copy(eid_hbm.at[t], idx_v.at[pl.ds(slot * PICKS, PICKS)], sem_fetch),
                pltpu.make_async_copy(side_hbm.at[t], side_v.at[pl.ds(slot * side_w, side_w)], sem_fetch))

    def gather(slot, c):
        ids = idx_v.at[pl.ds(slot * PICKS + c * SC_CHUNK, SC_CHUNK)]
        return pltpu.make_async_copy(tab_hbm.at[ids], bufs[c % 2], sem_rows[c % 2])

    def put(t, slot):
        return pltpu.make_async_copy(out_v.at[pl.ds(slot * out_w, out_w)], out_hbm.at[t], sem_put[slot])

    for cp in fetch(first, 0):
        cp.start()
    for cp in fetch(first, 0):
        cp.wait()
    gather(0, 0).start()
    gather(0, 1).start()

    @pl.loop(0, per_worker, step=2)
    def _(i):
        for slot in range(2):
            t = first + i + slot
            nxt = jnp.minimum(t + 1, last)
            for cp in fetch(nxt, 1 - slot):
                cp.start()

            @pl.when(i >= 2)
            def _():
                put(t, slot).wait()

            for c in range(chunks):
                gather(slot, c).wait()
                compute(c, bufs[c % 2], slot * side_w, slot * out_w)
                if c + 2 < chunks:
                    gather(slot, c + 2).start()
                else:
                    if c + 2 == chunks:
                        for cp in fetch(nxt, 1 - slot):
                            cp.wait()
                    gather(1 - slot, c + 2 - chunks).start()
            put(t, slot).start()

    gather(0, 0).wait()
    gather(0, 1).wait()
    put(last, 0).wait()
    put(last, 1).wait()


def _sc_hidden(eid, x, tab):
    n = eid.shape[0]
    lanes = SC_LANES

    def body(eid_hbm, x_hbm, tab_hbm, out_hbm, idx_v, x_v, hid_v, rows0, rows1, *sems):
        lane = lax.iota(jnp.int32, lanes)

        def compute(c, rows, x0, h0):
            for g in range(SC_CHUNK // lanes):
                def step(j, accs):
                    off = pl.multiple_of(j * lanes, lanes)
                    xj = x_v[pl.ds(x0 + off, lanes)]
                    return tuple(accs[r] + xj * rows[g * lanes + r, pl.ds(off, lanes)] for r in range(lanes))
                accs = lax.fori_loop(0, D_MODEL // lanes, step,
                                     tuple(jnp.zeros((lanes,), F32) for _ in range(lanes)))
                hv = jnp.zeros((lanes,), F32)
                for r in range(lanes):
                    hv = jnp.where(lane == r, jnp.sum(accs[r]), hv)
                hid_v[pl.ds(h0 + c * SC_CHUNK + g * lanes, lanes)] = hv

        _sc_token_loop(n, eid_hbm, x_hbm, tab_hbm, out_hbm, idx_v, x_v, hid_v, (rows0, rows1), sems, compute)

    return _sc_call(body, n, D_MODEL, PICKS)(eid, x, tab)


def _sc_combine(eid, gh, tab):
    n = eid.shape[0]
    lanes = SC_LANES
    nacc = SC_FEATURE_BLOCK // lanes

    def body(eid_hbm, g_hbm, tab_hbm, out_hbm, idx_v, g_v, out_v, rows0, rows1, *sems):
        def compute(c, rows, g0, o0):
            for fb in range(D_MODEL // SC_FEATURE_BLOCK):
                f0 = fb * SC_FEATURE_BLOCK
                if c == 0:
                    init = tuple(jnp.zeros((lanes,), F32) for _ in range(nacc))
                else:
                    init = tuple(out_v[pl.ds(o0 + f0 + q * lanes, lanes)] for q in range(nacc))

                def step(r, accs):
                    gk = plsc.load_gather(g_v, [jnp.full((lanes,), g0 + c * SC_CHUNK + r, jnp.int32)])
                    return tuple(accs[q] + gk * rows[r, pl.ds(f0 + q * lanes, lanes)] for q in range(nacc))
                accs = lax.fori_loop(0, SC_CHUNK, step, init)
                for q in range(nacc):
                    out_v[pl.ds(o0 + f0 + q * lanes, lanes)] = accs[q]

        _sc_token_loop(n, eid_hbm, g_hbm, tab_hbm, out_hbm, idx_v, g_v, out_v, (rows0, rows1), sems, compute)

    return _sc_call(body, n, PICKS, D_MODEL)(eid, gh, tab)


def _after_spec(after):
    return pl.BlockSpec((SUBLANES, after.shape[1]), lambda i: (0, 0))


def _gate_gelu_kernel(h_ref, g_ref, after_ref, o_ref):
    hid = h_ref[...]
    o_ref[...] = g_ref[...] * (0.5 * hid * (1.0 + lax.erf(hid * (2.0 ** -0.5))))


def _gate_gelu(hid, gate, tb, after):
    t = hid.shape[0]
    return pl.pallas_call(
        _gate_gelu_kernel, out_shape=jax.ShapeDtypeStruct((t, PICKS), F32), grid=(t // tb,),
        in_specs=[_pick_block(tb), _pick_block(tb), _after_spec(after)], out_specs=_pick_block(tb),
        compiler_params=pltpu.CompilerParams(dimension_semantics=("parallel",)),
        name="gate_gelu",
    )(hid, gate, after)


def _final_kernel(x_ref, p_ref, g_ref, *rest):
    rest[-1][...] = _rms(x_ref[...] + p_ref[...], g_ref[...])


def _final(x, p, g, tb, out_row0, out_rows, into=None):
    b0 = out_row0 // tb
    local = pl.BlockSpec((tb, D_MODEL), lambda i: (i, 0))
    in_specs = [local, local, _const_spec(g.shape)]
    args = [x, p, g]
    aliases = {}
    if into is not None:
        aliases = {len(args): 0}
        in_specs.append(pl.BlockSpec(memory_space=pl.ANY))
        args.append(into)
    return pl.pallas_call(
        _final_kernel, out_shape=jax.ShapeDtypeStruct((out_rows, D_MODEL), F32), grid=(p.shape[0] // tb,),
        in_specs=in_specs, out_specs=pl.BlockSpec((tb, D_MODEL), lambda i: (i + b0, 0)),
        input_output_aliases=aliases,
        compiler_params=pltpu.CompilerParams(dimension_semantics=("parallel",), vmem_limit_bytes=VMEM_LIMIT),
        name="final_norm",
    )(*args)


PACK_BLOCK = 256


def _pack_kernel(lo_ref, hi_ref, o_ref):
    def bf16_bits(x):
        return lax.bitcast_convert_type(x.astype(BF16).astype(F32), jnp.int32)
    word = bf16_bits(hi_ref[...]) | lax.shift_right_logical(bf16_bits(lo_ref[...]), 16)
    for s in range(SUBLANES):
        o_ref[pl.ds(s, PACK_BLOCK, stride=SUBLANES), :] = word[:, s * LANES:(s + 1) * LANES]


def _pack_table(tab):
    steps = HALF_EXPERTS // PACK_BLOCK
    return pl.pallas_call(
        _pack_kernel, out_shape=jax.ShapeDtypeStruct((HALF_EXPERTS * SUBLANES, LANES), jnp.int32), grid=(steps,),
        in_specs=[pl.BlockSpec((PACK_BLOCK, D_MODEL), lambda i: (i, 0)),
                  pl.BlockSpec((PACK_BLOCK, D_MODEL), lambda i: (i + steps, 0))],
        out_specs=pl.BlockSpec((PACK_BLOCK * SUBLANES, LANES), lambda i: (i, 0)),
        compiler_params=pltpu.CompilerParams(dimension_semantics=("parallel",)),
        name="pack_table",
    )(tab, tab)


def _col(v, n=LANES):
    return jnp.pad(v.astype(F32), (0, n - v.shape[0])).reshape(1, n)


def _row(v, n=2 * SUBLANES):
    return jnp.pad(v.astype(F32), (0, n - v.shape[0])).reshape(n, 1)


def _peer(x, w, tb_sel, tb_exp, n_sc, out_row0=0, out_rows=None, after=None, into=None, between=None):
    t = x.shape[0]
    out_rows = t if out_rows is None else out_rows
    n_tc = t - n_sc
    if n_sc:
        xn_s, eid_s, _, _, gate_s = _select(x, w["ln_ffn"], w["w_pq"], w["sub_keys"], SELECT_BLOCK,0, n_sc, after=after)
        hid_s = _sc_hidden(eid_s, xn_s, w["u_tab"])
    xn, _, tile, shift, gate = _select(x, w["ln_ffn"], w["w_pq"], w["sub_keys"], SELECT_BLOCK,n_sc, n_tc,
                                       after=xn_s if n_sc else None)
    gh = _hidden(tile, shift, xn, gate, w["u_pack"], tb_exp, after=into)
    if n_sc:
        other = None if between is None else between(gh)
        out_s = _sc_combine(eid_s, _gate_gelu(hid_s, gate_s, tb_sel, after=gh if other is None else other),
                            w["v_tab"])
    y = _combine(tile, shift, gh, w["v_pack"], x, w["final_norm"], tb_exp, n_sc, out_row0 + n_sc, out_rows, into)
    if n_sc:
        y = _final(x, out_s, w["final_norm"], tb_sel, out_row0, out_rows, into=y)
    return (y, other) if between is not None else y


def kernel(x_prompt, x_sample, mem_prompt, cache_conv_a, cache_conv_ssd, state_ssd, cache_mem_k, cache_mem_v, ln_mix_w, w_in, conv_a_w, conv_s_w, conv_s_b, dt_bias, a_log, d_skip, ssd_norm_w, w_out, ln_mem_w, mem_norm_w, w_mq, w_mk, w_mv, w_mo, ln_ffn_w, w_pq, sub_keys, u_tab, v_tab, final_norm_w):
    depth = w_in.shape[0]
    assert depth == 1
    l = 0
    n_main = 3 * D_CONV + D_SSD + XBC_DIM
    w_dt = w_in[l][:, n_main:]
    w = {
        "ln_mix": ln_mix_w[l].reshape(1, D_MODEL),
        "w_in": w_in[l][:, :n_main].astype(BF16),
        "w_dt": jnp.pad(w_dt, ((0, 0), (0, LANES - SSD_HEADS))).astype(BF16),
        "w_dtt": jnp.pad(w_dt.T, ((0, 2 * SUBLANES - SSD_HEADS), (0, 0))).astype(BF16),
        "conv_a_w": conv_a_w[l], "conv_s_w": conv_s_w[l], "conv_s_b": conv_s_b[l].reshape(1, XBC_DIM),
        "dtb_c": _col(dt_bias[l]), "alog_c": _col(a_log[l]), "dsk_c": _col(d_skip[l]),
        "dtb_r": _row(dt_bias[l]), "alog_r": _row(a_log[l]),
        "ssd_norm": ssd_norm_w[l].reshape(1, D_SSD),
        "w_out": w_out[l].astype(BF16),
        "ln_ffn": ln_ffn_w[l].reshape(1, D_MODEL),
        "w_pq": w_pq[l].astype(BF16).reshape(D_MODEL, PEER_HEADS, PEER_DK).transpose(1, 0, 2),
        "sub_keys": sub_keys[l].astype(BF16),
        "u_pack": _pack_table(u_tab[l]),
        "v_pack": _pack_table(v_tab[l]),
        "u_tab": u_tab[l], "v_tab": v_tab[l],
        "final_norm": final_norm_w.reshape(1, D_MODEL),
    }
    ln_mem = ln_mem_w[l].reshape(1, D_MODEL)
    wq, wo = w_mq[l].astype(BF16), w_mo[l].astype(BF16)

    bp, sp, _ = x_prompt.shape
    bs, ss, _ = x_sample.shape

    mk, mv = _memory_kv(mem_prompt.reshape(bp * N_MEM, D_MODEL), mem_norm_w[l].reshape(1, D_MODEL),
                        w_mk[l].astype(BF16), w_mv[l].astype(BF16))
    mk3, mv3 = mk.reshape(bp, N_MEM, D_MODEL), mv.reshape(bp, N_MEM, D_MODEL)
    n1 = SC_EARLY_SEQS
    h1, pa1, ps1, ph1 = _mixer(x_prompt, None, w, 256, 0, n1)
    h1 = _attn(h1, mk3, mv3, ln_mem, wq, wo, 256).reshape(n1 * sp, D_MODEL)

    xn1, eid1, _, _, gate1 = _select(h1, w["ln_ffn"], w["w_pq"], w["sub_keys"], SELECT_BLOCK,0, n1 * sp)
    hid1 = _sc_hidden(eid1, xn1, w["u_tab"])
    h2, pa2, ps2, ph2 = _mixer(x_prompt, None, w, 256, n1, bp - n1, after=xn1)
    h2 = _attn(h2, mk3, mv3, ln_mem, wq, wo, 256, kv0=n1).reshape((bp - n1) * sp, D_MODEL)

    hs, sa, ssd, sh = _mixer(x_sample, (cache_conv_a[l], cache_conv_ssd[l],
                                        state_ssd[l].reshape((bs,) + PAIR_STATE)), w, 256, after=h2)
    hs = _attn(hs, cache_mem_k[l], cache_mem_v[l], ln_mem, wq, wo, 256).reshape(bs * ss, D_MODEL)
    xn_q, _, tile_q, shift_q, gate_q = _select(hs, w["ln_ffn"], w["w_pq"], w["sub_keys"], SELECT_BLOCK,0, bs * ss)

    gh1 = _gate_gelu(hid1, gate1, 256, after=xn_q)
    out1 = _sc_combine(eid1, gh1, w["v_tab"])
    y1 = _final(h1, out1, w["final_norm"], 256, 0, bp * sp)
    def sample_experts(gh_main):
        gh_q = _hidden(tile_q, shift_q, xn_q, gate_q, w["u_pack"], 64, after=gh_main)
        return _combine(tile_q, shift_q, gh_q, w["v_pack"], hs, w["final_norm"], 64, 0, 0, bs * ss)

    y_prompt, y_sample = _peer(h2, w, 256, 64, SC_PROMPT_TOKENS, n1 * sp, bp * sp, after=gh1, into=y1,
                               between=sample_experts)
    y_prompt, y_sample = y_prompt.reshape(bp, sp, D_MODEL), y_sample.reshape(bs, ss, D_MODEL)
    pa, ps, ph = (jnp.concatenate(p, axis=0) for p in ((pa1, pa2), (ps1, ps2), (ph1, ph2)))

    kv_shape = (1, bp, N_MEM, MEM_HEADS, MEM_HEAD_DIM)
    st_shape = (SSD_HEADS, SSD_HEAD_DIM, SSD_STATE)
    return (y_prompt, y_sample, pa[None], ps[None], ph.reshape((1, bp) + st_shape), mk.reshape(kv_shape),
            mv.reshape(kv_shape), sa[None], ssd[None], sh.reshape((1, bs) + st_shape))
```

```python
import functools

import jax
import jax.numpy as jnp
from jax import lax
from jax.experimental import pallas as pl
from jax.experimental.pallas import tpu as pltpu
from jax.experimental.pallas import tpu_sc as plsc

D_MODEL = 1024
CHUNK = 64
D_CONV = 512
CONV_A_W = 3
D_SSD = 512
SSD_HEAD_DIM = 64
SSD_HEADS = 8
SSD_GROUPS = 2
SSD_HPG = 4
SSD_STATE = 128
SSD_CONV_W = 4
XBC_DIM = 1024
N_MEM = 256
MEM_HEADS = 4
MEM_HEAD_DIM = 256
PEER_HEADS = 8
N_KEYS = 128
N_EXPERTS = N_KEYS * N_KEYS
PEER_TOPK = 16
PEER_DK = 256
PEER_DK_HALF = 128
PICKS = PEER_HEADS * PEER_TOPK
EPS = 1e-6

LANES = 128
SUBLANES = 8
HALF_EXPERTS = N_EXPERTS // 2
PAIR_STATE = (SSD_HEADS // 2, 2 * SSD_HEAD_DIM, SSD_STATE)
VMEM_LIMIT = 56 * 1024 * 1024

F32 = jnp.float32
BF16 = jnp.bfloat16
HI = lax.Precision.HIGHEST


def _rms(x, g):
    return x * lax.rsqrt(jnp.mean(x * x, axis=-1, keepdims=True) + EPS) * g


def _silu(x):
    return x * (1.0 / (1.0 + jnp.exp(-x)))


def _softplus(x):
    return jnp.maximum(x, 0.0) + jnp.log1p(jnp.exp(-jnp.abs(x)))


def _dot(a, b):
    return jnp.dot(a.astype(BF16), b.astype(BF16), preferred_element_type=F32)


def _dot_nt(a, b):
    return lax.dot_general(a.astype(BF16), b.astype(BF16), (((1,), (1,)), ((), ())),
                           preferred_element_type=F32)


def _dot_tn(a, b):
    return lax.dot_general(a.astype(BF16), b.astype(BF16), (((0,), (0,)), ((), ())),
                           preferred_element_type=F32)


def _const_spec(shape):
    n = len(shape)
    return pl.BlockSpec(shape, lambda *_: (0,) * n)


def _mixer_kernel(has_hist, has_after, ts, lc, *refs):
    if has_after:
        refs = refs[1:]
    if has_hist:
        x_ref, hista_ref, hists_ref, h0_ref = refs[:4]
        refs = refs[4:]
    else:
        x_ref = refs[0]
        refs = refs[1:]
    (lnw_ref, win_ref, wdt_ref, wdtt_ref, caw_ref, csw_ref, csb_ref, dtb_c_ref, alog_c_ref, dsk_c_ref,
     dtb_r_ref, alog_r_ref, nw_ref, wout_ref,
     h1_ref, na_ref, ns_ref, nh_ref, cata, cats, hst) = refs
    s = pl.program_id(1)
    pad = SUBLANES

    @pl.when(s == 0)
    def _():
        if has_hist:
            cata[pad - 2:pad, :] = hista_ref[...]
            cats[pad - 3:pad, :] = hists_ref[...]
            hst[...] = h0_ref[...]
        else:
            cata[0:pad, :] = jnp.zeros((pad, D_CONV), F32)
            cats[0:pad, :] = jnp.zeros((pad, XBC_DIM), F32)
            hst[...] = jnp.zeros(hst.shape, F32)

    x = x_ref[...]
    xn = _rms(x, lnw_ref[...]).astype(BF16)
    proj = jnp.dot(xn, win_ref[...], preferred_element_type=F32)
    dt_c = jnp.dot(xn, wdt_ref[...], preferred_element_type=F32)
    dt_r = lax.dot_general(wdtt_ref[...], xn, (((1,), (1,)), ((), ())),
                           preferred_element_type=F32)
    g_b = proj[:, 0:D_CONV]
    g_c = proj[:, D_CONV:2 * D_CONV]
    v_in = proj[:, 2 * D_CONV:3 * D_CONV]
    z = proj[:, 3 * D_CONV:3 * D_CONV + D_SSD]
    xbc = proj[:, 3 * D_CONV + D_SSD:3 * D_CONV + D_SSD + XBC_DIM]

    ua = g_c * v_in
    cata[pad:pad + ts, :] = ua
    caw = caw_ref[...]
    conv_a = (caw[0:1] * cata[pad - 2:pad - 2 + ts, :] + caw[1:2] * cata[pad - 1:pad - 1 + ts, :]
              + caw[2:3] * ua)
    y_a = g_b * conv_a
    hist_a = cata[pad + ts - 2:pad + ts, :]
    na_ref[...] = hist_a
    cata[pad - 2:pad, :] = hist_a

    cats[pad:pad + ts, :] = xbc
    csw = csw_ref[...]
    conv_s = (csw[0:1] * cats[pad - 3:pad - 3 + ts, :] + csw[1:2] * cats[pad - 2:pad - 2 + ts, :]
              + csw[2:3] * cats[pad - 1:pad - 1 + ts, :] + csw[3:4] * xbc)
    hist_s = cats[pad + ts - 3:pad + ts, :]
    ns_ref[...] = hist_s
    cats[pad - 3:pad, :] = hist_s
    xbc_c = _silu(conv_s + csb_ref[...])
    xs = xbc_c[:, 0:D_SSD]

    dtp_c = _softplus(dt_c + dtb_c_ref[...])
    dtp_r = _softplus(dt_r + dtb_r_ref[...])
    a_c = dtp_c * (-jnp.exp(alog_c_ref[...]))
    a_r = dtp_r * (-jnp.exp(alog_r_ref[...]))
    dsk_c = dsk_c_ref[...]

    lane = lax.broadcasted_iota(jnp.int32, (1, LANES), 1)
    first_head = lane < SSD_HEAD_DIM
    row2 = lax.broadcasted_iota(jnp.int32, (2 * SSD_HEAD_DIM, 1), 0) < SSD_HEAD_DIM
    li = lax.broadcasted_iota(jnp.int32, (lc, lc), 0)
    si = lax.broadcasted_iota(jnp.int32, (lc, lc), 1)
    causal = si <= li
    tril = causal.astype(F32)
    jrow = lax.broadcasted_iota(jnp.int32, (ts, lc), 0)
    scol = lax.broadcasted_iota(jnp.int32, (ts, lc), 1)

    def pair(col, h):
        return jnp.where(first_head, col[:, h:h + 1], col[:, h + 1:h + 2])

    y_chunks = []
    for c in range(ts // lc):
        r0 = c * lc
        rows = slice(r0, r0 + lc)
        acum_c = jnp.dot(tril, a_c[rows], preferred_element_type=F32, precision=HI)
        upper = ((jrow >= r0) & (jrow <= r0 + scol)).astype(F32)
        acum_r = jnp.dot(a_r, upper, preferred_element_type=F32, precision=HI)
        last_c = acum_c[lc - 1:lc, :]
        dec_c = jnp.exp(last_c - acum_c)
        eac_c = jnp.exp(acum_c)
        y_pairs = []
        for g in range(SSD_GROUPS):
            b_g = xbc_c[rows, D_SSD + g * SSD_STATE:D_SSD + (g + 1) * SSD_STATE]
            c_g = xbc_c[rows, D_SSD + (SSD_GROUPS + g) * SSD_STATE:D_SSD + (SSD_GROUPS + g + 1) * SSD_STATE]
            cb = _dot_nt(c_g, b_g)
            for q in range(SSD_HPG // 2):
                h = g * SSD_HPG + 2 * q
                pi = h // 2
                xs_p = xs[rows, pi * LANES:(pi + 1) * LANES]
                xdt = xs_p * pair(dtp_c[rows], h)
                res = []
                for hh in (h, h + 1):
                    seg = acum_c[:, hh:hh + 1] - acum_r[hh:hh + 1, :]
                    m_h = cb * jnp.exp(jnp.where(causal, seg, -jnp.inf))
                    res.append(_dot(m_h, xdt))
                y_diag = jnp.where(first_head, res[0], res[1])
                h_in = hst[pi]
                y_off = _dot_nt(c_g, h_in) * pair(eac_c, h)
                st = _dot_tn(xdt * pair(dec_c, h), b_g)
                cd = jnp.exp(jnp.where(row2, last_c[:, h:h + 1], last_c[:, h + 1:h + 2]))
                hst[pi] = cd * h_in + st
                y_pairs.append(y_diag + y_off + pair(dsk_c, h) * xs_p)
        y_chunks.append(jnp.concatenate(y_pairs, axis=1))
    y = y_chunks[0] if len(y_chunks) == 1 else jnp.concatenate(y_chunks, axis=0)
    y_b = _rms(y * _silu(z), nw_ref[...])
    wout = wout_ref[...]
    out = (jnp.dot(y_a.astype(BF16), wout[0:D_CONV], preferred_element_type=F32)
           + jnp.dot(y_b.astype(BF16), wout[D_CONV:], preferred_element_type=F32))
    h1_ref[...] = x + out
    nh_ref[...] = hst[...]


def _mixer(x, hist, w, ts, b0=0, nb=None, after=None):
    _, s, _ = x.shape
    b = x.shape[0] if nb is None else nb
    lc = min(CHUNK, s)
    ts = min(ts, s)
    has_hist = hist is not None
    assert not (has_hist and b0)
    per_b3 = lambda shape: pl.BlockSpec((None,) + shape, lambda i, j: (i,) + (0,) * len(shape))
    in_specs = [pl.BlockSpec((None, ts, D_MODEL), lambda i, j: (i + b0, j, 0))]
    args = [x]
    if after is not None:
        in_specs.insert(0, pl.BlockSpec((SUBLANES, after.shape[1]), lambda i, j: (0, 0)))
        args.insert(0, after)
    if has_hist:
        in_specs += [per_b3((CONV_A_W - 1, D_CONV)), per_b3((SSD_CONV_W - 1, XBC_DIM)),
                     per_b3(PAIR_STATE)]
        args += list(hist)
    wargs = [w["ln_mix"], w["w_in"], w["w_dt"], w["w_dtt"], w["conv_a_w"], w["conv_s_w"], w["conv_s_b"],
             w["dtb_c"], w["alog_c"], w["dsk_c"], w["dtb_r"], w["alog_r"], w["ssd_norm"], w["w_out"]]
    in_specs += [_const_spec(a.shape) for a in wargs]
    args += wargs
    out_shape = (jax.ShapeDtypeStruct((b, s, D_MODEL), F32),
                 jax.ShapeDtypeStruct((b, CONV_A_W - 1, D_CONV), F32),
                 jax.ShapeDtypeStruct((b, SSD_CONV_W - 1, XBC_DIM), F32),
                 jax.ShapeDtypeStruct((b,) + PAIR_STATE, F32))
    out_specs = (pl.BlockSpec((None, ts, D_MODEL), lambda i, j: (i, j, 0)),
                 per_b3((CONV_A_W - 1, D_CONV)), per_b3((SSD_CONV_W - 1, XBC_DIM)),
                 per_b3(PAIR_STATE))
    return pl.pallas_call(
        functools.partial(_mixer_kernel, has_hist, after is not None, ts, lc),
        out_shape=out_shape, grid=(b, s // ts), in_specs=in_specs, out_specs=out_specs,
        scratch_shapes=[pltpu.VMEM((SUBLANES + ts, D_CONV), F32), pltpu.VMEM((SUBLANES + ts, XBC_DIM), F32),
                        pltpu.VMEM(PAIR_STATE, F32)],
        compiler_params=pltpu.CompilerParams(dimension_semantics=("parallel", "arbitrary"),
                                             vmem_limit_bytes=VMEM_LIMIT),
        name="mixer_hist" if has_hist else "mixer",
    )(*args)


def _memkv_kernel(m_ref, g_ref, wk_ref, wv_ref, k_ref, v_ref):
    mn = _rms(m_ref[...], g_ref[...]).astype(BF16)
    k_ref[...] = jnp.dot(mn, wk_ref[...], preferred_element_type=F32)
    v_ref[...] = jnp.dot(mn, wv_ref[...], preferred_element_type=F32)


def _memory_kv(mem, g, wk, wv, tm=256):
    t = mem.shape[0]
    blk = pl.BlockSpec((tm, D_MODEL), lambda i: (i, 0))
    return pl.pallas_call(
        _memkv_kernel,
        out_shape=(jax.ShapeDtypeStruct((t, D_MODEL), F32),) * 2,
        grid=(t // tm,), in_specs=[blk, _const_spec(g.shape), _const_spec(wk.shape), _const_spec(wv.shape)],
        out_specs=(blk, blk),
        compiler_params=pltpu.CompilerParams(dimension_semantics=("parallel",), vmem_limit_bytes=VMEM_LIMIT),
        name="memory_kv",
    )(mem, g, wk, wv)


def _attn_kernel(x_ref, k_ref, v_ref, g_ref, wq_ref, wo_ref, o_ref):
    x = x_ref[...]
    xn = _rms(x, g_ref[...]).astype(BF16)
    q = jnp.dot(xn, wq_ref[...], preferred_element_type=F32)
    by_head = len(k_ref.shape) == 3
    wo = wo_ref[...]
    acc = x
    for h in range(MEM_HEADS):
        cols = slice(h * MEM_HEAD_DIM, (h + 1) * MEM_HEAD_DIM)
        k_h = k_ref[:, h, :] if by_head else k_ref[:, cols]
        v_h = v_ref[:, h, :] if by_head else v_ref[:, cols]
        sc = _dot_nt(q[:, cols], k_h) * (MEM_HEAD_DIM ** -0.5)
        sc = sc - jnp.max(sc, axis=-1, keepdims=True)
        p = jnp.exp(sc)
        p = p / jnp.sum(p, axis=-1, keepdims=True)
        o_h = _dot(p, v_h)
        acc = acc + jnp.dot(o_h.astype(BF16), wo[cols, :], preferred_element_type=F32)
    o_ref[...] = acc


def _attn(x, k, v, g, wq, wo, ts, kv0=0):
    b, s, _ = x.shape
    ts = min(ts, s)
    xblk = pl.BlockSpec((None, ts, D_MODEL), lambda i, j: (i, j, 0))
    if k.ndim == 3:
        kvblk = pl.BlockSpec((None, N_MEM, D_MODEL), lambda i, j: (i + kv0, 0, 0))
    else:
        kvblk = pl.BlockSpec((None, N_MEM, MEM_HEADS, MEM_HEAD_DIM), lambda i, j: (i + kv0, 0, 0, 0))
    return pl.pallas_call(
        _attn_kernel,
        out_shape=jax.ShapeDtypeStruct((b, s, D_MODEL), F32),
        grid=(b, s // ts),
        in_specs=[xblk, kvblk, kvblk, _const_spec(g.shape), _const_spec(wq.shape), _const_spec(wo.shape)],
        out_specs=xblk,
        compiler_params=pltpu.CompilerParams(dimension_semantics=("parallel", "parallel"),
                                             vmem_limit_bytes=VMEM_LIMIT),
        name="attn",
    )(x, k, v, g, wq, wo)


def _tree_reduce(op, x):
    n = x.shape[0]
    while n > 1:
        half = n // 2
        head = op(x[:half], x[half:2 * half])
        x = head if n % 2 == 0 else jnp.concatenate([head, x[2 * half:]], axis=0)
        n = x.shape[0]
    return x


def _top_k_rows(sc, k, payload=None):
    n = sc.shape[0]
    rows = lax.broadcasted_iota(jnp.int32, sc.shape, 0).astype(F32)
    vals, outs = [], []
    for _ in range(k):
        m = _tree_reduce(jnp.maximum, sc)
        cand = jnp.where(sc == m, rows, float(n))
        first = _tree_reduce(jnp.minimum, cand)
        hit = cand == first
        sc = jnp.where(hit, -jnp.inf, sc)
        vals.append(m)
        if payload is None:
            outs.append(first.astype(jnp.int32))
        else:
            outs.append(jnp.sum(jnp.where(hit, payload, 0), axis=0, keepdims=True))
    return jnp.concatenate(vals, axis=0), jnp.concatenate(outs, axis=0)


def _pair_candidates(v1, i1, v2, i2):
    comb = [v1[a:a + 1] + v2[0:PEER_TOPK // (a + 1)] for a in range(PEER_TOPK)]
    cidx = [i1[a:a + 1] * N_KEYS + i2[0:PEER_TOPK // (a + 1)] for a in range(PEER_TOPK)]
    return jnp.concatenate(comb, axis=0), jnp.concatenate(cidx, axis=0)


SELECT_BLOCK = SUBLANES * LANES


def _select_kernel(tb, x_ref, g_ref, wq_ref, keys_ref, *rest):
    xn_ref, eid_ref, tile_ref, shift_ref, gate_ref, s1, s2, eid_s, gate_s = rest[-9:]
    groups = tb // LANES
    assert groups == SUBLANES
    xn = _rms(x_ref[...], g_ref[...])
    xn_ref[...] = xn
    xn = xn.astype(BF16)

    def head(h, carry):
        q = jnp.dot(xn, wq_ref[h], preferred_element_type=F32)
        for g in range(groups):
            qg = q[g * LANES:(g + 1) * LANES]
            s1[pl.ds(g, N_KEYS, stride=SUBLANES), :] = _dot_nt(keys_ref[h, 0], qg[:, 0:PEER_DK_HALF])
            s2[pl.ds(g, N_KEYS, stride=SUBLANES), :] = _dot_nt(keys_ref[h, 1], qg[:, PEER_DK_HALF:PEER_DK])
        v1, i1 = _top_k_rows(s1[...].reshape(N_KEYS, SUBLANES, LANES), PEER_TOPK)
        v2, i2 = _top_k_rows(s2[...].reshape(N_KEYS, SUBLANES, LANES), PEER_TOPK)
        comb, cidx = _pair_candidates(v1, i1, v2, i2)
        top, eid = _top_k_rows(comb, PEER_TOPK, cidx)
        ex = jnp.exp(top - top[0:1])
        gate = ex / jnp.sum(ex, axis=0, keepdims=True)
        rows = pl.ds(pl.multiple_of(h * PEER_TOPK * SUBLANES, PEER_TOPK * SUBLANES), PEER_TOPK * SUBLANES)
        gate_s[rows, :] = gate.reshape(PEER_TOPK * SUBLANES, LANES)
        eid_s[rows, :] = eid.reshape(PEER_TOPK * SUBLANES, LANES)
        return carry

    lax.fori_loop(0, PEER_HEADS, head, 0)
    for g in range(groups):
        toks = slice(g * LANES, (g + 1) * LANES)
        eid = eid_s[pl.ds(g, PICKS, stride=SUBLANES), :].T
        upper = eid >= HALF_EXPERTS
        eid_ref[toks, :] = eid
        tile_ref[toks, :] = jnp.where(upper, eid - HALF_EXPERTS, eid) * SUBLANES
        shift_ref[toks, :] = jnp.where(upper, 0, 16)
        gate_ref[toks, :] = gate_s[pl.ds(g, PICKS, stride=SUBLANES), :].T


def _select(x, g, wq, keys, tb, t0, t, after=None):
    b0 = t0 // tb
    extra = [] if after is None else [after]
    xin = pl.BlockSpec((tb, D_MODEL), lambda i: (i + b0, 0))
    xblk = pl.BlockSpec((tb, D_MODEL), lambda i: (i, 0))
    picks_i = jax.ShapeDtypeStruct((t, PICKS), jnp.int32)
    return pl.pallas_call(
        functools.partial(_select_kernel, tb),
        out_shape=(jax.ShapeDtypeStruct((t, D_MODEL), F32), picks_i, picks_i, picks_i,
                   jax.ShapeDtypeStruct((t, PICKS), F32)),
        grid=(t // tb,),
        in_specs=[xin, _const_spec(g.shape), _const_spec(wq.shape), _const_spec(keys.shape)]
        + [_after_spec(a) for a in extra],
        out_specs=(xblk,) + (_pick_block(tb),) * 4,
        scratch_shapes=[pltpu.VMEM((N_KEYS * SUBLANES, LANES), F32), pltpu.VMEM((N_KEYS * SUBLANES, LANES), F32),
                        pltpu.VMEM((PICKS * SUBLANES, LANES), jnp.int32), pltpu.VMEM((PICKS * SUBLANES, LANES), F32)],
        compiler_params=pltpu.CompilerParams(dimension_semantics=("parallel",), vmem_limit_bytes=VMEM_LIMIT),
        name="peer_select",
    )(x, g, wq, keys, *extra)


HI_HALF_MASK = -(1 << 16)


def _unpack(word, shift):
    return lax.bitcast_convert_type((word << shift) & HI_HALF_MASK, F32)


def _splat_picks(blk, one_pass):
    tb = blk.shape[0]
    r = lax.broadcasted_iota(jnp.int32, (PICKS, LANES), 0)
    c = lax.broadcasted_iota(jnp.int32, (PICKS, LANES), 1)
    eye = (r == c)[None]
    ones = jnp.ones((LANES, LANES), BF16)

    def one(piece):
        diag = jnp.where(eye, piece[:, None, :], 0.0).astype(BF16).reshape(tb * PICKS, LANES)
        return jnp.dot(diag, ones, preferred_element_type=F32)

    if one_pass:
        out = one(blk)
    else:
        hi = blk.astype(BF16).astype(F32)
        rest = blk - hi
        mid = rest.astype(BF16).astype(F32)
        out = one(hi) + one(mid) + one(rest - mid)
    return out.reshape(tb, PICKS, LANES)


def _full(ref, k):
    return jnp.broadcast_to(ref[k:k + 1, :], (SUBLANES, LANES))


def _expert_row(tab_ref, tile_ref, t, k, shifts):
    idx = pl.multiple_of(tile_ref[t, k], SUBLANES)
    return _unpack(tab_ref[pl.ds(idx, SUBLANES), :], _full(shifts, k))


def _hidden_kernel(tb, tile_ref, shift_ref, x_ref, gate_ref, tab_ref, *rest):
    gh_ref, bsh, stage, parts, xt = rest[-5:]
    bsh[...] = _splat_picks(shift_ref[...].astype(F32), True).astype(jnp.int32)
    for s in range(SUBLANES):
        xt[pl.ds(s, tb, stride=SUBLANES), :] = x_ref[:, s * LANES:(s + 1) * LANES]

    half = PICKS // 2
    quad = SUBLANES // 2
    low = lax.broadcasted_iota(jnp.int32, (SUBLANES, LANES), 0) < quad

    def token(t, carry):
        x = xt[pl.ds(pl.multiple_of(t * SUBLANES, SUBLANES), SUBLANES), :]
        shifts = bsh.at[t]
        for j in range(half):
            a = x * _expert_row(tab_ref, tile_ref, t, j, shifts)
            b = x * _expert_row(tab_ref, tile_ref, t, j + half, shifts)
            stage[j * SUBLANES:(j + 1) * SUBLANES, :] = jnp.where(low, a + pltpu.roll(a, quad, 0),
                                                                  b + pltpu.roll(b, quad, 0))
        for p in range(2):
            part = stage[pl.ds(p * quad, half, stride=SUBLANES), :]
            for s in range(1, quad):
                part = part + stage[pl.ds(p * quad + s, half, stride=SUBLANES), :]
            parts[t, p * half:(p + 1) * half, :] = part
        return carry

    lax.fori_loop(0, tb, token, 0)
    gh_ref[...] = jnp.sum(parts[...], axis=-1)
    hid = gh_ref[...]
    gelu = 0.5 * hid * (1.0 + lax.erf(hid * (2.0 ** -0.5)))
    gh_ref[...] = gate_ref[...] * gelu


def _pick_block(tb, space=None):
    return pl.BlockSpec((tb, PICKS), lambda i: (i, 0), memory_space=space)


def _table_spec(tab):
    return pl.BlockSpec(tab.shape, lambda i: (0, 0), pipeline_mode=pl.Buffered(1))


def _hidden(tile, shift, xg, gate, tab, tb, after=None):
    t = xg.shape[0]
    extra = [] if after is None else [after]
    return pl.pallas_call(
        functools.partial(_hidden_kernel, tb),
        out_shape=jax.ShapeDtypeStruct((t, PICKS), F32),
        grid=(t // tb,),
        in_specs=[_pick_block(tb, pltpu.SMEM), _pick_block(tb),
                  pl.BlockSpec((tb, D_MODEL), lambda i: (i, 0)),
                  _pick_block(tb), _table_spec(tab)] + [_after_spec(a) for a in extra],
        out_specs=_pick_block(tb),
        scratch_shapes=[pltpu.VMEM((tb, PICKS, LANES), jnp.int32), pltpu.VMEM((PICKS // 2 * SUBLANES, LANES), F32),
                        pltpu.VMEM((tb, PICKS, LANES), F32), pltpu.VMEM((tb * SUBLANES, LANES), F32)],
        compiler_params=pltpu.CompilerParams(dimension_semantics=("parallel",), vmem_limit_bytes=VMEM_LIMIT),
        name="peer_hidden",
    )(tile, shift, xg, gate, tab, *extra)


def _combine_kernel(tb, tile_ref, shift_ref, gh_ref, tab_ref, x_ref, g_ref, *rest):
    o_ref, bsh, bg, ot = rest[-4:]
    bsh[...] = _splat_picks(shift_ref[...].astype(F32), True).astype(jnp.int32)
    bg[...] = _splat_picks(gh_ref[...], False)

    def token(t, carry):
        shifts, gates = bsh.at[t], bg.at[t]
        acc = [jnp.zeros((SUBLANES, LANES), F32) for _ in range(4)]
        for k in range(PICKS):
            acc[k % 4] = acc[k % 4] + _full(gates, k) * _expert_row(tab_ref, tile_ref, t, k, shifts)
        ot[pl.ds(pl.multiple_of(t * SUBLANES, SUBLANES), SUBLANES), :] = (acc[0] + acc[1]) + (acc[2] + acc[3])
        return carry

    lax.fori_loop(0, tb, token, 0)
    for s in range(SUBLANES):
        o_ref[:, s * LANES:(s + 1) * LANES] = ot[pl.ds(s, tb, stride=SUBLANES), :]
    o_ref[...] = _rms(x_ref[...] + o_ref[...], g_ref[...])


def _combine(tile, shift, gh, tab, x, g, tb, x_row0, out_row0, out_rows, into=None):
    t = gh.shape[0]
    xb0, ob0 = x_row0 // tb, out_row0 // tb
    extra = [] if into is None else [into]
    return pl.pallas_call(
        functools.partial(_combine_kernel, tb),
        out_shape=jax.ShapeDtypeStruct((out_rows, D_MODEL), F32),
        grid=(t // tb,),
        in_specs=[_pick_block(tb, pltpu.SMEM), _pick_block(tb), _pick_block(tb), _table_spec(tab),
                  pl.BlockSpec((tb, D_MODEL), lambda i: (i + xb0, 0)), _const_spec(g.shape)]
        + [pl.BlockSpec(memory_space=pl.ANY) for _ in extra],
        input_output_aliases={6: 0} if extra else {},
        out_specs=pl.BlockSpec((tb, D_MODEL), lambda i: (i + ob0, 0)),
        scratch_shapes=[pltpu.VMEM((tb, PICKS, LANES), jnp.int32), pltpu.VMEM((tb, PICKS, LANES), F32),
                        pltpu.VMEM((tb * SUBLANES, LANES), F32)],
        compiler_params=pltpu.CompilerParams(dimension_semantics=("parallel",), vmem_limit_bytes=VMEM_LIMIT),
        name="peer_combine",
    )(tile, shift, gh, tab, x, g, *extra)


SC_LANES = 16
SC_WORKERS = 32
SC_CHUNK = 32
SC_FEATURE_BLOCK = 256
SC_EARLY_SEQS = 3
SC_PROMPT_TOKENS = 13312
SC_COMBINE_HANDBACK = 1024


def _sc_call(body, n, side_width, out_width):
    mesh = plsc.VectorSubcoreMesh(core_axis_name="c", subcore_axis_name="s")
    return pl.kernel(body, mesh=mesh, out_type=jax.ShapeDtypeStruct((n, out_width), F32),
                     compiler_params=pltpu.CompilerParams(needs_layout_passes=False),
                     scratch_types=[pltpu.VMEM((2 * PICKS,), jnp.int32), pltpu.VMEM((2 * side_width,), F32),
                                    pltpu.VMEM((2 * out_width,), F32),
                                    pltpu.VMEM((SC_CHUNK, D_MODEL), F32), pltpu.VMEM((SC_CHUNK, D_MODEL), F32)]
                     + [pltpu.SemaphoreType.DMA] * 5)


def _sc_token_loop(n, eid_hbm, side_hbm, tab_hbm, out_hbm, idx_v, side_v, out_v, bufs, sems, compute):
    sem_rows, sem_fetch, sem_put = sems[0:2], sems[2], sems[3:5]
    per_worker = n // SC_WORKERS
    first = (lax.axis_index("s") * 2 + lax.axis_index("c")) * per_worker
    last = first + per_worker - 1
    chunks = PICKS // SC_CHUNK
    side_w = side_v.shape[0] // 2
    out_w = out_v.shape[0] // 2

    def fetch(t, slot):
        return (pltpu.make_async_copy(eid_hbm.at[t], idx_v.at[pl.ds(slot * PICKS, PICKS)], sem_fetch),
                pltpu.make_async_copy(side_hbm.at[t], side_v.at[pl.ds(slot * side_w, side_w)], sem_fetch))

    def gather(slot, c):
        ids = idx_v.at[pl.ds(slot * PICKS + c * SC_CHUNK, SC_CHUNK)]
        return pltpu.make_async_copy(tab_hbm.at[ids], bufs[c % 2], sem_rows[c % 2])

    def put(t, slot):
        return pltpu.make_async_copy(out_v.at[pl.ds(slot * out_w, out_w)], out_hbm.at[t], sem_put[slot])

    for cp in fetch(first, 0):
        cp.start()
    for cp in fetch(first, 0):
        cp.wait()
    gather(0, 0).start()
    gather(0, 1).start()

    @pl.loop(0, per_worker, step=2)
    def _(i):
        for slot in range(2):
            t = first + i + slot
            nxt = jnp.minimum(t + 1, last)
            for cp in fetch(nxt, 1 - slot):
                cp.start()

            @pl.when(i >= 2)
            def _():
                put(t, slot).wait()

            for c in range(chunks):
                gather(slot, c).wait()
                compute(c, bufs[c % 2], slot * side_w, slot * out_w)
                if c + 2 < chunks:
                    gather(slot, c + 2).start()
                else:
                    if c + 2 == chunks:
                        for cp in fetch(nxt, 1 - slot):
                            cp.wait()
                    gather(1 - slot, c + 2 - chunks).start()
            put(t, slot).start()

    gather(0, 0).wait()
    gather(0, 1).wait()
    put(last, 0).wait()
    put(last, 1).wait()


def _sc_hidden(eid, x, tab):
    n = eid.shape[0]
    lanes = SC_LANES

    def body(eid_hbm, x_hbm, tab_hbm, out_hbm, idx_v, x_v, hid_v, rows0, rows1, *sems):
        lane = lax.iota(jnp.int32, lanes)

        def compute(c, rows, x0, h0):
            for g in range(SC_CHUNK // lanes):
                def step(j, accs):
                    off = pl.multiple_of(j * lanes, lanes)
                    xj = x_v[pl.ds(x0 + off, lanes)]
                    return tuple(accs[r] + xj * rows[g * lanes + r, pl.ds(off, lanes)] for r in range(lanes))
                accs = lax.fori_loop(0, D_MODEL // lanes, step,
                                     tuple(jnp.zeros((lanes,), F32) for _ in range(lanes)))
                hv = jnp.zeros((lanes,), F32)
                for r in range(lanes):
                    hv = jnp.where(lane == r, jnp.sum(accs[r]), hv)
                hid_v[pl.ds(h0 + c * SC_CHUNK + g * lanes, lanes)] = hv

        _sc_token_loop(n, eid_hbm, x_hbm, tab_hbm, out_hbm, idx_v, x_v, hid_v, (rows0, rows1), sems, compute)

    return _sc_call(body, n, D_MODEL, PICKS)(eid, x, tab)


def _sc_combine(eid, gh, tab, n=None):
    n = eid.shape[0] if n is None else n
    lanes = SC_LANES
    nacc = SC_FEATURE_BLOCK // lanes

    def body(eid_hbm, g_hbm, tab_hbm, out_hbm, idx_v, g_v, out_v, rows0, rows1, *sems):
        def compute(c, rows, g0, o0):
            for fb in range(D_MODEL // SC_FEATURE_BLOCK):
                f0 = fb * SC_FEATURE_BLOCK
                if c == 0:
                    init = tuple(jnp.zeros((lanes,), F32) for _ in range(nacc))
                else:
                    init = tuple(out_v[pl.ds(o0 + f0 + q * lanes, lanes)] for q in range(nacc))

                def step(r, accs):
                    gk = plsc.load_gather(g_v, [jnp.full((lanes,), g0 + c * SC_CHUNK + r, jnp.int32)])
                    return tuple(accs[q] + gk * rows[r, pl.ds(f0 + q * lanes, lanes)] for q in range(nacc))
                accs = lax.fori_loop(0, SC_CHUNK, step, init)
                for q in range(nacc):
                    out_v[pl.ds(o0 + f0 + q * lanes, lanes)] = accs[q]

        _sc_token_loop(n, eid_hbm, g_hbm, tab_hbm, out_hbm, idx_v, g_v, out_v, (rows0, rows1), sems, compute)

    return _sc_call(body, n, PICKS, D_MODEL)(eid, gh, tab)


def _after_spec(after):
    return pl.BlockSpec((SUBLANES, after.shape[1]), lambda i: (0, 0))


def _gate_gelu_kernel(h_ref, g_ref, after_ref, o_ref):
    hid = h_ref[...]
    o_ref[...] = g_ref[...] * (0.5 * hid * (1.0 + lax.erf(hid * (2.0 ** -0.5))))


def _gate_gelu(hid, gate, tb, after):
    t = hid.shape[0]
    return pl.pallas_call(
        _gate_gelu_kernel, out_shape=jax.ShapeDtypeStruct((t, PICKS), F32), grid=(t // tb,),
        in_specs=[_pick_block(tb), _pick_block(tb), _after_spec(after)], out_specs=_pick_block(tb),
        compiler_params=pltpu.CompilerParams(dimension_semantics=("parallel",)),
        name="gate_gelu",
    )(hid, gate, after)


def _final_kernel(x_ref, p_ref, g_ref, *rest):
    rest[-1][...] = _rms(x_ref[...] + p_ref[...], g_ref[...])


def _final(x, p, g, tb, out_row0, out_rows, into=None):
    b0 = out_row0 // tb
    local = pl.BlockSpec((tb, D_MODEL), lambda i: (i, 0))
    in_specs = [local, local, _const_spec(g.shape)]
    args = [x, p, g]
    aliases = {}
    if into is not None:
        aliases = {len(args): 0}
        in_specs.append(pl.BlockSpec(memory_space=pl.ANY))
        args.append(into)
    return pl.pallas_call(
        _final_kernel, out_shape=jax.ShapeDtypeStruct((out_rows, D_MODEL), F32), grid=(p.shape[0] // tb,),
        in_specs=in_specs, out_specs=pl.BlockSpec((tb, D_MODEL), lambda i: (i + b0, 0)),
        input_output_aliases=aliases,
        compiler_params=pltpu.CompilerParams(dimension_semantics=("parallel",), vmem_limit_bytes=VMEM_LIMIT),
        name="final_norm",
    )(*args)


PACK_BLOCK = 256


def _pack_kernel(lo_ref, hi_ref, o_ref):
    def bf16_bits(x):
        return lax.bitcast_convert_type(x.astype(BF16).astype(F32), jnp.int32)
    word = bf16_bits(hi_ref[...]) | lax.shift_right_logical(bf16_bits(lo_ref[...]), 16)
    for s in range(SUBLANES):
        o_ref[pl.ds(s, PACK_BLOCK, stride=SUBLANES), :] = word[:, s * LANES:(s + 1) * LANES]


def _pack_table(tab):
    steps = HALF_EXPERTS // PACK_BLOCK
    return pl.pallas_call(
        _pack_kernel, out_shape=jax.ShapeDtypeStruct((HALF_EXPERTS * SUBLANES, LANES), jnp.int32), grid=(steps,),
        in_specs=[pl.BlockSpec((PACK_BLOCK, D_MODEL), lambda i: (i, 0)),
                  pl.BlockSpec((PACK_BLOCK, D_MODEL), lambda i: (i + steps, 0))],
        out_specs=pl.BlockSpec((PACK_BLOCK * SUBLANES, LANES), lambda i: (i, 0)),
        compiler_params=pltpu.CompilerParams(dimension_semantics=("parallel",)),
        name="pack_table",
    )(tab, tab)


def _col(v, n=LANES):
    return jnp.pad(v.astype(F32), (0, n - v.shape[0])).reshape(1, n)


def _row(v, n=2 * SUBLANES):
    return jnp.pad(v.astype(F32), (0, n - v.shape[0])).reshape(n, 1)


def _peer(x, w, tb_sel, tb_exp, n_sc, out_row0=0, out_rows=None, after=None, into=None, between=None):
    t = x.shape[0]
    out_rows = t if out_rows is None else out_rows
    n_tc = t - n_sc
    n_sc2 = max(n_sc - SC_COMBINE_HANDBACK, 0)
    if n_sc:
        xn_s, eid_s, tile_s, shift_s, gate_s = _select(x, w["ln_ffn"], w["w_pq"], w["sub_keys"], SELECT_BLOCK, 0,
                                                       n_sc, after=after)
        hid_s = _sc_hidden(eid_s, xn_s, w["u_tab"])
    xn, _, tile, shift, gate = _select(x, w["ln_ffn"], w["w_pq"], w["sub_keys"], SELECT_BLOCK,n_sc, n_tc,
                                       after=xn_s if n_sc else None)
    gh = _hidden(tile, shift, xn, gate, w["u_pack"], tb_exp, after=into)
    if n_sc:
        other = None if between is None else between(gh)
        gh_s = _gate_gelu(hid_s, gate_s, tb_sel, after=gh if other is None else other)
        out_s = _sc_combine(eid_s, gh_s, w["v_tab"], n_sc2)
    y = _combine(tile, shift, gh, w["v_pack"], x, w["final_norm"], tb_exp, n_sc, out_row0 + n_sc, out_rows, into)
    if n_sc > n_sc2:
        y = _combine(tile_s[n_sc2:], shift_s[n_sc2:], gh_s[n_sc2:], w["v_pack"], x, w["final_norm"], tb_exp,
                     n_sc2, out_row0 + n_sc2, out_rows, y)
    if n_sc2:
        y = _final(x, out_s, w["final_norm"], tb_sel, out_row0, out_rows, into=y)
    return (y, other) if between is not None else y


def kernel(x_prompt, x_sample, mem_prompt, cache_conv_a, cache_conv_ssd, state_ssd, cache_mem_k, cache_mem_v, ln_mix_w, w_in, conv_a_w, conv_s_w, conv_s_b, dt_bias, a_log, d_skip, ssd_norm_w, w_out, ln_mem_w, mem_norm_w, w_mq, w_mk, w_mv, w_mo, ln_ffn_w, w_pq, sub_keys, u_tab, v_tab, final_norm_w):
    depth = w_in.shape[0]
    assert depth == 1
    l = 0
    n_main = 3 * D_CONV + D_SSD + XBC_DIM
    w_dt = w_in[l][:, n_main:]
    w = {
        "ln_mix": ln_mix_w[l].reshape(1, D_MODEL),
        "w_in": w_in[l][:, :n_main].astype(BF16),
        "w_dt": jnp.pad(w_dt, ((0, 0), (0, LANES - SSD_HEADS))).astype(BF16),
        "w_dtt": jnp.pad(w_dt.T, ((0, 2 * SUBLANES - SSD_HEADS), (0, 0))).astype(BF16),
        "conv_a_w": conv_a_w[l], "conv_s_w": conv_s_w[l], "conv_s_b": conv_s_b[l].reshape(1, XBC_DIM),
        "dtb_c": _col(dt_bias[l]), "alog_c": _col(a_log[l]), "dsk_c": _col(d_skip[l]),
        "dtb_r": _row(dt_bias[l]), "alog_r": _row(a_log[l]),
        "ssd_norm": ssd_norm_w[l].reshape(1, D_SSD),
        "w_out": w_out[l].astype(BF16),
        "ln_ffn": ln_ffn_w[l].reshape(1, D_MODEL),
        "w_pq": w_pq[l].astype(BF16).reshape(D_MODEL, PEER_HEADS, PEER_DK).transpose(1, 0, 2),
        "sub_keys": sub_keys[l].astype(BF16),
        "u_pack": _pack_table(u_tab[l]),
        "v_pack": _pack_table(v_tab[l]),
        "u_tab": u_tab[l], "v_tab": v_tab[l],
        "final_norm": final_norm_w.reshape(1, D_MODEL),
    }
    ln_mem = ln_mem_w[l].reshape(1, D_MODEL)
    wq, wo = w_mq[l].astype(BF16), w_mo[l].astype(BF16)

    bp, sp, _ = x_prompt.shape
    bs, ss, _ = x_sample.shape

    mk, mv = _memory_kv(mem_prompt.reshape(bp * N_MEM, D_MODEL), mem_norm_w[l].reshape(1, D_MODEL),
                        w_mk[l].astype(BF16), w_mv[l].astype(BF16))
    mk3, mv3 = mk.reshape(bp, N_MEM, D_MODEL), mv.reshape(bp, N_MEM, D_MODEL)
    n1 = SC_EARLY_SEQS
    h1, pa1, ps1, ph1 = _mixer(x_prompt, None, w, 256, 0, n1)
    h1 = _attn(h1, mk3, mv3, ln_mem, wq, wo, 256).reshape(n1 * sp, D_MODEL)

    xn1, eid1, _, _, gate1 = _select(h1, w["ln_ffn"], w["w_pq"], w["sub_keys"], SELECT_BLOCK,0, n1 * sp)
    hid1 = _sc_hidden(eid1, xn1, w["u_tab"])
    h2, pa2, ps2, ph2 = _mixer(x_prompt, None, w, 256, n1, bp - n1, after=xn1)
    h2 = _attn(h2, mk3, mv3, ln_mem, wq, wo, 256, kv0=n1).reshape((bp - n1) * sp, D_MODEL)

    hs, sa, ssd, sh = _mixer(x_sample, (cache_conv_a[l], cache_conv_ssd[l],
                                        state_ssd[l].reshape((bs,) + PAIR_STATE)), w, 256, after=h2)
    hs = _attn(hs, cache_mem_k[l], cache_mem_v[l], ln_mem, wq, wo, 256).reshape(bs * ss, D_MODEL)
    xn_q, _, tile_q, shift_q, gate_q = _select(hs, w["ln_ffn"], w["w_pq"], w["sub_keys"], SELECT_BLOCK,0, bs * ss)

    gh1 = _gate_gelu(hid1, gate1, 256, after=xn_q)
    out1 = _sc_combine(eid1, gh1, w["v_tab"])
    y1 = _final(h1, out1, w["final_norm"], 256, 0, bp * sp)
    def sample_experts(gh_main):
        gh_q = _hidden(tile_q, shift_q, xn_q, gate_q, w["u_pack"], 64, after=gh_main)
        return _combine(tile_q, shift_q, gh_q, w["v_pack"], hs, w["final_norm"], 64, 0, 0, bs * ss)

    y_prompt, y_sample = _peer(h2, w, 256, 64, SC_PROMPT_TOKENS, n1 * sp, bp * sp, after=gh1, into=y1,
                               between=sample_experts)
    y_prompt, y_sample = y_prompt.reshape(bp, sp, D_MODEL), y_sample.reshape(bs, ss, D_MODEL)
    pa, ps, ph = (jnp.concatenate(p, axis=0) for p in ((pa1, pa2), (ps1, ps2), (ph1, ph2)))

    kv_shape = (1, bp, N_MEM, MEM_HEADS, MEM_HEAD_DIM)
    st_shape = (SSD_HEADS, SSD_HEAD_DIM, SSD_STATE)
    return (y_prompt, y_sample, pa[None], ps[None], ph.reshape((1, bp) + st_shape), mk.reshape(kv_shape),
            mv.reshape(kv_shape), sa[None], ssd[None], sh.reshape((1, bs) + st_shape))
```

```python
import functools

import jax
import jax.numpy as jnp
from jax import lax
from jax.experimental import pallas as pl
from jax.experimental.pallas import tpu as pltpu
from jax.experimental.pallas import tpu_sc as plsc

D_MODEL = 1024
CHUNK = 64
D_CONV = 512
CONV_A_W = 3
D_SSD = 512
SSD_HEAD_DIM = 64
SSD_HEADS = 8
SSD_GROUPS = 2
SSD_HPG = 4
SSD_STATE = 128
SSD_CONV_W = 4
XBC_DIM = 1024
N_MEM = 256
MEM_HEADS = 4
MEM_HEAD_DIM = 256
PEER_HEADS = 8
N_KEYS = 128
N_EXPERTS = N_KEYS * N_KEYS
PEER_TOPK = 16
PEER_DK = 256
PEER_DK_HALF = 128
PICKS = PEER_HEADS * PEER_TOPK
EPS = 1e-6

LANES = 128
SUBLANES = 8
HALF_EXPERTS = N_EXPERTS // 2
PAIR_STATE = (SSD_HEADS // 2, 2 * SSD_HEAD_DIM, SSD_STATE)
VMEM_LIMIT = 56 * 1024 * 1024

F32 = jnp.float32
BF16 = jnp.bfloat16
HI = lax.Precision.HIGHEST


def _rms(x, g):
    return x * lax.rsqrt(jnp.mean(x * x, axis=-1, keepdims=True) + EPS) * g


def _silu(x):
    return x * (1.0 / (1.0 + jnp.exp(-x)))


def _softplus(x):
    return jnp.maximum(x, 0.0) + jnp.log1p(jnp.exp(-jnp.abs(x)))


def _dot(a, b):
    return jnp.dot(a.astype(BF16), b.astype(BF16), preferred_element_type=F32)


def _dot_nt(a, b):
    return lax.dot_general(a.astype(BF16), b.astype(BF16), (((1,), (1,)), ((), ())),
                           preferred_element_type=F32)


def _dot_tn(a, b):
    return lax.dot_general(a.astype(BF16), b.astype(BF16), (((0,), (0,)), ((), ())),
                           preferred_element_type=F32)


def _const_spec(shape):
    n = len(shape)
    return pl.BlockSpec(shape, lambda *_: (0,) * n)


def _mixer_kernel(has_hist, has_after, ts, lc, *refs):
    if has_after:
        refs = refs[1:]
    if has_hist:
        x_ref, hista_ref, hists_ref, h0_ref = refs[:4]
        refs = refs[4:]
    else:
        x_ref = refs[0]
        refs = refs[1:]
    (lnw_ref, win_ref, wdt_ref, wdtt_ref, caw_ref, csw_ref, csb_ref, dtb_c_ref, alog_c_ref, dsk_c_ref,
     dtb_r_ref, alog_r_ref, nw_ref, wout_ref,
     h1_ref, na_ref, ns_ref, nh_ref, cata, cats, hst) = refs
    s = pl.program_id(1)
    pad = SUBLANES

    @pl.when(s == 0)
    def _():
        if has_hist:
            cata[pad - 2:pad, :] = hista_ref[...]
            cats[pad - 3:pad, :] = hists_ref[...]
            hst[...] = h0_ref[...]
        else:
            cata[0:pad, :] = jnp.zeros((pad, D_CONV), F32)
            cats[0:pad, :] = jnp.zeros((pad, XBC_DIM), F32)
            hst[...] = jnp.zeros(hst.shape, F32)

    x = x_ref[...]
    xn = _rms(x, lnw_ref[...]).astype(BF16)
    proj = jnp.dot(xn, win_ref[...], preferred_element_type=F32)
    dt_c = jnp.dot(xn, wdt_ref[...], preferred_element_type=F32)
    dt_r = lax.dot_general(wdtt_ref[...], xn, (((1,), (1,)), ((), ())),
                           preferred_element_type=F32)
    g_b = proj[:, 0:D_CONV]
    g_c = proj[:, D_CONV:2 * D_CONV]
    v_in = proj[:, 2 * D_CONV:3 * D_CONV]
    z = proj[:, 3 * D_CONV:3 * D_CONV + D_SSD]
    xbc = proj[:, 3 * D_CONV + D_SSD:3 * D_CONV + D_SSD + XBC_DIM]

    ua = g_c * v_in
    cata[pad:pad + ts, :] = ua
    caw = caw_ref[...]
    conv_a = (caw[0:1] * cata[pad - 2:pad - 2 + ts, :] + caw[1:2] * cata[pad - 1:pad - 1 + ts, :]
              + caw[2:3] * ua)
    y_a = g_b * conv_a
    hist_a = cata[pad + ts - 2:pad + ts, :]
    na_ref[...] = hist_a
    cata[pad - 2:pad, :] = hist_a

    cats[pad:pad + ts, :] = xbc
    csw = csw_ref[...]
    conv_s = (csw[0:1] * cats[pad - 3:pad - 3 + ts, :] + csw[1:2] * cats[pad - 2:pad - 2 + ts, :]
              + csw[2:3] * cats[pad - 1:pad - 1 + ts, :] + csw[3:4] * xbc)
    hist_s = cats[pad + ts - 3:pad + ts, :]
    ns_ref[...] = hist_s
    cats[pad - 3:pad, :] = hist_s
    xbc_c = _silu(conv_s + csb_ref[...])
    xs = xbc_c[:, 0:D_SSD]

    dtp_c = _softplus(dt_c + dtb_c_ref[...])
    dtp_r = _softplus(dt_r + dtb_r_ref[...])
    a_c = dtp_c * (-jnp.exp(alog_c_ref[...]))
    a_r = dtp_r * (-jnp.exp(alog_r_ref[...]))
    dsk_c = dsk_c_ref[...]

    lane = lax.broadcasted_iota(jnp.int32, (1, LANES), 1)
    first_head = lane < SSD_HEAD_DIM
    row2 = lax.broadcasted_iota(jnp.int32, (2 * SSD_HEAD_DIM, 1), 0) < SSD_HEAD_DIM
    li = lax.broadcasted_iota(jnp.int32, (lc, lc), 0)
    si = lax.broadcasted_iota(jnp.int32, (lc, lc), 1)
    causal = si <= li
    tril = causal.astype(F32)
    jrow = lax.broadcasted_iota(jnp.int32, (ts, lc), 0)
    scol = lax.broadcasted_iota(jnp.int32, (ts, lc), 1)

    def pair(col, h):
        return jnp.where(first_head, col[:, h:h + 1], col[:, h + 1:h + 2])

    y_chunks = []
    for c in range(ts // lc):
        r0 = c * lc
        rows = slice(r0, r0 + lc)
        acum_c = jnp.dot(tril, a_c[rows], preferred_element_type=F32, precision=HI)
        upper = ((jrow >= r0) & (jrow <= r0 + scol)).astype(F32)
        acum_r = jnp.dot(a_r, upper, preferred_element_type=F32, precision=HI)
        last_c = acum_c[lc - 1:lc, :]
        dec_c = jnp.exp(last_c - acum_c)
        eac_c = jnp.exp(acum_c)
        y_pairs = []
        for g in range(SSD_GROUPS):
            b_g = xbc_c[rows, D_SSD + g * SSD_STATE:D_SSD + (g + 1) * SSD_STATE]
            c_g = xbc_c[rows, D_SSD + (SSD_GROUPS + g) * SSD_STATE:D_SSD + (SSD_GROUPS + g + 1) * SSD_STATE]
            cb = _dot_nt(c_g, b_g)
            for q in range(SSD_HPG // 2):
                h = g * SSD_HPG + 2 * q
                pi = h // 2
                xs_p = xs[rows, pi * LANES:(pi + 1) * LANES]
                xdt = xs_p * pair(dtp_c[rows], h)
                res = []
                for hh in (h, h + 1):
                    seg = acum_c[:, hh:hh + 1] - acum_r[hh:hh + 1, :]
                    m_h = cb * jnp.exp(jnp.where(causal, seg, -jnp.inf))
                    res.append(_dot(m_h, xdt))
                y_diag = jnp.where(first_head, res[0], res[1])
                h_in = hst[pi]
                y_off = _dot_nt(c_g, h_in) * pair(eac_c, h)
                st = _dot_tn(xdt * pair(dec_c, h), b_g)
                cd = jnp.exp(jnp.where(row2, last_c[:, h:h + 1], last_c[:, h + 1:h + 2]))
                hst[pi] = cd * h_in + st
                y_pairs.append(y_diag + y_off + pair(dsk_c, h) * xs_p)
        y_chunks.append(jnp.concatenate(y_pairs, axis=1))
    y = y_chunks[0] if len(y_chunks) == 1 else jnp.concatenate(y_chunks, axis=0)
    y_b = _rms(y * _silu(z), nw_ref[...])
    wout = wout_ref[...]
    out = (jnp.dot(y_a.astype(BF16), wout[0:D_CONV], preferred_element_type=F32)
           + jnp.dot(y_b.astype(BF16), wout[D_CONV:], preferred_element_type=F32))
    h1_ref[...] = x + out
    nh_ref[...] = hst[...]


def _mixer(x, hist, w, ts, b0=0, nb=None, after=None):
    _, s, _ = x.shape
    b = x.shape[0] if nb is None else nb
    lc = min(CHUNK, s)
    ts = min(ts, s)
    has_hist = hist is not None
    assert not (has_hist and b0)
    per_b3 = lambda shape: pl.BlockSpec((None,) + shape, lambda i, j: (i,) + (0,) * len(shape))
    in_specs = [pl.BlockSpec((None, ts, D_MODEL), lambda i, j: (i + b0, j, 0))]
    args = [x]
    if after is not None:
        in_specs.insert(0, pl.BlockSpec((SUBLANES, after.shape[1]), lambda i, j: (0, 0)))
        args.insert(0, after)
    if has_hist:
        in_specs += [per_b3((CONV_A_W - 1, D_CONV)), per_b3((SSD_CONV_W - 1, XBC_DIM)),
                     per_b3(PAIR_STATE)]
        args += list(hist)
    wargs = [w["ln_mix"], w["w_in"], w["w_dt"], w["w_dtt"], w["conv_a_w"], w["conv_s_w"], w["conv_s_b"],
             w["dtb_c"], w["alog_c"], w["dsk_c"], w["dtb_r"], w["alog_r"], w["ssd_norm"], w["w_out"]]
    in_specs += [_const_spec(a.shape) for a in wargs]
    args += wargs
    out_shape = (jax.ShapeDtypeStruct((b, s, D_MODEL), F32),
                 jax.ShapeDtypeStruct((b, CONV_A_W - 1, D_CONV), F32),
                 jax.ShapeDtypeStruct((b, SSD_CONV_W - 1, XBC_DIM), F32),
                 jax.ShapeDtypeStruct((b,) + PAIR_STATE, F32))
    out_specs = (pl.BlockSpec((None, ts, D_MODEL), lambda i, j: (i, j, 0)),
                 per_b3((CONV_A_W - 1, D_CONV)), per_b3((SSD_CONV_W - 1, XBC_DIM)),
                 per_b3(PAIR_STATE))
    return pl.pallas_call(
        functools.partial(_mixer_kernel, has_hist, after is not None, ts, lc),
        out_shape=out_shape, grid=(b, s // ts), in_specs=in_specs, out_specs=out_specs,
        scratch_shapes=[pltpu.VMEM((SUBLANES + ts, D_CONV), F32), pltpu.VMEM((SUBLANES + ts, XBC_DIM), F32),
                        pltpu.VMEM(PAIR_STATE, F32)],
        compiler_params=pltpu.CompilerParams(dimension_semantics=("parallel", "arbitrary"),
                                             vmem_limit_bytes=VMEM_LIMIT),
        name="mixer_hist" if has_hist else "mixer",
    )(*args)


def _memkv_kernel(m_ref, g_ref, wk_ref, wv_ref, k_ref, v_ref):
    mn = _rms(m_ref[...], g_ref[...]).astype(BF16)
    k_ref[...] = jnp.dot(mn, wk_ref[...], preferred_element_type=F32)
    v_ref[...] = jnp.dot(mn, wv_ref[...], preferred_element_type=F32)


def _memory_kv(mem, g, wk, wv, tm=256):
    t = mem.shape[0]
    blk = pl.BlockSpec((tm, D_MODEL), lambda i: (i, 0))
    return pl.pallas_call(
        _memkv_kernel,
        out_shape=(jax.ShapeDtypeStruct((t, D_MODEL), F32),) * 2,
        grid=(t // tm,), in_specs=[blk, _const_spec(g.shape), _const_spec(wk.shape), _const_spec(wv.shape)],
        out_specs=(blk, blk),
        compiler_params=pltpu.CompilerParams(dimension_semantics=("parallel",), vmem_limit_bytes=VMEM_LIMIT),
        name="memory_kv",
    )(mem, g, wk, wv)


def _attn_kernel(x_ref, k_ref, v_ref, g_ref, wq_ref, wo_ref, o_ref):
    x = x_ref[...]
    xn = _rms(x, g_ref[...]).astype(BF16)
    q = jnp.dot(xn, wq_ref[...], preferred_element_type=F32)
    by_head = len(k_ref.shape) == 3
    wo = wo_ref[...]
    acc = x
    for h in range(MEM_HEADS):
        cols = slice(h * MEM_HEAD_DIM, (h + 1) * MEM_HEAD_DIM)
        k_h = k_ref[:, h, :] if by_head else k_ref[:, cols]
        v_h = v_ref[:, h, :] if by_head else v_ref[:, cols]
        sc = _dot_nt(q[:, cols], k_h) * (MEM_HEAD_DIM ** -0.5)
        sc = sc - jnp.max(sc, axis=-1, keepdims=True)
        p = jnp.exp(sc)
        p = p / jnp.sum(p, axis=-1, keepdims=True)
        o_h = _dot(p, v_h)
        acc = acc + jnp.dot(o_h.astype(BF16), wo[cols, :], preferred_element_type=F32)
    o_ref[...] = acc


def _attn(x, k, v, g, wq, wo, ts, kv0=0):
    b, s, _ = x.shape
    ts = min(ts, s)
    xblk = pl.BlockSpec((None, ts, D_MODEL), lambda i, j: (i, j, 0))
    if k.ndim == 3:
        kvblk = pl.BlockSpec((None, N_MEM, D_MODEL), lambda i, j: (i + kv0, 0, 0))
    else:
        kvblk = pl.BlockSpec((None, N_MEM, MEM_HEADS, MEM_HEAD_DIM), lambda i, j: (i + kv0, 0, 0, 0))
    return pl.pallas_call(
        _attn_kernel,
        out_shape=jax.ShapeDtypeStruct((b, s, D_MODEL), F32),
        grid=(b, s // ts),
        in_specs=[xblk, kvblk, kvblk, _const_spec(g.shape), _const_spec(wq.shape), _const_spec(wo.shape)],
        out_specs=xblk,
        compiler_params=pltpu.CompilerParams(dimension_semantics=("parallel", "parallel"),
                                             vmem_limit_bytes=VMEM_LIMIT),
        name="attn",
    )(x, k, v, g, wq, wo)


def _tree_reduce(op, x):
    n = x.shape[0]
    while n > 1:
        half = n // 2
        head = op(x[:half], x[half:2 * half])
        x = head if n % 2 == 0 else jnp.concatenate([head, x[2 * half:]], axis=0)
        n = x.shape[0]
    return x


def _top_k_rows(sc, k, payload=None):
    n = sc.shape[0]
    rows = lax.broadcasted_iota(jnp.int32, sc.shape, 0).astype(F32)
    vals, outs = [], []
    for _ in range(k):
        m = _tree_reduce(jnp.maximum, sc)
        cand = jnp.where(sc == m, rows, float(n))
        first = _tree_reduce(jnp.minimum, cand)
        hit = cand == first
        sc = jnp.where(hit, -jnp.inf, sc)
        vals.append(m)
        if payload is None:
            outs.append(first.astype(jnp.int32))
        else:
            outs.append(jnp.sum(jnp.where(hit, payload, 0), axis=0, keepdims=True))
    return jnp.concatenate(vals, axis=0), jnp.concatenate(outs, axis=0)


def _pair_candidates(v1, i1, v2, i2):
    comb = [v1[a:a + 1] + v2[0:PEER_TOPK // (a + 1)] for a in range(PEER_TOPK)]
    cidx = [i1[a:a + 1] * N_KEYS + i2[0:PEER_TOPK // (a + 1)] for a in range(PEER_TOPK)]
    return jnp.concatenate(comb, axis=0), jnp.concatenate(cidx, axis=0)


SELECT_BLOCK = SUBLANES * LANES


def _select_kernel(tb, x_ref, g_ref, wq_ref, keys_ref, *rest):
    xn_ref, eid_ref, tile_ref, shift_ref, gate_ref, s1, s2, eid_s, gate_s = rest[-9:]
    groups = tb // LANES
    assert groups == SUBLANES
    xn = _rms(x_ref[...], g_ref[...])
    xn_ref[...] = xn
    xn = xn.astype(BF16)

    def head(h, carry):
        q = jnp.dot(xn, wq_ref[h], preferred_element_type=F32)
        for g in range(groups):
            qg = q[g * LANES:(g + 1) * LANES]
            s1[pl.ds(g, N_KEYS, stride=SUBLANES), :] = _dot_nt(keys_ref[h, 0], qg[:, 0:PEER_DK_HALF])
            s2[pl.ds(g, N_KEYS, stride=SUBLANES), :] = _dot_nt(keys_ref[h, 1], qg[:, PEER_DK_HALF:PEER_DK])
        v1, i1 = _top_k_rows(s1[...].reshape(N_KEYS, SUBLANES, LANES), PEER_TOPK)
        v2, i2 = _top_k_rows(s2[...].reshape(N_KEYS, SUBLANES, LANES), PEER_TOPK)
        comb, cidx = _pair_candidates(v1, i1, v2, i2)
        top, eid = _top_k_rows(comb, PEER_TOPK, cidx)
        ex = jnp.exp(top - top[0:1])
        gate = ex / jnp.sum(ex, axis=0, keepdims=True)
        rows = pl.ds(pl.multiple_of(h * PEER_TOPK * SUBLANES, PEER_TOPK * SUBLANES), PEER_TOPK * SUBLANES)
        gate_s[rows, :] = gate.reshape(PEER_TOPK * SUBLANES, LANES)
        eid_s[rows, :] = eid.reshape(PEER_TOPK * SUBLANES, LANES)
        return carry

    lax.fori_loop(0, PEER_HEADS, head, 0)
    for g in range(groups):
        toks = slice(g * LANES, (g + 1) * LANES)
        eid = eid_s[pl.ds(g, PICKS, stride=SUBLANES), :].T
        upper = eid >= HALF_EXPERTS
        eid_ref[toks, :] = eid
        tile_ref[toks, :] = jnp.where(upper, eid - HALF_EXPERTS, eid) * SUBLANES
        shift_ref[toks, :] = jnp.where(upper, 0, 16)
        gate_ref[toks, :] = gate_s[pl.ds(g, PICKS, stride=SUBLANES), :].T


def _select(x, g, wq, keys, tb, t0, t, after=None):
    b0 = t0 // tb
    extra = [] if after is None else [after]
    xin = pl.BlockSpec((tb, D_MODEL), lambda i: (i + b0, 0))
    xblk = pl.BlockSpec((tb, D_MODEL), lambda i: (i, 0))
    picks_i = jax.ShapeDtypeStruct((t, PICKS), jnp.int32)
    return pl.pallas_call(
        functools.partial(_select_kernel, tb),
        out_shape=(jax.ShapeDtypeStruct((t, D_MODEL), F32), picks_i, picks_i, picks_i,
                   jax.ShapeDtypeStruct((t, PICKS), F32)),
        grid=(t // tb,),
        in_specs=[xin, _const_spec(g.shape), _const_spec(wq.shape), _const_spec(keys.shape)]
        + [_after_spec(a) for a in extra],
        out_specs=(xblk,) + (_pick_block(tb),) * 4,
        scratch_shapes=[pltpu.VMEM((N_KEYS * SUBLANES, LANES), F32), pltpu.VMEM((N_KEYS * SUBLANES, LANES), F32),
                        pltpu.VMEM((PICKS * SUBLANES, LANES), jnp.int32), pltpu.VMEM((PICKS * SUBLANES, LANES), F32)],
        compiler_params=pltpu.CompilerParams(dimension_semantics=("parallel",), vmem_limit_bytes=VMEM_LIMIT),
        name="peer_select",
    )(x, g, wq, keys, *extra)


HI_HALF_MASK = -(1 << 16)


def _unpack(word, shift):
    return lax.bitcast_convert_type((word << shift) & HI_HALF_MASK, F32)


def _splat_picks(blk, one_pass):
    tb = blk.shape[0]
    r = lax.broadcasted_iota(jnp.int32, (PICKS, LANES), 0)
    c = lax.broadcasted_iota(jnp.int32, (PICKS, LANES), 1)
    eye = (r == c)[None]
    ones = jnp.ones((LANES, LANES), BF16)

    def one(piece):
        diag = jnp.where(eye, piece[:, None, :], 0.0).astype(BF16).reshape(tb * PICKS, LANES)
        return jnp.dot(diag, ones, preferred_element_type=F32)

    if one_pass:
        out = one(blk)
    else:
        hi = blk.astype(BF16).astype(F32)
        rest = blk - hi
        mid = rest.astype(BF16).astype(F32)
        out = one(hi) + one(mid) + one(rest - mid)
    return out.reshape(tb, PICKS, LANES)


def _full(ref, k):
    return jnp.broadcast_to(ref[k:k + 1, :], (SUBLANES, LANES))


def _expert_row(tab_ref, tile_ref, t, k, shifts):
    idx = pl.multiple_of(tile_ref[t, k], SUBLANES)
    return _unpack(tab_ref[pl.ds(idx, SUBLANES), :], _full(shifts, k))


def _hidden_kernel(tb, tile_ref, shift_ref, x_ref, gate_ref, tab_ref, *rest):
    gh_ref, bsh, stage, parts, xt = rest[-5:]
    bsh[...] = _splat_picks(shift_ref[...].astype(F32), True).astype(jnp.int32)
    for s in range(SUBLANES):
        xt[pl.ds(s, tb, stride=SUBLANES), :] = x_ref[:, s * LANES:(s + 1) * LANES]

    half = PICKS // 2
    quad = SUBLANES // 2
    low = lax.broadcasted_iota(jnp.int32, (SUBLANES, LANES), 0) < quad

    def token(t, carry):
        x = xt[pl.ds(pl.multiple_of(t * SUBLANES, SUBLANES), SUBLANES), :]
        shifts = bsh.at[t]
        for j in range(half):
            a = x * _expert_row(tab_ref, tile_ref, t, j, shifts)
            b = x * _expert_row(tab_ref, tile_ref, t, j + half, shifts)
            stage[j * SUBLANES:(j + 1) * SUBLANES, :] = jnp.where(low, a + pltpu.roll(a, quad, 0),
                                                                  b + pltpu.roll(b, quad, 0))
        for p in range(2):
            part = stage[pl.ds(p * quad, half, stride=SUBLANES), :]
            for s in range(1, quad):
                part = part + stage[pl.ds(p * quad + s, half, stride=SUBLANES), :]
            parts[t, p * half:(p + 1) * half, :] = part
        return carry

    lax.fori_loop(0, tb, token, 0)
    gh_ref[...] = jnp.sum(parts[...], axis=-1)
    hid = gh_ref[...]
    gelu = 0.5 * hid * (1.0 + lax.erf(hid * (2.0 ** -0.5)))
    gh_ref[...] = gate_ref[...] * gelu


def _pick_block(tb, space=None):
    return pl.BlockSpec((tb, PICKS), lambda i: (i, 0), memory_space=space)


def _table_spec(tab):
    return pl.BlockSpec(tab.shape, lambda i: (0, 0), pipeline_mode=pl.Buffered(1))


def _hidden(tile, shift, xg, gate, tab, tb, after=None):
    t = xg.shape[0]
    extra = [] if after is None else [after]
    return pl.pallas_call(
        functools.partial(_hidden_kernel, tb),
        out_shape=jax.ShapeDtypeStruct((t, PICKS), F32),
        grid=(t // tb,),
        in_specs=[_pick_block(tb, pltpu.SMEM), _pick_block(tb),
                  pl.BlockSpec((tb, D_MODEL), lambda i: (i, 0)),
                  _pick_block(tb), _table_spec(tab)] + [_after_spec(a) for a in extra],
        out_specs=_pick_block(tb),
        scratch_shapes=[pltpu.VMEM((tb, PICKS, LANES), jnp.int32), pltpu.VMEM((PICKS // 2 * SUBLANES, LANES), F32),
                        pltpu.VMEM((tb, PICKS, LANES), F32), pltpu.VMEM((tb * SUBLANES, LANES), F32)],
        compiler_params=pltpu.CompilerParams(dimension_semantics=("parallel",), vmem_limit_bytes=VMEM_LIMIT),
        name="peer_hidden",
    )(tile, shift, xg, gate, tab, *extra)


def _combine_kernel(tb, tile_ref, shift_ref, gh_ref, tab_ref, x_ref, g_ref, *rest):
    o_ref, bsh, bg, ot = rest[-4:]
    bsh[...] = _splat_picks(shift_ref[...].astype(F32), True).astype(jnp.int32)
    bg[...] = _splat_picks(gh_ref[...], False)

    def token(t, carry):
        shifts, gates = bsh.at[t], bg.at[t]
        acc = [jnp.zeros((SUBLANES, LANES), F32) for _ in range(4)]
        for k in range(PICKS):
            acc[k % 4] = acc[k % 4] + _full(gates, k) * _expert_row(tab_ref, tile_ref, t, k, shifts)
        ot[pl.ds(pl.multiple_of(t * SUBLANES, SUBLANES), SUBLANES), :] = (acc[0] + acc[1]) + (acc[2] + acc[3])
        return carry

    lax.fori_loop(0, tb, token, 0)
    for s in range(SUBLANES):
        o_ref[:, s * LANES:(s + 1) * LANES] = ot[pl.ds(s, tb, stride=SUBLANES), :]
    o_ref[...] = _rms(x_ref[...] + o_ref[...], g_ref[...])


def _combine(tile, shift, gh, tab, x, g, tb, x_row0, out_row0, out_rows, into=None):
    t = gh.shape[0]
    xb0, ob0 = x_row0 // tb, out_row0 // tb
    extra = [] if into is None else [into]
    return pl.pallas_call(
        functools.partial(_combine_kernel, tb),
        out_shape=jax.ShapeDtypeStruct((out_rows, D_MODEL), F32),
        grid=(t // tb,),
        in_specs=[_pick_block(tb, pltpu.SMEM), _pick_block(tb), _pick_block(tb), _table_spec(tab),
                  pl.BlockSpec((tb, D_MODEL), lambda i: (i + xb0, 0)), _const_spec(g.shape)]
        + [pl.BlockSpec(memory_space=pl.ANY) for _ in extra],
        input_output_aliases={6: 0} if extra else {},
        out_specs=pl.BlockSpec((tb, D_MODEL), lambda i: (i + ob0, 0)),
        scratch_shapes=[pltpu.VMEM((tb, PICKS, LANES), jnp.int32), pltpu.VMEM((tb, PICKS, LANES), F32),
                        pltpu.VMEM((tb * SUBLANES, LANES), F32)],
        compiler_params=pltpu.CompilerParams(dimension_semantics=("parallel",), vmem_limit_bytes=VMEM_LIMIT),
        name="peer_combine",
    )(tile, shift, gh, tab, x, g, *extra)


SC_LANES = 16
SC_WORKERS = 32
SC_CHUNK = 32
SC_FEATURE_BLOCK = 256
SC_EARLY_SEQS = 3
SC_PROMPT_TOKENS = 13312
SC_COMBINE_HANDBACK = 512


def _sc_call(body, n, side_width, out_width):
    mesh = plsc.VectorSubcoreMesh(core_axis_name="c", subcore_axis_name="s")
    return pl.kernel(body, mesh=mesh, out_type=jax.ShapeDtypeStruct((n, out_width), F32),
                     compiler_params=pltpu.CompilerParams(needs_layout_passes=False),
                     scratch_types=[pltpu.VMEM((2 * PICKS,), jnp.int32), pltpu.VMEM((2 * side_width,), F32),
                                    pltpu.VMEM((2 * out_width,), F32),
                                    pltpu.VMEM((SC_CHUNK, D_MODEL), F32), pltpu.VMEM((SC_CHUNK, D_MODEL), F32)]
                     + [pltpu.SemaphoreType.DMA] * 5)


def _sc_token_loop(n, eid_hbm, side_hbm, tab_hbm, out_hbm, idx_v, side_v, out_v, bufs, sems, compute):
    sem_rows, sem_fetch, sem_put = sems[0:2], sems[2], sems[3:5]
    per_worker = n // SC_WORKERS
    first = (lax.axis_index("s") * 2 + lax.axis_index("c")) * per_worker
    last = first + per_worker - 1
    chunks = PICKS // SC_CHUNK
    side_w = side_v.shape[0] // 2
    out_w = out_v.shape[0] // 2

    def fetch(t, slot):
        return (pltpu.make_async_copy(eid_hbm.at[t], idx_v.at[pl.ds(slot * PICKS, PICKS)], sem_fetch),
                pltpu.make_async_copy(side_hbm.at[t], side_v.at[pl.ds(slot * side_w, side_w)], sem_fetch))

    def gather(slot, c):
        ids = idx_v.at[pl.ds(slot * PICKS + c * SC_CHUNK, SC_CHUNK)]
        return pltpu.make_async_copy(tab_hbm.at[ids], bufs[c % 2], sem_rows[c % 2])

    def put(t, slot):
        return pltpu.make_async_copy(out_v.at[pl.ds(slot * out_w, out_w)], out_hbm.at[t], sem_put[slot])

    for cp in fetch(first, 0):
        cp.start()
    for cp in fetch(first, 0):
        cp.wait()
    gather(0, 0).start()
    gather(0, 1).start()

    @pl.loop(0, per_worker, step=2)
    def _(i):
        for slot in range(2):
            t = first + i + slot
            nxt = jnp.minimum(t + 1, last)
            for cp in fetch(nxt, 1 - slot):
                cp.start()

            @pl.when(i >= 2)
            def _():
                put(t, slot).wait()

            for c in range(chunks):
                gather(slot, c).wait()
                compute(c, bufs[c % 2], slot * side_w, slot * out_w)
                if c + 2 < chunks:
                    gather(slot, c + 2).start()
                else:
                    if c + 2 == chunks:
                        for cp in fetch(nxt, 1 - slot):
                            cp.wait()
                    gather(1 - slot, c + 2 - chunks).start()
            put(t, slot).start()

    gather(0, 0).wait()
    gather(0, 1).wait()
    put(last, 0).wait()
    put(last, 1).wait()


def _sc_hidden(eid, x, tab):
    n = eid.shape[0]
    lanes = SC_LANES

    def body(eid_hbm, x_hbm, tab_hbm, out_hbm, idx_v, x_v, hid_v, rows0, rows1, *sems):
        lane = lax.iota(jnp.int32, lanes)

        def compute(c, rows, x0, h0):
            for g in range(SC_CHUNK // lanes):
                def step(j, accs):
                    off = pl.multiple_of(j * lanes, lanes)
                    xj = x_v[pl.ds(x0 + off, lanes)]
                    return tuple(accs[r] + xj * rows[g * lanes + r, pl.ds(off, lanes)] for r in range(lanes))
                accs = lax.fori_loop(0, D_MODEL // lanes, step,
                                     tuple(jnp.zeros((lanes,), F32) for _ in range(lanes)))
                hv = jnp.zeros((lanes,), F32)
                for r in range(lanes):
                    hv = jnp.where(lane == r, jnp.sum(accs[r]), hv)
                hid_v[pl.ds(h0 + c * SC_CHUNK + g * lanes, lanes)] = hv

        _sc_token_loop(n, eid_hbm, x_hbm, tab_hbm, out_hbm, idx_v, x_v, hid_v, (rows0, rows1), sems, compute)

    return _sc_call(body, n, D_MODEL, PICKS)(eid, x, tab)


def _sc_combine(eid, gh, tab, n=None):
    n = eid.shape[0] if n is None else n
    lanes = SC_LANES
    nacc = SC_FEATURE_BLOCK // lanes

    def body(eid_hbm, g_hbm, tab_hbm, out_hbm, idx_v, g_v, out_v, rows0, rows1, *sems):
        def compute(c, rows, g0, o0):
            for fb in range(D_MODEL // SC_FEATURE_BLOCK):
                f0 = fb * SC_FEATURE_BLOCK
                if c == 0:
                    init = tuple(jnp.zeros((lanes,), F32) for _ in range(nacc))
                else:
                    init = tuple(out_v[pl.ds(o0 + f0 + q * lanes, lanes)] for q in range(nacc))

                def step(r, accs):
                    gk = plsc.load_gather(g_v, [jnp.full((lanes,), g0 + c * SC_CHUNK + r, jnp.int32)])
                    return tuple(accs[q] + gk * rows[r, pl.ds(f0 + q * lanes, lanes)] for q in range(nacc))
                accs = lax.fori_loop(0, SC_CHUNK, step, init)
                for q in range(nacc):
                    out_v[pl.ds(o0 + f0 + q * lanes, lanes)] = accs[q]

        _sc_token_loop(n, eid_hbm, g_hbm, tab_hbm, out_hbm, idx_v, g_v, out_v, (rows0, rows1), sems, compute)

    return _sc_call(body, n, PICKS, D_MODEL)(eid, gh, tab)


def _after_spec(after):
    return pl.BlockSpec((SUBLANES, after.shape[1]), lambda i: (0, 0))


def _gate_gelu_kernel(h_ref, g_ref, after_ref, o_ref):
    hid = h_ref[...]
    o_ref[...] = g_ref[...] * (0.5 * hid * (1.0 + lax.erf(hid * (2.0 ** -0.5))))


def _gate_gelu(hid, gate, tb, after):
    t = hid.shape[0]
    return pl.pallas_call(
        _gate_gelu_kernel, out_shape=jax.ShapeDtypeStruct((t, PICKS), F32), grid=(t // tb,),
        in_specs=[_pick_block(tb), _pick_block(tb), _after_spec(after)], out_specs=_pick_block(tb),
        compiler_params=pltpu.CompilerParams(dimension_semantics=("parallel",)),
        name="gate_gelu",
    )(hid, gate, after)


def _final_kernel(x_ref, p_ref, g_ref, *rest):
    rest[-1][...] = _rms(x_ref[...] + p_ref[...], g_ref[...])


def _final(x, p, g, tb, out_row0, out_rows, into=None):
    b0 = out_row0 // tb
    local = pl.BlockSpec((tb, D_MODEL), lambda i: (i, 0))
    in_specs = [local, local, _const_spec(g.shape)]
    args = [x, p, g]
    aliases = {}
    if into is not None:
        aliases = {len(args): 0}
        in_specs.append(pl.BlockSpec(memory_space=pl.ANY))
        args.append(into)
    return pl.pallas_call(
        _final_kernel, out_shape=jax.ShapeDtypeStruct((out_rows, D_MODEL), F32), grid=(p.shape[0] // tb,),
        in_specs=in_specs, out_specs=pl.BlockSpec((tb, D_MODEL), lambda i: (i + b0, 0)),
        input_output_aliases=aliases,
        compiler_params=pltpu.CompilerParams(dimension_semantics=("parallel",), vmem_limit_bytes=VMEM_LIMIT),
        name="final_norm",
    )(*args)


PACK_BLOCK = 256


def _pack_kernel(lo_ref, hi_ref, o_ref):
    def bf16_bits(x):
        return lax.bitcast_convert_type(x.astype(BF16).astype(F32), jnp.int32)
    word = bf16_bits(hi_ref[...]) | lax.shift_right_logical(bf16_bits(lo_ref[...]), 16)
    for s in range(SUBLANES):
        o_ref[pl.ds(s, PACK_BLOCK, stride=SUBLANES), :] = word[:, s * LANES:(s + 1) * LANES]


def _pack_table(tab):
    steps = HALF_EXPERTS // PACK_BLOCK
    return pl.pallas_call(
        _pack_kernel, out_shape=jax.ShapeDtypeStruct((HALF_EXPERTS * SUBLANES, LANES), jnp.int32), grid=(steps,),
        in_specs=[pl.BlockSpec((PACK_BLOCK, D_MODEL), lambda i: (i, 0)),
                  pl.BlockSpec((PACK_BLOCK, D_MODEL), lambda i: (i + steps, 0))],
        out_specs=pl.BlockSpec((PACK_BLOCK * SUBLANES, LANES), lambda i: (i, 0)),
        compiler_params=pltpu.CompilerParams(dimension_semantics=("parallel",)),
        name="pack_table",
    )(tab, tab)


def _col(v, n=LANES):
    return jnp.pad(v.astype(F32), (0, n - v.shape[0])).reshape(1, n)


def _row(v, n=2 * SUBLANES):
    return jnp.pad(v.astype(F32), (0, n - v.shape[0])).reshape(n, 1)


def _peer(x, w, tb_sel, tb_exp, n_sc, out_row0=0, out_rows=None, after=None, into=None, between=None):
    t = x.shape[0]
    out_rows = t if out_rows is None else out_rows
    n_tc = t - n_sc
    n_sc2 = max(n_sc - SC_COMBINE_HANDBACK, 0)
    if n_sc:
        xn_s, eid_s, tile_s, shift_s, gate_s = _select(x, w["ln_ffn"], w["w_pq"], w["sub_keys"], SELECT_BLOCK, 0,
                                                       n_sc, after=after)
        hid_s = _sc_hidden(eid_s, xn_s, w["u_tab"])
    xn, _, tile, shift, gate = _select(x, w["ln_ffn"], w["w_pq"], w["sub_keys"], SELECT_BLOCK,n_sc, n_tc,
                                       after=xn_s if n_sc else None)
    gh = _hidden(tile, shift, xn, gate, w["u_pack"], tb_exp, after=into)
    if n_sc:
        other = None if between is None else between(gh)
        gh_s = _gate_gelu(hid_s, gate_s, tb_sel, after=gh if other is None else other)
        out_s = _sc_combine(eid_s, gh_s, w["v_tab"], n_sc2)
    y = _combine(tile, shift, gh, w["v_pack"], x, w["final_norm"], tb_exp, n_sc, out_row0 + n_sc, out_rows, into)
    if n_sc > n_sc2:
        y = _combine(tile_s[n_sc2:], shift_s[n_sc2:], gh_s[n_sc2:], w["v_pack"], x, w["final_norm"], tb_exp,
                     n_sc2, out_row0 + n_sc2, out_rows, y)
    if n_sc2:
        y = _final(x, out_s, w["final_norm"], tb_sel, out_row0, out_rows, into=y)
    return (y, other) if between is not None else y


def kernel(x_prompt, x_sample, mem_prompt, cache_conv_a, cache_conv_ssd, state_ssd, cache_mem_k, cache_mem_v, ln_mix_w, w_in, conv_a_w, conv_s_w, conv_s_b, dt_bias, a_log, d_skip, ssd_norm_w, w_out, ln_mem_w, mem_norm_w, w_mq, w_mk, w_mv, w_mo, ln_ffn_w, w_pq, sub_keys, u_tab, v_tab, final_norm_w):
    depth = w_in.shape[0]
    assert depth == 1
    l = 0
    n_main = 3 * D_CONV + D_SSD + XBC_DIM
    w_dt = w_in[l][:, n_main:]
    w = {
        "ln_mix": ln_mix_w[l].reshape(1, D_MODEL),
        "w_in": w_in[l][:, :n_main].astype(BF16),
        "w_dt": jnp.pad(w_dt, ((0, 0), (0, LANES - SSD_HEADS))).astype(BF16),
        "w_dtt": jnp.pad(w_dt.T, ((0, 2 * SUBLANES - SSD_HEADS), (0, 0))).astype(BF16),
        "conv_a_w": conv_a_w[l], "conv_s_w": conv_s_w[l], "conv_s_b": conv_s_b[l].reshape(1, XBC_DIM),
        "dtb_c": _col(dt_bias[l]), "alog_c": _col(a_log[l]), "dsk_c": _col(d_skip[l]),
        "dtb_r": _row(dt_bias[l]), "alog_r": _row(a_log[l]),
        "ssd_norm": ssd_norm_w[l].reshape(1, D_SSD),
        "w_out": w_out[l].astype(BF16),
        "ln_ffn": ln_ffn_w[l].reshape(1, D_MODEL),
        "w_pq": w_pq[l].astype(BF16).reshape(D_MODEL, PEER_HEADS, PEER_DK).transpose(1, 0, 2),
        "sub_keys": sub_keys[l].astype(BF16),
        "u_pack": _pack_table(u_tab[l]),
        "v_pack": _pack_table(v_tab[l]),
        "u_tab": u_tab[l], "v_tab": v_tab[l],
        "final_norm": final_norm_w.reshape(1, D_MODEL),
    }
    ln_mem = ln_mem_w[l].reshape(1, D_MODEL)
    wq, wo = w_mq[l].astype(BF16), w_mo[l].astype(BF16)

    bp, sp, _ = x_prompt.shape
    bs, ss, _ = x_sample.shape

    mk, mv = _memory_kv(mem_prompt.reshape(bp * N_MEM, D_MODEL), mem_norm_w[l].reshape(1, D_MODEL),
                        w_mk[l].astype(BF16), w_mv[l].astype(BF16))
    mk3, mv3 = mk.reshape(bp, N_MEM, D_MODEL), mv.reshape(bp, N_MEM, D_MODEL)
    n1 = SC_EARLY_SEQS
    h1, pa1, ps1, ph1 = _mixer(x_prompt, None, w, 256, 0, n1)
    h1 = _attn(h1, mk3, mv3, ln_mem, wq, wo, 256).reshape(n1 * sp, D_MODEL)

    xn1, eid1, _, _, gate1 = _select(h1, w["ln_ffn"], w["w_pq"], w["sub_keys"], SELECT_BLOCK,0, n1 * sp)
    hid1 = _sc_hidden(eid1, xn1, w["u_tab"])
    h2, pa2, ps2, ph2 = _mixer(x_prompt, None, w, 256, n1, bp - n1, after=xn1)
    h2 = _attn(h2, mk3, mv3, ln_mem, wq, wo, 256, kv0=n1).reshape((bp - n1) * sp, D_MODEL)

    hs, sa, ssd, sh = _mixer(x_sample, (cache_conv_a[l], cache_conv_ssd[l],
                                        state_ssd[l].reshape((bs,) + PAIR_STATE)), w, 256, after=h2)
    hs = _attn(hs, cache_mem_k[l], cache_mem_v[l], ln_mem, wq, wo, 256).reshape(bs * ss, D_MODEL)
    xn_q, _, tile_q, shift_q, gate_q = _select(hs, w["ln_ffn"], w["w_pq"], w["sub_keys"], SELECT_BLOCK,0, bs * ss)

    gh1 = _gate_gelu(hid1, gate1, 256, after=xn_q)
    out1 = _sc_combine(eid1, gh1, w["v_tab"])
    y1 = _final(h1, out1, w["final_norm"], 256, 0, bp * sp)
    def sample_experts(gh_main):
        gh_q = _hidden(tile_q, shift_q, xn_q, gate_q, w["u_pack"], 64, after=gh_main)
        return _combine(tile_q, shift_q, gh_q, w["v_pack"], hs, w["final_norm"], 64, 0, 0, bs * ss)

    y_prompt, y_sample = _peer(h2, w, 256, 64, SC_PROMPT_TOKENS, n1 * sp, bp * sp, after=gh1, into=y1,
                               between=sample_experts)
    y_prompt, y_sample = y_prompt.reshape(bp, sp, D_MODEL), y_sample.reshape(bs, ss, D_MODEL)
    pa, ps, ph = (jnp.concatenate(p, axis=0) for p in ((pa1, pa2), (ps1, ps2), (ph1, ph2)))

    kv_shape = (1, bp, N_MEM, MEM_HEADS, MEM_HEAD_DIM)
    st_shape = (SSD_HEADS, SSD_HEAD_DIM, SSD_STATE)
    return (y_prompt, y_sample, pa[None], ps[None], ph.reshape((1, bp) + st_shape), mk.reshape(kv_shape),
            mv.reshape(kv_shape), sa[None], ssd[None], sh.reshape((1, bs) + st_shape))
```

```python
import functools

import jax
import jax.numpy as jnp
from jax import lax
from jax.experimental import pallas as pl
from jax.experimental.pallas import tpu as pltpu
from jax.experimental.pallas import tpu_sc as plsc

D_MODEL = 1024
CHUNK = 64
D_CONV = 512
CONV_A_W = 3
D_SSD = 512
SSD_HEAD_DIM = 64
SSD_HEADS = 8
SSD_GROUPS = 2
SSD_HPG = 4
SSD_STATE = 128
SSD_CONV_W = 4
XBC_DIM = 1024
N_MEM = 256
MEM_HEADS = 4
MEM_HEAD_DIM = 256
PEER_HEADS = 8
N_KEYS = 128
N_EXPERTS = N_KEYS * N_KEYS
PEER_TOPK = 16
PEER_DK = 256
PEER_DK_HALF = 128
PICKS = PEER_HEADS * PEER_TOPK
EPS = 1e-6

LANES = 128
SUBLANES = 8
HALF_EXPERTS = N_EXPERTS // 2
PAIR_STATE = (SSD_HEADS // 2, 2 * SSD_HEAD_DIM, SSD_STATE)
VMEM_LIMIT = 56 * 1024 * 1024

F32 = jnp.float32
BF16 = jnp.bfloat16
HI = lax.Precision.HIGHEST


def _rms(x, g):
    return x * lax.rsqrt(jnp.mean(x * x, axis=-1, keepdims=True) + EPS) * g


def _silu(x):
    return x * (1.0 / (1.0 + jnp.exp(-x)))


def _softplus(x):
    return jnp.maximum(x, 0.0) + jnp.log1p(jnp.exp(-jnp.abs(x)))


def _dot(a, b):
    return jnp.dot(a.astype(BF16), b.astype(BF16), preferred_element_type=F32)


def _dot_nt(a, b):
    return lax.dot_general(a.astype(BF16), b.astype(BF16), (((1,), (1,)), ((), ())),
                           preferred_element_type=F32)


def _dot_tn(a, b):
    return lax.dot_general(a.astype(BF16), b.astype(BF16), (((0,), (0,)), ((), ())),
                           preferred_element_type=F32)


def _const_spec(shape):
    n = len(shape)
    return pl.BlockSpec(shape, lambda *_: (0,) * n)


def _mixer_kernel(has_hist, has_after, ts, lc, *refs):
    if has_after:
        refs = refs[1:]
    if has_hist:
        x_ref, hista_ref, hists_ref, h0_ref = refs[:4]
        refs = refs[4:]
    else:
        x_ref = refs[0]
        refs = refs[1:]
    (lnw_ref, win_ref, wdt_ref, wdtt_ref, caw_ref, csw_ref, csb_ref, dtb_c_ref, alog_c_ref, dsk_c_ref,
     dtb_r_ref, alog_r_ref, nw_ref, wout_ref,
     h1_ref, na_ref, ns_ref, nh_ref, cata, cats, hst) = refs
    s = pl.program_id(1)
    pad = SUBLANES

    @pl.when(s == 0)
    def _():
        if has_hist:
            cata[pad - 2:pad, :] = hista_ref[...]
            cats[pad - 3:pad, :] = hists_ref[...]
            hst[...] = h0_ref[...]
        else:
            cata[0:pad, :] = jnp.zeros((pad, D_CONV), F32)
            cats[0:pad, :] = jnp.zeros((pad, XBC_DIM), F32)
            hst[...] = jnp.zeros(hst.shape, F32)

    x = x_ref[...]
    xn = _rms(x, lnw_ref[...]).astype(BF16)
    proj = jnp.dot(xn, win_ref[...], preferred_element_type=F32)
    dt_c = jnp.dot(xn, wdt_ref[...], preferred_element_type=F32)
    dt_r = lax.dot_general(wdtt_ref[...], xn, (((1,), (1,)), ((), ())),
                           preferred_element_type=F32)
    g_b = proj[:, 0:D_CONV]
    g_c = proj[:, D_CONV:2 * D_CONV]
    v_in = proj[:, 2 * D_CONV:3 * D_CONV]
    z = proj[:, 3 * D_CONV:3 * D_CONV + D_SSD]
    xbc = proj[:, 3 * D_CONV + D_SSD:3 * D_CONV + D_SSD + XBC_DIM]

    ua = g_c * v_in
    cata[pad:pad + ts, :] = ua
    caw = caw_ref[...]
    conv_a = (caw[0:1] * cata[pad - 2:pad - 2 + ts, :] + caw[1:2] * cata[pad - 1:pad - 1 + ts, :]
              + caw[2:3] * ua)
    y_a = g_b * conv_a
    hist_a = cata[pad + ts - 2:pad + ts, :]
    na_ref[...] = hist_a
    cata[pad - 2:pad, :] = hist_a

    cats[pad:pad + ts, :] = xbc
    csw = csw_ref[...]
    conv_s = (csw[0:1] * cats[pad - 3:pad - 3 + ts, :] + csw[1:2] * cats[pad - 2:pad - 2 + ts, :]
              + csw[2:3] * cats[pad - 1:pad - 1 + ts, :] + csw[3:4] * xbc)
    hist_s = cats[pad + ts - 3:pad + ts, :]
    ns_ref[...] = hist_s
    cats[pad - 3:pad, :] = hist_s
    xbc_c = _silu(conv_s + csb_ref[...])
    xs = xbc_c[:, 0:D_SSD]

    dtp_c = _softplus(dt_c + dtb_c_ref[...])
    dtp_r = _softplus(dt_r + dtb_r_ref[...])
    a_c = dtp_c * (-jnp.exp(alog_c_ref[...]))
    a_r = dtp_r * (-jnp.exp(alog_r_ref[...]))
    dsk_c = dsk_c_ref[...]

    lane = lax.broadcasted_iota(jnp.int32, (1, LANES), 1)
    first_head = lane < SSD_HEAD_DIM
    row2 = lax.broadcasted_iota(jnp.int32, (2 * SSD_HEAD_DIM, 1), 0) < SSD_HEAD_DIM
    li = lax.broadcasted_iota(jnp.int32, (lc, lc), 0)
    si = lax.broadcasted_iota(jnp.int32, (lc, lc), 1)
    causal = si <= li
    tril = causal.astype(F32)
    jrow = lax.broadcasted_iota(jnp.int32, (ts, lc), 0)
    scol = lax.broadcasted_iota(jnp.int32, (ts, lc), 1)

    def pair(col, h):
        return jnp.where(first_head, col[:, h:h + 1], col[:, h + 1:h + 2])

    y_chunks = []
    for c in range(ts // lc):
        r0 = c * lc
        rows = slice(r0, r0 + lc)
        acum_c = jnp.dot(tril, a_c[rows], preferred_element_type=F32, precision=HI)
        upper = ((jrow >= r0) & (jrow <= r0 + scol)).astype(F32)
        acum_r = jnp.dot(a_r, upper, preferred_element_type=F32, precision=HI)
        last_c = acum_c[lc - 1:lc, :]
        dec_c = jnp.exp(last_c - acum_c)
        eac_c = jnp.exp(acum_c)
        y_pairs = []
        for g in range(SSD_GROUPS):
            b_g = xbc_c[rows, D_SSD + g * SSD_STATE:D_SSD + (g + 1) * SSD_STATE]
            c_g = xbc_c[rows, D_SSD + (SSD_GROUPS + g) * SSD_STATE:D_SSD + (SSD_GROUPS + g + 1) * SSD_STATE]
            cb = _dot_nt(c_g, b_g)
            for q in range(SSD_HPG // 2):
                h = g * SSD_HPG + 2 * q
                pi = h // 2
                xs_p = xs[rows, pi * LANES:(pi + 1) * LANES]
                xdt = xs_p * pair(dtp_c[rows], h)
                res = []
                for hh in (h, h + 1):
                    seg = acum_c[:, hh:hh + 1] - acum_r[hh:hh + 1, :]
                    m_h = cb * jnp.exp(jnp.where(causal, seg, -jnp.inf))
                    res.append(_dot(m_h, xdt))
                y_diag = jnp.where(first_head, res[0], res[1])
                h_in = hst[pi]
                y_off = _dot_nt(c_g, h_in) * pair(eac_c, h)
                st = _dot_tn(xdt * pair(dec_c, h), b_g)
                cd = jnp.exp(jnp.where(row2, last_c[:, h:h + 1], last_c[:, h + 1:h + 2]))
                hst[pi] = cd * h_in + st
                y_pairs.append(y_diag + y_off + pair(dsk_c, h) * xs_p)
        y_chunks.append(jnp.concatenate(y_pairs, axis=1))
    y = y_chunks[0] if len(y_chunks) == 1 else jnp.concatenate(y_chunks, axis=0)
    y_b = _rms(y * _silu(z), nw_ref[...])
    wout = wout_ref[...]
    out = (jnp.dot(y_a.astype(BF16), wout[0:D_CONV], preferred_element_type=F32)
           + jnp.dot(y_b.astype(BF16), wout[D_CONV:], preferred_element_type=F32))
    h1_ref[...] = x + out
    nh_ref[...] = hst[...]


def _mixer(x, hist, w, ts, b0=0, nb=None, after=None):
    _, s, _ = x.shape
    b = x.shape[0] if nb is None else nb
    lc = min(CHUNK, s)
    ts = min(ts, s)
    has_hist = hist is not None
    assert not (has_hist and b0)
    per_b3 = lambda shape: pl.BlockSpec((None,) + shape, lambda i, j: (i,) + (0,) * len(shape))
    in_specs = [pl.BlockSpec((None, ts, D_MODEL), lambda i, j: (i + b0, j, 0))]
    args = [x]
    if after is not None:
        in_specs.insert(0, pl.BlockSpec((SUBLANES, after.shape[1]), lambda i, j: (0, 0)))
        args.insert(0, after)
    if has_hist:
        in_specs += [per_b3((CONV_A_W - 1, D_CONV)), per_b3((SSD_CONV_W - 1, XBC_DIM)),
                     per_b3(PAIR_STATE)]
        args += list(hist)
    wargs = [w["ln_mix"], w["w_in"], w["w_dt"], w["w_dtt"], w["conv_a_w"], w["conv_s_w"], w["conv_s_b"],
             w["dtb_c"], w["alog_c"], w["dsk_c"], w["dtb_r"], w["alog_r"], w["ssd_norm"], w["w_out"]]
    in_specs += [_const_spec(a.shape) for a in wargs]
    args += wargs
    out_shape = (jax.ShapeDtypeStruct((b, s, D_MODEL), F32),
                 jax.ShapeDtypeStruct((b, CONV_A_W - 1, D_CONV), F32),
                 jax.ShapeDtypeStruct((b, SSD_CONV_W - 1, XBC_DIM), F32),
                 jax.ShapeDtypeStruct((b,) + PAIR_STATE, F32))
    out_specs = (pl.BlockSpec((None, ts, D_MODEL), lambda i, j: (i, j, 0)),
                 per_b3((CONV_A_W - 1, D_CONV)), per_b3((SSD_CONV_W - 1, XBC_DIM)),
                 per_b3(PAIR_STATE))
    return pl.pallas_call(
        functools.partial(_mixer_kernel, has_hist, after is not None, ts, lc),
        out_shape=out_shape, grid=(b, s // ts), in_specs=in_specs, out_specs=out_specs,
        scratch_shapes=[pltpu.VMEM((SUBLANES + ts, D_CONV), F32), pltpu.VMEM((SUBLANES + ts, XBC_DIM), F32),
                        pltpu.VMEM(PAIR_STATE, F32)],
        compiler_params=pltpu.CompilerParams(dimension_semantics=("parallel", "arbitrary"),
                                             vmem_limit_bytes=VMEM_LIMIT),
        name="mixer_hist" if has_hist else "mixer",
    )(*args)


def _memkv_kernel(m_ref, g_ref, wk_ref, wv_ref, k_ref, v_ref):
    mn = _rms(m_ref[...], g_ref[...]).astype(BF16)
    k_ref[...] = jnp.dot(mn, wk_ref[...], preferred_element_type=F32)
    v_ref[...] = jnp.dot(mn, wv_ref[...], preferred_element_type=F32)


def _memory_kv(mem, g, wk, wv, tm=256):
    t = mem.shape[0]
    blk = pl.BlockSpec((tm, D_MODEL), lambda i: (i, 0))
    return pl.pallas_call(
        _memkv_kernel,
        out_shape=(jax.ShapeDtypeStruct((t, D_MODEL), F32),) * 2,
        grid=(t // tm,), in_specs=[blk, _const_spec(g.shape), _const_spec(wk.shape), _const_spec(wv.shape)],
        out_specs=(blk, blk),
        compiler_params=pltpu.CompilerParams(dimension_semantics=("parallel",), vmem_limit_bytes=VMEM_LIMIT),
        name="memory_kv",
    )(mem, g, wk, wv)


def _attn_kernel(x_ref, k_ref, v_ref, g_ref, wq_ref, wo_ref, o_ref):
    x = x_ref[...]
    xn = _rms(x, g_ref[...]).astype(BF16)
    q = jnp.dot(xn, wq_ref[...], preferred_element_type=F32)
    by_head = len(k_ref.shape) == 3
    wo = wo_ref[...]
    acc = x
    for h in range(MEM_HEADS):
        cols = slice(h * MEM_HEAD_DIM, (h + 1) * MEM_HEAD_DIM)
        k_h = k_ref[:, h, :] if by_head else k_ref[:, cols]
        v_h = v_ref[:, h, :] if by_head else v_ref[:, cols]
        sc = _dot_nt(q[:, cols], k_h) * (MEM_HEAD_DIM ** -0.5)
        sc = sc - jnp.max(sc, axis=-1, keepdims=True)
        p = jnp.exp(sc)
        p = p / jnp.sum(p, axis=-1, keepdims=True)
        o_h = _dot(p, v_h)
        acc = acc + jnp.dot(o_h.astype(BF16), wo[cols, :], preferred_element_type=F32)
    o_ref[...] = acc


def _attn(x, k, v, g, wq, wo, ts, kv0=0):
    b, s, _ = x.shape
    ts = min(ts, s)
    xblk = pl.BlockSpec((None, ts, D_MODEL), lambda i, j: (i, j, 0))
    if k.ndim == 3:
        kvblk = pl.BlockSpec((None, N_MEM, D_MODEL), lambda i, j: (i + kv0, 0, 0))
    else:
        kvblk = pl.BlockSpec((None, N_MEM, MEM_HEADS, MEM_HEAD_DIM), lambda i, j: (i + kv0, 0, 0, 0))
    return pl.pallas_call(
        _attn_kernel,
        out_shape=jax.ShapeDtypeStruct((b, s, D_MODEL), F32),
        grid=(b, s // ts),
        in_specs=[xblk, kvblk, kvblk, _const_spec(g.shape), _const_spec(wq.shape), _const_spec(wo.shape)],
        out_specs=xblk,
        compiler_params=pltpu.CompilerParams(dimension_semantics=("parallel", "parallel"),
                                             vmem_limit_bytes=VMEM_LIMIT),
        name="attn",
    )(x, k, v, g, wq, wo)


def _tree_reduce(op, x):
    n = x.shape[0]
    while n > 1:
        half = n // 2
        head = op(x[:half], x[half:2 * half])
        x = head if n % 2 == 0 else jnp.concatenate([head, x[2 * half:]], axis=0)
        n = x.shape[0]
    return x


def _top_k_rows(sc, k, payload=None):
    n = sc.shape[0]
    rows = lax.broadcasted_iota(jnp.int32, sc.shape, 0).astype(F32)
    vals, outs = [], []
    for _ in range(k):
        m = _tree_reduce(jnp.maximum, sc)
        cand = jnp.where(sc == m, rows, float(n))
        first = _tree_reduce(jnp.minimum, cand)
        hit = cand == first
        sc = jnp.where(hit, -jnp.inf, sc)
        vals.append(m)
        if payload is None:
            outs.append(first.astype(jnp.int32))
        else:
            outs.append(jnp.sum(jnp.where(hit, payload, 0), axis=0, keepdims=True))
    return jnp.concatenate(vals, axis=0), jnp.concatenate(outs, axis=0)


def _pair_candidates(v1, i1, v2, i2):
    comb = [v1[a:a + 1] + v2[0:PEER_TOPK // (a + 1)] for a in range(PEER_TOPK)]
    cidx = [i1[a:a + 1] * N_KEYS + i2[0:PEER_TOPK // (a + 1)] for a in range(PEER_TOPK)]
    return jnp.concatenate(comb, axis=0), jnp.concatenate(cidx, axis=0)


SELECT_BLOCK = SUBLANES * LANES


def _select_kernel(tb, x_ref, g_ref, wq_ref, keys_ref, *rest):
    xn_ref, eid_ref, tile_ref, shift_ref, gate_ref, s1, s2, eid_s, gate_s = rest[-9:]
    groups = tb // LANES
    assert groups == SUBLANES
    xn = _rms(x_ref[...], g_ref[...])
    xn_ref[...] = xn
    xn = xn.astype(BF16)

    def head(h, carry):
        q = jnp.dot(xn, wq_ref[h], preferred_element_type=F32)
        for g in range(groups):
            qg = q[g * LANES:(g + 1) * LANES]
            s1[pl.ds(g, N_KEYS, stride=SUBLANES), :] = _dot_nt(keys_ref[h, 0], qg[:, 0:PEER_DK_HALF])
            s2[pl.ds(g, N_KEYS, stride=SUBLANES), :] = _dot_nt(keys_ref[h, 1], qg[:, PEER_DK_HALF:PEER_DK])
        v1, i1 = _top_k_rows(s1[...].reshape(N_KEYS, SUBLANES, LANES), PEER_TOPK)
        v2, i2 = _top_k_rows(s2[...].reshape(N_KEYS, SUBLANES, LANES), PEER_TOPK)
        comb, cidx = _pair_candidates(v1, i1, v2, i2)
        top, eid = _top_k_rows(comb, PEER_TOPK, cidx)
        ex = jnp.exp(top - top[0:1])
        gate = ex / jnp.sum(ex, axis=0, keepdims=True)
        rows = pl.ds(pl.multiple_of(h * PEER_TOPK * SUBLANES, PEER_TOPK * SUBLANES), PEER_TOPK * SUBLANES)
        gate_s[rows, :] = gate.reshape(PEER_TOPK * SUBLANES, LANES)
        eid_s[rows, :] = eid.reshape(PEER_TOPK * SUBLANES, LANES)
        return carry

    lax.fori_loop(0, PEER_HEADS, head, 0)
    for g in range(groups):
        toks = slice(g * LANES, (g + 1) * LANES)
        eid = eid_s[pl.ds(g, PICKS, stride=SUBLANES), :].T
        upper = eid >= HALF_EXPERTS
        eid_ref[toks, :] = eid
        tile_ref[toks, :] = jnp.where(upper, eid - HALF_EXPERTS, eid) * SUBLANES
        shift_ref[toks, :] = jnp.where(upper, 0, 16)
        gate_ref[toks, :] = gate_s[pl.ds(g, PICKS, stride=SUBLANES), :].T


def _select(x, g, wq, keys, tb, t0, t, after=None):
    b0 = t0 // tb
    extra = [] if after is None else [after]
    xin = pl.BlockSpec((tb, D_MODEL), lambda i: (i + b0, 0))
    xblk = pl.BlockSpec((tb, D_MODEL), lambda i: (i, 0))
    picks_i = jax.ShapeDtypeStruct((t, PICKS), jnp.int32)
    return pl.pallas_call(
        functools.partial(_select_kernel, tb),
        out_shape=(jax.ShapeDtypeStruct((t, D_MODEL), F32), picks_i, picks_i, picks_i,
                   jax.ShapeDtypeStruct((t, PICKS), F32)),
        grid=(t // tb,),
        in_specs=[xin, _const_spec(g.shape), _const_spec(wq.shape), _const_spec(keys.shape)]
        + [_after_spec(a) for a in extra],
        out_specs=(xblk,) + (_pick_block(tb),) * 4,
        scratch_shapes=[pltpu.VMEM((N_KEYS * SUBLANES, LANES), F32), pltpu.VMEM((N_KEYS * SUBLANES, LANES), F32),
                        pltpu.VMEM((PICKS * SUBLANES, LANES), jnp.int32), pltpu.VMEM((PICKS * SUBLANES, LANES), F32)],
        compiler_params=pltpu.CompilerParams(dimension_semantics=("parallel",), vmem_limit_bytes=VMEM_LIMIT),
        name="peer_select",
    )(x, g, wq, keys, *extra)


HI_HALF_MASK = -(1 << 16)


def _unpack(word, shift):
    return lax.bitcast_convert_type((word << shift) & HI_HALF_MASK, F32)


def _splat_picks(blk, one_pass):
    tb = blk.shape[0]
    r = lax.broadcasted_iota(jnp.int32, (PICKS, LANES), 0)
    c = lax.broadcasted_iota(jnp.int32, (PICKS, LANES), 1)
    eye = (r == c)[None]
    ones = jnp.ones((LANES, LANES), BF16)

    def one(piece):
        diag = jnp.where(eye, piece[:, None, :], 0.0).astype(BF16).reshape(tb * PICKS, LANES)
        return jnp.dot(diag, ones, preferred_element_type=F32)

    if one_pass:
        out = one(blk)
    else:
        hi = blk.astype(BF16).astype(F32)
        rest = blk - hi
        mid = rest.astype(BF16).astype(F32)
        out = one(hi) + one(mid) + one(rest - mid)
    return out.reshape(tb, PICKS, LANES)


def _full(ref, k):
    return jnp.broadcast_to(ref[k:k + 1, :], (SUBLANES, LANES))


def _expert_row(tab_ref, tile_ref, t, k, shifts):
    idx = pl.multiple_of(tile_ref[t, k], SUBLANES)
    return _unpack(tab_ref[pl.ds(idx, SUBLANES), :], _full(shifts, k))


def _hidden_kernel(tb, tile_ref, shift_ref, x_ref, gate_ref, tab_ref, *rest):
    gh_ref, bsh, stage, parts, xt = rest[-5:]
    bsh[...] = _splat_picks(shift_ref[...].astype(F32), True).astype(jnp.int32)
    for s in range(SUBLANES):
        xt[pl.ds(s, tb, stride=SUBLANES), :] = x_ref[:, s * LANES:(s + 1) * LANES]

    half = PICKS // 2
    quad = SUBLANES // 2
    low = lax.broadcasted_iota(jnp.int32, (SUBLANES, LANES), 0) < quad

    def token(t, carry):
        x = xt[pl.ds(pl.multiple_of(t * SUBLANES, SUBLANES), SUBLANES), :]
        shifts = bsh.at[t]
        for j in range(half):
            a = x * _expert_row(tab_ref, tile_ref, t, j, shifts)
            b = x * _expert_row(tab_ref, tile_ref, t, j + half, shifts)
            stage[j * SUBLANES:(j + 1) * SUBLANES, :] = jnp.where(low, a + pltpu.roll(a, quad, 0),
                                                                  b + pltpu.roll(b, quad, 0))
        for p in range(2):
            part = stage[pl.ds(p * quad, half, stride=SUBLANES), :]
            for s in range(1, quad):
                part = part + stage[pl.ds(p * quad + s, half, stride=SUBLANES), :]
            parts[t, p * half:(p + 1) * half, :] = part
        return carry

    lax.fori_loop(0, tb, token, 0)
    gh_ref[...] = jnp.sum(parts[...], axis=-1)
    hid = gh_ref[...]
    gelu = 0.5 * hid * (1.0 + lax.erf(hid * (2.0 ** -0.5)))
    gh_ref[...] = gate_ref[...] * gelu


def _pick_block(tb, space=None):
    return pl.BlockSpec((tb, PICKS), lambda i: (i, 0), memory_space=space)


def _table_spec(tab):
    return pl.BlockSpec(tab.shape, lambda i: (0, 0), pipeline_mode=pl.Buffered(1))


def _hidden(tile, shift, xg, gate, tab, tb, after=None):
    t = xg.shape[0]
    extra = [] if after is None else [after]
    return pl.pallas_call(
        functools.partial(_hidden_kernel, tb),
        out_shape=jax.ShapeDtypeStruct((t, PICKS), F32),
        grid=(t // tb,),
        in_specs=[_pick_block(tb, pltpu.SMEM), _pick_block(tb),
                  pl.BlockSpec((tb, D_MODEL), lambda i: (i, 0)),
                  _pick_block(tb), _table_spec(tab)] + [_after_spec(a) for a in extra],
        out_specs=_pick_block(tb),
        scratch_shapes=[pltpu.VMEM((tb, PICKS, LANES), jnp.int32), pltpu.VMEM((PICKS // 2 * SUBLANES, LANES), F32),
                        pltpu.VMEM((tb, PICKS, LANES), F32), pltpu.VMEM((tb * SUBLANES, LANES), F32)],
        compiler_params=pltpu.CompilerParams(dimension_semantics=("parallel",), vmem_limit_bytes=VMEM_LIMIT),
        name="peer_hidden",
    )(tile, shift, xg, gate, tab, *extra)


def _combine_kernel(tb, tile_ref, shift_ref, gh_ref, tab_ref, x_ref, g_ref, *rest):
    o_ref, bsh, bg, ot = rest[-4:]
    bsh[...] = _splat_picks(shift_ref[...].astype(F32), True).astype(jnp.int32)
    bg[...] = _splat_picks(gh_ref[...], False)

    def token(t, carry):
        shifts, gates = bsh.at[t], bg.at[t]
        acc = [jnp.zeros((SUBLANES, LANES), F32) for _ in range(4)]
        for k in range(PICKS):
            acc[k % 4] = acc[k % 4] + _full(gates, k) * _expert_row(tab_ref, tile_ref, t, k, shifts)
        ot[pl.ds(pl.multiple_of(t * SUBLANES, SUBLANES), SUBLANES), :] = (acc[0] + acc[1]) + (acc[2] + acc[3])
        return carry

    lax.fori_loop(0, tb, token, 0)
    for s in range(SUBLANES):
        o_ref[:, s * LANES:(s + 1) * LANES] = ot[pl.ds(s, tb, stride=SUBLANES), :]
    o_ref[...] = _rms(x_ref[...] + o_ref[...], g_ref[...])


def _combine(tile, shift, gh, tab, x, g, tb, x_row0, out_row0, out_rows, into=None):
    t = gh.shape[0]
    xb0, ob0 = x_row0 // tb, out_row0 // tb
    extra = [] if into is None else [into]
    return pl.pallas_call(
        functools.partial(_combine_kernel, tb),
        out_shape=jax.ShapeDtypeStruct((out_rows, D_MODEL), F32),
        grid=(t // tb,),
        in_specs=[_pick_block(tb, pltpu.SMEM), _pick_block(tb), _pick_block(tb), _table_spec(tab),
                  pl.BlockSpec((tb, D_MODEL), lambda i: (i + xb0, 0)), _const_spec(g.shape)]
        + [pl.BlockSpec(memory_space=pl.ANY) for _ in extra],
        input_output_aliases={6: 0} if extra else {},
        out_specs=pl.BlockSpec((tb, D_MODEL), lambda i: (i + ob0, 0)),
        scratch_shapes=[pltpu.VMEM((tb, PICKS, LANES), jnp.int32), pltpu.VMEM((tb, PICKS, LANES), F32),
                        pltpu.VMEM((tb * SUBLANES, LANES), F32)],
        compiler_params=pltpu.CompilerParams(dimension_semantics=("parallel",), vmem_limit_bytes=VMEM_LIMIT),
        name="peer_combine",
    )(tile, shift, gh, tab, x, g, *extra)


SC_LANES = 16
SC_WORKERS = 32
SC_CHUNK = 32
SC_FEATURE_BLOCK = 256
SC_EARLY_SEQS = 3
SC_PROMPT_TOKENS = 13312
SC_COMBINE_HANDBACK = 512


def _sc_call(body, n, side_width, out_width):
    mesh = plsc.VectorSubcoreMesh(core_axis_name="c", subcore_axis_name="s")
    return pl.kernel(body, mesh=mesh, out_type=jax.ShapeDtypeStruct((n, out_width), F32),
                     compiler_params=pltpu.CompilerParams(needs_layout_passes=False),
                     scratch_types=[pltpu.VMEM((2 * PICKS,), jnp.int32), pltpu.VMEM((2 * side_width,), F32),
                                    pltpu.VMEM((2 * out_width,), F32),
                                    pltpu.VMEM((SC_CHUNK, D_MODEL), F32), pltpu.VMEM((SC_CHUNK, D_MODEL), F32)]
                     + [pltpu.SemaphoreType.DMA] * 5)


def _sc_token_loop(n, eid_hbm, side_hbm, tab_hbm, out_hbm, idx_v, side_v, out_v, bufs, sems, compute):
    sem_rows, sem_fetch, sem_put = sems[0:2], sems[2], sems[3:5]
    per_worker = n // SC_WORKERS
    first = (lax.axis_index("s") * 2 + lax.axis_index("c")) * per_worker
    last = first + per_worker - 1
    chunks = PICKS // SC_CHUNK
    side_w = side_v.shape[0] // 2
    out_w = out_v.shape[0] // 2

    def fetch(t, slot):
        return (pltpu.make_async_copy(eid_hbm.at[t], idx_v.at[pl.ds(slot * PICKS, PICKS)], sem_fetch),
                pltpu.make_async_copy(side_hbm.at[t], side_v.at[pl.ds(slot * side_w, side_w)], sem_fetch))

    def gather(slot, c):
        ids = idx_v.at[pl.ds(slot * PICKS + c * SC_CHUNK, SC_CHUNK)]
        return pltpu.make_async_copy(tab_hbm.at[ids], bufs[c % 2], sem_rows[c % 2])

    def put(t, slot):
        return pltpu.make_async_copy(out_v.at[pl.ds(slot * out_w, out_w)], out_hbm.at[t], sem_put[slot])

    for cp in fetch(first, 0):
        cp.start()
    for cp in fetch(first, 0):
        cp.wait()
    gather(0, 0).start()
    gather(0, 1).start()

    @pl.loop(0, per_worker, step=2)
    def _(i):
        for slot in range(2):
            t = first + i + slot
            nxt = jnp.minimum(t + 1, last)
            for cp in fetch(nxt, 1 - slot):
                cp.start()

            @pl.when(i >= 2)
            def _():
                put(t, slot).wait()

            for c in range(chunks):
                gather(slot, c).wait()
                compute(c, bufs[c % 2], slot * side_w, slot * out_w)
                if c + 2 < chunks:
                    gather(slot, c + 2).start()
                else:
                    if c + 2 == chunks:
                        for cp in fetch(nxt, 1 - slot):
                            cp.wait()
                    gather(1 - slot, c + 2 - chunks).start()
            put(t, slot).start()

    gather(0, 0).wait()
    gather(0, 1).wait()
    put(last, 0).wait()
    put(last, 1).wait()


def _sc_hidden(eid, x, tab):
    n = eid.shape[0]
    lanes = SC_LANES

    def body(eid_hbm, x_hbm, tab_hbm, out_hbm, idx_v, x_v, hid_v, rows0, rows1, *sems):
        lane = lax.iota(jnp.int32, lanes)

        def compute(c, rows, x0, h0):
            for g in range(SC_CHUNK // lanes):
                def step(j, accs):
                    off = pl.multiple_of(j * lanes, lanes)
                    xj = x_v[pl.ds(x0 + off, lanes)]
                    return tuple(accs[r] + xj * rows[g * lanes + r, pl.ds(off, lanes)] for r in range(lanes))
                accs = lax.fori_loop(0, D_MODEL // lanes, step,
                                     tuple(jnp.zeros((lanes,), F32) for _ in range(lanes)))
                hv = jnp.zeros((lanes,), F32)
                for r in range(lanes):
                    hv = jnp.where(lane == r, jnp.sum(accs[r]), hv)
                hid_v[pl.ds(h0 + c * SC_CHUNK + g * lanes, lanes)] = hv

        _sc_token_loop(n, eid_hbm, x_hbm, tab_hbm, out_hbm, idx_v, x_v, hid_v, (rows0, rows1), sems, compute)

    return _sc_call(body, n, D_MODEL, PICKS)(eid, x, tab)


def _sc_combine(eid, gh, tab, n=None):
    n = eid.shape[0] if n is None else n
    lanes = SC_LANES
    nacc = SC_FEATURE_BLOCK // lanes

    def body(eid_hbm, g_hbm, tab_hbm, out_hbm, idx_v, g_v, out_v, rows0, rows1, *sems):
        def compute(c, rows, g0, o0):
            for fb in range(D_MODEL // SC_FEATURE_BLOCK):
                f0 = fb * SC_FEATURE_BLOCK
                if c == 0:
                    init = tuple(jnp.zeros((lanes,), F32) for _ in range(nacc))
                else:
                    init = tuple(out_v[pl.ds(o0 + f0 + q * lanes, lanes)] for q in range(nacc))

                def step(r, accs):
                    gk = plsc.load_gather(g_v, [jnp.full((lanes,), g0 + c * SC_CHUNK + r, jnp.int32)])
                    return tuple(accs[q] + gk * rows[r, pl.ds(f0 + q * lanes, lanes)] for q in range(nacc))
                accs = lax.fori_loop(0, SC_CHUNK, step, init)
                for q in range(nacc):
                    out_v[pl.ds(o0 + f0 + q * lanes, lanes)] = accs[q]

        _sc_token_loop(n, eid_hbm, g_hbm, tab_hbm, out_hbm, idx_v, g_v, out_v, (rows0, rows1), sems, compute)

    return _sc_call(body, n, PICKS, D_MODEL)(eid, gh, tab)


def _after_spec(after):
    return pl.BlockSpec((SUBLANES, after.shape[1]), lambda i: (0, 0))


def _gate_gelu_kernel(h_ref, g_ref, after_ref, o_ref):
    hid = h_ref[...]
    o_ref[...] = g_ref[...] * (0.5 * hid * (1.0 + lax.erf(hid * (2.0 ** -0.5))))


def _gate_gelu(hid, gate, tb, after):
    t = hid.shape[0]
    return pl.pallas_call(
        _gate_gelu_kernel, out_shape=jax.ShapeDtypeStruct((t, PICKS), F32), grid=(t // tb,),
        in_specs=[_pick_block(tb), _pick_block(tb), _after_spec(after)], out_specs=_pick_block(tb),
        compiler_params=pltpu.CompilerParams(dimension_semantics=("parallel",)),
        name="gate_gelu",
    )(hid, gate, after)


def _final_kernel(x_ref, p_ref, g_ref, *rest):
    rest[-1][...] = _rms(x_ref[...] + p_ref[...], g_ref[...])


def _final(x, p, g, tb, out_row0, out_rows, into=None):
    b0 = out_row0 // tb
    local = pl.BlockSpec((tb, D_MODEL), lambda i: (i, 0))
    in_specs = [local, local, _const_spec(g.shape)]
    args = [x, p, g]
    aliases = {}
    if into is not None:
        aliases = {len(args): 0}
        in_specs.append(pl.BlockSpec(memory_space=pl.ANY))
        args.append(into)
    return pl.pallas_call(
        _final_kernel, out_shape=jax.ShapeDtypeStruct((out_rows, D_MODEL), F32), grid=(p.shape[0] // tb,),
        in_specs=in_specs, out_specs=pl.BlockSpec((tb, D_MODEL), lambda i: (i + b0, 0)),
        input_output_aliases=aliases,
        compiler_params=pltpu.CompilerParams(dimension_semantics=("parallel",), vmem_limit_bytes=VMEM_LIMIT),
        name="final_norm",
    )(*args)


PACK_BLOCK = 256


def _pack_kernel(lo_ref, hi_ref, o_ref):
    def bf16_bits(x):
        return lax.bitcast_convert_type(x.astype(BF16).astype(F32), jnp.int32)
    word = bf16_bits(hi_ref[...]) | lax.shift_right_logical(bf16_bits(lo_ref[...]), 16)
    for s in range(SUBLANES):
        o_ref[pl.ds(s, PACK_BLOCK, stride=SUBLANES), :] = word[:, s * LANES:(s + 1) * LANES]


def _pack_table(tab):
    steps = HALF_EXPERTS // PACK_BLOCK
    return pl.pallas_call(
        _pack_kernel, out_shape=jax.ShapeDtypeStruct((HALF_EXPERTS * SUBLANES, LANES), jnp.int32), grid=(steps,),
        in_specs=[pl.BlockSpec((PACK_BLOCK, D_MODEL), lambda i: (i, 0)),
                  pl.BlockSpec((PACK_BLOCK, D_MODEL), lambda i: (i + steps, 0))],
        out_specs=pl.BlockSpec((PACK_BLOCK * SUBLANES, LANES), lambda i: (i, 0)),
        compiler_params=pltpu.CompilerParams(dimension_semantics=("parallel",)),
        name="pack_table",
    )(tab, tab)


def _col(v, n=LANES):
    return jnp.pad(v.astype(F32), (0, n - v.shape[0])).reshape(1, n)


def _row(v, n=2 * SUBLANES):
    return jnp.pad(v.astype(F32), (0, n - v.shape[0])).reshape(n, 1)


def _peer(x, w, tb_sel, tb_exp, n_sc, out_row0=0, out_rows=None, after=None, into=None, between=None):
    t = x.shape[0]
    out_rows = t if out_rows is None else out_rows
    n_tc = t - n_sc
    n_sc2 = max(n_sc - SC_COMBINE_HANDBACK, 0)
    if n_sc:
        xn_s, eid_s, tile_s, shift_s, gate_s = _select(x, w["ln_ffn"], w["w_pq"], w["sub_keys"], SELECT_BLOCK, 0,
                                                       n_sc, after=after)
        hid_s = _sc_hidden(eid_s, xn_s, w["u_tab"])
    xn, _, tile, shift, gate = _select(x, w["ln_ffn"], w["w_pq"], w["sub_keys"], SELECT_BLOCK,n_sc, n_tc,
                                       after=xn_s if n_sc else None)
    gh = _hidden(tile, shift, xn, gate, w["u_pack"], tb_exp, after=into)
    if n_sc:
        other = None if between is None else between(gh)
        gh_s = _gate_gelu(hid_s, gate_s, tb_sel, after=gh if other is None else other)
        out_s = _sc_combine(eid_s, gh_s, w["v_tab"], n_sc2)
    y = _combine(tile, shift, gh, w["v_pack"], x, w["final_norm"], tb_exp, n_sc, out_row0 + n_sc, out_rows, into)
    if n_sc > n_sc2:
        y = _combine(tile_s[n_sc2:], shift_s[n_sc2:], gh_s[n_sc2:], w["v_pack"], x, w["final_norm"], tb_exp,
                     n_sc2, out_row0 + n_sc2, out_rows, y)
    if n_sc2:
        y = _final(x, out_s, w["final_norm"], tb_sel, out_row0, out_rows, into=y)
    return (y, other) if between is not None else y


def kernel(x_prompt, x_sample, mem_prompt, cache_conv_a, cache_conv_ssd, state_ssd, cache_mem_k, cache_mem_v, ln_mix_w, w_in, conv_a_w, conv_s_w, conv_s_b, dt_bias, a_log, d_skip, ssd_norm_w, w_out, ln_mem_w, mem_norm_w, w_mq, w_mk, w_mv, w_mo, ln_ffn_w, w_pq, sub_keys, u_tab, v_tab, final_norm_w):
    depth = w_in.shape[0]
    assert depth == 1
    l = 0
    n_main = 3 * D_CONV + D_SSD + XBC_DIM
    w_dt = w_in[l][:, n_main:]
    w = {
        "ln_mix": ln_mix_w[l].reshape(1, D_MODEL),
        "w_in": w_in[l][:, :n_main].astype(BF16),
        "w_dt": jnp.pad(w_dt, ((0, 0), (0, LANES - SSD_HEADS))).astype(BF16),
        "w_dtt": jnp.pad(w_dt.T, ((0, 2 * SUBLANES - SSD_HEADS), (0, 0))).astype(BF16),
        "conv_a_w": conv_a_w[l], "conv_s_w": conv_s_w[l], "conv_s_b": conv_s_b[l].reshape(1, XBC_DIM),
        "dtb_c": _col(dt_bias[l]), "alog_c": _col(a_log[l]), "dsk_c": _col(d_skip[l]),
        "dtb_r": _row(dt_bias[l]), "alog_r": _row(a_log[l]),
        "ssd_norm": ssd_norm_w[l].reshape(1, D_SSD),
        "w_out": w_out[l].astype(BF16),
        "ln_ffn": ln_ffn_w[l].reshape(1, D_MODEL),
        "w_pq": w_pq[l].astype(BF16).reshape(D_MODEL, PEER_HEADS, PEER_DK).transpose(1, 0, 2),
        "sub_keys": sub_keys[l].astype(BF16),
        "u_pack": _pack_table(u_tab[l]),
        "v_pack": _pack_table(v_tab[l]),
        "u_tab": u_tab[l], "v_tab": v_tab[l],
        "final_norm": final_norm_w.reshape(1, D_MODEL),
    }
    ln_mem = ln_mem_w[l].reshape(1, D_MODEL)
    wq, wo = w_mq[l].astype(BF16), w_mo[l].astype(BF16)

    bp, sp, _ = x_prompt.shape
    bs, ss, _ = x_sample.shape

    mk, mv = _memory_kv(mem_prompt.reshape(bp * N_MEM, D_MODEL), mem_norm_w[l].reshape(1, D_MODEL),
                        w_mk[l].astype(BF16), w_mv[l].astype(BF16))
    mk3, mv3 = mk.reshape(bp, N_MEM, D_MODEL), mv.reshape(bp, N_MEM, D_MODEL)
    n1 = SC_EARLY_SEQS
    h1, pa1, ps1, ph1 = _mixer(x_prompt, None, w, 512, 0, n1)
    h1 = _attn(h1, mk3, mv3, ln_mem, wq, wo, 256).reshape(n1 * sp, D_MODEL)

    xn1, eid1, _, _, gate1 = _select(h1, w["ln_ffn"], w["w_pq"], w["sub_keys"], SELECT_BLOCK,0, n1 * sp)
    hid1 = _sc_hidden(eid1, xn1, w["u_tab"])
    h2, pa2, ps2, ph2 = _mixer(x_prompt, None, w, 512, n1, bp - n1, after=xn1)
    h2 = _attn(h2, mk3, mv3, ln_mem, wq, wo, 256, kv0=n1).reshape((bp - n1) * sp, D_MODEL)

    hs, sa, ssd, sh = _mixer(x_sample, (cache_conv_a[l], cache_conv_ssd[l],
                                        state_ssd[l].reshape((bs,) + PAIR_STATE)), w, 256, after=h2)
    hs = _attn(hs, cache_mem_k[l], cache_mem_v[l], ln_mem, wq, wo, 256).reshape(bs * ss, D_MODEL)
    xn_q, _, tile_q, shift_q, gate_q = _select(hs, w["ln_ffn"], w["w_pq"], w["sub_keys"], SELECT_BLOCK,0, bs * ss)

    gh1 = _gate_gelu(hid1, gate1, 256, after=xn_q)
    out1 = _sc_combine(eid1, gh1, w["v_tab"])
    y1 = _final(h1, out1, w["final_norm"], 256, 0, bp * sp)
    def sample_experts(gh_main):
        gh_q = _hidden(tile_q, shift_q, xn_q, gate_q, w["u_pack"], 64, after=gh_main)
        return _combine(tile_q, shift_q, gh_q, w["v_pack"], hs, w["final_norm"], 64, 0, 0, bs * ss)

    y_prompt, y_sample = _peer(h2, w, 256, 64, SC_PROMPT_TOKENS, n1 * sp, bp * sp, after=gh1, into=y1,
                               between=sample_experts)
    y_prompt, y_sample = y_prompt.reshape(bp, sp, D_MODEL), y_sample.reshape(bs, ss, D_MODEL)
    pa, ps, ph = (jnp.concatenate(p, axis=0) for p in ((pa1, pa2), (ps1, ps2), (ph1, ph2)))

    kv_shape = (1, bp, N_MEM, MEM_HEADS, MEM_HEAD_DIM)
    st_shape = (SSD_HEADS, SSD_HEAD_DIM, SSD_STATE)
    return (y_prompt, y_sample, pa[None], ps[None], ph.reshape((1, bp) + st_shape), mk.reshape(kv_shape),
            mv.reshape(kv_shape), sa[None], ssd[None], sh.reshape((1, bs) + st_shape))
```

```python
import functools

import jax
import jax.numpy as jnp
from jax import lax
from jax.experimental import pallas as pl
from jax.experimental.pallas import tpu as pltpu
from jax.experimental.pallas import tpu_sc as plsc

D_MODEL = 1024
CHUNK = 64
D_CONV = 512
CONV_A_W = 3
D_SSD = 512
SSD_HEAD_DIM = 64
SSD_HEADS = 8
SSD_GROUPS = 2
SSD_HPG = 4
SSD_STATE = 128
SSD_CONV_W = 4
XBC_DIM = 1024
N_MEM = 256
MEM_HEADS = 4
MEM_HEAD_DIM = 256
PEER_HEADS = 8
N_KEYS = 128
N_EXPERTS = N_KEYS * N_KEYS
PEER_TOPK = 16
PEER_DK = 256
PEER_DK_HALF = 128
PICKS = PEER_HEADS * PEER_TOPK
EPS = 1e-6

LANES = 128
SUBLANES = 8
HALF_EXPERTS = N_EXPERTS // 2
PAIR_STATE = (SSD_HEADS // 2, 2 * SSD_HEAD_DIM, SSD_STATE)
VMEM_LIMIT = 56 * 1024 * 1024

F32 = jnp.float32
BF16 = jnp.bfloat16
HI = lax.Precision.HIGHEST


def _rms(x, g):
    return x * lax.rsqrt(jnp.mean(x * x, axis=-1, keepdims=True) + EPS) * g


def _silu(x):
    return x * (1.0 / (1.0 + jnp.exp(-x)))


def _softplus(x):
    return jnp.maximum(x, 0.0) + jnp.log1p(jnp.exp(-jnp.abs(x)))


def _dot(a, b):
    return jnp.dot(a.astype(BF16), b.astype(BF16), preferred_element_type=F32)


def _dot_nt(a, b):
    return lax.dot_general(a.astype(BF16), b.astype(BF16), (((1,), (1,)), ((), ())),
                           preferred_element_type=F32)


def _dot_tn(a, b):
    return lax.dot_general(a.astype(BF16), b.astype(BF16), (((0,), (0,)), ((), ())),
                           preferred_element_type=F32)


def _const_spec(shape):
    n = len(shape)
    return pl.BlockSpec(shape, lambda *_: (0,) * n)


def _mixer_kernel(has_hist, has_after, ts, lc, *refs):
    if has_after:
        refs = refs[1:]
    if has_hist:
        x_ref, hista_ref, hists_ref, h0_ref = refs[:4]
        refs = refs[4:]
    else:
        x_ref = refs[0]
        refs = refs[1:]
    (lnw_ref, win_ref, wdt_ref, wdtt_ref, caw_ref, csw_ref, csb_ref, dtb_c_ref, alog_c_ref, dsk_c_ref,
     dtb_r_ref, alog_r_ref, nw_ref, wout_ref,
     h1_ref, na_ref, ns_ref, nh_ref, cata, cats, hst) = refs
    s = pl.program_id(1)
    pad = SUBLANES

    @pl.when(s == 0)
    def _():
        if has_hist:
            cata[pad - 2:pad, :] = hista_ref[...]
            cats[pad - 3:pad, :] = hists_ref[...]
            hst[...] = h0_ref[...]
        else:
            cata[0:pad, :] = jnp.zeros((pad, D_CONV), F32)
            cats[0:pad, :] = jnp.zeros((pad, XBC_DIM), F32)
            hst[...] = jnp.zeros(hst.shape, F32)

    x = x_ref[...]
    xn = _rms(x, lnw_ref[...]).astype(BF16)
    proj = jnp.dot(xn, win_ref[...], preferred_element_type=F32)
    dt_c = jnp.dot(xn, wdt_ref[...], preferred_element_type=F32)
    dt_r = lax.dot_general(wdtt_ref[...], xn, (((1,), (1,)), ((), ())),
                           preferred_element_type=F32)
    g_b = proj[:, 0:D_CONV]
    g_c = proj[:, D_CONV:2 * D_CONV]
    v_in = proj[:, 2 * D_CONV:3 * D_CONV]
    z = proj[:, 3 * D_CONV:3 * D_CONV + D_SSD]
    xbc = proj[:, 3 * D_CONV + D_SSD:3 * D_CONV + D_SSD + XBC_DIM]

    ua = g_c * v_in
    cata[pad:pad + ts, :] = ua
    caw = caw_ref[...]
    conv_a = (caw[0:1] * cata[pad - 2:pad - 2 + ts, :] + caw[1:2] * cata[pad - 1:pad - 1 + ts, :]
              + caw[2:3] * ua)
    y_a = g_b * conv_a
    hist_a = cata[pad + ts - 2:pad + ts, :]
    na_ref[...] = hist_a
    cata[pad - 2:pad, :] = hist_a

    cats[pad:pad + ts, :] = xbc
    csw = csw_ref[...]
    conv_s = (csw[0:1] * cats[pad - 3:pad - 3 + ts, :] + csw[1:2] * cats[pad - 2:pad - 2 + ts, :]
              + csw[2:3] * cats[pad - 1:pad - 1 + ts, :] + csw[3:4] * xbc)
    hist_s = cats[pad + ts - 3:pad + ts, :]
    ns_ref[...] = hist_s
    cats[pad - 3:pad, :] = hist_s
    xbc_c = _silu(conv_s + csb_ref[...])
    xs = xbc_c[:, 0:D_SSD]

    dtp_c = _softplus(dt_c + dtb_c_ref[...])
    dtp_r = _softplus(dt_r + dtb_r_ref[...])
    a_c = dtp_c * (-jnp.exp(alog_c_ref[...]))
    a_r = dtp_r * (-jnp.exp(alog_r_ref[...]))
    dsk_c = dsk_c_ref[...]

    lane = lax.broadcasted_iota(jnp.int32, (1, LANES), 1)
    first_head = lane < SSD_HEAD_DIM
    row2 = lax.broadcasted_iota(jnp.int32, (2 * SSD_HEAD_DIM, 1), 0) < SSD_HEAD_DIM
    li = lax.broadcasted_iota(jnp.int32, (lc, lc), 0)
    si = lax.broadcasted_iota(jnp.int32, (lc, lc), 1)
    causal = si <= li
    tril = causal.astype(F32)
    jrow = lax.broadcasted_iota(jnp.int32, (ts, lc), 0)
    scol = lax.broadcasted_iota(jnp.int32, (ts, lc), 1)

    def pair(col, h):
        return jnp.where(first_head, col[:, h:h + 1], col[:, h + 1:h + 2])

    y_chunks = []
    for c in range(ts // lc):
        r0 = c * lc
        rows = slice(r0, r0 + lc)
        acum_c = jnp.dot(tril, a_c[rows], preferred_element_type=F32, precision=HI)
        upper = ((jrow >= r0) & (jrow <= r0 + scol)).astype(F32)
        acum_r = jnp.dot(a_r, upper, preferred_element_type=F32, precision=HI)
        last_c = acum_c[lc - 1:lc, :]
        dec_c = jnp.exp(last_c - acum_c)
        eac_c = jnp.exp(acum_c)
        y_pairs = []
        for g in range(SSD_GROUPS):
            b_g = xbc_c[rows, D_SSD + g * SSD_STATE:D_SSD + (g + 1) * SSD_STATE]
            c_g = xbc_c[rows, D_SSD + (SSD_GROUPS + g) * SSD_STATE:D_SSD + (SSD_GROUPS + g + 1) * SSD_STATE]
            cb = _dot_nt(c_g, b_g)
            for q in range(SSD_HPG // 2):
                h = g * SSD_HPG + 2 * q
                pi = h // 2
                xs_p = xs[rows, pi * LANES:(pi + 1) * LANES]
                xdt = xs_p * pair(dtp_c[rows], h)
                res = []
                for hh in (h, h + 1):
                    seg = acum_c[:, hh:hh + 1] - acum_r[hh:hh + 1, :]
                    m_h = cb * jnp.exp(jnp.where(causal, seg, -jnp.inf))
                    res.append(_dot(m_h, xdt))
                y_diag = jnp.where(first_head, res[0], res[1])
                h_in = hst[pi]
                y_off = _dot_nt(c_g, h_in) * pair(eac_c, h)
                st = _dot_tn(xdt * pair(dec_c, h), b_g)
                cd = jnp.exp(jnp.where(row2, last_c[:, h:h + 1], last_c[:, h + 1:h + 2]))
                hst[pi] = cd * h_in + st
                y_pairs.append(y_diag + y_off + pair(dsk_c, h) * xs_p)
        y_chunks.append(jnp.concatenate(y_pairs, axis=1))
    y = y_chunks[0] if len(y_chunks) == 1 else jnp.concatenate(y_chunks, axis=0)
    y_b = _rms(y * _silu(z), nw_ref[...])
    wout = wout_ref[...]
    out = (jnp.dot(y_a.astype(BF16), wout[0:D_CONV], preferred_element_type=F32)
           + jnp.dot(y_b.astype(BF16), wout[D_CONV:], preferred_element_type=F32))
    h1_ref[...] = x + out
    nh_ref[...] = hst[...]


def _mixer(x, hist, w, ts, b0=0, nb=None, after=None):
    _, s, _ = x.shape
    b = x.shape[0] if nb is None else nb
    lc = min(CHUNK, s)
    ts = min(ts, s)
    has_hist = hist is not None
    assert not (has_hist and b0)
    per_b3 = lambda shape: pl.BlockSpec((None,) + shape, lambda i, j: (i,) + (0,) * len(shape))
    in_specs = [pl.BlockSpec((None, ts, D_MODEL), lambda i, j: (i + b0, j, 0))]
    args = [x]
    if after is not None:
        in_specs.insert(0, pl.BlockSpec((SUBLANES, after.shape[1]), lambda i, j: (0, 0)))
        args.insert(0, after)
    if has_hist:
        in_specs += [per_b3((CONV_A_W - 1, D_CONV)), per_b3((SSD_CONV_W - 1, XBC_DIM)),
                     per_b3(PAIR_STATE)]
        args += list(hist)
    wargs = [w["ln_mix"], w["w_in"], w["w_dt"], w["w_dtt"], w["conv_a_w"], w["conv_s_w"], w["conv_s_b"],
             w["dtb_c"], w["alog_c"], w["dsk_c"], w["dtb_r"], w["alog_r"], w["ssd_norm"], w["w_out"]]
    in_specs += [_const_spec(a.shape) for a in wargs]
    args += wargs
    out_shape = (jax.ShapeDtypeStruct((b, s, D_MODEL), F32),
                 jax.ShapeDtypeStruct((b, CONV_A_W - 1, D_CONV), F32),
                 jax.ShapeDtypeStruct((b, SSD_CONV_W - 1, XBC_DIM), F32),
                 jax.ShapeDtypeStruct((b,) + PAIR_STATE, F32))
    out_specs = (pl.BlockSpec((None, ts, D_MODEL), lambda i, j: (i, j, 0)),
                 per_b3((CONV_A_W - 1, D_CONV)), per_b3((SSD_CONV_W - 1, XBC_DIM)),
                 per_b3(PAIR_STATE))
    return pl.pallas_call(
        functools.partial(_mixer_kernel, has_hist, after is not None, ts, lc),
        out_shape=out_shape, grid=(b, s // ts), in_specs=in_specs, out_specs=out_specs,
        scratch_shapes=[pltpu.VMEM((SUBLANES + ts, D_CONV), F32), pltpu.VMEM((SUBLANES + ts, XBC_DIM), F32),
                        pltpu.VMEM(PAIR_STATE, F32)],
        compiler_params=pltpu.CompilerParams(dimension_semantics=("parallel", "arbitrary"),
                                             vmem_limit_bytes=VMEM_LIMIT),
        name="mixer_hist" if has_hist else "mixer",
    )(*args)


def _memkv_kernel(m_ref, g_ref, wk_ref, wv_ref, k_ref, v_ref):
    mn = _rms(m_ref[...], g_ref[...]).astype(BF16)
    k_ref[...] = jnp.dot(mn, wk_ref[...], preferred_element_type=F32)
    v_ref[...] = jnp.dot(mn, wv_ref[...], preferred_element_type=F32)


def _memory_kv(mem, g, wk, wv, tm=256):
    t = mem.shape[0]
    blk = pl.BlockSpec((tm, D_MODEL), lambda i: (i, 0))
    return pl.pallas_call(
        _memkv_kernel,
        out_shape=(jax.ShapeDtypeStruct((t, D_MODEL), F32),) * 2,
        grid=(t // tm,), in_specs=[blk, _const_spec(g.shape), _const_spec(wk.shape), _const_spec(wv.shape)],
        out_specs=(blk, blk),
        compiler_params=pltpu.CompilerParams(dimension_semantics=("parallel",), vmem_limit_bytes=VMEM_LIMIT),
        name="memory_kv",
    )(mem, g, wk, wv)


def _attn_kernel(x_ref, k_ref, v_ref, g_ref, wq_ref, wo_ref, o_ref):
    x = x_ref[...]
    xn = _rms(x, g_ref[...]).astype(BF16)
    q = jnp.dot(xn, wq_ref[...], preferred_element_type=F32)
    by_head = len(k_ref.shape) == 3
    wo = wo_ref[...]
    acc = x
    for h in range(MEM_HEADS):
        cols = slice(h * MEM_HEAD_DIM, (h + 1) * MEM_HEAD_DIM)
        k_h = k_ref[:, h, :] if by_head else k_ref[:, cols]
        v_h = v_ref[:, h, :] if by_head else v_ref[:, cols]
        sc = _dot_nt(q[:, cols], k_h) * (MEM_HEAD_DIM ** -0.5)
        sc = sc - jnp.max(sc, axis=-1, keepdims=True)
        p = jnp.exp(sc)
        p = p / jnp.sum(p, axis=-1, keepdims=True)
        o_h = _dot(p, v_h)
        acc = acc + jnp.dot(o_h.astype(BF16), wo[cols, :], preferred_element_type=F32)
    o_ref[...] = acc


def _attn(x, k, v, g, wq, wo, ts, kv0=0):
    b, s, _ = x.shape
    ts = min(ts, s)
    xblk = pl.BlockSpec((None, ts, D_MODEL), lambda i, j: (i, j, 0))
    if k.ndim == 3:
        kvblk = pl.BlockSpec((None, N_MEM, D_MODEL), lambda i, j: (i + kv0, 0, 0))
    else:
        kvblk = pl.BlockSpec((None, N_MEM, MEM_HEADS, MEM_HEAD_DIM), lambda i, j: (i + kv0, 0, 0, 0))
    return pl.pallas_call(
        _attn_kernel,
        out_shape=jax.ShapeDtypeStruct((b, s, D_MODEL), F32),
        grid=(b, s // ts),
        in_specs=[xblk, kvblk, kvblk, _const_spec(g.shape), _const_spec(wq.shape), _const_spec(wo.shape)],
        out_specs=xblk,
        compiler_params=pltpu.CompilerParams(dimension_semantics=("parallel", "parallel"),
                                             vmem_limit_bytes=VMEM_LIMIT),
        name="attn",
    )(x, k, v, g, wq, wo)


def _tree_reduce(op, x):
    n = x.shape[0]
    while n > 1:
        half = n // 2
        head = op(x[:half], x[half:2 * half])
        x = head if n % 2 == 0 else jnp.concatenate([head, x[2 * half:]], axis=0)
        n = x.shape[0]
    return x


def _top_k_rows(sc, k, payload=None):
    n = sc.shape[0]
    rows = lax.broadcasted_iota(jnp.int32, sc.shape, 0).astype(F32)
    vals, outs = [], []
    for _ in range(k):
        m = _tree_reduce(jnp.maximum, sc)
        cand = jnp.where(sc == m, rows, float(n))
        first = _tree_reduce(jnp.minimum, cand)
        hit = cand == first
        sc = jnp.where(hit, -jnp.inf, sc)
        vals.append(m)
        if payload is None:
            outs.append(first.astype(jnp.int32))
        else:
            outs.append(jnp.sum(jnp.where(hit, payload, 0), axis=0, keepdims=True))
    return jnp.concatenate(vals, axis=0), jnp.concatenate(outs, axis=0)


def _pair_candidates(v1, i1, v2, i2):
    comb = [v1[a:a + 1] + v2[0:PEER_TOPK // (a + 1)] for a in range(PEER_TOPK)]
    cidx = [i1[a:a + 1] * N_KEYS + i2[0:PEER_TOPK // (a + 1)] for a in range(PEER_TOPK)]
    return jnp.concatenate(comb, axis=0), jnp.concatenate(cidx, axis=0)


SELECT_BLOCK = SUBLANES * LANES


def _select_kernel(tb, x_ref, g_ref, wq_ref, keys_ref, *rest):
    xn_ref, eid_ref, tile_ref, shift_ref, gate_ref, s1, s2, eid_s, gate_s = rest[-9:]
    groups = tb // LANES
    assert groups == SUBLANES
    xn = _rms(x_ref[...], g_ref[...])
    xn_ref[...] = xn
    xn = xn.astype(BF16)

    def head(h, carry):
        q = jnp.dot(xn, wq_ref[h], preferred_element_type=F32)
        for g in range(groups):
            qg = q[g * LANES:(g + 1) * LANES]
            s1[pl.ds(g, N_KEYS, stride=SUBLANES), :] = _dot_nt(keys_ref[h, 0], qg[:, 0:PEER_DK_HALF])
            s2[pl.ds(g, N_KEYS, stride=SUBLANES), :] = _dot_nt(keys_ref[h, 1], qg[:, PEER_DK_HALF:PEER_DK])
        v1, i1 = _top_k_rows(s1[...].reshape(N_KEYS, SUBLANES, LANES), PEER_TOPK)
        v2, i2 = _top_k_rows(s2[...].reshape(N_KEYS, SUBLANES, LANES), PEER_TOPK)
        comb, cidx = _pair_candidates(v1, i1, v2, i2)
        top, eid = _top_k_rows(comb, PEER_TOPK, cidx)
        ex = jnp.exp(top - top[0:1])
        gate = ex / jnp.sum(ex, axis=0, keepdims=True)
        rows = pl.ds(pl.multiple_of(h * PEER_TOPK * SUBLANES, PEER_TOPK * SUBLANES), PEER_TOPK * SUBLANES)
        gate_s[rows, :] = gate.reshape(PEER_TOPK * SUBLANES, LANES)
        eid_s[rows, :] = eid.reshape(PEER_TOPK * SUBLANES, LANES)
        return carry

    lax.fori_loop(0, PEER_HEADS, head, 0)
    for g in range(groups):
        toks = slice(g * LANES, (g + 1) * LANES)
        eid = eid_s[pl.ds(g, PICKS, stride=SUBLANES), :].T
        upper = eid >= HALF_EXPERTS
        eid_ref[toks, :] = eid
        tile_ref[toks, :] = jnp.where(upper, eid - HALF_EXPERTS, eid) * SUBLANES
        shift_ref[toks, :] = jnp.where(upper, 0, 16)
        gate_ref[toks, :] = gate_s[pl.ds(g, PICKS, stride=SUBLANES), :].T


def _select(x, g, wq, keys, tb, t0, t, after=None):
    b0 = t0 // tb
    extra = [] if after is None else [after]
    xin = pl.BlockSpec((tb, D_MODEL), lambda i: (i + b0, 0))
    xblk = pl.BlockSpec((tb, D_MODEL), lambda i: (i, 0))
    picks_i = jax.ShapeDtypeStruct((t, PICKS), jnp.int32)
    return pl.pallas_call(
        functools.partial(_select_kernel, tb),
        out_shape=(jax.ShapeDtypeStruct((t, D_MODEL), F32), picks_i, picks_i, picks_i,
                   jax.ShapeDtypeStruct((t, PICKS), F32)),
        grid=(t // tb,),
        in_specs=[xin, _const_spec(g.shape), _const_spec(wq.shape), _const_spec(keys.shape)]
        + [_after_spec(a) for a in extra],
        out_specs=(xblk,) + (_pick_block(tb),) * 4,
        scratch_shapes=[pltpu.VMEM((N_KEYS * SUBLANES, LANES), F32), pltpu.VMEM((N_KEYS * SUBLANES, LANES), F32),
                        pltpu.VMEM((PICKS * SUBLANES, LANES), jnp.int32), pltpu.VMEM((PICKS * SUBLANES, LANES), F32)],
        compiler_params=pltpu.CompilerParams(dimension_semantics=("parallel",), vmem_limit_bytes=VMEM_LIMIT),
        name="peer_select",
    )(x, g, wq, keys, *extra)


HI_HALF_MASK = -(1 << 16)


def _unpack(word, shift):
    return lax.bitcast_convert_type((word << shift) & HI_HALF_MASK, F32)


def _splat_picks(blk, one_pass):
    tb = blk.shape[0]
    r = lax.broadcasted_iota(jnp.int32, (PICKS, LANES), 0)
    c = lax.broadcasted_iota(jnp.int32, (PICKS, LANES), 1)
    eye = (r == c)[None]
    ones = jnp.ones((LANES, LANES), BF16)

    def one(piece):
        diag = jnp.where(eye, piece[:, None, :], 0.0).astype(BF16).reshape(tb * PICKS, LANES)
        return jnp.dot(diag, ones, preferred_element_type=F32)

    if one_pass:
        out = one(blk)
    else:
        hi = blk.astype(BF16).astype(F32)
        rest = blk - hi
        mid = rest.astype(BF16).astype(F32)
        out = one(hi) + one(mid) + one(rest - mid)
    return out.reshape(tb, PICKS, LANES)


def _full(ref, k):
    return jnp.broadcast_to(ref[k:k + 1, :], (SUBLANES, LANES))


def _expert_row(tab_ref, tile_ref, t, k, shifts):
    idx = pl.multiple_of(tile_ref[t, k], SUBLANES)
    return _unpack(tab_ref[pl.ds(idx, SUBLANES), :], _full(shifts, k))


def _hidden_kernel(tb, tile_ref, shift_ref, x_ref, gate_ref, tab_ref, *rest):
    gh_ref, bsh, stage, parts, xt = rest[-5:]
    bsh[...] = _splat_picks(shift_ref[...].astype(F32), True).astype(jnp.int32)
    for s in range(SUBLANES):
        xt[pl.ds(s, tb, stride=SUBLANES), :] = x_ref[:, s * LANES:(s + 1) * LANES]

    half = PICKS // 2
    quad = SUBLANES // 2
    low = lax.broadcasted_iota(jnp.int32, (SUBLANES, LANES), 0) < quad

    def token(t, carry):
        x = xt[pl.ds(pl.multiple_of(t * SUBLANES, SUBLANES), SUBLANES), :]
        shifts = bsh.at[t]
        for j in range(half):
            a = x * _expert_row(tab_ref, tile_ref, t, j, shifts)
            b = x * _expert_row(tab_ref, tile_ref, t, j + half, shifts)
            stage[j * SUBLANES:(j + 1) * SUBLANES, :] = jnp.where(low, a + pltpu.roll(a, quad, 0),
                                                                  b + pltpu.roll(b, quad, 0))
        for p in range(2):
            part = stage[pl.ds(p * quad, half, stride=SUBLANES), :]
            for s in range(1, quad):
                part = part + stage[pl.ds(p * quad + s, half, stride=SUBLANES), :]
            parts[t, p * half:(p + 1) * half, :] = part
        return carry

    lax.fori_loop(0, tb, token, 0)
    gh_ref[...] = jnp.sum(parts[...], axis=-1)
    hid = gh_ref[...]
    gelu = 0.5 * hid * (1.0 + lax.erf(hid * (2.0 ** -0.5)))
    gh_ref[...] = gate_ref[...] * gelu


def _pick_block(tb, space=None):
    return pl.BlockSpec((tb, PICKS), lambda i: (i, 0), memory_space=space)


def _table_spec(tab):
    return pl.BlockSpec(tab.shape, lambda i: (0, 0), pipeline_mode=pl.Buffered(1))


def _hidden(tile, shift, xg, gate, tab, tb, after=None):
    t = xg.shape[0]
    extra = [] if after is None else [after]
    return pl.pallas_call(
        functools.partial(_hidden_kernel, tb),
        out_shape=jax.ShapeDtypeStruct((t, PICKS), F32),
        grid=(t // tb,),
        in_specs=[_pick_block(tb, pltpu.SMEM), _pick_block(tb),
                  pl.BlockSpec((tb, D_MODEL), lambda i: (i, 0)),
                  _pick_block(tb), _table_spec(tab)] + [_after_spec(a) for a in extra],
        out_specs=_pick_block(tb),
        scratch_shapes=[pltpu.VMEM((tb, PICKS, LANES), jnp.int32), pltpu.VMEM((PICKS // 2 * SUBLANES, LANES), F32),
                        pltpu.VMEM((tb, PICKS, LANES), F32), pltpu.VMEM((tb * SUBLANES, LANES), F32)],
        compiler_params=pltpu.CompilerParams(dimension_semantics=("parallel",), vmem_limit_bytes=VMEM_LIMIT),
        name="peer_hidden",
    )(tile, shift, xg, gate, tab, *extra)


def _combine_kernel(tb, tile_ref, shift_ref, gh_ref, tab_ref, x_ref, g_ref, *rest):
    o_ref, bsh, bg, ot = rest[-4:]
    bsh[...] = _splat_picks(shift_ref[...].astype(F32), True).astype(jnp.int32)
    bg[...] = _splat_picks(gh_ref[...], False)

    def token(t, carry):
        shifts, gates = bsh.at[t], bg.at[t]
        acc = [jnp.zeros((SUBLANES, LANES), F32) for _ in range(4)]
        for k in range(PICKS):
            acc[k % 4] = acc[k % 4] + _full(gates, k) * _expert_row(tab_ref, tile_ref, t, k, shifts)
        ot[pl.ds(pl.multiple_of(t * SUBLANES, SUBLANES), SUBLANES), :] = (acc[0] + acc[1]) + (acc[2] + acc[3])
        return carry

    lax.fori_loop(0, tb, token, 0)
    for s in range(SUBLANES):
        o_ref[:, s * LANES:(s + 1) * LANES] = ot[pl.ds(s, tb, stride=SUBLANES), :]
    o_ref[...] = _rms(x_ref[...] + o_ref[...], g_ref[...])


def _combine(tile, shift, gh, tab, x, g, tb, x_row0, out_row0, out_rows, into=None):
    t = gh.shape[0]
    xb0, ob0 = x_row0 // tb, out_row0 // tb
    extra = [] if into is None else [into]
    return pl.pallas_call(
        functools.partial(_combine_kernel, tb),
        out_shape=jax.ShapeDtypeStruct((out_rows, D_MODEL), F32),
        grid=(t // tb,),
        in_specs=[_pick_block(tb, pltpu.SMEM), _pick_block(tb), _pick_block(tb), _table_spec(tab),
                  pl.BlockSpec((tb, D_MODEL), lambda i: (i + xb0, 0)), _const_spec(g.shape)]
        + [pl.BlockSpec(memory_space=pl.ANY) for _ in extra],
        input_output_aliases={6: 0} if extra else {},
        out_specs=pl.BlockSpec((tb, D_MODEL), lambda i: (i + ob0, 0)),
        scratch_shapes=[pltpu.VMEM((tb, PICKS, LANES), jnp.int32), pltpu.VMEM((tb, PICKS, LANES), F32),
                        pltpu.VMEM((tb * SUBLANES, LANES), F32)],
        compiler_params=pltpu.CompilerParams(dimension_semantics=("parallel",), vmem_limit_bytes=VMEM_LIMIT),
        name="peer_combine",
    )(tile, shift, gh, tab, x, g, *extra)


SC_LANES = 16
SC_WORKERS = 32
SC_CHUNK = 32
SC_FEATURE_BLOCK = 256
SC_EARLY_SEQS = 3
SC_PROMPT_TOKENS = 13312
SC_COMBINE_HANDBACK = 512


def _sc_call(body, n, side_width, out_width):
    mesh = plsc.VectorSubcoreMesh(core_axis_name="c", subcore_axis_name="s")
    return pl.kernel(body, mesh=mesh, out_type=jax.ShapeDtypeStruct((n, out_width), F32),
                     compiler_params=pltpu.CompilerParams(needs_layout_passes=False),
                     scratch_types=[pltpu.VMEM((2 * PICKS,), jnp.int32), pltpu.VMEM((2 * side_width,), F32),
                                    pltpu.VMEM((2 * out_width,), F32),
                                    pltpu.VMEM((SC_CHUNK, D_MODEL), F32), pltpu.VMEM((SC_CHUNK, D_MODEL), F32)]
                     + [pltpu.SemaphoreType.DMA] * 5)


def _sc_token_loop(n, eid_hbm, side_hbm, tab_hbm, out_hbm, idx_v, side_v, out_v, bufs, sems, compute):
    sem_rows, sem_fetch, sem_put = sems[0:2], sems[2], sems[3:5]
    per_worker = n // SC_WORKERS
    first = (lax.axis_index("s") * 2 + lax.axis_index("c")) * per_worker
    last = first + per_worker - 1
    chunks = PICKS // SC_CHUNK
    side_w = side_v.shape[0] // 2
    out_w = out_v.shape[0] // 2

    def fetch(t, slot):
        return (pltpu.make_async_copy(eid_hbm.at[t], idx_v.at[pl.ds(slot * PICKS, PICKS)], sem_fetch),
                pltpu.make_async_copy(side_hbm.at[t], side_v.at[pl.ds(slot * side_w, side_w)], sem_fetch))

    def gather(slot, c):
        ids = idx_v.at[pl.ds(slot * PICKS + c * SC_CHUNK, SC_CHUNK)]
        return pltpu.make_async_copy(tab_hbm.at[ids], bufs[c % 2], sem_rows[c % 2])

    def put(t, slot):
        return pltpu.make_async_copy(out_v.at[pl.ds(slot * out_w, out_w)], out_hbm.at[t], sem_put[slot])

    for cp in fetch(first, 0):
        cp.start()
    for cp in fetch(first, 0):
        cp.wait()
    gather(0, 0).start()
    gather(0, 1).start()

    @pl.loop(0, per_worker, step=2)
    def _(i):
        for slot in range(2):
            t = first + i + slot
            nxt = jnp.minimum(t + 1, last)
            for cp in fetch(nxt, 1 - slot):
                cp.start()

            @pl.when(i >= 2)
            def _():
                put(t, slot).wait()

            for c in range(chunks):
                gather(slot, c).wait()
                compute(c, bufs[c % 2], slot * side_w, slot * out_w)
                if c + 2 < chunks:
                    gather(slot, c + 2).start()
                else:
                    if c + 2 == chunks:
                        for cp in fetch(nxt, 1 - slot):
                            cp.wait()
                    gather(1 - slot, c + 2 - chunks).start()
            put(t, slot).start()

    gather(0, 0).wait()
    gather(0, 1).wait()
    put(last, 0).wait()
    put(last, 1).wait()


def _sc_hidden(eid, x, tab):
    n = eid.shape[0]
    lanes = SC_LANES

    def body(eid_hbm, x_hbm, tab_hbm, out_hbm, idx_v, x_v, hid_v, rows0, rows1, *sems):
        lane = lax.iota(jnp.int32, lanes)

        def compute(c, rows, x0, h0):
            for g in range(SC_CHUNK // lanes):
                def step(j, accs):
                    off = pl.multiple_of(j * lanes, lanes)
                    xj = x_v[pl.ds(x0 + off, lanes)]
                    return tuple(accs[r] + xj * rows[g * lanes + r, pl.ds(off, lanes)] for r in range(lanes))
                accs = lax.fori_loop(0, D_MODEL // lanes, step,
                                     tuple(jnp.zeros((lanes,), F32) for _ in range(lanes)))
                hv = jnp.zeros((lanes,), F32)
                for r in range(lanes):
                    hv = jnp.where(lane == r, jnp.sum(accs[r]), hv)
                hid_v[pl.ds(h0 + c * SC_CHUNK + g * lanes, lanes)] = hv

        _sc_token_loop(n, eid_hbm, x_hbm, tab_hbm, out_hbm, idx_v, x_v, hid_v, (rows0, rows1), sems, compute)

    return _sc_call(body, n, D_MODEL, PICKS)(eid, x, tab)


def _sc_combine(eid, gh, tab, n=None):
    n = eid.shape[0] if n is None else n
    lanes = SC_LANES
    nacc = SC_FEATURE_BLOCK // lanes

    def body(eid_hbm, g_hbm, tab_hbm, out_hbm, idx_v, g_v, out_v, rows0, rows1, *sems):
        def compute(c, rows, g0, o0):
            for fb in range(D_MODEL // SC_FEATURE_BLOCK):
                f0 = fb * SC_FEATURE_BLOCK
                if c == 0:
                    init = tuple(jnp.zeros((lanes,), F32) for _ in range(nacc))
                else:
                    init = tuple(out_v[pl.ds(o0 + f0 + q * lanes, lanes)] for q in range(nacc))

                def step(r, accs):
                    gk = plsc.load_gather(g_v, [jnp.full((lanes,), g0 + c * SC_CHUNK + r, jnp.int32)])
                    return tuple(accs[q] + gk * rows[r, pl.ds(f0 + q * lanes, lanes)] for q in range(nacc))
                accs = lax.fori_loop(0, SC_CHUNK, step, init)
                for q in range(nacc):
                    out_v[pl.ds(o0 + f0 + q * lanes, lanes)] = accs[q]

        _sc_token_loop(n, eid_hbm, g_hbm, tab_hbm, out_hbm, idx_v, g_v, out_v, (rows0, rows1), sems, compute)

    return _sc_call(body, n, PICKS, D_MODEL)(eid, gh, tab)


def _after_spec(after):
    return pl.BlockSpec((SUBLANES, after.shape[1]), lambda i: (0, 0))


def _gate_gelu_kernel(h_ref, g_ref, after_ref, o_ref):
    hid = h_ref[...]
    o_ref[...] = g_ref[...] * (0.5 * hid * (1.0 + lax.erf(hid * (2.0 ** -0.5))))


def _gate_gelu(hid, gate, tb, after):
    t = hid.shape[0]
    return pl.pallas_call(
        _gate_gelu_kernel, out_shape=jax.ShapeDtypeStruct((t, PICKS), F32), grid=(t // tb,),
        in_specs=[_pick_block(tb), _pick_block(tb), _after_spec(after)], out_specs=_pick_block(tb),
        compiler_params=pltpu.CompilerParams(dimension_semantics=("parallel",)),
        name="gate_gelu",
    )(hid, gate, after)


def _final_kernel(x_ref, p_ref, g_ref, *rest):
    rest[-1][...] = _rms(x_ref[...] + p_ref[...], g_ref[...])


def _final(x, p, g, tb, out_row0, out_rows, into=None):
    b0 = out_row0 // tb
    local = pl.BlockSpec((tb, D_MODEL), lambda i: (i, 0))
    in_specs = [local, local, _const_spec(g.shape)]
    args = [x, p, g]
    aliases = {}
    if into is not None:
        aliases = {len(args): 0}
        in_specs.append(pl.BlockSpec(memory_space=pl.ANY))
        args.append(into)
    return pl.pallas_call(
        _final_kernel, out_shape=jax.ShapeDtypeStruct((out_rows, D_MODEL), F32), grid=(p.shape[0] // tb,),
        in_specs=in_specs, out_specs=pl.BlockSpec((tb, D_MODEL), lambda i: (i + b0, 0)),
        input_output_aliases=aliases,
        compiler_params=pltpu.CompilerParams(dimension_semantics=("parallel",), vmem_limit_bytes=VMEM_LIMIT),
        name="final_norm",
    )(*args)


PACK_BLOCK = 256


def _pack_kernel(lo_ref, hi_ref, o_ref):
    def bf16_bits(x):
        return lax.bitcast_convert_type(x.astype(BF16).astype(F32), jnp.int32)
    word = bf16_bits(hi_ref[...]) | lax.shift_right_logical(bf16_bits(lo_ref[...]), 16)
    for s in range(SUBLANES):
        o_ref[pl.ds(s, PACK_BLOCK, stride=SUBLANES), :] = word[:, s * LANES:(s + 1) * LANES]


def _pack_table(tab):
    steps = HALF_EXPERTS // PACK_BLOCK
    return pl.pallas_call(
        _pack_kernel, out_shape=jax.ShapeDtypeStruct((HALF_EXPERTS * SUBLANES, LANES), jnp.int32), grid=(steps,),
        in_specs=[pl.BlockSpec((PACK_BLOCK, D_MODEL), lambda i: (i, 0)),
                  pl.BlockSpec((PACK_BLOCK, D_MODEL), lambda i: (i + steps, 0))],
        out_specs=pl.BlockSpec((PACK_BLOCK * SUBLANES, LANES), lambda i: (i, 0)),
        compiler_params=pltpu.CompilerParams(dimension_semantics=("parallel",)),
        name="pack_table",
    )(tab, tab)


def _col(v, n=LANES):
    return jnp.pad(v.astype(F32), (0, n - v.shape[0])).reshape(1, n)


def _row(v, n=2 * SUBLANES):
    return jnp.pad(v.astype(F32), (0, n - v.shape[0])).reshape(n, 1)


def _peer(x, w, tb_sel, tb_exp, n_sc, out_row0=0, out_rows=None, after=None, into=None, between=None):
    t = x.shape[0]
    out_rows = t if out_rows is None else out_rows
    n_tc = t - n_sc
    n_sc2 = max(n_sc - SC_COMBINE_HANDBACK, 0)
    if n_sc:
        xn_s, eid_s, tile_s, shift_s, gate_s = _select(x, w["ln_ffn"], w["w_pq"], w["sub_keys"], SELECT_BLOCK, 0,
                                                       n_sc, after=after)
        hid_s = _sc_hidden(eid_s, xn_s, w["u_tab"])
    xn, _, tile, shift, gate = _select(x, w["ln_ffn"], w["w_pq"], w["sub_keys"], SELECT_BLOCK,n_sc, n_tc,
                                       after=xn_s if n_sc else None)
    gh = _hidden(tile, shift, xn, gate, w["u_pack"], tb_exp, after=into)
    if n_sc:
        other = None if between is None else between(gh)
        gh_s = _gate_gelu(hid_s, gate_s, tb_sel, after=gh if other is None else other)
        out_s = _sc_combine(eid_s, gh_s, w["v_tab"], n_sc2)
    y = _combine(tile, shift, gh, w["v_pack"], x, w["final_norm"], tb_exp, n_sc, out_row0 + n_sc, out_rows, into)
    if n_sc > n_sc2:
        y = _combine(tile_s[n_sc2:], shift_s[n_sc2:], gh_s[n_sc2:], w["v_pack"], x, w["final_norm"], tb_exp,
                     n_sc2, out_row0 + n_sc2, out_rows, y)
    if n_sc2:
        y = _final(x, out_s, w["final_norm"], tb_sel, out_row0, out_rows, into=y)
    return (y, other) if between is not None else y


def kernel(x_prompt, x_sample, mem_prompt, cache_conv_a, cache_conv_ssd, state_ssd, cache_mem_k, cache_mem_v, ln_mix_w, w_in, conv_a_w, conv_s_w, conv_s_b, dt_bias, a_log, d_skip, ssd_norm_w, w_out, ln_mem_w, mem_norm_w, w_mq, w_mk, w_mv, w_mo, ln_ffn_w, w_pq, sub_keys, u_tab, v_tab, final_norm_w):
    depth = w_in.shape[0]
    assert depth == 1
    l = 0
    n_main = 3 * D_CONV + D_SSD + XBC_DIM
    w_dt = w_in[l][:, n_main:]
    w = {
        "ln_mix": ln_mix_w[l].reshape(1, D_MODEL),
        "w_in": w_in[l][:, :n_main].astype(BF16),
        "w_dt": jnp.pad(w_dt, ((0, 0), (0, LANES - SSD_HEADS))).astype(BF16),
        "w_dtt": jnp.pad(w_dt.T, ((0, 2 * SUBLANES - SSD_HEADS), (0, 0))).astype(BF16),
        "conv_a_w": conv_a_w[l], "conv_s_w": conv_s_w[l], "conv_s_b": conv_s_b[l].reshape(1, XBC_DIM),
        "dtb_c": _col(dt_bias[l]), "alog_c": _col(a_log[l]), "dsk_c": _col(d_skip[l]),
        "dtb_r": _row(dt_bias[l]), "alog_r": _row(a_log[l]),
        "ssd_norm": ssd_norm_w[l].reshape(1, D_SSD),
        "w_out": w_out[l].astype(BF16),
        "ln_ffn": ln_ffn_w[l].reshape(1, D_MODEL),
        "w_pq": w_pq[l].astype(BF16).reshape(D_MODEL, PEER_HEADS, PEER_DK).transpose(1, 0, 2),
        "sub_keys": sub_keys[l].astype(BF16),
        "u_pack": _pack_table(u_tab[l]),
        "v_pack": _pack_table(v_tab[l]),
        "u_tab": u_tab[l], "v_tab": v_tab[l],
        "final_norm": final_norm_w.reshape(1, D_MODEL),
    }
    ln_mem = ln_mem_w[l].reshape(1, D_MODEL)
    wq, wo = w_mq[l].astype(BF16), w_mo[l].astype(BF16)

    bp, sp, _ = x_prompt.shape
    bs, ss, _ = x_sample.shape

    mk, mv = _memory_kv(mem_prompt.reshape(bp * N_MEM, D_MODEL), mem_norm_w[l].reshape(1, D_MODEL),
                        w_mk[l].astype(BF16), w_mv[l].astype(BF16))
    mk3, mv3 = mk.reshape(bp, N_MEM, D_MODEL), mv.reshape(bp, N_MEM, D_MODEL)
    n1 = SC_EARLY_SEQS
    h1, pa1, ps1, ph1 = _mixer(x_prompt, None, w, 512, 0, n1)
    h1 = _attn(h1, mk3, mv3, ln_mem, wq, wo, 512).reshape(n1 * sp, D_MODEL)

    xn1, eid1, _, _, gate1 = _select(h1, w["ln_ffn"], w["w_pq"], w["sub_keys"], SELECT_BLOCK,0, n1 * sp)
    hid1 = _sc_hidden(eid1, xn1, w["u_tab"])
    h2, pa2, ps2, ph2 = _mixer(x_prompt, None, w, 512, n1, bp - n1, after=xn1)
    h2 = _attn(h2, mk3, mv3, ln_mem, wq, wo, 512, kv0=n1).reshape((bp - n1) * sp, D_MODEL)

    hs, sa, ssd, sh = _mixer(x_sample, (cache_conv_a[l], cache_conv_ssd[l],
                                        state_ssd[l].reshape((bs,) + PAIR_STATE)), w, 256, after=h2)
    hs = _attn(hs, cache_mem_k[l], cache_mem_v[l], ln_mem, wq, wo, 256).reshape(bs * ss, D_MODEL)
    xn_q, _, tile_q, shift_q, gate_q = _select(hs, w["ln_ffn"], w["w_pq"], w["sub_keys"], SELECT_BLOCK,0, bs * ss)

    gh1 = _gate_gelu(hid1, gate1, 256, after=xn_q)
    out1 = _sc_combine(eid1, gh1, w["v_tab"])
    y1 = _final(h1, out1, w["final_norm"], 256, 0, bp * sp)
    def sample_experts(gh_main):
        gh_q = _hidden(tile_q, shift_q, xn_q, gate_q, w["u_pack"], 64, after=gh_main)
        return _combine(tile_q, shift_q, gh_q, w["v_pack"], hs, w["final_norm"], 64, 0, 0, bs * ss)

    y_prompt, y_sample = _peer(h2, w, 256, 64, SC_PROMPT_TOKENS, n1 * sp, bp * sp, after=gh1, into=y1,
                               between=sample_experts)
    y_prompt, y_sample = y_prompt.reshape(bp, sp, D_MODEL), y_sample.reshape(bs, ss, D_MODEL)
    pa, ps, ph = (jnp.concatenate(p, axis=0) for p in ((pa1, pa2), (ps1, ps2), (ph1, ph2)))

    kv_shape = (1, bp, N_MEM, MEM_HEADS, MEM_HEAD_DIM)
    st_shape = (SSD_HEADS, SSD_HEAD_DIM, SSD_STATE)
    return (y_prompt, y_sample, pa[None], ps[None], ph.reshape((1, bp) + st_shape), mk.reshape(kv_shape),
            mv.reshape(kv_shape), sa[None], ssd[None], sh.reshape((1, bs) + st_shape))
```

```python
import functools

import jax
import jax.numpy as jnp
from jax import lax
from jax.experimental import pallas as pl
from jax.experimental.pallas import tpu as pltpu
from jax.experimental.pallas import tpu_sc as plsc

D_MODEL = 1024
CHUNK = 64
D_CONV = 512
CONV_A_W = 3
D_SSD = 512
SSD_HEAD_DIM = 64
SSD_HEADS = 8
SSD_GROUPS = 2
SSD_HPG = 4
SSD_STATE = 128
SSD_CONV_W = 4
XBC_DIM = 1024
N_MEM = 256
MEM_HEADS = 4
MEM_HEAD_DIM = 256
PEER_HEADS = 8
N_KEYS = 128
N_EXPERTS = N_KEYS * N_KEYS
PEER_TOPK = 16
PEER_DK = 256
PEER_DK_HALF = 128
PICKS = PEER_HEADS * PEER_TOPK
EPS = 1e-6

LANES = 128
SUBLANES = 8
HALF_EXPERTS = N_EXPERTS // 2
PAIR_STATE = (SSD_HEADS // 2, 2 * SSD_HEAD_DIM, SSD_STATE)
VMEM_LIMIT = 56 * 1024 * 1024

F32 = jnp.float32
BF16 = jnp.bfloat16
HI = lax.Precision.HIGHEST


def _rms(x, g):
    return x * lax.rsqrt(jnp.mean(x * x, axis=-1, keepdims=True) + EPS) * g


def _silu(x):
    return x * (1.0 / (1.0 + jnp.exp(-x)))


def _softplus(x):
    return jnp.maximum(x, 0.0) + jnp.log1p(jnp.exp(-jnp.abs(x)))


def _dot(a, b):
    return jnp.dot(a.astype(BF16), b.astype(BF16), preferred_element_type=F32)


def _dot_nt(a, b):
    return lax.dot_general(a.astype(BF16), b.astype(BF16), (((1,), (1,)), ((), ())),
                           preferred_element_type=F32)


def _dot_tn(a, b):
    return lax.dot_general(a.astype(BF16), b.astype(BF16), (((0,), (0,)), ((), ())),
                           preferred_element_type=F32)


def _const_spec(shape):
    n = len(shape)
    return pl.BlockSpec(shape, lambda *_: (0,) * n)


def _mixer_kernel(has_hist, has_after, ts, lc, *refs):
    if has_after:
        refs = refs[1:]
    if has_hist:
        x_ref, hista_ref, hists_ref, h0_ref = refs[:4]
        refs = refs[4:]
    else:
        x_ref = refs[0]
        refs = refs[1:]
    (lnw_ref, win_ref, wdt_ref, wdtt_ref, caw_ref, csw_ref, csb_ref, dtb_c_ref, alog_c_ref, dsk_c_ref,
     dtb_r_ref, alog_r_ref, nw_ref, wout_ref,
     h1_ref, na_ref, ns_ref, nh_ref, cata, cats, hst) = refs
    s = pl.program_id(1)
    pad = SUBLANES

    @pl.when(s == 0)
    def _():
        if has_hist:
            cata[pad - 2:pad, :] = hista_ref[...]
            cats[pad - 3:pad, :] = hists_ref[...]
            hst[...] = h0_ref[...]
        else:
            cata[0:pad, :] = jnp.zeros((pad, D_CONV), F32)
            cats[0:pad, :] = jnp.zeros((pad, XBC_DIM), F32)
            hst[...] = jnp.zeros(hst.shape, F32)

    x = x_ref[...]
    xn = _rms(x, lnw_ref[...]).astype(BF16)
    proj = jnp.dot(xn, win_ref[...], preferred_element_type=F32)
    dt_c = jnp.dot(xn, wdt_ref[...], preferred_element_type=F32)
    dt_r = lax.dot_general(wdtt_ref[...], xn, (((1,), (1,)), ((), ())),
                           preferred_element_type=F32)
    g_b = proj[:, 0:D_CONV]
    g_c = proj[:, D_CONV:2 * D_CONV]
    v_in = proj[:, 2 * D_CONV:3 * D_CONV]
    z = proj[:, 3 * D_CONV:3 * D_CONV + D_SSD]
    xbc = proj[:, 3 * D_CONV + D_SSD:3 * D_CONV + D_SSD + XBC_DIM]

    ua = g_c * v_in
    cata[pad:pad + ts, :] = ua
    caw = caw_ref[...]
    conv_a = (caw[0:1] * cata[pad - 2:pad - 2 + ts, :] + caw[1:2] * cata[pad - 1:pad - 1 + ts, :]
              + caw[2:3] * ua)
    y_a = g_b * conv_a
    hist_a = cata[pad + ts - 2:pad + ts, :]
    na_ref[...] = hist_a
    cata[pad - 2:pad, :] = hist_a

    cats[pad:pad + ts, :] = xbc
    csw = csw_ref[...]
    conv_s = (csw[0:1] * cats[pad - 3:pad - 3 + ts, :] + csw[1:2] * cats[pad - 2:pad - 2 + ts, :]
              + csw[2:3] * cats[pad - 1:pad - 1 + ts, :] + csw[3:4] * xbc)
    hist_s = cats[pad + ts - 3:pad + ts, :]
    ns_ref[...] = hist_s
    cats[pad - 3:pad, :] = hist_s
    xbc_c = _silu(conv_s + csb_ref[...])
    xs = xbc_c[:, 0:D_SSD]

    dtp_c = _softplus(dt_c + dtb_c_ref[...])
    dtp_r = _softplus(dt_r + dtb_r_ref[...])
    a_c = dtp_c * (-jnp.exp(alog_c_ref[...]))
    a_r = dtp_r * (-jnp.exp(alog_r_ref[...]))
    dsk_c = dsk_c_ref[...]

    lane = lax.broadcasted_iota(jnp.int32, (1, LANES), 1)
    first_head = lane < SSD_HEAD_DIM
    row2 = lax.broadcasted_iota(jnp.int32, (2 * SSD_HEAD_DIM, 1), 0) < SSD_HEAD_DIM
    li = lax.broadcasted_iota(jnp.int32, (lc, lc), 0)
    si = lax.broadcasted_iota(jnp.int32, (lc, lc), 1)
    causal = si <= li
    tril = causal.astype(F32)
    jrow = lax.broadcasted_iota(jnp.int32, (ts, lc), 0)
    scol = lax.broadcasted_iota(jnp.int32, (ts, lc), 1)

    def pair(col, h):
        return jnp.where(first_head, col[:, h:h + 1], col[:, h + 1:h + 2])

    y_chunks = []
    for c in range(ts // lc):
        r0 = c * lc
        rows = slice(r0, r0 + lc)
        acum_c = jnp.dot(tril, a_c[rows], preferred_element_type=F32, precision=HI)
        upper = ((jrow >= r0) & (jrow <= r0 + scol)).astype(F32)
        acum_r = jnp.dot(a_r, upper, preferred_element_type=F32, precision=HI)
        last_c = acum_c[lc - 1:lc, :]
        dec_c = jnp.exp(last_c - acum_c)
        eac_c = jnp.exp(acum_c)
        y_pairs = []
        for g in range(SSD_GROUPS):
            b_g = xbc_c[rows, D_SSD + g * SSD_STATE:D_SSD + (g + 1) * SSD_STATE]
            c_g = xbc_c[rows, D_SSD + (SSD_GROUPS + g) * SSD_STATE:D_SSD + (SSD_GROUPS + g + 1) * SSD_STATE]
            cb = _dot_nt(c_g, b_g)
            for q in range(SSD_HPG // 2):
                h = g * SSD_HPG + 2 * q
                pi = h // 2
                xs_p = xs[rows, pi * LANES:(pi + 1) * LANES]
                xdt = xs_p * pair(dtp_c[rows], h)
                res = []
                for hh in (h, h + 1):
                    seg = acum_c[:, hh:hh + 1] - acum_r[hh:hh + 1, :]
                    m_h = cb * jnp.exp(jnp.where(causal, seg, -jnp.inf))
                    res.append(_dot(m_h, xdt))
                y_diag = jnp.where(first_head, res[0], res[1])
                h_in = hst[pi]
                y_off = _dot_nt(c_g, h_in) * pair(eac_c, h)
                st = _dot_tn(xdt * pair(dec_c, h), b_g)
                cd = jnp.exp(jnp.where(row2, last_c[:, h:h + 1], last_c[:, h + 1:h + 2]))
                hst[pi] = cd * h_in + st
                y_pairs.append(y_diag + y_off + pair(dsk_c, h) * xs_p)
        y_chunks.append(jnp.concatenate(y_pairs, axis=1))
    y = y_chunks[0] if len(y_chunks) == 1 else jnp.concatenate(y_chunks, axis=0)
    y_b = _rms(y * _silu(z), nw_ref[...])
    wout = wout_ref[...]
    out = (jnp.dot(y_a.astype(BF16), wout[0:D_CONV], preferred_element_type=F32)
           + jnp.dot(y_b.astype(BF16), wout[D_CONV:], preferred_element_type=F32))
    h1_ref[...] = x + out
    nh_ref[...] = hst[...]


def _mixer(x, hist, w, ts, b0=0, nb=None, after=None):
    _, s, _ = x.shape
    b = x.shape[0] if nb is None else nb
    lc = min(CHUNK, s)
    ts = min(ts, s)
    has_hist = hist is not None
    assert not (has_hist and b0)
    per_b3 = lambda shape: pl.BlockSpec((None,) + shape, lambda i, j: (i,) + (0,) * len(shape))
    in_specs = [pl.BlockSpec((None, ts, D_MODEL), lambda i, j: (i + b0, j, 0))]
    args = [x]
    if after is not None:
        in_specs.insert(0, pl.BlockSpec((SUBLANES, after.shape[1]), lambda i, j: (0, 0)))
        args.insert(0, after)
    if has_hist:
        in_specs += [per_b3((CONV_A_W - 1, D_CONV)), per_b3((SSD_CONV_W - 1, XBC_DIM)),
                     per_b3(PAIR_STATE)]
        args += list(hist)
    wargs = [w["ln_mix"], w["w_in"], w["w_dt"], w["w_dtt"], w["conv_a_w"], w["conv_s_w"], w["conv_s_b"],
             w["dtb_c"], w["alog_c"], w["dsk_c"], w["dtb_r"], w["alog_r"], w["ssd_norm"], w["w_out"]]
    in_specs += [_const_spec(a.shape) for a in wargs]
    args += wargs
    out_shape = (jax.ShapeDtypeStruct((b, s, D_MODEL), F32),
                 jax.ShapeDtypeStruct((b, CONV_A_W - 1, D_CONV), F32),
                 jax.ShapeDtypeStruct((b, SSD_CONV_W - 1, XBC_DIM), F32),
                 jax.ShapeDtypeStruct((b,) + PAIR_STATE, F32))
    out_specs = (pl.BlockSpec((None, ts, D_MODEL), lambda i, j: (i, j, 0)),
                 per_b3((CONV_A_W - 1, D_CONV)), per_b3((SSD_CONV_W - 1, XBC_DIM)),
                 per_b3(PAIR_STATE))
    return pl.pallas_call(
        functools.partial(_mixer_kernel, has_hist, after is not None, ts, lc),
        out_shape=out_shape, grid=(b, s // ts), in_specs=in_specs, out_specs=out_specs,
        scratch_shapes=[pltpu.VMEM((SUBLANES + ts, D_CONV), F32), pltpu.VMEM((SUBLANES + ts, XBC_DIM), F32),
                        pltpu.VMEM(PAIR_STATE, F32)],
        compiler_params=pltpu.CompilerParams(dimension_semantics=("parallel", "arbitrary"),
                                             vmem_limit_bytes=VMEM_LIMIT),
        name="mixer_hist" if has_hist else "mixer",
    )(*args)


def _memkv_kernel(m_ref, g_ref, wk_ref, wv_ref, k_ref, v_ref):
    mn = _rms(m_ref[...], g_ref[...]).astype(BF16)
    k_ref[...] = jnp.dot(mn, wk_ref[...], preferred_element_type=F32)
    v_ref[...] = jnp.dot(mn, wv_ref[...], preferred_element_type=F32)


def _memory_kv(mem, g, wk, wv, tm=256):
    t = mem.shape[0]
    blk = pl.BlockSpec((tm, D_MODEL), lambda i: (i, 0))
    return pl.pallas_call(
        _memkv_kernel,
        out_shape=(jax.ShapeDtypeStruct((t, D_MODEL), F32),) * 2,
        grid=(t // tm,), in_specs=[blk, _const_spec(g.shape), _const_spec(wk.shape), _const_spec(wv.shape)],
        out_specs=(blk, blk),
        compiler_params=pltpu.CompilerParams(dimension_semantics=("parallel",), vmem_limit_bytes=VMEM_LIMIT),
        name="memory_kv",
    )(mem, g, wk, wv)


def _attn_kernel(x_ref, k_ref, v_ref, g_ref, wq_ref, wo_ref, o_ref):
    x = x_ref[...]
    xn = _rms(x, g_ref[...]).astype(BF16)
    q = jnp.dot(xn, wq_ref[...], preferred_element_type=F32)
    by_head = len(k_ref.shape) == 3
    wo = wo_ref[...]
    acc = x
    for h in range(MEM_HEADS):
        cols = slice(h * MEM_HEAD_DIM, (h + 1) * MEM_HEAD_DIM)
        k_h = k_ref[:, h, :] if by_head else k_ref[:, cols]
        v_h = v_ref[:, h, :] if by_head else v_ref[:, cols]
        sc = _dot_nt(q[:, cols], k_h) * (MEM_HEAD_DIM ** -0.5)
        sc = sc - jnp.max(sc, axis=-1, keepdims=True)
        p = jnp.exp(sc)
        p = p / jnp.sum(p, axis=-1, keepdims=True)
        o_h = _dot(p, v_h)
        acc = acc + jnp.dot(o_h.astype(BF16), wo[cols, :], preferred_element_type=F32)
    o_ref[...] = acc


def _attn(x, k, v, g, wq, wo, ts, kv0=0):
    b, s, _ = x.shape
    ts = min(ts, s)
    xblk = pl.BlockSpec((None, ts, D_MODEL), lambda i, j: (i, j, 0))
    if k.ndim == 3:
        kvblk = pl.BlockSpec((None, N_MEM, D_MODEL), lambda i, j: (i + kv0, 0, 0))
    else:
        kvblk = pl.BlockSpec((None, N_MEM, MEM_HEADS, MEM_HEAD_DIM), lambda i, j: (i + kv0, 0, 0, 0))
    return pl.pallas_call(
        _attn_kernel,
        out_shape=jax.ShapeDtypeStruct((b, s, D_MODEL), F32),
        grid=(b, s // ts),
        in_specs=[xblk, kvblk, kvblk, _const_spec(g.shape), _const_spec(wq.shape), _const_spec(wo.shape)],
        out_specs=xblk,
        compiler_params=pltpu.CompilerParams(dimension_semantics=("parallel", "parallel"),
                                             vmem_limit_bytes=VMEM_LIMIT),
        name="attn",
    )(x, k, v, g, wq, wo)


def _tree_reduce(op, x):
    n = x.shape[0]
    while n > 1:
        half = n // 2
        head = op(x[:half], x[half:2 * half])
        x = head if n % 2 == 0 else jnp.concatenate([head, x[2 * half:]], axis=0)
        n = x.shape[0]
    return x


def _top_k_rows(sc, k, payload=None):
    n = sc.shape[0]
    rows = lax.broadcasted_iota(jnp.int32, sc.shape, 0).astype(F32)
    vals, outs = [], []
    for _ in range(k):
        m = _tree_reduce(jnp.maximum, sc)
        cand = jnp.where(sc == m, rows, float(n))
        first = _tree_reduce(jnp.minimum, cand)
        hit = cand == first
        sc = jnp.where(hit, -jnp.inf, sc)
        vals.append(m)
        if payload is None:
            outs.append(first.astype(jnp.int32))
        else:
            outs.append(jnp.sum(jnp.where(hit, payload, 0), axis=0, keepdims=True))
    return jnp.concatenate(vals, axis=0), jnp.concatenate(outs, axis=0)


def _pair_candidates(v1, i1, v2, i2):
    comb = [v1[a:a + 1] + v2[0:PEER_TOPK // (a + 1)] for a in range(PEER_TOPK)]
    cidx = [i1[a:a + 1] * N_KEYS + i2[0:PEER_TOPK // (a + 1)] for a in range(PEER_TOPK)]
    return jnp.concatenate(comb, axis=0), jnp.concatenate(cidx, axis=0)


SELECT_BLOCK = SUBLANES * LANES


def _select_kernel(tb, x_ref, g_ref, wq_ref, keys_ref, *rest):
    xn_ref, eid_ref, tile_ref, shift_ref, gate_ref, s1, s2, eid_s, gate_s = rest[-9:]
    groups = tb // LANES
    assert groups == SUBLANES
    xn = _rms(x_ref[...], g_ref[...])
    xn_ref[...] = xn
    xn = xn.astype(BF16)

    def head(h, carry):
        q = jnp.dot(xn, wq_ref[h], preferred_element_type=F32)
        for g in range(groups):
            qg = q[g * LANES:(g + 1) * LANES]
            s1[pl.ds(g, N_KEYS, stride=SUBLANES), :] = _dot_nt(keys_ref[h, 0], qg[:, 0:PEER_DK_HALF])
            s2[pl.ds(g, N_KEYS, stride=SUBLANES), :] = _dot_nt(keys_ref[h, 1], qg[:, PEER_DK_HALF:PEER_DK])
        v1, i1 = _top_k_rows(s1[...].reshape(N_KEYS, SUBLANES, LANES), PEER_TOPK)
        v2, i2 = _top_k_rows(s2[...].reshape(N_KEYS, SUBLANES, LANES), PEER_TOPK)
        comb, cidx = _pair_candidates(v1, i1, v2, i2)
        top, eid = _top_k_rows(comb, PEER_TOPK, cidx)
        ex = jnp.exp(top - top[0:1])
        gate = ex / jnp.sum(ex, axis=0, keepdims=True)
        rows = pl.ds(pl.multiple_of(h * PEER_TOPK * SUBLANES, PEER_TOPK * SUBLANES), PEER_TOPK * SUBLANES)
        gate_s[rows, :] = gate.reshape(PEER_TOPK * SUBLANES, LANES)
        eid_s[rows, :] = eid.reshape(PEER_TOPK * SUBLANES, LANES)
        return carry

    lax.fori_loop(0, PEER_HEADS, head, 0)
    for g in range(groups):
        toks = slice(g * LANES, (g + 1) * LANES)
        eid = eid_s[pl.ds(g, PICKS, stride=SUBLANES), :].T
        upper = eid >= HALF_EXPERTS
        eid_ref[toks, :] = eid
        tile_ref[toks, :] = jnp.where(upper, eid - HALF_EXPERTS, eid) * SUBLANES
        shift_ref[toks, :] = jnp.where(upper, 0, 16)
        gate_ref[toks, :] = gate_s[pl.ds(g, PICKS, stride=SUBLANES), :].T


def _select(x, g, wq, keys, tb, t0, t, after=None):
    b0 = t0 // tb
    extra = [] if after is None else [after]
    xin = pl.BlockSpec((tb, D_MODEL), lambda i: (i + b0, 0))
    xblk = pl.BlockSpec((tb, D_MODEL), lambda i: (i, 0))
    picks_i = jax.ShapeDtypeStruct((t, PICKS), jnp.int32)
    return pl.pallas_call(
        functools.partial(_select_kernel, tb),
        out_shape=(jax.ShapeDtypeStruct((t, D_MODEL), F32), picks_i, picks_i, picks_i,
                   jax.ShapeDtypeStruct((t, PICKS), F32)),
        grid=(t // tb,),
        in_specs=[xin, _const_spec(g.shape), _const_spec(wq.shape), _const_spec(keys.shape)]
        + [_after_spec(a) for a in extra],
        out_specs=(xblk,) + (_pick_block(tb),) * 4,
        scratch_shapes=[pltpu.VMEM((N_KEYS * SUBLANES, LANES), F32), pltpu.VMEM((N_KEYS * SUBLANES, LANES), F32),
                        pltpu.VMEM((PICKS * SUBLANES, LANES), jnp.int32), pltpu.VMEM((PICKS * SUBLANES, LANES), F32)],
        compiler_params=pltpu.CompilerParams(dimension_semantics=("parallel",), vmem_limit_bytes=VMEM_LIMIT),
        name="peer_select",
    )(x, g, wq, keys, *extra)


HI_HALF_MASK = -(1 << 16)


def _unpack(word, shift):
    return lax.bitcast_convert_type((word << shift) & HI_HALF_MASK, F32)


def _splat_picks(blk, one_pass):
    tb = blk.shape[0]
    r = lax.broadcasted_iota(jnp.int32, (PICKS, LANES), 0)
    c = lax.broadcasted_iota(jnp.int32, (PICKS, LANES), 1)
    eye = (r == c)[None]
    ones = jnp.ones((LANES, LANES), BF16)

    def one(piece):
        diag = jnp.where(eye, piece[:, None, :], 0.0).astype(BF16).reshape(tb * PICKS, LANES)
        return jnp.dot(diag, ones, preferred_element_type=F32)

    if one_pass:
        out = one(blk)
    else:
        hi = blk.astype(BF16).astype(F32)
        rest = blk - hi
        mid = rest.astype(BF16).astype(F32)
        out = one(hi) + one(mid) + one(rest - mid)
    return out.reshape(tb, PICKS, LANES)


def _full(ref, k):
    return jnp.broadcast_to(ref[k:k + 1, :], (SUBLANES, LANES))


def _expert_row(tab_ref, tile_ref, t, k, shifts):
    idx = pl.multiple_of(tile_ref[t, k], SUBLANES)
    return _unpack(tab_ref[pl.ds(idx, SUBLANES), :], _full(shifts, k))


def _hidden_kernel(tb, tile_ref, shift_ref, x_ref, gate_ref, tab_ref, *rest):
    gh_ref, bsh, stage, parts, xt = rest[-5:]
    bsh[...] = _splat_picks(shift_ref[...].astype(F32), True).astype(jnp.int32)
    for s in range(SUBLANES):
        xt[pl.ds(s, tb, stride=SUBLANES), :] = x_ref[:, s * LANES:(s + 1) * LANES]

    half = PICKS // 2
    quad = SUBLANES // 2
    low = lax.broadcasted_iota(jnp.int32, (SUBLANES, LANES), 0) < quad

    def token(t, carry):
        x = xt[pl.ds(pl.multiple_of(t * SUBLANES, SUBLANES), SUBLANES), :]
        shifts = bsh.at[t]
        for j in range(half):
            a = x * _expert_row(tab_ref, tile_ref, t, j, shifts)
            b = x * _expert_row(tab_ref, tile_ref, t, j + half, shifts)
            stage[j * SUBLANES:(j + 1) * SUBLANES, :] = jnp.where(low, a + pltpu.roll(a, quad, 0),
                                                                  b + pltpu.roll(b, quad, 0))
        for p in range(2):
            part = stage[pl.ds(p * quad, half, stride=SUBLANES), :]
            for s in range(1, quad):
                part = part + stage[pl.ds(p * quad + s, half, stride=SUBLANES), :]
            parts[t, p * half:(p + 1) * half, :] = part
        return carry

    lax.fori_loop(0, tb, token, 0)
    gh_ref[...] = jnp.sum(parts[...], axis=-1)
    hid = gh_ref[...]
    gelu = 0.5 * hid * (1.0 + lax.erf(hid * (2.0 ** -0.5)))
    gh_ref[...] = gate_ref[...] * gelu


def _pick_block(tb, space=None):
    return pl.BlockSpec((tb, PICKS), lambda i: (i, 0), memory_space=space)


def _table_spec(tab):
    return pl.BlockSpec(tab.shape, lambda i: (0, 0), pipeline_mode=pl.Buffered(1))


def _hidden(tile, shift, xg, gate, tab, tb, after=None):
    t = xg.shape[0]
    extra = [] if after is None else [after]
    return pl.pallas_call(
        functools.partial(_hidden_kernel, tb),
        out_shape=jax.ShapeDtypeStruct((t, PICKS), F32),
        grid=(t // tb,),
        in_specs=[_pick_block(tb, pltpu.SMEM), _pick_block(tb),
                  pl.BlockSpec((tb, D_MODEL), lambda i: (i, 0)),
                  _pick_block(tb), _table_spec(tab)] + [_after_spec(a) for a in extra],
        out_specs=_pick_block(tb),
        scratch_shapes=[pltpu.VMEM((tb, PICKS, LANES), jnp.int32), pltpu.VMEM((PICKS // 2 * SUBLANES, LANES), F32),
                        pltpu.VMEM((tb, PICKS, LANES), F32), pltpu.VMEM((tb * SUBLANES, LANES), F32)],
        compiler_params=pltpu.CompilerParams(dimension_semantics=("parallel",), vmem_limit_bytes=VMEM_LIMIT),
        name="peer_hidden",
    )(tile, shift, xg, gate, tab, *extra)


def _combine_kernel(tb, tile_ref, shift_ref, gh_ref, tab_ref, x_ref, g_ref, *rest):
    o_ref, bsh, bg, ot = rest[-4:]
    bsh[...] = _splat_picks(shift_ref[...].astype(F32), True).astype(jnp.int32)
    bg[...] = _splat_picks(gh_ref[...], False)

    def token(t, carry):
        shifts, gates = bsh.at[t], bg.at[t]
        acc = [jnp.zeros((SUBLANES, LANES), F32) for _ in range(4)]
        for k in range(PICKS):
            acc[k % 4] = acc[k % 4] + _full(gates, k) * _expert_row(tab_ref, tile_ref, t, k, shifts)
        ot[pl.ds(pl.multiple_of(t * SUBLANES, SUBLANES), SUBLANES), :] = (acc[0] + acc[1]) + (acc[2] + acc[3])
        return carry

    lax.fori_loop(0, tb, token, 0)
    for s in range(SUBLANES):
        o_ref[:, s * LANES:(s + 1) * LANES] = ot[pl.ds(s, tb, stride=SUBLANES), :]
    o_ref[...] = _rms(x_ref[...] + o_ref[...], g_ref[...])


def _combine(tile, shift, gh, tab, x, g, tb, x_row0, out_row0, out_rows, into=None):
    t = gh.shape[0]
    xb0, ob0 = x_row0 // tb, out_row0 // tb
    extra = [] if into is None else [into]
    return pl.pallas_call(
        functools.partial(_combine_kernel, tb),
        out_shape=jax.ShapeDtypeStruct((out_rows, D_MODEL), F32),
        grid=(t // tb,),
        in_specs=[_pick_block(tb, pltpu.SMEM), _pick_block(tb), _pick_block(tb), _table_spec(tab),
                  pl.BlockSpec((tb, D_MODEL), lambda i: (i + xb0, 0)), _const_spec(g.shape)]
        + [pl.BlockSpec(memory_space=pl.ANY) for _ in extra],
        input_output_aliases={6: 0} if extra else {},
        out_specs=pl.BlockSpec((tb, D_MODEL), lambda i: (i + ob0, 0)),
        scratch_shapes=[pltpu.VMEM((tb, PICKS, LANES), jnp.int32), pltpu.VMEM((tb, PICKS, LANES), F32),
                        pltpu.VMEM((tb * SUBLANES, LANES), F32)],
        compiler_params=pltpu.CompilerParams(dimension_semantics=("parallel",), vmem_limit_bytes=VMEM_LIMIT),
        name="peer_combine",
    )(tile, shift, gh, tab, x, g, *extra)


SC_LANES = 16
SC_WORKERS = 32
SC_CHUNK = 32
SC_FEATURE_BLOCK = 256
SC_EARLY_SEQS = 3
SC_PROMPT_TOKENS = 13312
SC_COMBINE_HANDBACK = 512


def _sc_call(body, n, side_width, out_width):
    mesh = plsc.VectorSubcoreMesh(core_axis_name="c", subcore_axis_name="s")
    return pl.kernel(body, mesh=mesh, out_type=jax.ShapeDtypeStruct((n, out_width), F32),
                     compiler_params=pltpu.CompilerParams(needs_layout_passes=False),
                     scratch_types=[pltpu.VMEM((2 * PICKS,), jnp.int32), pltpu.VMEM((2 * side_width,), F32),
                                    pltpu.VMEM((2 * out_width,), F32),
                                    pltpu.VMEM((SC_CHUNK, D_MODEL), F32), pltpu.VMEM((SC_CHUNK, D_MODEL), F32)]
                     + [pltpu.SemaphoreType.DMA] * 5)


def _sc_token_loop(n, eid_hbm, side_hbm, tab_hbm, out_hbm, idx_v, side_v, out_v, bufs, sems, compute):
    sem_rows, sem_fetch, sem_put = sems[0:2], sems[2], sems[3:5]
    per_worker = n // SC_WORKERS
    first = (lax.axis_index("s") * 2 + lax.axis_index("c")) * per_worker
    last = first + per_worker - 1
    chunks = PICKS // SC_CHUNK
    side_w = side_v.shape[0] // 2
    out_w = out_v.shape[0] // 2

    def fetch(t, slot):
        return (pltpu.make_async_copy(eid_hbm.at[t], idx_v.at[pl.ds(slot * PICKS, PICKS)], sem_fetch),
                pltpu.make_async_copy(side_hbm.at[t], side_v.at[pl.ds(slot * side_w, side_w)], sem_fetch))

    def gather(slot, c):
        ids = idx_v.at[pl.ds(slot * PICKS + c * SC_CHUNK, SC_CHUNK)]
        return pltpu.make_async_copy(tab_hbm.at[ids], bufs[c % 2], sem_rows[c % 2])

    def put(t, slot):
        return pltpu.make_async_copy(out_v.at[pl.ds(slot * out_w, out_w)], out_hbm.at[t], sem_put[slot])

    for cp in fetch(first, 0):
        cp.start()
    for cp in fetch(first, 0):
        cp.wait()
    gather(0, 0).start()
    gather(0, 1).start()

    @pl.loop(0, per_worker, step=2)
    def _(i):
        for slot in range(2):
            t = first + i + slot
            nxt = jnp.minimum(t + 1, last)
            for cp in fetch(nxt, 1 - slot):
                cp.start()

            @pl.when(i >= 2)
            def _():
                put(t, slot).wait()

            for c in range(chunks):
                gather(slot, c).wait()
                compute(c, bufs[c % 2], slot * side_w, slot * out_w)
                if c + 2 < chunks:
                    gather(slot, c + 2).start()
                else:
                    if c + 2 == chunks:
                        for cp in fetch(nxt, 1 - slot):
                            cp.wait()
                    gather(1 - slot, c + 2 - chunks).start()
            put(t, slot).start()

    gather(0, 0).wait()
    gather(0, 1).wait()
    put(last, 0).wait()
    put(last, 1).wait()


def _sc_hidden(eid, x, tab):
    n = eid.shape[0]
    lanes = SC_LANES

    def body(eid_hbm, x_hbm, tab_hbm, out_hbm, idx_v, x_v, hid_v, rows0, rows1, *sems):
        lane = lax.iota(jnp.int32, lanes)

        def compute(c, rows, x0, h0):
            for g in range(SC_CHUNK // lanes):
                def step(j, accs):
                    off = pl.multiple_of(j * lanes, lanes)
                    xj = x_v[pl.ds(x0 + off, lanes)]
                    return tuple(accs[r] + xj * rows[g * lanes + r, pl.ds(off, lanes)] for r in range(lanes))
                accs = lax.fori_loop(0, D_MODEL // lanes, step,
                                     tuple(jnp.zeros((lanes,), F32) for _ in range(lanes)))
                hv = jnp.zeros((lanes,), F32)
                for r in range(lanes):
                    hv = jnp.where(lane == r, jnp.sum(accs[r]), hv)
                hid_v[pl.ds(h0 + c * SC_CHUNK + g * lanes, lanes)] = hv

        _sc_token_loop(n, eid_hbm, x_hbm, tab_hbm, out_hbm, idx_v, x_v, hid_v, (rows0, rows1), sems, compute)

    return _sc_call(body, n, D_MODEL, PICKS)(eid, x, tab)


def _sc_combine(eid, gh, tab, n=None):
    n = eid.shape[0] if n is None else n
    lanes = SC_LANES
    nacc = SC_FEATURE_BLOCK // lanes

    def body(eid_hbm, g_hbm, tab_hbm, out_hbm, idx_v, g_v, out_v, rows0, rows1, *sems):
        def compute(c, rows, g0, o0):
            for fb in range(D_MODEL // SC_FEATURE_BLOCK):
                f0 = fb * SC_FEATURE_BLOCK
                if c == 0:
                    init = tuple(jnp.zeros((lanes,), F32) for _ in range(nacc))
                else:
                    init = tuple(out_v[pl.ds(o0 + f0 + q * lanes, lanes)] for q in range(nacc))

                def step(r, accs):
                    gk = plsc.load_gather(g_v, [jnp.full((lanes,), g0 + c * SC_CHUNK + r, jnp.int32)])
                    return tuple(accs[q] + gk * rows[r, pl.ds(f0 + q * lanes, lanes)] for q in range(nacc))
                accs = lax.fori_loop(0, SC_CHUNK, step, init)
                for q in range(nacc):
                    out_v[pl.ds(o0 + f0 + q * lanes, lanes)] = accs[q]

        _sc_token_loop(n, eid_hbm, g_hbm, tab_hbm, out_hbm, idx_v, g_v, out_v, (rows0, rows1), sems, compute)

    return _sc_call(body, n, PICKS, D_MODEL)(eid, gh, tab)


def _after_spec(after):
    return pl.BlockSpec((SUBLANES, after.shape[1]), lambda i: (0, 0))


def _gate_gelu_kernel(h_ref, g_ref, after_ref, o_ref):
    hid = h_ref[...]
    o_ref[...] = g_ref[...] * (0.5 * hid * (1.0 + lax.erf(hid * (2.0 ** -0.5))))


def _gate_gelu(hid, gate, tb, after):
    t = hid.shape[0]
    return pl.pallas_call(
        _gate_gelu_kernel, out_shape=jax.ShapeDtypeStruct((t, PICKS), F32), grid=(t // tb,),
        in_specs=[_pick_block(tb), _pick_block(tb), _after_spec(after)], out_specs=_pick_block(tb),
        compiler_params=pltpu.CompilerParams(dimension_semantics=("parallel",)),
        name="gate_gelu",
    )(hid, gate, after)


def _final_kernel(x_ref, p_ref, g_ref, *rest):
    rest[-1][...] = _rms(x_ref[...] + p_ref[...], g_ref[...])


def _final(x, p, g, tb, out_row0, out_rows, into=None):
    b0 = out_row0 // tb
    local = pl.BlockSpec((tb, D_MODEL), lambda i: (i, 0))
    in_specs = [local, local, _const_spec(g.shape)]
    args = [x, p, g]
    aliases = {}
    if into is not None:
        aliases = {len(args): 0}
        in_specs.append(pl.BlockSpec(memory_space=pl.ANY))
        args.append(into)
    return pl.pallas_call(
        _final_kernel, out_shape=jax.ShapeDtypeStruct((out_rows, D_MODEL), F32), grid=(p.shape[0] // tb,),
        in_specs=in_specs, out_specs=pl.BlockSpec((tb, D_MODEL), lambda i: (i + b0, 0)),
        input_output_aliases=aliases,
        compiler_params=pltpu.CompilerParams(dimension_semantics=("parallel",), vmem_limit_bytes=VMEM_LIMIT),
        name="final_norm",
    )(*args)


PACK_BLOCK = 256


def _pack_kernel(lo_ref, hi_ref, o_ref):
    def bf16_bits(x):
        return lax.bitcast_convert_type(x.astype(BF16).astype(F32), jnp.int32)
    word = bf16_bits(hi_ref[...]) | lax.shift_right_logical(bf16_bits(lo_ref[...]), 16)
    for s in range(SUBLANES):
        o_ref[pl.ds(s, PACK_BLOCK, stride=SUBLANES), :] = word[:, s * LANES:(s + 1) * LANES]


def _pack_table(tab):
    steps = HALF_EXPERTS // PACK_BLOCK
    return pl.pallas_call(
        _pack_kernel, out_shape=jax.ShapeDtypeStruct((HALF_EXPERTS * SUBLANES, LANES), jnp.int32), grid=(steps,),
        in_specs=[pl.BlockSpec((PACK_BLOCK, D_MODEL), lambda i: (i, 0)),
                  pl.BlockSpec((PACK_BLOCK, D_MODEL), lambda i: (i + steps, 0))],
        out_specs=pl.BlockSpec((PACK_BLOCK * SUBLANES, LANES), lambda i: (i, 0)),
        compiler_params=pltpu.CompilerParams(dimension_semantics=("parallel",)),
        name="pack_table",
    )(tab, tab)


def _col(v, n=LANES):
    return jnp.pad(v.astype(F32), (0, n - v.shape[0])).reshape(1, n)


def _row(v, n=2 * SUBLANES):
    return jnp.pad(v.astype(F32), (0, n - v.shape[0])).reshape(n, 1)


def _peer(x, w, tb_sel, tb_exp, n_sc, out_row0=0, out_rows=None, after=None, into=None, between=None):
    t = x.shape[0]
    out_rows = t if out_rows is None else out_rows
    n_tc = t - n_sc
    n_sc2 = max(n_sc - SC_COMBINE_HANDBACK, 0)
    if n_sc:
        xn_s, eid_s, tile_s, shift_s, gate_s = _select(x, w["ln_ffn"], w["w_pq"], w["sub_keys"], SELECT_BLOCK, 0,
                                                       n_sc, after=after)
        hid_s = _sc_hidden(eid_s, xn_s, w["u_tab"])
    xn, _, tile, shift, gate = _select(x, w["ln_ffn"], w["w_pq"], w["sub_keys"], SELECT_BLOCK,n_sc, n_tc,
                                       after=xn_s if n_sc else None)
    gh = _hidden(tile, shift, xn, gate, w["u_pack"], tb_exp, after=into)
    if n_sc:
        other = None if between is None else between(gh)
        gh_s = _gate_gelu(hid_s, gate_s, tb_sel, after=gh if other is None else other)
        out_s = _sc_combine(eid_s, gh_s, w["v_tab"], n_sc2)
    y = _combine(tile, shift, gh, w["v_pack"], x, w["final_norm"], tb_exp, n_sc, out_row0 + n_sc, out_rows, into)
    if n_sc > n_sc2:
        y = _combine(tile_s[n_sc2:], shift_s[n_sc2:], gh_s[n_sc2:], w["v_pack"], x, w["final_norm"], tb_exp,
                     n_sc2, out_row0 + n_sc2, out_rows, y)
    if n_sc2:
        y = _final(x, out_s, w["final_norm"], tb_sel, out_row0, out_rows, into=y)
    return (y, other) if between is not None else y


def kernel(x_prompt, x_sample, mem_prompt, cache_conv_a, cache_conv_ssd, state_ssd, cache_mem_k, cache_mem_v, ln_mix_w, w_in, conv_a_w, conv_s_w, conv_s_b, dt_bias, a_log, d_skip, ssd_norm_w, w_out, ln_mem_w, mem_norm_w, w_mq, w_mk, w_mv, w_mo, ln_ffn_w, w_pq, sub_keys, u_tab, v_tab, final_norm_w):
    depth = w_in.shape[0]
    assert depth == 1
    l = 0
    n_main = 3 * D_CONV + D_SSD + XBC_DIM
    w_dt = w_in[l][:, n_main:]
    w = {
        "ln_mix": ln_mix_w[l].reshape(1, D_MODEL),
        "w_in": w_in[l][:, :n_main].astype(BF16),
        "w_dt": jnp.pad(w_dt, ((0, 0), (0, LANES - SSD_HEADS))).astype(BF16),
        "w_dtt": jnp.pad(w_dt.T, ((0, 2 * SUBLANES - SSD_HEADS), (0, 0))).astype(BF16),
        "conv_a_w": conv_a_w[l], "conv_s_w": conv_s_w[l], "conv_s_b": conv_s_b[l].reshape(1, XBC_DIM),
        "dtb_c": _col(dt_bias[l]), "alog_c": _col(a_log[l]), "dsk_c": _col(d_skip[l]),
        "dtb_r": _row(dt_bias[l]), "alog_r": _row(a_log[l]),
        "ssd_norm": ssd_norm_w[l].reshape(1, D_SSD),
        "w_out": w_out[l].astype(BF16),
        "ln_ffn": ln_ffn_w[l].reshape(1, D_MODEL),
        "w_pq": w_pq[l].astype(BF16).reshape(D_MODEL, PEER_HEADS, PEER_DK).transpose(1, 0, 2),
        "sub_keys": sub_keys[l].astype(BF16),
        "u_pack": _pack_table(u_tab[l]),
        "v_pack": _pack_table(v_tab[l]),
        "u_tab": u_tab[l], "v_tab": v_tab[l],
        "final_norm": final_norm_w.reshape(1, D_MODEL),
    }
    ln_mem = ln_mem_w[l].reshape(1, D_MODEL)
    wq, wo = w_mq[l].astype(BF16), w_mo[l].astype(BF16)

    bp, sp, _ = x_prompt.shape
    bs, ss, _ = x_sample.shape

    mk, mv = _memory_kv(mem_prompt.reshape(bp * N_MEM, D_MODEL), mem_norm_w[l].reshape(1, D_MODEL),
                        w_mk[l].astype(BF16), w_mv[l].astype(BF16))
    mk3, mv3 = mk.reshape(bp, N_MEM, D_MODEL), mv.reshape(bp, N_MEM, D_MODEL)
    n1 = SC_EARLY_SEQS
    h1, pa1, ps1, ph1 = _mixer(x_prompt, None, w, 512, 0, n1)
    h1 = _attn(h1, mk3, mv3, ln_mem, wq, wo, 512).reshape(n1 * sp, D_MODEL)

    xn1, eid1, _, _, gate1 = _select(h1, w["ln_ffn"], w["w_pq"], w["sub_keys"], SELECT_BLOCK,0, n1 * sp)
    hid1 = _sc_hidden(eid1, xn1, w["u_tab"])
    h2, pa2, ps2, ph2 = _mixer(x_prompt, None, w, 512, n1, bp - n1, after=xn1)
    h2 = _attn(h2, mk3, mv3, ln_mem, wq, wo, 512, kv0=n1).reshape((bp - n1) * sp, D_MODEL)

    hs, sa, ssd, sh = _mixer(x_sample, (cache_conv_a[l], cache_conv_ssd[l],
                                        state_ssd[l].reshape((bs,) + PAIR_STATE)), w, 256, after=h2)
    hs = _attn(hs, cache_mem_k[l], cache_mem_v[l], ln_mem, wq, wo, 256).reshape(bs * ss, D_MODEL)
    xn_q, _, tile_q, shift_q, gate_q = _select(hs, w["ln_ffn"], w["w_pq"], w["sub_keys"], SELECT_BLOCK, 0, bs * ss,
                                               after=w["u_pack"])

    gh1 = _gate_gelu(hid1, gate1, 256, after=xn_q)
    out1 = _sc_combine(eid1, gh1, w["v_tab"])
    y1 = _final(h1, out1, w["final_norm"], 256, 0, bp * sp)
    def sample_experts(gh_main):
        gh_q = _hidden(tile_q, shift_q, xn_q, gate_q, w["u_pack"], 64, after=gh_main)
        return _combine(tile_q, shift_q, gh_q, w["v_pack"], hs, w["final_norm"], 64, 0, 0, bs * ss)

    y_prompt, y_sample = _peer(h2, w, 256, 64, SC_PROMPT_TOKENS, n1 * sp, bp * sp, after=gh1, into=y1,
                               between=sample_experts)
    y_prompt, y_sample = y_prompt.reshape(bp, sp, D_MODEL), y_sample.reshape(bs, ss, D_MODEL)
    pa, ps, ph = (jnp.concatenate(p, axis=0) for p in ((pa1, pa2), (ps1, ps2), (ph1, ph2)))

    kv_shape = (1, bp, N_MEM, MEM_HEADS, MEM_HEAD_DIM)
    st_shape = (SSD_HEADS, SSD_HEAD_DIM, SSD_STATE)
    return (y_prompt, y_sample, pa[None], ps[None], ph.reshape((1, bp) + st_shape), mk.reshape(kv_shape),
            mv.reshape(kv_shape), sa[None], ssd[None], sh.reshape((1, bs) + st_shape))
```
